```python
import math
import jax, jax.numpy as jnp
from jax import lax
import numpy as np

D_MODEL = 1024
BATCH = 8
SEQ = 4096
DEPTH = 1

MEM_LEN = 256
HEAD_DIM = 64
GMLP_HEADS = 4
ATTN_HEADS = 8
MEM_HEADS = 4
GMLP_WIDTH = GMLP_HEADS * HEAD_DIM
ATTN_WIDTH = ATTN_HEADS * HEAD_DIM
MEM_WIDTH = MEM_HEADS * HEAD_DIM
MIX_WIDTH = GMLP_WIDTH + ATTN_WIDTH + MEM_WIDTH
IN_WIDTH = 3 * GMLP_WIDTH + 4 * ATTN_WIDTH + 2 * MEM_WIDTH
CHUNK = 128
BLOCK = 128
DILATED_CONFIGS = ((128, 1), (512, 4), (2048, 16))
PAD_MULT = max(d for _, d in DILATED_CONFIGS) * BLOCK
EPS = 1e-6

kernel_name = "hybrid_gmlp_dilated_memory_layer"


def _rms(x, g):
    xf = x.astype(jnp.float32)
    y = xf * lax.rsqrt(jnp.mean(xf * xf, axis=-1, keepdims=True) + EPS)
    return (y * g.astype(jnp.float32)).astype(x.dtype)


def _dilated_branch(q, k, v, dilation, n_win):
    B, Sp, H, hd = q.shape
    L = Sp // dilation
    nb = L // BLOCK

    def to_blocks(t):
        return t.reshape(B, L, dilation, H, hd).transpose(0, 2, 3, 1, 4).reshape(B, dilation, H, nb, BLOCK, hd)

    def with_prev(t):
        prev = jnp.pad(t, ((0, 0), (0, 0), (0, 0), (1, 0), (0, 0), (0, 0)))[:, :, :, :-1]
        return jnp.concatenate([prev, t], axis=4)

    qb = to_blocks(q)
    kc = with_prev(to_blocks(k))
    vc = with_prev(to_blocks(v))
    s = jnp.einsum('bdhnqc,bdhnkc->bdhnqk', qb, kc).astype(jnp.float32) * (1.0 / math.sqrt(hd))
    qi = jnp.arange(BLOCK)[:, None] + BLOCK
    ki = jnp.arange(2 * BLOCK)[None, :]
    rel = qi - ki
    blk = jnp.arange(nb)[:, None, None]
    valid = (rel >= 0) & (rel <= n_win) & ((blk > 0) | (ki >= BLOCK))
    s = jnp.where(valid, s, -jnp.inf)
    lse = jax.nn.logsumexp(s, axis=-1)
    p = jnp.exp(s - lse[..., None])
    o = jnp.einsum('bdhnqk,bdhnkc->bdhnqc', p.astype(v.dtype), vc)
    o = o.reshape(B, dilation, H, L, hd).transpose(0, 3, 1, 2, 4).reshape(B, Sp, H, hd)
    lse = lse.reshape(B, dilation, H, L).transpose(0, 3, 1, 2).reshape(B, Sp, H)
    return o, lse


def _dilated_attention(q, k, v):
    B, S, H, hd = q.shape
    Sp = ((S + PAD_MULT - 1) // PAD_MULT) * PAD_MULT
    pad = ((0, 0), (0, Sp - S), (0, 0), (0, 0))
    qp, kp, vp = jnp.pad(q, pad), jnp.pad(k, pad), jnp.pad(v, pad)
    outs, lses = [], []
    for window, dil in DILATED_CONFIGS:
        o, l = _dilated_branch(qp, kp, vp, dil, window // dil)
        outs.append(o)
        lses.append(l)
    w = jax.nn.softmax(jnp.stack(lses, axis=0), axis=0)
    out = sum(w[i][..., None] * outs[i].astype(jnp.float32) for i in range(len(outs)))
    return out[:, :S].astype(q.dtype)


def _chunked_gmlp(u, v, v_gain, w_s, b_s):
    B, S, GH, hd = v.shape
    nc = S // CHUNK
    vn = _rms(v, v_gain).reshape(B, nc, CHUNK, GH, hd)
    tril = jnp.tril(jnp.ones((CHUNK, CHUNK), dtype=w_s.dtype))
    sp = jnp.einsum('hts,bcshd->bcthd', w_s * tril, vn) + b_s.T[:, :, None]
    return u * sp.reshape(B, S, GH, hd)


def _memory_attention(qm, mem, mem_gain, w_mem_kv, q_gain, k_gain):
    B, S, H, hd = qm.shape
    kv = _rms(mem, mem_gain) @ w_mem_kv
    mk, mv = jnp.split(kv, 2, axis=-1)
    mk = _rms(mk.reshape(B, -1, H, hd), k_gain)
    mv = mv.reshape(B, -1, H, hd)
    qn = _rms(qm, q_gain)
    s = jnp.einsum('bshc,bmhc->bhsm', qn, mk).astype(jnp.float32) * (1.0 / math.sqrt(hd))
    p = jax.nn.softmax(s, axis=-1)
    return jnp.einsum('bhsm,bmhc->bshc', p.astype(mv.dtype), mv)


def _fwd_setup_inputs(seed: int = 0) -> dict:
    key = jax.random.key(seed)
    ks = jax.random.split(key, 16)
    f32 = jnp.float32
    x = jax.random.normal(ks[0], (BATCH, SEQ, D_MODEL), f32)
    mem = jax.random.normal(ks[1], (BATCH, MEM_LEN, D_MODEL), f32)
    norm_gain = 1.0 + 0.02 * jax.random.normal(ks[2], (DEPTH, D_MODEL), f32)
    w_in = jax.random.normal(ks[3], (DEPTH, D_MODEL, IN_WIDTH), f32) * D_MODEL ** -0.5
    gmlp_v_gain = 1.0 + 0.02 * jax.random.normal(ks[4], (DEPTH, GMLP_HEADS, HEAD_DIM), f32)
    gmlp_w_s = jax.random.normal(ks[5], (DEPTH, GMLP_HEADS, CHUNK, CHUNK), f32) * CHUNK ** -0.5
    gmlp_b = 1.0 + 0.02 * jax.random.normal(ks[6], (DEPTH, GMLP_HEADS, CHUNK), f32)
    attn_q_gain = 1.0 + 0.02 * jax.random.normal(ks[7], (DEPTH, HEAD_DIM), f32)
    attn_k_gain = 1.0 + 0.02 * jax.random.normal(ks[8], (DEPTH, HEAD_DIM), f32)
    mem_norm_gain = 1.0 + 0.02 * jax.random.normal(ks[9], (DEPTH, D_MODEL), f32)
    w_mem_kv = jax.random.normal(ks[10], (DEPTH, D_MODEL, 2 * MEM_WIDTH), f32) * D_MODEL ** -0.5
    mem_q_gain = 1.0 + 0.02 * jax.random.normal(ks[11], (DEPTH, HEAD_DIM), f32)
    mem_k_gain = 1.0 + 0.02 * jax.random.normal(ks[12], (DEPTH, HEAD_DIM), f32)
    w_out = jax.random.normal(ks[13], (DEPTH, MIX_WIDTH, D_MODEL), f32) * MIX_WIDTH ** -0.5
    return {"x": x, "mem": mem, "norm_gain": norm_gain, "w_in": w_in,
            "gmlp_v_gain": gmlp_v_gain, "gmlp_w_s": gmlp_w_s, "gmlp_b": gmlp_b,
            "attn_q_gain": attn_q_gain, "attn_k_gain": attn_k_gain,
            "mem_norm_gain": mem_norm_gain, "w_mem_kv": w_mem_kv,
            "mem_q_gain": mem_q_gain, "mem_k_gain": mem_k_gain, "w_out": w_out}


def _fwd_reference(x, mem, norm_gain, w_in, gmlp_v_gain, gmlp_w_s, gmlp_b,
              attn_q_gain, attn_k_gain, mem_norm_gain, w_mem_kv,
              mem_q_gain, mem_k_gain, w_out):
    B, S, _ = x.shape
    split_points = np.cumsum([GMLP_WIDTH] * 3 + [ATTN_WIDTH] * 4 + [MEM_WIDTH])
    for l in range(DEPTH):
        h = _rms(x, norm_gain[l])
        proj = h @ w_in[l]
        g_u, g_v, g_gate, a_q, a_k, a_v, a_gate, m_q, m_gate = jnp.split(proj, split_points, axis=-1)

        y_g = _chunked_gmlp(g_u.reshape(B, S, GMLP_HEADS, HEAD_DIM), g_v.reshape(B, S, GMLP_HEADS, HEAD_DIM),
                            gmlp_v_gain[l], gmlp_w_s[l], gmlp_b[l]).reshape(B, S, GMLP_WIDTH)
        y_g = y_g * jax.nn.silu(g_gate)

        q = _rms(a_q.reshape(B, S, ATTN_HEADS, HEAD_DIM), attn_q_gain[l])
        k = _rms(a_k.reshape(B, S, ATTN_HEADS, HEAD_DIM), attn_k_gain[l])
        v = a_v.reshape(B, S, ATTN_HEADS, HEAD_DIM)
        y_a = _dilated_attention(q, k, v).reshape(B, S, ATTN_WIDTH) * jax.nn.silu(a_gate)

        y_m = _memory_attention(m_q.reshape(B, S, MEM_HEADS, HEAD_DIM), mem, mem_norm_gain[l], w_mem_kv[l],
                                mem_q_gain[l], mem_k_gain[l]).reshape(B, S, MEM_WIDTH)
        y_m = y_m * jax.nn.silu(m_gate)

        y = jnp.concatenate([y_g, y_a, y_m], axis=-1) @ w_out[l]
        x = x + y
    return x


import jax as _jax
import jax.numpy as _jnp

TWIN_FORMAT = 'train_step'
FWD_PARAMS = ['x', 'mem', 'norm_gain', 'w_in', 'gmlp_v_gain', 'gmlp_w_s', 'gmlp_b', 'attn_q_gain', 'attn_k_gain', 'mem_norm_gain', 'w_mem_kv', 'mem_q_gain', 'mem_k_gain', 'w_out']
TWIN_WEIGHTS = ['norm_gain', 'w_in', 'gmlp_v_gain', 'gmlp_w_s', 'gmlp_b', 'attn_q_gain', 'attn_k_gain', 'mem_norm_gain', 'w_mem_kv', 'mem_q_gain', 'mem_k_gain', 'w_out']
TWIN_DIFF_INPUT = 'x'
TWIN_INPUTS = ['x', 'mem', 'norm_gain', 'w_in', 'gmlp_v_gain', 'gmlp_w_s', 'gmlp_b', 'attn_q_gain', 'attn_k_gain', 'mem_norm_gain', 'w_mem_kv', 'mem_q_gain', 'mem_k_gain', 'w_out', 'loss_target', 'm_norm_gain', 'm_w_in', 'm_gmlp_v_gain', 'm_gmlp_w_s', 'm_gmlp_b', 'm_attn_q_gain', 'm_attn_k_gain', 'm_mem_norm_gain', 'm_w_mem_kv', 'm_mem_q_gain', 'm_mem_k_gain', 'm_w_out', 'v_norm_gain', 'v_w_in', 'v_gmlp_v_gain', 'v_gmlp_w_s', 'v_gmlp_b', 'v_attn_q_gain', 'v_attn_k_gain', 'v_mem_norm_gain', 'v_w_mem_kv', 'v_mem_q_gain', 'v_mem_k_gain', 'v_w_out']
TWIN_OUTPUTS = ['loss', 'grad_x', 'grad_norm_gain', 'grad_w_in', 'grad_gmlp_v_gain', 'grad_gmlp_w_s', 'grad_gmlp_b', 'grad_attn_q_gain', 'grad_attn_k_gain', 'grad_mem_norm_gain', 'grad_w_mem_kv', 'grad_mem_q_gain', 'grad_mem_k_gain', 'grad_w_out', 'delta_norm_gain', 'delta_w_in', 'delta_gmlp_v_gain', 'delta_gmlp_w_s', 'delta_gmlp_b', 'delta_attn_q_gain', 'delta_attn_k_gain', 'delta_mem_norm_gain', 'delta_w_mem_kv', 'delta_mem_q_gain', 'delta_mem_k_gain', 'delta_w_out', 'new_m_norm_gain', 'new_m_w_in', 'new_m_gmlp_v_gain', 'new_m_gmlp_w_s', 'new_m_gmlp_b', 'new_m_attn_q_gain', 'new_m_attn_k_gain', 'new_m_mem_norm_gain', 'new_m_w_mem_kv', 'new_m_mem_q_gain', 'new_m_mem_k_gain', 'new_m_w_out', 'new_v_norm_gain', 'new_v_w_in', 'new_v_gmlp_v_gain', 'new_v_gmlp_w_s', 'new_v_gmlp_b', 'new_v_attn_q_gain', 'new_v_attn_k_gain', 'new_v_mem_norm_gain', 'new_v_w_mem_kv', 'new_v_mem_q_gain', 'new_v_mem_k_gain', 'new_v_w_out']
TWIN_LEAF_KINDS = {'loss': 'loss', 'grad_x': 'grad_x', 'grad_norm_gain': 'grad_w', 'grad_w_in': 'grad_w', 'grad_gmlp_v_gain': 'grad_w', 'grad_gmlp_w_s': 'grad_w', 'grad_gmlp_b': 'grad_w', 'grad_attn_q_gain': 'grad_w', 'grad_attn_k_gain': 'grad_w', 'grad_mem_norm_gain': 'grad_w', 'grad_w_mem_kv': 'grad_w', 'grad_mem_q_gain': 'grad_w', 'grad_mem_k_gain': 'grad_w', 'grad_w_out': 'grad_w', 'delta_norm_gain': 'delta_w', 'delta_w_in': 'delta_w', 'delta_gmlp_v_gain': 'delta_w', 'delta_gmlp_w_s': 'delta_w', 'delta_gmlp_b': 'delta_w', 'delta_attn_q_gain': 'delta_w', 'delta_attn_k_gain': 'delta_w', 'delta_mem_norm_gain': 'delta_w', 'delta_w_mem_kv': 'delta_w', 'delta_mem_q_gain': 'delta_w', 'delta_mem_k_gain': 'delta_w', 'delta_w_out': 'delta_w', 'new_m_norm_gain': 'new_m', 'new_m_w_in': 'new_m', 'new_m_gmlp_v_gain': 'new_m', 'new_m_gmlp_w_s': 'new_m', 'new_m_gmlp_b': 'new_m', 'new_m_attn_q_gain': 'new_m', 'new_m_attn_k_gain': 'new_m', 'new_m_mem_norm_gain': 'new_m', 'new_m_w_mem_kv': 'new_m', 'new_m_mem_q_gain': 'new_m', 'new_m_mem_k_gain': 'new_m', 'new_m_w_out': 'new_m', 'new_v_norm_gain': 'new_v', 'new_v_w_in': 'new_v', 'new_v_gmlp_v_gain': 'new_v', 'new_v_gmlp_w_s': 'new_v', 'new_v_gmlp_b': 'new_v', 'new_v_attn_q_gain': 'new_v', 'new_v_attn_k_gain': 'new_v', 'new_v_mem_norm_gain': 'new_v', 'new_v_w_mem_kv': 'new_v', 'new_v_mem_q_gain': 'new_v', 'new_v_mem_k_gain': 'new_v', 'new_v_w_out': 'new_v'}


def _forward(args):
    return _fwd_reference(*[args[k] for k in FWD_PARAMS])


def _output_shape():
    def fwd():
        inp = _fwd_setup_inputs(0)
        return _fwd_reference(*[inp[k] for k in FWD_PARAMS])
    out = _jax.eval_shape(fwd)
    return out.shape, out.dtype

N_MICROBATCH = 1
ADAM_LR = 0.001
ADAM_B1 = 0.9
ADAM_B2 = 0.999
ADAM_EPS = 1e-08
ADAM_WD = 0.01
ADAM_STEP = 10
PER_EXAMPLE_BATCH_AXIS = {'x': 0, 'mem': 0, 'loss_target': 0}
SHARED_INPUTS = []
_WEIGHT_DTYPES = {'norm_gain': _jnp.float32, 'w_in': _jnp.float32, 'gmlp_v_gain': _jnp.float32, 'gmlp_w_s': _jnp.float32, 'gmlp_b': _jnp.float32, 'attn_q_gain': _jnp.float32, 'attn_k_gain': _jnp.float32, 'mem_norm_gain': _jnp.float32, 'w_mem_kv': _jnp.float32, 'mem_q_gain': _jnp.float32, 'mem_k_gain': _jnp.float32, 'w_out': _jnp.float32}
MOMENT_SCALE = {'norm_gain': 9.321654e+00, 'w_in': 2.525505e-01, 'gmlp_v_gain': 5.659202e+00, 'gmlp_w_s': 3.326005e-01, 'gmlp_b': 5.670026e+00, 'attn_q_gain': 8.103602e-01, 'attn_k_gain': 8.098860e-01, 'mem_norm_gain': 2.906147e-02, 'w_mem_kv': 1.932712e-02, 'mem_q_gain': 4.435759e-01, 'mem_k_gain': 4.432918e-01, 'w_out': 2.461579e-01}


def _to_microbatches(a, axis):
    t = _jnp.moveaxis(a, axis, 0)
    t = t.reshape((N_MICROBATCH, t.shape[0] // N_MICROBATCH) + t.shape[1:])
    return _jnp.moveaxis(t, 1, axis + 1)


def setup_inputs(seed: int = 0) -> dict:
    inp = _fwd_setup_inputs(seed)
    key = _jax.random.fold_in(_jax.random.key(seed), 7919)
    shape, _ = _output_shape()
    out = dict(inp)
    out["loss_target"] = _jax.random.normal(_jax.random.fold_in(key, 0), shape, _jnp.float32)
    for i, name in enumerate(TWIN_WEIGHTS):
        w = inp[name].astype(_jnp.float32)
        if MOMENT_SCALE is None:
            s = _jnp.sqrt(_jnp.mean(_jnp.square(w)) + 1e-30)
        else:
            s = MOMENT_SCALE[name]
        km, kv = _jax.random.split(_jax.random.fold_in(key, i + 1))
        out[name] = w
        out["m_" + name] = s * _jax.random.normal(km, w.shape, _jnp.float32)
        out["v_" + name] = (s * s) * _jax.random.uniform(kv, w.shape, _jnp.float32, 0.5, 1.5)
    if N_MICROBATCH > 1:
        for name, axis in PER_EXAMPLE_BATCH_AXIS.items():
            out[name] = _to_microbatches(out[name], axis)
    return {'x': out['x'], 'mem': out['mem'], 'norm_gain': out['norm_gain'], 'w_in': out['w_in'], 'gmlp_v_gain': out['gmlp_v_gain'], 'gmlp_w_s': out['gmlp_w_s'], 'gmlp_b': out['gmlp_b'], 'attn_q_gain': out['attn_q_gain'], 'attn_k_gain': out['attn_k_gain'], 'mem_norm_gain': out['mem_norm_gain'], 'w_mem_kv': out['w_mem_kv'], 'mem_q_gain': out['mem_q_gain'], 'mem_k_gain': out['mem_k_gain'], 'w_out': out['w_out'], 'loss_target': out['loss_target'], 'm_norm_gain': out['m_norm_gain'], 'm_w_in': out['m_w_in'], 'm_gmlp_v_gain': out['m_gmlp_v_gain'], 'm_gmlp_w_s': out['m_gmlp_w_s'], 'm_gmlp_b': out['m_gmlp_b'], 'm_attn_q_gain': out['m_attn_q_gain'], 'm_attn_k_gain': out['m_attn_k_gain'], 'm_mem_norm_gain': out['m_mem_norm_gain'], 'm_w_mem_kv': out['m_w_mem_kv'], 'm_mem_q_gain': out['m_mem_q_gain'], 'm_mem_k_gain': out['m_mem_k_gain'], 'm_w_out': out['m_w_out'], 'v_norm_gain': out['v_norm_gain'], 'v_w_in': out['v_w_in'], 'v_gmlp_v_gain': out['v_gmlp_v_gain'], 'v_gmlp_w_s': out['v_gmlp_w_s'], 'v_gmlp_b': out['v_gmlp_b'], 'v_attn_q_gain': out['v_attn_q_gain'], 'v_attn_k_gain': out['v_attn_k_gain'], 'v_mem_norm_gain': out['v_mem_norm_gain'], 'v_w_mem_kv': out['v_w_mem_kv'], 'v_mem_q_gain': out['v_mem_q_gain'], 'v_mem_k_gain': out['v_mem_k_gain'], 'v_w_out': out['v_w_out']}


def _loss(weights, diff, rest, loss_target):
    with _jax.named_scope("forward"):
        args = {**rest, TWIN_DIFF_INPUT: diff, **{k: w.astype(_WEIGHT_DTYPES[k]) for k, w in weights.items()}}
        y = _forward(args)
    with _jax.named_scope("loss_head"):
        err = _jnp.square(y.astype(_jnp.float32) - loss_target)
        return 0.5 * _jnp.sum(_jnp.mean(err, axis=-1)) if err.ndim else 0.5 * err


def _adamw(w, g, m, v):
    m = ADAM_B1 * m + (1.0 - ADAM_B1) * g
    v = ADAM_B2 * v + (1.0 - ADAM_B2) * _jnp.square(g)
    m_hat = m / (1.0 - ADAM_B1 ** ADAM_STEP)
    v_hat = v / (1.0 - ADAM_B2 ** ADAM_STEP)
    delta = -ADAM_LR * (m_hat / (_jnp.sqrt(v_hat) + ADAM_EPS) + ADAM_WD * w)
    return delta, m, v


def reference(x, mem, norm_gain, w_in, gmlp_v_gain, gmlp_w_s, gmlp_b, attn_q_gain, attn_k_gain, mem_norm_gain, w_mem_kv, mem_q_gain, mem_k_gain, w_out, loss_target, m_norm_gain, m_w_in, m_gmlp_v_gain, m_gmlp_w_s, m_gmlp_b, m_attn_q_gain, m_attn_k_gain, m_mem_norm_gain, m_w_mem_kv, m_mem_q_gain, m_mem_k_gain, m_w_out, v_norm_gain, v_w_in, v_gmlp_v_gain, v_gmlp_w_s, v_gmlp_b, v_attn_q_gain, v_attn_k_gain, v_mem_norm_gain, v_w_mem_kv, v_mem_q_gain, v_mem_k_gain, v_w_out):
    given = dict(x=x, mem=mem, norm_gain=norm_gain, w_in=w_in, gmlp_v_gain=gmlp_v_gain, gmlp_w_s=gmlp_w_s, gmlp_b=gmlp_b, attn_q_gain=attn_q_gain, attn_k_gain=attn_k_gain, mem_norm_gain=mem_norm_gain, w_mem_kv=w_mem_kv, mem_q_gain=mem_q_gain, mem_k_gain=mem_k_gain, w_out=w_out, loss_target=loss_target, m_norm_gain=m_norm_gain, m_w_in=m_w_in, m_gmlp_v_gain=m_gmlp_v_gain, m_gmlp_w_s=m_gmlp_w_s, m_gmlp_b=m_gmlp_b, m_attn_q_gain=m_attn_q_gain, m_attn_k_gain=m_attn_k_gain, m_mem_norm_gain=m_mem_norm_gain, m_w_mem_kv=m_w_mem_kv, m_mem_q_gain=m_mem_q_gain, m_mem_k_gain=m_mem_k_gain, m_w_out=m_w_out, v_norm_gain=v_norm_gain, v_w_in=v_w_in, v_gmlp_v_gain=v_gmlp_v_gain, v_gmlp_w_s=v_gmlp_w_s, v_gmlp_b=v_gmlp_b, v_attn_q_gain=v_attn_q_gain, v_attn_k_gain=v_attn_k_gain, v_mem_norm_gain=v_mem_norm_gain, v_w_mem_kv=v_w_mem_kv, v_mem_q_gain=v_mem_q_gain, v_mem_k_gain=v_mem_k_gain, v_w_out=v_w_out)
    weights = {n: given[n] for n in TWIN_WEIGHTS}
    shared = {n: given[n] for n in SHARED_INPUTS}
    per_example = {n: given[n] for n in ['x', 'mem']}
    grad_fn = _jax.value_and_grad(_loss, argnums=(0, 1))

    def one_microbatch(ex, loss_target):
        ex = dict(ex)
        diff = ex.pop(TWIN_DIFF_INPUT)
        return grad_fn(weights, diff, {**shared, **ex}, loss_target)

    if N_MICROBATCH == 1:
        loss, (grad_w, grad_x) = one_microbatch(per_example, given["loss_target"])
    else:
        def body(carry, xs):
            loss_sum, grad_sum = carry
            l_k, (gw_k, gx_k) = one_microbatch(xs[0], xs[1])
            with _jax.named_scope("update"):
                return (loss_sum + l_k, _jax.tree.map(_jnp.add, grad_sum, gw_k)), gx_k

        init = (_jnp.zeros((), _jnp.float32), _jax.tree.map(_jnp.zeros_like, weights))
        (loss, grad_w), grad_x = _jax.lax.scan(body, init, (per_example, given["loss_target"]))
    with _jax.named_scope("update"):
        delta_w, new_m, new_v = {}, {}, {}
        for n in TWIN_WEIGHTS:
            delta_w[n], new_m[n], new_v[n] = _adamw(weights[n], grad_w[n], given["m_" + n], given["v_" + n])
    return (loss, grad_x, *[grad_w[n] for n in TWIN_WEIGHTS], *[delta_w[n] for n in TWIN_WEIGHTS],
            *[new_m[n] for n in TWIN_WEIGHTS], *[new_v[n] for n in TWIN_WEIGHTS])
```

```python
import functools
import math

import jax
import jax.numpy as jnp
from jax import lax
from jax.experimental import pallas as pl
from jax.experimental.pallas import tpu as pltpu

F32 = jnp.float32
BF16 = jnp.bfloat16

N_DEV = 8
D_MODEL = 1024
HEAD_DIM = 64
GMLP_WIDTH = 256
ATTN_WIDTH = 512
MEM_WIDTH = 256
MEM_LEN = 256
IN_WIDTH = 3 * GMLP_WIDTH + 4 * ATTN_WIDTH + 2 * MEM_WIDTH
CHUNK = 128
BLOCK = 128
DILATIONS = (1, 4, 16)
EPS = 1e-6
SCALE = 1.0 / math.sqrt(HEAD_DIM)
NEG = -1e30

ADAM_LR = 0.001
ADAM_B1 = 0.9
ADAM_B2 = 0.999
ADAM_EPS = 1e-08
ADAM_WD = 0.01
ADAM_STEP = 10

MIB = 1024 * 1024
MESH = pl.DeviceIdType.MESH

COL_AQ, COL_AK, COL_AV, COL_AG = 6, 10, 14, 18


def _params(vmem_mib, semantics=None):
    kw = dict(vmem_limit_bytes=vmem_mib * MIB)
    if semantics is not None:
        kw["dimension_semantics"] = semantics
    return pltpu.CompilerParams(**kw)


def _split_dot(x, sel_bf):
    hi = x.astype(BF16)
    lo = (x - hi.astype(F32)).astype(BF16)
    return jnp.dot(hi, sel_bf, preferred_element_type=F32) + jnp.dot(lo, sel_bf, preferred_element_type=F32)


def _nt(a, b):
    return lax.dot_general(a, b, (((1,), (1,)), ((), ())), preferred_element_type=F32)


def _tn(a, b):
    return lax.dot_general(a, b, (((0,), (0,)), ((), ())), preferred_element_type=F32)


def _silu_parts(g):
    sg = jax.nn.sigmoid(g)
    return g * sg, sg * (1.0 + g * (1.0 - sg))


def _head_index(shape):
    return lax.shift_right_logical(lax.broadcasted_iota(jnp.int32, shape, 1), HEAD_DIM.bit_length() - 1)


def _head_blockdiag(width):
    i = jnp.arange(width) // HEAD_DIM
    return (i[:, None] == i[None, :]).astype(BF16)


def _rms_proj(x, gain, w_bf):
    S, D = x.shape
    N = w_bf.shape[1]
    tm = 256

    def body(x_ref, g_ref, w_ref, proj_ref, h_ref):
        xv = x_ref[...]
        r = lax.rsqrt(jnp.mean(xv * xv, axis=-1, keepdims=True) + EPS)
        h = ((xv * r) * g_ref[...]).astype(BF16)
        h_ref[...] = h
        proj_ref[...] = jnp.dot(h, w_ref[...], preferred_element_type=F32)

    return pl.pallas_call(
        body, name="rms_proj", grid=(S // tm,),
        in_specs=[pl.BlockSpec((tm, D), lambda i: (i, 0)), pl.BlockSpec((1, D), lambda i: (0, 0)),
                  pl.BlockSpec((D, N), lambda i: (0, 0))],
        out_specs=[pl.BlockSpec((tm, N), lambda i: (i, 0)), pl.BlockSpec((tm, D), lambda i: (i, 0))],
        out_shape=[jax.ShapeDtypeStruct((S, N), F32), jax.ShapeDtypeStruct((S, D), BF16)],
        compiler_params=_params(40, ("arbitrary",)),
    )(x, gain, w_bf)


def _gmlp_masked_weights(ws_ref, transpose):
    t = lax.broadcasted_iota(jnp.int32, (CHUNK, CHUNK), 0)
    s = lax.broadcasted_iota(jnp.int32, (CHUNK, CHUNK), 1)
    parts = []
    for h in range(4):
        wm = jnp.where(s <= t, ws_ref[h], 0.0)
        parts.append(wm.T if transpose else wm)
    return jnp.concatenate(parts, axis=1).astype(BF16)


def _head_stack(v, head):
    return jnp.concatenate([jnp.where(head == h, v, 0.0) for h in range(4)], axis=0).astype(BF16)


def _gmlp_fwd(proj, vg, w_s, b2, bd):
    S = proj.shape[0]
    tm = 512

    def body(u_ref, v_ref, g_ref, vg_ref, ws_ref, b2_ref, bd_ref, y_ref):
        v = v_ref[...]
        ms = _split_dot(v * v, bd_ref[...]) * (1.0 / HEAD_DIM)
        vn = (v * lax.rsqrt(ms + EPS)) * vg_ref[...]
        wcat = _gmlp_masked_weights(ws_ref, False)
        head = _head_index((CHUNK, GMLP_WIDTH))
        for c in range(tm // CHUNK):
            rows = slice(c * CHUNK, (c + 1) * CHUNK)
            sp = jnp.dot(wcat, _head_stack(vn[rows], head), preferred_element_type=F32) + b2_ref[...]
            silu, _ = _silu_parts(g_ref[rows, :])
            y_ref[rows, :] = ((u_ref[rows, :] * sp) * silu).astype(BF16)

    col = lambda j: pl.BlockSpec((tm, GMLP_WIDTH), lambda i, j=j: (i, j))
    const = lambda shape: pl.BlockSpec(shape, lambda i: (0,) * len(shape))
    return pl.pallas_call(
        body, name="gmlp_fwd", grid=(S // tm,),
        in_specs=[col(0), col(1), col(2), const((1, GMLP_WIDTH)), const((4, CHUNK, CHUNK)),
                  const((CHUNK, GMLP_WIDTH)), const((GMLP_WIDTH, GMLP_WIDTH))],
        out_specs=pl.BlockSpec((tm, GMLP_WIDTH), lambda i: (i, 0)),
        out_shape=jax.ShapeDtypeStruct((S, GMLP_WIDTH), BF16),
        compiler_params=_params(24, ("arbitrary",)),
    )(proj, proj, proj, vg, w_s, b2, bd)


def _gmlp_bwd(proj, dycat, vg, w_s, b2, bd):
    S = proj.shape[0]
    tm = 512
    nsteps = S // tm

    def body(u_ref, v_ref, g_ref, dy_ref, vg_ref, ws_ref, b2_ref, bd_ref,
             du_ref, dv_ref, dg_ref, dws_ref, db2_ref, dvg_ref):
        i = pl.program_id(0)

        @pl.when(i == 0)
        def _():
            dws_ref[...] = jnp.zeros_like(dws_ref)
            db2_ref[...] = jnp.zeros_like(db2_ref)
            dvg_ref[...] = jnp.zeros_like(dvg_ref)

        bdv = bd_ref[...]
        v = v_ref[...]
        ms = _split_dot(v * v, bdv) * (1.0 / HEAD_DIM)
        rv = lax.rsqrt(ms + EPS)
        xhat = v * rv
        vgv = vg_ref[...]
        vn = xhat * vgv
        wcat = _gmlp_masked_weights(ws_ref, False)
        wcat_t = _gmlp_masked_weights(ws_ref, True)
        head = _head_index((CHUNK, GMLP_WIDTH))
        dvg = jnp.zeros((1, GMLP_WIDTH), F32)
        for c in range(tm // CHUNK):
            rows = slice(c * CHUNK, (c + 1) * CHUNK)
            vn_c = vn[rows]
            spb = jnp.dot(wcat, _head_stack(vn_c, head), preferred_element_type=F32) + b2_ref[...]
            silu, dsilu = _silu_parts(g_ref[rows, :])
            dy = dy_ref[rows, :]
            u = u_ref[rows, :]
            du_ref[rows, :] = (dy * spb * silu).astype(BF16)
            dg_ref[rows, :] = (dy * u * spb * dsilu).astype(BF16)
            dsp = dy * u * silu
            db2_ref[...] += dsp
            dstack = _head_stack(dsp, head)
            dvn = jnp.dot(wcat_t, dstack, preferred_element_type=F32)
            dws_ref[...] += _nt(dstack, vn_c.astype(BF16))
            xh = xhat[rows]
            a = dvn * vgv
            mean_ax = _split_dot(a * xh, bdv) * (1.0 / HEAD_DIM)
            dv_ref[rows, :] = (rv[rows] * (a - xh * mean_ax)).astype(BF16)
            dvg = dvg + jnp.sum(dvn * xh, axis=0, keepdims=True)
        dvg_ref[...] += dvg

        @pl.when(i == nsteps - 1)
        def _():
            t = lax.broadcasted_iota(jnp.int32, (4 * CHUNK, CHUNK), 0) % CHUNK
            s = lax.broadcasted_iota(jnp.int32, (4 * CHUNK, CHUNK), 1)
            dws_ref[...] = jnp.where(s <= t, dws_ref[...], 0.0)
            db2_ref[...] = _split_dot(db2_ref[...], bdv)

    col = lambda j: pl.BlockSpec((tm, GMLP_WIDTH), lambda i, j=j: (i, j))
    const = lambda shape: pl.BlockSpec(shape, lambda i: (0,) * len(shape))
    tile = pl.BlockSpec((tm, GMLP_WIDTH), lambda i: (i, 0))
    piece = jax.ShapeDtypeStruct((S, GMLP_WIDTH), BF16)
    return pl.pallas_call(
        body, name="gmlp_bwd", grid=(nsteps,),
        in_specs=[col(0), col(1), col(2), col(0), const((1, GMLP_WIDTH)), const((4, CHUNK, CHUNK)),
                  const((CHUNK, GMLP_WIDTH)), const((GMLP_WIDTH, GMLP_WIDTH))],
        out_specs=[tile, tile, tile, const((4 * CHUNK, CHUNK)), const((CHUNK, GMLP_WIDTH)), const((1, GMLP_WIDTH))],
        out_shape=[piece, piece, piece, jax.ShapeDtypeStruct((4 * CHUNK, CHUNK), F32),
                   jax.ShapeDtypeStruct((CHUNK, GMLP_WIDTH), F32), jax.ShapeDtypeStruct((1, GMLP_WIDTH), F32)],
        compiler_params=_params(32, ("arbitrary",)),
    )(proj, proj, proj, dycat, vg, w_s, b2, bd)


def _band_masks():
    qi = lax.broadcasted_iota(jnp.int32, (2 * BLOCK, 2 * BLOCK), 0) % BLOCK
    ki = lax.broadcasted_iota(jnp.int32, (2 * BLOCK, 2 * BLOCK), 1)
    band = ((ki < BLOCK) & (ki >= qi)) | ((ki >= BLOCK) & ((ki - BLOCK) <= qi))
    qf = lax.broadcasted_iota(jnp.int32, (2 * BLOCK, BLOCK), 0) % BLOCK
    kf = lax.broadcasted_iota(jnp.int32, (2 * BLOCK, BLOCK), 1)
    return band, kf <= qf


def _two_heads(q, lo):
    zero = jnp.zeros_like(q)
    return jnp.concatenate([jnp.where(lo, q, zero), jnp.where(lo, zero, q)], axis=0)


def _fill_class_major(dst, src, d, S, convert):
    L = S // d
    for r in range(d):
        def step(n, carry, r=r):
            rows = src[pl.ds(r + n * (BLOCK * d), BLOCK, stride=d), :] if d > 1 else \
                src[pl.ds(pl.multiple_of(n * BLOCK, BLOCK), BLOCK), :]
            dst[pl.ds(pl.multiple_of(r * L + n * BLOCK, BLOCK), BLOCK), :] = convert(rows)
            return carry
        lax.fori_loop(0, L // BLOCK, step, 0)


def _attn_fwd(proj, qg2, kg2, bd):
    S = proj.shape[0]
    npairs = ATTN_WIDTH // 128
    tn = 512

    def body(q_ref, k_ref, v_ref, g_ref, qg_ref, kg_ref, bd_ref, y_ref, att_ref, lse_ref,
             qn, kn, qc, kc, vc, ocm, lcm):
        bdv = bd_ref[...]
        lo = lax.broadcasted_iota(jnp.int32, (BLOCK, 128), 1) < HEAD_DIM
        band_mask, first_mask = _band_masks()

        def norm_step(i, carry):
            rows = pl.ds(pl.multiple_of(i * tn, tn), tn)
            qv = q_ref[rows, :]
            kv = k_ref[rows, :]
            qn[rows, :] = (qv * lax.rsqrt(_split_dot(qv * qv, bdv) * (1.0 / HEAD_DIM) + EPS)) * (qg_ref[...] * SCALE)
            kn[rows, :] = (kv * lax.rsqrt(_split_dot(kv * kv, bdv) * (1.0 / HEAD_DIM) + EPS)) * kg_ref[...]
            return carry
        lax.fori_loop(0, S // tn, norm_step, 0)

        def block(base, first):
            q2 = _two_heads(qc[pl.ds(base, BLOCK), :], lo)
            if first:
                kb = kc[pl.ds(base, BLOCK), :]
                vb = vc[pl.ds(base, BLOCK), :]
                valid = first_mask
            else:
                kb = kc[pl.ds(base - BLOCK, 2 * BLOCK), :]
                vb = vc[pl.ds(base - BLOCK, 2 * BLOCK), :]
                valid = band_mask
            s = jnp.where(valid, _nt(q2, kb), NEG)
            m = jnp.max(s, axis=-1, keepdims=True)
            e = jnp.exp(s - m)
            l = jnp.sum(e, axis=-1, keepdims=True)
            o2 = jnp.dot((e * (1.0 / l)).astype(BF16), vb, preferred_element_type=F32)
            lse = m + jnp.log(l)
            o = jnp.where(lo, o2[:BLOCK], o2[BLOCK:])
            ls = jnp.where(lo, lse[:BLOCK], lse[BLOCK:])
            return o, ls

        to_bf = lambda t: t.astype(BF16)
        for d in DILATIONS:
            L = S // d
            _fill_class_major(qc, qn, d, S, to_bf)
            _fill_class_major(kc, kn, d, S, to_bf)
            _fill_class_major(vc, v_ref, d, S, to_bf)
            o_dst, l_dst = (att_ref, lse_ref) if d == 1 else (ocm, lcm)

            def class_body(r, carry, L=L, o_dst=o_dst, l_dst=l_dst):
                base0 = pl.multiple_of(r * L, BLOCK)
                o, ls = block(base0, True)
                o_dst[pl.ds(base0, BLOCK), :] = o
                l_dst[pl.ds(base0, BLOCK), :] = ls

                def nstep(n, c2):
                    base = pl.multiple_of(base0 + n * BLOCK, BLOCK)
                    o, ls = block(base, False)
                    o_dst[pl.ds(base, BLOCK), :] = o
                    l_dst[pl.ds(base, BLOCK), :] = ls
                    return c2
                lax.fori_loop(1, L // BLOCK, nstep, 0)
                return carry
            lax.fori_loop(0, d, class_body, 0)

            if d > 1:
                for r in range(d):
                    def merge(n, carry, r=r, L=L, d=d):
                        nat = pl.ds(r + n * (BLOCK * d), BLOCK, stride=d)
                        cm = pl.ds(pl.multiple_of(r * L + n * BLOCK, BLOCK), BLOCK)
                        la, lb = lse_ref[nat, :], lcm[cm, :]
                        m = jnp.maximum(la, lb)
                        wa, wb = jnp.exp(la - m), jnp.exp(lb - m)
                        t = wa + wb
                        att_ref[nat, :] = (wa * att_ref[nat, :] + wb * ocm[cm, :]) / t
                        lse_ref[nat, :] = m + jnp.log(t)
                        return carry
                    lax.fori_loop(0, L // BLOCK, merge, 0)

        def gate_step(i, carry):
            rows = pl.ds(pl.multiple_of(i * tn, tn), tn)
            silu, _ = _silu_parts(g_ref[rows, :])
            y_ref[rows, :] = (att_ref[rows, :] * silu).astype(BF16)
            return carry
        lax.fori_loop(0, S // tn, gate_step, 0)

    col = lambda j0: pl.BlockSpec((S, 128), lambda p, j0=j0: (0, j0 + p))
    const = lambda shape: pl.BlockSpec(shape, lambda p: (0,) * len(shape))
    out = pl.BlockSpec((S, 128), lambda p: (0, p))
    return pl.pallas_call(
        body, name="attn_fwd", grid=(npairs,),
        in_specs=[col(COL_AQ), col(COL_AK), col(COL_AV), col(COL_AG), const((1, 128)), const((1, 128)),
                  const((128, 128))],
        out_specs=[out, out, out],
        out_shape=[jax.ShapeDtypeStruct((S, ATTN_WIDTH), BF16), jax.ShapeDtypeStruct((S, ATTN_WIDTH), F32),
                   jax.ShapeDtypeStruct((S, ATTN_WIDTH), F32)],
        scratch_shapes=[pltpu.VMEM((S, 128), F32), pltpu.VMEM((S, 128), F32),
                        pltpu.VMEM((S, 128), BF16), pltpu.VMEM((S, 128), BF16), pltpu.VMEM((S, 128), BF16),
                        pltpu.VMEM((S, 128), F32), pltpu.VMEM((S, 128), F32)],
        compiler_params=_params(56, ("arbitrary",)),
    )(proj, proj, proj, proj, qg2, kg2, bd)


def _attn_bwd_prep(proj, dycat, att, bd):
    S = proj.shape[0]
    tm = 512

    def body(g_ref, dy_ref, att_ref, bd_ref, do_ref, dd_ref, dg_ref):
        silu, dsilu = _silu_parts(g_ref[...])
        dy = dy_ref[...]
        at = att_ref[...]
        do = dy * silu
        do_ref[...] = do
        dd_ref[...] = _split_dot(do * at, bd_ref[...])
        dg_ref[...] = (dy * at * dsilu).astype(BF16)

    tile = lambda j0: pl.BlockSpec((tm, 256), lambda i, j, j0=j0: (i, j0 + j))
    return pl.pallas_call(
        body, name="attn_bwd_prep", grid=(S // tm, ATTN_WIDTH // 256),
        in_specs=[tile(COL_AG // 2), tile(1), tile(0), pl.BlockSpec((256, 256), lambda i, j: (0, 0))],
        out_specs=[tile(0), tile(0), tile(0)],
        out_shape=[jax.ShapeDtypeStruct((S, ATTN_WIDTH), F32), jax.ShapeDtypeStruct((S, ATTN_WIDTH), F32),
                   jax.ShapeDtypeStruct((S, ATTN_WIDTH), BF16)],
        compiler_params=_params(32, ("arbitrary", "arbitrary")),
    )(proj, dycat, att, bd)


def _attn_bwd(proj, do, dd, lse, qg2, kg2, bd):
    S = proj.shape[0]
    npairs = ATTN_WIDTH // 128
    tn = 512

    def body(q_ref, k_ref, v_ref, do_ref, dd_ref, lse_ref, qg_ref, kg_ref, bd_ref,
             dq_ref, dk_ref, dv_ref, dqg_ref, dkg_ref,
             qn, kn, qc, kc, vc, doc, lsec, ddc, dqc, dkc, dvc, dqa, dka, dva):
        bdv = bd_ref[...]
        lo = lax.broadcasted_iota(jnp.int32, (BLOCK, 128), 1) < HEAD_DIM
        band_mask, first_mask = _band_masks()

        def norm_step(i, carry):
            rows = pl.ds(pl.multiple_of(i * tn, tn), tn)
            qv = q_ref[rows, :]
            kv = k_ref[rows, :]
            qn[rows, :] = (qv * lax.rsqrt(_split_dot(qv * qv, bdv) * (1.0 / HEAD_DIM) + EPS)) * (qg_ref[...] * SCALE)
            kn[rows, :] = (kv * lax.rsqrt(_split_dot(kv * kv, bdv) * (1.0 / HEAD_DIM) + EPS)) * kg_ref[...]
            return carry
        lax.fori_loop(0, S // tn, norm_step, 0)

        def column(t):
            return jnp.concatenate([t[:, 0:1], t[:, HEAD_DIM:HEAD_DIM + 1]], axis=0)

        def block(base, first):
            own = pl.ds(base, BLOCK)
            q2 = _two_heads(qc[own, :], lo)
            do2 = _two_heads(doc[own, :], lo)
            lse2 = column(lsec[own, :])
            dd2 = column(ddc[own, :])
            keys = own if first else pl.ds(base - BLOCK, 2 * BLOCK)
            valid = first_mask if first else band_mask
            kb = kc[keys, :]
            vb = vc[keys, :]
            p = jnp.exp(jnp.where(valid, _nt(q2, kb), NEG) - lse2)
            ds = p * (_nt(do2, vb) - dd2)
            pb = p.astype(BF16)
            dsb = ds.astype(BF16)
            dvc[keys, :] += _tn(pb, do2)
            dkc[keys, :] += _tn(dsb, q2)
            dq2 = jnp.dot(dsb, kb, preferred_element_type=F32)
            dqc[own, :] = jnp.where(lo, dq2[:BLOCK], dq2[BLOCK:])

        to_bf = lambda t: t.astype(BF16)
        keep = lambda t: t
        for d in DILATIONS:
            L = S // d
            _fill_class_major(qc, qn, d, S, to_bf)
            _fill_class_major(kc, kn, d, S, to_bf)
            _fill_class_major(vc, v_ref, d, S, to_bf)
            _fill_class_major(doc, do_ref, d, S, to_bf)
            _fill_class_major(lsec, lse_ref, d, S, keep)
            _fill_class_major(ddc, dd_ref, d, S, keep)

            def zero_step(i, carry):
                rows = pl.ds(pl.multiple_of(i * tn, tn), tn)
                dkc[rows, :] = jnp.zeros((tn, 128), F32)
                dvc[rows, :] = jnp.zeros((tn, 128), F32)
                return carry
            lax.fori_loop(0, S // tn, zero_step, 0)

            def class_body(r, carry, L=L):
                base0 = pl.multiple_of(r * L, BLOCK)
                block(base0, True)

                def nstep(n, c2):
                    block(pl.multiple_of(base0 + n * BLOCK, BLOCK), False)
                    return c2
                lax.fori_loop(1, L // BLOCK, nstep, 0)
                return carry
            lax.fori_loop(0, d, class_body, 0)

            for r in range(d):
                def fold(n, carry, r=r, L=L, d=d):
                    cm = pl.ds(pl.multiple_of(r * L + n * BLOCK, BLOCK), BLOCK)
                    if d == 1:
                        dqa[cm, :] = dqc[cm, :]
                        dka[cm, :] = dkc[cm, :]
                        dva[cm, :] = dvc[cm, :]
                    else:
                        nat = pl.ds(r + n * (BLOCK * d), BLOCK, stride=d)
                        dqa[nat, :] = dqa[nat, :] + dqc[cm, :]
                        dka[nat, :] = dka[nat, :] + dkc[cm, :]
                        dva[nat, :] = dva[nat, :] + dvc[cm, :]
                    return carry
                lax.fori_loop(0, L // BLOCK, fold, 0)

        def out_step(i, carry):
            dqg, dkg = carry
            rows = pl.ds(pl.multiple_of(i * tn, tn), tn)
            qv = q_ref[rows, :]
            kv = k_ref[rows, :]
            rq = lax.rsqrt(_split_dot(qv * qv, bdv) * (1.0 / HEAD_DIM) + EPS)
            rk = lax.rsqrt(_split_dot(kv * kv, bdv) * (1.0 / HEAD_DIM) + EPS)
            qh = qv * rq
            kh = kv * rk
            dqs = dqa[rows, :] * SCALE
            dkn = dka[rows, :]
            aq = dqs * qg_ref[...]
            ak = dkn * kg_ref[...]
            dq_ref[rows, :] = (rq * (aq - qh * (_split_dot(aq * qh, bdv) * (1.0 / HEAD_DIM)))).astype(BF16)
            dk_ref[rows, :] = (rk * (ak - kh * (_split_dot(ak * kh, bdv) * (1.0 / HEAD_DIM)))).astype(BF16)
            dv_ref[rows, :] = dva[rows, :].astype(BF16)
            dqg = dqg + jnp.sum(dqs * qh, axis=0, keepdims=True)
            dkg = dkg + jnp.sum(dkn * kh, axis=0, keepdims=True)
            return dqg, dkg
        zero = jnp.zeros((1, 128), F32)
        dqg, dkg = lax.fori_loop(0, S // tn, out_step, (zero, zero))
        dqg_ref[0] = dqg
        dkg_ref[0] = dkg

    once = pl.Buffered(1)
    col = lambda j0: pl.BlockSpec((S, 128), lambda p, j0=j0: (0, j0 + p), pipeline_mode=once)
    const = lambda shape: pl.BlockSpec(shape, lambda p: (0,) * len(shape))
    out = pl.BlockSpec((S, 128), lambda p: (0, p))
    gain_out = pl.BlockSpec((1, 1, 128), lambda p: (p, 0, 0))
    piece = jax.ShapeDtypeStruct((S, ATTN_WIDTH), BF16)
    gains = jax.ShapeDtypeStruct((npairs, 1, 128), F32)
    f32buf = pltpu.VMEM((S, 128), F32)
    bf16buf = pltpu.VMEM((S, 128), BF16)
    return pl.pallas_call(
        body, name="attn_bwd", grid=(npairs,),
        in_specs=[col(COL_AQ), col(COL_AK), col(COL_AV), col(0), col(0), col(0), const((1, 128)), const((1, 128)),
                  const((128, 128))],
        out_specs=[out, out, out, gain_out, gain_out],
        out_shape=[piece, piece, piece, gains, gains],
        scratch_shapes=[f32buf, f32buf, bf16buf, bf16buf, bf16buf, bf16buf, f32buf, f32buf,
                        f32buf, f32buf, f32buf, f32buf, f32buf, f32buf],
        compiler_params=_params(56, ("arbitrary",)),
    )(proj, proj, proj, do, dd, lse, qg2, kg2, bd)


def _mem_kv(mem, gain, wkv_bf, kg4, bd):
    def body(mem_ref, g_ref, w_ref, kg_ref, bd_ref, hm_ref, kraw_ref, mk_ref, mv_ref):
        mv_ = mem_ref[...]
        r = lax.rsqrt(jnp.mean(mv_ * mv_, axis=-1, keepdims=True) + EPS)
        hm = ((mv_ * r) * g_ref[...]).astype(BF16)
        hm_ref[...] = hm
        kv = jnp.dot(hm, w_ref[...], preferred_element_type=F32)
        kraw = kv[:, :MEM_WIDTH]
        kraw_ref[...] = kraw
        ms = _split_dot(kraw * kraw, bd_ref[...]) * (1.0 / HEAD_DIM)
        mk_ref[...] = (kraw * lax.rsqrt(ms + EPS)) * kg_ref[...]
        mv_ref[...] = kv[:, MEM_WIDTH:]

    sq = jax.ShapeDtypeStruct((MEM_LEN, MEM_WIDTH), F32)
    return pl.pallas_call(
        body, name="mem_kv",
        out_shape=[jax.ShapeDtypeStruct((MEM_LEN, D_MODEL), BF16), sq, sq, sq],
        compiler_params=_params(16),
    )(mem, gain, wkv_bf, kg4, bd)


def _mem_fwd(proj, mk, mv, qg4, bd):
    S = proj.shape[0]
    tm = 512

    def body(q_ref, g_ref, mk_ref, mv_ref, qg_ref, bd_ref, y_ref, om_ref):
        qv = q_ref[...]
        ms = _split_dot(qv * qv, bd_ref[...]) * (1.0 / HEAD_DIM)
        qs = (qv * lax.rsqrt(ms + EPS)) * (qg_ref[...] * SCALE)
        mkb = mk_ref[...].astype(BF16)
        mvb = mv_ref[...].astype(BF16)
        head = _head_index((tm, MEM_WIDTH))
        o = jnp.zeros((tm, MEM_WIDTH), F32)
        for h in range(4):
            s = _nt(jnp.where(head == h, qs, 0.0).astype(BF16), mkb)
            e = jnp.exp(s - jnp.max(s, axis=-1, keepdims=True))
            p = e * (1.0 / jnp.sum(e, axis=-1, keepdims=True))
            o = jnp.where(head == h, jnp.dot(p.astype(BF16), mvb, preferred_element_type=F32), o)
        om_ref[...] = o
        silu, _ = _silu_parts(g_ref[...])
        y_ref[...] = (o * silu).astype(BF16)

    col = lambda j: pl.BlockSpec((tm, MEM_WIDTH), lambda i, j=j: (i, j))
    const = lambda shape: pl.BlockSpec(shape, lambda i: (0,) * len(shape))
    tile = pl.BlockSpec((tm, MEM_WIDTH), lambda i: (i, 0))
    return pl.pallas_call(
        body, name="mem_fwd", grid=(S // tm,),
        in_specs=[col(11), col(12), const((MEM_LEN, MEM_WIDTH)), const((MEM_LEN, MEM_WIDTH)), const((1, MEM_WIDTH)),
                  const((MEM_WIDTH, MEM_WIDTH))],
        out_specs=[tile, tile],
        out_shape=[jax.ShapeDtypeStruct((S, MEM_WIDTH), BF16), jax.ShapeDtypeStruct((S, MEM_WIDTH), F32)],
        compiler_params=_params(24, ("arbitrary",)),
    )(proj, proj, mk, mv, qg4, bd)


def _mem_bwd(proj, dycat, om, mk, mv, qg4, bd):
    S = proj.shape[0]
    tm = 512

    def body(q_ref, g_ref, dy_ref, om_ref, mk_ref, mv_ref, qg_ref, bd_ref,
             dq_ref, dg_ref, dmk_ref, dmv_ref, dqg_ref):
        i = pl.program_id(0)

        @pl.when(i == 0)
        def _():
            dmk_ref[...] = jnp.zeros_like(dmk_ref)
            dmv_ref[...] = jnp.zeros_like(dmv_ref)
            dqg_ref[...] = jnp.zeros_like(dqg_ref)

        bdv = bd_ref[...]
        qv = q_ref[...]
        rq = lax.rsqrt(_split_dot(qv * qv, bdv) * (1.0 / HEAD_DIM) + EPS)
        qh = qv * rq
        qs = qh * (qg_ref[...] * SCALE)
        silu, dsilu = _silu_parts(g_ref[...])
        dy = dy_ref[...]
        o = om_ref[...]
        do = dy * silu
        dg_ref[...] = (dy * o * dsilu).astype(BF16)
        dd = _split_dot(do * o, bdv)
        mkb = mk_ref[...].astype(BF16)
        mvb = mv_ref[...].astype(BF16)
        head = _head_index((tm, MEM_WIDTH))
        dqs = jnp.zeros((tm, MEM_WIDTH), F32)
        for h in range(4):
            qhd = jnp.where(head == h, qs, 0.0).astype(BF16)
            doh = jnp.where(head == h, do, 0.0).astype(BF16)
            s = _nt(qhd, mkb)
            e = jnp.exp(s - jnp.max(s, axis=-1, keepdims=True))
            p = e * (1.0 / jnp.sum(e, axis=-1, keepdims=True))
            ds = p * (_nt(doh, mvb) - dd[:, h * HEAD_DIM:h * HEAD_DIM + 1])
            dsb = ds.astype(BF16)
            dmv_ref[...] += _tn(p.astype(BF16), doh)
            dmk_ref[...] += _tn(dsb, qhd)
            dqs = jnp.where(head == h, jnp.dot(dsb, mkb, preferred_element_type=F32), dqs)
        dqs = dqs * SCALE
        a = dqs * qg_ref[...]
        dq_ref[...] = (rq * (a - qh * (_split_dot(a * qh, bdv) * (1.0 / HEAD_DIM)))).astype(BF16)
        dqg_ref[...] += jnp.sum(dqs * qh, axis=0, keepdims=True)

    col = lambda j: pl.BlockSpec((tm, MEM_WIDTH), lambda i, j=j: (i, j))
    const = lambda shape: pl.BlockSpec(shape, lambda i: (0,) * len(shape))
    tile = pl.BlockSpec((tm, MEM_WIDTH), lambda i: (i, 0))
    piece = jax.ShapeDtypeStruct((S, MEM_WIDTH), BF16)
    sq = jax.ShapeDtypeStruct((MEM_LEN, MEM_WIDTH), F32)
    return pl.pallas_call(
        body, name="mem_bwd", grid=(S // tm,),
        in_specs=[col(11), col(12), col(3), tile, const((MEM_LEN, MEM_WIDTH)), const((MEM_LEN, MEM_WIDTH)),
                  const((1, MEM_WIDTH)), const((MEM_WIDTH, MEM_WIDTH))],
        out_specs=[tile, tile, const((MEM_LEN, MEM_WIDTH)), const((MEM_LEN, MEM_WIDTH)), const((1, MEM_WIDTH))],
        out_shape=[piece, piece, sq, sq, jax.ShapeDtypeStruct((1, MEM_WIDTH), F32)],
        compiler_params=_params(32, ("arbitrary",)),
    )(proj, proj, dycat, om, mk, mv, qg4, bd)


def _mem_kv_bwd(dmk, dmv, kraw, mem, gain, kg4, wkv_bf, hm_bf, bd):
    def body(dmk_ref, dmv_ref, kraw_ref, mem_ref, g_ref, kg_ref, w_ref, hm_ref, bd_ref, dw_ref, dg_ref, dkg_ref):
        bdv = bd_ref[...]
        kraw = kraw_ref[...]
        rk = lax.rsqrt(_split_dot(kraw * kraw, bdv) * (1.0 / HEAD_DIM) + EPS)
        kh = kraw * rk
        dmkv = dmk_ref[...]
        a = dmkv * kg_ref[...]
        dkraw = rk * (a - kh * (_split_dot(a * kh, bdv) * (1.0 / HEAD_DIM)))
        dkg_ref[...] = jnp.sum(dmkv * kh, axis=0, keepdims=True)
        dkv = jnp.concatenate([dkraw, dmv_ref[...]], axis=1).astype(BF16)
        dw_ref[...] = _tn(hm_ref[...], dkv)
        dhm = _nt(dkv, w_ref[...])
        mv_ = mem_ref[...]
        r = lax.rsqrt(jnp.mean(mv_ * mv_, axis=-1, keepdims=True) + EPS)
        dg_ref[...] = jnp.sum(dhm * (mv_ * r), axis=0, keepdims=True)

    return pl.pallas_call(
        body, name="mem_kv_bwd",
        out_shape=[jax.ShapeDtypeStruct((D_MODEL, 2 * MEM_WIDTH), F32), jax.ShapeDtypeStruct((1, D_MODEL), F32),
                   jax.ShapeDtypeStruct((1, MEM_WIDTH), F32)],
        compiler_params=_params(24),
    )(dmk, dmv, kraw, mem, gain, kg4, wkv_bf, hm_bf, bd)


def _out_loss(yg, ya, ym, x, tgt, wout_bf):
    S, D = x.shape
    tm = 256

    def body(yg_ref, ya_ref, ym_ref, x_ref, t_ref, w_ref, dout_ref, dycat_ref, dw_ref, loss_ref):
        @pl.when(pl.program_id(0) == 0)
        def _():
            dw_ref[...] = jnp.zeros_like(dw_ref)
            loss_ref[...] = jnp.zeros_like(loss_ref)

        ycat = jnp.concatenate([yg_ref[...], ya_ref[...], ym_ref[...]], axis=1)
        w = w_ref[...]
        diff = (x_ref[...] + jnp.dot(ycat, w, preferred_element_type=F32)) - t_ref[...]
        loss_ref[...] += jnp.sum(diff * diff, axis=0, keepdims=True)
        dout = diff * (1.0 / D)
        dout_ref[...] = dout
        db = dout.astype(BF16)
        dycat_ref[...] = _nt(db, w)
        dw_ref[...] += _tn(ycat, db)

    tile = lambda w: pl.BlockSpec((tm, w), lambda i: (i, 0))
    const = lambda shape: pl.BlockSpec(shape, lambda i: (0,) * len(shape))
    return pl.pallas_call(
        body, name="out_loss", grid=(S // tm,),
        in_specs=[tile(GMLP_WIDTH), tile(ATTN_WIDTH), tile(MEM_WIDTH), tile(D), tile(D), const((D, D))],
        out_specs=[tile(D), tile(D), const((D, D)), const((1, D))],
        out_shape=[jax.ShapeDtypeStruct((S, D), F32), jax.ShapeDtypeStruct((S, D), F32),
                   jax.ShapeDtypeStruct((D, D), F32), jax.ShapeDtypeStruct((1, D), F32)],
        compiler_params=_params(40, ("arbitrary",)),
    )(yg, ya, ym, x, tgt, wout_bf)


def _piece_specs(pieces, tm):
    return [pl.BlockSpec((tm, p.shape[1]), lambda i: (i, 0)) for p in pieces]


def _in_bwd_dx(pieces, x, dout, gain, win_bf):
    S, D = x.shape
    N = win_bf.shape[1]
    tm = 256
    n = len(pieces)

    def body(*refs):
        piece_refs = refs[:n]
        x_ref, dout_ref, g_ref, w_ref, gx_ref, dg_ref = refs[n:]

        @pl.when(pl.program_id(0) == 0)
        def _():
            dg_ref[...] = jnp.zeros_like(dg_ref)

        dproj = jnp.concatenate([r[...] for r in piece_refs], axis=1)
        dh = _nt(dproj, w_ref[...])
        xv = x_ref[...]
        r = lax.rsqrt(jnp.mean(xv * xv, axis=-1, keepdims=True) + EPS)
        xh = xv * r
        a = dh * g_ref[...]
        gx_ref[...] = dout_ref[...] + r * (a - xh * jnp.mean(a * xh, axis=-1, keepdims=True))
        dg_ref[...] += jnp.sum(dh * xh, axis=0, keepdims=True)

    tile = pl.BlockSpec((tm, D), lambda i: (i, 0))
    const = lambda shape: pl.BlockSpec(shape, lambda i: (0,) * len(shape))
    return pl.pallas_call(
        body, name="in_bwd_dx", grid=(S // tm,),
        in_specs=_piece_specs(pieces, tm) + [tile, tile, const((1, D)), const((D, N))],
        out_specs=[tile, const((1, D))],
        out_shape=[jax.ShapeDtypeStruct((S, D), F32), jax.ShapeDtypeStruct((1, D), F32)],
        compiler_params=_params(40, ("arbitrary",)),
    )(*pieces, x, dout, gain, win_bf)


def _in_bwd_dw(pieces, h_bf):
    S, D = h_bf.shape
    N = sum(p.shape[1] for p in pieces)
    tm = 256
    n = len(pieces)

    def body(*refs):
        piece_refs = refs[:n]
        h_ref, dw_ref = refs[n:]

        @pl.when(pl.program_id(0) == 0)
        def _():
            dw_ref[...] = jnp.zeros_like(dw_ref)

        dproj = jnp.concatenate([r[...] for r in piece_refs], axis=1)
        dw_ref[...] += _tn(h_ref[...], dproj)

    return pl.pallas_call(
        body, name="in_bwd_dw", grid=(S // tm,),
        in_specs=_piece_specs(pieces, tm) + [pl.BlockSpec((tm, D), lambda i: (i, 0))],
        out_specs=pl.BlockSpec((D, N), lambda i: (0, 0)),
        out_shape=jax.ShapeDtypeStruct((D, N), F32),
        compiler_params=_params(48, ("arbitrary",)),
    )(*pieces, h_bf)


def _place():
    x, y, c = lax.axis_index("x"), lax.axis_index("y"), lax.axis_index("c")
    chips = [(1 - x, y), (x, 1 - y), (1 - x, 1 - y)]
    return x, y, c, chips


def _all_gather(arrs, name):
    n = len(arrs)

    def body(*refs):
        ins, outs = refs[:n], refs[n:2 * n]
        send_sems, recv_sems, local_sems = refs[2 * n:]
        x, y, c, chips = _place()
        me, sibling = (x, y, c), (x, y, 1 - c)

        def rows(a, px, py, pc):
            m = ins[a].shape[0]
            return outs[a].at[pl.ds((4 * px + 2 * py + pc) * m, m), :]

        def copy(a, k, block, to, src=None):
            return pltpu.make_async_remote_copy(
                src_ref=rows(a, *block) if src is None else src, dst_ref=rows(a, *block),
                send_sem=send_sems.at[a, k], recv_sem=recv_sems.at[a, k], device_id=to, device_id_type=MESH)

        mine = [pltpu.make_async_copy(ins[a], rows(a, *me), local_sems.at[a]) for a in range(n)]
        for cp in mine:
            cp.start()
        first = []
        for a in range(n):
            first.append(copy(a, 0, me, sibling, src=ins[a]))
            first += [copy(a, 1 + j, me, (*chip, c), src=ins[a]) for j, chip in enumerate(chips)]
        for cp in first:
            cp.start()
        passed = []
        for j, chip in enumerate(chips):
            for a in range(n):
                copy(a, 1 + j, (*chip, c), me).wait_recv()
                fwd = copy(a, 4 + j, (*chip, c), sibling)
                fwd.start()
                passed.append(fwd)
        for a in range(n):
            copy(a, 0, sibling, me).wait_recv()
            for j, chip in enumerate(chips):
                copy(a, 4 + j, (*chip, 1 - c), me).wait_recv()
        for cp in first + passed:
            cp.wait_send()
        for cp in mine:
            cp.wait()

    vmem = pl.BlockSpec(memory_space=pltpu.VMEM)
    return pl.pallas_call(
        body, name=name,
        out_shape=[jax.ShapeDtypeStruct((N_DEV * a.shape[0], a.shape[1]), a.dtype) for a in arrs],
        in_specs=[vmem] * n, out_specs=[vmem] * n,
        scratch_shapes=[pltpu.SemaphoreType.DMA((n, 7)), pltpu.SemaphoreType.DMA((n, 7)), pltpu.SemaphoreType.DMA((n,))],
        compiler_params=_params(40),
    )(*arrs)


def _reduce_scatter(arrs, name):
    n = len(arrs)
    tr = 128

    def body(*refs):
        ins, outs = refs[:n], refs[n:2 * n]
        half, quarter = refs[2 * n:3 * n], refs[3 * n:4 * n]
        send_sems, recv_sems = refs[4 * n:]
        x, y, c, chips = _place()
        sibling = (x, y, 1 - c)

        to_sibling = [pltpu.make_async_remote_copy(
            src_ref=ins[a].at[2 * q + (1 - c)], dst_ref=half[a].at[q], send_sem=send_sems.at[a, q],
            recv_sem=recv_sems.at[a, q], device_id=sibling, device_id_type=MESH) for a in range(n) for q in range(4)]
        for cp in to_sibling:
            cp.start()
        for cp in to_sibling:
            cp.wait_recv()

        def add_rows(a, fn):
            m = ins[a].shape[1]
            def step(i, carry):
                fn(pl.ds(pl.multiple_of(i * tr, tr), tr))
                return carry
            lax.fori_loop(0, m // tr, step, 0)

        for a in range(n):
            for q in range(4):
                def add_half(rows, a=a, q=q):
                    half[a][q, rows, :] = ins[a][2 * q + c, rows, :] + half[a][q, rows, :]
                add_rows(a, add_half)

        to_chips = [pltpu.make_async_remote_copy(
            src_ref=half[a].at[2 * chip[0] + chip[1]], dst_ref=quarter[a].at[k], send_sem=send_sems.at[a, 4 + k],
            recv_sem=recv_sems.at[a, 4 + k], device_id=(*chip, c), device_id_type=MESH)
            for a in range(n) for k, chip in enumerate(chips)]
        for cp in to_chips:
            cp.start()
        for cp in to_chips:
            cp.wait_recv()
        for a in range(n):
            def add_quarters(rows, a=a):
                outs[a][rows, :] = ((half[a][2 * x + y, rows, :] + quarter[a][0, rows, :])
                                    + (quarter[a][1, rows, :] + quarter[a][2, rows, :]))
            add_rows(a, add_quarters)
        for cp in to_sibling + to_chips:
            cp.wait_send()

    vmem = pl.BlockSpec(memory_space=pltpu.VMEM)
    return pl.pallas_call(
        body, name=name,
        out_shape=[jax.ShapeDtypeStruct(a.shape[1:], F32) for a in arrs],
        in_specs=[vmem] * n, out_specs=[vmem] * n,
        scratch_shapes=[pltpu.VMEM((4,) + a.shape[1:], F32) for a in arrs]
        + [pltpu.VMEM((3,) + a.shape[1:], F32) for a in arrs]
        + [pltpu.SemaphoreType.DMA((n, 7)), pltpu.SemaphoreType.DMA((n, 7))],
        compiler_params=_params(48),
    )(*arrs)


def _adamw_math(w, g, m, v):
    m = ADAM_B1 * m + (1.0 - ADAM_B1) * g
    v = ADAM_B2 * v + (1.0 - ADAM_B2) * (g * g)
    m_hat = m / (1.0 - ADAM_B1 ** ADAM_STEP)
    v_hat = v / (1.0 - ADAM_B2 ** ADAM_STEP)
    delta = -ADAM_LR * (m_hat / (jnp.sqrt(v_hat) + ADAM_EPS) + ADAM_WD * w)
    return delta, m, v


def _adamw(w, g, m, v, name):
    R, C = w.shape
    tr = 128 if R % 128 == 0 else R

    def body(w_ref, g_ref, m_ref, v_ref, d_ref, nm_ref, nv_ref):
        d_ref[...], nm_ref[...], nv_ref[...] = _adamw_math(w_ref[...], g_ref[...], m_ref[...], v_ref[...])

    tile = pl.BlockSpec((tr, C), lambda i: (i, 0))
    out = jax.ShapeDtypeStruct((R, C), F32)
    return pl.pallas_call(
        body, name=name, grid=(R // tr,), in_specs=[tile] * 4, out_specs=[tile] * 3, out_shape=[out] * 3,
        compiler_params=_params(16, ("arbitrary",)),
    )(w, g, m, v)


def _adamw_summed(w, g_all, m, v, name):
    R, C = w.shape

    def body(w_ref, g_ref, m_ref, v_ref, gs_ref, d_ref, nm_ref, nv_ref):
        g = g_ref[0:R, :]
        for k in range(1, N_DEV):
            g = g + g_ref[k * R:(k + 1) * R, :]
        gs_ref[...] = g
        d_ref[...], nm_ref[...], nv_ref[...] = _adamw_math(w_ref[...], g, m_ref[...], v_ref[...])

    out = jax.ShapeDtypeStruct((R, C), F32)
    return pl.pallas_call(body, name=name, out_shape=[out] * 4, compiler_params=_params(16))(w, g_all, m, v)


SMALL = ("norm_gain", "gmlp_v_gain", "gmlp_w_s", "gmlp_b", "attn_q_gain", "attn_k_gain", "mem_norm_gain",
         "mem_q_gain", "mem_k_gain")
WEIGHTS = ("norm_gain", "w_in", "gmlp_v_gain", "gmlp_w_s", "gmlp_b", "attn_q_gain", "attn_k_gain",
           "mem_norm_gain", "w_mem_kv", "mem_q_gain", "mem_k_gain", "w_out")


def _pack(tree):
    return jnp.concatenate([tree[k].reshape(-1) for k in SMALL]).reshape(-1, 128)


def _unpack(packed, like):
    flat = packed.reshape(-1)
    out, at = {}, 0
    for k in SMALL:
        out[k] = flat[at:at + like[k].size].reshape(like[k].shape)
        at += like[k].size
    return out


def _local_grads(x, mem, tgt, w, win_bf, wkv_bf, wout_bf):
    bd128, bd256 = _head_blockdiag(128), _head_blockdiag(256)
    gain = w["norm_gain"].reshape(1, D_MODEL)
    vg = w["gmlp_v_gain"].reshape(1, GMLP_WIDTH)
    w_s = w["gmlp_w_s"].reshape(4, CHUNK, CHUNK)
    b2 = jnp.repeat(w["gmlp_b"].reshape(4, CHUNK).T, HEAD_DIM, axis=1)
    qg2 = jnp.tile(w["attn_q_gain"].reshape(1, HEAD_DIM), (1, 2))
    kg2 = jnp.tile(w["attn_k_gain"].reshape(1, HEAD_DIM), (1, 2))
    mqg4 = jnp.tile(w["mem_q_gain"].reshape(1, HEAD_DIM), (1, 4))
    mkg4 = jnp.tile(w["mem_k_gain"].reshape(1, HEAD_DIM), (1, 4))
    mgain = w["mem_norm_gain"].reshape(1, D_MODEL)

    proj, h_bf = _rms_proj(x, gain, win_bf)
    yg = _gmlp_fwd(proj, vg, w_s, b2, bd256)
    ya, att, lse = _attn_fwd(proj, qg2, kg2, bd128)
    hm_bf, kraw, mk, mv = _mem_kv(mem, mgain, wkv_bf, mkg4, bd256)
    ym, om = _mem_fwd(proj, mk, mv, mqg4, bd256)
    dout, dycat, dwout, sq = _out_loss(yg, ya, ym, x, tgt, wout_bf)

    du, dgv, dgg, dws, db2, dvg = _gmlp_bwd(proj, dycat, vg, w_s, b2, bd256)
    do, dd, dag = _attn_bwd_prep(proj, dycat, att, bd256)
    dq, dk, dv, dqg, dkg = _attn_bwd(proj, do, dd, lse, qg2, kg2, bd128)
    dmq, dmg, dmk, dmv, dmqg = _mem_bwd(proj, dycat, om, mk, mv, mqg4, bd256)
    dwkv, dmgain, dmkg = _mem_kv_bwd(dmk, dmv, kraw, mem, mgain, mkg4, wkv_bf, hm_bf, bd256)
    pieces = [du, dgv, dgg, dq, dk, dv, dag, dmq, dmg]
    grad_x, dgain = _in_bwd_dx(pieces, x, dout, gain, win_bf)
    dwin = _in_bwd_dw(pieces, h_bf)

    small = {
        "norm_gain": dgain,
        "gmlp_v_gain": dvg,
        "gmlp_w_s": dws,
        "gmlp_b": db2[:, ::HEAD_DIM].T,
        "attn_q_gain": dqg.reshape(-1, HEAD_DIM).sum(axis=0),
        "attn_k_gain": dkg.reshape(-1, HEAD_DIM).sum(axis=0),
        "mem_norm_gain": dmgain,
        "mem_q_gain": dmqg.reshape(-1, HEAD_DIM).sum(axis=0),
        "mem_k_gain": dmkg.reshape(-1, HEAD_DIM).sum(axis=0),
    }
    return sq, grad_x, dwin, dwkv, dwout, small


def kernel(x, mem, norm_gain, w_in, gmlp_v_gain, gmlp_w_s, gmlp_b, attn_q_gain, attn_k_gain, mem_norm_gain, w_mem_kv, mem_q_gain, mem_k_gain, w_out, loss_target, m_norm_gain, m_w_in, m_gmlp_v_gain, m_gmlp_w_s, m_gmlp_b, m_attn_q_gain, m_attn_k_gain, m_mem_norm_gain, m_w_mem_kv, m_mem_q_gain, m_mem_k_gain, m_w_out, v_norm_gain, v_w_in, v_gmlp_v_gain, v_gmlp_w_s, v_gmlp_b, v_attn_q_gain, v_attn_k_gain, v_mem_norm_gain, v_w_mem_kv, v_mem_q_gain, v_mem_k_gain, v_w_out):
    w = dict(norm_gain=norm_gain, w_in=w_in, gmlp_v_gain=gmlp_v_gain, gmlp_w_s=gmlp_w_s, gmlp_b=gmlp_b,
             attn_q_gain=attn_q_gain, attn_k_gain=attn_k_gain, mem_norm_gain=mem_norm_gain, w_mem_kv=w_mem_kv,
             mem_q_gain=mem_q_gain, mem_k_gain=mem_k_gain, w_out=w_out)
    m = dict(norm_gain=m_norm_gain, w_in=m_w_in, gmlp_v_gain=m_gmlp_v_gain, gmlp_w_s=m_gmlp_w_s, gmlp_b=m_gmlp_b,
             attn_q_gain=m_attn_q_gain, attn_k_gain=m_attn_k_gain, mem_norm_gain=m_mem_norm_gain,
             w_mem_kv=m_w_mem_kv, mem_q_gain=m_mem_q_gain, mem_k_gain=m_mem_k_gain, w_out=m_w_out)
    v = dict(norm_gain=v_norm_gain, w_in=v_w_in, gmlp_v_gain=v_gmlp_v_gain, gmlp_w_s=v_gmlp_w_s, gmlp_b=v_gmlp_b,
             attn_q_gain=v_attn_q_gain, attn_k_gain=v_attn_k_gain, mem_norm_gain=v_mem_norm_gain,
             w_mem_kv=v_w_mem_kv, mem_q_gain=v_mem_q_gain, mem_k_gain=v_mem_k_gain, w_out=v_w_out)
    S = x.shape[1]
    n_in = w_in.shape[2]
    win_sh, wkv_sh, wout_sh = w_in[0], w_mem_kv[0], w_out[0]

    win_all, wkv_bf, wout_bf = _all_gather(
        [win_sh.astype(BF16), wkv_sh.astype(BF16), wout_sh.astype(BF16)], "gather_weights")
    win_bf = win_all.reshape(N_DEV, D_MODEL, n_in).transpose(1, 0, 2).reshape(D_MODEL, N_DEV * n_in)

    sq, grad_x, dwin, dwkv, dwout, small = _local_grads(x[0], mem[0], loss_target[0], w, win_bf, wkv_bf, wout_bf)
    loss = lax.psum(jnp.sum(sq) * (0.5 / D_MODEL), ("x", "y", "c"))

    dwin_blocks = dwin.reshape(D_MODEL, N_DEV, n_in).transpose(1, 0, 2)
    (g_win,) = _reduce_scatter([dwin_blocks], "reduce_w_in")
    g_wkv, g_wout = _reduce_scatter(
        [dwkv.reshape(N_DEV, D_MODEL // N_DEV, -1), dwout.reshape(N_DEV, D_MODEL // N_DEV, -1)], "reduce_w_kv_out")
    (small_all,) = _all_gather([_pack(small)], "gather_small_grads")

    out_g, out_d, out_m, out_v = {}, {}, {}, {}
    for name, g in (("w_in", g_win), ("w_mem_kv", g_wkv), ("w_out", g_wout)):
        d_, m_, v_ = _adamw(w[name][0], g, m[name][0], v[name][0], "adamw_" + name)
        out_g[name], out_d[name], out_m[name], out_v[name] = g[None], d_[None], m_[None], v_[None]
    gs, ds, ms, vs = _adamw_summed(_pack(w), small_all, _pack(m), _pack(v), "adamw_small")
    for tree, packed in ((out_g, gs), (out_d, ds), (out_m, ms), (out_v, vs)):
        tree.update(_unpack(packed, w))

    return (loss, grad_x[None], *[out_g[k] for k in WEIGHTS], *[out_d[k] for k in WEIGHTS],
            *[out_m[k] for k in WEIGHTS], *[out_v[k] for k in WEIGHTS])
```

```python
import functools
import math

import jax
import jax.numpy as jnp
from jax import lax
from jax.experimental import pallas as pl
from jax.experimental.pallas import tpu as pltpu

F32 = jnp.float32
BF16 = jnp.bfloat16

N_DEV = 8
D_MODEL = 1024
HEAD_DIM = 64
GMLP_WIDTH = 256
ATTN_WIDTH = 512
MEM_WIDTH = 256
MEM_LEN = 256
IN_WIDTH = 3 * GMLP_WIDTH + 4 * ATTN_WIDTH + 2 * MEM_WIDTH
CHUNK = 128
BLOCK = 128
DILATIONS = (1, 4, 16)
EPS = 1e-6
SCALE = 1.0 / math.sqrt(HEAD_DIM)
NEG = -1e30

ADAM_LR = 0.001
ADAM_B1 = 0.9
ADAM_B2 = 0.999
ADAM_EPS = 1e-08
ADAM_WD = 0.01
ADAM_STEP = 10

MIB = 1024 * 1024
MESH = pl.DeviceIdType.MESH

COL_AQ, COL_AK, COL_AV, COL_AG = 6, 10, 14, 18


def _params(vmem_mib, semantics=None):
    kw = dict(vmem_limit_bytes=vmem_mib * MIB)
    if semantics is not None:
        kw["dimension_semantics"] = semantics
    return pltpu.CompilerParams(**kw)


def _split_dot(x, sel_bf):
    hi = x.astype(BF16)
    lo = (x - hi.astype(F32)).astype(BF16)
    return jnp.dot(hi, sel_bf, preferred_element_type=F32) + jnp.dot(lo, sel_bf, preferred_element_type=F32)


def _nt(a, b):
    return lax.dot_general(a, b, (((1,), (1,)), ((), ())), preferred_element_type=F32)


def _tn(a, b):
    return lax.dot_general(a, b, (((0,), (0,)), ((), ())), preferred_element_type=F32)


def _silu_parts(g):
    sg = jax.nn.sigmoid(g)
    return g * sg, sg * (1.0 + g * (1.0 - sg))


def _head_index(shape):
    return lax.shift_right_logical(lax.broadcasted_iota(jnp.int32, shape, 1), HEAD_DIM.bit_length() - 1)


def _head_blockdiag(width):
    i = jnp.arange(width) // HEAD_DIM
    return (i[:, None] == i[None, :]).astype(BF16)


def _rms_proj(x, gain, w_bf):
    S, D = x.shape
    N = w_bf.shape[1]
    tm = 256

    def body(x_ref, g_ref, w_ref, proj_ref, h_ref):
        xv = x_ref[...]
        r = lax.rsqrt(jnp.mean(xv * xv, axis=-1, keepdims=True) + EPS)
        h = ((xv * r) * g_ref[...]).astype(BF16)
        h_ref[...] = h
        proj_ref[...] = jnp.dot(h, w_ref[...], preferred_element_type=F32)

    return pl.pallas_call(
        body, name="rms_proj", grid=(S // tm,),
        in_specs=[pl.BlockSpec((tm, D), lambda i: (i, 0)), pl.BlockSpec((1, D), lambda i: (0, 0)),
                  pl.BlockSpec((D, N), lambda i: (0, 0))],
        out_specs=[pl.BlockSpec((tm, N), lambda i: (i, 0)), pl.BlockSpec((tm, D), lambda i: (i, 0))],
        out_shape=[jax.ShapeDtypeStruct((S, N), F32), jax.ShapeDtypeStruct((S, D), BF16)],
        compiler_params=_params(40, ("arbitrary",)),
    )(x, gain, w_bf)


def _gmlp_masked_weights(ws_ref, transpose):
    t = lax.broadcasted_iota(jnp.int32, (CHUNK, CHUNK), 0)
    s = lax.broadcasted_iota(jnp.int32, (CHUNK, CHUNK), 1)
    parts = []
    for h in range(4):
        wm = jnp.where(s <= t, ws_ref[h], 0.0)
        parts.append(wm.T if transpose else wm)
    return jnp.concatenate(parts, axis=1).astype(BF16)


def _head_stack(v, head):
    return jnp.concatenate([jnp.where(head == h, v, 0.0) for h in range(4)], axis=0).astype(BF16)


def _gmlp_fwd(proj, vg, w_s, b2, bd):
    S = proj.shape[0]
    tm = 512

    def body(u_ref, v_ref, g_ref, vg_ref, ws_ref, b2_ref, bd_ref, y_ref):
        v = v_ref[...]
        ms = _split_dot(v * v, bd_ref[...]) * (1.0 / HEAD_DIM)
        vn = (v * lax.rsqrt(ms + EPS)) * vg_ref[...]
        wcat = _gmlp_masked_weights(ws_ref, False)
        head = _head_index((CHUNK, GMLP_WIDTH))
        for c in range(tm // CHUNK):
            rows = slice(c * CHUNK, (c + 1) * CHUNK)
            sp = jnp.dot(wcat, _head_stack(vn[rows], head), preferred_element_type=F32) + b2_ref[...]
            silu, _ = _silu_parts(g_ref[rows, :])
            y_ref[rows, :] = ((u_ref[rows, :] * sp) * silu).astype(BF16)

    col = lambda j: pl.BlockSpec((tm, GMLP_WIDTH), lambda i, j=j: (i, j))
    const = lambda shape: pl.BlockSpec(shape, lambda i: (0,) * len(shape))
    return pl.pallas_call(
        body, name="gmlp_fwd", grid=(S // tm,),
        in_specs=[col(0), col(1), col(2), const((1, GMLP_WIDTH)), const((4, CHUNK, CHUNK)),
                  const((CHUNK, GMLP_WIDTH)), const((GMLP_WIDTH, GMLP_WIDTH))],
        out_specs=pl.BlockSpec((tm, GMLP_WIDTH), lambda i: (i, 0)),
        out_shape=jax.ShapeDtypeStruct((S, GMLP_WIDTH), BF16),
        compiler_params=_params(24, ("arbitrary",)),
    )(proj, proj, proj, vg, w_s, b2, bd)


def _gmlp_bwd(proj, dycat, vg, w_s, b2, bd):
    S = proj.shape[0]
    tm = 512
    nsteps = S // tm

    def body(u_ref, v_ref, g_ref, dy_ref, vg_ref, ws_ref, b2_ref, bd_ref,
             du_ref, dv_ref, dg_ref, dws_ref, db2_ref, dvg_ref):
        i = pl.program_id(0)

        @pl.when(i == 0)
        def _():
            dws_ref[...] = jnp.zeros_like(dws_ref)
            db2_ref[...] = jnp.zeros_like(db2_ref)
            dvg_ref[...] = jnp.zeros_like(dvg_ref)

        bdv = bd_ref[...]
        v = v_ref[...]
        ms = _split_dot(v * v, bdv) * (1.0 / HEAD_DIM)
        rv = lax.rsqrt(ms + EPS)
        xhat = v * rv
        vgv = vg_ref[...]
        vn = xhat * vgv
        wcat = _gmlp_masked_weights(ws_ref, False)
        wcat_t = _gmlp_masked_weights(ws_ref, True)
        head = _head_index((CHUNK, GMLP_WIDTH))
        dvg = jnp.zeros((1, GMLP_WIDTH), F32)
        for c in range(tm // CHUNK):
            rows = slice(c * CHUNK, (c + 1) * CHUNK)
            vn_c = vn[rows]
            spb = jnp.dot(wcat, _head_stack(vn_c, head), preferred_element_type=F32) + b2_ref[...]
            silu, dsilu = _silu_parts(g_ref[rows, :])
            dy = dy_ref[rows, :]
            u = u_ref[rows, :]
            du_ref[rows, :] = (dy * spb * silu).astype(BF16)
            dg_ref[rows, :] = (dy * u * spb * dsilu).astype(BF16)
            dsp = dy * u * silu
            db2_ref[...] += dsp
            dstack = _head_stack(dsp, head)
            dvn = jnp.dot(wcat_t, dstack, preferred_element_type=F32)
            dws_ref[...] += _nt(dstack, vn_c.astype(BF16))
            xh = xhat[rows]
            a = dvn * vgv
            mean_ax = _split_dot(a * xh, bdv) * (1.0 / HEAD_DIM)
            dv_ref[rows, :] = (rv[rows] * (a - xh * mean_ax)).astype(BF16)
            dvg = dvg + jnp.sum(dvn * xh, axis=0, keepdims=True)
        dvg_ref[...] += dvg

        @pl.when(i == nsteps - 1)
        def _():
            t = lax.broadcasted_iota(jnp.int32, (4 * CHUNK, CHUNK), 0) % CHUNK
            s = lax.broadcasted_iota(jnp.int32, (4 * CHUNK, CHUNK), 1)
            dws_ref[...] = jnp.where(s <= t, dws_ref[...], 0.0)
            db2_ref[...] = _split_dot(db2_ref[...], bdv)

    col = lambda j: pl.BlockSpec((tm, GMLP_WIDTH), lambda i, j=j: (i, j))
    const = lambda shape: pl.BlockSpec(shape, lambda i: (0,) * len(shape))
    tile = pl.BlockSpec((tm, GMLP_WIDTH), lambda i: (i, 0))
    piece = jax.ShapeDtypeStruct((S, GMLP_WIDTH), BF16)
    return pl.pallas_call(
        body, name="gmlp_bwd", grid=(nsteps,),
        in_specs=[col(0), col(1), col(2), col(0), const((1, GMLP_WIDTH)), const((4, CHUNK, CHUNK)),
                  const((CHUNK, GMLP_WIDTH)), const((GMLP_WIDTH, GMLP_WIDTH))],
        out_specs=[tile, tile, tile, const((4 * CHUNK, CHUNK)), const((CHUNK, GMLP_WIDTH)), const((1, GMLP_WIDTH))],
        out_shape=[piece, piece, piece, jax.ShapeDtypeStruct((4 * CHUNK, CHUNK), F32),
                   jax.ShapeDtypeStruct((CHUNK, GMLP_WIDTH), F32), jax.ShapeDtypeStruct((1, GMLP_WIDTH), F32)],
        compiler_params=_params(32, ("arbitrary",)),
    )(proj, proj, proj, dycat, vg, w_s, b2, bd)


def _band_mask():
    qi = lax.broadcasted_iota(jnp.int32, (2 * BLOCK, 2 * BLOCK), 0) % BLOCK
    ki = lax.broadcasted_iota(jnp.int32, (2 * BLOCK, 2 * BLOCK), 1)
    return ((ki < BLOCK) & (ki >= qi)) | ((ki >= BLOCK) & ((ki - BLOCK) <= qi))


def _first_block_bias(blk, blocks_per_class):
    kcol = lax.broadcasted_iota(jnp.int32, (1, 2 * BLOCK), 1)
    kill = jnp.where((blk & (blocks_per_class - 1)) == 0, NEG, 0.0)
    return jnp.where(kcol < BLOCK, kill, 0.0)


def _two_heads(q, lo):
    zero = jnp.zeros_like(q)
    return jnp.concatenate([jnp.where(lo, q, zero), jnp.where(lo, zero, q)], axis=0)


def _fill_class_major(dst, src, d, S, convert, pad=0):
    L = S // d
    for r in range(d):
        def step(n, carry, r=r):
            rows = src[pl.ds(r + n * (BLOCK * d), BLOCK, stride=d), :] if d > 1 else \
                src[pl.ds(pl.multiple_of(n * BLOCK, BLOCK), BLOCK), :]
            dst[pl.ds(pl.multiple_of(pad + r * L + n * BLOCK, BLOCK), BLOCK), :] = convert(rows)
            return carry
        lax.fori_loop(0, L // BLOCK, step, 0)


def _for_blocks(n_blocks, unroll, fn):
    def group(g, carry):
        for u in range(unroll):
            fn(g * unroll + u)
        return carry
    lax.fori_loop(0, n_blocks // unroll, group, 0)


def _attn_fwd(proj, qg2, kg2, bd):
    S = proj.shape[0]
    npairs = ATTN_WIDTH // 128
    tn = 512

    def body(q_ref, k_ref, v_ref, g_ref, qg_ref, kg_ref, bd_ref, y_ref, att_ref, lse_ref,
             qn, kn, qc, kc, vc, ocm, lcm):
        bdv = bd_ref[...]
        lo = lax.broadcasted_iota(jnp.int32, (BLOCK, 128), 1) < HEAD_DIM
        band_mask = _band_mask()
        kc[pl.ds(0, BLOCK), :] = jnp.zeros((BLOCK, 128), BF16)
        vc[pl.ds(0, BLOCK), :] = jnp.zeros((BLOCK, 128), BF16)

        def norm_step(i, carry):
            rows = pl.ds(pl.multiple_of(i * tn, tn), tn)
            qv = q_ref[rows, :]
            kv = k_ref[rows, :]
            qn[rows, :] = (qv * lax.rsqrt(_split_dot(qv * qv, bdv) * (1.0 / HEAD_DIM) + EPS)) * (qg_ref[...] * SCALE)
            kn[rows, :] = (kv * lax.rsqrt(_split_dot(kv * kv, bdv) * (1.0 / HEAD_DIM) + EPS)) * kg_ref[...]
            return carry
        lax.fori_loop(0, S // tn, norm_step, 0)

        def block(blk, blocks_per_class, o_dst, l_dst):
            own = pl.ds(pl.multiple_of(blk * BLOCK, BLOCK), BLOCK)
            keys = pl.ds(pl.multiple_of(blk * BLOCK, BLOCK), 2 * BLOCK)
            q2 = _two_heads(qc[own, :], lo)
            s = jnp.where(band_mask, _nt(q2, kc[keys, :]), NEG) + _first_block_bias(blk, blocks_per_class)
            m = jnp.max(s, axis=-1, keepdims=True)
            e = jnp.exp(s - m)
            l = jnp.sum(e, axis=-1, keepdims=True)
            o2 = jnp.dot((e * (1.0 / l)).astype(BF16), vc[keys, :], preferred_element_type=F32)
            lse = m + jnp.log(l)
            o_dst[own, :] = jnp.where(lo, o2[:BLOCK], o2[BLOCK:])
            l_dst[own, :] = jnp.where(lo, lse[:BLOCK], lse[BLOCK:])

        to_bf = lambda t: t.astype(BF16)
        for d in DILATIONS:
            L = S // d
            _fill_class_major(qc, qn, d, S, to_bf)
            _fill_class_major(kc, kn, d, S, to_bf, pad=BLOCK)
            _fill_class_major(vc, v_ref, d, S, to_bf, pad=BLOCK)
            o_dst, l_dst = (att_ref, lse_ref) if d == 1 else (ocm, lcm)
            _for_blocks(S // BLOCK, 4, functools.partial(block, blocks_per_class=L // BLOCK, o_dst=o_dst, l_dst=l_dst))

            if d > 1:
                for r in range(d):
                    def merge(n, carry, r=r, L=L, d=d):
                        nat = pl.ds(r + n * (BLOCK * d), BLOCK, stride=d)
                        cm = pl.ds(pl.multiple_of(r * L + n * BLOCK, BLOCK), BLOCK)
                        la, lb = lse_ref[nat, :], lcm[cm, :]
                        m = jnp.maximum(la, lb)
                        wa, wb = jnp.exp(la - m), jnp.exp(lb - m)
                        t = wa + wb
                        att_ref[nat, :] = (wa * att_ref[nat, :] + wb * ocm[cm, :]) / t
                        lse_ref[nat, :] = m + jnp.log(t)
                        return carry
                    lax.fori_loop(0, L // BLOCK, merge, 0)

        def gate_step(i, carry):
            rows = pl.ds(pl.multiple_of(i * tn, tn), tn)
            silu, _ = _silu_parts(g_ref[rows, :])
            y_ref[rows, :] = (att_ref[rows, :] * silu).astype(BF16)
            return carry
        lax.fori_loop(0, S // tn, gate_step, 0)

    col = lambda j0: pl.BlockSpec((S, 128), lambda p, j0=j0: (0, j0 + p))
    const = lambda shape: pl.BlockSpec(shape, lambda p: (0,) * len(shape))
    out = pl.BlockSpec((S, 128), lambda p: (0, p))
    return pl.pallas_call(
        body, name="attn_fwd", grid=(npairs,),
        in_specs=[col(COL_AQ), col(COL_AK), col(COL_AV), col(COL_AG), const((1, 128)), const((1, 128)),
                  const((128, 128))],
        out_specs=[out, out, out],
        out_shape=[jax.ShapeDtypeStruct((S, ATTN_WIDTH), BF16), jax.ShapeDtypeStruct((S, ATTN_WIDTH), F32),
                   jax.ShapeDtypeStruct((S, ATTN_WIDTH), F32)],
        scratch_shapes=[pltpu.VMEM((S, 128), F32), pltpu.VMEM((S, 128), F32),
                        pltpu.VMEM((S, 128), BF16), pltpu.VMEM((S + BLOCK, 128), BF16),
                        pltpu.VMEM((S + BLOCK, 128), BF16),
                        pltpu.VMEM((S, 128), F32), pltpu.VMEM((S, 128), F32)],
        compiler_params=_params(56, ("arbitrary",)),
    )(proj, proj, proj, proj, qg2, kg2, bd)


def _attn_bwd_prep(proj, dycat, att, bd):
    S = proj.shape[0]
    tm = 512

    def body(g_ref, dy_ref, att_ref, bd_ref, do_ref, dd_ref, dg_ref):
        silu, dsilu = _silu_parts(g_ref[...])
        dy = dy_ref[...]
        at = att_ref[...]
        do = dy * silu
        do_ref[...] = do
        dd_ref[...] = _split_dot(do * at, bd_ref[...])
        dg_ref[...] = (dy * at * dsilu).astype(BF16)

    tile = lambda j0: pl.BlockSpec((tm, 256), lambda i, j, j0=j0: (i, j0 + j))
    return pl.pallas_call(
        body, name="attn_bwd_prep", grid=(S // tm, ATTN_WIDTH // 256),
        in_specs=[tile(COL_AG // 2), tile(1), tile(0), pl.BlockSpec((256, 256), lambda i, j: (0, 0))],
        out_specs=[tile(0), tile(0), tile(0)],
        out_shape=[jax.ShapeDtypeStruct((S, ATTN_WIDTH), F32), jax.ShapeDtypeStruct((S, ATTN_WIDTH), F32),
                   jax.ShapeDtypeStruct((S, ATTN_WIDTH), BF16)],
        compiler_params=_params(32, ("arbitrary", "arbitrary")),
    )(proj, dycat, att, bd)


def _attn_bwd(proj, do, dd, lse, qg2, kg2, bd):
    S = proj.shape[0]
    npairs = ATTN_WIDTH // 128
    tn = 512

    def body(q_ref, k_ref, v_ref, do_ref, dd_ref, lse_ref, qg_ref, kg_ref, bd_ref,
             dq_ref, dk_ref, dv_ref, dqg_ref, dkg_ref,
             qn, kn, qc, kc, vc, doc, lsec, ddc, dqc, dk_own, dv_own, dk_prev, dv_prev, dqa, dka, dva):
        bdv = bd_ref[...]
        lo = lax.broadcasted_iota(jnp.int32, (BLOCK, 128), 1) < HEAD_DIM
        band_mask = _band_mask()
        zeros_bf = jnp.zeros((BLOCK, 128), BF16)
        zeros_f = jnp.zeros((BLOCK, 128), F32)
        kc[pl.ds(0, BLOCK), :] = zeros_bf
        vc[pl.ds(0, BLOCK), :] = zeros_bf
        dk_prev[pl.ds(S, BLOCK), :] = zeros_f
        dv_prev[pl.ds(S, BLOCK), :] = zeros_f

        def norm_step(i, carry):
            rows = pl.ds(pl.multiple_of(i * tn, tn), tn)
            qv = q_ref[rows, :]
            kv = k_ref[rows, :]
            qn[rows, :] = (qv * lax.rsqrt(_split_dot(qv * qv, bdv) * (1.0 / HEAD_DIM) + EPS)) * (qg_ref[...] * SCALE)
            kn[rows, :] = (kv * lax.rsqrt(_split_dot(kv * kv, bdv) * (1.0 / HEAD_DIM) + EPS)) * kg_ref[...]
            return carry
        lax.fori_loop(0, S // tn, norm_step, 0)

        def column(t):
            return jnp.concatenate([t[:, 0:1], t[:, HEAD_DIM:HEAD_DIM + 1]], axis=0)

        def block(blk, blocks_per_class):
            own = pl.ds(pl.multiple_of(blk * BLOCK, BLOCK), BLOCK)
            keys = pl.ds(pl.multiple_of(blk * BLOCK, BLOCK), 2 * BLOCK)
            q2 = _two_heads(qc[own, :], lo)
            do2 = _two_heads(doc[own, :], lo)
            lse2 = column(lsec[own, :])
            dd2 = column(ddc[own, :])
            kb = kc[keys, :]
            vb = vc[keys, :]
            s = jnp.where(band_mask, _nt(q2, kb), NEG) + _first_block_bias(blk, blocks_per_class)
            p = jnp.exp(s - lse2)
            ds = p * (_nt(do2, vb) - dd2)
            pb = p.astype(BF16)
            dsb = ds.astype(BF16)
            dv_band = _tn(pb, do2)
            dk_band = _tn(dsb, q2)
            dv_prev[own, :] = dv_band[:BLOCK]
            dv_own[own, :] = dv_band[BLOCK:]
            dk_prev[own, :] = dk_band[:BLOCK]
            dk_own[own, :] = dk_band[BLOCK:]
            dq2 = jnp.dot(dsb, kb, preferred_element_type=F32)
            dqc[own, :] = jnp.where(lo, dq2[:BLOCK], dq2[BLOCK:])

        to_bf = lambda t: t.astype(BF16)
        keep = lambda t: t
        for d in DILATIONS:
            L = S // d
            _fill_class_major(qc, qn, d, S, to_bf)
            _fill_class_major(kc, kn, d, S, to_bf, pad=BLOCK)
            _fill_class_major(vc, v_ref, d, S, to_bf, pad=BLOCK)
            _fill_class_major(doc, do_ref, d, S, to_bf)
            _fill_class_major(lsec, lse_ref, d, S, keep)
            _fill_class_major(ddc, dd_ref, d, S, keep)
            _for_blocks(S // BLOCK, 2, functools.partial(block, blocks_per_class=L // BLOCK))

            for r in range(d):
                def fold(n, carry, r=r, L=L, d=d):
                    cm = pl.ds(pl.multiple_of(r * L + n * BLOCK, BLOCK), BLOCK)
                    nxt = pl.ds(pl.multiple_of(r * L + (n + 1) * BLOCK, BLOCK), BLOCK)
                    dk = dk_own[cm, :] + dk_prev[nxt, :]
                    dv = dv_own[cm, :] + dv_prev[nxt, :]
                    if d == 1:
                        dqa[cm, :] = dqc[cm, :]
                        dka[cm, :] = dk
                        dva[cm, :] = dv
                    else:
                        nat = pl.ds(r + n * (BLOCK * d), BLOCK, stride=d)
                        dqa[nat, :] = dqa[nat, :] + dqc[cm, :]
                        dka[nat, :] = dka[nat, :] + dk
                        dva[nat, :] = dva[nat, :] + dv
                    return carry
                lax.fori_loop(0, L // BLOCK, fold, 0)

        def out_step(i, carry):
            dqg, dkg = carry
            rows = pl.ds(pl.multiple_of(i * tn, tn), tn)
            qv = q_ref[rows, :]
            kv = k_ref[rows, :]
            rq = lax.rsqrt(_split_dot(qv * qv, bdv) * (1.0 / HEAD_DIM) + EPS)
            rk = lax.rsqrt(_split_dot(kv * kv, bdv) * (1.0 / HEAD_DIM) + EPS)
            qh = qv * rq
            kh = kv * rk
            dqs = dqa[rows, :] * SCALE
            dkn = dka[rows, :]
            aq = dqs * qg_ref[...]
            ak = dkn * kg_ref[...]
            dq_ref[rows, :] = (rq * (aq - qh * (_split_dot(aq * qh, bdv) * (1.0 / HEAD_DIM)))).astype(BF16)
            dk_ref[rows, :] = (rk * (ak - kh * (_split_dot(ak * kh, bdv) * (1.0 / HEAD_DIM)))).astype(BF16)
            dv_ref[rows, :] = dva[rows, :].astype(BF16)
            dqg = dqg + jnp.sum(dqs * qh, axis=0, keepdims=True)
            dkg = dkg + jnp.sum(dkn * kh, axis=0, keepdims=True)
            return dqg, dkg
        zero = jnp.zeros((1, 128), F32)
        dqg, dkg = lax.fori_loop(0, S // tn, out_step, (zero, zero))
        dqg_ref[0] = dqg
        dkg_ref[0] = dkg

    once = pl.Buffered(1)
    col = lambda j0: pl.BlockSpec((S, 128), lambda p, j0=j0: (0, j0 + p), pipeline_mode=once)
    const = lambda shape: pl.BlockSpec(shape, lambda p: (0,) * len(shape))
    out = pl.BlockSpec((S, 128), lambda p: (0, p))
    gain_out = pl.BlockSpec((1, 1, 128), lambda p: (p, 0, 0))
    piece = jax.ShapeDtypeStruct((S, ATTN_WIDTH), BF16)
    gains = jax.ShapeDtypeStruct((npairs, 1, 128), F32)
    f32buf = pltpu.VMEM((S, 128), F32)
    f32pad = pltpu.VMEM((S + BLOCK, 128), F32)
    bf16buf = pltpu.VMEM((S, 128), BF16)
    bf16pad = pltpu.VMEM((S + BLOCK, 128), BF16)
    return pl.pallas_call(
        body, name="attn_bwd", grid=(npairs,),
        in_specs=[col(COL_AQ), col(COL_AK), col(COL_AV), col(0), col(0), col(0), const((1, 128)), const((1, 128)),
                  const((128, 128))],
        out_specs=[out, out, out, gain_out, gain_out],
        out_shape=[piece, piece, piece, gains, gains],
        scratch_shapes=[f32buf, f32buf, bf16buf, bf16pad, bf16pad, bf16buf, f32buf, f32buf,
                        f32buf, f32buf, f32buf, f32pad, f32pad, f32buf, f32buf, f32buf],
        compiler_params=_params(56, ("arbitrary",)),
    )(proj, proj, proj, do, dd, lse, qg2, kg2, bd)


def _mem_kv(mem, gain, wkv_bf, kg4, bd):
    def body(mem_ref, g_ref, w_ref, kg_ref, bd_ref, hm_ref, kraw_ref, mk_ref, mv_ref):
        mv_ = mem_ref[...]
        r = lax.rsqrt(jnp.mean(mv_ * mv_, axis=-1, keepdims=True) + EPS)
        hm = ((mv_ * r) * g_ref[...]).astype(BF16)
        hm_ref[...] = hm
        kv = jnp.dot(hm, w_ref[...], preferred_element_type=F32)
        kraw = kv[:, :MEM_WIDTH]
        kraw_ref[...] = kraw
        ms = _split_dot(kraw * kraw, bd_ref[...]) * (1.0 / HEAD_DIM)
        mk_ref[...] = (kraw * lax.rsqrt(ms + EPS)) * kg_ref[...]
        mv_ref[...] = kv[:, MEM_WIDTH:]

    sq = jax.ShapeDtypeStruct((MEM_LEN, MEM_WIDTH), F32)
    return pl.pallas_call(
        body, name="mem_kv",
        out_shape=[jax.ShapeDtypeStruct((MEM_LEN, D_MODEL), BF16), sq, sq, sq],
        compiler_params=_params(16),
    )(mem, gain, wkv_bf, kg4, bd)


def _mem_fwd(proj, mk, mv, qg4, bd):
    S = proj.shape[0]
    tm = 512

    def body(q_ref, g_ref, mk_ref, mv_ref, qg_ref, bd_ref, y_ref, om_ref):
        qv = q_ref[...]
        ms = _split_dot(qv * qv, bd_ref[...]) * (1.0 / HEAD_DIM)
        qs = (qv * lax.rsqrt(ms + EPS)) * (qg_ref[...] * SCALE)
        mkb = mk_ref[...].astype(BF16)
        mvb = mv_ref[...].astype(BF16)
        head = _head_index((tm, MEM_WIDTH))
        o = jnp.zeros((tm, MEM_WIDTH), F32)
        for h in range(4):
            s = _nt(jnp.where(head == h, qs, 0.0).astype(BF16), mkb)
            e = jnp.exp(s - jnp.max(s, axis=-1, keepdims=True))
            p = e * (1.0 / jnp.sum(e, axis=-1, keepdims=True))
            o = jnp.where(head == h, jnp.dot(p.astype(BF16), mvb, preferred_element_type=F32), o)
        om_ref[...] = o
        silu, _ = _silu_parts(g_ref[...])
        y_ref[...] = (o * silu).astype(BF16)

    col = lambda j: pl.BlockSpec((tm, MEM_WIDTH), lambda i, j=j: (i, j))
    const = lambda shape: pl.BlockSpec(shape, lambda i: (0,) * len(shape))
    tile = pl.BlockSpec((tm, MEM_WIDTH), lambda i: (i, 0))
    return pl.pallas_call(
        body, name="mem_fwd", grid=(S // tm,),
        in_specs=[col(11), col(12), const((MEM_LEN, MEM_WIDTH)), const((MEM_LEN, MEM_WIDTH)), const((1, MEM_WIDTH)),
                  const((MEM_WIDTH, MEM_WIDTH))],
        out_specs=[tile, tile],
        out_shape=[jax.ShapeDtypeStruct((S, MEM_WIDTH), BF16), jax.ShapeDtypeStruct((S, MEM_WIDTH), F32)],
        compiler_params=_params(24, ("arbitrary",)),
    )(proj, proj, mk, mv, qg4, bd)


def _mem_bwd(proj, dycat, om, mk, mv, qg4, bd):
    S = proj.shape[0]
    tm = 512

    def body(q_ref, g_ref, dy_ref, om_ref, mk_ref, mv_ref, qg_ref, bd_ref,
             dq_ref, dg_ref, dmk_ref, dmv_ref, dqg_ref):
        i = pl.program_id(0)

        @pl.when(i == 0)
        def _():
            dmk_ref[...] = jnp.zeros_like(dmk_ref)
            dmv_ref[...] = jnp.zeros_like(dmv_ref)
            dqg_ref[...] = jnp.zeros_like(dqg_ref)

        bdv = bd_ref[...]
        qv = q_ref[...]
        rq = lax.rsqrt(_split_dot(qv * qv, bdv) * (1.0 / HEAD_DIM) + EPS)
        qh = qv * rq
        qs = qh * (qg_ref[...] * SCALE)
        silu, dsilu = _silu_parts(g_ref[...])
        dy = dy_ref[...]
        o = om_ref[...]
        do = dy * silu
        dg_ref[...] = (dy * o * dsilu).astype(BF16)
        dd = _split_dot(do * o, bdv)
        mkb = mk_ref[...].astype(BF16)
        mvb = mv_ref[...].astype(BF16)
        head = _head_index((tm, MEM_WIDTH))
        dqs = jnp.zeros((tm, MEM_WIDTH), F32)
        for h in range(4):
            qhd = jnp.where(head == h, qs, 0.0).astype(BF16)
            doh = jnp.where(head == h, do, 0.0).astype(BF16)
            s = _nt(qhd, mkb)
            e = jnp.exp(s - jnp.max(s, axis=-1, keepdims=True))
            p = e * (1.0 / jnp.sum(e, axis=-1, keepdims=True))
            ds = p * (_nt(doh, mvb) - dd[:, h * HEAD_DIM:h * HEAD_DIM + 1])
            dsb = ds.astype(BF16)
            dmv_ref[...] += _tn(p.astype(BF16), doh)
            dmk_ref[...] += _tn(dsb, qhd)
            dqs = jnp.where(head == h, jnp.dot(dsb, mkb, preferred_element_type=F32), dqs)
        dqs = dqs * SCALE
        a = dqs * qg_ref[...]
        dq_ref[...] = (rq * (a - qh * (_split_dot(a * qh, bdv) * (1.0 / HEAD_DIM)))).astype(BF16)
        dqg_ref[...] += jnp.sum(dqs * qh, axis=0, keepdims=True)

    col = lambda j: pl.BlockSpec((tm, MEM_WIDTH), lambda i, j=j: (i, j))
    const = lambda shape: pl.BlockSpec(shape, lambda i: (0,) * len(shape))
    tile = pl.BlockSpec((tm, MEM_WIDTH), lambda i: (i, 0))
    piece = jax.ShapeDtypeStruct((S, MEM_WIDTH), BF16)
    sq = jax.ShapeDtypeStruct((MEM_LEN, MEM_WIDTH), F32)
    return pl.pallas_call(
        body, name="mem_bwd", grid=(S // tm,),
        in_specs=[col(11), col(12), col(3), tile, const((MEM_LEN, MEM_WIDTH)), const((MEM_LEN, MEM_WIDTH)),
                  const((1, MEM_WIDTH)), const((MEM_WIDTH, MEM_WIDTH))],
        out_specs=[tile, tile, const((MEM_LEN, MEM_WIDTH)), const((MEM_LEN, MEM_WIDTH)), const((1, MEM_WIDTH))],
        out_shape=[piece, piece, sq, sq, jax.ShapeDtypeStruct((1, MEM_WIDTH), F32)],
        compiler_params=_params(32, ("arbitrary",)),
    )(proj, proj, dycat, om, mk, mv, qg4, bd)


def _mem_kv_bwd(dmk, dmv, kraw, mem, gain, kg4, wkv_bf, hm_bf, bd):
    def body(dmk_ref, dmv_ref, kraw_ref, mem_ref, g_ref, kg_ref, w_ref, hm_ref, bd_ref, dw_ref, dg_ref, dkg_ref):
        bdv = bd_ref[...]
        kraw = kraw_ref[...]
        rk = lax.rsqrt(_split_dot(kraw * kraw, bdv) * (1.0 / HEAD_DIM) + EPS)
        kh = kraw * rk
        dmkv = dmk_ref[...]
        a = dmkv * kg_ref[...]
        dkraw = rk * (a - kh * (_split_dot(a * kh, bdv) * (1.0 / HEAD_DIM)))
        dkg_ref[...] = jnp.sum(dmkv * kh, axis=0, keepdims=True)
        dkv = jnp.concatenate([dkraw, dmv_ref[...]], axis=1).astype(BF16)
        dw_ref[...] = _tn(hm_ref[...], dkv)
        dhm = _nt(dkv, w_ref[...])
        mv_ = mem_ref[...]
        r = lax.rsqrt(jnp.mean(mv_ * mv_, axis=-1, keepdims=True) + EPS)
        dg_ref[...] = jnp.sum(dhm * (mv_ * r), axis=0, keepdims=True)

    return pl.pallas_call(
        body, name="mem_kv_bwd",
        out_shape=[jax.ShapeDtypeStruct((D_MODEL, 2 * MEM_WIDTH), F32), jax.ShapeDtypeStruct((1, D_MODEL), F32),
                   jax.ShapeDtypeStruct((1, MEM_WIDTH), F32)],
        compiler_params=_params(24),
    )(dmk, dmv, kraw, mem, gain, kg4, wkv_bf, hm_bf, bd)


def _out_loss(yg, ya, ym, x, tgt, wout_bf):
    S, D = x.shape
    tm = 256

    def body(yg_ref, ya_ref, ym_ref, x_ref, t_ref, w_ref, dout_ref, dycat_ref, dw_ref, loss_ref):
        @pl.when(pl.program_id(0) == 0)
        def _():
            dw_ref[...] = jnp.zeros_like(dw_ref)
            loss_ref[...] = jnp.zeros_like(loss_ref)

        ycat = jnp.concatenate([yg_ref[...], ya_ref[...], ym_ref[...]], axis=1)
        w = w_ref[...]
        diff = (x_ref[...] + jnp.dot(ycat, w, preferred_element_type=F32)) - t_ref[...]
        loss_ref[...] += jnp.sum(diff * diff, axis=0, keepdims=True)
        dout = diff * (1.0 / D)
        dout_ref[...] = dout
        db = dout.astype(BF16)
        dycat_ref[...] = _nt(db, w)
        dw_ref[...] += _tn(ycat, db)

    tile = lambda w: pl.BlockSpec((tm, w), lambda i: (i, 0))
    const = lambda shape: pl.BlockSpec(shape, lambda i: (0,) * len(shape))
    return pl.pallas_call(
        body, name="out_loss", grid=(S // tm,),
        in_specs=[tile(GMLP_WIDTH), tile(ATTN_WIDTH), tile(MEM_WIDTH), tile(D), tile(D), const((D, D))],
        out_specs=[tile(D), tile(D), const((D, D)), const((1, D))],
        out_shape=[jax.ShapeDtypeStruct((S, D), F32), jax.ShapeDtypeStruct((S, D), F32),
                   jax.ShapeDtypeStruct((D, D), F32), jax.ShapeDtypeStruct((1, D), F32)],
        compiler_params=_params(40, ("arbitrary",)),
    )(yg, ya, ym, x, tgt, wout_bf)


def _piece_specs(pieces, tm):
    return [pl.BlockSpec((tm, p.shape[1]), lambda i: (i, 0)) for p in pieces]


def _in_bwd_dx(pieces, x, dout, gain, win_bf):
    S, D = x.shape
    N = win_bf.shape[1]
    tm = 256
    n = len(pieces)

    def body(*refs):
        piece_refs = refs[:n]
        x_ref, dout_ref, g_ref, w_ref, gx_ref, dg_ref = refs[n:]

        @pl.when(pl.program_id(0) == 0)
        def _():
            dg_ref[...] = jnp.zeros_like(dg_ref)

        dproj = jnp.concatenate([r[...] for r in piece_refs], axis=1)
        dh = _nt(dproj, w_ref[...])
        xv = x_ref[...]
        r = lax.rsqrt(jnp.mean(xv * xv, axis=-1, keepdims=True) + EPS)
        xh = xv * r
        a = dh * g_ref[...]
        gx_ref[...] = dout_ref[...] + r * (a - xh * jnp.mean(a * xh, axis=-1, keepdims=True))
        dg_ref[...] += jnp.sum(dh * xh, axis=0, keepdims=True)

    tile = pl.BlockSpec((tm, D), lambda i: (i, 0))
    const = lambda shape: pl.BlockSpec(shape, lambda i: (0,) * len(shape))
    return pl.pallas_call(
        body, name="in_bwd_dx", grid=(S // tm,),
        in_specs=_piece_specs(pieces, tm) + [tile, tile, const((1, D)), const((D, N))],
        out_specs=[tile, const((1, D))],
        out_shape=[jax.ShapeDtypeStruct((S, D), F32), jax.ShapeDtypeStruct((1, D), F32)],
        compiler_params=_params(40, ("arbitrary",)),
    )(*pieces, x, dout, gain, win_bf)


def _in_bwd_dw(pieces, h_bf):
    S, D = h_bf.shape
    N = sum(p.shape[1] for p in pieces)
    tm = 256
    n = len(pieces)

    def body(*refs):
        piece_refs = refs[:n]
        h_ref, dw_ref = refs[n:]

        @pl.when(pl.program_id(0) == 0)
        def _():
            dw_ref[...] = jnp.zeros_like(dw_ref)

        dproj = jnp.concatenate([r[...] for r in piece_refs], axis=1)
        dw_ref[...] += _tn(h_ref[...], dproj)

    return pl.pallas_call(
        body, name="in_bwd_dw", grid=(S // tm,),
        in_specs=_piece_specs(pieces, tm) + [pl.BlockSpec((tm, D), lambda i: (i, 0))],
        out_specs=pl.BlockSpec((D, N), lambda i: (0, 0)),
        out_shape=jax.ShapeDtypeStruct((D, N), F32),
        compiler_params=_params(48, ("arbitrary",)),
    )(*pieces, h_bf)


def _place():
    x, y, c = lax.axis_index("x"), lax.axis_index("y"), lax.axis_index("c")
    chips = [(1 - x, y), (x, 1 - y), (1 - x, 1 - y)]
    return x, y, c, chips


def _all_gather(arrs, name):
    n = len(arrs)

    def body(*refs):
        ins, outs = refs[:n], refs[n:2 * n]
        send_sems, recv_sems, local_sems = refs[2 * n:]
        x, y, c, chips = _place()
        me, sibling = (x, y, c), (x, y, 1 - c)

        def rows(a, px, py, pc):
            m = ins[a].shape[0]
            return outs[a].at[pl.ds((4 * px + 2 * py + pc) * m, m), :]

        def copy(a, k, block, to, src=None):
            return pltpu.make_async_remote_copy(
                src_ref=rows(a, *block) if src is None else src, dst_ref=rows(a, *block),
                send_sem=send_sems.at[a, k], recv_sem=recv_sems.at[a, k], device_id=to, device_id_type=MESH)

        mine = [pltpu.make_async_copy(ins[a], rows(a, *me), local_sems.at[a]) for a in range(n)]
        for cp in mine:
            cp.start()
        first = []
        for a in range(n):
            first.append(copy(a, 0, me, sibling, src=ins[a]))
            first += [copy(a, 1 + j, me, (*chip, c), src=ins[a]) for j, chip in enumerate(chips)]
        for cp in first:
            cp.start()
        passed = []
        for j, chip in enumerate(chips):
            for a in range(n):
                copy(a, 1 + j, (*chip, c), me).wait_recv()
                fwd = copy(a, 4 + j, (*chip, c), sibling)
                fwd.start()
                passed.append(fwd)
        for a in range(n):
            copy(a, 0, sibling, me).wait_recv()
            for j, chip in enumerate(chips):
                copy(a, 4 + j, (*chip, 1 - c), me).wait_recv()
        for cp in first + passed:
            cp.wait_send()
        for cp in mine:
            cp.wait()

    vmem = pl.BlockSpec(memory_space=pltpu.VMEM)
    return pl.pallas_call(
        body, name=name,
        out_shape=[jax.ShapeDtypeStruct((N_DEV * a.shape[0], a.shape[1]), a.dtype) for a in arrs],
        in_specs=[vmem] * n, out_specs=[vmem] * n,
        scratch_shapes=[pltpu.SemaphoreType.DMA((n, 7)), pltpu.SemaphoreType.DMA((n, 7)), pltpu.SemaphoreType.DMA((n,))],
        compiler_params=_params(40),
    )(*arrs)


def _reduce_scatter(arrs, name):
    n = len(arrs)
    tr = 128

    def body(*refs):
        ins, outs = refs[:n], refs[n:2 * n]
        half, quarter = refs[2 * n:3 * n], refs[3 * n:4 * n]
        send_sems, recv_sems = refs[4 * n:]
        x, y, c, chips = _place()
        sibling = (x, y, 1 - c)

        to_sibling = [pltpu.make_async_remote_copy(
            src_ref=ins[a].at[2 * q + (1 - c)], dst_ref=half[a].at[q], send_sem=send_sems.at[a, q],
            recv_sem=recv_sems.at[a, q], device_id=sibling, device_id_type=MESH) for a in range(n) for q in range(4)]
        for cp in to_sibling:
            cp.start()
        for cp in to_sibling:
            cp.wait_recv()

        def add_rows(a, fn):
            m = ins[a].shape[1]
            def step(i, carry):
                fn(pl.ds(pl.multiple_of(i * tr, tr), tr))
                return carry
            lax.fori_loop(0, m // tr, step, 0)

        for a in range(n):
            for q in range(4):
                def add_half(rows, a=a, q=q):
                    both = ins[a][2 * q + c, rows, :].astype(F32) + half[a][q, rows, :].astype(F32)
                    half[a][q, rows, :] = both.astype(BF16)
                add_rows(a, add_half)

        to_chips = [pltpu.make_async_remote_copy(
            src_ref=half[a].at[2 * chip[0] + chip[1]], dst_ref=quarter[a].at[k], send_sem=send_sems.at[a, 4 + k],
            recv_sem=recv_sems.at[a, 4 + k], device_id=(*chip, c), device_id_type=MESH)
            for a in range(n) for k, chip in enumerate(chips)]
        for cp in to_chips:
            cp.start()
        for cp in to_chips:
            cp.wait_recv()
        for a in range(n):
            def add_quarters(rows, a=a):
                f = lambda t: t.astype(F32)
                outs[a][rows, :] = ((f(half[a][2 * x + y, rows, :]) + f(quarter[a][0, rows, :]))
                                    + (f(quarter[a][1, rows, :]) + f(quarter[a][2, rows, :])))
            add_rows(a, add_quarters)
        for cp in to_sibling + to_chips:
            cp.wait_send()

    vmem = pl.BlockSpec(memory_space=pltpu.VMEM)
    return pl.pallas_call(
        body, name=name,
        out_shape=[jax.ShapeDtypeStruct(a.shape[1:], F32) for a in arrs],
        in_specs=[vmem] * n, out_specs=[vmem] * n,
        scratch_shapes=[pltpu.VMEM((4,) + a.shape[1:], BF16) for a in arrs]
        + [pltpu.VMEM((3,) + a.shape[1:], BF16) for a in arrs]
        + [pltpu.SemaphoreType.DMA((n, 7)), pltpu.SemaphoreType.DMA((n, 7))],
        compiler_params=_params(48),
    )(*arrs)


def _adamw_math(w, g, m, v):
    m = ADAM_B1 * m + (1.0 - ADAM_B1) * g
    v = ADAM_B2 * v + (1.0 - ADAM_B2) * (g * g)
    m_hat = m / (1.0 - ADAM_B1 ** ADAM_STEP)
    v_hat = v / (1.0 - ADAM_B2 ** ADAM_STEP)
    delta = -ADAM_LR * (m_hat / (jnp.sqrt(v_hat) + ADAM_EPS) + ADAM_WD * w)
    return delta, m, v


def _adamw(w, g, m, v, name):
    R, C = w.shape
    tr = 128 if R % 128 == 0 else R

    def body(w_ref, g_ref, m_ref, v_ref, d_ref, nm_ref, nv_ref):
        d_ref[...], nm_ref[...], nv_ref[...] = _adamw_math(w_ref[...], g_ref[...], m_ref[...], v_ref[...])

    tile = pl.BlockSpec((tr, C), lambda i: (i, 0))
    out = jax.ShapeDtypeStruct((R, C), F32)
    return pl.pallas_call(
        body, name=name, grid=(R // tr,), in_specs=[tile] * 4, out_specs=[tile] * 3, out_shape=[out] * 3,
        compiler_params=_params(16, ("arbitrary",)),
    )(w, g, m, v)


def _adamw_summed(w, g_all, m, v, name):
    R, C = w.shape

    def body(w_ref, g_ref, m_ref, v_ref, gs_ref, d_ref, nm_ref, nv_ref):
        g = g_ref[0:R, :]
        for k in range(1, N_DEV):
            g = g + g_ref[k * R:(k + 1) * R, :]
        gs_ref[...] = g
        d_ref[...], nm_ref[...], nv_ref[...] = _adamw_math(w_ref[...], g, m_ref[...], v_ref[...])

    out = jax.ShapeDtypeStruct((R, C), F32)
    return pl.pallas_call(body, name=name, out_shape=[out] * 4, compiler_params=_params(16))(w, g_all, m, v)


SMALL = ("norm_gain", "gmlp_v_gain", "gmlp_w_s", "gmlp_b", "attn_q_gain", "attn_k_gain", "mem_norm_gain",
         "mem_q_gain", "mem_k_gain")
WEIGHTS = ("norm_gain", "w_in", "gmlp_v_gain", "gmlp_w_s", "gmlp_b", "attn_q_gain", "attn_k_gain",
           "mem_norm_gain", "w_mem_kv", "mem_q_gain", "mem_k_gain", "w_out")


def _pack(tree):
    return jnp.concatenate([tree[k].reshape(-1) for k in SMALL]).reshape(-1, 128)


def _unpack(packed, like):
    flat = packed.reshape(-1)
    out, at = {}, 0
    for k in SMALL:
        out[k] = flat[at:at + like[k].size].reshape(like[k].shape)
        at += like[k].size
    return out


def _local_grads(x, mem, tgt, w, win_bf, wkv_bf, wout_bf):
    bd128, bd256 = _head_blockdiag(128), _head_blockdiag(256)
    gain = w["norm_gain"].reshape(1, D_MODEL)
    vg = w["gmlp_v_gain"].reshape(1, GMLP_WIDTH)
    w_s = w["gmlp_w_s"].reshape(4, CHUNK, CHUNK)
    b2 = jnp.repeat(w["gmlp_b"].reshape(4, CHUNK).T, HEAD_DIM, axis=1)
    qg2 = jnp.tile(w["attn_q_gain"].reshape(1, HEAD_DIM), (1, 2))
    kg2 = jnp.tile(w["attn_k_gain"].reshape(1, HEAD_DIM), (1, 2))
    mqg4 = jnp.tile(w["mem_q_gain"].reshape(1, HEAD_DIM), (1, 4))
    mkg4 = jnp.tile(w["mem_k_gain"].reshape(1, HEAD_DIM), (1, 4))
    mgain = w["mem_norm_gain"].reshape(1, D_MODEL)

    proj, h_bf = _rms_proj(x, gain, win_bf)
    yg = _gmlp_fwd(proj, vg, w_s, b2, bd256)
    ya, att, lse = _attn_fwd(proj, qg2, kg2, bd128)
    hm_bf, kraw, mk, mv = _mem_kv(mem, mgain, wkv_bf, mkg4, bd256)
    ym, om = _mem_fwd(proj, mk, mv, mqg4, bd256)
    dout, dycat, dwout, sq = _out_loss(yg, ya, ym, x, tgt, wout_bf)

    du, dgv, dgg, dws, db2, dvg = _gmlp_bwd(proj, dycat, vg, w_s, b2, bd256)
    do, dd, dag = _attn_bwd_prep(proj, dycat, att, bd256)
    dq, dk, dv, dqg, dkg = _attn_bwd(proj, do, dd, lse, qg2, kg2, bd128)
    dmq, dmg, dmk, dmv, dmqg = _mem_bwd(proj, dycat, om, mk, mv, mqg4, bd256)
    dwkv, dmgain, dmkg = _mem_kv_bwd(dmk, dmv, kraw, mem, mgain, mkg4, wkv_bf, hm_bf, bd256)
    pieces = [du, dgv, dgg, dq, dk, dv, dag, dmq, dmg]
    grad_x, dgain = _in_bwd_dx(pieces, x, dout, gain, win_bf)
    dwin = _in_bwd_dw(pieces, h_bf)

    small = {
        "norm_gain": dgain,
        "gmlp_v_gain": dvg,
        "gmlp_w_s": dws,
        "gmlp_b": db2[:, ::HEAD_DIM].T,
        "attn_q_gain": dqg.reshape(-1, HEAD_DIM).sum(axis=0),
        "attn_k_gain": dkg.reshape(-1, HEAD_DIM).sum(axis=0),
        "mem_norm_gain": dmgain,
        "mem_q_gain": dmqg.reshape(-1, HEAD_DIM).sum(axis=0),
        "mem_k_gain": dmkg.reshape(-1, HEAD_DIM).sum(axis=0),
    }
    return sq, grad_x, dwin, dwkv, dwout, small


def kernel(x, mem, norm_gain, w_in, gmlp_v_gain, gmlp_w_s, gmlp_b, attn_q_gain, attn_k_gain, mem_norm_gain, w_mem_kv, mem_q_gain, mem_k_gain, w_out, loss_target, m_norm_gain, m_w_in, m_gmlp_v_gain, m_gmlp_w_s, m_gmlp_b, m_attn_q_gain, m_attn_k_gain, m_mem_norm_gain, m_w_mem_kv, m_mem_q_gain, m_mem_k_gain, m_w_out, v_norm_gain, v_w_in, v_gmlp_v_gain, v_gmlp_w_s, v_gmlp_b, v_attn_q_gain, v_attn_k_gain, v_mem_norm_gain, v_w_mem_kv, v_mem_q_gain, v_mem_k_gain, v_w_out):
    w = dict(norm_gain=norm_gain, w_in=w_in, gmlp_v_gain=gmlp_v_gain, gmlp_w_s=gmlp_w_s, gmlp_b=gmlp_b,
             attn_q_gain=attn_q_gain, attn_k_gain=attn_k_gain, mem_norm_gain=mem_norm_gain, w_mem_kv=w_mem_kv,
             mem_q_gain=mem_q_gain, mem_k_gain=mem_k_gain, w_out=w_out)
    m = dict(norm_gain=m_norm_gain, w_in=m_w_in, gmlp_v_gain=m_gmlp_v_gain, gmlp_w_s=m_gmlp_w_s, gmlp_b=m_gmlp_b,
             attn_q_gain=m_attn_q_gain, attn_k_gain=m_attn_k_gain, mem_norm_gain=m_mem_norm_gain,
             w_mem_kv=m_w_mem_kv, mem_q_gain=m_mem_q_gain, mem_k_gain=m_mem_k_gain, w_out=m_w_out)
    v = dict(norm_gain=v_norm_gain, w_in=v_w_in, gmlp_v_gain=v_gmlp_v_gain, gmlp_w_s=v_gmlp_w_s, gmlp_b=v_gmlp_b,
             attn_q_gain=v_attn_q_gain, attn_k_gain=v_attn_k_gain, mem_norm_gain=v_mem_norm_gain,
             w_mem_kv=v_w_mem_kv, mem_q_gain=v_mem_q_gain, mem_k_gain=v_mem_k_gain, w_out=v_w_out)
    S = x.shape[1]
    n_in = w_in.shape[2]
    win_sh, wkv_sh, wout_sh = w_in[0], w_mem_kv[0], w_out[0]

    win_all, wkv_bf, wout_bf = _all_gather(
        [win_sh.astype(BF16), wkv_sh.astype(BF16), wout_sh.astype(BF16)], "gather_weights")
    win_bf = win_all.reshape(N_DEV, D_MODEL, n_in).transpose(1, 0, 2).reshape(D_MODEL, N_DEV * n_in)

    sq, grad_x, dwin, dwkv, dwout, small = _local_grads(x[0], mem[0], loss_target[0], w, win_bf, wkv_bf, wout_bf)
    loss = lax.psum(jnp.sum(sq) * (0.5 / D_MODEL), ("x", "y", "c"))

    dwin_blocks = dwin.reshape(D_MODEL, N_DEV, n_in).transpose(1, 0, 2).astype(BF16)
    (g_win,) = _reduce_scatter([dwin_blocks], "reduce_w_in")
    g_wkv, g_wout = _reduce_scatter(
        [dwkv.reshape(N_DEV, D_MODEL // N_DEV, -1).astype(BF16),
         dwout.reshape(N_DEV, D_MODEL // N_DEV, -1).astype(BF16)], "reduce_w_kv_out")
    (small_all,) = _all_gather([_pack(small)], "gather_small_grads")

    out_g, out_d, out_m, out_v = {}, {}, {}, {}
    for name, g in (("w_in", g_win), ("w_mem_kv", g_wkv), ("w_out", g_wout)):
        d_, m_, v_ = _adamw(w[name][0], g, m[name][0], v[name][0], "adamw_" + name)
        out_g[name], out_d[name], out_m[name], out_v[name] = g[None], d_[None], m_[None], v_[None]
    gs, ds, ms, vs = _adamw_summed(_pack(w), small_all, _pack(m), _pack(v), "adamw_small")
    for tree, packed in ((out_g, gs), (out_d, ds), (out_m, ms), (out_v, vs)):
        tree.update(_unpack(packed, w))

    return (loss, grad_x[None], *[out_g[k] for k in WEIGHTS], *[out_d[k] for k in WEIGHTS],
            *[out_m[k] for k in WEIGHTS], *[out_v[k] for k in WEIGHTS])
```

```python
import functools
import math

import jax
import jax.numpy as jnp
from jax import lax
from jax.experimental import pallas as pl
from jax.experimental.pallas import tpu as pltpu

F32 = jnp.float32
BF16 = jnp.bfloat16

N_DEV = 8
D_MODEL = 1024
HEAD_DIM = 64
GMLP_WIDTH = 256
ATTN_WIDTH = 512
MEM_WIDTH = 256
MEM_LEN = 256
IN_WIDTH = 3 * GMLP_WIDTH + 4 * ATTN_WIDTH + 2 * MEM_WIDTH
CHUNK = 128
BLOCK = 128
DILATIONS = (1, 4, 16)
EPS = 1e-6
SCALE = 1.0 / math.sqrt(HEAD_DIM)
NEG = -1e30

ADAM_LR = 0.001
ADAM_B1 = 0.9
ADAM_B2 = 0.999
ADAM_EPS = 1e-08
ADAM_WD = 0.01
ADAM_STEP = 10

MIB = 1024 * 1024
MESH = pl.DeviceIdType.MESH

COL_AQ, COL_AK, COL_AV, COL_AG = 6, 10, 14, 18


def _params(vmem_mib, semantics=None):
    kw = dict(vmem_limit_bytes=vmem_mib * MIB)
    if semantics is not None:
        kw["dimension_semantics"] = semantics
    return pltpu.CompilerParams(**kw)


def _split_dot(x, sel_bf):
    hi = x.astype(BF16)
    lo = (x - hi.astype(F32)).astype(BF16)
    return jnp.dot(hi, sel_bf, preferred_element_type=F32) + jnp.dot(lo, sel_bf, preferred_element_type=F32)


def _nt(a, b):
    return lax.dot_general(a, b, (((1,), (1,)), ((), ())), preferred_element_type=F32)


def _tn(a, b):
    return lax.dot_general(a, b, (((0,), (0,)), ((), ())), preferred_element_type=F32)


def _silu_parts(g):
    sg = jax.nn.sigmoid(g)
    return g * sg, sg * (1.0 + g * (1.0 - sg))


def _head_index(shape):
    return lax.shift_right_logical(lax.broadcasted_iota(jnp.int32, shape, 1), HEAD_DIM.bit_length() - 1)


def _head_blockdiag(width):
    i = jnp.arange(width) // HEAD_DIM
    return (i[:, None] == i[None, :]).astype(BF16)


def _rms_proj(x, gain, w_bf):
    S, D = x.shape
    N = w_bf.shape[1]
    tm = 256

    def body(x_ref, g_ref, w_ref, proj_ref, h_ref):
        xv = x_ref[...]
        r = lax.rsqrt(jnp.mean(xv * xv, axis=-1, keepdims=True) + EPS)
        h = ((xv * r) * g_ref[...]).astype(BF16)
        h_ref[...] = h
        proj_ref[...] = jnp.dot(h, w_ref[...], preferred_element_type=F32)

    return pl.pallas_call(
        body, name="rms_proj", grid=(S // tm,),
        in_specs=[pl.BlockSpec((tm, D), lambda i: (i, 0)), pl.BlockSpec((1, D), lambda i: (0, 0)),
                  pl.BlockSpec((D, N), lambda i: (0, 0))],
        out_specs=[pl.BlockSpec((tm, N), lambda i: (i, 0)), pl.BlockSpec((tm, D), lambda i: (i, 0))],
        out_shape=[jax.ShapeDtypeStruct((S, N), F32), jax.ShapeDtypeStruct((S, D), BF16)],
        compiler_params=_params(40, ("arbitrary",)),
    )(x, gain, w_bf)


def _gmlp_masked_weights(ws_ref, transpose):
    t = lax.broadcasted_iota(jnp.int32, (CHUNK, CHUNK), 0)
    s = lax.broadcasted_iota(jnp.int32, (CHUNK, CHUNK), 1)
    parts = []
    for h in range(4):
        wm = jnp.where(s <= t, ws_ref[h], 0.0)
        parts.append(wm.T if transpose else wm)
    return jnp.concatenate(parts, axis=1).astype(BF16)


def _head_stack(v, head):
    return jnp.concatenate([jnp.where(head == h, v, 0.0) for h in range(4)], axis=0).astype(BF16)


def _gmlp_fwd(proj, vg, w_s, b2, bd):
    S = proj.shape[0]
    tm = 512

    def body(u_ref, v_ref, g_ref, vg_ref, ws_ref, b2_ref, bd_ref, y_ref):
        v = v_ref[...]
        ms = _split_dot(v * v, bd_ref[...]) * (1.0 / HEAD_DIM)
        vn = (v * lax.rsqrt(ms + EPS)) * vg_ref[...]
        wcat = _gmlp_masked_weights(ws_ref, False)
        head = _head_index((CHUNK, GMLP_WIDTH))
        for c in range(tm // CHUNK):
            rows = slice(c * CHUNK, (c + 1) * CHUNK)
            sp = jnp.dot(wcat, _head_stack(vn[rows], head), preferred_element_type=F32) + b2_ref[...]
            silu, _ = _silu_parts(g_ref[rows, :])
            y_ref[rows, :] = ((u_ref[rows, :] * sp) * silu).astype(BF16)

    col = lambda j: pl.BlockSpec((tm, GMLP_WIDTH), lambda i, j=j: (i, j))
    const = lambda shape: pl.BlockSpec(shape, lambda i: (0,) * len(shape))
    return pl.pallas_call(
        body, name="gmlp_fwd", grid=(S // tm,),
        in_specs=[col(0), col(1), col(2), const((1, GMLP_WIDTH)), const((4, CHUNK, CHUNK)),
                  const((CHUNK, GMLP_WIDTH)), const((GMLP_WIDTH, GMLP_WIDTH))],
        out_specs=pl.BlockSpec((tm, GMLP_WIDTH), lambda i: (i, 0)),
        out_shape=jax.ShapeDtypeStruct((S, GMLP_WIDTH), BF16),
        compiler_params=_params(24, ("arbitrary",)),
    )(proj, proj, proj, vg, w_s, b2, bd)


def _gmlp_bwd(proj, dycat, vg, w_s, b2, bd):
    S = proj.shape[0]
    tm = 512
    nsteps = S // tm

    def body(u_ref, v_ref, g_ref, dy_ref, vg_ref, ws_ref, b2_ref, bd_ref,
             du_ref, dv_ref, dg_ref, dws_ref, db2_ref, dvg_ref):
        i = pl.program_id(0)

        @pl.when(i == 0)
        def _():
            dws_ref[...] = jnp.zeros_like(dws_ref)
            db2_ref[...] = jnp.zeros_like(db2_ref)
            dvg_ref[...] = jnp.zeros_like(dvg_ref)

        bdv = bd_ref[...]
        v = v_ref[...]
        ms = _split_dot(v * v, bdv) * (1.0 / HEAD_DIM)
        rv = lax.rsqrt(ms + EPS)
        xhat = v * rv
        vgv = vg_ref[...]
        vn = xhat * vgv
        wcat = _gmlp_masked_weights(ws_ref, False)
        wcat_t = _gmlp_masked_weights(ws_ref, True)
        head = _head_index((CHUNK, GMLP_WIDTH))
        dvg = jnp.zeros((1, GMLP_WIDTH), F32)
        for c in range(tm // CHUNK):
            rows = slice(c * CHUNK, (c + 1) * CHUNK)
            vn_c = vn[rows]
            spb = jnp.dot(wcat, _head_stack(vn_c, head), preferred_element_type=F32) + b2_ref[...]
            silu, dsilu = _silu_parts(g_ref[rows, :])
            dy = dy_ref[rows, :]
            u = u_ref[rows, :]
            du_ref[rows, :] = (dy * spb * silu).astype(BF16)
            dg_ref[rows, :] = (dy * u * spb * dsilu).astype(BF16)
            dsp = dy * u * silu
            db2_ref[...] += dsp
            dstack = _head_stack(dsp, head)
            dvn = jnp.dot(wcat_t, dstack, preferred_element_type=F32)
            dws_ref[...] += _nt(dstack, vn_c.astype(BF16))
            xh = xhat[rows]
            a = dvn * vgv
            mean_ax = _split_dot(a * xh, bdv) * (1.0 / HEAD_DIM)
            dv_ref[rows, :] = (rv[rows] * (a - xh * mean_ax)).astype(BF16)
            dvg = dvg + jnp.sum(dvn * xh, axis=0, keepdims=True)
        dvg_ref[...] += dvg

        @pl.when(i == nsteps - 1)
        def _():
            t = lax.broadcasted_iota(jnp.int32, (4 * CHUNK, CHUNK), 0) % CHUNK
            s = lax.broadcasted_iota(jnp.int32, (4 * CHUNK, CHUNK), 1)
            dws_ref[...] = jnp.where(s <= t, dws_ref[...], 0.0)
            db2_ref[...] = _split_dot(db2_ref[...], bdv)

    col = lambda j: pl.BlockSpec((tm, GMLP_WIDTH), lambda i, j=j: (i, j))
    const = lambda shape: pl.BlockSpec(shape, lambda i: (0,) * len(shape))
    tile = pl.BlockSpec((tm, GMLP_WIDTH), lambda i: (i, 0))
    piece = jax.ShapeDtypeStruct((S, GMLP_WIDTH), BF16)
    return pl.pallas_call(
        body, name="gmlp_bwd", grid=(nsteps,),
        in_specs=[col(0), col(1), col(2), col(0), const((1, GMLP_WIDTH)), const((4, CHUNK, CHUNK)),
                  const((CHUNK, GMLP_WIDTH)), const((GMLP_WIDTH, GMLP_WIDTH))],
        out_specs=[tile, tile, tile, const((4 * CHUNK, CHUNK)), const((CHUNK, GMLP_WIDTH)), const((1, GMLP_WIDTH))],
        out_shape=[piece, piece, piece, jax.ShapeDtypeStruct((4 * CHUNK, CHUNK), F32),
                   jax.ShapeDtypeStruct((CHUNK, GMLP_WIDTH), F32), jax.ShapeDtypeStruct((1, GMLP_WIDTH), F32)],
        compiler_params=_params(32, ("arbitrary",)),
    )(proj, proj, proj, dycat, vg, w_s, b2, bd)


def _band_mask():
    qi = lax.broadcasted_iota(jnp.int32, (2 * BLOCK, 2 * BLOCK), 0) % BLOCK
    ki = lax.broadcasted_iota(jnp.int32, (2 * BLOCK, 2 * BLOCK), 1)
    return ((ki < BLOCK) & (ki >= qi)) | ((ki >= BLOCK) & ((ki - BLOCK) <= qi))


def _first_block_bias(blk, blocks_per_class):
    kcol = lax.broadcasted_iota(jnp.int32, (1, 2 * BLOCK), 1)
    kill = jnp.where((blk & (blocks_per_class - 1)) == 0, NEG, 0.0)
    return jnp.where(kcol < BLOCK, kill, 0.0)


def _two_heads(q, lo):
    zero = jnp.zeros_like(q)
    return jnp.concatenate([jnp.where(lo, q, zero), jnp.where(lo, zero, q)], axis=0)


def _block_tokens(blk, d, S):
    if d == 1:
        return pl.ds(pl.multiple_of(blk * BLOCK, BLOCK), BLOCK)
    blocks_per_class = S // d // BLOCK
    r = lax.shift_right_logical(blk, blocks_per_class.bit_length() - 1)
    n = blk & (blocks_per_class - 1)
    return pl.ds(r + n * (BLOCK * d), BLOCK, stride=d)


def _padded_block(blk):
    return pl.ds(pl.multiple_of((blk + 1) * BLOCK, BLOCK), BLOCK)


def _for_blocks(n_blocks, unroll, fn):
    def group(g, carry):
        for u in range(unroll):
            fn(g * unroll + u)
        return carry
    lax.fori_loop(0, n_blocks // unroll, group, 0)


def _attn_fwd(proj, qg2, kg2, bd):
    S = proj.shape[0]
    npairs = ATTN_WIDTH // 128
    tn = 512

    def body(q_ref, k_ref, v_ref, g_ref, qg_ref, kg_ref, bd_ref, y_ref, att_ref, lse_ref, qn, kn, kc, vc):
        bdv = bd_ref[...]
        lo = lax.broadcasted_iota(jnp.int32, (BLOCK, 128), 1) < HEAD_DIM
        band_mask = _band_mask()
        kc[pl.ds(0, BLOCK), :] = jnp.zeros((BLOCK, 128), BF16)
        vc[pl.ds(0, BLOCK), :] = jnp.zeros((BLOCK, 128), BF16)

        def norm_step(i, carry):
            rows = pl.ds(pl.multiple_of(i * tn, tn), tn)
            qv = q_ref[rows, :]
            kv = k_ref[rows, :]
            qn[rows, :] = (qv * lax.rsqrt(_split_dot(qv * qv, bdv) * (1.0 / HEAD_DIM) + EPS)) * (qg_ref[...] * SCALE)
            kn[rows, :] = (kv * lax.rsqrt(_split_dot(kv * kv, bdv) * (1.0 / HEAD_DIM) + EPS)) * kg_ref[...]
            return carry
        lax.fori_loop(0, S // tn, norm_step, 0)

        def fill(blk, d):
            tokens = _block_tokens(blk, d, S)
            kc[_padded_block(blk), :] = kn[tokens, :].astype(BF16)
            vc[_padded_block(blk), :] = v_ref[tokens, :].astype(BF16)

        def block(blk, d):
            tokens = _block_tokens(blk, d, S)
            keys = pl.ds(pl.multiple_of(blk * BLOCK, BLOCK), 2 * BLOCK)
            q2 = _two_heads(qn[tokens, :].astype(BF16), lo)
            s = jnp.where(band_mask, _nt(q2, kc[keys, :]), NEG) + _first_block_bias(blk, S // d // BLOCK)
            m = jnp.max(s, axis=-1, keepdims=True)
            e = jnp.exp(s - m)
            l = jnp.sum(e, axis=-1, keepdims=True)
            o2 = jnp.dot(e.astype(BF16), vc[keys, :], preferred_element_type=F32) * (1.0 / l)
            lse2 = m + jnp.log(l)
            o = jnp.where(lo, o2[:BLOCK], o2[BLOCK:])
            lse = jnp.where(lo, lse2[:BLOCK], lse2[BLOCK:])
            if d > 1:
                la = lse_ref[tokens, :]
                mx = jnp.maximum(la, lse)
                wa, wb = jnp.exp(la - mx), jnp.exp(lse - mx)
                t = wa + wb
                o = (wa * att_ref[tokens, :] + wb * o) / t
                lse = mx + jnp.log(t)
            att_ref[tokens, :] = o
            lse_ref[tokens, :] = lse

        for d in DILATIONS:
            _for_blocks(S // BLOCK, 4, functools.partial(fill, d=d))
            _for_blocks(S // BLOCK, 8, functools.partial(block, d=d))

        def gate_step(i, carry):
            rows = pl.ds(pl.multiple_of(i * tn, tn), tn)
            silu, _ = _silu_parts(g_ref[rows, :])
            y_ref[rows, :] = (att_ref[rows, :] * silu).astype(BF16)
            return carry
        lax.fori_loop(0, S // tn, gate_step, 0)

    col = lambda j0: pl.BlockSpec((S, 128), lambda p, j0=j0: (0, j0 + p))
    const = lambda shape: pl.BlockSpec(shape, lambda p: (0,) * len(shape))
    out = pl.BlockSpec((S, 128), lambda p: (0, p))
    return pl.pallas_call(
        body, name="attn_fwd", grid=(npairs,),
        in_specs=[col(COL_AQ), col(COL_AK), col(COL_AV), col(COL_AG), const((1, 128)), const((1, 128)),
                  const((128, 128))],
        out_specs=[out, out, out],
        out_shape=[jax.ShapeDtypeStruct((S, ATTN_WIDTH), BF16), jax.ShapeDtypeStruct((S, ATTN_WIDTH), F32),
                   jax.ShapeDtypeStruct((S, ATTN_WIDTH), F32)],
        scratch_shapes=[pltpu.VMEM((S, 128), F32), pltpu.VMEM((S, 128), F32),
                        pltpu.VMEM((S + BLOCK, 128), BF16), pltpu.VMEM((S + BLOCK, 128), BF16)],
        compiler_params=_params(48, ("arbitrary",)),
    )(proj, proj, proj, proj, qg2, kg2, bd)


def _attn_bwd_prep(proj, dycat, att, bd):
    S = proj.shape[0]
    tm = 512

    def body(g_ref, dy_ref, att_ref, bd_ref, do_ref, dd_ref, dg_ref):
        silu, dsilu = _silu_parts(g_ref[...])
        dy = dy_ref[...]
        at = att_ref[...]
        do = dy * silu
        do_ref[...] = do
        dd_ref[...] = _split_dot(do * at, bd_ref[...])
        dg_ref[...] = (dy * at * dsilu).astype(BF16)

    tile = lambda j0: pl.BlockSpec((tm, 256), lambda i, j, j0=j0: (i, j0 + j))
    return pl.pallas_call(
        body, name="attn_bwd_prep", grid=(S // tm, ATTN_WIDTH // 256),
        in_specs=[tile(COL_AG // 2), tile(1), tile(0), pl.BlockSpec((256, 256), lambda i, j: (0, 0))],
        out_specs=[tile(0), tile(0), tile(0)],
        out_shape=[jax.ShapeDtypeStruct((S, ATTN_WIDTH), F32), jax.ShapeDtypeStruct((S, ATTN_WIDTH), F32),
                   jax.ShapeDtypeStruct((S, ATTN_WIDTH), BF16)],
        compiler_params=_params(32, ("arbitrary", "arbitrary")),
    )(proj, dycat, att, bd)


def _attn_bwd(proj, do, dd, lse, qg2, kg2, bd):
    S = proj.shape[0]
    npairs = ATTN_WIDTH // 128
    tn = 512

    def body(q_ref, k_ref, v_ref, do_ref, dd_ref, lse_ref, qg_ref, kg_ref, bd_ref,
             dq_ref, dk_ref, dv_ref, dqg_ref, dkg_ref,
             qn, kn, kc, vc, dk_own, dv_own, dk_prev, dv_prev, dqa, dka, dva):
        bdv = bd_ref[...]
        lo = lax.broadcasted_iota(jnp.int32, (BLOCK, 128), 1) < HEAD_DIM
        zeros_bf = jnp.zeros((BLOCK, 128), BF16)
        zeros_f = jnp.zeros((BLOCK, 128), F32)
        kc[pl.ds(0, BLOCK), :] = zeros_bf
        vc[pl.ds(0, BLOCK), :] = zeros_bf
        dk_prev[pl.ds(S, BLOCK), :] = zeros_f
        dv_prev[pl.ds(S, BLOCK), :] = zeros_f

        def norm_step(i, carry):
            rows = pl.ds(pl.multiple_of(i * tn, tn), tn)
            qv = q_ref[rows, :]
            kv = k_ref[rows, :]
            qn[rows, :] = (qv * lax.rsqrt(_split_dot(qv * qv, bdv) * (1.0 / HEAD_DIM) + EPS)) * (qg_ref[...] * SCALE)
            kn[rows, :] = (kv * lax.rsqrt(_split_dot(kv * kv, bdv) * (1.0 / HEAD_DIM) + EPS)) * kg_ref[...]
            return carry
        lax.fori_loop(0, S // tn, norm_step, 0)

        kt = lax.broadcasted_iota(jnp.int32, (2 * BLOCK, 2 * BLOCK), 0)
        qt = lax.broadcasted_iota(jnp.int32, (2 * BLOCK, 2 * BLOCK), 1) % BLOCK
        band_mask_t = ((kt < BLOCK) & (kt >= qt)) | ((kt >= BLOCK) & ((kt - BLOCK) <= qt))

        def per_query_row(t):
            tt = t.T
            return jnp.concatenate([tt[0:1, :], tt[HEAD_DIM:HEAD_DIM + 1, :]], axis=1)

        def fill(blk, d):
            tokens = _block_tokens(blk, d, S)
            kc[_padded_block(blk), :] = kn[tokens, :].astype(BF16)
            vc[_padded_block(blk), :] = v_ref[tokens, :].astype(BF16)

        def block(blk, d):
            tokens = _block_tokens(blk, d, S)
            own = pl.ds(pl.multiple_of(blk * BLOCK, BLOCK), BLOCK)
            keys = pl.ds(pl.multiple_of(blk * BLOCK, BLOCK), 2 * BLOCK)
            q2 = _two_heads(qn[tokens, :].astype(BF16), lo)
            do2 = _two_heads(do_ref[tokens, :].astype(BF16), lo)
            lse_row = per_query_row(lse_ref[tokens, :])
            dd_row = per_query_row(dd_ref[tokens, :])
            kb = kc[keys, :]
            vb = vc[keys, :]
            st = jnp.where(band_mask_t, _nt(kb, q2), NEG)
            kill = jnp.where((blk & (S // d // BLOCK - 1)) == 0, NEG, 0.0)
            st = jnp.concatenate([st[:BLOCK] + kill, st[BLOCK:]], axis=0)
            pt = jnp.exp(st - lse_row)
            dst = pt * (_nt(vb, do2) - dd_row)
            ptb = pt.astype(BF16)
            dstb = dst.astype(BF16)
            dv_band = jnp.dot(ptb, do2, preferred_element_type=F32)
            dk_band = jnp.dot(dstb, q2, preferred_element_type=F32)
            dv_prev[own, :] = dv_band[:BLOCK]
            dv_own[own, :] = dv_band[BLOCK:]
            dk_prev[own, :] = dk_band[:BLOCK]
            dk_own[own, :] = dk_band[BLOCK:]
            dq2 = _tn(dstb, kb)
            dq = jnp.where(lo, dq2[:BLOCK], dq2[BLOCK:])
            dqa[tokens, :] = dq if d == 1 else dqa[tokens, :] + dq

        def fold(blk, d):
            tokens = _block_tokens(blk, d, S)
            own = pl.ds(pl.multiple_of(blk * BLOCK, BLOCK), BLOCK)
            dk = dk_own[own, :] + dk_prev[_padded_block(blk), :]
            dv = dv_own[own, :] + dv_prev[_padded_block(blk), :]
            dka[tokens, :] = dk if d == 1 else dka[tokens, :] + dk
            dva[tokens, :] = dv if d == 1 else dva[tokens, :] + dv

        for d in DILATIONS:
            _for_blocks(S // BLOCK, 4, functools.partial(fill, d=d))
            _for_blocks(S // BLOCK, 8, functools.partial(block, d=d))
            _for_blocks(S // BLOCK, 4, functools.partial(fold, d=d))

        def out_step(i, carry):
            dqg, dkg = carry
            rows = pl.ds(pl.multiple_of(i * tn, tn), tn)
            qv = q_ref[rows, :]
            kv = k_ref[rows, :]
            rq = lax.rsqrt(_split_dot(qv * qv, bdv) * (1.0 / HEAD_DIM) + EPS)
            rk = lax.rsqrt(_split_dot(kv * kv, bdv) * (1.0 / HEAD_DIM) + EPS)
            qh = qv * rq
            kh = kv * rk
            dqs = dqa[rows, :] * SCALE
            dkn = dka[rows, :]
            aq = dqs * qg_ref[...]
            ak = dkn * kg_ref[...]
            dq_ref[rows, :] = (rq * (aq - qh * (_split_dot(aq * qh, bdv) * (1.0 / HEAD_DIM)))).astype(BF16)
            dk_ref[rows, :] = (rk * (ak - kh * (_split_dot(ak * kh, bdv) * (1.0 / HEAD_DIM)))).astype(BF16)
            dv_ref[rows, :] = dva[rows, :].astype(BF16)
            dqg = dqg + jnp.sum(dqs * qh, axis=0, keepdims=True)
            dkg = dkg + jnp.sum(dkn * kh, axis=0, keepdims=True)
            return dqg, dkg
        zero = jnp.zeros((1, 128), F32)
        dqg, dkg = lax.fori_loop(0, S // tn, out_step, (zero, zero))
        dqg_ref[0] = dqg
        dkg_ref[0] = dkg

    once = pl.Buffered(1)
    col = lambda j0: pl.BlockSpec((S, 128), lambda p, j0=j0: (0, j0 + p), pipeline_mode=once)
    const = lambda shape: pl.BlockSpec(shape, lambda p: (0,) * len(shape))
    out = pl.BlockSpec((S, 128), lambda p: (0, p))
    gain_out = pl.BlockSpec((1, 1, 128), lambda p: (p, 0, 0))
    piece = jax.ShapeDtypeStruct((S, ATTN_WIDTH), BF16)
    gains = jax.ShapeDtypeStruct((npairs, 1, 128), F32)
    f32buf = pltpu.VMEM((S, 128), F32)
    f32pad = pltpu.VMEM((S + BLOCK, 128), F32)
    bf16pad = pltpu.VMEM((S + BLOCK, 128), BF16)
    return pl.pallas_call(
        body, name="attn_bwd", grid=(npairs,),
        in_specs=[col(COL_AQ), col(COL_AK), col(COL_AV), col(0), col(0), col(0), const((1, 128)), const((1, 128)),
                  const((128, 128))],
        out_specs=[out, out, out, gain_out, gain_out],
        out_shape=[piece, piece, piece, gains, gains],
        scratch_shapes=[f32buf, f32buf, bf16pad, bf16pad, f32buf, f32buf, f32pad, f32pad, f32buf, f32buf, f32buf],
        compiler_params=_params(48, ("arbitrary",)),
    )(proj, proj, proj, do, dd, lse, qg2, kg2, bd)


def _mem_kv(mem, gain, wkv_bf, kg4, bd):
    def body(mem_ref, g_ref, w_ref, kg_ref, bd_ref, hm_ref, kraw_ref, mk_ref, mv_ref):
        mv_ = mem_ref[...]
        r = lax.rsqrt(jnp.mean(mv_ * mv_, axis=-1, keepdims=True) + EPS)
        hm = ((mv_ * r) * g_ref[...]).astype(BF16)
        hm_ref[...] = hm
        kv = jnp.dot(hm, w_ref[...], preferred_element_type=F32)
        kraw = kv[:, :MEM_WIDTH]
        kraw_ref[...] = kraw
        ms = _split_dot(kraw * kraw, bd_ref[...]) * (1.0 / HEAD_DIM)
        mk_ref[...] = (kraw * lax.rsqrt(ms + EPS)) * kg_ref[...]
        mv_ref[...] = kv[:, MEM_WIDTH:]

    sq = jax.ShapeDtypeStruct((MEM_LEN, MEM_WIDTH), F32)
    return pl.pallas_call(
        body, name="mem_kv",
        out_shape=[jax.ShapeDtypeStruct((MEM_LEN, D_MODEL), BF16), sq, sq, sq],
        compiler_params=_params(16),
    )(mem, gain, wkv_bf, kg4, bd)


def _mem_fwd(proj, mk, mv, qg4, bd):
    S = proj.shape[0]
    tm = 512

    def body(q_ref, g_ref, mk_ref, mv_ref, qg_ref, bd_ref, y_ref, om_ref):
        qv = q_ref[...]
        ms = _split_dot(qv * qv, bd_ref[...]) * (1.0 / HEAD_DIM)
        qs = (qv * lax.rsqrt(ms + EPS)) * (qg_ref[...] * SCALE)
        mkb = mk_ref[...].astype(BF16)
        mvb = mv_ref[...].astype(BF16)
        head = _head_index((tm, MEM_WIDTH))
        o = jnp.zeros((tm, MEM_WIDTH), F32)
        for h in range(4):
            s = _nt(jnp.where(head == h, qs, 0.0).astype(BF16), mkb)
            e = jnp.exp(s - jnp.max(s, axis=-1, keepdims=True))
            p = e * (1.0 / jnp.sum(e, axis=-1, keepdims=True))
            o = jnp.where(head == h, jnp.dot(p.astype(BF16), mvb, preferred_element_type=F32), o)
        om_ref[...] = o
        silu, _ = _silu_parts(g_ref[...])
        y_ref[...] = (o * silu).astype(BF16)

    col = lambda j: pl.BlockSpec((tm, MEM_WIDTH), lambda i, j=j: (i, j))
    const = lambda shape: pl.BlockSpec(shape, lambda i: (0,) * len(shape))
    tile = pl.BlockSpec((tm, MEM_WIDTH), lambda i: (i, 0))
    return pl.pallas_call(
        body, name="mem_fwd", grid=(S // tm,),
        in_specs=[col(11), col(12), const((MEM_LEN, MEM_WIDTH)), const((MEM_LEN, MEM_WIDTH)), const((1, MEM_WIDTH)),
                  const((MEM_WIDTH, MEM_WIDTH))],
        out_specs=[tile, tile],
        out_shape=[jax.ShapeDtypeStruct((S, MEM_WIDTH), BF16), jax.ShapeDtypeStruct((S, MEM_WIDTH), F32)],
        compiler_params=_params(24, ("arbitrary",)),
    )(proj, proj, mk, mv, qg4, bd)


def _mem_bwd(proj, dycat, om, mk, mv, qg4, bd):
    S = proj.shape[0]
    tm = 512

    def body(q_ref, g_ref, dy_ref, om_ref, mk_ref, mv_ref, qg_ref, bd_ref,
             dq_ref, dg_ref, dmk_ref, dmv_ref, dqg_ref):
        i = pl.program_id(0)

        @pl.when(i == 0)
        def _():
            dmk_ref[...] = jnp.zeros_like(dmk_ref)
            dmv_ref[...] = jnp.zeros_like(dmv_ref)
            dqg_ref[...] = jnp.zeros_like(dqg_ref)

        bdv = bd_ref[...]
        qv = q_ref[...]
        rq = lax.rsqrt(_split_dot(qv * qv, bdv) * (1.0 / HEAD_DIM) + EPS)
        qh = qv * rq
        qs = qh * (qg_ref[...] * SCALE)
        silu, dsilu = _silu_parts(g_ref[...])
        dy = dy_ref[...]
        o = om_ref[...]
        do = dy * silu
        dg_ref[...] = (dy * o * dsilu).astype(BF16)
        dd = _split_dot(do * o, bdv)
        mkb = mk_ref[...].astype(BF16)
        mvb = mv_ref[...].astype(BF16)
        head = _head_index((tm, MEM_WIDTH))
        dqs = jnp.zeros((tm, MEM_WIDTH), F32)
        for h in range(4):
            qhd = jnp.where(head == h, qs, 0.0).astype(BF16)
            doh = jnp.where(head == h, do, 0.0).astype(BF16)
            s = _nt(qhd, mkb)
            e = jnp.exp(s - jnp.max(s, axis=-1, keepdims=True))
            p = e * (1.0 / jnp.sum(e, axis=-1, keepdims=True))
            ds = p * (_nt(doh, mvb) - dd[:, h * HEAD_DIM:h * HEAD_DIM + 1])
            dsb = ds.astype(BF16)
            dmv_ref[...] += _tn(p.astype(BF16), doh)
            dmk_ref[...] += _tn(dsb, qhd)
            dqs = jnp.where(head == h, jnp.dot(dsb, mkb, preferred_element_type=F32), dqs)
        dqs = dqs * SCALE
        a = dqs * qg_ref[...]
        dq_ref[...] = (rq * (a - qh * (_split_dot(a * qh, bdv) * (1.0 / HEAD_DIM)))).astype(BF16)
        dqg_ref[...] += jnp.sum(dqs * qh, axis=0, keepdims=True)

    col = lambda j: pl.BlockSpec((tm, MEM_WIDTH), lambda i, j=j: (i, j))
    const = lambda shape: pl.BlockSpec(shape, lambda i: (0,) * len(shape))
    tile = pl.BlockSpec((tm, MEM_WIDTH), lambda i: (i, 0))
    piece = jax.ShapeDtypeStruct((S, MEM_WIDTH), BF16)
    sq = jax.ShapeDtypeStruct((MEM_LEN, MEM_WIDTH), F32)
    return pl.pallas_call(
        body, name="mem_bwd", grid=(S // tm,),
        in_specs=[col(11), col(12), col(3), tile, const((MEM_LEN, MEM_WIDTH)), const((MEM_LEN, MEM_WIDTH)),
                  const((1, MEM_WIDTH)), const((MEM_WIDTH, MEM_WIDTH))],
        out_specs=[tile, tile, const((MEM_LEN, MEM_WIDTH)), const((MEM_LEN, MEM_WIDTH)), const((1, MEM_WIDTH))],
        out_shape=[piece, piece, sq, sq, jax.ShapeDtypeStruct((1, MEM_WIDTH), F32)],
        compiler_params=_params(32, ("arbitrary",)),
    )(proj, proj, dycat, om, mk, mv, qg4, bd)


def _mem_kv_bwd(dmk, dmv, kraw, mem, gain, kg4, wkv_bf, hm_bf, bd):
    def body(dmk_ref, dmv_ref, kraw_ref, mem_ref, g_ref, kg_ref, w_ref, hm_ref, bd_ref, dw_ref, dg_ref, dkg_ref):
        bdv = bd_ref[...]
        kraw = kraw_ref[...]
        rk = lax.rsqrt(_split_dot(kraw * kraw, bdv) * (1.0 / HEAD_DIM) + EPS)
        kh = kraw * rk
        dmkv = dmk_ref[...]
        a = dmkv * kg_ref[...]
        dkraw = rk * (a - kh * (_split_dot(a * kh, bdv) * (1.0 / HEAD_DIM)))
        dkg_ref[...] = jnp.sum(dmkv * kh, axis=0, keepdims=True)
        dkv = jnp.concatenate([dkraw, dmv_ref[...]], axis=1).astype(BF16)
        dw_ref[...] = _tn(hm_ref[...], dkv)
        dhm = _nt(dkv, w_ref[...])
        mv_ = mem_ref[...]
        r = lax.rsqrt(jnp.mean(mv_ * mv_, axis=-1, keepdims=True) + EPS)
        dg_ref[...] = jnp.sum(dhm * (mv_ * r), axis=0, keepdims=True)

    return pl.pallas_call(
        body, name="mem_kv_bwd",
        out_shape=[jax.ShapeDtypeStruct((D_MODEL, 2 * MEM_WIDTH), F32), jax.ShapeDtypeStruct((1, D_MODEL), F32),
                   jax.ShapeDtypeStruct((1, MEM_WIDTH), F32)],
        compiler_params=_params(24),
    )(dmk, dmv, kraw, mem, gain, kg4, wkv_bf, hm_bf, bd)


def _out_loss(yg, ya, ym, x, tgt, wout_bf):
    S, D = x.shape
    tm = 256

    def body(yg_ref, ya_ref, ym_ref, x_ref, t_ref, w_ref, dout_ref, dycat_ref, dw_ref, loss_ref):
        @pl.when(pl.program_id(0) == 0)
        def _():
            dw_ref[...] = jnp.zeros_like(dw_ref)
            loss_ref[...] = jnp.zeros_like(loss_ref)

        ycat = jnp.concatenate([yg_ref[...], ya_ref[...], ym_ref[...]], axis=1)
        w = w_ref[...]
        diff = (x_ref[...] + jnp.dot(ycat, w, preferred_element_type=F32)) - t_ref[...]
        loss_ref[...] += jnp.sum(diff * diff, axis=0, keepdims=True)
        dout = diff * (1.0 / D)
        dout_ref[...] = dout
        db = dout.astype(BF16)
        dycat_ref[...] = _nt(db, w)
        dw_ref[...] += _tn(ycat, db)

    tile = lambda w: pl.BlockSpec((tm, w), lambda i: (i, 0))
    const = lambda shape: pl.BlockSpec(shape, lambda i: (0,) * len(shape))
    return pl.pallas_call(
        body, name="out_loss", grid=(S // tm,),
        in_specs=[tile(GMLP_WIDTH), tile(ATTN_WIDTH), tile(MEM_WIDTH), tile(D), tile(D), const((D, D))],
        out_specs=[tile(D), tile(D), const((D, D)), const((1, D))],
        out_shape=[jax.ShapeDtypeStruct((S, D), F32), jax.ShapeDtypeStruct((S, D), F32),
                   jax.ShapeDtypeStruct((D, D), F32), jax.ShapeDtypeStruct((1, D), F32)],
        compiler_params=_params(40, ("arbitrary",)),
    )(yg, ya, ym, x, tgt, wout_bf)


def _piece_specs(pieces, tm):
    return [pl.BlockSpec((tm, p.shape[1]), lambda i: (i, 0)) for p in pieces]


def _in_bwd_dx(pieces, x, dout, gain, win_bf):
    S, D = x.shape
    N = win_bf.shape[1]
    tm = 256
    n = len(pieces)

    def body(*refs):
        piece_refs = refs[:n]
        x_ref, dout_ref, g_ref, w_ref, gx_ref, dg_ref = refs[n:]

        @pl.when(pl.program_id(0) == 0)
        def _():
            dg_ref[...] = jnp.zeros_like(dg_ref)

        dproj = jnp.concatenate([r[...] for r in piece_refs], axis=1)
        dh = _nt(dproj, w_ref[...])
        xv = x_ref[...]
        r = lax.rsqrt(jnp.mean(xv * xv, axis=-1, keepdims=True) + EPS)
        xh = xv * r
        a = dh * g_ref[...]
        gx_ref[...] = dout_ref[...] + r * (a - xh * jnp.mean(a * xh, axis=-1, keepdims=True))
        dg_ref[...] += jnp.sum(dh * xh, axis=0, keepdims=True)

    tile = pl.BlockSpec((tm, D), lambda i: (i, 0))
    const = lambda shape: pl.BlockSpec(shape, lambda i: (0,) * len(shape))
    return pl.pallas_call(
        body, name="in_bwd_dx", grid=(S // tm,),
        in_specs=_piece_specs(pieces, tm) + [tile, tile, const((1, D)), const((D, N))],
        out_specs=[tile, const((1, D))],
        out_shape=[jax.ShapeDtypeStruct((S, D), F32), jax.ShapeDtypeStruct((1, D), F32)],
        compiler_params=_params(40, ("arbitrary",)),
    )(*pieces, x, dout, gain, win_bf)


def _in_bwd_dw(pieces, h_bf):
    S, D = h_bf.shape
    N = sum(p.shape[1] for p in pieces)
    tm = 256
    n = len(pieces)

    def body(*refs):
        piece_refs = refs[:n]
        h_ref, dw_ref = refs[n:]

        @pl.when(pl.program_id(0) == 0)
        def _():
            dw_ref[...] = jnp.zeros_like(dw_ref)

        dproj = jnp.concatenate([r[...] for r in piece_refs], axis=1)
        dw_ref[...] += _tn(h_ref[...], dproj)

    return pl.pallas_call(
        body, name="in_bwd_dw", grid=(S // tm,),
        in_specs=_piece_specs(pieces, tm) + [pl.BlockSpec((tm, D), lambda i: (i, 0))],
        out_specs=pl.BlockSpec((D, N), lambda i: (0, 0)),
        out_shape=jax.ShapeDtypeStruct((D, N), F32),
        compiler_params=_params(48, ("arbitrary",)),
    )(*pieces, h_bf)


def _place():
    x, y, c = lax.axis_index("x"), lax.axis_index("y"), lax.axis_index("c")
    chips = [(1 - x, y), (x, 1 - y), (1 - x, 1 - y)]
    return x, y, c, chips


def _all_gather(arrs, name):
    n = len(arrs)

    def body(*refs):
        ins, outs = refs[:n], refs[n:2 * n]
        send_sems, recv_sems, local_sems = refs[2 * n:]
        x, y, c, chips = _place()
        me, sibling = (x, y, c), (x, y, 1 - c)

        def rows(a, px, py, pc):
            m = ins[a].shape[0]
            return outs[a].at[pl.ds((4 * px + 2 * py + pc) * m, m), :]

        def copy(a, k, block, to, src=None):
            return pltpu.make_async_remote_copy(
                src_ref=rows(a, *block) if src is None else src, dst_ref=rows(a, *block),
                send_sem=send_sems.at[a, k], recv_sem=recv_sems.at[a, k], device_id=to, device_id_type=MESH)

        mine = [pltpu.make_async_copy(ins[a], rows(a, *me), local_sems.at[a]) for a in range(n)]
        for cp in mine:
            cp.start()
        first = []
        for a in range(n):
            first.append(copy(a, 0, me, sibling, src=ins[a]))
            first += [copy(a, 1 + j, me, (*chip, c), src=ins[a]) for j, chip in enumerate(chips)]
        for cp in first:
            cp.start()
        passed = []
        for j, chip in enumerate(chips):
            for a in range(n):
                copy(a, 1 + j, (*chip, c), me).wait_recv()
                fwd = copy(a, 4 + j, (*chip, c), sibling)
                fwd.start()
                passed.append(fwd)
        for a in range(n):
            copy(a, 0, sibling, me).wait_recv()
            for j, chip in enumerate(chips):
                copy(a, 4 + j, (*chip, 1 - c), me).wait_recv()
        for cp in first + passed:
            cp.wait_send()
        for cp in mine:
            cp.wait()

    vmem = pl.BlockSpec(memory_space=pltpu.VMEM)
    return pl.pallas_call(
        body, name=name,
        out_shape=[jax.ShapeDtypeStruct((N_DEV * a.shape[0], a.shape[1]), a.dtype) for a in arrs],
        in_specs=[vmem] * n, out_specs=[vmem] * n,
        scratch_shapes=[pltpu.SemaphoreType.DMA((n, 7)), pltpu.SemaphoreType.DMA((n, 7)), pltpu.SemaphoreType.DMA((n,))],
        compiler_params=_params(40),
    )(*arrs)


def _reduce_scatter(arrs, name):
    n = len(arrs)
    tr = 128

    def body(*refs):
        ins, outs = refs[:n], refs[n:2 * n]
        half, quarter = refs[2 * n:3 * n], refs[3 * n:4 * n]
        send_sems, recv_sems = refs[4 * n:]
        x, y, c, chips = _place()
        sibling = (x, y, 1 - c)

        to_sibling = [pltpu.make_async_remote_copy(
            src_ref=ins[a].at[2 * q + (1 - c)], dst_ref=half[a].at[q], send_sem=send_sems.at[a, q],
            recv_sem=recv_sems.at[a, q], device_id=sibling, device_id_type=MESH) for a in range(n) for q in range(4)]
        for cp in to_sibling:
            cp.start()
        for cp in to_sibling:
            cp.wait_recv()

        def add_rows(a, fn):
            m = ins[a].shape[1]
            def step(i, carry):
                fn(pl.ds(pl.multiple_of(i * tr, tr), tr))
                return carry
            lax.fori_loop(0, m // tr, step, 0)

        for a in range(n):
            for q in range(4):
                def add_half(rows, a=a, q=q):
                    both = ins[a][2 * q + c, rows, :].astype(F32) + half[a][q, rows, :].astype(F32)
                    half[a][q, rows, :] = both.astype(BF16)
                add_rows(a, add_half)

        to_chips = [pltpu.make_async_remote_copy(
            src_ref=half[a].at[2 * chip[0] + chip[1]], dst_ref=quarter[a].at[k], send_sem=send_sems.at[a, 4 + k],
            recv_sem=recv_sems.at[a, 4 + k], device_id=(*chip, c), device_id_type=MESH)
            for a in range(n) for k, chip in enumerate(chips)]
        for cp in to_chips:
            cp.start()
        for cp in to_chips:
            cp.wait_recv()
        for a in range(n):
            def add_quarters(rows, a=a):
                f = lambda t: t.astype(F32)
                outs[a][rows, :] = ((f(half[a][2 * x + y, rows, :]) + f(quarter[a][0, rows, :]))
                                    + (f(quarter[a][1, rows, :]) + f(quarter[a][2, rows, :])))
            add_rows(a, add_quarters)
        for cp in to_sibling + to_chips:
            cp.wait_send()

    vmem = pl.BlockSpec(memory_space=pltpu.VMEM)
    return pl.pallas_call(
        body, name=name,
        out_shape=[jax.ShapeDtypeStruct(a.shape[1:], F32) for a in arrs],
        in_specs=[vmem] * n, out_specs=[vmem] * n,
        scratch_shapes=[pltpu.VMEM((4,) + a.shape[1:], BF16) for a in arrs]
        + [pltpu.VMEM((3,) + a.shape[1:], BF16) for a in arrs]
        + [pltpu.SemaphoreType.DMA((n, 7)), pltpu.SemaphoreType.DMA((n, 7))],
        compiler_params=_params(48),
    )(*arrs)


def _adamw_math(w, g, m, v):
    m = ADAM_B1 * m + (1.0 - ADAM_B1) * g
    v = ADAM_B2 * v + (1.0 - ADAM_B2) * (g * g)
    m_hat = m / (1.0 - ADAM_B1 ** ADAM_STEP)
    v_hat = v / (1.0 - ADAM_B2 ** ADAM_STEP)
    delta = -ADAM_LR * (m_hat / (jnp.sqrt(v_hat) + ADAM_EPS) + ADAM_WD * w)
    return delta, m, v


def _adamw(w, g, m, v, name):
    R, C = w.shape
    tr = 128 if R % 128 == 0 else R

    def body(w_ref, g_ref, m_ref, v_ref, d_ref, nm_ref, nv_ref):
        d_ref[...], nm_ref[...], nv_ref[...] = _adamw_math(w_ref[...], g_ref[...], m_ref[...], v_ref[...])

    tile = pl.BlockSpec((tr, C), lambda i: (i, 0))
    out = jax.ShapeDtypeStruct((R, C), F32)
    return pl.pallas_call(
        body, name=name, grid=(R // tr,), in_specs=[tile] * 4, out_specs=[tile] * 3, out_shape=[out] * 3,
        compiler_params=_params(16, ("arbitrary",)),
    )(w, g, m, v)


def _adamw_summed(w, g_all, m, v, name):
    R, C = w.shape

    def body(w_ref, g_ref, m_ref, v_ref, gs_ref, d_ref, nm_ref, nv_ref):
        g = g_ref[0:R, :]
        for k in range(1, N_DEV):
            g = g + g_ref[k * R:(k + 1) * R, :]
        gs_ref[...] = g
        d_ref[...], nm_ref[...], nv_ref[...] = _adamw_math(w_ref[...], g, m_ref[...], v_ref[...])

    out = jax.ShapeDtypeStruct((R, C), F32)
    return pl.pallas_call(body, name=name, out_shape=[out] * 4, compiler_params=_params(16))(w, g_all, m, v)


SMALL = ("norm_gain", "gmlp_v_gain", "gmlp_w_s", "gmlp_b", "attn_q_gain", "attn_k_gain", "mem_norm_gain",
         "mem_q_gain", "mem_k_gain")
WEIGHTS = ("norm_gain", "w_in", "gmlp_v_gain", "gmlp_w_s", "gmlp_b", "attn_q_gain", "attn_k_gain",
           "mem_norm_gain", "w_mem_kv", "mem_q_gain", "mem_k_gain", "w_out")


def _pack(tree):
    return jnp.concatenate([tree[k].reshape(-1) for k in SMALL]).reshape(-1, 128)


def _unpack(packed, like):
    flat = packed.reshape(-1)
    out, at = {}, 0
    for k in SMALL:
        out[k] = flat[at:at + like[k].size].reshape(like[k].shape)
        at += like[k].size
    return out


def _local_grads(x, mem, tgt, w, win_bf, wkv_bf, wout_bf):
    bd128, bd256 = _head_blockdiag(128), _head_blockdiag(256)
    gain = w["norm_gain"].reshape(1, D_MODEL)
    vg = w["gmlp_v_gain"].reshape(1, GMLP_WIDTH)
    w_s = w["gmlp_w_s"].reshape(4, CHUNK, CHUNK)
    b2 = jnp.repeat(w["gmlp_b"].reshape(4, CHUNK).T, HEAD_DIM, axis=1)
    qg2 = jnp.tile(w["attn_q_gain"].reshape(1, HEAD_DIM), (1, 2))
    kg2 = jnp.tile(w["attn_k_gain"].reshape(1, HEAD_DIM), (1, 2))
    mqg4 = jnp.tile(w["mem_q_gain"].reshape(1, HEAD_DIM), (1, 4))
    mkg4 = jnp.tile(w["mem_k_gain"].reshape(1, HEAD_DIM), (1, 4))
    mgain = w["mem_norm_gain"].reshape(1, D_MODEL)

    proj, h_bf = _rms_proj(x, gain, win_bf)
    yg = _gmlp_fwd(proj, vg, w_s, b2, bd256)
    ya, att, lse = _attn_fwd(proj, qg2, kg2, bd128)
    hm_bf, kraw, mk, mv = _mem_kv(mem, mgain, wkv_bf, mkg4, bd256)
    ym, om = _mem_fwd(proj, mk, mv, mqg4, bd256)
    dout, dycat, dwout, sq = _out_loss(yg, ya, ym, x, tgt, wout_bf)

    du, dgv, dgg, dws, db2, dvg = _gmlp_bwd(proj, dycat, vg, w_s, b2, bd256)
    do, dd, dag = _attn_bwd_prep(proj, dycat, att, bd256)
    dq, dk, dv, dqg, dkg = _attn_bwd(proj, do, dd, lse, qg2, kg2, bd128)
    dmq, dmg, dmk, dmv, dmqg = _mem_bwd(proj, dycat, om, mk, mv, mqg4, bd256)
    dwkv, dmgain, dmkg = _mem_kv_bwd(dmk, dmv, kraw, mem, mgain, mkg4, wkv_bf, hm_bf, bd256)
    pieces = [du, dgv, dgg, dq, dk, dv, dag, dmq, dmg]
    grad_x, dgain = _in_bwd_dx(pieces, x, dout, gain, win_bf)
    dwin = _in_bwd_dw(pieces, h_bf)

    small = {
        "norm_gain": dgain,
        "gmlp_v_gain": dvg,
        "gmlp_w_s": dws,
        "gmlp_b": db2[:, ::HEAD_DIM].T,
        "attn_q_gain": dqg.reshape(-1, HEAD_DIM).sum(axis=0),
        "attn_k_gain": dkg.reshape(-1, HEAD_DIM).sum(axis=0),
        "mem_norm_gain": dmgain,
        "mem_q_gain": dmqg.reshape(-1, HEAD_DIM).sum(axis=0),
        "mem_k_gain": dmkg.reshape(-1, HEAD_DIM).sum(axis=0),
    }
    return sq, grad_x, dwin, dwkv, dwout, small


def kernel(x, mem, norm_gain, w_in, gmlp_v_gain, gmlp_w_s, gmlp_b, attn_q_gain, attn_k_gain, mem_norm_gain, w_mem_kv, mem_q_gain, mem_k_gain, w_out, loss_target, m_norm_gain, m_w_in, m_gmlp_v_gain, m_gmlp_w_s, m_gmlp_b, m_attn_q_gain, m_attn_k_gain, m_mem_norm_gain, m_w_mem_kv, m_mem_q_gain, m_mem_k_gain, m_w_out, v_norm_gain, v_w_in, v_gmlp_v_gain, v_gmlp_w_s, v_gmlp_b, v_attn_q_gain, v_attn_k_gain, v_mem_norm_gain, v_w_mem_kv, v_mem_q_gain, v_mem_k_gain, v_w_out):
    w = dict(norm_gain=norm_gain, w_in=w_in, gmlp_v_gain=gmlp_v_gain, gmlp_w_s=gmlp_w_s, gmlp_b=gmlp_b,
             attn_q_gain=attn_q_gain, attn_k_gain=attn_k_gain, mem_norm_gain=mem_norm_gain, w_mem_kv=w_mem_kv,
             mem_q_gain=mem_q_gain, mem_k_gain=mem_k_gain, w_out=w_out)
    m = dict(norm_gain=m_norm_gain, w_in=m_w_in, gmlp_v_gain=m_gmlp_v_gain, gmlp_w_s=m_gmlp_w_s, gmlp_b=m_gmlp_b,
             attn_q_gain=m_attn_q_gain, attn_k_gain=m_attn_k_gain, mem_norm_gain=m_mem_norm_gain,
             w_mem_kv=m_w_mem_kv, mem_q_gain=m_mem_q_gain, mem_k_gain=m_mem_k_gain, w_out=m_w_out)
    v = dict(norm_gain=v_norm_gain, w_in=v_w_in, gmlp_v_gain=v_gmlp_v_gain, gmlp_w_s=v_gmlp_w_s, gmlp_b=v_gmlp_b,
             attn_q_gain=v_attn_q_gain, attn_k_gain=v_attn_k_gain, mem_norm_gain=v_mem_norm_gain,
             w_mem_kv=v_w_mem_kv, mem_q_gain=v_mem_q_gain, mem_k_gain=v_mem_k_gain, w_out=v_w_out)
    S = x.shape[1]
    n_in = w_in.shape[2]
    win_sh, wkv_sh, wout_sh = w_in[0], w_mem_kv[0], w_out[0]

    win_all, wkv_bf, wout_bf = _all_gather(
        [win_sh.astype(BF16), wkv_sh.astype(BF16), wout_sh.astype(BF16)], "gather_weights")
    win_bf = win_all.reshape(N_DEV, D_MODEL, n_in).transpose(1, 0, 2).reshape(D_MODEL, N_DEV * n_in)

    sq, grad_x, dwin, dwkv, dwout, small = _local_grads(x[0], mem[0], loss_target[0], w, win_bf, wkv_bf, wout_bf)
    loss = lax.psum(jnp.sum(sq) * (0.5 / D_MODEL), ("x", "y", "c"))

    dwin_blocks = dwin.reshape(D_MODEL, N_DEV, n_in).transpose(1, 0, 2).astype(BF16)
    (g_win,) = _reduce_scatter([dwin_blocks], "reduce_w_in")
    g_wkv, g_wout = _reduce_scatter(
        [dwkv.reshape(N_DEV, D_MODEL // N_DEV, -1).astype(BF16),
         dwout.reshape(N_DEV, D_MODEL // N_DEV, -1).astype(BF16)], "reduce_w_kv_out")
    (small_all,) = _all_gather([_pack(small)], "gather_small_grads")

    out_g, out_d, out_m, out_v = {}, {}, {}, {}
    for name, g in (("w_in", g_win), ("w_mem_kv", g_wkv), ("w_out", g_wout)):
        d_, m_, v_ = _adamw(w[name][0], g, m[name][0], v[name][0], "adamw_" + name)
        out_g[name], out_d[name], out_m[name], out_v[name] = g[None], d_[None], m_[None], v_[None]
    gs, ds, ms, vs = _adamw_summed(_pack(w), small_all, _pack(m), _pack(v), "adamw_small")
    for tree, packed in ((out_g, gs), (out_d, ds), (out_m, ms), (out_v, vs)):
        tree.update(_unpack(packed, w))

    return (loss, grad_x[None], *[out_g[k] for k in WEIGHTS], *[out_d[k] for k in WEIGHTS],
            *[out_m[k] for k in WEIGHTS], *[out_v[k] for k in WEIGHTS])
```

```python
import functools
import math

import jax
import jax.numpy as jnp
from jax import lax
from jax.experimental import pallas as pl
from jax.experimental.pallas import tpu as pltpu

F32 = jnp.float32
BF16 = jnp.bfloat16

N_DEV = 8
D_MODEL = 1024
HEAD_DIM = 64
GMLP_WIDTH = 256
ATTN_WIDTH = 512
MEM_WIDTH = 256
MEM_LEN = 256
IN_WIDTH = 3 * GMLP_WIDTH + 4 * ATTN_WIDTH + 2 * MEM_WIDTH
CHUNK = 128
BLOCK = 128
DILATIONS = (1, 4, 16)
EPS = 1e-6
SCALE = 1.0 / math.sqrt(HEAD_DIM)
NEG = -1e30

ADAM_LR = 0.001
ADAM_B1 = 0.9
ADAM_B2 = 0.999
ADAM_EPS = 1e-08
ADAM_WD = 0.01
ADAM_STEP = 10

MIB = 1024 * 1024
MESH = pl.DeviceIdType.MESH

COL_AQ, COL_AK, COL_AV, COL_AG = 6, 10, 14, 18


def _params(vmem_mib, semantics=None):
    kw = dict(vmem_limit_bytes=vmem_mib * MIB)
    if semantics is not None:
        kw["dimension_semantics"] = semantics
    return pltpu.CompilerParams(**kw)


def _split_dot(x, sel_bf):
    hi = x.astype(BF16)
    lo = (x - hi.astype(F32)).astype(BF16)
    return jnp.dot(hi, sel_bf, preferred_element_type=F32) + jnp.dot(lo, sel_bf, preferred_element_type=F32)


def _nt(a, b):
    return lax.dot_general(a, b, (((1,), (1,)), ((), ())), preferred_element_type=F32)


def _tn(a, b):
    return lax.dot_general(a, b, (((0,), (0,)), ((), ())), preferred_element_type=F32)


def _silu_parts(g):
    sg = jax.nn.sigmoid(g)
    return g * sg, sg * (1.0 + g * (1.0 - sg))


def _head_index(shape):
    return lax.shift_right_logical(lax.broadcasted_iota(jnp.int32, shape, 1), HEAD_DIM.bit_length() - 1)


def _head_blockdiag(width):
    i = jnp.arange(width) // HEAD_DIM
    return (i[:, None] == i[None, :]).astype(BF16)


def _rms_proj(x, gain, w_blocks):
    S, D = x.shape
    n_dev, _, n = w_blocks.shape
    N = n_dev * n
    tm = 256

    def body(x_ref, g_ref, wb_ref, proj_ref, h_ref, w_ref):
        @pl.when(pl.program_id(0) == 0)
        def _():
            for j in range(n_dev):
                w_ref[:, n * j:n * (j + 1)] = wb_ref[j]

        xv = x_ref[...]
        r = lax.rsqrt(jnp.mean(xv * xv, axis=-1, keepdims=True) + EPS)
        h = ((xv * r) * g_ref[...]).astype(BF16)
        h_ref[...] = h
        proj_ref[...] = jnp.dot(h, w_ref[...], preferred_element_type=F32)

    return pl.pallas_call(
        body, name="rms_proj", grid=(S // tm,),
        in_specs=[pl.BlockSpec((tm, D), lambda i: (i, 0)), pl.BlockSpec((1, D), lambda i: (0, 0)),
                  pl.BlockSpec((n_dev, D, n), lambda i: (0, 0, 0), pipeline_mode=pl.Buffered(1))],
        out_specs=[pl.BlockSpec((tm, N), lambda i: (i, 0)), pl.BlockSpec((tm, D), lambda i: (i, 0)),
                   pl.BlockSpec((D, N), lambda i: (0, 0))],
        out_shape=[jax.ShapeDtypeStruct((S, N), F32), jax.ShapeDtypeStruct((S, D), BF16),
                   jax.ShapeDtypeStruct((D, N), BF16)],
        compiler_params=_params(48, ("arbitrary",)),
    )(x, gain, w_blocks)


def _gmlp_masked_weights(ws_ref, transpose):
    t = lax.broadcasted_iota(jnp.int32, (CHUNK, CHUNK), 0)
    s = lax.broadcasted_iota(jnp.int32, (CHUNK, CHUNK), 1)
    parts = []
    for h in range(4):
        wm = jnp.where(s <= t, ws_ref[h], 0.0)
        parts.append(wm.T if transpose else wm)
    return jnp.concatenate(parts, axis=1).astype(BF16)


def _head_stack(v, head):
    return jnp.concatenate([jnp.where(head == h, v, 0.0) for h in range(4)], axis=0).astype(BF16)


def _gmlp_fwd(proj, vg, w_s, b2, bd):
    S = proj.shape[0]
    tm = 512

    def body(u_ref, v_ref, g_ref, vg_ref, ws_ref, b2_ref, bd_ref, y_ref):
        v = v_ref[...]
        ms = _split_dot(v * v, bd_ref[...]) * (1.0 / HEAD_DIM)
        vn = (v * lax.rsqrt(ms + EPS)) * vg_ref[...]
        wcat = _gmlp_masked_weights(ws_ref, False)
        head = _head_index((CHUNK, GMLP_WIDTH))
        for c in range(tm // CHUNK):
            rows = slice(c * CHUNK, (c + 1) * CHUNK)
            sp = jnp.dot(wcat, _head_stack(vn[rows], head), preferred_element_type=F32) + b2_ref[...]
            silu, _ = _silu_parts(g_ref[rows, :])
            y_ref[rows, :] = ((u_ref[rows, :] * sp) * silu).astype(BF16)

    col = lambda j: pl.BlockSpec((tm, GMLP_WIDTH), lambda i, j=j: (i, j))
    const = lambda shape: pl.BlockSpec(shape, lambda i: (0,) * len(shape))
    return pl.pallas_call(
        body, name="gmlp_fwd", grid=(S // tm,),
        in_specs=[col(0), col(1), col(2), const((1, GMLP_WIDTH)), const((4, CHUNK, CHUNK)),
                  const((CHUNK, GMLP_WIDTH)), const((GMLP_WIDTH, GMLP_WIDTH))],
        out_specs=pl.BlockSpec((tm, GMLP_WIDTH), lambda i: (i, 0)),
        out_shape=jax.ShapeDtypeStruct((S, GMLP_WIDTH), BF16),
        compiler_params=_params(24, ("arbitrary",)),
    )(proj, proj, proj, vg, w_s, b2, bd)


def _gmlp_bwd(proj, dycat, vg, w_s, b2, bd):
    S = proj.shape[0]
    tm = 512
    nsteps = S // tm

    def body(u_ref, v_ref, g_ref, dy_ref, vg_ref, ws_ref, b2_ref, bd_ref,
             du_ref, dv_ref, dg_ref, dws_ref, db2_ref, dvg_ref):
        i = pl.program_id(0)

        @pl.when(i == 0)
        def _():
            dws_ref[...] = jnp.zeros_like(dws_ref)
            db2_ref[...] = jnp.zeros_like(db2_ref)
            dvg_ref[...] = jnp.zeros_like(dvg_ref)

        bdv = bd_ref[...]
        v = v_ref[...]
        ms = _split_dot(v * v, bdv) * (1.0 / HEAD_DIM)
        rv = lax.rsqrt(ms + EPS)
        xhat = v * rv
        vgv = vg_ref[...]
        vn = xhat * vgv
        wcat = _gmlp_masked_weights(ws_ref, False)
        wcat_t = _gmlp_masked_weights(ws_ref, True)
        head = _head_index((CHUNK, GMLP_WIDTH))
        dvg = jnp.zeros((1, GMLP_WIDTH), F32)
        for c in range(tm // CHUNK):
            rows = slice(c * CHUNK, (c + 1) * CHUNK)
            vn_c = vn[rows]
            spb = jnp.dot(wcat, _head_stack(vn_c, head), preferred_element_type=F32) + b2_ref[...]
            silu, dsilu = _silu_parts(g_ref[rows, :])
            dy = dy_ref[rows, :]
            u = u_ref[rows, :]
            du_ref[rows, :] = (dy * spb * silu).astype(BF16)
            dg_ref[rows, :] = (dy * u * spb * dsilu).astype(BF16)
            dsp = dy * u * silu
            db2_ref[...] += dsp
            dstack = _head_stack(dsp, head)
            dvn = jnp.dot(wcat_t, dstack, preferred_element_type=F32)
            dws_ref[...] += _nt(dstack, vn_c.astype(BF16))
            xh = xhat[rows]
            a = dvn * vgv
            mean_ax = _split_dot(a * xh, bdv) * (1.0 / HEAD_DIM)
            dv_ref[rows, :] = (rv[rows] * (a - xh * mean_ax)).astype(BF16)
            dvg = dvg + jnp.sum(dvn * xh, axis=0, keepdims=True)
        dvg_ref[...] += dvg

        @pl.when(i == nsteps - 1)
        def _():
            t = lax.broadcasted_iota(jnp.int32, (4 * CHUNK, CHUNK), 0) % CHUNK
            s = lax.broadcasted_iota(jnp.int32, (4 * CHUNK, CHUNK), 1)
            dws_ref[...] = jnp.where(s <= t, dws_ref[...], 0.0)
            db2_ref[...] = _split_dot(db2_ref[...], bdv)

    col = lambda j: pl.BlockSpec((tm, GMLP_WIDTH), lambda i, j=j: (i, j))
    const = lambda shape: pl.BlockSpec(shape, lambda i: (0,) * len(shape))
    tile = pl.BlockSpec((tm, GMLP_WIDTH), lambda i: (i, 0))
    piece = jax.ShapeDtypeStruct((S, GMLP_WIDTH), BF16)
    return pl.pallas_call(
        body, name="gmlp_bwd", grid=(nsteps,),
        in_specs=[col(0), col(1), col(2), col(0), const((1, GMLP_WIDTH)), const((4, CHUNK, CHUNK)),
                  const((CHUNK, GMLP_WIDTH)), const((GMLP_WIDTH, GMLP_WIDTH))],
        out_specs=[tile, tile, tile, const((4 * CHUNK, CHUNK)), const((CHUNK, GMLP_WIDTH)), const((1, GMLP_WIDTH))],
        out_shape=[piece, piece, piece, jax.ShapeDtypeStruct((4 * CHUNK, CHUNK), F32),
                   jax.ShapeDtypeStruct((CHUNK, GMLP_WIDTH), F32), jax.ShapeDtypeStruct((1, GMLP_WIDTH), F32)],
        compiler_params=_params(32, ("arbitrary",)),
    )(proj, proj, proj, dycat, vg, w_s, b2, bd)


def _band_mask():
    qi = lax.broadcasted_iota(jnp.int32, (2 * BLOCK, 2 * BLOCK), 0) % BLOCK
    ki = lax.broadcasted_iota(jnp.int32, (2 * BLOCK, 2 * BLOCK), 1)
    return ((ki < BLOCK) & (ki >= qi)) | ((ki >= BLOCK) & ((ki - BLOCK) <= qi))


def _first_block_bias(blk, blocks_per_class):
    kcol = lax.broadcasted_iota(jnp.int32, (1, 2 * BLOCK), 1)
    kill = jnp.where((blk & (blocks_per_class - 1)) == 0, NEG, 0.0)
    return jnp.where(kcol < BLOCK, kill, 0.0)


def _two_heads(q, lo):
    zero = jnp.zeros_like(q)
    return jnp.concatenate([jnp.where(lo, q, zero), jnp.where(lo, zero, q)], axis=0)


def _block_tokens(blk, d, S):
    if d == 1:
        return pl.ds(pl.multiple_of(blk * BLOCK, BLOCK), BLOCK)
    blocks_per_class = S // d // BLOCK
    r = lax.shift_right_logical(blk, blocks_per_class.bit_length() - 1)
    n = blk & (blocks_per_class - 1)
    return pl.ds(r + n * (BLOCK * d), BLOCK, stride=d)


def _padded_block(blk):
    return pl.ds(pl.multiple_of((blk + 1) * BLOCK, BLOCK), BLOCK)


def _for_blocks(n_blocks, unroll, fn):
    def group(g, carry):
        for u in range(unroll):
            fn(g * unroll + u)
        return carry
    lax.fori_loop(0, n_blocks // unroll, group, 0)


def _attn_fwd(proj, qg2, kg2, bd):
    S = proj.shape[0]
    npairs = ATTN_WIDTH // 128
    tn = 512

    def body(q_ref, k_ref, v_ref, g_ref, qg_ref, kg_ref, bd_ref, y_ref, att_ref, lse_ref, qn, kn, kc, vc):
        bdv = bd_ref[...]
        lo = lax.broadcasted_iota(jnp.int32, (BLOCK, 128), 1) < HEAD_DIM
        band_mask = _band_mask()
        kc[pl.ds(0, BLOCK), :] = jnp.zeros((BLOCK, 128), BF16)
        vc[pl.ds(0, BLOCK), :] = jnp.zeros((BLOCK, 128), BF16)

        def norm_step(i, carry):
            rows = pl.ds(pl.multiple_of(i * tn, tn), tn)
            qv = q_ref[rows, :]
            kv = k_ref[rows, :]
            qn[rows, :] = (qv * lax.rsqrt(_split_dot(qv * qv, bdv) * (1.0 / HEAD_DIM) + EPS)) * (qg_ref[...] * SCALE)
            kn[rows, :] = (kv * lax.rsqrt(_split_dot(kv * kv, bdv) * (1.0 / HEAD_DIM) + EPS)) * kg_ref[...]
            return carry
        lax.fori_loop(0, S // tn, norm_step, 0)

        def fill(blk, d):
            tokens = _block_tokens(blk, d, S)
            kc[_padded_block(blk), :] = kn[tokens, :].astype(BF16)
            vc[_padded_block(blk), :] = v_ref[tokens, :].astype(BF16)

        def block(blk, d):
            tokens = _block_tokens(blk, d, S)
            keys = pl.ds(pl.multiple_of(blk * BLOCK, BLOCK), 2 * BLOCK)
            q2 = _two_heads(qn[tokens, :].astype(BF16), lo)
            s = jnp.where(band_mask, _nt(q2, kc[keys, :]), NEG) + _first_block_bias(blk, S // d // BLOCK)
            m = jnp.max(s, axis=-1, keepdims=True)
            e = jnp.exp(s - m)
            l = jnp.sum(e, axis=-1, keepdims=True)
            o2 = jnp.dot(e.astype(BF16), vc[keys, :], preferred_element_type=F32) * (1.0 / l)
            lse2 = m + jnp.log(l)
            o = jnp.where(lo, o2[:BLOCK], o2[BLOCK:])
            lse = jnp.where(lo, lse2[:BLOCK], lse2[BLOCK:])
            if d > 1:
                la = lse_ref[tokens, :]
                mx = jnp.maximum(la, lse)
                wa, wb = jnp.exp(la - mx), jnp.exp(lse - mx)
                t = wa + wb
                o = (wa * att_ref[tokens, :] + wb * o) / t
                lse = mx + jnp.log(t)
            att_ref[tokens, :] = o
            lse_ref[tokens, :] = lse

        for d in DILATIONS:
            _for_blocks(S // BLOCK, 4, functools.partial(fill, d=d))
            _for_blocks(S // BLOCK, 8, functools.partial(block, d=d))

        def gate_step(i, carry):
            rows = pl.ds(pl.multiple_of(i * tn, tn), tn)
            silu, _ = _silu_parts(g_ref[rows, :])
            y_ref[rows, :] = (att_ref[rows, :] * silu).astype(BF16)
            return carry
        lax.fori_loop(0, S // tn, gate_step, 0)

    col = lambda j0: pl.BlockSpec((S, 128), lambda p, j0=j0: (0, j0 + p))
    const = lambda shape: pl.BlockSpec(shape, lambda p: (0,) * len(shape))
    out = pl.BlockSpec((S, 128), lambda p: (0, p))
    return pl.pallas_call(
        body, name="attn_fwd", grid=(npairs,),
        in_specs=[col(COL_AQ), col(COL_AK), col(COL_AV), col(COL_AG), const((1, 128)), const((1, 128)),
                  const((128, 128))],
        out_specs=[out, out, out],
        out_shape=[jax.ShapeDtypeStruct((S, ATTN_WIDTH), BF16), jax.ShapeDtypeStruct((S, ATTN_WIDTH), F32),
                   jax.ShapeDtypeStruct((S, ATTN_WIDTH), F32)],
        scratch_shapes=[pltpu.VMEM((S, 128), F32), pltpu.VMEM((S, 128), F32),
                        pltpu.VMEM((S + BLOCK, 128), BF16), pltpu.VMEM((S + BLOCK, 128), BF16)],
        compiler_params=_params(48, ("arbitrary",)),
    )(proj, proj, proj, proj, qg2, kg2, bd)


def _attn_bwd_prep(proj, dycat, att, bd):
    S = proj.shape[0]
    tm = 512

    def body(g_ref, dy_ref, att_ref, bd_ref, do_ref, dd_ref, dg_ref):
        silu, dsilu = _silu_parts(g_ref[...])
        dy = dy_ref[...]
        at = att_ref[...]
        do = dy * silu
        do_ref[...] = do
        dd_ref[...] = _split_dot(do * at, bd_ref[...])
        dg_ref[...] = (dy * at * dsilu).astype(BF16)

    tile = lambda j0: pl.BlockSpec((tm, 256), lambda i, j, j0=j0: (i, j0 + j))
    return pl.pallas_call(
        body, name="attn_bwd_prep", grid=(S // tm, ATTN_WIDTH // 256),
        in_specs=[tile(COL_AG // 2), tile(1), tile(0), pl.BlockSpec((256, 256), lambda i, j: (0, 0))],
        out_specs=[tile(0), tile(0), tile(0)],
        out_shape=[jax.ShapeDtypeStruct((S, ATTN_WIDTH), F32), jax.ShapeDtypeStruct((S, ATTN_WIDTH), F32),
                   jax.ShapeDtypeStruct((S, ATTN_WIDTH), BF16)],
        compiler_params=_params(32, ("arbitrary", "arbitrary")),
    )(proj, dycat, att, bd)


def _attn_bwd(proj, do, dd, lse, qg2, kg2, bd):
    S = proj.shape[0]
    npairs = ATTN_WIDTH // 128
    tn = 512

    def body(q_ref, k_ref, v_ref, do_ref, dd_ref, lse_ref, qg_ref, kg_ref, bd_ref,
             dq_ref, dk_ref, dv_ref, dqg_ref, dkg_ref,
             qn, kn, kc, vc, dk_own, dv_own, dk_prev, dv_prev, dqa, dka, dva):
        bdv = bd_ref[...]
        lo = lax.broadcasted_iota(jnp.int32, (BLOCK, 128), 1) < HEAD_DIM
        zeros_bf = jnp.zeros((BLOCK, 128), BF16)
        zeros_f = jnp.zeros((BLOCK, 128), F32)
        kc[pl.ds(0, BLOCK), :] = zeros_bf
        vc[pl.ds(0, BLOCK), :] = zeros_bf
        dk_prev[pl.ds(S, BLOCK), :] = zeros_f
        dv_prev[pl.ds(S, BLOCK), :] = zeros_f

        def norm_step(i, carry):
            rows = pl.ds(pl.multiple_of(i * tn, tn), tn)
            qv = q_ref[rows, :]
            kv = k_ref[rows, :]
            qn[rows, :] = (qv * lax.rsqrt(_split_dot(qv * qv, bdv) * (1.0 / HEAD_DIM) + EPS)) * (qg_ref[...] * SCALE)
            kn[rows, :] = (kv * lax.rsqrt(_split_dot(kv * kv, bdv) * (1.0 / HEAD_DIM) + EPS)) * kg_ref[...]
            return carry
        lax.fori_loop(0, S // tn, norm_step, 0)

        kt = lax.broadcasted_iota(jnp.int32, (2 * BLOCK, 2 * BLOCK), 0)
        qt = lax.broadcasted_iota(jnp.int32, (2 * BLOCK, 2 * BLOCK), 1) % BLOCK
        band_mask_t = ((kt < BLOCK) & (kt >= qt)) | ((kt >= BLOCK) & ((kt - BLOCK) <= qt))

        def per_query_row(t):
            tt = t.T
            return jnp.concatenate([tt[0:1, :], tt[HEAD_DIM:HEAD_DIM + 1, :]], axis=1)

        def fill(blk, d):
            tokens = _block_tokens(blk, d, S)
            kc[_padded_block(blk), :] = kn[tokens, :].astype(BF16)
            vc[_padded_block(blk), :] = v_ref[tokens, :].astype(BF16)

        def block(blk, d):
            tokens = _block_tokens(blk, d, S)
            own = pl.ds(pl.multiple_of(blk * BLOCK, BLOCK), BLOCK)
            keys = pl.ds(pl.multiple_of(blk * BLOCK, BLOCK), 2 * BLOCK)
            q2 = _two_heads(qn[tokens, :].astype(BF16), lo)
            do2 = _two_heads(do_ref[tokens, :].astype(BF16), lo)
            lse_row = per_query_row(lse_ref[tokens, :])
            dd_row = per_query_row(dd_ref[tokens, :])
            kb = kc[keys, :]
            vb = vc[keys, :]
            st = jnp.where(band_mask_t, _nt(kb, q2), NEG)
            kill = jnp.where((blk & (S // d // BLOCK - 1)) == 0, NEG, 0.0)
            st = jnp.concatenate([st[:BLOCK] + kill, st[BLOCK:]], axis=0)
            pt = jnp.exp(st - lse_row)
            dst = pt * (_nt(vb, do2) - dd_row)
            ptb = pt.astype(BF16)
            dstb = dst.astype(BF16)
            dv_band = jnp.dot(ptb, do2, preferred_element_type=F32)
            dk_band = jnp.dot(dstb, q2, preferred_element_type=F32)
            dv_prev[own, :] = dv_band[:BLOCK]
            dv_own[own, :] = dv_band[BLOCK:]
            dk_prev[own, :] = dk_band[:BLOCK]
            dk_own[own, :] = dk_band[BLOCK:]
            dq2 = _tn(dstb, kb)
            dq = jnp.where(lo, dq2[:BLOCK], dq2[BLOCK:])
            dqa[tokens, :] = dq if d == 1 else dqa[tokens, :] + dq

        def fold(blk, d):
            tokens = _block_tokens(blk, d, S)
            own = pl.ds(pl.multiple_of(blk * BLOCK, BLOCK), BLOCK)
            dk = dk_own[own, :] + dk_prev[_padded_block(blk), :]
            dv = dv_own[own, :] + dv_prev[_padded_block(blk), :]
            dka[tokens, :] = dk if d == 1 else dka[tokens, :] + dk
            dva[tokens, :] = dv if d == 1 else dva[tokens, :] + dv

        for d in DILATIONS:
            _for_blocks(S // BLOCK, 4, functools.partial(fill, d=d))
            _for_blocks(S // BLOCK, 8, functools.partial(block, d=d))
            _for_blocks(S // BLOCK, 4, functools.partial(fold, d=d))

        def out_step(i, carry):
            dqg, dkg = carry
            rows = pl.ds(pl.multiple_of(i * tn, tn), tn)
            qv = q_ref[rows, :]
            kv = k_ref[rows, :]
            rq = lax.rsqrt(_split_dot(qv * qv, bdv) * (1.0 / HEAD_DIM) + EPS)
            rk = lax.rsqrt(_split_dot(kv * kv, bdv) * (1.0 / HEAD_DIM) + EPS)
            qh = qv * rq
            kh = kv * rk
            dqs = dqa[rows, :] * SCALE
            dkn = dka[rows, :]
            aq = dqs * qg_ref[...]
            ak = dkn * kg_ref[...]
            dq_ref[rows, :] = (rq * (aq - qh * (_split_dot(aq * qh, bdv) * (1.0 / HEAD_DIM)))).astype(BF16)
            dk_ref[rows, :] = (rk * (ak - kh * (_split_dot(ak * kh, bdv) * (1.0 / HEAD_DIM)))).astype(BF16)
            dv_ref[rows, :] = dva[rows, :].astype(BF16)
            dqg = dqg + jnp.sum(dqs * qh, axis=0, keepdims=True)
            dkg = dkg + jnp.sum(dkn * kh, axis=0, keepdims=True)
            return dqg, dkg
        zero = jnp.zeros((1, 128), F32)
        dqg, dkg = lax.fori_loop(0, S // tn, out_step, (zero, zero))
        dqg_ref[0] = dqg
        dkg_ref[0] = dkg

    once = pl.Buffered(1)
    col = lambda j0: pl.BlockSpec((S, 128), lambda p, j0=j0: (0, j0 + p), pipeline_mode=once)
    const = lambda shape: pl.BlockSpec(shape, lambda p: (0,) * len(shape))
    out = pl.BlockSpec((S, 128), lambda p: (0, p))
    gain_out = pl.BlockSpec((1, 1, 128), lambda p: (p, 0, 0))
    piece = jax.ShapeDtypeStruct((S, ATTN_WIDTH), BF16)
    gains = jax.ShapeDtypeStruct((npairs, 1, 128), F32)
    f32buf = pltpu.VMEM((S, 128), F32)
    f32pad = pltpu.VMEM((S + BLOCK, 128), F32)
    bf16pad = pltpu.VMEM((S + BLOCK, 128), BF16)
    return pl.pallas_call(
        body, name="attn_bwd", grid=(npairs,),
        in_specs=[col(COL_AQ), col(COL_AK), col(COL_AV), col(0), col(0), col(0), const((1, 128)), const((1, 128)),
                  const((128, 128))],
        out_specs=[out, out, out, gain_out, gain_out],
        out_shape=[piece, piece, piece, gains, gains],
        scratch_shapes=[f32buf, f32buf, bf16pad, bf16pad, f32buf, f32buf, f32pad, f32pad, f32buf, f32buf, f32buf],
        compiler_params=_params(48, ("arbitrary",)),
    )(proj, proj, proj, do, dd, lse, qg2, kg2, bd)


def _mem_kv(mem, gain, wkv_bf, kg4, bd):
    def body(mem_ref, g_ref, w_ref, kg_ref, bd_ref, hm_ref, kraw_ref, mk_ref, mv_ref):
        mv_ = mem_ref[...]
        r = lax.rsqrt(jnp.mean(mv_ * mv_, axis=-1, keepdims=True) + EPS)
        hm = ((mv_ * r) * g_ref[...]).astype(BF16)
        hm_ref[...] = hm
        kv = jnp.dot(hm, w_ref[...], preferred_element_type=F32)
        kraw = kv[:, :MEM_WIDTH]
        kraw_ref[...] = kraw
        ms = _split_dot(kraw * kraw, bd_ref[...]) * (1.0 / HEAD_DIM)
        mk_ref[...] = (kraw * lax.rsqrt(ms + EPS)) * kg_ref[...]
        mv_ref[...] = kv[:, MEM_WIDTH:]

    sq = jax.ShapeDtypeStruct((MEM_LEN, MEM_WIDTH), F32)
    return pl.pallas_call(
        body, name="mem_kv",
        out_shape=[jax.ShapeDtypeStruct((MEM_LEN, D_MODEL), BF16), sq, sq, sq],
        compiler_params=_params(16),
    )(mem, gain, wkv_bf, kg4, bd)


def _mem_fwd(proj, mk, mv, qg4, bd):
    S = proj.shape[0]
    tm = 512

    def body(q_ref, g_ref, mk_ref, mv_ref, qg_ref, bd_ref, y_ref, om_ref):
        qv = q_ref[...]
        ms = _split_dot(qv * qv, bd_ref[...]) * (1.0 / HEAD_DIM)
        qs = (qv * lax.rsqrt(ms + EPS)) * (qg_ref[...] * SCALE)
        mkb = mk_ref[...].astype(BF16)
        mvb = mv_ref[...].astype(BF16)
        head = _head_index((tm, MEM_WIDTH))
        o = jnp.zeros((tm, MEM_WIDTH), F32)
        for h in range(4):
            s = _nt(jnp.where(head == h, qs, 0.0).astype(BF16), mkb)
            e = jnp.exp(s - jnp.max(s, axis=-1, keepdims=True))
            p = e * (1.0 / jnp.sum(e, axis=-1, keepdims=True))
            o = jnp.where(head == h, jnp.dot(p.astype(BF16), mvb, preferred_element_type=F32), o)
        om_ref[...] = o
        silu, _ = _silu_parts(g_ref[...])
        y_ref[...] = (o * silu).astype(BF16)

    col = lambda j: pl.BlockSpec((tm, MEM_WIDTH), lambda i, j=j: (i, j))
    const = lambda shape: pl.BlockSpec(shape, lambda i: (0,) * len(shape))
    tile = pl.BlockSpec((tm, MEM_WIDTH), lambda i: (i, 0))
    return pl.pallas_call(
        body, name="mem_fwd", grid=(S // tm,),
        in_specs=[col(11), col(12), const((MEM_LEN, MEM_WIDTH)), const((MEM_LEN, MEM_WIDTH)), const((1, MEM_WIDTH)),
                  const((MEM_WIDTH, MEM_WIDTH))],
        out_specs=[tile, tile],
        out_shape=[jax.ShapeDtypeStruct((S, MEM_WIDTH), BF16), jax.ShapeDtypeStruct((S, MEM_WIDTH), F32)],
        compiler_params=_params(24, ("arbitrary",)),
    )(proj, proj, mk, mv, qg4, bd)


def _mem_bwd(proj, dycat, om, mk, mv, qg4, bd):
    S = proj.shape[0]
    tm = 512

    def body(q_ref, g_ref, dy_ref, om_ref, mk_ref, mv_ref, qg_ref, bd_ref,
             dq_ref, dg_ref, dmk_ref, dmv_ref, dqg_ref):
        i = pl.program_id(0)

        @pl.when(i == 0)
        def _():
            dmk_ref[...] = jnp.zeros_like(dmk_ref)
            dmv_ref[...] = jnp.zeros_like(dmv_ref)
            dqg_ref[...] = jnp.zeros_like(dqg_ref)

        bdv = bd_ref[...]
        qv = q_ref[...]
        rq = lax.rsqrt(_split_dot(qv * qv, bdv) * (1.0 / HEAD_DIM) + EPS)
        qh = qv * rq
        qs = qh * (qg_ref[...] * SCALE)
        silu, dsilu = _silu_parts(g_ref[...])
        dy = dy_ref[...]
        o = om_ref[...]
        do = dy * silu
        dg_ref[...] = (dy * o * dsilu).astype(BF16)
        dd = _split_dot(do * o, bdv)
        mkb = mk_ref[...].astype(BF16)
        mvb = mv_ref[...].astype(BF16)
        head = _head_index((tm, MEM_WIDTH))
        dqs = jnp.zeros((tm, MEM_WIDTH), F32)
        for h in range(4):
            qhd = jnp.where(head == h, qs, 0.0).astype(BF16)
            doh = jnp.where(head == h, do, 0.0).astype(BF16)
            s = _nt(qhd, mkb)
            e = jnp.exp(s - jnp.max(s, axis=-1, keepdims=True))
            p = e * (1.0 / jnp.sum(e, axis=-1, keepdims=True))
            ds = p * (_nt(doh, mvb) - dd[:, h * HEAD_DIM:h * HEAD_DIM + 1])
            dsb = ds.astype(BF16)
            dmv_ref[...] += _tn(p.astype(BF16), doh)
            dmk_ref[...] += _tn(dsb, qhd)
            dqs = jnp.where(head == h, jnp.dot(dsb, mkb, preferred_element_type=F32), dqs)
        dqs = dqs * SCALE
        a = dqs * qg_ref[...]
        dq_ref[...] = (rq * (a - qh * (_split_dot(a * qh, bdv) * (1.0 / HEAD_DIM)))).astype(BF16)
        dqg_ref[...] += jnp.sum(dqs * qh, axis=0, keepdims=True)

    col = lambda j: pl.BlockSpec((tm, MEM_WIDTH), lambda i, j=j: (i, j))
    const = lambda shape: pl.BlockSpec(shape, lambda i: (0,) * len(shape))
    tile = pl.BlockSpec((tm, MEM_WIDTH), lambda i: (i, 0))
    piece = jax.ShapeDtypeStruct((S, MEM_WIDTH), BF16)
    sq = jax.ShapeDtypeStruct((MEM_LEN, MEM_WIDTH), F32)
    return pl.pallas_call(
        body, name="mem_bwd", grid=(S // tm,),
        in_specs=[col(11), col(12), col(3), tile, const((MEM_LEN, MEM_WIDTH)), const((MEM_LEN, MEM_WIDTH)),
                  const((1, MEM_WIDTH)), const((MEM_WIDTH, MEM_WIDTH))],
        out_specs=[tile, tile, const((MEM_LEN, MEM_WIDTH)), const((MEM_LEN, MEM_WIDTH)), const((1, MEM_WIDTH))],
        out_shape=[piece, piece, sq, sq, jax.ShapeDtypeStruct((1, MEM_WIDTH), F32)],
        compiler_params=_params(32, ("arbitrary",)),
    )(proj, proj, dycat, om, mk, mv, qg4, bd)


def _mem_kv_bwd(dmk, dmv, kraw, mem, gain, kg4, wkv_bf, hm_bf, bd):
    def body(dmk_ref, dmv_ref, kraw_ref, mem_ref, g_ref, kg_ref, w_ref, hm_ref, bd_ref, dw_ref, dg_ref, dkg_ref):
        bdv = bd_ref[...]
        kraw = kraw_ref[...]
        rk = lax.rsqrt(_split_dot(kraw * kraw, bdv) * (1.0 / HEAD_DIM) + EPS)
        kh = kraw * rk
        dmkv = dmk_ref[...]
        a = dmkv * kg_ref[...]
        dkraw = rk * (a - kh * (_split_dot(a * kh, bdv) * (1.0 / HEAD_DIM)))
        dkg_ref[...] = jnp.sum(dmkv * kh, axis=0, keepdims=True)
        dkv = jnp.concatenate([dkraw, dmv_ref[...]], axis=1).astype(BF16)
        dw = _tn(hm_ref[...], dkv).astype(BF16)
        rows_blk = D_MODEL // N_DEV
        for j in range(N_DEV):
            dw_ref[j] = dw[rows_blk * j:rows_blk * (j + 1)]
        dhm = _nt(dkv, w_ref[...])
        mv_ = mem_ref[...]
        r = lax.rsqrt(jnp.mean(mv_ * mv_, axis=-1, keepdims=True) + EPS)
        dg_ref[...] = jnp.sum(dhm * (mv_ * r), axis=0, keepdims=True)

    return pl.pallas_call(
        body, name="mem_kv_bwd",
        out_shape=[jax.ShapeDtypeStruct((N_DEV, D_MODEL // N_DEV, 2 * MEM_WIDTH), BF16),
                   jax.ShapeDtypeStruct((1, D_MODEL), F32), jax.ShapeDtypeStruct((1, MEM_WIDTH), F32)],
        compiler_params=_params(24),
    )(dmk, dmv, kraw, mem, gain, kg4, wkv_bf, hm_bf, bd)


def _out_loss(yg, ya, ym, x, tgt, wout_bf):
    S, D = x.shape
    tm = 256

    nsteps = S // tm
    rows_blk = D // N_DEV

    def body(yg_ref, ya_ref, ym_ref, x_ref, t_ref, w_ref, dout_ref, dycat_ref, dw_ref, loss_ref, acc_ref):
        i = pl.program_id(0)

        @pl.when(i == 0)
        def _():
            acc_ref[...] = jnp.zeros_like(acc_ref)
            loss_ref[...] = jnp.zeros_like(loss_ref)

        ycat = jnp.concatenate([yg_ref[...], ya_ref[...], ym_ref[...]], axis=1)
        w = w_ref[...]
        diff = (x_ref[...] + jnp.dot(ycat, w, preferred_element_type=F32)) - t_ref[...]
        loss_ref[...] += jnp.sum(diff * diff, axis=0, keepdims=True)
        dout = diff * (1.0 / D)
        dout_ref[...] = dout
        db = dout.astype(BF16)
        dycat_ref[...] = _nt(db, w)
        acc_ref[...] += _tn(ycat, db)

        @pl.when(i == nsteps - 1)
        def _():
            for j in range(N_DEV):
                dw_ref[j] = acc_ref[rows_blk * j:rows_blk * (j + 1), :].astype(BF16)

    tile = lambda w: pl.BlockSpec((tm, w), lambda i: (i, 0))
    const = lambda shape: pl.BlockSpec(shape, lambda i: (0,) * len(shape))
    return pl.pallas_call(
        body, name="out_loss", grid=(nsteps,),
        in_specs=[tile(GMLP_WIDTH), tile(ATTN_WIDTH), tile(MEM_WIDTH), tile(D), tile(D), const((D, D))],
        out_specs=[tile(D), tile(D), const((N_DEV, rows_blk, D)), const((1, D))],
        out_shape=[jax.ShapeDtypeStruct((S, D), F32), jax.ShapeDtypeStruct((S, D), F32),
                   jax.ShapeDtypeStruct((N_DEV, rows_blk, D), BF16), jax.ShapeDtypeStruct((1, D), F32)],
        scratch_shapes=[pltpu.VMEM((D, D), F32)],
        compiler_params=_params(40, ("arbitrary",)),
    )(yg, ya, ym, x, tgt, wout_bf)


def _piece_specs(pieces, tm):
    return [pl.BlockSpec((tm, p.shape[1]), lambda i: (i, 0)) for p in pieces]


def _in_bwd_dx(pieces, x, dout, gain, win_bf):
    S, D = x.shape
    N = win_bf.shape[1]
    tm = 256
    n = len(pieces)

    def body(*refs):
        piece_refs = refs[:n]
        x_ref, dout_ref, g_ref, w_ref, gx_ref, dg_ref = refs[n:]

        @pl.when(pl.program_id(0) == 0)
        def _():
            dg_ref[...] = jnp.zeros_like(dg_ref)

        dproj = jnp.concatenate([r[...] for r in piece_refs], axis=1)
        dh = _nt(dproj, w_ref[...])
        xv = x_ref[...]
        r = lax.rsqrt(jnp.mean(xv * xv, axis=-1, keepdims=True) + EPS)
        xh = xv * r
        a = dh * g_ref[...]
        gx_ref[...] = dout_ref[...] + r * (a - xh * jnp.mean(a * xh, axis=-1, keepdims=True))
        dg_ref[...] += jnp.sum(dh * xh, axis=0, keepdims=True)

    tile = pl.BlockSpec((tm, D), lambda i: (i, 0))
    const = lambda shape: pl.BlockSpec(shape, lambda i: (0,) * len(shape))
    return pl.pallas_call(
        body, name="in_bwd_dx", grid=(S // tm,),
        in_specs=_piece_specs(pieces, tm) + [tile, tile, const((1, D)), const((D, N))],
        out_specs=[tile, const((1, D))],
        out_shape=[jax.ShapeDtypeStruct((S, D), F32), jax.ShapeDtypeStruct((1, D), F32)],
        compiler_params=_params(40, ("arbitrary",)),
    )(*pieces, x, dout, gain, win_bf)


def _in_bwd_dw(pieces, h_bf):
    S, D = h_bf.shape
    N = sum(p.shape[1] for p in pieces)
    n_blk = N // N_DEV
    tm = 256
    n = len(pieces)
    nsteps = S // tm

    def body(*refs):
        piece_refs = refs[:n]
        h_ref, dw_ref, acc_ref = refs[n:]
        i = pl.program_id(0)

        @pl.when(i == 0)
        def _():
            acc_ref[...] = jnp.zeros_like(acc_ref)

        dproj = jnp.concatenate([r[...] for r in piece_refs], axis=1)
        acc_ref[...] += _tn(h_ref[...], dproj)

        @pl.when(i == nsteps - 1)
        def _():
            for j in range(N_DEV):
                dw_ref[j] = acc_ref[:, n_blk * j:n_blk * (j + 1)].astype(BF16)

    return pl.pallas_call(
        body, name="in_bwd_dw", grid=(nsteps,),
        in_specs=_piece_specs(pieces, tm) + [pl.BlockSpec((tm, D), lambda i: (i, 0))],
        out_specs=pl.BlockSpec((N_DEV, D, n_blk), lambda i: (0, 0, 0)),
        out_shape=jax.ShapeDtypeStruct((N_DEV, D, n_blk), BF16),
        scratch_shapes=[pltpu.VMEM((D, N), F32)],
        compiler_params=_params(48, ("arbitrary",)),
    )(*pieces, h_bf)


def _place():
    x, y, c = lax.axis_index("x"), lax.axis_index("y"), lax.axis_index("c")
    chips = [(1 - x, y), (x, 1 - y), (1 - x, 1 - y)]
    return x, y, c, chips


def _all_gather_exchange(srcs, outs, send_sems, recv_sems, local_sems):
    n = len(srcs)
    x, y, c, chips = _place()
    me, sibling = (x, y, c), (x, y, 1 - c)

    def rows(a, px, py, pc):
        m = srcs[a].shape[0]
        return outs[a].at[pl.ds((4 * px + 2 * py + pc) * m, m), :]

    def copy(a, k, block, to, src=None):
        return pltpu.make_async_remote_copy(
            src_ref=rows(a, *block) if src is None else src, dst_ref=rows(a, *block),
            send_sem=send_sems.at[a, k], recv_sem=recv_sems.at[a, k], device_id=to, device_id_type=MESH)

    mine = [pltpu.make_async_copy(srcs[a], rows(a, *me), local_sems.at[a]) for a in range(n)]
    for cp in mine:
        cp.start()
    first = []
    for a in range(n):
        first.append(copy(a, 0, me, sibling, src=srcs[a]))
        first += [copy(a, 1 + j, me, (*chip, c), src=srcs[a]) for j, chip in enumerate(chips)]
    for cp in first:
        cp.start()
    passed = []
    for j, chip in enumerate(chips):
        for a in range(n):
            copy(a, 1 + j, (*chip, c), me).wait_recv()
            fwd = copy(a, 4 + j, (*chip, c), sibling)
            fwd.start()
            passed.append(fwd)
    for a in range(n):
        copy(a, 0, sibling, me).wait_recv()
        for j, chip in enumerate(chips):
            copy(a, 4 + j, (*chip, 1 - c), me).wait_recv()
    for cp in first + passed:
        cp.wait_send()
    for cp in mine:
        cp.wait()


def _gather_weights(shards):
    n = len(shards)
    tr = 128

    def body(*refs):
        ins, outs, casts = refs[:n], refs[n:2 * n], refs[2 * n:3 * n]
        for a in range(n):
            def cast(i, carry, a=a):
                rows = pl.ds(pl.multiple_of(i * tr, tr), tr)
                casts[a][rows, :] = ins[a][rows, :].astype(BF16)
                return carry
            lax.fori_loop(0, ins[a].shape[0] // tr, cast, 0)
        _all_gather_exchange(casts, outs, *refs[3 * n:])

    vmem = pl.BlockSpec(memory_space=pltpu.VMEM)
    return pl.pallas_call(
        body, name="gather_weights",
        out_shape=[jax.ShapeDtypeStruct((N_DEV * a.shape[0], a.shape[1]), BF16) for a in shards],
        in_specs=[vmem] * n, out_specs=[vmem] * n,
        scratch_shapes=[pltpu.VMEM(a.shape, BF16) for a in shards]
        + [pltpu.SemaphoreType.DMA((n, 7)), pltpu.SemaphoreType.DMA((n, 7)), pltpu.SemaphoreType.DMA((n,))],
        compiler_params=_params(40),
    )(*shards)


ROW_NORM, ROW_MEM_NORM, ROW_V_GAIN, ROW_B, ROW_ATTN_GAINS, ROW_MEM_GAINS, ROW_W_S, ROW_LOSS = 0, 8, 16, 18, 22, 23, 24, 536
SMALL_ROWS = 544


def _gather_small(dgain, dmgain, dvg, db2, dqg, dkg, dmqg, dmkg, dws, sq):
    def body(dgain_ref, dmgain_ref, dvg_ref, db2_ref, dqg_ref, dkg_ref, dmqg_ref, dmkg_ref, dws_ref, sq_ref,
             out_ref, mine, send_sems, recv_sems, local_sems):
        first = lax.broadcasted_iota(jnp.int32, (1, 128), 1) < HEAD_DIM
        for i in range(8):
            cols = slice(128 * i, 128 * (i + 1))
            mine[ROW_NORM + i:ROW_NORM + i + 1, :] = dgain_ref[:, cols]
            mine[ROW_MEM_NORM + i:ROW_MEM_NORM + i + 1, :] = dmgain_ref[:, cols]
            mine[ROW_LOSS + i:ROW_LOSS + i + 1, :] = sq_ref[:, cols]
        mine[ROW_V_GAIN:ROW_V_GAIN + 1, :] = dvg_ref[:, 0:128]
        mine[ROW_V_GAIN + 1:ROW_V_GAIN + 2, :] = dvg_ref[:, 128:256]
        bt = db2_ref[...].T
        for h in range(4):
            mine[ROW_B + h:ROW_B + h + 1, :] = bt[HEAD_DIM * h:HEAD_DIM * h + 1, :]

        def fold_heads(t):
            return t + pltpu.roll(t, HEAD_DIM, axis=1)
        aq = fold_heads(dqg_ref[0] + dqg_ref[1] + dqg_ref[2] + dqg_ref[3])
        ak = fold_heads(dkg_ref[0] + dkg_ref[1] + dkg_ref[2] + dkg_ref[3])
        mine[ROW_ATTN_GAINS:ROW_ATTN_GAINS + 1, :] = jnp.where(first, aq, ak)
        mq = fold_heads(dmqg_ref[:, 0:128] + dmqg_ref[:, 128:256])
        mk = fold_heads(dmkg_ref[:, 0:128] + dmkg_ref[:, 128:256])
        mine[ROW_MEM_GAINS:ROW_MEM_GAINS + 1, :] = jnp.where(first, mq, mk)
        mine[ROW_W_S:ROW_W_S + 4 * CHUNK, :] = dws_ref[...]
        _all_gather_exchange([mine], [out_ref], send_sems, recv_sems, local_sems)

    return pl.pallas_call(
        body, name="gather_small_grads",
        out_shape=jax.ShapeDtypeStruct((N_DEV * SMALL_ROWS, 128), F32),
        scratch_shapes=[pltpu.VMEM((SMALL_ROWS, 128), F32), pltpu.SemaphoreType.DMA((1, 7)),
                        pltpu.SemaphoreType.DMA((1, 7)), pltpu.SemaphoreType.DMA((1,))],
        compiler_params=_params(16),
    )(dgain, dmgain, dvg, db2, dqg, dkg, dmqg, dmkg, dws, sq)


def _reduce_scatter(arrs, name):
    n = len(arrs)
    tr = 128

    def body(*refs):
        ins, outs = refs[:n], refs[n:2 * n]
        half, quarter = refs[2 * n:3 * n], refs[3 * n:4 * n]
        send_sems, recv_sems = refs[4 * n:]
        x, y, c, chips = _place()
        sibling = (x, y, 1 - c)

        to_sibling = [pltpu.make_async_remote_copy(
            src_ref=ins[a].at[2 * q + (1 - c)], dst_ref=half[a].at[q], send_sem=send_sems.at[a, q],
            recv_sem=recv_sems.at[a, q], device_id=sibling, device_id_type=MESH) for a in range(n) for q in range(4)]
        for cp in to_sibling:
            cp.start()
        for cp in to_sibling:
            cp.wait_recv()

        def add_rows(a, fn):
            m = ins[a].shape[1]
            def step(i, carry):
                fn(pl.ds(pl.multiple_of(i * tr, tr), tr))
                return carry
            lax.fori_loop(0, m // tr, step, 0)

        for a in range(n):
            for q in range(4):
                def add_half(rows, a=a, q=q):
                    both = ins[a][2 * q + c, rows, :].astype(F32) + half[a][q, rows, :].astype(F32)
                    half[a][q, rows, :] = both.astype(BF16)
                add_rows(a, add_half)

        to_chips = [pltpu.make_async_remote_copy(
            src_ref=half[a].at[2 * chip[0] + chip[1]], dst_ref=quarter[a].at[k], send_sem=send_sems.at[a, 4 + k],
            recv_sem=recv_sems.at[a, 4 + k], device_id=(*chip, c), device_id_type=MESH)
            for a in range(n) for k, chip in enumerate(chips)]
        for cp in to_chips:
            cp.start()
        for cp in to_chips:
            cp.wait_recv()
        for a in range(n):
            def add_quarters(rows, a=a):
                f = lambda t: t.astype(F32)
                outs[a][rows, :] = ((f(half[a][2 * x + y, rows, :]) + f(quarter[a][0, rows, :]))
                                    + (f(quarter[a][1, rows, :]) + f(quarter[a][2, rows, :])))
            add_rows(a, add_quarters)
        for cp in to_sibling + to_chips:
            cp.wait_send()

    vmem = pl.BlockSpec(memory_space=pltpu.VMEM)
    return pl.pallas_call(
        body, name=name,
        out_shape=[jax.ShapeDtypeStruct(a.shape[1:], F32) for a in arrs],
        in_specs=[vmem] * n, out_specs=[vmem] * n,
        scratch_shapes=[pltpu.VMEM((4,) + a.shape[1:], BF16) for a in arrs]
        + [pltpu.VMEM((3,) + a.shape[1:], BF16) for a in arrs]
        + [pltpu.SemaphoreType.DMA((n, 7)), pltpu.SemaphoreType.DMA((n, 7))],
        compiler_params=_params(48),
    )(*arrs)


def _adamw_math(w, g, m, v):
    m = ADAM_B1 * m + (1.0 - ADAM_B1) * g
    v = ADAM_B2 * v + (1.0 - ADAM_B2) * (g * g)
    m_hat = m / (1.0 - ADAM_B1 ** ADAM_STEP)
    v_hat = v / (1.0 - ADAM_B2 ** ADAM_STEP)
    delta = -ADAM_LR * (m_hat / (jnp.sqrt(v_hat) + ADAM_EPS) + ADAM_WD * w)
    return delta, m, v


def _adamw(w, g, m, v, name):
    R, C = w.shape
    tr = 128 if R % 128 == 0 else R

    def body(w_ref, g_ref, m_ref, v_ref, d_ref, nm_ref, nv_ref):
        d_ref[...], nm_ref[...], nv_ref[...] = _adamw_math(w_ref[...], g_ref[...], m_ref[...], v_ref[...])

    tile = pl.BlockSpec((tr, C), lambda i: (i, 0))
    out = jax.ShapeDtypeStruct((R, C), F32)
    return pl.pallas_call(
        body, name=name, grid=(R // tr,), in_specs=[tile] * 4, out_specs=[tile] * 3, out_shape=[out] * 3,
        compiler_params=_params(16, ("arbitrary",)),
    )(w, g, m, v)


SMALL = ("norm_gain", "gmlp_v_gain", "gmlp_w_s", "gmlp_b", "attn_q_gain", "attn_k_gain", "mem_norm_gain",
         "mem_q_gain", "mem_k_gain")
WEIGHTS = ("norm_gain", "w_in", "gmlp_v_gain", "gmlp_w_s", "gmlp_b", "attn_q_gain", "attn_k_gain",
           "mem_norm_gain", "w_mem_kv", "mem_q_gain", "mem_k_gain", "w_out")


def _adamw_small(w, m, v, g_all):
    k = len(SMALL)
    half = slice(0, HEAD_DIM), slice(HEAD_DIM, 2 * HEAD_DIM)

    def body(*refs):
        w_refs, m_refs, v_refs = refs[:k], refs[k:2 * k], refs[2 * k:3 * k]
        g_ref = refs[3 * k]
        outs = refs[3 * k + 1:7 * k + 1]
        loss_ref, gsum = refs[7 * k + 1:]

        part = SMALL_ROWS // 4
        for p in range(4):
            acc = g_ref[part * p:part * (p + 1), :]
            for dev in range(1, N_DEV):
                acc = acc + g_ref[dev * SMALL_ROWS + part * p:dev * SMALL_ROWS + part * (p + 1), :]
            gsum[part * p:part * (p + 1), :] = acc

        def update(name, at, g):
            i = SMALL.index(name)
            d, nm, nv = _adamw_math(w_refs[i][at], g, m_refs[i][at], v_refs[i][at])
            outs[i][at], outs[k + i][at], outs[2 * k + i][at], outs[3 * k + i][at] = g, d, nm, nv

        for i in range(8):
            at = (slice(0, 1), slice(128 * i, 128 * (i + 1)))
            update("norm_gain", at, gsum[ROW_NORM + i:ROW_NORM + i + 1, :])
            update("mem_norm_gain", at, gsum[ROW_MEM_NORM + i:ROW_MEM_NORM + i + 1, :])
        for h in range(4):
            row = (0, slice(h, h + 1), slice(None))
            update("gmlp_v_gain", row, gsum[ROW_V_GAIN + h // 2:ROW_V_GAIN + h // 2 + 1, half[h % 2]])
            update("gmlp_b", row, gsum[ROW_B + h:ROW_B + h + 1, :])
            update("gmlp_w_s", (0, h), gsum[ROW_W_S + CHUNK * h:ROW_W_S + CHUNK * (h + 1), :])
        whole = (slice(0, 1), slice(None))
        update("attn_q_gain", whole, gsum[ROW_ATTN_GAINS:ROW_ATTN_GAINS + 1, half[0]])
        update("attn_k_gain", whole, gsum[ROW_ATTN_GAINS:ROW_ATTN_GAINS + 1, half[1]])
        update("mem_q_gain", whole, gsum[ROW_MEM_GAINS:ROW_MEM_GAINS + 1, half[0]])
        update("mem_k_gain", whole, gsum[ROW_MEM_GAINS:ROW_MEM_GAINS + 1, half[1]])
        loss_ref[...] = jnp.sum(gsum[ROW_LOSS:ROW_LOSS + 8, :], keepdims=True) * (0.5 / D_MODEL)

    shapes = [jax.ShapeDtypeStruct(w[name].shape, F32) for name in SMALL]
    res = pl.pallas_call(
        body, name="adamw_small",
        out_shape=shapes * 4 + [jax.ShapeDtypeStruct((1, 1), F32)],
        scratch_shapes=[pltpu.VMEM((SMALL_ROWS, 128), F32)],
        compiler_params=_params(16),
    )(*[w[n] for n in SMALL], *[m[n] for n in SMALL], *[v[n] for n in SMALL], g_all)
    trees = [dict(zip(SMALL, res[j * k:(j + 1) * k])) for j in range(4)]
    return (*trees, res[4 * k])


def _local_grads(x, mem, tgt, w, win_blocks, wkv_bf, wout_bf):
    bd128, bd256 = _head_blockdiag(128), _head_blockdiag(256)
    gain = w["norm_gain"].reshape(1, D_MODEL)
    vg = w["gmlp_v_gain"].reshape(1, GMLP_WIDTH)
    w_s = w["gmlp_w_s"].reshape(4, CHUNK, CHUNK)
    b2 = jnp.repeat(w["gmlp_b"].reshape(4, CHUNK).T, HEAD_DIM, axis=1)
    qg2 = jnp.tile(w["attn_q_gain"].reshape(1, HEAD_DIM), (1, 2))
    kg2 = jnp.tile(w["attn_k_gain"].reshape(1, HEAD_DIM), (1, 2))
    mqg4 = jnp.tile(w["mem_q_gain"].reshape(1, HEAD_DIM), (1, 4))
    mkg4 = jnp.tile(w["mem_k_gain"].reshape(1, HEAD_DIM), (1, 4))
    mgain = w["mem_norm_gain"].reshape(1, D_MODEL)

    proj, h_bf, win_bf = _rms_proj(x, gain, win_blocks)
    yg = _gmlp_fwd(proj, vg, w_s, b2, bd256)
    ya, att, lse = _attn_fwd(proj, qg2, kg2, bd128)
    hm_bf, kraw, mk, mv = _mem_kv(mem, mgain, wkv_bf, mkg4, bd256)
    ym, om = _mem_fwd(proj, mk, mv, mqg4, bd256)
    dout, dycat, dwout, sq = _out_loss(yg, ya, ym, x, tgt, wout_bf)

    du, dgv, dgg, dws, db2, dvg = _gmlp_bwd(proj, dycat, vg, w_s, b2, bd256)
    do, dd, dag = _attn_bwd_prep(proj, dycat, att, bd256)
    dq, dk, dv, dqg, dkg = _attn_bwd(proj, do, dd, lse, qg2, kg2, bd128)
    dmq, dmg, dmk, dmv, dmqg = _mem_bwd(proj, dycat, om, mk, mv, mqg4, bd256)
    dwkv, dmgain, dmkg = _mem_kv_bwd(dmk, dmv, kraw, mem, mgain, mkg4, wkv_bf, hm_bf, bd256)
    pieces = [du, dgv, dgg, dq, dk, dv, dag, dmq, dmg]
    grad_x, dgain = _in_bwd_dx(pieces, x, dout, gain, win_bf)
    dwin = _in_bwd_dw(pieces, h_bf)
    return grad_x, dwin, dwkv, dwout, (dgain, dmgain, dvg, db2, dqg, dkg, dmqg, dmkg, dws, sq)


def kernel(x, mem, norm_gain, w_in, gmlp_v_gain, gmlp_w_s, gmlp_b, attn_q_gain, attn_k_gain, mem_norm_gain, w_mem_kv, mem_q_gain, mem_k_gain, w_out, loss_target, m_norm_gain, m_w_in, m_gmlp_v_gain, m_gmlp_w_s, m_gmlp_b, m_attn_q_gain, m_attn_k_gain, m_mem_norm_gain, m_w_mem_kv, m_mem_q_gain, m_mem_k_gain, m_w_out, v_norm_gain, v_w_in, v_gmlp_v_gain, v_gmlp_w_s, v_gmlp_b, v_attn_q_gain, v_attn_k_gain, v_mem_norm_gain, v_w_mem_kv, v_mem_q_gain, v_mem_k_gain, v_w_out):
    w = dict(norm_gain=norm_gain, w_in=w_in, gmlp_v_gain=gmlp_v_gain, gmlp_w_s=gmlp_w_s, gmlp_b=gmlp_b,
             attn_q_gain=attn_q_gain, attn_k_gain=attn_k_gain, mem_norm_gain=mem_norm_gain, w_mem_kv=w_mem_kv,
             mem_q_gain=mem_q_gain, mem_k_gain=mem_k_gain, w_out=w_out)
    m = dict(norm_gain=m_norm_gain, w_in=m_w_in, gmlp_v_gain=m_gmlp_v_gain, gmlp_w_s=m_gmlp_w_s, gmlp_b=m_gmlp_b,
             attn_q_gain=m_attn_q_gain, attn_k_gain=m_attn_k_gain, mem_norm_gain=m_mem_norm_gain,
             w_mem_kv=m_w_mem_kv, mem_q_gain=m_mem_q_gain, mem_k_gain=m_mem_k_gain, w_out=m_w_out)
    v = dict(norm_gain=v_norm_gain, w_in=v_w_in, gmlp_v_gain=v_gmlp_v_gain, gmlp_w_s=v_gmlp_w_s, gmlp_b=v_gmlp_b,
             attn_q_gain=v_attn_q_gain, attn_k_gain=v_attn_k_gain, mem_norm_gain=v_mem_norm_gain,
             w_mem_kv=v_w_mem_kv, mem_q_gain=v_mem_q_gain, mem_k_gain=v_mem_k_gain, w_out=v_w_out)
    n_in = w_in.shape[2]

    win_all, wkv_bf, wout_bf = _gather_weights([w_in[0], w_mem_kv[0], w_out[0]])
    win_blocks = win_all.reshape(N_DEV, D_MODEL, n_in)

    grad_x, dwin, dwkv, dwout, small = _local_grads(x[0], mem[0], loss_target[0], w, win_blocks, wkv_bf, wout_bf)

    (g_win,) = _reduce_scatter([dwin], "reduce_w_in")
    g_wkv, g_wout = _reduce_scatter([dwkv, dwout], "reduce_w_kv_out")
    small_all = _gather_small(*small)

    out_g, out_d, out_m, out_v, loss = _adamw_small(w, m, v, small_all)
    for name, g in (("w_in", g_win), ("w_mem_kv", g_wkv), ("w_out", g_wout)):
        d_, m_, v_ = _adamw(w[name][0], g, m[name][0], v[name][0], "adamw_" + name)
        out_g[name], out_d[name], out_m[name], out_v[name] = g[None], d_[None], m_[None], v_[None]

    return (loss.reshape(()), grad_x[None], *[out_g[k] for k in WEIGHTS], *[out_d[k] for k in WEIGHTS],
            *[out_m[k] for k in WEIGHTS], *[out_v[k] for k in WEIGHTS])
```

```python
import functools
import math

import jax
import jax.numpy as jnp
from jax import lax
from jax.experimental import pallas as pl
from jax.experimental.pallas import tpu as pltpu

F32 = jnp.float32
BF16 = jnp.bfloat16

N_DEV = 8
D_MODEL = 1024
HEAD_DIM = 64
GMLP_WIDTH = 256
ATTN_WIDTH = 512
MEM_WIDTH = 256
MEM_LEN = 256
IN_WIDTH = 3 * GMLP_WIDTH + 4 * ATTN_WIDTH + 2 * MEM_WIDTH
CHUNK = 128
BLOCK = 128
DILATIONS = (1, 4, 16)
EPS = 1e-6
SCALE = 1.0 / math.sqrt(HEAD_DIM)
NEG = -1e30

ADAM_LR = 0.001
ADAM_B1 = 0.9
ADAM_B2 = 0.999
ADAM_EPS = 1e-08
ADAM_WD = 0.01
ADAM_STEP = 10

MIB = 1024 * 1024
MESH = pl.DeviceIdType.MESH

COL_AQ, COL_AK, COL_AV, COL_AG = 6, 10, 14, 18


def _params(vmem_mib, semantics=None):
    kw = dict(vmem_limit_bytes=vmem_mib * MIB)
    if semantics is not None:
        kw["dimension_semantics"] = semantics
    return pltpu.CompilerParams(**kw)


def _hbm(*arrs):
    return [pltpu.with_memory_space_constraint(a, pltpu.HBM) for a in arrs]


def _split_dot(x, sel_bf):
    hi = x.astype(BF16)
    lo = (x - hi.astype(F32)).astype(BF16)
    return jnp.dot(hi, sel_bf, preferred_element_type=F32) + jnp.dot(lo, sel_bf, preferred_element_type=F32)


def _nt(a, b):
    return lax.dot_general(a, b, (((1,), (1,)), ((), ())), preferred_element_type=F32)


def _tn(a, b):
    return lax.dot_general(a, b, (((0,), (0,)), ((), ())), preferred_element_type=F32)


def _silu_parts(g):
    sg = jax.nn.sigmoid(g)
    return g * sg, sg * (1.0 + g * (1.0 - sg))


def _head_index(shape):
    return lax.shift_right_logical(lax.broadcasted_iota(jnp.int32, shape, 1), HEAD_DIM.bit_length() - 1)


def _head_blockdiag(width):
    i = jnp.arange(width) // HEAD_DIM
    return (i[:, None] == i[None, :]).astype(BF16)


def _rms_proj(x, gain, w_blocks):
    S, D = x.shape
    n_dev, _, n = w_blocks.shape
    N = n_dev * n
    tm = 256

    def body(x_ref, g_ref, wb_ref, proj_ref, h_ref, w_ref):
        @pl.when(pl.program_id(0) == 0)
        def _():
            for j in range(n_dev):
                w_ref[:, n * j:n * (j + 1)] = wb_ref[j]

        xv = x_ref[...]
        r = lax.rsqrt(jnp.mean(xv * xv, axis=-1, keepdims=True) + EPS)
        h = ((xv * r) * g_ref[...]).astype(BF16)
        h_ref[...] = h
        proj_ref[...] = jnp.dot(h, w_ref[...], preferred_element_type=F32)

    return pl.pallas_call(
        body, name="rms_proj", grid=(S // tm,),
        in_specs=[pl.BlockSpec((tm, D), lambda i: (i, 0)), pl.BlockSpec((1, D), lambda i: (0, 0)),
                  pl.BlockSpec((n_dev, D, n), lambda i: (0, 0, 0), pipeline_mode=pl.Buffered(1))],
        out_specs=[pl.BlockSpec((tm, N), lambda i: (i, 0)), pl.BlockSpec((tm, D), lambda i: (i, 0)),
                   pl.BlockSpec((D, N), lambda i: (0, 0))],
        out_shape=[pltpu.HBM((S, N), F32), pltpu.HBM((S, D), BF16), pltpu.HBM((D, N), BF16)],
        compiler_params=_params(48, ("arbitrary",)),
    )(*_hbm(x, gain, w_blocks))


def _gmlp_masked_weights(ws_ref, transpose):
    t = lax.broadcasted_iota(jnp.int32, (CHUNK, CHUNK), 0)
    s = lax.broadcasted_iota(jnp.int32, (CHUNK, CHUNK), 1)
    parts = []
    for h in range(4):
        wm = jnp.where(s <= t, ws_ref[h], 0.0)
        parts.append(wm.T if transpose else wm)
    return jnp.concatenate(parts, axis=1).astype(BF16)


def _head_stack(v, head):
    return jnp.concatenate([jnp.where(head == h, v, 0.0) for h in range(4)], axis=0).astype(BF16)


def _gmlp_fwd(proj, vg, w_s, b2, bd):
    S = proj.shape[0]
    tm = 512

    def body(u_ref, v_ref, g_ref, vg_ref, ws_ref, b2_ref, bd_ref, y_ref):
        v = v_ref[...]
        ms = _split_dot(v * v, bd_ref[...]) * (1.0 / HEAD_DIM)
        vn = (v * lax.rsqrt(ms + EPS)) * vg_ref[...]
        wcat = _gmlp_masked_weights(ws_ref, False)
        head = _head_index((CHUNK, GMLP_WIDTH))
        for c in range(tm // CHUNK):
            rows = slice(c * CHUNK, (c + 1) * CHUNK)
            sp = jnp.dot(wcat, _head_stack(vn[rows], head), preferred_element_type=F32) + b2_ref[...]
            silu, _ = _silu_parts(g_ref[rows, :])
            y_ref[rows, :] = ((u_ref[rows, :] * sp) * silu).astype(BF16)

    col = lambda j: pl.BlockSpec((tm, GMLP_WIDTH), lambda i, j=j: (i, j))
    const = lambda shape: pl.BlockSpec(shape, lambda i: (0,) * len(shape))
    return pl.pallas_call(
        body, name="gmlp_fwd", grid=(S // tm,),
        in_specs=[col(0), col(1), col(2), const((1, GMLP_WIDTH)), const((4, CHUNK, CHUNK)),
                  const((CHUNK, GMLP_WIDTH)), const((GMLP_WIDTH, GMLP_WIDTH))],
        out_specs=pl.BlockSpec((tm, GMLP_WIDTH), lambda i: (i, 0)),
        out_shape=pltpu.HBM((S, GMLP_WIDTH), BF16),
        compiler_params=_params(24, ("arbitrary",)),
    )(*_hbm(proj, proj, proj, vg, w_s, b2, bd))


def _gmlp_bwd(proj, dycat, vg, w_s, b2, bd):
    S = proj.shape[0]
    tm = 512
    nsteps = S // tm

    def body(u_ref, v_ref, g_ref, dy_ref, vg_ref, ws_ref, b2_ref, bd_ref,
             du_ref, dv_ref, dg_ref, dws_ref, db2_ref, dvg_ref):
        i = pl.program_id(0)

        @pl.when(i == 0)
        def _():
            dws_ref[...] = jnp.zeros_like(dws_ref)
            db2_ref[...] = jnp.zeros_like(db2_ref)
            dvg_ref[...] = jnp.zeros_like(dvg_ref)

        bdv = bd_ref[...]
        v = v_ref[...]
        ms = _split_dot(v * v, bdv) * (1.0 / HEAD_DIM)
        rv = lax.rsqrt(ms + EPS)
        xhat = v * rv
        vgv = vg_ref[...]
        vn = xhat * vgv
        wcat = _gmlp_masked_weights(ws_ref, False)
        wcat_t = _gmlp_masked_weights(ws_ref, True)
        head = _head_index((CHUNK, GMLP_WIDTH))
        dvg = jnp.zeros((1, GMLP_WIDTH), F32)
        for c in range(tm // CHUNK):
            rows = slice(c * CHUNK, (c + 1) * CHUNK)
            vn_c = vn[rows]
            spb = jnp.dot(wcat, _head_stack(vn_c, head), preferred_element_type=F32) + b2_ref[...]
            silu, dsilu = _silu_parts(g_ref[rows, :])
            dy = dy_ref[rows, :]
            u = u_ref[rows, :]
            du_ref[rows, :] = (dy * spb * silu).astype(BF16)
            dg_ref[rows, :] = (dy * u * spb * dsilu).astype(BF16)
            dsp = dy * u * silu
            db2_ref[...] += dsp
            dstack = _head_stack(dsp, head)
            dvn = jnp.dot(wcat_t, dstack, preferred_element_type=F32)
            dws_ref[...] += _nt(dstack, vn_c.astype(BF16))
            xh = xhat[rows]
            a = dvn * vgv
            mean_ax = _split_dot(a * xh, bdv) * (1.0 / HEAD_DIM)
            dv_ref[rows, :] = (rv[rows] * (a - xh * mean_ax)).astype(BF16)
            dvg = dvg + jnp.sum(dvn * xh, axis=0, keepdims=True)
        dvg_ref[...] += dvg

        @pl.when(i == nsteps - 1)
        def _():
            t = lax.broadcasted_iota(jnp.int32, (4 * CHUNK, CHUNK), 0) % CHUNK
            s = lax.broadcasted_iota(jnp.int32, (4 * CHUNK, CHUNK), 1)
            dws_ref[...] = jnp.where(s <= t, dws_ref[...], 0.0)
            db2_ref[...] = _split_dot(db2_ref[...], bdv)

    col = lambda j: pl.BlockSpec((tm, GMLP_WIDTH), lambda i, j=j: (i, j))
    const = lambda shape: pl.BlockSpec(shape, lambda i: (0,) * len(shape))
    tile = pl.BlockSpec((tm, GMLP_WIDTH), lambda i: (i, 0))
    piece = pltpu.HBM((S, GMLP_WIDTH), BF16)
    return pl.pallas_call(
        body, name="gmlp_bwd", grid=(nsteps,),
        in_specs=[col(0), col(1), col(2), col(0), const((1, GMLP_WIDTH)), const((4, CHUNK, CHUNK)),
                  const((CHUNK, GMLP_WIDTH)), const((GMLP_WIDTH, GMLP_WIDTH))],
        out_specs=[tile, tile, tile, const((4 * CHUNK, CHUNK)), const((CHUNK, GMLP_WIDTH)), const((1, GMLP_WIDTH))],
        out_shape=[piece, piece, piece, pltpu.HBM((4 * CHUNK, CHUNK), F32),
                   pltpu.HBM((CHUNK, GMLP_WIDTH), F32), pltpu.HBM((1, GMLP_WIDTH), F32)],
        compiler_params=_params(32, ("arbitrary",)),
    )(*_hbm(proj, proj, proj, dycat, vg, w_s, b2, bd))


def _band_mask():
    qi = lax.broadcasted_iota(jnp.int32, (2 * BLOCK, 2 * BLOCK), 0) % BLOCK
    ki = lax.broadcasted_iota(jnp.int32, (2 * BLOCK, 2 * BLOCK), 1)
    return ((ki < BLOCK) & (ki >= qi)) | ((ki >= BLOCK) & ((ki - BLOCK) <= qi))


def _first_block_bias(blk, blocks_per_class):
    kcol = lax.broadcasted_iota(jnp.int32, (1, 2 * BLOCK), 1)
    kill = jnp.where((blk & (blocks_per_class - 1)) == 0, NEG, 0.0)
    return jnp.where(kcol < BLOCK, kill, 0.0)


def _two_heads(q, lo):
    zero = jnp.zeros_like(q)
    return jnp.concatenate([jnp.where(lo, q, zero), jnp.where(lo, zero, q)], axis=0)


def _block_tokens(blk, d, S):
    if d == 1:
        return pl.ds(pl.multiple_of(blk * BLOCK, BLOCK), BLOCK)
    blocks_per_class = S // d // BLOCK
    r = lax.shift_right_logical(blk, blocks_per_class.bit_length() - 1)
    n = blk & (blocks_per_class - 1)
    return pl.ds(r + n * (BLOCK * d), BLOCK, stride=d)


def _padded_block(blk):
    return pl.ds(pl.multiple_of((blk + 1) * BLOCK, BLOCK), BLOCK)


def _for_blocks(n_blocks, unroll, fn):
    def group(g, carry):
        for u in range(unroll):
            fn(g * unroll + u)
        return carry
    lax.fori_loop(0, n_blocks // unroll, group, 0)


def _attn_fwd(proj, qg2, kg2, bd):
    S = proj.shape[0]
    npairs = ATTN_WIDTH // 128
    tn = 512

    def body(q_ref, k_ref, v_ref, g_ref, qg_ref, kg_ref, bd_ref, y_ref, att_ref, lse_ref, qn, kn, kc, vc):
        bdv = bd_ref[...]
        lo = lax.broadcasted_iota(jnp.int32, (BLOCK, 128), 1) < HEAD_DIM
        band_mask = _band_mask()
        kc[pl.ds(0, BLOCK), :] = jnp.zeros((BLOCK, 128), BF16)
        vc[pl.ds(0, BLOCK), :] = jnp.zeros((BLOCK, 128), BF16)

        def norm_step(i, carry):
            rows = pl.ds(pl.multiple_of(i * tn, tn), tn)
            qv = q_ref[rows, :]
            kv = k_ref[rows, :]
            qn[rows, :] = (qv * lax.rsqrt(_split_dot(qv * qv, bdv) * (1.0 / HEAD_DIM) + EPS)) * (qg_ref[...] * SCALE)
            kn[rows, :] = (kv * lax.rsqrt(_split_dot(kv * kv, bdv) * (1.0 / HEAD_DIM) + EPS)) * kg_ref[...]
            return carry
        lax.fori_loop(0, S // tn, norm_step, 0)

        def fill(blk, d):
            tokens = _block_tokens(blk, d, S)
            kc[_padded_block(blk), :] = kn[tokens, :].astype(BF16)
            vc[_padded_block(blk), :] = v_ref[tokens, :].astype(BF16)

        def block(blk, d):
            tokens = _block_tokens(blk, d, S)
            keys = pl.ds(pl.multiple_of(blk * BLOCK, BLOCK), 2 * BLOCK)
            q2 = _two_heads(qn[tokens, :].astype(BF16), lo)
            s = jnp.where(band_mask, _nt(q2, kc[keys, :]), NEG) + _first_block_bias(blk, S // d // BLOCK)
            m = jnp.max(s, axis=-1, keepdims=True)
            e = jnp.exp(s - m)
            l = jnp.sum(e, axis=-1, keepdims=True)
            o2 = jnp.dot(e.astype(BF16), vc[keys, :], preferred_element_type=F32) * (1.0 / l)
            lse2 = m + jnp.log(l)
            o = jnp.where(lo, o2[:BLOCK], o2[BLOCK:])
            lse = jnp.where(lo, lse2[:BLOCK], lse2[BLOCK:])
            if d > 1:
                la = lse_ref[tokens, :]
                mx = jnp.maximum(la, lse)
                wa, wb = jnp.exp(la - mx), jnp.exp(lse - mx)
                t = wa + wb
                o = (wa * att_ref[tokens, :] + wb * o) / t
                lse = mx + jnp.log(t)
            att_ref[tokens, :] = o
            lse_ref[tokens, :] = lse

        for d in DILATIONS:
            _for_blocks(S // BLOCK, 4, functools.partial(fill, d=d))
            _for_blocks(S // BLOCK, 8, functools.partial(block, d=d))

        def gate_step(i, carry):
            rows = pl.ds(pl.multiple_of(i * tn, tn), tn)
            silu, _ = _silu_parts(g_ref[rows, :])
            y_ref[rows, :] = (att_ref[rows, :] * silu).astype(BF16)
            return carry
        lax.fori_loop(0, S // tn, gate_step, 0)

    col = lambda j0: pl.BlockSpec((S, 128), lambda p, j0=j0: (0, j0 + p))
    const = lambda shape: pl.BlockSpec(shape, lambda p: (0,) * len(shape))
    out = pl.BlockSpec((S, 128), lambda p: (0, p))
    return pl.pallas_call(
        body, name="attn_fwd", grid=(npairs,),
        in_specs=[col(COL_AQ), col(COL_AK), col(COL_AV), col(COL_AG), const((1, 128)), const((1, 128)),
                  const((128, 128))],
        out_specs=[out, out, out],
        out_shape=[pltpu.HBM((S, ATTN_WIDTH), BF16), pltpu.HBM((S, ATTN_WIDTH), F32),
                   pltpu.HBM((S, ATTN_WIDTH), F32)],
        scratch_shapes=[pltpu.VMEM((S, 128), F32), pltpu.VMEM((S, 128), F32),
                        pltpu.VMEM((S + BLOCK, 128), BF16), pltpu.VMEM((S + BLOCK, 128), BF16)],
        compiler_params=_params(48, ("arbitrary",)),
    )(*_hbm(proj, proj, proj, proj, qg2, kg2, bd))


def _attn_bwd_prep(proj, dycat, att, bd):
    S = proj.shape[0]
    tm = 512

    def body(g_ref, dy_ref, att_ref, bd_ref, do_ref, dd_ref, dg_ref):
        silu, dsilu = _silu_parts(g_ref[...])
        dy = dy_ref[...]
        at = att_ref[...]
        do = dy * silu
        do_ref[...] = do
        dd_ref[...] = _split_dot(do * at, bd_ref[...])
        dg_ref[...] = (dy * at * dsilu).astype(BF16)

    tile = lambda j0: pl.BlockSpec((tm, 256), lambda i, j, j0=j0: (i, j0 + j))
    return pl.pallas_call(
        body, name="attn_bwd_prep", grid=(S // tm, ATTN_WIDTH // 256),
        in_specs=[tile(COL_AG // 2), tile(1), tile(0), pl.BlockSpec((256, 256), lambda i, j: (0, 0))],
        out_specs=[tile(0), tile(0), tile(0)],
        out_shape=[pltpu.HBM((S, ATTN_WIDTH), F32), pltpu.HBM((S, ATTN_WIDTH), F32),
                   pltpu.HBM((S, ATTN_WIDTH), BF16)],
        compiler_params=_params(32, ("arbitrary", "arbitrary")),
    )(*_hbm(proj, dycat, att, bd))


def _attn_bwd(proj, do, dd, lse, qg2, kg2, bd):
    S = proj.shape[0]
    npairs = ATTN_WIDTH // 128
    tn = 512

    def body(q_ref, k_ref, v_ref, do_ref, dd_ref, lse_ref, qg_ref, kg_ref, bd_ref,
             dq_ref, dk_ref, dv_ref, dqg_ref, dkg_ref,
             qn, kn, kc, vc, dk_own, dv_own, dk_prev, dv_prev, dqa, dka, dva):
        bdv = bd_ref[...]
        lo = lax.broadcasted_iota(jnp.int32, (BLOCK, 128), 1) < HEAD_DIM
        zeros_bf = jnp.zeros((BLOCK, 128), BF16)
        zeros_f = jnp.zeros((BLOCK, 128), F32)
        kc[pl.ds(0, BLOCK), :] = zeros_bf
        vc[pl.ds(0, BLOCK), :] = zeros_bf
        dk_prev[pl.ds(S, BLOCK), :] = zeros_f
        dv_prev[pl.ds(S, BLOCK), :] = zeros_f

        def norm_step(i, carry):
            rows = pl.ds(pl.multiple_of(i * tn, tn), tn)
            qv = q_ref[rows, :]
            kv = k_ref[rows, :]
            qn[rows, :] = (qv * lax.rsqrt(_split_dot(qv * qv, bdv) * (1.0 / HEAD_DIM) + EPS)) * (qg_ref[...] * SCALE)
            kn[rows, :] = (kv * lax.rsqrt(_split_dot(kv * kv, bdv) * (1.0 / HEAD_DIM) + EPS)) * kg_ref[...]
            return carry
        lax.fori_loop(0, S // tn, norm_step, 0)

        kt = lax.broadcasted_iota(jnp.int32, (2 * BLOCK, 2 * BLOCK), 0)
        qt = lax.broadcasted_iota(jnp.int32, (2 * BLOCK, 2 * BLOCK), 1) % BLOCK
        band_mask_t = ((kt < BLOCK) & (kt >= qt)) | ((kt >= BLOCK) & ((kt - BLOCK) <= qt))

        def per_query_row(t):
            tt = t.T
            return jnp.concatenate([tt[0:1, :], tt[HEAD_DIM:HEAD_DIM + 1, :]], axis=1)

        def fill(blk, d):
            tokens = _block_tokens(blk, d, S)
            kc[_padded_block(blk), :] = kn[tokens, :].astype(BF16)
            vc[_padded_block(blk), :] = v_ref[tokens, :].astype(BF16)

        def block(blk, d):
            tokens = _block_tokens(blk, d, S)
            own = pl.ds(pl.multiple_of(blk * BLOCK, BLOCK), BLOCK)
            keys = pl.ds(pl.multiple_of(blk * BLOCK, BLOCK), 2 * BLOCK)
            q2 = _two_heads(qn[tokens, :].astype(BF16), lo)
            do2 = _two_heads(do_ref[tokens, :].astype(BF16), lo)
            lse_row = per_query_row(lse_ref[tokens, :])
            dd_row = per_query_row(dd_ref[tokens, :])
            kb = kc[keys, :]
            vb = vc[keys, :]
            st = jnp.where(band_mask_t, _nt(kb, q2), NEG)
            kill = jnp.where((blk & (S // d // BLOCK - 1)) == 0, NEG, 0.0)
            st = jnp.concatenate([st[:BLOCK] + kill, st[BLOCK:]], axis=0)
            pt = jnp.exp(st - lse_row)
            dst = pt * (_nt(vb, do2) - dd_row)
            ptb = pt.astype(BF16)
            dstb = dst.astype(BF16)
            dv_band = jnp.dot(ptb, do2, preferred_element_type=F32)
            dk_band = jnp.dot(dstb, q2, preferred_element_type=F32)
            dv_prev[own, :] = dv_band[:BLOCK]
            dv_own[own, :] = dv_band[BLOCK:]
            dk_prev[own, :] = dk_band[:BLOCK]
            dk_own[own, :] = dk_band[BLOCK:]
            dq2 = _tn(dstb, kb)
            dq = jnp.where(lo, dq2[:BLOCK], dq2[BLOCK:])
            dqa[tokens, :] = dq if d == 1 else dqa[tokens, :] + dq

        def fold(blk, d):
            tokens = _block_tokens(blk, d, S)
            own = pl.ds(pl.multiple_of(blk * BLOCK, BLOCK), BLOCK)
            dk = dk_own[own, :] + dk_prev[_padded_block(blk), :]
            dv = dv_own[own, :] + dv_prev[_padded_block(blk), :]
            dka[tokens, :] = dk if d == 1 else dka[tokens, :] + dk
            dva[tokens, :] = dv if d == 1 else dva[tokens, :] + dv

        for d in DILATIONS:
            _for_blocks(S // BLOCK, 4, functools.partial(fill, d=d))
            _for_blocks(S // BLOCK, 8, functools.partial(block, d=d))
            _for_blocks(S // BLOCK, 4, functools.partial(fold, d=d))

        def out_step(i, carry):
            dqg, dkg = carry
            rows = pl.ds(pl.multiple_of(i * tn, tn), tn)
            qv = q_ref[rows, :]
            kv = k_ref[rows, :]
            rq = lax.rsqrt(_split_dot(qv * qv, bdv) * (1.0 / HEAD_DIM) + EPS)
            rk = lax.rsqrt(_split_dot(kv * kv, bdv) * (1.0 / HEAD_DIM) + EPS)
            qh = qv * rq
            kh = kv * rk
            dqs = dqa[rows, :] * SCALE
            dkn = dka[rows, :]
            aq = dqs * qg_ref[...]
            ak = dkn * kg_ref[...]
            dq_ref[rows, :] = (rq * (aq - qh * (_split_dot(aq * qh, bdv) * (1.0 / HEAD_DIM)))).astype(BF16)
            dk_ref[rows, :] = (rk * (ak - kh * (_split_dot(ak * kh, bdv) * (1.0 / HEAD_DIM)))).astype(BF16)
            dv_ref[rows, :] = dva[rows, :].astype(BF16)
            dqg = dqg + jnp.sum(dqs * qh, axis=0, keepdims=True)
            dkg = dkg + jnp.sum(dkn * kh, axis=0, keepdims=True)
            return dqg, dkg
        zero = jnp.zeros((1, 128), F32)
        dqg, dkg = lax.fori_loop(0, S // tn, out_step, (zero, zero))
        dqg_ref[0] = dqg
        dkg_ref[0] = dkg

    once = pl.Buffered(1)
    col = lambda j0: pl.BlockSpec((S, 128), lambda p, j0=j0: (0, j0 + p), pipeline_mode=once)
    const = lambda shape: pl.BlockSpec(shape, lambda p: (0,) * len(shape))
    out = pl.BlockSpec((S, 128), lambda p: (0, p))
    gain_out = pl.BlockSpec((1, 1, 128), lambda p: (p, 0, 0))
    piece = pltpu.HBM((S, ATTN_WIDTH), BF16)
    gains = pltpu.HBM((npairs, 1, 128), F32)
    f32buf = pltpu.VMEM((S, 128), F32)
    f32pad = pltpu.VMEM((S + BLOCK, 128), F32)
    bf16pad = pltpu.VMEM((S + BLOCK, 128), BF16)
    return pl.pallas_call(
        body, name="attn_bwd", grid=(npairs,),
        in_specs=[col(COL_AQ), col(COL_AK), col(COL_AV), col(0), col(0), col(0), const((1, 128)), const((1, 128)),
                  const((128, 128))],
        out_specs=[out, out, out, gain_out, gain_out],
        out_shape=[piece, piece, piece, gains, gains],
        scratch_shapes=[f32buf, f32buf, bf16pad, bf16pad, f32buf, f32buf, f32pad, f32pad, f32buf, f32buf, f32buf],
        compiler_params=_params(48, ("arbitrary",)),
    )(*_hbm(proj, proj, proj, do, dd, lse, qg2, kg2, bd))


def _mem_kv(mem, gain, wkv_bf, kg4, bd):
    def body(mem_ref, g_ref, w_ref, kg_ref, bd_ref, hm_ref, kraw_ref, mk_ref, mv_ref):
        mv_ = mem_ref[...]
        r = lax.rsqrt(jnp.mean(mv_ * mv_, axis=-1, keepdims=True) + EPS)
        hm = ((mv_ * r) * g_ref[...]).astype(BF16)
        hm_ref[...] = hm
        kv = jnp.dot(hm, w_ref[...], preferred_element_type=F32)
        kraw = kv[:, :MEM_WIDTH]
        kraw_ref[...] = kraw
        ms = _split_dot(kraw * kraw, bd_ref[...]) * (1.0 / HEAD_DIM)
        mk_ref[...] = (kraw * lax.rsqrt(ms + EPS)) * kg_ref[...]
        mv_ref[...] = kv[:, MEM_WIDTH:]

    sq = jax.ShapeDtypeStruct((MEM_LEN, MEM_WIDTH), F32)
    return pl.pallas_call(
        body, name="mem_kv",
        out_shape=[jax.ShapeDtypeStruct((MEM_LEN, D_MODEL), BF16), sq, sq, sq],
        compiler_params=_params(16),
    )(mem, gain, wkv_bf, kg4, bd)


def _mem_fwd(proj, mk, mv, qg4, bd):
    S = proj.shape[0]
    tm = 512

    def body(q_ref, g_ref, mk_ref, mv_ref, qg_ref, bd_ref, y_ref, om_ref):
        qv = q_ref[...]
        ms = _split_dot(qv * qv, bd_ref[...]) * (1.0 / HEAD_DIM)
        qs = (qv * lax.rsqrt(ms + EPS)) * (qg_ref[...] * SCALE)
        mkb = mk_ref[...].astype(BF16)
        mvb = mv_ref[...].astype(BF16)
        head = _head_index((tm, MEM_WIDTH))
        o = jnp.zeros((tm, MEM_WIDTH), F32)
        for h in range(4):
            s = _nt(jnp.where(head == h, qs, 0.0).astype(BF16), mkb)
            e = jnp.exp(s - jnp.max(s, axis=-1, keepdims=True))
            p = e * (1.0 / jnp.sum(e, axis=-1, keepdims=True))
            o = jnp.where(head == h, jnp.dot(p.astype(BF16), mvb, preferred_element_type=F32), o)
        om_ref[...] = o
        silu, _ = _silu_parts(g_ref[...])
        y_ref[...] = (o * silu).astype(BF16)

    col = lambda j: pl.BlockSpec((tm, MEM_WIDTH), lambda i, j=j: (i, j))
    const = lambda shape: pl.BlockSpec(shape, lambda i: (0,) * len(shape))
    tile = pl.BlockSpec((tm, MEM_WIDTH), lambda i: (i, 0))
    return pl.pallas_call(
        body, name="mem_fwd", grid=(S // tm,),
        in_specs=[col(11), col(12), const((MEM_LEN, MEM_WIDTH)), const((MEM_LEN, MEM_WIDTH)), const((1, MEM_WIDTH)),
                  const((MEM_WIDTH, MEM_WIDTH))],
        out_specs=[tile, tile],
        out_shape=[pltpu.HBM((S, MEM_WIDTH), BF16), pltpu.HBM((S, MEM_WIDTH), F32)],
        compiler_params=_params(24, ("arbitrary",)),
    )(*_hbm(proj, proj, mk, mv, qg4, bd))


def _mem_bwd(proj, dycat, om, mk, mv, qg4, bd):
    S = proj.shape[0]
    tm = 512

    def body(q_ref, g_ref, dy_ref, om_ref, mk_ref, mv_ref, qg_ref, bd_ref,
             dq_ref, dg_ref, dmk_ref, dmv_ref, dqg_ref):
        i = pl.program_id(0)

        @pl.when(i == 0)
        def _():
            dmk_ref[...] = jnp.zeros_like(dmk_ref)
            dmv_ref[...] = jnp.zeros_like(dmv_ref)
            dqg_ref[...] = jnp.zeros_like(dqg_ref)

        bdv = bd_ref[...]
        qv = q_ref[...]
        rq = lax.rsqrt(_split_dot(qv * qv, bdv) * (1.0 / HEAD_DIM) + EPS)
        qh = qv * rq
        qs = qh * (qg_ref[...] * SCALE)
        silu, dsilu = _silu_parts(g_ref[...])
        dy = dy_ref[...]
        o = om_ref[...]
        do = dy * silu
        dg_ref[...] = (dy * o * dsilu).astype(BF16)
        dd = _split_dot(do * o, bdv)
        mkb = mk_ref[...].astype(BF16)
        mvb = mv_ref[...].astype(BF16)
        head = _head_index((tm, MEM_WIDTH))
        dqs = jnp.zeros((tm, MEM_WIDTH), F32)
        for h in range(4):
            qhd = jnp.where(head == h, qs, 0.0).astype(BF16)
            doh = jnp.where(head == h, do, 0.0).astype(BF16)
            s = _nt(qhd, mkb)
            e = jnp.exp(s - jnp.max(s, axis=-1, keepdims=True))
            p = e * (1.0 / jnp.sum(e, axis=-1, keepdims=True))
            ds = p * (_nt(doh, mvb) - dd[:, h * HEAD_DIM:h * HEAD_DIM + 1])
            dsb = ds.astype(BF16)
            dmv_ref[...] += _tn(p.astype(BF16), doh)
            dmk_ref[...] += _tn(dsb, qhd)
            dqs = jnp.where(head == h, jnp.dot(dsb, mkb, preferred_element_type=F32), dqs)
        dqs = dqs * SCALE
        a = dqs * qg_ref[...]
        dq_ref[...] = (rq * (a - qh * (_split_dot(a * qh, bdv) * (1.0 / HEAD_DIM)))).astype(BF16)
        dqg_ref[...] += jnp.sum(dqs * qh, axis=0, keepdims=True)

    col = lambda j: pl.BlockSpec((tm, MEM_WIDTH), lambda i, j=j: (i, j))
    const = lambda shape: pl.BlockSpec(shape, lambda i: (0,) * len(shape))
    tile = pl.BlockSpec((tm, MEM_WIDTH), lambda i: (i, 0))
    piece = pltpu.HBM((S, MEM_WIDTH), BF16)
    sq = pltpu.HBM((MEM_LEN, MEM_WIDTH), F32)
    return pl.pallas_call(
        body, name="mem_bwd", grid=(S // tm,),
        in_specs=[col(11), col(12), col(3), tile, const((MEM_LEN, MEM_WIDTH)), const((MEM_LEN, MEM_WIDTH)),
                  const((1, MEM_WIDTH)), const((MEM_WIDTH, MEM_WIDTH))],
        out_specs=[tile, tile, const((MEM_LEN, MEM_WIDTH)), const((MEM_LEN, MEM_WIDTH)), const((1, MEM_WIDTH))],
        out_shape=[piece, piece, sq, sq, pltpu.HBM((1, MEM_WIDTH), F32)],
        compiler_params=_params(32, ("arbitrary",)),
    )(*_hbm(proj, proj, dycat, om, mk, mv, qg4, bd))


def _mem_kv_bwd(dmk, dmv, kraw, mem, gain, kg4, wkv_bf, hm_bf, bd):
    def body(dmk_ref, dmv_ref, kraw_ref, mem_ref, g_ref, kg_ref, w_ref, hm_ref, bd_ref, dw_ref, dg_ref, dkg_ref):
        bdv = bd_ref[...]
        kraw = kraw_ref[...]
        rk = lax.rsqrt(_split_dot(kraw * kraw, bdv) * (1.0 / HEAD_DIM) + EPS)
        kh = kraw * rk
        dmkv = dmk_ref[...]
        a = dmkv * kg_ref[...]
        dkraw = rk * (a - kh * (_split_dot(a * kh, bdv) * (1.0 / HEAD_DIM)))
        dkg_ref[...] = jnp.sum(dmkv * kh, axis=0, keepdims=True)
        dkv = jnp.concatenate([dkraw, dmv_ref[...]], axis=1).astype(BF16)
        dw = _tn(hm_ref[...], dkv).astype(BF16)
        rows_blk = D_MODEL // N_DEV
        for j in range(N_DEV):
            dw_ref[j] = dw[rows_blk * j:rows_blk * (j + 1)]
        dhm = _nt(dkv, w_ref[...])
        mv_ = mem_ref[...]
        r = lax.rsqrt(jnp.mean(mv_ * mv_, axis=-1, keepdims=True) + EPS)
        dg_ref[...] = jnp.sum(dhm * (mv_ * r), axis=0, keepdims=True)

    return pl.pallas_call(
        body, name="mem_kv_bwd",
        out_shape=[jax.ShapeDtypeStruct((N_DEV, D_MODEL // N_DEV, 2 * MEM_WIDTH), BF16),
                   jax.ShapeDtypeStruct((1, D_MODEL), F32), jax.ShapeDtypeStruct((1, MEM_WIDTH), F32)],
        compiler_params=_params(24),
    )(dmk, dmv, kraw, mem, gain, kg4, wkv_bf, hm_bf, bd)


def _out_loss(yg, ya, ym, x, tgt, wout_bf):
    S, D = x.shape
    tm = 256

    nsteps = S // tm
    rows_blk = D // N_DEV

    def body(yg_ref, ya_ref, ym_ref, x_ref, t_ref, w_ref, dout_ref, dycat_ref, dw_ref, loss_ref, acc_ref):
        i = pl.program_id(0)

        @pl.when(i == 0)
        def _():
            acc_ref[...] = jnp.zeros_like(acc_ref)
            loss_ref[...] = jnp.zeros_like(loss_ref)

        ycat = jnp.concatenate([yg_ref[...], ya_ref[...], ym_ref[...]], axis=1)
        w = w_ref[...]
        diff = (x_ref[...] + jnp.dot(ycat, w, preferred_element_type=F32)) - t_ref[...]
        loss_ref[...] += jnp.sum(diff * diff, axis=0, keepdims=True)
        dout = diff * (1.0 / D)
        dout_ref[...] = dout
        db = dout.astype(BF16)
        dycat_ref[...] = _nt(db, w)
        acc_ref[...] += _tn(ycat, db)

        @pl.when(i == nsteps - 1)
        def _():
            for j in range(N_DEV):
                dw_ref[j] = acc_ref[rows_blk * j:rows_blk * (j + 1), :].astype(BF16)

    tile = lambda w: pl.BlockSpec((tm, w), lambda i: (i, 0))
    const = lambda shape: pl.BlockSpec(shape, lambda i: (0,) * len(shape))
    return pl.pallas_call(
        body, name="out_loss", grid=(nsteps,),
        in_specs=[tile(GMLP_WIDTH), tile(ATTN_WIDTH), tile(MEM_WIDTH), tile(D), tile(D), const((D, D))],
        out_specs=[tile(D), tile(D), const((N_DEV, rows_blk, D)), const((1, D))],
        out_shape=[pltpu.HBM((S, D), F32), pltpu.HBM((S, D), F32),
                   pltpu.HBM((N_DEV, rows_blk, D), BF16), pltpu.HBM((1, D), F32)],
        scratch_shapes=[pltpu.VMEM((D, D), F32)],
        compiler_params=_params(40, ("arbitrary",)),
    )(*_hbm(yg, ya, ym, x, tgt, wout_bf))


def _piece_specs(pieces, tm):
    return [pl.BlockSpec((tm, p.shape[1]), lambda i: (i, 0)) for p in pieces]


def _in_bwd_dx(pieces, x, dout, gain, win_bf):
    S, D = x.shape
    N = win_bf.shape[1]
    tm = 256
    n = len(pieces)

    def body(*refs):
        piece_refs = refs[:n]
        x_ref, dout_ref, g_ref, w_ref, gx_ref, dg_ref = refs[n:]

        @pl.when(pl.program_id(0) == 0)
        def _():
            dg_ref[...] = jnp.zeros_like(dg_ref)

        dproj = jnp.concatenate([r[...] for r in piece_refs], axis=1)
        dh = _nt(dproj, w_ref[...])
        xv = x_ref[...]
        r = lax.rsqrt(jnp.mean(xv * xv, axis=-1, keepdims=True) + EPS)
        xh = xv * r
        a = dh * g_ref[...]
        gx_ref[...] = dout_ref[...] + r * (a - xh * jnp.mean(a * xh, axis=-1, keepdims=True))
        dg_ref[...] += jnp.sum(dh * xh, axis=0, keepdims=True)

    tile = pl.BlockSpec((tm, D), lambda i: (i, 0))
    const = lambda shape: pl.BlockSpec(shape, lambda i: (0,) * len(shape))
    return pl.pallas_call(
        body, name="in_bwd_dx", grid=(S // tm,),
        in_specs=_piece_specs(pieces, tm) + [tile, tile, const((1, D)), const((D, N))],
        out_specs=[tile, const((1, D))],
        out_shape=[pltpu.HBM((S, D), F32), pltpu.HBM((1, D), F32)],
        compiler_params=_params(40, ("arbitrary",)),
    )(*_hbm(*pieces, x, dout, gain, win_bf))


def _in_bwd_dw(pieces, h_bf):
    S, D = h_bf.shape
    N = sum(p.shape[1] for p in pieces)
    n_blk = N // N_DEV
    tm = 256
    n = len(pieces)
    nsteps = S // tm

    def body(*refs):
        piece_refs = refs[:n]
        h_ref, dw_ref, acc_ref = refs[n:]
        i = pl.program_id(0)

        @pl.when(i == 0)
        def _():
            acc_ref[...] = jnp.zeros_like(acc_ref)

        dproj = jnp.concatenate([r[...] for r in piece_refs], axis=1)
        acc_ref[...] += _tn(h_ref[...], dproj)

        @pl.when(i == nsteps - 1)
        def _():
            for j in range(N_DEV):
                dw_ref[j] = acc_ref[:, n_blk * j:n_blk * (j + 1)].astype(BF16)

    return pl.pallas_call(
        body, name="in_bwd_dw", grid=(nsteps,),
        in_specs=_piece_specs(pieces, tm) + [pl.BlockSpec((tm, D), lambda i: (i, 0))],
        out_specs=pl.BlockSpec((N_DEV, D, n_blk), lambda i: (0, 0, 0)),
        out_shape=pltpu.HBM((N_DEV, D, n_blk), BF16),
        scratch_shapes=[pltpu.VMEM((D, N), F32)],
        compiler_params=_params(48, ("arbitrary",)),
    )(*_hbm(*pieces, h_bf))


def _place():
    x, y, c = lax.axis_index("x"), lax.axis_index("y"), lax.axis_index("c")
    chips = [(1 - x, y), (x, 1 - y), (1 - x, 1 - y)]
    return x, y, c, chips


def _all_gather_exchange(srcs, outs, send_sems, recv_sems, local_sems):
    n = len(srcs)
    x, y, c, chips = _place()
    me, sibling = (x, y, c), (x, y, 1 - c)

    def rows(a, px, py, pc):
        m = srcs[a].shape[0]
        return outs[a].at[pl.ds((4 * px + 2 * py + pc) * m, m), :]

    def copy(a, k, block, to, src=None):
        return pltpu.make_async_remote_copy(
            src_ref=rows(a, *block) if src is None else src, dst_ref=rows(a, *block),
            send_sem=send_sems.at[a, k], recv_sem=recv_sems.at[a, k], device_id=to, device_id_type=MESH)

    mine = [pltpu.make_async_copy(srcs[a], rows(a, *me), local_sems.at[a]) for a in range(n)]
    for cp in mine:
        cp.start()
    first = []
    for a in range(n):
        first.append(copy(a, 0, me, sibling, src=srcs[a]))
        first += [copy(a, 1 + j, me, (*chip, c), src=srcs[a]) for j, chip in enumerate(chips)]
    for cp in first:
        cp.start()
    passed = []
    for j, chip in enumerate(chips):
        for a in range(n):
            copy(a, 1 + j, (*chip, c), me).wait_recv()
            fwd = copy(a, 4 + j, (*chip, c), sibling)
            fwd.start()
            passed.append(fwd)
    for a in range(n):
        copy(a, 0, sibling, me).wait_recv()
        for j, chip in enumerate(chips):
            copy(a, 4 + j, (*chip, 1 - c), me).wait_recv()
    for cp in first + passed:
        cp.wait_send()
    for cp in mine:
        cp.wait()


def _gather_weights(shards):
    n = len(shards)
    tr = 128

    def body(*refs):
        ins, outs, casts = refs[:n], refs[n:2 * n], refs[2 * n:3 * n]
        for a in range(n):
            def cast(i, carry, a=a):
                rows = pl.ds(pl.multiple_of(i * tr, tr), tr)
                casts[a][rows, :] = ins[a][rows, :].astype(BF16)
                return carry
            lax.fori_loop(0, ins[a].shape[0] // tr, cast, 0)
        _all_gather_exchange(casts, outs, *refs[3 * n:])

    vmem = pl.BlockSpec(memory_space=pltpu.VMEM)
    return pl.pallas_call(
        body, name="gather_weights",
        out_shape=[jax.ShapeDtypeStruct((N_DEV * a.shape[0], a.shape[1]), BF16) for a in shards],
        in_specs=[vmem] * n, out_specs=[vmem] * n,
        scratch_shapes=[pltpu.VMEM(a.shape, BF16) for a in shards]
        + [pltpu.SemaphoreType.DMA((n, 7)), pltpu.SemaphoreType.DMA((n, 7)), pltpu.SemaphoreType.DMA((n,))],
        compiler_params=_params(40),
    )(*shards)


ROW_NORM, ROW_MEM_NORM, ROW_V_GAIN, ROW_B, ROW_ATTN_GAINS, ROW_MEM_GAINS, ROW_W_S, ROW_LOSS = 0, 8, 16, 18, 22, 23, 24, 536
SMALL_ROWS = 544


def _gather_small(dgain, dmgain, dvg, db2, dqg, dkg, dmqg, dmkg, dws, sq):
    def body(dgain_ref, dmgain_ref, dvg_ref, db2_ref, dqg_ref, dkg_ref, dmqg_ref, dmkg_ref, dws_ref, sq_ref,
             out_ref, mine, send_sems, recv_sems, local_sems):
        first = lax.broadcasted_iota(jnp.int32, (1, 128), 1) < HEAD_DIM
        for i in range(8):
            cols = slice(128 * i, 128 * (i + 1))
            mine[ROW_NORM + i:ROW_NORM + i + 1, :] = dgain_ref[:, cols]
            mine[ROW_MEM_NORM + i:ROW_MEM_NORM + i + 1, :] = dmgain_ref[:, cols]
            mine[ROW_LOSS + i:ROW_LOSS + i + 1, :] = sq_ref[:, cols]
        mine[ROW_V_GAIN:ROW_V_GAIN + 1, :] = dvg_ref[:, 0:128]
        mine[ROW_V_GAIN + 1:ROW_V_GAIN + 2, :] = dvg_ref[:, 128:256]
        bt = db2_ref[...].T
        for h in range(4):
            mine[ROW_B + h:ROW_B + h + 1, :] = bt[HEAD_DIM * h:HEAD_DIM * h + 1, :]

        def fold_heads(t):
            return t + pltpu.roll(t, HEAD_DIM, axis=1)
        aq = fold_heads(dqg_ref[0] + dqg_ref[1] + dqg_ref[2] + dqg_ref[3])
        ak = fold_heads(dkg_ref[0] + dkg_ref[1] + dkg_ref[2] + dkg_ref[3])
        mine[ROW_ATTN_GAINS:ROW_ATTN_GAINS + 1, :] = jnp.where(first, aq, ak)
        mq = fold_heads(dmqg_ref[:, 0:128] + dmqg_ref[:, 128:256])
        mk = fold_heads(dmkg_ref[:, 0:128] + dmkg_ref[:, 128:256])
        mine[ROW_MEM_GAINS:ROW_MEM_GAINS + 1, :] = jnp.where(first, mq, mk)
        mine[ROW_W_S:ROW_W_S + 4 * CHUNK, :] = dws_ref[...]
        _all_gather_exchange([mine], [out_ref], send_sems, recv_sems, local_sems)

    return pl.pallas_call(
        body, name="gather_small_grads",
        out_shape=jax.ShapeDtypeStruct((N_DEV * SMALL_ROWS, 128), F32),
        scratch_shapes=[pltpu.VMEM((SMALL_ROWS, 128), F32), pltpu.SemaphoreType.DMA((1, 7)),
                        pltpu.SemaphoreType.DMA((1, 7)), pltpu.SemaphoreType.DMA((1,))],
        compiler_params=_params(16),
    )(dgain, dmgain, dvg, db2, dqg, dkg, dmqg, dmkg, dws, sq)


def _reduce_scatter(arrs, name):
    n = len(arrs)
    tr = 128

    def body(*refs):
        ins, outs = refs[:n], refs[n:2 * n]
        half, quarter = refs[2 * n:3 * n], refs[3 * n:4 * n]
        send_sems, recv_sems = refs[4 * n:]
        x, y, c, chips = _place()
        sibling = (x, y, 1 - c)

        to_sibling = [pltpu.make_async_remote_copy(
            src_ref=ins[a].at[2 * q + (1 - c)], dst_ref=half[a].at[q], send_sem=send_sems.at[a, q],
            recv_sem=recv_sems.at[a, q], device_id=sibling, device_id_type=MESH) for a in range(n) for q in range(4)]
        for cp in to_sibling:
            cp.start()
        for cp in to_sibling:
            cp.wait_recv()

        def add_rows(a, fn):
            m = ins[a].shape[1]
            def step(i, carry):
                fn(pl.ds(pl.multiple_of(i * tr, tr), tr))
                return carry
            lax.fori_loop(0, m // tr, step, 0)

        for a in range(n):
            for q in range(4):
                def add_half(rows, a=a, q=q):
                    both = ins[a][2 * q + c, rows, :].astype(F32) + half[a][q, rows, :].astype(F32)
                    half[a][q, rows, :] = both.astype(BF16)
                add_rows(a, add_half)

        to_chips = [pltpu.make_async_remote_copy(
            src_ref=half[a].at[2 * chip[0] + chip[1]], dst_ref=quarter[a].at[k], send_sem=send_sems.at[a, 4 + k],
            recv_sem=recv_sems.at[a, 4 + k], device_id=(*chip, c), device_id_type=MESH)
            for a in range(n) for k, chip in enumerate(chips)]
        for cp in to_chips:
            cp.start()
        for cp in to_chips:
            cp.wait_recv()
        for a in range(n):
            def add_quarters(rows, a=a):
                f = lambda t: t.astype(F32)
                outs[a][rows, :] = ((f(half[a][2 * x + y, rows, :]) + f(quarter[a][0, rows, :]))
                                    + (f(quarter[a][1, rows, :]) + f(quarter[a][2, rows, :])))
            add_rows(a, add_quarters)
        for cp in to_sibling + to_chips:
            cp.wait_send()

    vmem = pl.BlockSpec(memory_space=pltpu.VMEM)
    return pl.pallas_call(
        body, name=name,
        out_shape=[jax.ShapeDtypeStruct(a.shape[1:], F32) for a in arrs],
        in_specs=[vmem] * n, out_specs=[vmem] * n,
        scratch_shapes=[pltpu.VMEM((4,) + a.shape[1:], BF16) for a in arrs]
        + [pltpu.VMEM((3,) + a.shape[1:], BF16) for a in arrs]
        + [pltpu.SemaphoreType.DMA((n, 7)), pltpu.SemaphoreType.DMA((n, 7))],
        compiler_params=_params(48),
    )(*arrs)


def _adamw_math(w, g, m, v):
    m = ADAM_B1 * m + (1.0 - ADAM_B1) * g
    v = ADAM_B2 * v + (1.0 - ADAM_B2) * (g * g)
    m_hat = m / (1.0 - ADAM_B1 ** ADAM_STEP)
    v_hat = v / (1.0 - ADAM_B2 ** ADAM_STEP)
    delta = -ADAM_LR * (m_hat / (jnp.sqrt(v_hat) + ADAM_EPS) + ADAM_WD * w)
    return delta, m, v


def _adamw(w, g, m, v, name):
    R, C = w.shape
    tr = 128 if R % 128 == 0 else R

    def body(w_ref, g_ref, m_ref, v_ref, d_ref, nm_ref, nv_ref):
        d_ref[...], nm_ref[...], nv_ref[...] = _adamw_math(w_ref[...], g_ref[...], m_ref[...], v_ref[...])

    tile = pl.BlockSpec((tr, C), lambda i: (i, 0))
    out = pltpu.HBM((R, C), F32)
    return pl.pallas_call(
        body, name=name, grid=(R // tr,), in_specs=[tile] * 4, out_specs=[tile] * 3, out_shape=[out] * 3,
        compiler_params=_params(16, ("arbitrary",)),
    )(*_hbm(w, g, m, v))


SMALL = ("norm_gain", "gmlp_v_gain", "gmlp_w_s", "gmlp_b", "attn_q_gain", "attn_k_gain", "mem_norm_gain",
         "mem_q_gain", "mem_k_gain")
WEIGHTS = ("norm_gain", "w_in", "gmlp_v_gain", "gmlp_w_s", "gmlp_b", "attn_q_gain", "attn_k_gain",
           "mem_norm_gain", "w_mem_kv", "mem_q_gain", "mem_k_gain", "w_out")


def _adamw_small(w, m, v, g_all):
    k = len(SMALL)
    half = slice(0, HEAD_DIM), slice(HEAD_DIM, 2 * HEAD_DIM)

    def body(*refs):
        w_refs, m_refs, v_refs = refs[:k], refs[k:2 * k], refs[2 * k:3 * k]
        g_ref = refs[3 * k]
        outs = refs[3 * k + 1:7 * k + 1]
        loss_ref, gsum = refs[7 * k + 1:]

        part = SMALL_ROWS // 4
        for p in range(4):
            acc = g_ref[part * p:part * (p + 1), :]
            for dev in range(1, N_DEV):
                acc = acc + g_ref[dev * SMALL_ROWS + part * p:dev * SMALL_ROWS + part * (p + 1), :]
            gsum[part * p:part * (p + 1), :] = acc

        def update(name, at, g):
            i = SMALL.index(name)
            d, nm, nv = _adamw_math(w_refs[i][at], g, m_refs[i][at], v_refs[i][at])
            outs[i][at], outs[k + i][at], outs[2 * k + i][at], outs[3 * k + i][at] = g, d, nm, nv

        for i in range(8):
            at = (slice(0, 1), slice(128 * i, 128 * (i + 1)))
            update("norm_gain", at, gsum[ROW_NORM + i:ROW_NORM + i + 1, :])
            update("mem_norm_gain", at, gsum[ROW_MEM_NORM + i:ROW_MEM_NORM + i + 1, :])
        for h in range(4):
            row = (0, slice(h, h + 1), slice(None))
            update("gmlp_v_gain", row, gsum[ROW_V_GAIN + h // 2:ROW_V_GAIN + h // 2 + 1, half[h % 2]])
            update("gmlp_b", row, gsum[ROW_B + h:ROW_B + h + 1, :])
            update("gmlp_w_s", (0, h), gsum[ROW_W_S + CHUNK * h:ROW_W_S + CHUNK * (h + 1), :])
        whole = (slice(0, 1), slice(None))
        update("attn_q_gain", whole, gsum[ROW_ATTN_GAINS:ROW_ATTN_GAINS + 1, half[0]])
        update("attn_k_gain", whole, gsum[ROW_ATTN_GAINS:ROW_ATTN_GAINS + 1, half[1]])
        update("mem_q_gain", whole, gsum[ROW_MEM_GAINS:ROW_MEM_GAINS + 1, half[0]])
        update("mem_k_gain", whole, gsum[ROW_MEM_GAINS:ROW_MEM_GAINS + 1, half[1]])
        loss_ref[...] = jnp.sum(gsum[ROW_LOSS:ROW_LOSS + 8, :], keepdims=True) * (0.5 / D_MODEL)

    shapes = [jax.ShapeDtypeStruct(w[name].shape, F32) for name in SMALL]
    res = pl.pallas_call(
        body, name="adamw_small",
        out_shape=shapes * 4 + [jax.ShapeDtypeStruct((1, 1), F32)],
        scratch_shapes=[pltpu.VMEM((SMALL_ROWS, 128), F32)],
        compiler_params=_params(16),
    )(*[w[n] for n in SMALL], *[m[n] for n in SMALL], *[v[n] for n in SMALL], g_all)
    trees = [dict(zip(SMALL, res[j * k:(j + 1) * k])) for j in range(4)]
    return (*trees, res[4 * k])


def _local_grads(x, mem, tgt, w, win_blocks, wkv_bf, wout_bf):
    bd128, bd256 = _head_blockdiag(128), _head_blockdiag(256)
    gain = w["norm_gain"].reshape(1, D_MODEL)
    vg = w["gmlp_v_gain"].reshape(1, GMLP_WIDTH)
    w_s = w["gmlp_w_s"].reshape(4, CHUNK, CHUNK)
    b2 = jnp.repeat(w["gmlp_b"].reshape(4, CHUNK).T, HEAD_DIM, axis=1)
    qg2 = jnp.tile(w["attn_q_gain"].reshape(1, HEAD_DIM), (1, 2))
    kg2 = jnp.tile(w["attn_k_gain"].reshape(1, HEAD_DIM), (1, 2))
    mqg4 = jnp.tile(w["mem_q_gain"].reshape(1, HEAD_DIM), (1, 4))
    mkg4 = jnp.tile(w["mem_k_gain"].reshape(1, HEAD_DIM), (1, 4))
    mgain = w["mem_norm_gain"].reshape(1, D_MODEL)

    proj, h_bf, win_bf = _rms_proj(x, gain, win_blocks)
    yg = _gmlp_fwd(proj, vg, w_s, b2, bd256)
    ya, att, lse = _attn_fwd(proj, qg2, kg2, bd128)
    hm_bf, kraw, mk, mv = _mem_kv(mem, mgain, wkv_bf, mkg4, bd256)
    ym, om = _mem_fwd(proj, mk, mv, mqg4, bd256)
    dout, dycat, dwout, sq = _out_loss(yg, ya, ym, x, tgt, wout_bf)

    du, dgv, dgg, dws, db2, dvg = _gmlp_bwd(proj, dycat, vg, w_s, b2, bd256)
    do, dd, dag = _attn_bwd_prep(proj, dycat, att, bd256)
    dq, dk, dv, dqg, dkg = _attn_bwd(proj, do, dd, lse, qg2, kg2, bd128)
    dmq, dmg, dmk, dmv, dmqg = _mem_bwd(proj, dycat, om, mk, mv, mqg4, bd256)
    dwkv, dmgain, dmkg = _mem_kv_bwd(dmk, dmv, kraw, mem, mgain, mkg4, wkv_bf, hm_bf, bd256)
    pieces = [du, dgv, dgg, dq, dk, dv, dag, dmq, dmg]
    grad_x, dgain = _in_bwd_dx(pieces, x, dout, gain, win_bf)
    dwin = _in_bwd_dw(pieces, h_bf)
    return grad_x, dwin, dwkv, dwout, (dgain, dmgain, dvg, db2, dqg, dkg, dmqg, dmkg, dws, sq)


def kernel(x, mem, norm_gain, w_in, gmlp_v_gain, gmlp_w_s, gmlp_b, attn_q_gain, attn_k_gain, mem_norm_gain, w_mem_kv, mem_q_gain, mem_k_gain, w_out, loss_target, m_norm_gain, m_w_in, m_gmlp_v_gain, m_gmlp_w_s, m_gmlp_b, m_attn_q_gain, m_attn_k_gain, m_mem_norm_gain, m_w_mem_kv, m_mem_q_gain, m_mem_k_gain, m_w_out, v_norm_gain, v_w_in, v_gmlp_v_gain, v_gmlp_w_s, v_gmlp_b, v_attn_q_gain, v_attn_k_gain, v_mem_norm_gain, v_w_mem_kv, v_mem_q_gain, v_mem_k_gain, v_w_out):
    w = dict(norm_gain=norm_gain, w_in=w_in, gmlp_v_gain=gmlp_v_gain, gmlp_w_s=gmlp_w_s, gmlp_b=gmlp_b,
             attn_q_gain=attn_q_gain, attn_k_gain=attn_k_gain, mem_norm_gain=mem_norm_gain, w_mem_kv=w_mem_kv,
             mem_q_gain=mem_q_gain, mem_k_gain=mem_k_gain, w_out=w_out)
    m = dict(norm_gain=m_norm_gain, w_in=m_w_in, gmlp_v_gain=m_gmlp_v_gain, gmlp_w_s=m_gmlp_w_s, gmlp_b=m_gmlp_b,
             attn_q_gain=m_attn_q_gain, attn_k_gain=m_attn_k_gain, mem_norm_gain=m_mem_norm_gain,
             w_mem_kv=m_w_mem_kv, mem_q_gain=m_mem_q_gain, mem_k_gain=m_mem_k_gain, w_out=m_w_out)
    v = dict(norm_gain=v_norm_gain, w_in=v_w_in, gmlp_v_gain=v_gmlp_v_gain, gmlp_w_s=v_gmlp_w_s, gmlp_b=v_gmlp_b,
             attn_q_gain=v_attn_q_gain, attn_k_gain=v_attn_k_gain, mem_norm_gain=v_mem_norm_gain,
             w_mem_kv=v_w_mem_kv, mem_q_gain=v_mem_q_gain, mem_k_gain=v_mem_k_gain, w_out=v_w_out)
    n_in = w_in.shape[2]

    win_all, wkv_bf, wout_bf = _gather_weights([w_in[0], w_mem_kv[0], w_out[0]])
    win_blocks = win_all.reshape(N_DEV, D_MODEL, n_in)

    grad_x, dwin, dwkv, dwout, small = _local_grads(x[0], mem[0], loss_target[0], w, win_blocks, wkv_bf, wout_bf)

    (g_win,) = _reduce_scatter([dwin], "reduce_w_in")
    g_wkv, g_wout = _reduce_scatter([dwkv, dwout], "reduce_w_kv_out")
    small_all = _gather_small(*small)

    out_g, out_d, out_m, out_v, loss = _adamw_small(w, m, v, small_all)
    for name, g in (("w_in", g_win), ("w_mem_kv", g_wkv), ("w_out", g_wout)):
        d_, m_, v_ = _adamw(w[name][0], g, m[name][0], v[name][0], "adamw_" + name)
        out_g[name], out_d[name], out_m[name], out_v[name] = g[None], d_[None], m_[None], v_[None]

    return (loss.reshape(()), grad_x[None], *[out_g[k] for k in WEIGHTS], *[out_d[k] for k in WEIGHTS],
            *[out_m[k] for k in WEIGHTS], *[out_v[k] for k in WEIGHTS])
```

```python
import functools
import math

import jax
import jax.numpy as jnp
from jax import lax
from jax.experimental import pallas as pl
from jax.experimental.pallas import tpu as pltpu

F32 = jnp.float32
BF16 = jnp.bfloat16

N_DEV = 8
D_MODEL = 1024
HEAD_DIM = 64
GMLP_WIDTH = 256
ATTN_WIDTH = 512
MEM_WIDTH = 256
MEM_LEN = 256
IN_WIDTH = 3 * GMLP_WIDTH + 4 * ATTN_WIDTH + 2 * MEM_WIDTH
CHUNK = 128
BLOCK = 128
DILATIONS = (1, 4, 16)
EPS = 1e-6
SCALE = 1.0 / math.sqrt(HEAD_DIM)
NEG = -1e30

ADAM_LR = 0.001
ADAM_B1 = 0.9
ADAM_B2 = 0.999
ADAM_EPS = 1e-08
ADAM_WD = 0.01
ADAM_STEP = 10

MIB = 1024 * 1024
MESH = pl.DeviceIdType.MESH

COL_AQ, COL_AK, COL_AV, COL_AG = 6, 10, 14, 18


def _params(vmem_mib, semantics=None):
    kw = dict(vmem_limit_bytes=vmem_mib * MIB)
    if semantics is not None:
        kw["dimension_semantics"] = semantics
    return pltpu.CompilerParams(**kw)


def _hbm(*arrs):
    return [pltpu.with_memory_space_constraint(a, pltpu.HBM) for a in arrs]


def _split_dot(x, sel_bf):
    hi = x.astype(BF16)
    lo = (x - hi.astype(F32)).astype(BF16)
    return jnp.dot(hi, sel_bf, preferred_element_type=F32) + jnp.dot(lo, sel_bf, preferred_element_type=F32)


def _nt(a, b):
    return lax.dot_general(a, b, (((1,), (1,)), ((), ())), preferred_element_type=F32)


def _tn(a, b):
    return lax.dot_general(a, b, (((0,), (0,)), ((), ())), preferred_element_type=F32)


def _silu_parts(g):
    sg = jax.nn.sigmoid(g)
    return g * sg, sg * (1.0 + g * (1.0 - sg))


def _head_index(shape):
    return lax.shift_right_logical(lax.broadcasted_iota(jnp.int32, shape, 1), HEAD_DIM.bit_length() - 1)


def _head_blockdiag(width):
    i = jnp.arange(width) // HEAD_DIM
    return (i[:, None] == i[None, :]).astype(BF16)


def _rms_proj(x, gain, w_t):
    S, D = x.shape
    N = w_t.shape[0]
    tm = 256

    def body(x_ref, g_ref, w_ref, proj_ref, h_ref):
        xv = x_ref[...]
        r = lax.rsqrt(jnp.mean(xv * xv, axis=-1, keepdims=True) + EPS)
        h = ((xv * r) * g_ref[...]).astype(BF16)
        h_ref[...] = h
        proj_ref[...] = _nt(h, w_ref[...])

    return pl.pallas_call(
        body, name="rms_proj", grid=(S // tm,),
        in_specs=[pl.BlockSpec((tm, D), lambda i: (i, 0)), pl.BlockSpec((1, D), lambda i: (0, 0)),
                  pl.BlockSpec((N, D), lambda i: (0, 0))],
        out_specs=[pl.BlockSpec((tm, N), lambda i: (i, 0)), pl.BlockSpec((tm, D), lambda i: (i, 0))],
        out_shape=[pltpu.HBM((S, N), F32), pltpu.HBM((S, D), BF16)],
        compiler_params=_params(40, ("arbitrary",)),
    )(*_hbm(x, gain, w_t))


def _gmlp_masked_weights(ws_ref, transpose):
    t = lax.broadcasted_iota(jnp.int32, (CHUNK, CHUNK), 0)
    s = lax.broadcasted_iota(jnp.int32, (CHUNK, CHUNK), 1)
    parts = []
    for h in range(4):
        wm = jnp.where(s <= t, ws_ref[h], 0.0)
        parts.append(wm.T if transpose else wm)
    return jnp.concatenate(parts, axis=1).astype(BF16)


def _head_stack(v, head):
    return jnp.concatenate([jnp.where(head == h, v, 0.0) for h in range(4)], axis=0).astype(BF16)


def _gmlp_fwd(proj, vg, w_s, b2, bd):
    S = proj.shape[0]
    tm = 512

    def body(u_ref, v_ref, g_ref, vg_ref, ws_ref, b2_ref, bd_ref, y_ref):
        v = v_ref[...]
        ms = _split_dot(v * v, bd_ref[...]) * (1.0 / HEAD_DIM)
        vn = (v * lax.rsqrt(ms + EPS)) * vg_ref[...]
        wcat = _gmlp_masked_weights(ws_ref, False)
        head = _head_index((CHUNK, GMLP_WIDTH))
        for c in range(tm // CHUNK):
            rows = slice(c * CHUNK, (c + 1) * CHUNK)
            sp = jnp.dot(wcat, _head_stack(vn[rows], head), preferred_element_type=F32) + b2_ref[...]
            silu, _ = _silu_parts(g_ref[rows, :])
            y_ref[rows, :] = ((u_ref[rows, :] * sp) * silu).astype(BF16)

    col = lambda j: pl.BlockSpec((tm, GMLP_WIDTH), lambda i, j=j: (i, j))
    const = lambda shape: pl.BlockSpec(shape, lambda i: (0,) * len(shape))
    return pl.pallas_call(
        body, name="gmlp_fwd", grid=(S // tm,),
        in_specs=[col(0), col(1), col(2), const((1, GMLP_WIDTH)), const((4, CHUNK, CHUNK)),
                  const((CHUNK, GMLP_WIDTH)), const((GMLP_WIDTH, GMLP_WIDTH))],
        out_specs=pl.BlockSpec((tm, GMLP_WIDTH), lambda i: (i, 0)),
        out_shape=pltpu.HBM((S, GMLP_WIDTH), BF16),
        compiler_params=_params(24, ("arbitrary",)),
    )(*_hbm(proj, proj, proj, vg, w_s, b2, bd))


def _gmlp_bwd(proj, dycat, vg, w_s, b2, bd):
    S = proj.shape[0]
    tm = 512
    nsteps = S // tm

    def body(u_ref, v_ref, g_ref, dy_ref, vg_ref, ws_ref, b2_ref, bd_ref,
             du_ref, dv_ref, dg_ref, dws_ref, db2_ref, dvg_ref):
        i = pl.program_id(0)

        @pl.when(i == 0)
        def _():
            dws_ref[...] = jnp.zeros_like(dws_ref)
            db2_ref[...] = jnp.zeros_like(db2_ref)
            dvg_ref[...] = jnp.zeros_like(dvg_ref)

        bdv = bd_ref[...]
        v = v_ref[...]
        ms = _split_dot(v * v, bdv) * (1.0 / HEAD_DIM)
        rv = lax.rsqrt(ms + EPS)
        xhat = v * rv
        vgv = vg_ref[...]
        vn = xhat * vgv
        wcat = _gmlp_masked_weights(ws_ref, False)
        wcat_t = _gmlp_masked_weights(ws_ref, True)
        head = _head_index((CHUNK, GMLP_WIDTH))
        dvg = jnp.zeros((1, GMLP_WIDTH), F32)
        for c in range(tm // CHUNK):
            rows = slice(c * CHUNK, (c + 1) * CHUNK)
            vn_c = vn[rows]
            spb = jnp.dot(wcat, _head_stack(vn_c, head), preferred_element_type=F32) + b2_ref[...]
            silu, dsilu = _silu_parts(g_ref[rows, :])
            dy = dy_ref[rows, :]
            u = u_ref[rows, :]
            du_ref[rows, :] = (dy * spb * silu).astype(BF16)
            dg_ref[rows, :] = (dy * u * spb * dsilu).astype(BF16)
            dsp = dy * u * silu
            db2_ref[...] += dsp
            dstack = _head_stack(dsp, head)
            dvn = jnp.dot(wcat_t, dstack, preferred_element_type=F32)
            dws_ref[...] += _nt(dstack, vn_c.astype(BF16))
            xh = xhat[rows]
            a = dvn * vgv
            mean_ax = _split_dot(a * xh, bdv) * (1.0 / HEAD_DIM)
            dv_ref[rows, :] = (rv[rows] * (a - xh * mean_ax)).astype(BF16)
            dvg = dvg + jnp.sum(dvn * xh, axis=0, keepdims=True)
        dvg_ref[...] += dvg

        @pl.when(i == nsteps - 1)
        def _():
            t = lax.broadcasted_iota(jnp.int32, (4 * CHUNK, CHUNK), 0) % CHUNK
            s = lax.broadcasted_iota(jnp.int32, (4 * CHUNK, CHUNK), 1)
            dws_ref[...] = jnp.where(s <= t, dws_ref[...], 0.0)
            db2_ref[...] = _split_dot(db2_ref[...], bdv)

    col = lambda j: pl.BlockSpec((tm, GMLP_WIDTH), lambda i, j=j: (i, j))
    const = lambda shape: pl.BlockSpec(shape, lambda i: (0,) * len(shape))
    tile = pl.BlockSpec((tm, GMLP_WIDTH), lambda i: (i, 0))
    piece = pltpu.HBM((S, GMLP_WIDTH), BF16)
    return pl.pallas_call(
        body, name="gmlp_bwd", grid=(nsteps,),
        in_specs=[col(0), col(1), col(2), col(0), const((1, GMLP_WIDTH)), const((4, CHUNK, CHUNK)),
                  const((CHUNK, GMLP_WIDTH)), const((GMLP_WIDTH, GMLP_WIDTH))],
        out_specs=[tile, tile, tile, const((4 * CHUNK, CHUNK)), const((CHUNK, GMLP_WIDTH)), const((1, GMLP_WIDTH))],
        out_shape=[piece, piece, piece, pltpu.HBM((4 * CHUNK, CHUNK), F32),
                   pltpu.HBM((CHUNK, GMLP_WIDTH), F32), pltpu.HBM((1, GMLP_WIDTH), F32)],
        compiler_params=_params(32, ("arbitrary",)),
    )(*_hbm(proj, proj, proj, dycat, vg, w_s, b2, bd))


def _band_mask():
    qi = lax.broadcasted_iota(jnp.int32, (2 * BLOCK, 2 * BLOCK), 0) % BLOCK
    ki = lax.broadcasted_iota(jnp.int32, (2 * BLOCK, 2 * BLOCK), 1)
    return ((ki < BLOCK) & (ki >= qi)) | ((ki >= BLOCK) & ((ki - BLOCK) <= qi))


def _first_block_bias(blk, blocks_per_class):
    kcol = lax.broadcasted_iota(jnp.int32, (1, 2 * BLOCK), 1)
    kill = jnp.where((blk & (blocks_per_class - 1)) == 0, NEG, 0.0)
    return jnp.where(kcol < BLOCK, kill, 0.0)


def _two_heads(q, lo):
    zero = jnp.zeros_like(q)
    return jnp.concatenate([jnp.where(lo, q, zero), jnp.where(lo, zero, q)], axis=0)


def _block_tokens(blk, d, S):
    if d == 1:
        return pl.ds(pl.multiple_of(blk * BLOCK, BLOCK), BLOCK)
    blocks_per_class = S // d // BLOCK
    r = lax.shift_right_logical(blk, blocks_per_class.bit_length() - 1)
    n = blk & (blocks_per_class - 1)
    return pl.ds(r + n * (BLOCK * d), BLOCK, stride=d)


def _padded_block(blk):
    return pl.ds(pl.multiple_of((blk + 1) * BLOCK, BLOCK), BLOCK)


def _for_blocks(n_blocks, unroll, fn):
    def group(g, carry):
        for u in range(unroll):
            fn(g * unroll + u)
        return carry
    lax.fori_loop(0, n_blocks // unroll, group, 0)


def _attn_fwd(proj, qg2, kg2, bd):
    S = proj.shape[0]
    npairs = ATTN_WIDTH // 128
    tn = 512

    def body(q_ref, k_ref, v_ref, g_ref, qg_ref, kg_ref, bd_ref, y_ref, att_ref, lse_ref, qn, kn, kc, vc):
        bdv = bd_ref[...]
        lo = lax.broadcasted_iota(jnp.int32, (BLOCK, 128), 1) < HEAD_DIM
        band_mask = _band_mask()
        kc[pl.ds(0, BLOCK), :] = jnp.zeros((BLOCK, 128), BF16)
        vc[pl.ds(0, BLOCK), :] = jnp.zeros((BLOCK, 128), BF16)

        def norm_step(i, carry):
            rows = pl.ds(pl.multiple_of(i * tn, tn), tn)
            qv = q_ref[rows, :]
            kv = k_ref[rows, :]
            qn[rows, :] = (qv * lax.rsqrt(_split_dot(qv * qv, bdv) * (1.0 / HEAD_DIM) + EPS)) * (qg_ref[...] * SCALE)
            kn[rows, :] = (kv * lax.rsqrt(_split_dot(kv * kv, bdv) * (1.0 / HEAD_DIM) + EPS)) * kg_ref[...]
            return carry
        lax.fori_loop(0, S // tn, norm_step, 0)

        def fill(blk, d):
            tokens = _block_tokens(blk, d, S)
            kc[_padded_block(blk), :] = kn[tokens, :].astype(BF16)
            vc[_padded_block(blk), :] = v_ref[tokens, :].astype(BF16)

        def block(blk, d):
            tokens = _block_tokens(blk, d, S)
            keys = pl.ds(pl.multiple_of(blk * BLOCK, BLOCK), 2 * BLOCK)
            q2 = _two_heads(qn[tokens, :].astype(BF16), lo)
            s = jnp.where(band_mask, _nt(q2, kc[keys, :]), NEG) + _first_block_bias(blk, S // d // BLOCK)
            m = jnp.max(s, axis=-1, keepdims=True)
            e = jnp.exp(s - m)
            l = jnp.sum(e, axis=-1, keepdims=True)
            o2 = jnp.dot(e.astype(BF16), vc[keys, :], preferred_element_type=F32) * (1.0 / l)
            lse2 = m + jnp.log(l)
            o = jnp.where(lo, o2[:BLOCK], o2[BLOCK:])
            lse = jnp.where(lo, lse2[:BLOCK], lse2[BLOCK:])
            if d > 1:
                la = lse_ref[tokens, :]
                mx = jnp.maximum(la, lse)
                wa, wb = jnp.exp(la - mx), jnp.exp(lse - mx)
                t = wa + wb
                o = (wa * att_ref[tokens, :] + wb * o) / t
                lse = mx + jnp.log(t)
            att_ref[tokens, :] = o
            lse_ref[tokens, :] = lse

        for d in DILATIONS:
            _for_blocks(S // BLOCK, 4, functools.partial(fill, d=d))
            _for_blocks(S // BLOCK, 8, functools.partial(block, d=d))

        def gate_step(i, carry):
            rows = pl.ds(pl.multiple_of(i * tn, tn), tn)
            silu, _ = _silu_parts(g_ref[rows, :])
            y_ref[rows, :] = (att_ref[rows, :] * silu).astype(BF16)
            return carry
        lax.fori_loop(0, S // tn, gate_step, 0)

    col = lambda j0: pl.BlockSpec((S, 128), lambda p, j0=j0: (0, j0 + p))
    const = lambda shape: pl.BlockSpec(shape, lambda p: (0,) * len(shape))
    out = pl.BlockSpec((S, 128), lambda p: (0, p))
    return pl.pallas_call(
        body, name="attn_fwd", grid=(npairs,),
        in_specs=[col(COL_AQ), col(COL_AK), col(COL_AV), col(COL_AG), const((1, 128)), const((1, 128)),
                  const((128, 128))],
        out_specs=[out, out, out],
        out_shape=[pltpu.HBM((S, ATTN_WIDTH), BF16), pltpu.HBM((S, ATTN_WIDTH), F32),
                   pltpu.HBM((S, ATTN_WIDTH), F32)],
        scratch_shapes=[pltpu.VMEM((S, 128), F32), pltpu.VMEM((S, 128), F32),
                        pltpu.VMEM((S + BLOCK, 128), BF16), pltpu.VMEM((S + BLOCK, 128), BF16)],
        compiler_params=_params(48, ("arbitrary",)),
    )(*_hbm(proj, proj, proj, proj, qg2, kg2, bd))


def _attn_bwd_prep(proj, dycat, att, bd):
    S = proj.shape[0]
    tm = 512

    def body(g_ref, dy_ref, att_ref, bd_ref, do_ref, dd_ref, dg_ref):
        silu, dsilu = _silu_parts(g_ref[...])
        dy = dy_ref[...]
        at = att_ref[...]
        do = dy * silu
        do_ref[...] = do
        dd_ref[...] = _split_dot(do * at, bd_ref[...])
        dg_ref[...] = (dy * at * dsilu).astype(BF16)

    tile = lambda j0: pl.BlockSpec((tm, 256), lambda i, j, j0=j0: (i, j0 + j))
    return pl.pallas_call(
        body, name="attn_bwd_prep", grid=(S // tm, ATTN_WIDTH // 256),
        in_specs=[tile(COL_AG // 2), tile(1), tile(0), pl.BlockSpec((256, 256), lambda i, j: (0, 0))],
        out_specs=[tile(0), tile(0), tile(0)],
        out_shape=[pltpu.HBM((S, ATTN_WIDTH), F32), pltpu.HBM((S, ATTN_WIDTH), F32),
                   pltpu.HBM((S, ATTN_WIDTH), BF16)],
        compiler_params=_params(32, ("arbitrary", "arbitrary")),
    )(*_hbm(proj, dycat, att, bd))


def _attn_bwd(proj, do, dd, lse, qg2, kg2, bd):
    S = proj.shape[0]
    npairs = ATTN_WIDTH // 128
    tn = 512

    def body(q_ref, k_ref, v_ref, do_ref, dd_ref, lse_ref, qg_ref, kg_ref, bd_ref,
             dq_ref, dk_ref, dv_ref, dqg_ref, dkg_ref,
             qn, kn, kc, vc, dk_own, dv_own, dk_prev, dv_prev, dqa, dka, dva):
        bdv = bd_ref[...]
        lo = lax.broadcasted_iota(jnp.int32, (BLOCK, 128), 1) < HEAD_DIM
        zeros_bf = jnp.zeros((BLOCK, 128), BF16)
        zeros_f = jnp.zeros((BLOCK, 128), F32)
        kc[pl.ds(0, BLOCK), :] = zeros_bf
        vc[pl.ds(0, BLOCK), :] = zeros_bf
        dk_prev[pl.ds(S, BLOCK), :] = zeros_f
        dv_prev[pl.ds(S, BLOCK), :] = zeros_f

        def norm_step(i, carry):
            rows = pl.ds(pl.multiple_of(i * tn, tn), tn)
            qv = q_ref[rows, :]
            kv = k_ref[rows, :]
            qn[rows, :] = (qv * lax.rsqrt(_split_dot(qv * qv, bdv) * (1.0 / HEAD_DIM) + EPS)) * (qg_ref[...] * SCALE)
            kn[rows, :] = (kv * lax.rsqrt(_split_dot(kv * kv, bdv) * (1.0 / HEAD_DIM) + EPS)) * kg_ref[...]
            return carry
        lax.fori_loop(0, S // tn, norm_step, 0)

        kt = lax.broadcasted_iota(jnp.int32, (2 * BLOCK, 2 * BLOCK), 0)
        qt = lax.broadcasted_iota(jnp.int32, (2 * BLOCK, 2 * BLOCK), 1) % BLOCK
        band_mask_t = ((kt < BLOCK) & (kt >= qt)) | ((kt >= BLOCK) & ((kt - BLOCK) <= qt))

        def per_query_row(t):
            tt = t.T
            return jnp.concatenate([tt[0:1, :], tt[HEAD_DIM:HEAD_DIM + 1, :]], axis=1)

        def fill(blk, d):
            tokens = _block_tokens(blk, d, S)
            kc[_padded_block(blk), :] = kn[tokens, :].astype(BF16)
            vc[_padded_block(blk), :] = v_ref[tokens, :].astype(BF16)

        def block(blk, d):
            tokens = _block_tokens(blk, d, S)
            own = pl.ds(pl.multiple_of(blk * BLOCK, BLOCK), BLOCK)
            keys = pl.ds(pl.multiple_of(blk * BLOCK, BLOCK), 2 * BLOCK)
            q2 = _two_heads(qn[tokens, :].astype(BF16), lo)
            do2 = _two_heads(do_ref[tokens, :].astype(BF16), lo)
            lse_row = per_query_row(lse_ref[tokens, :])
            dd_row = per_query_row(dd_ref[tokens, :])
            kb = kc[keys, :]
            vb = vc[keys, :]
            st = jnp.where(band_mask_t, _nt(kb, q2), NEG)
            kill = jnp.where((blk & (S // d // BLOCK - 1)) == 0, NEG, 0.0)
            st = jnp.concatenate([st[:BLOCK] + kill, st[BLOCK:]], axis=0)
            pt = jnp.exp(st - lse_row)
            dst = pt * (_nt(vb, do2) - dd_row)
            ptb = pt.astype(BF16)
            dstb = dst.astype(BF16)
            dv_band = jnp.dot(ptb, do2, preferred_element_type=F32)
            dk_band = jnp.dot(dstb, q2, preferred_element_type=F32)
            dv_prev[own, :] = dv_band[:BLOCK]
            dv_own[own, :] = dv_band[BLOCK:]
            dk_prev[own, :] = dk_band[:BLOCK]
            dk_own[own, :] = dk_band[BLOCK:]
            dq2 = _tn(dstb, kb)
            dq = jnp.where(lo, dq2[:BLOCK], dq2[BLOCK:])
            dqa[tokens, :] = dq if d == 1 else dqa[tokens, :] + dq

        def fold(blk, d):
            tokens = _block_tokens(blk, d, S)
            own = pl.ds(pl.multiple_of(blk * BLOCK, BLOCK), BLOCK)
            dk = dk_own[own, :] + dk_prev[_padded_block(blk), :]
            dv = dv_own[own, :] + dv_prev[_padded_block(blk), :]
            dka[tokens, :] = dk if d == 1 else dka[tokens, :] + dk
            dva[tokens, :] = dv if d == 1 else dva[tokens, :] + dv

        for d in DILATIONS:
            _for_blocks(S // BLOCK, 4, functools.partial(fill, d=d))
            _for_blocks(S // BLOCK, 8, functools.partial(block, d=d))
            _for_blocks(S // BLOCK, 4, functools.partial(fold, d=d))

        def out_step(i, carry):
            dqg, dkg = carry
            rows = pl.ds(pl.multiple_of(i * tn, tn), tn)
            qv = q_ref[rows, :]
            kv = k_ref[rows, :]
            rq = lax.rsqrt(_split_dot(qv * qv, bdv) * (1.0 / HEAD_DIM) + EPS)
            rk = lax.rsqrt(_split_dot(kv * kv, bdv) * (1.0 / HEAD_DIM) + EPS)
            qh = qv * rq
            kh = kv * rk
            dqs = dqa[rows, :] * SCALE
            dkn = dka[rows, :]
            aq = dqs * qg_ref[...]
            ak = dkn * kg_ref[...]
            dq_ref[rows, :] = (rq * (aq - qh * (_split_dot(aq * qh, bdv) * (1.0 / HEAD_DIM)))).astype(BF16)
            dk_ref[rows, :] = (rk * (ak - kh * (_split_dot(ak * kh, bdv) * (1.0 / HEAD_DIM)))).astype(BF16)
            dv_ref[rows, :] = dva[rows, :].astype(BF16)
            dqg = dqg + jnp.sum(dqs * qh, axis=0, keepdims=True)
            dkg = dkg + jnp.sum(dkn * kh, axis=0, keepdims=True)
            return dqg, dkg
        zero = jnp.zeros((1, 128), F32)
        dqg, dkg = lax.fori_loop(0, S // tn, out_step, (zero, zero))
        dqg_ref[0] = dqg
        dkg_ref[0] = dkg

    once = pl.Buffered(1)
    col = lambda j0: pl.BlockSpec((S, 128), lambda p, j0=j0: (0, j0 + p), pipeline_mode=once)
    const = lambda shape: pl.BlockSpec(shape, lambda p: (0,) * len(shape))
    out = pl.BlockSpec((S, 128), lambda p: (0, p))
    gain_out = pl.BlockSpec((1, 1, 128), lambda p: (p, 0, 0))
    piece = pltpu.HBM((S, ATTN_WIDTH), BF16)
    gains = pltpu.HBM((npairs, 1, 128), F32)
    f32buf = pltpu.VMEM((S, 128), F32)
    f32pad = pltpu.VMEM((S + BLOCK, 128), F32)
    bf16pad = pltpu.VMEM((S + BLOCK, 128), BF16)
    return pl.pallas_call(
        body, name="attn_bwd", grid=(npairs,),
        in_specs=[col(COL_AQ), col(COL_AK), col(COL_AV), col(0), col(0), col(0), const((1, 128)), const((1, 128)),
                  const((128, 128))],
        out_specs=[out, out, out, gain_out, gain_out],
        out_shape=[piece, piece, piece, gains, gains],
        scratch_shapes=[f32buf, f32buf, bf16pad, bf16pad, f32buf, f32buf, f32pad, f32pad, f32buf, f32buf, f32buf],
        compiler_params=_params(48, ("arbitrary",)),
    )(*_hbm(proj, proj, proj, do, dd, lse, qg2, kg2, bd))


def _mem_kv(mem, gain, wkv_bf, kg4, bd):
    def body(mem_ref, g_ref, w_ref, kg_ref, bd_ref, hm_ref, kraw_ref, mk_ref, mv_ref):
        mv_ = mem_ref[...]
        r = lax.rsqrt(jnp.mean(mv_ * mv_, axis=-1, keepdims=True) + EPS)
        hm = ((mv_ * r) * g_ref[...]).astype(BF16)
        hm_ref[...] = hm
        kv = jnp.dot(hm, w_ref[...], preferred_element_type=F32)
        kraw = kv[:, :MEM_WIDTH]
        kraw_ref[...] = kraw
        ms = _split_dot(kraw * kraw, bd_ref[...]) * (1.0 / HEAD_DIM)
        mk_ref[...] = (kraw * lax.rsqrt(ms + EPS)) * kg_ref[...]
        mv_ref[...] = kv[:, MEM_WIDTH:]

    sq = jax.ShapeDtypeStruct((MEM_LEN, MEM_WIDTH), F32)
    return pl.pallas_call(
        body, name="mem_kv",
        out_shape=[jax.ShapeDtypeStruct((MEM_LEN, D_MODEL), BF16), sq, sq, sq],
        compiler_params=_params(16),
    )(mem, gain, wkv_bf, kg4, bd)


def _mem_fwd(proj, mk, mv, qg4, bd):
    S = proj.shape[0]
    tm = 512

    def body(q_ref, g_ref, mk_ref, mv_ref, qg_ref, bd_ref, y_ref, om_ref):
        qv = q_ref[...]
        ms = _split_dot(qv * qv, bd_ref[...]) * (1.0 / HEAD_DIM)
        qs = (qv * lax.rsqrt(ms + EPS)) * (qg_ref[...] * SCALE)
        mkb = mk_ref[...].astype(BF16)
        mvb = mv_ref[...].astype(BF16)
        head = _head_index((tm, MEM_WIDTH))
        o = jnp.zeros((tm, MEM_WIDTH), F32)
        for h in range(4):
            s = _nt(jnp.where(head == h, qs, 0.0).astype(BF16), mkb)
            e = jnp.exp(s - jnp.max(s, axis=-1, keepdims=True))
            p = e * (1.0 / jnp.sum(e, axis=-1, keepdims=True))
            o = jnp.where(head == h, jnp.dot(p.astype(BF16), mvb, preferred_element_type=F32), o)
        om_ref[...] = o
        silu, _ = _silu_parts(g_ref[...])
        y_ref[...] = (o * silu).astype(BF16)

    col = lambda j: pl.BlockSpec((tm, MEM_WIDTH), lambda i, j=j: (i, j))
    const = lambda shape: pl.BlockSpec(shape, lambda i: (0,) * len(shape))
    tile = pl.BlockSpec((tm, MEM_WIDTH), lambda i: (i, 0))
    return pl.pallas_call(
        body, name="mem_fwd", grid=(S // tm,),
        in_specs=[col(11), col(12), const((MEM_LEN, MEM_WIDTH)), const((MEM_LEN, MEM_WIDTH)), const((1, MEM_WIDTH)),
                  const((MEM_WIDTH, MEM_WIDTH))],
        out_specs=[tile, tile],
        out_shape=[pltpu.HBM((S, MEM_WIDTH), BF16), pltpu.HBM((S, MEM_WIDTH), F32)],
        compiler_params=_params(24, ("arbitrary",)),
    )(*_hbm(proj, proj, mk, mv, qg4, bd))


def _mem_bwd(proj, dycat, om, mk, mv, qg4, bd):
    S = proj.shape[0]
    tm = 512

    def body(q_ref, g_ref, dy_ref, om_ref, mk_ref, mv_ref, qg_ref, bd_ref,
             dq_ref, dg_ref, dmk_ref, dmv_ref, dqg_ref):
        i = pl.program_id(0)

        @pl.when(i == 0)
        def _():
            dmk_ref[...] = jnp.zeros_like(dmk_ref)
            dmv_ref[...] = jnp.zeros_like(dmv_ref)
            dqg_ref[...] = jnp.zeros_like(dqg_ref)

        bdv = bd_ref[...]
        qv = q_ref[...]
        rq = lax.rsqrt(_split_dot(qv * qv, bdv) * (1.0 / HEAD_DIM) + EPS)
        qh = qv * rq
        qs = qh * (qg_ref[...] * SCALE)
        silu, dsilu = _silu_parts(g_ref[...])
        dy = dy_ref[...]
        o = om_ref[...]
        do = dy * silu
        dg_ref[...] = (dy * o * dsilu).astype(BF16)
        dd = _split_dot(do * o, bdv)
        mkb = mk_ref[...].astype(BF16)
        mvb = mv_ref[...].astype(BF16)
        head = _head_index((tm, MEM_WIDTH))
        dqs = jnp.zeros((tm, MEM_WIDTH), F32)
        for h in range(4):
            qhd = jnp.where(head == h, qs, 0.0).astype(BF16)
            doh = jnp.where(head == h, do, 0.0).astype(BF16)
            s = _nt(qhd, mkb)
            e = jnp.exp(s - jnp.max(s, axis=-1, keepdims=True))
            p = e * (1.0 / jnp.sum(e, axis=-1, keepdims=True))
            ds = p * (_nt(doh, mvb) - dd[:, h * HEAD_DIM:h * HEAD_DIM + 1])
            dsb = ds.astype(BF16)
            dmv_ref[...] += _tn(p.astype(BF16), doh)
            dmk_ref[...] += _tn(dsb, qhd)
            dqs = jnp.where(head == h, jnp.dot(dsb, mkb, preferred_element_type=F32), dqs)
        dqs = dqs * SCALE
        a = dqs * qg_ref[...]
        dq_ref[...] = (rq * (a - qh * (_split_dot(a * qh, bdv) * (1.0 / HEAD_DIM)))).astype(BF16)
        dqg_ref[...] += jnp.sum(dqs * qh, axis=0, keepdims=True)

    col = lambda j: pl.BlockSpec((tm, MEM_WIDTH), lambda i, j=j: (i, j))
    const = lambda shape: pl.BlockSpec(shape, lambda i: (0,) * len(shape))
    tile = pl.BlockSpec((tm, MEM_WIDTH), lambda i: (i, 0))
    piece = pltpu.HBM((S, MEM_WIDTH), BF16)
    sq = pltpu.HBM((MEM_LEN, MEM_WIDTH), F32)
    return pl.pallas_call(
        body, name="mem_bwd", grid=(S // tm,),
        in_specs=[col(11), col(12), col(3), tile, const((MEM_LEN, MEM_WIDTH)), const((MEM_LEN, MEM_WIDTH)),
                  const((1, MEM_WIDTH)), const((MEM_WIDTH, MEM_WIDTH))],
        out_specs=[tile, tile, const((MEM_LEN, MEM_WIDTH)), const((MEM_LEN, MEM_WIDTH)), const((1, MEM_WIDTH))],
        out_shape=[piece, piece, sq, sq, pltpu.HBM((1, MEM_WIDTH), F32)],
        compiler_params=_params(32, ("arbitrary",)),
    )(*_hbm(proj, proj, dycat, om, mk, mv, qg4, bd))


def _mem_kv_bwd(dmk, dmv, kraw, mem, gain, kg4, wkv_bf, hm_bf, bd):
    def body(dmk_ref, dmv_ref, kraw_ref, mem_ref, g_ref, kg_ref, w_ref, hm_ref, bd_ref, dw_ref, dg_ref, dkg_ref):
        bdv = bd_ref[...]
        kraw = kraw_ref[...]
        rk = lax.rsqrt(_split_dot(kraw * kraw, bdv) * (1.0 / HEAD_DIM) + EPS)
        kh = kraw * rk
        dmkv = dmk_ref[...]
        a = dmkv * kg_ref[...]
        dkraw = rk * (a - kh * (_split_dot(a * kh, bdv) * (1.0 / HEAD_DIM)))
        dkg_ref[...] = jnp.sum(dmkv * kh, axis=0, keepdims=True)
        dkv = jnp.concatenate([dkraw, dmv_ref[...]], axis=1).astype(BF16)
        dw = _tn(hm_ref[...], dkv).astype(BF16)
        rows_blk = D_MODEL // N_DEV
        for j in range(N_DEV):
            dw_ref[j] = dw[rows_blk * j:rows_blk * (j + 1)]
        dhm = _nt(dkv, w_ref[...])
        mv_ = mem_ref[...]
        r = lax.rsqrt(jnp.mean(mv_ * mv_, axis=-1, keepdims=True) + EPS)
        dg_ref[...] = jnp.sum(dhm * (mv_ * r), axis=0, keepdims=True)

    return pl.pallas_call(
        body, name="mem_kv_bwd",
        out_shape=[jax.ShapeDtypeStruct((N_DEV, D_MODEL // N_DEV, 2 * MEM_WIDTH), BF16),
                   jax.ShapeDtypeStruct((1, D_MODEL), F32), jax.ShapeDtypeStruct((1, MEM_WIDTH), F32)],
        compiler_params=_params(24),
    )(dmk, dmv, kraw, mem, gain, kg4, wkv_bf, hm_bf, bd)


def _out_loss(yg, ya, ym, x, tgt, wout_bf):
    S, D = x.shape
    tm = 256

    nsteps = S // tm
    rows_blk = D // N_DEV

    def body(yg_ref, ya_ref, ym_ref, x_ref, t_ref, w_ref, dout_ref, dycat_ref, dw_ref, loss_ref, acc_ref):
        i = pl.program_id(0)

        @pl.when(i == 0)
        def _():
            acc_ref[...] = jnp.zeros_like(acc_ref)
            loss_ref[...] = jnp.zeros_like(loss_ref)

        ycat = jnp.concatenate([yg_ref[...], ya_ref[...], ym_ref[...]], axis=1)
        w = w_ref[...]
        diff = (x_ref[...] + jnp.dot(ycat, w, preferred_element_type=F32)) - t_ref[...]
        loss_ref[...] += jnp.sum(diff * diff, axis=0, keepdims=True)
        dout = diff * (1.0 / D)
        dout_ref[...] = dout
        db = dout.astype(BF16)
        dycat_ref[...] = _nt(db, w)
        acc_ref[...] += _tn(ycat, db)

        @pl.when(i == nsteps - 1)
        def _():
            for j in range(N_DEV):
                dw_ref[j] = acc_ref[rows_blk * j:rows_blk * (j + 1), :].astype(BF16)

    tile = lambda w: pl.BlockSpec((tm, w), lambda i: (i, 0))
    const = lambda shape: pl.BlockSpec(shape, lambda i: (0,) * len(shape))
    return pl.pallas_call(
        body, name="out_loss", grid=(nsteps,),
        in_specs=[tile(GMLP_WIDTH), tile(ATTN_WIDTH), tile(MEM_WIDTH), tile(D), tile(D), const((D, D))],
        out_specs=[tile(D), tile(D), const((N_DEV, rows_blk, D)), const((1, D))],
        out_shape=[pltpu.HBM((S, D), F32), pltpu.HBM((S, D), F32),
                   pltpu.HBM((N_DEV, rows_blk, D), BF16), pltpu.HBM((1, D), F32)],
        scratch_shapes=[pltpu.VMEM((D, D), F32)],
        compiler_params=_params(40, ("arbitrary",)),
    )(*_hbm(yg, ya, ym, x, tgt, wout_bf))


def _piece_specs(pieces, tm):
    return [pl.BlockSpec((tm, p.shape[1]), lambda i: (i, 0)) for p in pieces]


def _in_bwd_dx(pieces, x, dout, gain, w_t):
    S, D = x.shape
    N = w_t.shape[0]
    tm = 256
    n = len(pieces)

    def body(*refs):
        piece_refs = refs[:n]
        x_ref, dout_ref, g_ref, w_ref, gx_ref, dg_ref = refs[n:]

        @pl.when(pl.program_id(0) == 0)
        def _():
            dg_ref[...] = jnp.zeros_like(dg_ref)

        dproj = jnp.concatenate([r[...] for r in piece_refs], axis=1)
        dh = jnp.dot(dproj, w_ref[...], preferred_element_type=F32)
        xv = x_ref[...]
        r = lax.rsqrt(jnp.mean(xv * xv, axis=-1, keepdims=True) + EPS)
        xh = xv * r
        a = dh * g_ref[...]
        gx_ref[...] = dout_ref[...] + r * (a - xh * jnp.mean(a * xh, axis=-1, keepdims=True))
        dg_ref[...] += jnp.sum(dh * xh, axis=0, keepdims=True)

    tile = pl.BlockSpec((tm, D), lambda i: (i, 0))
    const = lambda shape: pl.BlockSpec(shape, lambda i: (0,) * len(shape))
    return pl.pallas_call(
        body, name="in_bwd_dx", grid=(S // tm,),
        in_specs=_piece_specs(pieces, tm) + [tile, tile, const((1, D)), const((N, D))],
        out_specs=[tile, const((1, D))],
        out_shape=[pltpu.HBM((S, D), F32), pltpu.HBM((1, D), F32)],
        compiler_params=_params(40, ("arbitrary",)),
    )(*_hbm(*pieces, x, dout, gain, w_t))


def _in_bwd_dw(pieces, h_bf):
    S, D = h_bf.shape
    N = sum(p.shape[1] for p in pieces)
    n_blk = N // N_DEV
    tm = 512
    n = len(pieces)
    nsteps = S // tm

    def body(*refs):
        piece_refs = refs[:n]
        h_ref, dw_ref, acc_ref = refs[n:]
        i = pl.program_id(0)

        @pl.when(i == 0)
        def _():
            acc_ref[...] = jnp.zeros_like(acc_ref)

        dproj = jnp.concatenate([r[...] for r in piece_refs], axis=1)
        acc_ref[...] += _tn(h_ref[...], dproj)

        @pl.when(i == nsteps - 1)
        def _():
            for j in range(N_DEV):
                dw_ref[j] = acc_ref[:, n_blk * j:n_blk * (j + 1)].T.astype(BF16)

    return pl.pallas_call(
        body, name="in_bwd_dw", grid=(nsteps,),
        in_specs=_piece_specs(pieces, tm) + [pl.BlockSpec((tm, D), lambda i: (i, 0))],
        out_specs=pl.BlockSpec((N_DEV, n_blk, D), lambda i: (0, 0, 0)),
        out_shape=pltpu.HBM((N_DEV, n_blk, D), BF16),
        scratch_shapes=[pltpu.VMEM((D, N), F32)],
        compiler_params=_params(48, ("arbitrary",)),
    )(*_hbm(*pieces, h_bf))


def _row_step(m):
    return max(t for t in range(16, 257, 16) if m % t == 0)


def _place():
    x, y, c = lax.axis_index("x"), lax.axis_index("y"), lax.axis_index("c")
    chips = [(1 - x, y), (x, 1 - y), (1 - x, 1 - y)]
    return x, y, c, chips


def _all_gather_exchange(srcs, outs, send_sems, recv_sems, local_sems):
    n = len(srcs)
    x, y, c, chips = _place()
    me, sibling = (x, y, c), (x, y, 1 - c)

    def rows(a, px, py, pc):
        m = srcs[a].shape[0]
        return outs[a].at[pl.ds((4 * px + 2 * py + pc) * m, m), :]

    def copy(a, k, block, to, src=None):
        return pltpu.make_async_remote_copy(
            src_ref=rows(a, *block) if src is None else src, dst_ref=rows(a, *block),
            send_sem=send_sems.at[a, k], recv_sem=recv_sems.at[a, k], device_id=to, device_id_type=MESH)

    mine = [pltpu.make_async_copy(srcs[a], rows(a, *me), local_sems.at[a]) for a in range(n)]
    for cp in mine:
        cp.start()
    first = []
    for a in range(n):
        first.append(copy(a, 0, me, sibling, src=srcs[a]))
        first += [copy(a, 1 + j, me, (*chip, c), src=srcs[a]) for j, chip in enumerate(chips)]
    for cp in first:
        cp.start()
    passed = []
    for j, chip in enumerate(chips):
        for a in range(n):
            copy(a, 1 + j, (*chip, c), me).wait_recv()
            fwd = copy(a, 4 + j, (*chip, c), sibling)
            fwd.start()
            passed.append(fwd)
    for a in range(n):
        copy(a, 0, sibling, me).wait_recv()
        for j, chip in enumerate(chips):
            copy(a, 4 + j, (*chip, 1 - c), me).wait_recv()
    for cp in first + passed:
        cp.wait_send()
    for cp in mine:
        cp.wait()


def _gather_weights(shards):
    n = len(shards)

    def body(*refs):
        ins, outs, casts = refs[:n], refs[n:2 * n], refs[2 * n:3 * n]
        for a in range(n):
            tr = _row_step(ins[a].shape[0])

            def cast(i, carry, a=a, tr=tr):
                rows = pl.ds(pl.multiple_of(i * tr, tr), tr)
                casts[a][rows, :] = ins[a][rows, :].astype(BF16)
                return carry
            lax.fori_loop(0, ins[a].shape[0] // tr, cast, 0)
        _all_gather_exchange(casts, outs, *refs[3 * n:])

    vmem = pl.BlockSpec(memory_space=pltpu.VMEM)
    return pl.pallas_call(
        body, name="gather_weights",
        out_shape=[jax.ShapeDtypeStruct((N_DEV * a.shape[0], a.shape[1]), BF16) for a in shards],
        in_specs=[vmem] * n, out_specs=[vmem] * n,
        scratch_shapes=[pltpu.VMEM(a.shape, BF16) for a in shards]
        + [pltpu.SemaphoreType.DMA((n, 7)), pltpu.SemaphoreType.DMA((n, 7)), pltpu.SemaphoreType.DMA((n,))],
        compiler_params=_params(40),
    )(*shards)


ROW_NORM, ROW_MEM_NORM, ROW_V_GAIN, ROW_B, ROW_ATTN_GAINS, ROW_MEM_GAINS, ROW_W_S, ROW_LOSS = 0, 8, 16, 18, 22, 23, 24, 536
SMALL_ROWS = 544


def _gather_small(dgain, dmgain, dvg, db2, dqg, dkg, dmqg, dmkg, dws, sq):
    def body(dgain_ref, dmgain_ref, dvg_ref, db2_ref, dqg_ref, dkg_ref, dmqg_ref, dmkg_ref, dws_ref, sq_ref,
             out_ref, mine, send_sems, recv_sems, local_sems):
        first = lax.broadcasted_iota(jnp.int32, (1, 128), 1) < HEAD_DIM
        for i in range(8):
            cols = slice(128 * i, 128 * (i + 1))
            mine[ROW_NORM + i:ROW_NORM + i + 1, :] = dgain_ref[:, cols]
            mine[ROW_MEM_NORM + i:ROW_MEM_NORM + i + 1, :] = dmgain_ref[:, cols]
            mine[ROW_LOSS + i:ROW_LOSS + i + 1, :] = sq_ref[:, cols]
        mine[ROW_V_GAIN:ROW_V_GAIN + 1, :] = dvg_ref[:, 0:128]
        mine[ROW_V_GAIN + 1:ROW_V_GAIN + 2, :] = dvg_ref[:, 128:256]
        bt = db2_ref[...].T
        for h in range(4):
            mine[ROW_B + h:ROW_B + h + 1, :] = bt[HEAD_DIM * h:HEAD_DIM * h + 1, :]

        def fold_heads(t):
            return t + pltpu.roll(t, HEAD_DIM, axis=1)
        aq = fold_heads(dqg_ref[0] + dqg_ref[1] + dqg_ref[2] + dqg_ref[3])
        ak = fold_heads(dkg_ref[0] + dkg_ref[1] + dkg_ref[2] + dkg_ref[3])
        mine[ROW_ATTN_GAINS:ROW_ATTN_GAINS + 1, :] = jnp.where(first, aq, ak)
        mq = fold_heads(dmqg_ref[:, 0:128] + dmqg_ref[:, 128:256])
        mk = fold_heads(dmkg_ref[:, 0:128] + dmkg_ref[:, 128:256])
        mine[ROW_MEM_GAINS:ROW_MEM_GAINS + 1, :] = jnp.where(first, mq, mk)
        mine[ROW_W_S:ROW_W_S + 4 * CHUNK, :] = dws_ref[...]
        _all_gather_exchange([mine], [out_ref], send_sems, recv_sems, local_sems)

    return pl.pallas_call(
        body, name="gather_small_grads",
        out_shape=jax.ShapeDtypeStruct((N_DEV * SMALL_ROWS, 128), F32),
        scratch_shapes=[pltpu.VMEM((SMALL_ROWS, 128), F32), pltpu.SemaphoreType.DMA((1, 7)),
                        pltpu.SemaphoreType.DMA((1, 7)), pltpu.SemaphoreType.DMA((1,))],
        compiler_params=_params(16),
    )(dgain, dmgain, dvg, db2, dqg, dkg, dmqg, dmkg, dws, sq)


def _reduce_scatter(arrs, name):
    n = len(arrs)

    def body(*refs):
        ins, outs = refs[:n], refs[n:2 * n]
        half, quarter = refs[2 * n:3 * n], refs[3 * n:4 * n]
        send_sems, recv_sems = refs[4 * n:]
        x, y, c, chips = _place()
        sibling = (x, y, 1 - c)

        to_sibling = [pltpu.make_async_remote_copy(
            src_ref=ins[a].at[2 * q + (1 - c)], dst_ref=half[a].at[q], send_sem=send_sems.at[a, q],
            recv_sem=recv_sems.at[a, q], device_id=sibling, device_id_type=MESH) for a in range(n) for q in range(4)]
        for cp in to_sibling:
            cp.start()
        for cp in to_sibling:
            cp.wait_recv()

        def add_rows(a, fn):
            m = ins[a].shape[1]
            tr = _row_step(m)
            def step(i, carry):
                fn(pl.ds(pl.multiple_of(i * tr, tr), tr))
                return carry
            lax.fori_loop(0, m // tr, step, 0)

        for a in range(n):
            for q in range(4):
                def add_half(rows, a=a, q=q):
                    both = ins[a][2 * q + c, rows, :].astype(F32) + half[a][q, rows, :].astype(F32)
                    half[a][q, rows, :] = both.astype(BF16)
                add_rows(a, add_half)

        to_chips = [pltpu.make_async_remote_copy(
            src_ref=half[a].at[2 * chip[0] + chip[1]], dst_ref=quarter[a].at[k], send_sem=send_sems.at[a, 4 + k],
            recv_sem=recv_sems.at[a, 4 + k], device_id=(*chip, c), device_id_type=MESH)
            for a in range(n) for k, chip in enumerate(chips)]
        for cp in to_chips:
            cp.start()
        for cp in to_chips:
            cp.wait_recv()
        for a in range(n):
            def add_quarters(rows, a=a):
                f = lambda t: t.astype(F32)
                outs[a][rows, :] = ((f(half[a][2 * x + y, rows, :]) + f(quarter[a][0, rows, :]))
                                    + (f(quarter[a][1, rows, :]) + f(quarter[a][2, rows, :])))
            add_rows(a, add_quarters)
        for cp in to_sibling + to_chips:
            cp.wait_send()

    vmem = pl.BlockSpec(memory_space=pltpu.VMEM)
    return pl.pallas_call(
        body, name=name,
        out_shape=[jax.ShapeDtypeStruct(a.shape[1:], F32) for a in arrs],
        in_specs=[vmem] * n, out_specs=[vmem] * n,
        scratch_shapes=[pltpu.VMEM((4,) + a.shape[1:], BF16) for a in arrs]
        + [pltpu.VMEM((3,) + a.shape[1:], BF16) for a in arrs]
        + [pltpu.SemaphoreType.DMA((n, 7)), pltpu.SemaphoreType.DMA((n, 7))],
        compiler_params=_params(48),
    )(*arrs)


def _adamw_math(w, g, m, v):
    m = ADAM_B1 * m + (1.0 - ADAM_B1) * g
    v = ADAM_B2 * v + (1.0 - ADAM_B2) * (g * g)
    m_hat = m / (1.0 - ADAM_B1 ** ADAM_STEP)
    v_hat = v / (1.0 - ADAM_B2 ** ADAM_STEP)
    delta = -ADAM_LR * (m_hat / (jnp.sqrt(v_hat) + ADAM_EPS) + ADAM_WD * w)
    return delta, m, v


def _adamw(w, g, m, v, name):
    R, C = w.shape
    tr = _row_step(R)

    def body(w_ref, g_ref, m_ref, v_ref, d_ref, nm_ref, nv_ref):
        d_ref[...], nm_ref[...], nv_ref[...] = _adamw_math(w_ref[...], g_ref[...], m_ref[...], v_ref[...])

    tile = pl.BlockSpec((tr, C), lambda i: (i, 0))
    out = pltpu.HBM((R, C), F32)
    return pl.pallas_call(
        body, name=name, grid=(R // tr,), in_specs=[tile] * 4, out_specs=[tile] * 3, out_shape=[out] * 3,
        compiler_params=_params(16, ("arbitrary",)),
    )(*_hbm(w, g, m, v))


SMALL = ("norm_gain", "gmlp_v_gain", "gmlp_w_s", "gmlp_b", "attn_q_gain", "attn_k_gain", "mem_norm_gain",
         "mem_q_gain", "mem_k_gain")
WEIGHTS = ("norm_gain", "w_in", "gmlp_v_gain", "gmlp_w_s", "gmlp_b", "attn_q_gain", "attn_k_gain",
           "mem_norm_gain", "w_mem_kv", "mem_q_gain", "mem_k_gain", "w_out")


def _adamw_small(w, m, v, g_all):
    k = len(SMALL)
    half = slice(0, HEAD_DIM), slice(HEAD_DIM, 2 * HEAD_DIM)

    def body(*refs):
        w_refs, m_refs, v_refs = refs[:k], refs[k:2 * k], refs[2 * k:3 * k]
        g_ref = refs[3 * k]
        outs = refs[3 * k + 1:7 * k + 1]
        loss_ref, gsum = refs[7 * k + 1:]

        part = SMALL_ROWS // 4
        for p in range(4):
            acc = g_ref[part * p:part * (p + 1), :]
            for dev in range(1, N_DEV):
                acc = acc + g_ref[dev * SMALL_ROWS + part * p:dev * SMALL_ROWS + part * (p + 1), :]
            gsum[part * p:part * (p + 1), :] = acc

        def update(name, at, g):
            i = SMALL.index(name)
            d, nm, nv = _adamw_math(w_refs[i][at], g, m_refs[i][at], v_refs[i][at])
            outs[i][at], outs[k + i][at], outs[2 * k + i][at], outs[3 * k + i][at] = g, d, nm, nv

        for i in range(8):
            at = (slice(0, 1), slice(128 * i, 128 * (i + 1)))
            update("norm_gain", at, gsum[ROW_NORM + i:ROW_NORM + i + 1, :])
            update("mem_norm_gain", at, gsum[ROW_MEM_NORM + i:ROW_MEM_NORM + i + 1, :])
        for h in range(4):
            row = (0, slice(h, h + 1), slice(None))
            update("gmlp_v_gain", row, gsum[ROW_V_GAIN + h // 2:ROW_V_GAIN + h // 2 + 1, half[h % 2]])
            update("gmlp_b", row, gsum[ROW_B + h:ROW_B + h + 1, :])
            update("gmlp_w_s", (0, h), gsum[ROW_W_S + CHUNK * h:ROW_W_S + CHUNK * (h + 1), :])
        whole = (slice(0, 1), slice(None))
        update("attn_q_gain", whole, gsum[ROW_ATTN_GAINS:ROW_ATTN_GAINS + 1, half[0]])
        update("attn_k_gain", whole, gsum[ROW_ATTN_GAINS:ROW_ATTN_GAINS + 1, half[1]])
        update("mem_q_gain", whole, gsum[ROW_MEM_GAINS:ROW_MEM_GAINS + 1, half[0]])
        update("mem_k_gain", whole, gsum[ROW_MEM_GAINS:ROW_MEM_GAINS + 1, half[1]])
        loss_ref[...] = jnp.sum(gsum[ROW_LOSS:ROW_LOSS + 8, :], keepdims=True) * (0.5 / D_MODEL)

    shapes = [jax.ShapeDtypeStruct(w[name].shape, F32) for name in SMALL]
    res = pl.pallas_call(
        body, name="adamw_small",
        out_shape=shapes * 4 + [jax.ShapeDtypeStruct((1, 1), F32)],
        scratch_shapes=[pltpu.VMEM((SMALL_ROWS, 128), F32)],
        compiler_params=_params(16),
    )(*[w[n] for n in SMALL], *[m[n] for n in SMALL], *[v[n] for n in SMALL], g_all)
    trees = [dict(zip(SMALL, res[j * k:(j + 1) * k])) for j in range(4)]
    return (*trees, res[4 * k])


def _local_grads(x, mem, tgt, w, win_t, wkv_bf, wout_bf):
    bd128, bd256 = _head_blockdiag(128), _head_blockdiag(256)
    gain = w["norm_gain"].reshape(1, D_MODEL)
    vg = w["gmlp_v_gain"].reshape(1, GMLP_WIDTH)
    w_s = w["gmlp_w_s"].reshape(4, CHUNK, CHUNK)
    b2 = jnp.repeat(w["gmlp_b"].reshape(4, CHUNK).T, HEAD_DIM, axis=1)
    qg2 = jnp.tile(w["attn_q_gain"].reshape(1, HEAD_DIM), (1, 2))
    kg2 = jnp.tile(w["attn_k_gain"].reshape(1, HEAD_DIM), (1, 2))
    mqg4 = jnp.tile(w["mem_q_gain"].reshape(1, HEAD_DIM), (1, 4))
    mkg4 = jnp.tile(w["mem_k_gain"].reshape(1, HEAD_DIM), (1, 4))
    mgain = w["mem_norm_gain"].reshape(1, D_MODEL)

    proj, h_bf = _rms_proj(x, gain, win_t)
    yg = _gmlp_fwd(proj, vg, w_s, b2, bd256)
    ya, att, lse = _attn_fwd(proj, qg2, kg2, bd128)
    hm_bf, kraw, mk, mv = _mem_kv(mem, mgain, wkv_bf, mkg4, bd256)
    ym, om = _mem_fwd(proj, mk, mv, mqg4, bd256)
    dout, dycat, dwout, sq = _out_loss(yg, ya, ym, x, tgt, wout_bf)

    du, dgv, dgg, dws, db2, dvg = _gmlp_bwd(proj, dycat, vg, w_s, b2, bd256)
    do, dd, dag = _attn_bwd_prep(proj, dycat, att, bd256)
    dq, dk, dv, dqg, dkg = _attn_bwd(proj, do, dd, lse, qg2, kg2, bd128)
    dmq, dmg, dmk, dmv, dmqg = _mem_bwd(proj, dycat, om, mk, mv, mqg4, bd256)
    dwkv, dmgain, dmkg = _mem_kv_bwd(dmk, dmv, kraw, mem, mgain, mkg4, wkv_bf, hm_bf, bd256)
    pieces = [du, dgv, dgg, dq, dk, dv, dag, dmq, dmg]
    grad_x, dgain = _in_bwd_dx(pieces, x, dout, gain, win_t)
    dwin = _in_bwd_dw(pieces, h_bf)
    return grad_x, dwin, dwkv, dwout, (dgain, dmgain, dvg, db2, dqg, dkg, dmqg, dmkg, dws, sq)


def kernel(x, mem, norm_gain, w_in, gmlp_v_gain, gmlp_w_s, gmlp_b, attn_q_gain, attn_k_gain, mem_norm_gain, w_mem_kv, mem_q_gain, mem_k_gain, w_out, loss_target, m_norm_gain, m_w_in, m_gmlp_v_gain, m_gmlp_w_s, m_gmlp_b, m_attn_q_gain, m_attn_k_gain, m_mem_norm_gain, m_w_mem_kv, m_mem_q_gain, m_mem_k_gain, m_w_out, v_norm_gain, v_w_in, v_gmlp_v_gain, v_gmlp_w_s, v_gmlp_b, v_attn_q_gain, v_attn_k_gain, v_mem_norm_gain, v_w_mem_kv, v_mem_q_gain, v_mem_k_gain, v_w_out):
    w = dict(norm_gain=norm_gain, w_in=w_in, gmlp_v_gain=gmlp_v_gain, gmlp_w_s=gmlp_w_s, gmlp_b=gmlp_b,
             attn_q_gain=attn_q_gain, attn_k_gain=attn_k_gain, mem_norm_gain=mem_norm_gain, w_mem_kv=w_mem_kv,
             mem_q_gain=mem_q_gain, mem_k_gain=mem_k_gain, w_out=w_out)
    m = dict(norm_gain=m_norm_gain, w_in=m_w_in, gmlp_v_gain=m_gmlp_v_gain, gmlp_w_s=m_gmlp_w_s, gmlp_b=m_gmlp_b,
             attn_q_gain=m_attn_q_gain, attn_k_gain=m_attn_k_gain, mem_norm_gain=m_mem_norm_gain,
             w_mem_kv=m_w_mem_kv, mem_q_gain=m_mem_q_gain, mem_k_gain=m_mem_k_gain, w_out=m_w_out)
    v = dict(norm_gain=v_norm_gain, w_in=v_w_in, gmlp_v_gain=v_gmlp_v_gain, gmlp_w_s=v_gmlp_w_s, gmlp_b=v_gmlp_b,
             attn_q_gain=v_attn_q_gain, attn_k_gain=v_attn_k_gain, mem_norm_gain=v_mem_norm_gain,
             w_mem_kv=v_w_mem_kv, mem_q_gain=v_mem_q_gain, mem_k_gain=v_mem_k_gain, w_out=v_w_out)
    transposed = lambda t: jnp.transpose(t[0])

    win_t, wkv_bf, wout_bf = _gather_weights([transposed(w_in), w_mem_kv[0], w_out[0]])

    grad_x, dwin, dwkv, dwout, small = _local_grads(x[0], mem[0], loss_target[0], w, win_t, wkv_bf, wout_bf)

    (g_win,) = _reduce_scatter([dwin], "reduce_w_in")
    g_wkv, g_wout = _reduce_scatter([dwkv, dwout], "reduce_w_kv_out")
    small_all = _gather_small(*small)

    out_g, out_d, out_m, out_v, loss = _adamw_small(w, m, v, small_all)
    d_, m_, v_ = _adamw(transposed(w_in), g_win, transposed(m_w_in), transposed(v_w_in), "adamw_w_in")
    for tree, t in ((out_g, g_win), (out_d, d_), (out_m, m_), (out_v, v_)):
        tree["w_in"] = jnp.transpose(t)[None]
    for name, g in (("w_mem_kv", g_wkv), ("w_out", g_wout)):
        d_, m_, v_ = _adamw(w[name][0], g, m[name][0], v[name][0], "adamw_" + name)
        out_g[name], out_d[name], out_m[name], out_v[name] = g[None], d_[None], m_[None], v_[None]

    return (loss.reshape(()), grad_x[None], *[out_g[k] for k in WEIGHTS], *[out_d[k] for k in WEIGHTS],
            *[out_m[k] for k in WEIGHTS], *[out_v[k] for k in WEIGHTS])
```

```python
import functools
import math

import jax
import jax.numpy as jnp
from jax import lax
from jax.experimental import pallas as pl
from jax.experimental.pallas import tpu as pltpu

F32 = jnp.float32
BF16 = jnp.bfloat16

N_DEV = 8
D_MODEL = 1024
HEAD_DIM = 64
GMLP_WIDTH = 256
ATTN_WIDTH = 512
MEM_WIDTH = 256
MEM_LEN = 256
IN_WIDTH = 3 * GMLP_WIDTH + 4 * ATTN_WIDTH + 2 * MEM_WIDTH
CHUNK = 128
BLOCK = 128
DILATIONS = (1, 4, 16)
EPS = 1e-6
SCALE = 1.0 / math.sqrt(HEAD_DIM)
NEG = -1e30

ADAM_LR = 0.001
ADAM_B1 = 0.9
ADAM_B2 = 0.999
ADAM_EPS = 1e-08
ADAM_WD = 0.01
ADAM_STEP = 10

MIB = 1024 * 1024
MESH = pl.DeviceIdType.MESH

COL_AQ, COL_AK, COL_AV, COL_AG = 6, 10, 14, 18


def _params(vmem_mib, semantics=None):
    kw = dict(vmem_limit_bytes=vmem_mib * MIB)
    if semantics is not None:
        kw["dimension_semantics"] = semantics
    return pltpu.CompilerParams(**kw)


def _hbm(*arrs):
    return [pltpu.with_memory_space_constraint(a, pltpu.HBM) for a in arrs]


def _split_dot(x, sel_bf):
    hi = x.astype(BF16)
    lo = (x - hi.astype(F32)).astype(BF16)
    return jnp.dot(hi, sel_bf, preferred_element_type=F32) + jnp.dot(lo, sel_bf, preferred_element_type=F32)


def _nt(a, b):
    return lax.dot_general(a, b, (((1,), (1,)), ((), ())), preferred_element_type=F32)


def _tn(a, b):
    return lax.dot_general(a, b, (((0,), (0,)), ((), ())), preferred_element_type=F32)


def _silu_parts(g):
    sg = jax.nn.sigmoid(g)
    return g * sg, sg * (1.0 + g * (1.0 - sg))


def _head_index(shape):
    return lax.shift_right_logical(lax.broadcasted_iota(jnp.int32, shape, 1), HEAD_DIM.bit_length() - 1)


def _head_blockdiag(width):
    i = jnp.arange(width) // HEAD_DIM
    return (i[:, None] == i[None, :]).astype(BF16)


def _rms_proj(x, gain, w_t):
    S, D = x.shape
    N = w_t.shape[0]
    tm = 256

    def body(x_ref, g_ref, w_ref, proj_ref, h_ref):
        xv = x_ref[...]
        r = lax.rsqrt(jnp.mean(xv * xv, axis=-1, keepdims=True) + EPS)
        h = ((xv * r) * g_ref[...]).astype(BF16)
        h_ref[...] = h
        proj_ref[...] = _nt(h, w_ref[...])

    return pl.pallas_call(
        body, name="rms_proj", grid=(S // tm,),
        in_specs=[pl.BlockSpec((tm, D), lambda i: (i, 0)), pl.BlockSpec((1, D), lambda i: (0, 0)),
                  pl.BlockSpec((N, D), lambda i: (0, 0))],
        out_specs=[pl.BlockSpec((tm, N), lambda i: (i, 0)), pl.BlockSpec((tm, D), lambda i: (i, 0))],
        out_shape=[pltpu.HBM((S, N), F32), pltpu.HBM((S, D), BF16)],
        compiler_params=_params(40, ("arbitrary",)),
    )(*_hbm(x, gain, w_t))


def _gmlp_masked_weights(ws_ref, transpose):
    t = lax.broadcasted_iota(jnp.int32, (CHUNK, CHUNK), 0)
    s = lax.broadcasted_iota(jnp.int32, (CHUNK, CHUNK), 1)
    parts = []
    for h in range(4):
        wm = jnp.where(s <= t, ws_ref[h], 0.0)
        parts.append(wm.T if transpose else wm)
    return jnp.concatenate(parts, axis=1).astype(BF16)


def _head_stack(v, head):
    return jnp.concatenate([jnp.where(head == h, v, 0.0) for h in range(4)], axis=0).astype(BF16)


def _gmlp_fwd(proj, vg, w_s, b2, bd):
    S = proj.shape[0]
    tm = 512

    def body(u_ref, v_ref, g_ref, vg_ref, ws_ref, b2_ref, bd_ref, y_ref):
        v = v_ref[...]
        ms = _split_dot(v * v, bd_ref[...]) * (1.0 / HEAD_DIM)
        vn = (v * lax.rsqrt(ms + EPS)) * vg_ref[...]
        wcat = _gmlp_masked_weights(ws_ref, False)
        head = _head_index((CHUNK, GMLP_WIDTH))
        for c in range(tm // CHUNK):
            rows = slice(c * CHUNK, (c + 1) * CHUNK)
            sp = jnp.dot(wcat, _head_stack(vn[rows], head), preferred_element_type=F32) + b2_ref[...]
            silu, _ = _silu_parts(g_ref[rows, :])
            y_ref[rows, :] = ((u_ref[rows, :] * sp) * silu).astype(BF16)

    col = lambda j: pl.BlockSpec((tm, GMLP_WIDTH), lambda i, j=j: (i, j))
    const = lambda shape: pl.BlockSpec(shape, lambda i: (0,) * len(shape))
    return pl.pallas_call(
        body, name="gmlp_fwd", grid=(S // tm,),
        in_specs=[col(0), col(1), col(2), const((1, GMLP_WIDTH)), const((4, CHUNK, CHUNK)),
                  const((CHUNK, GMLP_WIDTH)), const((GMLP_WIDTH, GMLP_WIDTH))],
        out_specs=pl.BlockSpec((tm, GMLP_WIDTH), lambda i: (i, 0)),
        out_shape=pltpu.HBM((S, GMLP_WIDTH), BF16),
        compiler_params=_params(24, ("arbitrary",)),
    )(*_hbm(proj, proj, proj, vg, w_s, b2, bd))


def _gmlp_bwd(proj, dycat, vg, w_s, b2, bd):
    S = proj.shape[0]
    tm = 512
    nsteps = S // tm

    def body(u_ref, v_ref, g_ref, dy_ref, vg_ref, ws_ref, b2_ref, bd_ref,
             du_ref, dv_ref, dg_ref, dws_ref, db2_ref, dvg_ref):
        i = pl.program_id(0)

        @pl.when(i == 0)
        def _():
            dws_ref[...] = jnp.zeros_like(dws_ref)
            db2_ref[...] = jnp.zeros_like(db2_ref)
            dvg_ref[...] = jnp.zeros_like(dvg_ref)

        bdv = bd_ref[...]
        v = v_ref[...]
        ms = _split_dot(v * v, bdv) * (1.0 / HEAD_DIM)
        rv = lax.rsqrt(ms + EPS)
        xhat = v * rv
        vgv = vg_ref[...]
        vn = xhat * vgv
        wcat = _gmlp_masked_weights(ws_ref, False)
        wcat_t = _gmlp_masked_weights(ws_ref, True)
        head = _head_index((CHUNK, GMLP_WIDTH))
        dvg = jnp.zeros((1, GMLP_WIDTH), F32)
        for c in range(tm // CHUNK):
            rows = slice(c * CHUNK, (c + 1) * CHUNK)
            vn_c = vn[rows]
            spb = jnp.dot(wcat, _head_stack(vn_c, head), preferred_element_type=F32) + b2_ref[...]
            silu, dsilu = _silu_parts(g_ref[rows, :])
            dy = dy_ref[rows, :]
            u = u_ref[rows, :]
            du_ref[rows, :] = (dy * spb * silu).astype(BF16)
            dg_ref[rows, :] = (dy * u * spb * dsilu).astype(BF16)
            dsp = dy * u * silu
            db2_ref[...] += dsp
            dstack = _head_stack(dsp, head)
            dvn = jnp.dot(wcat_t, dstack, preferred_element_type=F32)
            dws_ref[...] += _nt(dstack, vn_c.astype(BF16))
            xh = xhat[rows]
            a = dvn * vgv
            mean_ax = _split_dot(a * xh, bdv) * (1.0 / HEAD_DIM)
            dv_ref[rows, :] = (rv[rows] * (a - xh * mean_ax)).astype(BF16)
            dvg = dvg + jnp.sum(dvn * xh, axis=0, keepdims=True)
        dvg_ref[...] += dvg

        @pl.when(i == nsteps - 1)
        def _():
            t = lax.broadcasted_iota(jnp.int32, (4 * CHUNK, CHUNK), 0) % CHUNK
            s = lax.broadcasted_iota(jnp.int32, (4 * CHUNK, CHUNK), 1)
            dws_ref[...] = jnp.where(s <= t, dws_ref[...], 0.0)
            db2_ref[...] = _split_dot(db2_ref[...], bdv)

    col = lambda j: pl.BlockSpec((tm, GMLP_WIDTH), lambda i, j=j: (i, j))
    const = lambda shape: pl.BlockSpec(shape, lambda i: (0,) * len(shape))
    tile = pl.BlockSpec((tm, GMLP_WIDTH), lambda i: (i, 0))
    piece = pltpu.HBM((S, GMLP_WIDTH), BF16)
    return pl.pallas_call(
        body, name="gmlp_bwd", grid=(nsteps,),
        in_specs=[col(0), col(1), col(2), col(0), const((1, GMLP_WIDTH)), const((4, CHUNK, CHUNK)),
                  const((CHUNK, GMLP_WIDTH)), const((GMLP_WIDTH, GMLP_WIDTH))],
        out_specs=[tile, tile, tile, const((4 * CHUNK, CHUNK)), const((CHUNK, GMLP_WIDTH)), const((1, GMLP_WIDTH))],
        out_shape=[piece, piece, piece, pltpu.HBM((4 * CHUNK, CHUNK), F32),
                   pltpu.HBM((CHUNK, GMLP_WIDTH), F32), pltpu.HBM((1, GMLP_WIDTH), F32)],
        compiler_params=_params(32, ("arbitrary",)),
    )(*_hbm(proj, proj, proj, dycat, vg, w_s, b2, bd))


def _band_mask():
    qi = lax.broadcasted_iota(jnp.int32, (2 * BLOCK, 2 * BLOCK), 0) % BLOCK
    ki = lax.broadcasted_iota(jnp.int32, (2 * BLOCK, 2 * BLOCK), 1)
    return ((ki < BLOCK) & (ki >= qi)) | ((ki >= BLOCK) & ((ki - BLOCK) <= qi))


def _first_block_bias(blk, blocks_per_class):
    kcol = lax.broadcasted_iota(jnp.int32, (1, 2 * BLOCK), 1)
    kill = jnp.where((blk & (blocks_per_class - 1)) == 0, NEG, 0.0)
    return jnp.where(kcol < BLOCK, kill, 0.0)


def _two_heads(q, lo):
    zero = jnp.zeros_like(q)
    return jnp.concatenate([jnp.where(lo, q, zero), jnp.where(lo, zero, q)], axis=0)


def _block_tokens(blk, d, S):
    if d == 1:
        return pl.ds(pl.multiple_of(blk * BLOCK, BLOCK), BLOCK)
    blocks_per_class = S // d // BLOCK
    r = lax.shift_right_logical(blk, blocks_per_class.bit_length() - 1)
    n = blk & (blocks_per_class - 1)
    return pl.ds(r + n * (BLOCK * d), BLOCK, stride=d)


def _padded_block(blk):
    return pl.ds(pl.multiple_of((blk + 1) * BLOCK, BLOCK), BLOCK)


def _for_blocks(n_blocks, unroll, fn):
    def group(g, carry):
        for u in range(unroll):
            fn(g * unroll + u)
        return carry
    lax.fori_loop(0, n_blocks // unroll, group, 0)


def _attn_fwd(proj, qg2, kg2, bd):
    S = proj.shape[0]
    npairs = ATTN_WIDTH // 128
    tn = 512

    def body(q_ref, k_ref, v_ref, g_ref, qg_ref, kg_ref, bd_ref, y_ref, att_ref, lse_ref, qn, kn, kc, vc):
        bdv = bd_ref[...]
        lo = lax.broadcasted_iota(jnp.int32, (BLOCK, 128), 1) < HEAD_DIM
        band_mask = _band_mask()
        kc[pl.ds(0, BLOCK), :] = jnp.zeros((BLOCK, 128), BF16)
        vc[pl.ds(0, BLOCK), :] = jnp.zeros((BLOCK, 128), BF16)

        def norm_step(i, carry):
            rows = pl.ds(pl.multiple_of(i * tn, tn), tn)
            qv = q_ref[rows, :]
            kv = k_ref[rows, :]
            qn[rows, :] = (qv * lax.rsqrt(_split_dot(qv * qv, bdv) * (1.0 / HEAD_DIM) + EPS)) * (qg_ref[...] * SCALE)
            kn[rows, :] = (kv * lax.rsqrt(_split_dot(kv * kv, bdv) * (1.0 / HEAD_DIM) + EPS)) * kg_ref[...]
            return carry
        lax.fori_loop(0, S // tn, norm_step, 0)

        def fill(blk, d):
            tokens = _block_tokens(blk, d, S)
            kc[_padded_block(blk), :] = kn[tokens, :].astype(BF16)
            vc[_padded_block(blk), :] = v_ref[tokens, :].astype(BF16)

        def block(blk, d):
            tokens = _block_tokens(blk, d, S)
            keys = pl.ds(pl.multiple_of(blk * BLOCK, BLOCK), 2 * BLOCK)
            q2 = _two_heads(qn[tokens, :].astype(BF16), lo)
            s = jnp.where(band_mask, _nt(q2, kc[keys, :]), NEG) + _first_block_bias(blk, S // d // BLOCK)
            m = jnp.max(s, axis=-1, keepdims=True)
            e = jnp.exp(s - m)
            l = jnp.sum(e, axis=-1, keepdims=True)
            o2 = jnp.dot(e.astype(BF16), vc[keys, :], preferred_element_type=F32) * (1.0 / l)
            lse2 = m + jnp.log(l)
            o = jnp.where(lo, o2[:BLOCK], o2[BLOCK:])
            lse = jnp.where(lo, lse2[:BLOCK], lse2[BLOCK:])
            if d > 1:
                la = lse_ref[tokens, :]
                mx = jnp.maximum(la, lse)
                wa, wb = jnp.exp(la - mx), jnp.exp(lse - mx)
                t = wa + wb
                o = (wa * att_ref[tokens, :] + wb * o) / t
                lse = mx + jnp.log(t)
            att_ref[tokens, :] = o
            lse_ref[tokens, :] = lse

        for d in DILATIONS:
            _for_blocks(S // BLOCK, 4, functools.partial(fill, d=d))
            _for_blocks(S // BLOCK, 8, functools.partial(block, d=d))

        def gate_step(i, carry):
            rows = pl.ds(pl.multiple_of(i * tn, tn), tn)
            silu, _ = _silu_parts(g_ref[rows, :])
            y_ref[rows, :] = (att_ref[rows, :] * silu).astype(BF16)
            return carry
        lax.fori_loop(0, S // tn, gate_step, 0)

    col = lambda j0: pl.BlockSpec((S, 128), lambda p, j0=j0: (0, j0 + p))
    const = lambda shape: pl.BlockSpec(shape, lambda p: (0,) * len(shape))
    out = pl.BlockSpec((S, 128), lambda p: (0, p))
    return pl.pallas_call(
        body, name="attn_fwd", grid=(npairs,),
        in_specs=[col(COL_AQ), col(COL_AK), col(COL_AV), col(COL_AG), const((1, 128)), const((1, 128)),
                  const((128, 128))],
        out_specs=[out, out, out],
        out_shape=[pltpu.HBM((S, ATTN_WIDTH), BF16), pltpu.HBM((S, ATTN_WIDTH), F32),
                   pltpu.HBM((S, ATTN_WIDTH), F32)],
        scratch_shapes=[pltpu.VMEM((S, 128), F32), pltpu.VMEM((S, 128), F32),
                        pltpu.VMEM((S + BLOCK, 128), BF16), pltpu.VMEM((S + BLOCK, 128), BF16)],
        compiler_params=_params(48, ("arbitrary",)),
    )(*_hbm(proj, proj, proj, proj, qg2, kg2, bd))


def _attn_bwd_prep(proj, dycat, att, bd):
    S = proj.shape[0]
    tm = 512

    def body(g_ref, dy_ref, att_ref, bd_ref, do_ref, dd_ref, dg_ref):
        silu, dsilu = _silu_parts(g_ref[...])
        dy = dy_ref[...]
        at = att_ref[...]
        do = dy * silu
        do_ref[...] = do
        dd_ref[...] = _split_dot(do * at, bd_ref[...])
        dg_ref[...] = (dy * at * dsilu).astype(BF16)

    tile = lambda j0: pl.BlockSpec((tm, 256), lambda i, j, j0=j0: (i, j0 + j))
    return pl.pallas_call(
        body, name="attn_bwd_prep", grid=(S // tm, ATTN_WIDTH // 256),
        in_specs=[tile(COL_AG // 2), tile(1), tile(0), pl.BlockSpec((256, 256), lambda i, j: (0, 0))],
        out_specs=[tile(0), tile(0), tile(0)],
        out_shape=[pltpu.HBM((S, ATTN_WIDTH), F32), pltpu.HBM((S, ATTN_WIDTH), F32),
                   pltpu.HBM((S, ATTN_WIDTH), BF16)],
        compiler_params=_params(32, ("arbitrary", "arbitrary")),
    )(*_hbm(proj, dycat, att, bd))


def _attn_bwd(proj, do, dd, lse, qg2, kg2, bd):
    S = proj.shape[0]
    npairs = ATTN_WIDTH // 128
    tn = 512

    def body(q_ref, k_ref, v_ref, do_ref, dd_ref, lse_ref, qg_ref, kg_ref, bd_ref,
             dq_ref, dk_ref, dv_ref, dqg_ref, dkg_ref,
             qn, kn, kc, vc, dk_own, dv_own, dk_prev, dv_prev, dqa, dka, dva):
        bdv = bd_ref[...]
        lo = lax.broadcasted_iota(jnp.int32, (BLOCK, 128), 1) < HEAD_DIM
        zeros_bf = jnp.zeros((BLOCK, 128), BF16)
        zeros_f = jnp.zeros((BLOCK, 128), F32)
        kc[pl.ds(0, BLOCK), :] = zeros_bf
        vc[pl.ds(0, BLOCK), :] = zeros_bf
        dk_prev[pl.ds(S, BLOCK), :] = zeros_f
        dv_prev[pl.ds(S, BLOCK), :] = zeros_f

        def norm_step(i, carry):
            rows = pl.ds(pl.multiple_of(i * tn, tn), tn)
            qv = q_ref[rows, :]
            kv = k_ref[rows, :]
            qn[rows, :] = (qv * lax.rsqrt(_split_dot(qv * qv, bdv) * (1.0 / HEAD_DIM) + EPS)) * (qg_ref[...] * SCALE)
            kn[rows, :] = (kv * lax.rsqrt(_split_dot(kv * kv, bdv) * (1.0 / HEAD_DIM) + EPS)) * kg_ref[...]
            return carry
        lax.fori_loop(0, S // tn, norm_step, 0)

        kt = lax.broadcasted_iota(jnp.int32, (2 * BLOCK, 2 * BLOCK), 0)
        qt = lax.broadcasted_iota(jnp.int32, (2 * BLOCK, 2 * BLOCK), 1) % BLOCK
        band_mask_t = ((kt < BLOCK) & (kt >= qt)) | ((kt >= BLOCK) & ((kt - BLOCK) <= qt))

        def per_query_row(t):
            tt = t.T
            return jnp.concatenate([tt[0:1, :], tt[HEAD_DIM:HEAD_DIM + 1, :]], axis=1)

        def fill(blk, d):
            tokens = _block_tokens(blk, d, S)
            kc[_padded_block(blk), :] = kn[tokens, :].astype(BF16)
            vc[_padded_block(blk), :] = v_ref[tokens, :].astype(BF16)

        def block(blk, d):
            tokens = _block_tokens(blk, d, S)
            own = pl.ds(pl.multiple_of(blk * BLOCK, BLOCK), BLOCK)
            keys = pl.ds(pl.multiple_of(blk * BLOCK, BLOCK), 2 * BLOCK)
            q2 = _two_heads(qn[tokens, :].astype(BF16), lo)
            do2 = _two_heads(do_ref[tokens, :].astype(BF16), lo)
            lse_row = per_query_row(lse_ref[tokens, :])
            dd_row = per_query_row(dd_ref[tokens, :])
            kb = kc[keys, :]
            vb = vc[keys, :]
            st = jnp.where(band_mask_t, _nt(kb, q2), NEG)
            kill = jnp.where((blk & (S // d // BLOCK - 1)) == 0, NEG, 0.0)
            st = jnp.concatenate([st[:BLOCK] + kill, st[BLOCK:]], axis=0)
            pt = jnp.exp(st - lse_row)
            dst = pt * (_nt(vb, do2) - dd_row)
            ptb = pt.astype(BF16)
            dstb = dst.astype(BF16)
            dv_band = jnp.dot(ptb, do2, preferred_element_type=F32)
            dk_band = jnp.dot(dstb, q2, preferred_element_type=F32)
            dv_prev[own, :] = dv_band[:BLOCK]
            dv_own[own, :] = dv_band[BLOCK:]
            dk_prev[own, :] = dk_band[:BLOCK]
            dk_own[own, :] = dk_band[BLOCK:]
            dq2 = _tn(dstb, kb)
            dq = jnp.where(lo, dq2[:BLOCK], dq2[BLOCK:])
            dqa[tokens, :] = dq if d == 1 else dqa[tokens, :] + dq

        def fold(blk, d):
            tokens = _block_tokens(blk, d, S)
            own = pl.ds(pl.multiple_of(blk * BLOCK, BLOCK), BLOCK)
            dk = dk_own[own, :] + dk_prev[_padded_block(blk), :]
            dv = dv_own[own, :] + dv_prev[_padded_block(blk), :]
            dka[tokens, :] = dk if d == 1 else dka[tokens, :] + dk
            dva[tokens, :] = dv if d == 1 else dva[tokens, :] + dv

        for d in DILATIONS:
            _for_blocks(S // BLOCK, 4, functools.partial(fill, d=d))
            _for_blocks(S // BLOCK, 8, functools.partial(block, d=d))
            _for_blocks(S // BLOCK, 4, functools.partial(fold, d=d))

        def out_step(i, carry):
            dqg, dkg = carry
            rows = pl.ds(pl.multiple_of(i * tn, tn), tn)
            qv = q_ref[rows, :]
            kv = k_ref[rows, :]
            rq = lax.rsqrt(_split_dot(qv * qv, bdv) * (1.0 / HEAD_DIM) + EPS)
            rk = lax.rsqrt(_split_dot(kv * kv, bdv) * (1.0 / HEAD_DIM) + EPS)
            qh = qv * rq
            kh = kv * rk
            dqs = dqa[rows, :] * SCALE
            dkn = dka[rows, :]
            aq = dqs * qg_ref[...]
            ak = dkn * kg_ref[...]
            dq_ref[rows, :] = (rq * (aq - qh * (_split_dot(aq * qh, bdv) * (1.0 / HEAD_DIM)))).astype(BF16)
            dk_ref[rows, :] = (rk * (ak - kh * (_split_dot(ak * kh, bdv) * (1.0 / HEAD_DIM)))).astype(BF16)
            dv_ref[rows, :] = dva[rows, :].astype(BF16)
            dqg = dqg + jnp.sum(dqs * qh, axis=0, keepdims=True)
            dkg = dkg + jnp.sum(dkn * kh, axis=0, keepdims=True)
            return dqg, dkg
        zero = jnp.zeros((1, 128), F32)
        dqg, dkg = lax.fori_loop(0, S // tn, out_step, (zero, zero))
        dqg_ref[0] = dqg
        dkg_ref[0] = dkg

    once = pl.Buffered(1)
    col = lambda j0: pl.BlockSpec((S, 128), lambda p, j0=j0: (0, j0 + p), pipeline_mode=once)
    const = lambda shape: pl.BlockSpec(shape, lambda p: (0,) * len(shape))
    out = pl.BlockSpec((S, 128), lambda p: (0, p))
    gain_out = pl.BlockSpec((1, 1, 128), lambda p: (p, 0, 0))
    piece = pltpu.HBM((S, ATTN_WIDTH), BF16)
    gains = pltpu.HBM((npairs, 1, 128), F32)
    f32buf = pltpu.VMEM((S, 128), F32)
    f32pad = pltpu.VMEM((S + BLOCK, 128), F32)
    bf16pad = pltpu.VMEM((S + BLOCK, 128), BF16)
    return pl.pallas_call(
        body, name="attn_bwd", grid=(npairs,),
        in_specs=[col(COL_AQ), col(COL_AK), col(COL_AV), col(0), col(0), col(0), const((1, 128)), const((1, 128)),
                  const((128, 128))],
        out_specs=[out, out, out, gain_out, gain_out],
        out_shape=[piece, piece, piece, gains, gains],
        scratch_shapes=[f32buf, f32buf, bf16pad, bf16pad, f32buf, f32buf, f32pad, f32pad, f32buf, f32buf, f32buf],
        compiler_params=_params(48, ("arbitrary",)),
    )(*_hbm(proj, proj, proj, do, dd, lse, qg2, kg2, bd))


def _mem_kv(mem, gain, wkv_bf, kg4, bd):
    def body(mem_ref, g_ref, w_ref, kg_ref, bd_ref, hm_ref, kraw_ref, mk_ref, mv_ref):
        mv_ = mem_ref[...]
        r = lax.rsqrt(jnp.mean(mv_ * mv_, axis=-1, keepdims=True) + EPS)
        hm = ((mv_ * r) * g_ref[...]).astype(BF16)
        hm_ref[...] = hm
        kv = jnp.dot(hm, w_ref[...], preferred_element_type=F32)
        kraw = kv[:, :MEM_WIDTH]
        kraw_ref[...] = kraw
        ms = _split_dot(kraw * kraw, bd_ref[...]) * (1.0 / HEAD_DIM)
        mk_ref[...] = (kraw * lax.rsqrt(ms + EPS)) * kg_ref[...]
        mv_ref[...] = kv[:, MEM_WIDTH:]

    sq = jax.ShapeDtypeStruct((MEM_LEN, MEM_WIDTH), F32)
    return pl.pallas_call(
        body, name="mem_kv",
        out_shape=[jax.ShapeDtypeStruct((MEM_LEN, D_MODEL), BF16), sq, sq, sq],
        compiler_params=_params(16),
    )(mem, gain, wkv_bf, kg4, bd)


def _mem_fwd(proj, mk, mv, qg4, bd):
    S = proj.shape[0]
    tm = 512

    def body(q_ref, g_ref, mk_ref, mv_ref, qg_ref, bd_ref, y_ref, om_ref):
        qv = q_ref[...]
        ms = _split_dot(qv * qv, bd_ref[...]) * (1.0 / HEAD_DIM)
        qs = (qv * lax.rsqrt(ms + EPS)) * (qg_ref[...] * SCALE)
        mkb = mk_ref[...].astype(BF16)
        mvb = mv_ref[...].astype(BF16)
        head = _head_index((tm, MEM_WIDTH))
        o = jnp.zeros((tm, MEM_WIDTH), F32)
        for h in range(4):
            s = _nt(jnp.where(head == h, qs, 0.0).astype(BF16), mkb)
            e = jnp.exp(s - jnp.max(s, axis=-1, keepdims=True))
            p = e * (1.0 / jnp.sum(e, axis=-1, keepdims=True))
            o = jnp.where(head == h, jnp.dot(p.astype(BF16), mvb, preferred_element_type=F32), o)
        om_ref[...] = o
        silu, _ = _silu_parts(g_ref[...])
        y_ref[...] = (o * silu).astype(BF16)

    col = lambda j: pl.BlockSpec((tm, MEM_WIDTH), lambda i, j=j: (i, j))
    const = lambda shape: pl.BlockSpec(shape, lambda i: (0,) * len(shape))
    tile = pl.BlockSpec((tm, MEM_WIDTH), lambda i: (i, 0))
    return pl.pallas_call(
        body, name="mem_fwd", grid=(S // tm,),
        in_specs=[col(11), col(12), const((MEM_LEN, MEM_WIDTH)), const((MEM_LEN, MEM_WIDTH)), const((1, MEM_WIDTH)),
                  const((MEM_WIDTH, MEM_WIDTH))],
        out_specs=[tile, tile],
        out_shape=[pltpu.HBM((S, MEM_WIDTH), BF16), pltpu.HBM((S, MEM_WIDTH), F32)],
        compiler_params=_params(24, ("arbitrary",)),
    )(*_hbm(proj, proj, mk, mv, qg4, bd))


def _mem_bwd(proj, dycat, om, mk, mv, qg4, bd):
    S = proj.shape[0]
    tm = 512

    def body(q_ref, g_ref, dy_ref, om_ref, mk_ref, mv_ref, qg_ref, bd_ref,
             dq_ref, dg_ref, dmk_ref, dmv_ref, dqg_ref):
        i = pl.program_id(0)

        @pl.when(i == 0)
        def _():
            dmk_ref[...] = jnp.zeros_like(dmk_ref)
            dmv_ref[...] = jnp.zeros_like(dmv_ref)
            dqg_ref[...] = jnp.zeros_like(dqg_ref)

        bdv = bd_ref[...]
        qv = q_ref[...]
        rq = lax.rsqrt(_split_dot(qv * qv, bdv) * (1.0 / HEAD_DIM) + EPS)
        qh = qv * rq
        qs = qh * (qg_ref[...] * SCALE)
        silu, dsilu = _silu_parts(g_ref[...])
        dy = dy_ref[...]
        o = om_ref[...]
        do = dy * silu
        dg_ref[...] = (dy * o * dsilu).astype(BF16)
        dd = _split_dot(do * o, bdv)
        mkb = mk_ref[...].astype(BF16)
        mvb = mv_ref[...].astype(BF16)
        head = _head_index((tm, MEM_WIDTH))
        dqs = jnp.zeros((tm, MEM_WIDTH), F32)
        for h in range(4):
            qhd = jnp.where(head == h, qs, 0.0).astype(BF16)
            doh = jnp.where(head == h, do, 0.0).astype(BF16)
            s = _nt(qhd, mkb)
            e = jnp.exp(s - jnp.max(s, axis=-1, keepdims=True))
            p = e * (1.0 / jnp.sum(e, axis=-1, keepdims=True))
            ds = p * (_nt(doh, mvb) - dd[:, h * HEAD_DIM:h * HEAD_DIM + 1])
            dsb = ds.astype(BF16)
            dmv_ref[...] += _tn(p.astype(BF16), doh)
            dmk_ref[...] += _tn(dsb, qhd)
            dqs = jnp.where(head == h, jnp.dot(dsb, mkb, preferred_element_type=F32), dqs)
        dqs = dqs * SCALE
        a = dqs * qg_ref[...]
        dq_ref[...] = (rq * (a - qh * (_split_dot(a * qh, bdv) * (1.0 / HEAD_DIM)))).astype(BF16)
        dqg_ref[...] += jnp.sum(dqs * qh, axis=0, keepdims=True)

    col = lambda j: pl.BlockSpec((tm, MEM_WIDTH), lambda i, j=j: (i, j))
    const = lambda shape: pl.BlockSpec(shape, lambda i: (0,) * len(shape))
    tile = pl.BlockSpec((tm, MEM_WIDTH), lambda i: (i, 0))
    piece = pltpu.HBM((S, MEM_WIDTH), BF16)
    sq = pltpu.HBM((MEM_LEN, MEM_WIDTH), F32)
    return pl.pallas_call(
        body, name="mem_bwd", grid=(S // tm,),
        in_specs=[col(11), col(12), col(3), tile, const((MEM_LEN, MEM_WIDTH)), const((MEM_LEN, MEM_WIDTH)),
                  const((1, MEM_WIDTH)), const((MEM_WIDTH, MEM_WIDTH))],
        out_specs=[tile, tile, const((MEM_LEN, MEM_WIDTH)), const((MEM_LEN, MEM_WIDTH)), const((1, MEM_WIDTH))],
        out_shape=[piece, piece, sq, sq, pltpu.HBM((1, MEM_WIDTH), F32)],
        compiler_params=_params(32, ("arbitrary",)),
    )(*_hbm(proj, proj, dycat, om, mk, mv, qg4, bd))


def _mem_kv_bwd(dmk, dmv, kraw, mem, gain, kg4, wkv_bf, hm_bf, bd):
    def body(dmk_ref, dmv_ref, kraw_ref, mem_ref, g_ref, kg_ref, w_ref, hm_ref, bd_ref, dw_ref, dg_ref, dkg_ref):
        bdv = bd_ref[...]
        kraw = kraw_ref[...]
        rk = lax.rsqrt(_split_dot(kraw * kraw, bdv) * (1.0 / HEAD_DIM) + EPS)
        kh = kraw * rk
        dmkv = dmk_ref[...]
        a = dmkv * kg_ref[...]
        dkraw = rk * (a - kh * (_split_dot(a * kh, bdv) * (1.0 / HEAD_DIM)))
        dkg_ref[...] = jnp.sum(dmkv * kh, axis=0, keepdims=True)
        dkv = jnp.concatenate([dkraw, dmv_ref[...]], axis=1).astype(BF16)
        dw = _tn(hm_ref[...], dkv).astype(BF16)
        rows_blk = D_MODEL // N_DEV
        for j in range(N_DEV):
            dw_ref[j] = dw[rows_blk * j:rows_blk * (j + 1)]
        dhm = _nt(dkv, w_ref[...])
        mv_ = mem_ref[...]
        r = lax.rsqrt(jnp.mean(mv_ * mv_, axis=-1, keepdims=True) + EPS)
        dg_ref[...] = jnp.sum(dhm * (mv_ * r), axis=0, keepdims=True)

    return pl.pallas_call(
        body, name="mem_kv_bwd",
        out_shape=[jax.ShapeDtypeStruct((N_DEV, D_MODEL // N_DEV, 2 * MEM_WIDTH), BF16),
                   jax.ShapeDtypeStruct((1, D_MODEL), F32), jax.ShapeDtypeStruct((1, MEM_WIDTH), F32)],
        compiler_params=_params(24),
    )(dmk, dmv, kraw, mem, gain, kg4, wkv_bf, hm_bf, bd)


def _out_loss(yg, ya, ym, x, tgt, wout_bf):
    S, D = x.shape
    tm = 256

    nsteps = S // tm
    rows_blk = D // N_DEV

    def body(yg_ref, ya_ref, ym_ref, x_ref, t_ref, w_ref, dout_ref, dycat_ref, dw_ref, loss_ref, acc_ref):
        i = pl.program_id(0)

        @pl.when(i == 0)
        def _():
            acc_ref[...] = jnp.zeros_like(acc_ref)
            loss_ref[...] = jnp.zeros_like(loss_ref)

        ycat = jnp.concatenate([yg_ref[...], ya_ref[...], ym_ref[...]], axis=1)
        w = w_ref[...]
        diff = (x_ref[...] + jnp.dot(ycat, w, preferred_element_type=F32)) - t_ref[...]
        loss_ref[...] += jnp.sum(diff * diff, axis=0, keepdims=True)
        dout = diff * (1.0 / D)
        dout_ref[...] = dout
        db = dout.astype(BF16)
        dycat_ref[...] = _nt(db, w)
        acc_ref[...] += _tn(ycat, db)

        @pl.when(i == nsteps - 1)
        def _():
            for j in range(N_DEV):
                dw_ref[j] = acc_ref[rows_blk * j:rows_blk * (j + 1), :].astype(BF16)

    tile = lambda w: pl.BlockSpec((tm, w), lambda i: (i, 0))
    const = lambda shape: pl.BlockSpec(shape, lambda i: (0,) * len(shape))
    return pl.pallas_call(
        body, name="out_loss", grid=(nsteps,),
        in_specs=[tile(GMLP_WIDTH), tile(ATTN_WIDTH), tile(MEM_WIDTH), tile(D), tile(D), const((D, D))],
        out_specs=[tile(D), tile(D), const((N_DEV, rows_blk, D)), const((1, D))],
        out_shape=[pltpu.HBM((S, D), F32), pltpu.HBM((S, D), F32),
                   pltpu.HBM((N_DEV, rows_blk, D), BF16), pltpu.HBM((1, D), F32)],
        scratch_shapes=[pltpu.VMEM((D, D), F32)],
        compiler_params=_params(40, ("arbitrary",)),
    )(*_hbm(yg, ya, ym, x, tgt, wout_bf))


def _piece_specs(pieces, tm):
    return [pl.BlockSpec((tm, p.shape[1]), lambda i: (i, 0)) for p in pieces]


def _in_bwd_dx(pieces, x, dout, gain, w_t, dw_blocks):
    S, D = x.shape
    N = w_t.shape[0]
    tm = 256
    n = len(pieces)
    nsteps = S // tm
    middle_step = nsteps // 4

    def body(*refs):
        piece_refs = refs[:n]
        x_ref, dout_ref, g_ref, w_ref, dwb_ref, gx_ref, dg_ref, gw_ref = refs[n:n + 8]
        rs = _ReduceScatter([dwb_ref], [gw_ref], *refs[n + 8:])
        i = pl.program_id(0)

        @pl.when(i == 0)
        def _():
            dg_ref[...] = jnp.zeros_like(dg_ref)
            rs.start()

        @pl.when(i == middle_step)
        def _():
            rs.middle()

        dproj = jnp.concatenate([r[...] for r in piece_refs], axis=1)
        dh = jnp.dot(dproj, w_ref[...], preferred_element_type=F32)
        xv = x_ref[...]
        r = lax.rsqrt(jnp.mean(xv * xv, axis=-1, keepdims=True) + EPS)
        xh = xv * r
        a = dh * g_ref[...]
        gx_ref[...] = dout_ref[...] + r * (a - xh * jnp.mean(a * xh, axis=-1, keepdims=True))
        dg_ref[...] += jnp.sum(dh * xh, axis=0, keepdims=True)

        @pl.when(i == nsteps - 1)
        def _():
            rs.finish()

    tile = pl.BlockSpec((tm, D), lambda i: (i, 0))
    const = lambda shape: pl.BlockSpec(shape, lambda i: (0,) * len(shape))
    vmem = pl.BlockSpec(memory_space=pltpu.VMEM)
    return pl.pallas_call(
        body, name="in_bwd_dx", grid=(nsteps,),
        in_specs=_piece_specs(pieces, tm) + [tile, tile, const((1, D)), const((N, D)), vmem],
        out_specs=[tile, const((1, D)), vmem],
        out_shape=[pltpu.HBM((S, D), F32), pltpu.HBM((1, D), F32), jax.ShapeDtypeStruct(dw_blocks.shape[1:], F32)],
        scratch_shapes=_reduce_scatter_scratch([dw_blocks]),
        compiler_params=_params(56, ("arbitrary",)),
    )(*_hbm(*pieces, x, dout, gain, w_t), dw_blocks)


def _in_bwd_dw(pieces, h_bf, others):
    S, D = h_bf.shape
    N = sum(p.shape[1] for p in pieces)
    n_blk = N // N_DEV
    tm = 512
    n = len(pieces)
    k = len(others)
    nsteps = S // tm

    def body(*refs):
        piece_refs = refs[:n]
        h_ref = refs[n]
        other_refs = refs[n + 1:n + 1 + k]
        dw_ref = refs[n + 1 + k]
        sum_refs = refs[n + 2 + k:n + 2 + 2 * k]
        acc_ref = refs[n + 2 + 2 * k]
        rs = _ReduceScatter(other_refs, sum_refs, *refs[n + 3 + 2 * k:])
        i = pl.program_id(0)

        @pl.when(i == 0)
        def _():
            acc_ref[...] = jnp.zeros_like(acc_ref)
            rs.start()

        @pl.when(i == 1)
        def _():
            rs.middle()

        dproj = jnp.concatenate([r[...] for r in piece_refs], axis=1)
        acc_ref[...] += _tn(h_ref[...], dproj)

        @pl.when(i == nsteps - 1)
        def _():
            for j in range(N_DEV):
                dw_ref[j] = acc_ref[:, n_blk * j:n_blk * (j + 1)].T.astype(BF16)
            rs.finish()

    vmem = pl.BlockSpec(memory_space=pltpu.VMEM)
    return pl.pallas_call(
        body, name="in_bwd_dw", grid=(nsteps,),
        in_specs=_piece_specs(pieces, tm) + [pl.BlockSpec((tm, D), lambda i: (i, 0))] + [vmem] * k,
        out_specs=[pl.BlockSpec((N_DEV, n_blk, D), lambda i: (0, 0, 0))] + [vmem] * k,
        out_shape=[pltpu.HBM((N_DEV, n_blk, D), BF16)] + [jax.ShapeDtypeStruct(o.shape[1:], F32) for o in others],
        scratch_shapes=[pltpu.VMEM((D, N), F32)] + _reduce_scatter_scratch(others),
        compiler_params=_params(56, ("arbitrary",)),
    )(*_hbm(*pieces, h_bf), *others)


def _row_step(m):
    return max(t for t in range(16, 257, 16) if m % t == 0)


def _place():
    x, y, c = lax.axis_index("x"), lax.axis_index("y"), lax.axis_index("c")
    chips = [(1 - x, y), (x, 1 - y), (1 - x, 1 - y)]
    return x, y, c, chips


def _all_gather_exchange(srcs, outs, send_sems, recv_sems, local_sems):
    n = len(srcs)
    x, y, c, chips = _place()
    me, sibling = (x, y, c), (x, y, 1 - c)

    def rows(a, px, py, pc):
        m = srcs[a].shape[0]
        return outs[a].at[pl.ds((4 * px + 2 * py + pc) * m, m), :]

    def copy(a, k, block, to, src=None):
        return pltpu.make_async_remote_copy(
            src_ref=rows(a, *block) if src is None else src, dst_ref=rows(a, *block),
            send_sem=send_sems.at[a, k], recv_sem=recv_sems.at[a, k], device_id=to, device_id_type=MESH)

    mine = [pltpu.make_async_copy(srcs[a], rows(a, *me), local_sems.at[a]) for a in range(n)]
    for cp in mine:
        cp.start()
    first = []
    for a in range(n):
        first.append(copy(a, 0, me, sibling, src=srcs[a]))
        first += [copy(a, 1 + j, me, (*chip, c), src=srcs[a]) for j, chip in enumerate(chips)]
    for cp in first:
        cp.start()
    passed = []
    for j, chip in enumerate(chips):
        for a in range(n):
            copy(a, 1 + j, (*chip, c), me).wait_recv()
            fwd = copy(a, 4 + j, (*chip, c), sibling)
            fwd.start()
            passed.append(fwd)
    for a in range(n):
        copy(a, 0, sibling, me).wait_recv()
        for j, chip in enumerate(chips):
            copy(a, 4 + j, (*chip, 1 - c), me).wait_recv()
    for cp in first + passed:
        cp.wait_send()
    for cp in mine:
        cp.wait()


def _gather_weights(shards):
    n = len(shards)

    def body(*refs):
        ins, outs, casts = refs[:n], refs[n:2 * n], refs[2 * n:3 * n]
        for a in range(n):
            tr = _row_step(ins[a].shape[0])

            def cast(i, carry, a=a, tr=tr):
                rows = pl.ds(pl.multiple_of(i * tr, tr), tr)
                casts[a][rows, :] = ins[a][rows, :].astype(BF16)
                return carry
            lax.fori_loop(0, ins[a].shape[0] // tr, cast, 0)
        _all_gather_exchange(casts, outs, *refs[3 * n:])

    vmem = pl.BlockSpec(memory_space=pltpu.VMEM)
    return pl.pallas_call(
        body, name="gather_weights",
        out_shape=[jax.ShapeDtypeStruct((N_DEV * a.shape[0], a.shape[1]), BF16) for a in shards],
        in_specs=[vmem] * n, out_specs=[vmem] * n,
        scratch_shapes=[pltpu.VMEM(a.shape, BF16) for a in shards]
        + [pltpu.SemaphoreType.DMA((n, 7)), pltpu.SemaphoreType.DMA((n, 7)), pltpu.SemaphoreType.DMA((n,))],
        compiler_params=_params(40),
    )(*shards)


ROW_NORM, ROW_MEM_NORM, ROW_V_GAIN, ROW_B, ROW_ATTN_GAINS, ROW_MEM_GAINS, ROW_W_S, ROW_LOSS = 0, 8, 16, 18, 22, 23, 24, 536
SMALL_ROWS = 544


def _gather_small(dgain, dmgain, dvg, db2, dqg, dkg, dmqg, dmkg, dws, sq):
    def body(dgain_ref, dmgain_ref, dvg_ref, db2_ref, dqg_ref, dkg_ref, dmqg_ref, dmkg_ref, dws_ref, sq_ref,
             out_ref, mine, send_sems, recv_sems, local_sems):
        first = lax.broadcasted_iota(jnp.int32, (1, 128), 1) < HEAD_DIM
        for i in range(8):
            cols = slice(128 * i, 128 * (i + 1))
            mine[ROW_NORM + i:ROW_NORM + i + 1, :] = dgain_ref[:, cols]
            mine[ROW_MEM_NORM + i:ROW_MEM_NORM + i + 1, :] = dmgain_ref[:, cols]
            mine[ROW_LOSS + i:ROW_LOSS + i + 1, :] = sq_ref[:, cols]
        mine[ROW_V_GAIN:ROW_V_GAIN + 1, :] = dvg_ref[:, 0:128]
        mine[ROW_V_GAIN + 1:ROW_V_GAIN + 2, :] = dvg_ref[:, 128:256]
        bt = db2_ref[...].T
        for h in range(4):
            mine[ROW_B + h:ROW_B + h + 1, :] = bt[HEAD_DIM * h:HEAD_DIM * h + 1, :]

        def fold_heads(t):
            return t + pltpu.roll(t, HEAD_DIM, axis=1)
        aq = fold_heads(dqg_ref[0] + dqg_ref[1] + dqg_ref[2] + dqg_ref[3])
        ak = fold_heads(dkg_ref[0] + dkg_ref[1] + dkg_ref[2] + dkg_ref[3])
        mine[ROW_ATTN_GAINS:ROW_ATTN_GAINS + 1, :] = jnp.where(first, aq, ak)
        mq = fold_heads(dmqg_ref[:, 0:128] + dmqg_ref[:, 128:256])
        mk = fold_heads(dmkg_ref[:, 0:128] + dmkg_ref[:, 128:256])
        mine[ROW_MEM_GAINS:ROW_MEM_GAINS + 1, :] = jnp.where(first, mq, mk)
        mine[ROW_W_S:ROW_W_S + 4 * CHUNK, :] = dws_ref[...]
        _all_gather_exchange([mine], [out_ref], send_sems, recv_sems, local_sems)

    return pl.pallas_call(
        body, name="gather_small_grads",
        out_shape=jax.ShapeDtypeStruct((N_DEV * SMALL_ROWS, 128), F32),
        scratch_shapes=[pltpu.VMEM((SMALL_ROWS, 128), F32), pltpu.SemaphoreType.DMA((1, 7)),
                        pltpu.SemaphoreType.DMA((1, 7)), pltpu.SemaphoreType.DMA((1,))],
        compiler_params=_params(16),
    )(dgain, dmgain, dvg, db2, dqg, dkg, dmqg, dmkg, dws, sq)


def _reduce_scatter_scratch(arrs):
    n = len(arrs)
    return ([pltpu.VMEM((4,) + a.shape[1:], BF16) for a in arrs] + [pltpu.VMEM((3,) + a.shape[1:], BF16) for a in arrs]
            + [pltpu.SemaphoreType.DMA((n, 7)), pltpu.SemaphoreType.DMA((n, 7))])


class _ReduceScatter:
    def __init__(self, ins, outs, *scratch):
        n = len(ins)
        self.n, self.ins, self.outs = n, ins, outs
        self.half, self.quarter = scratch[:n], scratch[n:2 * n]
        self.send_sems, self.recv_sems = scratch[2 * n:]

    def _to_sibling(self):
        x, y, c, _ = _place()
        return [pltpu.make_async_remote_copy(
            src_ref=self.ins[a].at[2 * q + (1 - c)], dst_ref=self.half[a].at[q], send_sem=self.send_sems.at[a, q],
            recv_sem=self.recv_sems.at[a, q], device_id=(x, y, 1 - c), device_id_type=MESH)
            for a in range(self.n) for q in range(4)]

    def _to_chips(self):
        _, _, c, chips = _place()
        return [pltpu.make_async_remote_copy(
            src_ref=self.half[a].at[2 * chip[0] + chip[1]], dst_ref=self.quarter[a].at[k],
            send_sem=self.send_sems.at[a, 4 + k], recv_sem=self.recv_sems.at[a, 4 + k], device_id=(*chip, c),
            device_id_type=MESH) for a in range(self.n) for k, chip in enumerate(chips)]

    def _rows(self, a, fn):
        m = self.ins[a].shape[1]
        tr = _row_step(m)

        def step(i, carry):
            fn(pl.ds(pl.multiple_of(i * tr, tr), tr))
            return carry
        lax.fori_loop(0, m // tr, step, 0)

    def start(self):
        for cp in self._to_sibling():
            cp.start()

    def middle(self):
        _, _, c, _ = _place()
        for cp in self._to_sibling():
            cp.wait_recv()
        for a in range(self.n):
            for q in range(4):
                def add_half(rows, a=a, q=q):
                    both = self.ins[a][2 * q + c, rows, :].astype(F32) + self.half[a][q, rows, :].astype(F32)
                    self.half[a][q, rows, :] = both.astype(BF16)
                self._rows(a, add_half)
        for cp in self._to_chips():
            cp.start()

    def finish(self):
        x, y, _, _ = _place()
        for cp in self._to_chips():
            cp.wait_recv()
        for a in range(self.n):
            def add_quarters(rows, a=a):
                f = lambda t: t.astype(F32)
                self.outs[a][rows, :] = ((f(self.half[a][2 * x + y, rows, :]) + f(self.quarter[a][0, rows, :]))
                                         + (f(self.quarter[a][1, rows, :]) + f(self.quarter[a][2, rows, :])))
            self._rows(a, add_quarters)
        for cp in self._to_sibling() + self._to_chips():
            cp.wait_send()


def _adamw_math(w, g, m, v):
    m = ADAM_B1 * m + (1.0 - ADAM_B1) * g
    v = ADAM_B2 * v + (1.0 - ADAM_B2) * (g * g)
    m_hat = m / (1.0 - ADAM_B1 ** ADAM_STEP)
    v_hat = v / (1.0 - ADAM_B2 ** ADAM_STEP)
    delta = -ADAM_LR * (m_hat / (jnp.sqrt(v_hat) + ADAM_EPS) + ADAM_WD * w)
    return delta, m, v


def _adamw(w, g, m, v, name):
    R, C = w.shape
    tr = _row_step(R)

    def body(w_ref, g_ref, m_ref, v_ref, d_ref, nm_ref, nv_ref):
        d_ref[...], nm_ref[...], nv_ref[...] = _adamw_math(w_ref[...], g_ref[...], m_ref[...], v_ref[...])

    tile = pl.BlockSpec((tr, C), lambda i: (i, 0))
    out = pltpu.HBM((R, C), F32)
    return pl.pallas_call(
        body, name=name, grid=(R // tr,), in_specs=[tile] * 4, out_specs=[tile] * 3, out_shape=[out] * 3,
        compiler_params=_params(16, ("arbitrary",)),
    )(*_hbm(w, g, m, v))


SMALL = ("norm_gain", "gmlp_v_gain", "gmlp_w_s", "gmlp_b", "attn_q_gain", "attn_k_gain", "mem_norm_gain",
         "mem_q_gain", "mem_k_gain")
WEIGHTS = ("norm_gain", "w_in", "gmlp_v_gain", "gmlp_w_s", "gmlp_b", "attn_q_gain", "attn_k_gain",
           "mem_norm_gain", "w_mem_kv", "mem_q_gain", "mem_k_gain", "w_out")


def _adamw_small(w, m, v, g_all):
    k = len(SMALL)
    half = slice(0, HEAD_DIM), slice(HEAD_DIM, 2 * HEAD_DIM)

    def body(*refs):
        w_refs, m_refs, v_refs = refs[:k], refs[k:2 * k], refs[2 * k:3 * k]
        g_ref = refs[3 * k]
        outs = refs[3 * k + 1:7 * k + 1]
        loss_ref, gsum = refs[7 * k + 1:]

        part = SMALL_ROWS // 4
        for p in range(4):
            acc = g_ref[part * p:part * (p + 1), :]
            for dev in range(1, N_DEV):
                acc = acc + g_ref[dev * SMALL_ROWS + part * p:dev * SMALL_ROWS + part * (p + 1), :]
            gsum[part * p:part * (p + 1), :] = acc

        def update(name, at, g):
            i = SMALL.index(name)
            d, nm, nv = _adamw_math(w_refs[i][at], g, m_refs[i][at], v_refs[i][at])
            outs[i][at], outs[k + i][at], outs[2 * k + i][at], outs[3 * k + i][at] = g, d, nm, nv

        for i in range(8):
            at = (slice(0, 1), slice(128 * i, 128 * (i + 1)))
            update("norm_gain", at, gsum[ROW_NORM + i:ROW_NORM + i + 1, :])
            update("mem_norm_gain", at, gsum[ROW_MEM_NORM + i:ROW_MEM_NORM + i + 1, :])
        for h in range(4):
            row = (0, slice(h, h + 1), slice(None))
            update("gmlp_v_gain", row, gsum[ROW_V_GAIN + h // 2:ROW_V_GAIN + h // 2 + 1, half[h % 2]])
            update("gmlp_b", row, gsum[ROW_B + h:ROW_B + h + 1, :])
            update("gmlp_w_s", (0, h), gsum[ROW_W_S + CHUNK * h:ROW_W_S + CHUNK * (h + 1), :])
        whole = (slice(0, 1), slice(None))
        update("attn_q_gain", whole, gsum[ROW_ATTN_GAINS:ROW_ATTN_GAINS + 1, half[0]])
        update("attn_k_gain", whole, gsum[ROW_ATTN_GAINS:ROW_ATTN_GAINS + 1, half[1]])
        update("mem_q_gain", whole, gsum[ROW_MEM_GAINS:ROW_MEM_GAINS + 1, half[0]])
        update("mem_k_gain", whole, gsum[ROW_MEM_GAINS:ROW_MEM_GAINS + 1, half[1]])
        loss_ref[...] = jnp.sum(gsum[ROW_LOSS:ROW_LOSS + 8, :], keepdims=True) * (0.5 / D_MODEL)

    shapes = [jax.ShapeDtypeStruct(w[name].shape, F32) for name in SMALL]
    res = pl.pallas_call(
        body, name="adamw_small",
        out_shape=shapes * 4 + [jax.ShapeDtypeStruct((1, 1), F32)],
        scratch_shapes=[pltpu.VMEM((SMALL_ROWS, 128), F32)],
        compiler_params=_params(16),
    )(*[w[n] for n in SMALL], *[m[n] for n in SMALL], *[v[n] for n in SMALL], g_all)
    trees = [dict(zip(SMALL, res[j * k:(j + 1) * k])) for j in range(4)]
    return (*trees, res[4 * k])


def _grads(x, mem, tgt, w, win_t, wkv_bf, wout_bf):
    bd128, bd256 = _head_blockdiag(128), _head_blockdiag(256)
    gain = w["norm_gain"].reshape(1, D_MODEL)
    vg = w["gmlp_v_gain"].reshape(1, GMLP_WIDTH)
    w_s = w["gmlp_w_s"].reshape(4, CHUNK, CHUNK)
    b2 = jnp.repeat(w["gmlp_b"].reshape(4, CHUNK).T, HEAD_DIM, axis=1)
    qg2 = jnp.tile(w["attn_q_gain"].reshape(1, HEAD_DIM), (1, 2))
    kg2 = jnp.tile(w["attn_k_gain"].reshape(1, HEAD_DIM), (1, 2))
    mqg4 = jnp.tile(w["mem_q_gain"].reshape(1, HEAD_DIM), (1, 4))
    mkg4 = jnp.tile(w["mem_k_gain"].reshape(1, HEAD_DIM), (1, 4))
    mgain = w["mem_norm_gain"].reshape(1, D_MODEL)

    proj, h_bf = _rms_proj(x, gain, win_t)
    yg = _gmlp_fwd(proj, vg, w_s, b2, bd256)
    ya, att, lse = _attn_fwd(proj, qg2, kg2, bd128)
    hm_bf, kraw, mk, mv = _mem_kv(mem, mgain, wkv_bf, mkg4, bd256)
    ym, om = _mem_fwd(proj, mk, mv, mqg4, bd256)
    dout, dycat, dwout, sq = _out_loss(yg, ya, ym, x, tgt, wout_bf)

    du, dgv, dgg, dws, db2, dvg = _gmlp_bwd(proj, dycat, vg, w_s, b2, bd256)
    do, dd, dag = _attn_bwd_prep(proj, dycat, att, bd256)
    dq, dk, dv, dqg, dkg = _attn_bwd(proj, do, dd, lse, qg2, kg2, bd128)
    dmq, dmg, dmk, dmv, dmqg = _mem_bwd(proj, dycat, om, mk, mv, mqg4, bd256)
    dwkv, dmgain, dmkg = _mem_kv_bwd(dmk, dmv, kraw, mem, mgain, mkg4, wkv_bf, hm_bf, bd256)
    pieces = [du, dgv, dgg, dq, dk, dv, dag, dmq, dmg]
    dwin, g_wkv, g_wout = _in_bwd_dw(pieces, h_bf, [dwkv, dwout])
    grad_x, dgain, g_win = _in_bwd_dx(pieces, x, dout, gain, win_t, dwin)
    return grad_x, g_win, g_wkv, g_wout, (dgain, dmgain, dvg, db2, dqg, dkg, dmqg, dmkg, dws, sq)


def kernel(x, mem, norm_gain, w_in, gmlp_v_gain, gmlp_w_s, gmlp_b, attn_q_gain, attn_k_gain, mem_norm_gain, w_mem_kv, mem_q_gain, mem_k_gain, w_out, loss_target, m_norm_gain, m_w_in, m_gmlp_v_gain, m_gmlp_w_s, m_gmlp_b, m_attn_q_gain, m_attn_k_gain, m_mem_norm_gain, m_w_mem_kv, m_mem_q_gain, m_mem_k_gain, m_w_out, v_norm_gain, v_w_in, v_gmlp_v_gain, v_gmlp_w_s, v_gmlp_b, v_attn_q_gain, v_attn_k_gain, v_mem_norm_gain, v_w_mem_kv, v_mem_q_gain, v_mem_k_gain, v_w_out):
    w = dict(norm_gain=norm_gain, w_in=w_in, gmlp_v_gain=gmlp_v_gain, gmlp_w_s=gmlp_w_s, gmlp_b=gmlp_b,
             attn_q_gain=attn_q_gain, attn_k_gain=attn_k_gain, mem_norm_gain=mem_norm_gain, w_mem_kv=w_mem_kv,
             mem_q_gain=mem_q_gain, mem_k_gain=mem_k_gain, w_out=w_out)
    m = dict(norm_gain=m_norm_gain, w_in=m_w_in, gmlp_v_gain=m_gmlp_v_gain, gmlp_w_s=m_gmlp_w_s, gmlp_b=m_gmlp_b,
             attn_q_gain=m_attn_q_gain, attn_k_gain=m_attn_k_gain, mem_norm_gain=m_mem_norm_gain,
             w_mem_kv=m_w_mem_kv, mem_q_gain=m_mem_q_gain, mem_k_gain=m_mem_k_gain, w_out=m_w_out)
    v = dict(norm_gain=v_norm_gain, w_in=v_w_in, gmlp_v_gain=v_gmlp_v_gain, gmlp_w_s=v_gmlp_w_s, gmlp_b=v_gmlp_b,
             attn_q_gain=v_attn_q_gain, attn_k_gain=v_attn_k_gain, mem_norm_gain=v_mem_norm_gain,
             w_mem_kv=v_w_mem_kv, mem_q_gain=v_mem_q_gain, mem_k_gain=v_mem_k_gain, w_out=v_w_out)
    transposed = lambda t: jnp.transpose(t[0])

    win_t, wkv_bf, wout_bf = _gather_weights([transposed(w_in), w_mem_kv[0], w_out[0]])

    grad_x, g_win, g_wkv, g_wout, small = _grads(x[0], mem[0], loss_target[0], w, win_t, wkv_bf, wout_bf)
    small_all = _gather_small(*small)

    out_g, out_d, out_m, out_v, loss = _adamw_small(w, m, v, small_all)
    d_, m_, v_ = _adamw(transposed(w_in), g_win, transposed(m_w_in), transposed(v_w_in), "adamw_w_in")
    for tree, t in ((out_g, g_win), (out_d, d_), (out_m, m_), (out_v, v_)):
        tree["w_in"] = jnp.transpose(t)[None]
    for name, g in (("w_mem_kv", g_wkv), ("w_out", g_wout)):
        d_, m_, v_ = _adamw(w[name][0], g, m[name][0], v[name][0], "adamw_" + name)
        out_g[name], out_d[name], out_m[name], out_v[name] = g[None], d_[None], m_[None], v_[None]

    return (loss.reshape(()), grad_x[None], *[out_g[k] for k in WEIGHTS], *[out_d[k] for k in WEIGHTS],
            *[out_m[k] for k in WEIGHTS], *[out_v[k] for k in WEIGHTS])
```

```python
import functools
import math

import jax
import jax.numpy as jnp
from jax import lax
from jax.experimental import pallas as pl
from jax.experimental.pallas import tpu as pltpu

F32 = jnp.float32
BF16 = jnp.bfloat16

N_DEV = 8
D_MODEL = 1024
HEAD_DIM = 64
GMLP_WIDTH = 256
ATTN_WIDTH = 512
MEM_WIDTH = 256
MEM_LEN = 256
IN_WIDTH = 3 * GMLP_WIDTH + 4 * ATTN_WIDTH + 2 * MEM_WIDTH
CHUNK = 128
BLOCK = 128
DILATIONS = (1, 4, 16)
EPS = 1e-6
SCALE = 1.0 / math.sqrt(HEAD_DIM)
NEG = -1e30

ADAM_LR = 0.001
ADAM_B1 = 0.9
ADAM_B2 = 0.999
ADAM_EPS = 1e-08
ADAM_WD = 0.01
ADAM_STEP = 10

MIB = 1024 * 1024
MESH = pl.DeviceIdType.MESH

COL_AQ, COL_AK, COL_AV, COL_AG = 6, 10, 14, 18


def _params(vmem_mib, semantics=None):
    kw = dict(vmem_limit_bytes=vmem_mib * MIB)
    if semantics is not None:
        kw["dimension_semantics"] = semantics
    return pltpu.CompilerParams(**kw)


def _hbm(*arrs):
    return [pltpu.with_memory_space_constraint(a, pltpu.HBM) for a in arrs]


def _split_dot(x, sel_bf):
    hi = x.astype(BF16)
    lo = (x - hi.astype(F32)).astype(BF16)
    return jnp.dot(hi, sel_bf, preferred_element_type=F32) + jnp.dot(lo, sel_bf, preferred_element_type=F32)


def _nt(a, b):
    return lax.dot_general(a, b, (((1,), (1,)), ((), ())), preferred_element_type=F32)


def _tn(a, b):
    return lax.dot_general(a, b, (((0,), (0,)), ((), ())), preferred_element_type=F32)


def _silu_parts(g):
    sg = jax.nn.sigmoid(g)
    return g * sg, sg * (1.0 + g * (1.0 - sg))


def _head_index(shape):
    return lax.shift_right_logical(lax.broadcasted_iota(jnp.int32, shape, 1), HEAD_DIM.bit_length() - 1)


def _head_blockdiag(width):
    i = jnp.arange(width) // HEAD_DIM
    return (i[:, None] == i[None, :]).astype(BF16)


def _rms_proj(x, gain, w_t):
    S, D = x.shape
    N = w_t.shape[0]
    tm = 256

    def body(x_ref, g_ref, w_ref, proj_ref, h_ref):
        xv = x_ref[...]
        r = lax.rsqrt(jnp.mean(xv * xv, axis=-1, keepdims=True) + EPS)
        h = ((xv * r) * g_ref[...]).astype(BF16)
        h_ref[...] = h
        proj_ref[...] = _nt(h, w_ref[...])

    return pl.pallas_call(
        body, name="rms_proj", grid=(S // tm,),
        in_specs=[pl.BlockSpec((tm, D), lambda i: (i, 0)), pl.BlockSpec((1, D), lambda i: (0, 0)),
                  pl.BlockSpec((N, D), lambda i: (0, 0))],
        out_specs=[pl.BlockSpec((tm, N), lambda i: (i, 0)), pl.BlockSpec((tm, D), lambda i: (i, 0))],
        out_shape=[pltpu.HBM((S, N), F32), pltpu.HBM((S, D), BF16)],
        compiler_params=_params(40, ("arbitrary",)),
    )(*_hbm(x, gain, w_t))


def _gmlp_masked_weights(ws_ref, transpose):
    t = lax.broadcasted_iota(jnp.int32, (CHUNK, CHUNK), 0)
    s = lax.broadcasted_iota(jnp.int32, (CHUNK, CHUNK), 1)
    parts = []
    for h in range(4):
        wm = jnp.where(s <= t, ws_ref[h], 0.0)
        parts.append(wm.T if transpose else wm)
    return jnp.concatenate(parts, axis=1).astype(BF16)


def _head_stack(v, head):
    return jnp.concatenate([jnp.where(head == h, v, 0.0) for h in range(4)], axis=0).astype(BF16)


def _gmlp_fwd(proj, vg, w_s, b2, bd):
    S = proj.shape[0]
    tm = 512

    def body(u_ref, v_ref, g_ref, vg_ref, ws_ref, b2_ref, bd_ref, y_ref):
        v = v_ref[...]
        ms = _split_dot(v * v, bd_ref[...]) * (1.0 / HEAD_DIM)
        vn = (v * lax.rsqrt(ms + EPS)) * vg_ref[...]
        wcat = _gmlp_masked_weights(ws_ref, False)
        head = _head_index((CHUNK, GMLP_WIDTH))
        for c in range(tm // CHUNK):
            rows = slice(c * CHUNK, (c + 1) * CHUNK)
            sp = jnp.dot(wcat, _head_stack(vn[rows], head), preferred_element_type=F32) + b2_ref[...]
            silu, _ = _silu_parts(g_ref[rows, :])
            y_ref[rows, :] = ((u_ref[rows, :] * sp) * silu).astype(BF16)

    col = lambda j: pl.BlockSpec((tm, GMLP_WIDTH), lambda i, j=j: (i, j))
    const = lambda shape: pl.BlockSpec(shape, lambda i: (0,) * len(shape))
    return pl.pallas_call(
        body, name="gmlp_fwd", grid=(S // tm,),
        in_specs=[col(0), col(1), col(2), const((1, GMLP_WIDTH)), const((4, CHUNK, CHUNK)),
                  const((CHUNK, GMLP_WIDTH)), const((GMLP_WIDTH, GMLP_WIDTH))],
        out_specs=pl.BlockSpec((tm, GMLP_WIDTH), lambda i: (i, 0)),
        out_shape=pltpu.HBM((S, GMLP_WIDTH), BF16),
        compiler_params=_params(24, ("arbitrary",)),
    )(*_hbm(proj, proj, proj, vg, w_s, b2, bd))


def _gmlp_bwd(proj, dycat, vg, w_s, b2, bd):
    S = proj.shape[0]
    tm = 512
    nsteps = S // tm

    def body(u_ref, v_ref, g_ref, dy_ref, vg_ref, ws_ref, b2_ref, bd_ref,
             du_ref, dv_ref, dg_ref, dws_ref, db2_ref, dvg_ref):
        i = pl.program_id(0)

        @pl.when(i == 0)
        def _():
            dws_ref[...] = jnp.zeros_like(dws_ref)
            db2_ref[...] = jnp.zeros_like(db2_ref)
            dvg_ref[...] = jnp.zeros_like(dvg_ref)

        bdv = bd_ref[...]
        v = v_ref[...]
        ms = _split_dot(v * v, bdv) * (1.0 / HEAD_DIM)
        rv = lax.rsqrt(ms + EPS)
        xhat = v * rv
        vgv = vg_ref[...]
        vn = xhat * vgv
        wcat = _gmlp_masked_weights(ws_ref, False)
        wcat_t = _gmlp_masked_weights(ws_ref, True)
        head = _head_index((CHUNK, GMLP_WIDTH))
        dvg = jnp.zeros((1, GMLP_WIDTH), F32)
        for c in range(tm // CHUNK):
            rows = slice(c * CHUNK, (c + 1) * CHUNK)
            vn_c = vn[rows]
            spb = jnp.dot(wcat, _head_stack(vn_c, head), preferred_element_type=F32) + b2_ref[...]
            silu, dsilu = _silu_parts(g_ref[rows, :])
            dy = dy_ref[rows, :]
            u = u_ref[rows, :]
            du_ref[rows, :] = (dy * spb * silu).astype(BF16)
            dg_ref[rows, :] = (dy * u * spb * dsilu).astype(BF16)
            dsp = dy * u * silu
            db2_ref[...] += dsp
            dstack = _head_stack(dsp, head)
            dvn = jnp.dot(wcat_t, dstack, preferred_element_type=F32)
            dws_ref[...] += _nt(dstack, vn_c.astype(BF16))
            xh = xhat[rows]
            a = dvn * vgv
            mean_ax = _split_dot(a * xh, bdv) * (1.0 / HEAD_DIM)
            dv_ref[rows, :] = (rv[rows] * (a - xh * mean_ax)).astype(BF16)
            dvg = dvg + jnp.sum(dvn * xh, axis=0, keepdims=True)
        dvg_ref[...] += dvg

        @pl.when(i == nsteps - 1)
        def _():
            t = lax.broadcasted_iota(jnp.int32, (4 * CHUNK, CHUNK), 0) % CHUNK
            s = lax.broadcasted_iota(jnp.int32, (4 * CHUNK, CHUNK), 1)
            dws_ref[...] = jnp.where(s <= t, dws_ref[...], 0.0)
            db2_ref[...] = _split_dot(db2_ref[...], bdv)

    col = lambda j: pl.BlockSpec((tm, GMLP_WIDTH), lambda i, j=j: (i, j))
    const = lambda shape: pl.BlockSpec(shape, lambda i: (0,) * len(shape))
    tile = pl.BlockSpec((tm, GMLP_WIDTH), lambda i: (i, 0))
    piece = pltpu.HBM((S, GMLP_WIDTH), BF16)
    return pl.pallas_call(
        body, name="gmlp_bwd", grid=(nsteps,),
        in_specs=[col(0), col(1), col(2), col(0), const((1, GMLP_WIDTH)), const((4, CHUNK, CHUNK)),
                  const((CHUNK, GMLP_WIDTH)), const((GMLP_WIDTH, GMLP_WIDTH))],
        out_specs=[tile, tile, tile, const((4 * CHUNK, CHUNK)), const((CHUNK, GMLP_WIDTH)), const((1, GMLP_WIDTH))],
        out_shape=[piece, piece, piece, pltpu.HBM((4 * CHUNK, CHUNK), F32),
                   pltpu.HBM((CHUNK, GMLP_WIDTH), F32), pltpu.HBM((1, GMLP_WIDTH), F32)],
        compiler_params=_params(32, ("arbitrary",)),
    )(*_hbm(proj, proj, proj, dycat, vg, w_s, b2, bd))


def _band_mask():
    qi = lax.broadcasted_iota(jnp.int32, (2 * BLOCK, 2 * BLOCK), 0) % BLOCK
    ki = lax.broadcasted_iota(jnp.int32, (2 * BLOCK, 2 * BLOCK), 1)
    return ((ki < BLOCK) & (ki >= qi)) | ((ki >= BLOCK) & ((ki - BLOCK) <= qi))


def _first_block_bias(blk, blocks_per_class):
    kcol = lax.broadcasted_iota(jnp.int32, (1, 2 * BLOCK), 1)
    kill = jnp.where((blk & (blocks_per_class - 1)) == 0, NEG, 0.0)
    return jnp.where(kcol < BLOCK, kill, 0.0)


def _two_heads(q, lo):
    zero = jnp.zeros_like(q)
    return jnp.concatenate([jnp.where(lo, q, zero), jnp.where(lo, zero, q)], axis=0)


def _block_tokens(blk, d, S):
    if d == 1:
        return pl.ds(pl.multiple_of(blk * BLOCK, BLOCK), BLOCK)
    blocks_per_class = S // d // BLOCK
    r = lax.shift_right_logical(blk, blocks_per_class.bit_length() - 1)
    n = blk & (blocks_per_class - 1)
    return pl.ds(r + n * (BLOCK * d), BLOCK, stride=d)


def _padded_block(blk):
    return pl.ds(pl.multiple_of((blk + 1) * BLOCK, BLOCK), BLOCK)


def _for_blocks(n_blocks, unroll, fn):
    def group(g, carry):
        for u in range(unroll):
            fn(g * unroll + u)
        return carry
    lax.fori_loop(0, n_blocks // unroll, group, 0)


def _attn_fwd(proj, qg2, kg2, bd):
    S = proj.shape[0]
    npairs = ATTN_WIDTH // 128
    tn = 512

    def body(q_ref, k_ref, v_ref, g_ref, qg_ref, kg_ref, bd_ref, y_ref, att_ref, lse_ref, qn, kn, kc, vc):
        bdv = bd_ref[...]
        lo = lax.broadcasted_iota(jnp.int32, (BLOCK, 128), 1) < HEAD_DIM
        band_mask = _band_mask()
        kc[pl.ds(0, BLOCK), :] = jnp.zeros((BLOCK, 128), BF16)
        vc[pl.ds(0, BLOCK), :] = jnp.zeros((BLOCK, 128), BF16)

        def norm_step(i, carry):
            rows = pl.ds(pl.multiple_of(i * tn, tn), tn)
            qv = q_ref[rows, :]
            kv = k_ref[rows, :]
            qn[rows, :] = (qv * lax.rsqrt(_split_dot(qv * qv, bdv) * (1.0 / HEAD_DIM) + EPS)) * (qg_ref[...] * SCALE)
            kn[rows, :] = (kv * lax.rsqrt(_split_dot(kv * kv, bdv) * (1.0 / HEAD_DIM) + EPS)) * kg_ref[...]
            return carry
        lax.fori_loop(0, S // tn, norm_step, 0)

        def fill(blk, d):
            tokens = _block_tokens(blk, d, S)
            kc[_padded_block(blk), :] = kn[tokens, :].astype(BF16)
            vc[_padded_block(blk), :] = v_ref[tokens, :].astype(BF16)

        def block(blk, d):
            tokens = _block_tokens(blk, d, S)
            keys = pl.ds(pl.multiple_of(blk * BLOCK, BLOCK), 2 * BLOCK)
            q2 = _two_heads(qn[tokens, :].astype(BF16), lo)
            s = jnp.where(band_mask, _nt(q2, kc[keys, :]), NEG) + _first_block_bias(blk, S // d // BLOCK)
            m = jnp.max(s, axis=-1, keepdims=True)
            e = jnp.exp(s - m)
            l = jnp.sum(e, axis=-1, keepdims=True)
            o2 = jnp.dot(e.astype(BF16), vc[keys, :], preferred_element_type=F32) * (1.0 / l)
            lse2 = m + jnp.log(l)
            o = jnp.where(lo, o2[:BLOCK], o2[BLOCK:])
            lse = jnp.where(lo, lse2[:BLOCK], lse2[BLOCK:])
            if d > 1:
                la = lse_ref[tokens, :]
                mx = jnp.maximum(la, lse)
                wa, wb = jnp.exp(la - mx), jnp.exp(lse - mx)
                t = wa + wb
                o = (wa * att_ref[tokens, :] + wb * o) / t
                lse = mx + jnp.log(t)
            att_ref[tokens, :] = o
            lse_ref[tokens, :] = lse

        for d in DILATIONS:
            _for_blocks(S // BLOCK, 4, functools.partial(fill, d=d))
            _for_blocks(S // BLOCK, 8, functools.partial(block, d=d))

        def gate_step(i, carry):
            rows = pl.ds(pl.multiple_of(i * tn, tn), tn)
            silu, _ = _silu_parts(g_ref[rows, :])
            y_ref[rows, :] = (att_ref[rows, :] * silu).astype(BF16)
            return carry
        lax.fori_loop(0, S // tn, gate_step, 0)

    col = lambda j0: pl.BlockSpec((S, 128), lambda p, j0=j0: (0, j0 + p))
    const = lambda shape: pl.BlockSpec(shape, lambda p: (0,) * len(shape))
    out = pl.BlockSpec((S, 128), lambda p: (0, p))
    return pl.pallas_call(
        body, name="attn_fwd", grid=(npairs,),
        in_specs=[col(COL_AQ), col(COL_AK), col(COL_AV), col(COL_AG), const((1, 128)), const((1, 128)),
                  const((128, 128))],
        out_specs=[out, out, out],
        out_shape=[pltpu.HBM((S, ATTN_WIDTH), BF16), pltpu.HBM((S, ATTN_WIDTH), F32),
                   pltpu.HBM((S, ATTN_WIDTH), F32)],
        scratch_shapes=[pltpu.VMEM((S, 128), F32), pltpu.VMEM((S, 128), F32),
                        pltpu.VMEM((S + BLOCK, 128), BF16), pltpu.VMEM((S + BLOCK, 128), BF16)],
        compiler_params=_params(48, ("arbitrary",)),
    )(*_hbm(proj, proj, proj, proj, qg2, kg2, bd))


def _attn_bwd(proj, dycat, att, lse, qg2, kg2, bd):
    S = proj.shape[0]
    npairs = ATTN_WIDTH // 128
    tn = 512

    def body(q_ref, k_ref, v_ref, g_ref, dy_ref, att_ref, lse_ref, qg_ref, kg_ref, bd_ref,
             dq_ref, dk_ref, dv_ref, dg_ref, dqg_ref, dkg_ref,
             qn, kn, rq_s, rk_s, kc, vc, do_s, dd_s, dqa, dka, dva):
        bdv = bd_ref[...]
        lo = lax.broadcasted_iota(jnp.int32, (BLOCK, 128), 1) < HEAD_DIM
        kc[pl.ds(0, BLOCK), :] = jnp.zeros((BLOCK, 128), BF16)
        vc[pl.ds(0, BLOCK), :] = jnp.zeros((BLOCK, 128), BF16)

        def prepare(i, carry):
            rows = pl.ds(pl.multiple_of(i * tn, tn), tn)
            qv = q_ref[rows, :]
            kv = k_ref[rows, :]
            rq = lax.rsqrt(_split_dot(qv * qv, bdv) * (1.0 / HEAD_DIM) + EPS)
            rk = lax.rsqrt(_split_dot(kv * kv, bdv) * (1.0 / HEAD_DIM) + EPS)
            rq_s[rows, :] = rq
            rk_s[rows, :] = rk
            qn[rows, :] = (qv * rq) * (qg_ref[...] * SCALE)
            kn[rows, :] = (kv * rk) * kg_ref[...]
            silu, dsilu = _silu_parts(g_ref[rows, :])
            dy = dy_ref[rows, :]
            at = att_ref[rows, :]
            do = dy * silu
            do_s[rows, :] = do
            dd_s[rows, :] = _split_dot(do * at, bdv)
            dg_ref[rows, :] = (dy * at * dsilu).astype(BF16)
            dka[rows, :] = jnp.zeros((tn, 128), F32)
            dva[rows, :] = jnp.zeros((tn, 128), F32)
            return carry
        lax.fori_loop(0, S // tn, prepare, 0)

        kt = lax.broadcasted_iota(jnp.int32, (2 * BLOCK, 2 * BLOCK), 0)
        qt = lax.broadcasted_iota(jnp.int32, (2 * BLOCK, 2 * BLOCK), 1) % BLOCK
        band_mask_t = ((kt < BLOCK) & (kt >= qt)) | ((kt >= BLOCK) & ((kt - BLOCK) <= qt))

        def per_query_row(t):
            tt = t.T
            return jnp.concatenate([tt[0:1, :], tt[HEAD_DIM:HEAD_DIM + 1, :]], axis=1)

        def fill(blk, d):
            tokens = _block_tokens(blk, d, S)
            kc[_padded_block(blk), :] = kn[tokens, :].astype(BF16)
            vc[_padded_block(blk), :] = v_ref[tokens, :].astype(BF16)

        def block(blk, d):
            tokens = _block_tokens(blk, d, S)
            keys = pl.ds(pl.multiple_of(blk * BLOCK, BLOCK), 2 * BLOCK)
            first = (blk & (S // d // BLOCK - 1)) == 0
            q2 = _two_heads(qn[tokens, :].astype(BF16), lo)
            do2 = _two_heads(do_s[tokens, :].astype(BF16), lo)
            lse_row = per_query_row(lse_ref[tokens, :])
            dd_row = per_query_row(dd_s[tokens, :])
            kb = kc[keys, :]
            vb = vc[keys, :]
            st = jnp.where(band_mask_t, _nt(kb, q2), NEG)
            st = jnp.concatenate([st[:BLOCK] + jnp.where(first, NEG, 0.0), st[BLOCK:]], axis=0)
            pt = jnp.exp(st - lse_row)
            dst = pt * (_nt(vb, do2) - dd_row)
            ptb = pt.astype(BF16)
            dstb = dst.astype(BF16)
            dv_band = jnp.dot(ptb, do2, preferred_element_type=F32)
            dk_band = jnp.dot(dstb, q2, preferred_element_type=F32)
            before = _block_tokens(jnp.where(first, blk, blk - 1), d, S)
            dka[before, :] = dka[before, :] + dk_band[:BLOCK]
            dva[before, :] = dva[before, :] + dv_band[:BLOCK]
            dka[tokens, :] = dka[tokens, :] + dk_band[BLOCK:]
            dva[tokens, :] = dva[tokens, :] + dv_band[BLOCK:]
            dq2 = _tn(dstb, kb)
            dq = jnp.where(lo, dq2[:BLOCK], dq2[BLOCK:])
            dqa[tokens, :] = dq if d == 1 else dqa[tokens, :] + dq

        for d in DILATIONS:
            _for_blocks(S // BLOCK, 4, functools.partial(fill, d=d))
            _for_blocks(S // BLOCK, 8, functools.partial(block, d=d))

        def out_step(i, carry):
            dqg, dkg = carry
            rows = pl.ds(pl.multiple_of(i * tn, tn), tn)
            rq = rq_s[rows, :]
            rk = rk_s[rows, :]
            qh = q_ref[rows, :] * rq
            kh = k_ref[rows, :] * rk
            dqs = dqa[rows, :] * SCALE
            dkn = dka[rows, :]
            aq = dqs * qg_ref[...]
            ak = dkn * kg_ref[...]
            dq_ref[rows, :] = (rq * (aq - qh * (_split_dot(aq * qh, bdv) * (1.0 / HEAD_DIM)))).astype(BF16)
            dk_ref[rows, :] = (rk * (ak - kh * (_split_dot(ak * kh, bdv) * (1.0 / HEAD_DIM)))).astype(BF16)
            dv_ref[rows, :] = dva[rows, :].astype(BF16)
            dqg = dqg + jnp.sum(dqs * qh, axis=0, keepdims=True)
            dkg = dkg + jnp.sum(dkn * kh, axis=0, keepdims=True)
            return dqg, dkg
        zero = jnp.zeros((1, 128), F32)
        dqg, dkg = lax.fori_loop(0, S // tn, out_step, (zero, zero))
        dqg_ref[0] = dqg
        dkg_ref[0] = dkg

    once = pl.Buffered(1)
    col = lambda j0: pl.BlockSpec((S, 128), lambda p, j0=j0: (0, j0 + p), pipeline_mode=once)
    const = lambda shape: pl.BlockSpec(shape, lambda p: (0,) * len(shape))
    out = pl.BlockSpec((S, 128), lambda p: (0, p))
    gain_out = pl.BlockSpec((1, 1, 128), lambda p: (p, 0, 0))
    piece = pltpu.HBM((S, ATTN_WIDTH), BF16)
    gains = pltpu.HBM((npairs, 1, 128), F32)
    f32buf = pltpu.VMEM((S, 128), F32)
    bf16pad = pltpu.VMEM((S + BLOCK, 128), BF16)
    return pl.pallas_call(
        body, name="attn_bwd", grid=(npairs,),
        in_specs=[col(COL_AQ), col(COL_AK), col(COL_AV), col(COL_AG), col(GMLP_WIDTH // 128), col(0), col(0),
                  const((1, 128)), const((1, 128)), const((128, 128))],
        out_specs=[out, out, out, out, gain_out, gain_out],
        out_shape=[piece, piece, piece, piece, gains, gains],
        scratch_shapes=[f32buf, f32buf, f32buf, f32buf, bf16pad, bf16pad, f32buf, f32buf, f32buf, f32buf, f32buf],
        compiler_params=_params(52, ("arbitrary",)),
    )(*_hbm(proj, proj, proj, proj, dycat, att, lse, qg2, kg2, bd))


def _mem_kv(mem, gain, wkv_bf, kg4, bd):
    def body(mem_ref, g_ref, w_ref, kg_ref, bd_ref, hm_ref, kraw_ref, mk_ref, mv_ref):
        mv_ = mem_ref[...]
        r = lax.rsqrt(jnp.mean(mv_ * mv_, axis=-1, keepdims=True) + EPS)
        hm = ((mv_ * r) * g_ref[...]).astype(BF16)
        hm_ref[...] = hm
        kv = jnp.dot(hm, w_ref[...], preferred_element_type=F32)
        kraw = kv[:, :MEM_WIDTH]
        kraw_ref[...] = kraw
        ms = _split_dot(kraw * kraw, bd_ref[...]) * (1.0 / HEAD_DIM)
        mk_ref[...] = (kraw * lax.rsqrt(ms + EPS)) * kg_ref[...]
        mv_ref[...] = kv[:, MEM_WIDTH:]

    sq = jax.ShapeDtypeStruct((MEM_LEN, MEM_WIDTH), F32)
    return pl.pallas_call(
        body, name="mem_kv",
        out_shape=[jax.ShapeDtypeStruct((MEM_LEN, D_MODEL), BF16), sq, sq, sq],
        compiler_params=_params(16),
    )(mem, gain, wkv_bf, kg4, bd)


def _mem_fwd(proj, mk, mv, qg4, bd):
    S = proj.shape[0]
    tm = 512

    def body(q_ref, g_ref, mk_ref, mv_ref, qg_ref, bd_ref, y_ref, om_ref):
        qv = q_ref[...]
        ms = _split_dot(qv * qv, bd_ref[...]) * (1.0 / HEAD_DIM)
        qs = (qv * lax.rsqrt(ms + EPS)) * (qg_ref[...] * SCALE)
        mkb = mk_ref[...].astype(BF16)
        mvb = mv_ref[...].astype(BF16)
        head = _head_index((tm, MEM_WIDTH))
        o = jnp.zeros((tm, MEM_WIDTH), F32)
        for h in range(4):
            s = _nt(jnp.where(head == h, qs, 0.0).astype(BF16), mkb)
            e = jnp.exp(s - jnp.max(s, axis=-1, keepdims=True))
            p = e * (1.0 / jnp.sum(e, axis=-1, keepdims=True))
            o = jnp.where(head == h, jnp.dot(p.astype(BF16), mvb, preferred_element_type=F32), o)
        om_ref[...] = o
        silu, _ = _silu_parts(g_ref[...])
        y_ref[...] = (o * silu).astype(BF16)

    col = lambda j: pl.BlockSpec((tm, MEM_WIDTH), lambda i, j=j: (i, j))
    const = lambda shape: pl.BlockSpec(shape, lambda i: (0,) * len(shape))
    tile = pl.BlockSpec((tm, MEM_WIDTH), lambda i: (i, 0))
    return pl.pallas_call(
        body, name="mem_fwd", grid=(S // tm,),
        in_specs=[col(11), col(12), const((MEM_LEN, MEM_WIDTH)), const((MEM_LEN, MEM_WIDTH)), const((1, MEM_WIDTH)),
                  const((MEM_WIDTH, MEM_WIDTH))],
        out_specs=[tile, tile],
        out_shape=[pltpu.HBM((S, MEM_WIDTH), BF16), pltpu.HBM((S, MEM_WIDTH), F32)],
        compiler_params=_params(24, ("arbitrary",)),
    )(*_hbm(proj, proj, mk, mv, qg4, bd))


def _mem_bwd(proj, dycat, om, mk, mv, qg4, bd):
    S = proj.shape[0]
    tm = 512

    def body(q_ref, g_ref, dy_ref, om_ref, mk_ref, mv_ref, qg_ref, bd_ref,
             dq_ref, dg_ref, dmk_ref, dmv_ref, dqg_ref):
        i = pl.program_id(0)

        @pl.when(i == 0)
        def _():
            dmk_ref[...] = jnp.zeros_like(dmk_ref)
            dmv_ref[...] = jnp.zeros_like(dmv_ref)
            dqg_ref[...] = jnp.zeros_like(dqg_ref)

        bdv = bd_ref[...]
        qv = q_ref[...]
        rq = lax.rsqrt(_split_dot(qv * qv, bdv) * (1.0 / HEAD_DIM) + EPS)
        qh = qv * rq
        qs = qh * (qg_ref[...] * SCALE)
        silu, dsilu = _silu_parts(g_ref[...])
        dy = dy_ref[...]
        o = om_ref[...]
        do = dy * silu
        dg_ref[...] = (dy * o * dsilu).astype(BF16)
        dd = _split_dot(do * o, bdv)
        mkb = mk_ref[...].astype(BF16)
        mvb = mv_ref[...].astype(BF16)
        head = _head_index((tm, MEM_WIDTH))
        dqs = jnp.zeros((tm, MEM_WIDTH), F32)
        for h in range(4):
            qhd = jnp.where(head == h, qs, 0.0).astype(BF16)
            doh = jnp.where(head == h, do, 0.0).astype(BF16)
            s = _nt(qhd, mkb)
            e = jnp.exp(s - jnp.max(s, axis=-1, keepdims=True))
            p = e * (1.0 / jnp.sum(e, axis=-1, keepdims=True))
            ds = p * (_nt(doh, mvb) - dd[:, h * HEAD_DIM:h * HEAD_DIM + 1])
            dsb = ds.astype(BF16)
            dmv_ref[...] += _tn(p.astype(BF16), doh)
            dmk_ref[...] += _tn(dsb, qhd)
            dqs = jnp.where(head == h, jnp.dot(dsb, mkb, preferred_element_type=F32), dqs)
        dqs = dqs * SCALE
        a = dqs * qg_ref[...]
        dq_ref[...] = (rq * (a - qh * (_split_dot(a * qh, bdv) * (1.0 / HEAD_DIM)))).astype(BF16)
        dqg_ref[...] += jnp.sum(dqs * qh, axis=0, keepdims=True)

    col = lambda j: pl.BlockSpec((tm, MEM_WIDTH), lambda i, j=j: (i, j))
    const = lambda shape: pl.BlockSpec(shape, lambda i: (0,) * len(shape))
    tile = pl.BlockSpec((tm, MEM_WIDTH), lambda i: (i, 0))
    piece = pltpu.HBM((S, MEM_WIDTH), BF16)
    sq = pltpu.HBM((MEM_LEN, MEM_WIDTH), F32)
    return pl.pallas_call(
        body, name="mem_bwd", grid=(S // tm,),
        in_specs=[col(11), col(12), col(3), tile, const((MEM_LEN, MEM_WIDTH)), const((MEM_LEN, MEM_WIDTH)),
                  const((1, MEM_WIDTH)), const((MEM_WIDTH, MEM_WIDTH))],
        out_specs=[tile, tile, const((MEM_LEN, MEM_WIDTH)), const((MEM_LEN, MEM_WIDTH)), const((1, MEM_WIDTH))],
        out_shape=[piece, piece, sq, sq, pltpu.HBM((1, MEM_WIDTH), F32)],
        compiler_params=_params(32, ("arbitrary",)),
    )(*_hbm(proj, proj, dycat, om, mk, mv, qg4, bd))


def _mem_kv_bwd(dmk, dmv, kraw, mem, gain, kg4, wkv_bf, hm_bf, bd):
    def body(dmk_ref, dmv_ref, kraw_ref, mem_ref, g_ref, kg_ref, w_ref, hm_ref, bd_ref, dw_ref, dg_ref, dkg_ref):
        bdv = bd_ref[...]
        kraw = kraw_ref[...]
        rk = lax.rsqrt(_split_dot(kraw * kraw, bdv) * (1.0 / HEAD_DIM) + EPS)
        kh = kraw * rk
        dmkv = dmk_ref[...]
        a = dmkv * kg_ref[...]
        dkraw = rk * (a - kh * (_split_dot(a * kh, bdv) * (1.0 / HEAD_DIM)))
        dkg_ref[...] = jnp.sum(dmkv * kh, axis=0, keepdims=True)
        dkv = jnp.concatenate([dkraw, dmv_ref[...]], axis=1).astype(BF16)
        dw = _tn(hm_ref[...], dkv).astype(BF16)
        rows_blk = D_MODEL // N_DEV
        for j in range(N_DEV):
            dw_ref[j] = dw[rows_blk * j:rows_blk * (j + 1)]
        dhm = _nt(dkv, w_ref[...])
        mv_ = mem_ref[...]
        r = lax.rsqrt(jnp.mean(mv_ * mv_, axis=-1, keepdims=True) + EPS)
        dg_ref[...] = jnp.sum(dhm * (mv_ * r), axis=0, keepdims=True)

    return pl.pallas_call(
        body, name="mem_kv_bwd",
        out_shape=[jax.ShapeDtypeStruct((N_DEV, D_MODEL // N_DEV, 2 * MEM_WIDTH), BF16),
                   jax.ShapeDtypeStruct((1, D_MODEL), F32), jax.ShapeDtypeStruct((1, MEM_WIDTH), F32)],
        compiler_params=_params(24),
    )(dmk, dmv, kraw, mem, gain, kg4, wkv_bf, hm_bf, bd)


def _out_loss(yg, ya, ym, x, tgt, wout_bf):
    S, D = x.shape
    tm = 256

    nsteps = S // tm
    rows_blk = D // N_DEV

    def body(yg_ref, ya_ref, ym_ref, x_ref, t_ref, w_ref, dout_ref, dycat_ref, dw_ref, loss_ref, acc_ref):
        i = pl.program_id(0)

        @pl.when(i == 0)
        def _():
            acc_ref[...] = jnp.zeros_like(acc_ref)
            loss_ref[...] = jnp.zeros_like(loss_ref)

        ycat = jnp.concatenate([yg_ref[...], ya_ref[...], ym_ref[...]], axis=1)
        w = w_ref[...]
        diff = (x_ref[...] + jnp.dot(ycat, w, preferred_element_type=F32)) - t_ref[...]
        loss_ref[...] += jnp.sum(diff * diff, axis=0, keepdims=True)
        dout = diff * (1.0 / D)
        dout_ref[...] = dout
        db = dout.astype(BF16)
        dycat_ref[...] = _nt(db, w)
        acc_ref[...] += _tn(ycat, db)

        @pl.when(i == nsteps - 1)
        def _():
            for j in range(N_DEV):
                dw_ref[j] = acc_ref[rows_blk * j:rows_blk * (j + 1), :].astype(BF16)

    tile = lambda w: pl.BlockSpec((tm, w), lambda i: (i, 0))
    const = lambda shape: pl.BlockSpec(shape, lambda i: (0,) * len(shape))
    return pl.pallas_call(
        body, name="out_loss", grid=(nsteps,),
        in_specs=[tile(GMLP_WIDTH), tile(ATTN_WIDTH), tile(MEM_WIDTH), tile(D), tile(D), const((D, D))],
        out_specs=[tile(D), tile(D), const((N_DEV, rows_blk, D)), const((1, D))],
        out_shape=[pltpu.HBM((S, D), F32), pltpu.HBM((S, D), F32),
                   pltpu.HBM((N_DEV, rows_blk, D), BF16), pltpu.HBM((1, D), F32)],
        scratch_shapes=[pltpu.VMEM((D, D), F32)],
        compiler_params=_params(40, ("arbitrary",)),
    )(*_hbm(yg, ya, ym, x, tgt, wout_bf))


def _piece_specs(pieces, tm):
    return [pl.BlockSpec((tm, p.shape[1]), lambda i: (i, 0)) for p in pieces]


def _in_bwd_dx(pieces, x, dout, gain, w_t, dw_blocks):
    S, D = x.shape
    N = w_t.shape[0]
    tm = 256
    n = len(pieces)
    nsteps = S // tm
    middle_step = nsteps // 4

    def body(*refs):
        piece_refs = refs[:n]
        x_ref, dout_ref, g_ref, w_ref, dwb_ref, gx_ref, dg_ref, gw_ref = refs[n:n + 8]
        rs = _ReduceScatter([dwb_ref], [gw_ref], *refs[n + 8:])
        i = pl.program_id(0)

        @pl.when(i == 0)
        def _():
            dg_ref[...] = jnp.zeros_like(dg_ref)
            rs.start()

        @pl.when(i == middle_step)
        def _():
            rs.middle()

        dproj = jnp.concatenate([r[...] for r in piece_refs], axis=1)
        dh = jnp.dot(dproj, w_ref[...], preferred_element_type=F32)
        xv = x_ref[...]
        r = lax.rsqrt(jnp.mean(xv * xv, axis=-1, keepdims=True) + EPS)
        xh = xv * r
        a = dh * g_ref[...]
        gx_ref[...] = dout_ref[...] + r * (a - xh * jnp.mean(a * xh, axis=-1, keepdims=True))
        dg_ref[...] += jnp.sum(dh * xh, axis=0, keepdims=True)

        @pl.when(i == nsteps - 1)
        def _():
            rs.finish()

    tile = pl.BlockSpec((tm, D), lambda i: (i, 0))
    const = lambda shape: pl.BlockSpec(shape, lambda i: (0,) * len(shape))
    vmem = pl.BlockSpec(memory_space=pltpu.VMEM)
    return pl.pallas_call(
        body, name="in_bwd_dx", grid=(nsteps,),
        in_specs=_piece_specs(pieces, tm) + [tile, tile, const((1, D)), const((N, D)), vmem],
        out_specs=[tile, const((1, D)), vmem],
        out_shape=[pltpu.HBM((S, D), F32), pltpu.HBM((1, D), F32), jax.ShapeDtypeStruct(dw_blocks.shape[1:], F32)],
        scratch_shapes=_reduce_scatter_scratch([dw_blocks]),
        compiler_params=_params(56, ("arbitrary",)),
    )(*_hbm(*pieces, x, dout, gain, w_t), dw_blocks)


def _in_bwd_dw(pieces, h_bf, others):
    S, D = h_bf.shape
    N = sum(p.shape[1] for p in pieces)
    n_blk = N // N_DEV
    tm = 512
    n = len(pieces)
    k = len(others)
    nsteps = S // tm

    def body(*refs):
        piece_refs = refs[:n]
        h_ref = refs[n]
        other_refs = refs[n + 1:n + 1 + k]
        dw_ref = refs[n + 1 + k]
        sum_refs = refs[n + 2 + k:n + 2 + 2 * k]
        acc_ref = refs[n + 2 + 2 * k]
        rs = _ReduceScatter(other_refs, sum_refs, *refs[n + 3 + 2 * k:])
        i = pl.program_id(0)

        @pl.when(i == 0)
        def _():
            acc_ref[...] = jnp.zeros_like(acc_ref)
            rs.start()

        @pl.when(i == 1)
        def _():
            rs.middle()

        dproj = jnp.concatenate([r[...] for r in piece_refs], axis=1)
        acc_ref[...] += _tn(h_ref[...], dproj)

        @pl.when(i == nsteps - 1)
        def _():
            for j in range(N_DEV):
                dw_ref[j] = acc_ref[:, n_blk * j:n_blk * (j + 1)].T.astype(BF16)
            rs.finish()

    vmem = pl.BlockSpec(memory_space=pltpu.VMEM)
    return pl.pallas_call(
        body, name="in_bwd_dw", grid=(nsteps,),
        in_specs=_piece_specs(pieces, tm) + [pl.BlockSpec((tm, D), lambda i: (i, 0))] + [vmem] * k,
        out_specs=[pl.BlockSpec((N_DEV, n_blk, D), lambda i: (0, 0, 0))] + [vmem] * k,
        out_shape=[pltpu.HBM((N_DEV, n_blk, D), BF16)] + [jax.ShapeDtypeStruct(o.shape[1:], F32) for o in others],
        scratch_shapes=[pltpu.VMEM((D, N), F32)] + _reduce_scatter_scratch(others),
        compiler_params=_params(56, ("arbitrary",)),
    )(*_hbm(*pieces, h_bf), *others)


def _row_step(m):
    return max(t for t in range(16, 257, 16) if m % t == 0)


def _place():
    x, y, c = lax.axis_index("x"), lax.axis_index("y"), lax.axis_index("c")
    chips = [(1 - x, y), (x, 1 - y), (1 - x, 1 - y)]
    return x, y, c, chips


def _all_gather_exchange(srcs, outs, send_sems, recv_sems, local_sems):
    n = len(srcs)
    x, y, c, chips = _place()
    me, sibling = (x, y, c), (x, y, 1 - c)

    def rows(a, px, py, pc):
        m = srcs[a].shape[0]
        return outs[a].at[pl.ds((4 * px + 2 * py + pc) * m, m), :]

    def copy(a, k, block, to, src=None):
        return pltpu.make_async_remote_copy(
            src_ref=rows(a, *block) if src is None else src, dst_ref=rows(a, *block),
            send_sem=send_sems.at[a, k], recv_sem=recv_sems.at[a, k], device_id=to, device_id_type=MESH)

    mine = [pltpu.make_async_copy(srcs[a], rows(a, *me), local_sems.at[a]) for a in range(n)]
    for cp in mine:
        cp.start()
    first = []
    for a in range(n):
        first.append(copy(a, 0, me, sibling, src=srcs[a]))
        first += [copy(a, 1 + j, me, (*chip, c), src=srcs[a]) for j, chip in enumerate(chips)]
    for cp in first:
        cp.start()
    passed = []
    for j, chip in enumerate(chips):
        for a in range(n):
            copy(a, 1 + j, (*chip, c), me).wait_recv()
            fwd = copy(a, 4 + j, (*chip, c), sibling)
            fwd.start()
            passed.append(fwd)
    for a in range(n):
        copy(a, 0, sibling, me).wait_recv()
        for j, chip in enumerate(chips):
            copy(a, 4 + j, (*chip, 1 - c), me).wait_recv()
    for cp in first + passed:
        cp.wait_send()
    for cp in mine:
        cp.wait()


def _gather_weights(shards):
    n = len(shards)

    def body(*refs):
        ins, outs, casts = refs[:n], refs[n:2 * n], refs[2 * n:3 * n]
        for a in range(n):
            tr = _row_step(ins[a].shape[0])

            def cast(i, carry, a=a, tr=tr):
                rows = pl.ds(pl.multiple_of(i * tr, tr), tr)
                casts[a][rows, :] = ins[a][rows, :].astype(BF16)
                return carry
            lax.fori_loop(0, ins[a].shape[0] // tr, cast, 0)
        _all_gather_exchange(casts, outs, *refs[3 * n:])

    vmem = pl.BlockSpec(memory_space=pltpu.VMEM)
    return pl.pallas_call(
        body, name="gather_weights",
        out_shape=[jax.ShapeDtypeStruct((N_DEV * a.shape[0], a.shape[1]), BF16) for a in shards],
        in_specs=[vmem] * n, out_specs=[vmem] * n,
        scratch_shapes=[pltpu.VMEM(a.shape, BF16) for a in shards]
        + [pltpu.SemaphoreType.DMA((n, 7)), pltpu.SemaphoreType.DMA((n, 7)), pltpu.SemaphoreType.DMA((n,))],
        compiler_params=_params(40),
    )(*shards)


ROW_NORM, ROW_MEM_NORM, ROW_V_GAIN, ROW_B, ROW_ATTN_GAINS, ROW_MEM_GAINS, ROW_W_S, ROW_LOSS = 0, 8, 16, 18, 22, 23, 24, 536
SMALL_ROWS = 544


def _gather_small(dgain, dmgain, dvg, db2, dqg, dkg, dmqg, dmkg, dws, sq):
    def body(dgain_ref, dmgain_ref, dvg_ref, db2_ref, dqg_ref, dkg_ref, dmqg_ref, dmkg_ref, dws_ref, sq_ref,
             out_ref, mine, send_sems, recv_sems, local_sems):
        first = lax.broadcasted_iota(jnp.int32, (1, 128), 1) < HEAD_DIM
        for i in range(8):
            cols = slice(128 * i, 128 * (i + 1))
            mine[ROW_NORM + i:ROW_NORM + i + 1, :] = dgain_ref[:, cols]
            mine[ROW_MEM_NORM + i:ROW_MEM_NORM + i + 1, :] = dmgain_ref[:, cols]
            mine[ROW_LOSS + i:ROW_LOSS + i + 1, :] = sq_ref[:, cols]
        mine[ROW_V_GAIN:ROW_V_GAIN + 1, :] = dvg_ref[:, 0:128]
        mine[ROW_V_GAIN + 1:ROW_V_GAIN + 2, :] = dvg_ref[:, 128:256]
        bt = db2_ref[...].T
        for h in range(4):
            mine[ROW_B + h:ROW_B + h + 1, :] = bt[HEAD_DIM * h:HEAD_DIM * h + 1, :]

        def fold_heads(t):
            return t + pltpu.roll(t, HEAD_DIM, axis=1)
        aq = fold_heads(dqg_ref[0] + dqg_ref[1] + dqg_ref[2] + dqg_ref[3])
        ak = fold_heads(dkg_ref[0] + dkg_ref[1] + dkg_ref[2] + dkg_ref[3])
        mine[ROW_ATTN_GAINS:ROW_ATTN_GAINS + 1, :] = jnp.where(first, aq, ak)
        mq = fold_heads(dmqg_ref[:, 0:128] + dmqg_ref[:, 128:256])
        mk = fold_heads(dmkg_ref[:, 0:128] + dmkg_ref[:, 128:256])
        mine[ROW_MEM_GAINS:ROW_MEM_GAINS + 1, :] = jnp.where(first, mq, mk)
        mine[ROW_W_S:ROW_W_S + 4 * CHUNK, :] = dws_ref[...]
        _all_gather_exchange([mine], [out_ref], send_sems, recv_sems, local_sems)

    return pl.pallas_call(
        body, name="gather_small_grads",
        out_shape=jax.ShapeDtypeStruct((N_DEV * SMALL_ROWS, 128), F32),
        scratch_shapes=[pltpu.VMEM((SMALL_ROWS, 128), F32), pltpu.SemaphoreType.DMA((1, 7)),
                        pltpu.SemaphoreType.DMA((1, 7)), pltpu.SemaphoreType.DMA((1,))],
        compiler_params=_params(16),
    )(dgain, dmgain, dvg, db2, dqg, dkg, dmqg, dmkg, dws, sq)


def _reduce_scatter_scratch(arrs):
    n = len(arrs)
    return ([pltpu.VMEM((4,) + a.shape[1:], BF16) for a in arrs] + [pltpu.VMEM((3,) + a.shape[1:], BF16) for a in arrs]
            + [pltpu.SemaphoreType.DMA((n, 7)), pltpu.SemaphoreType.DMA((n, 7))])


class _ReduceScatter:
    def __init__(self, ins, outs, *scratch):
        n = len(ins)
        self.n, self.ins, self.outs = n, ins, outs
        self.half, self.quarter = scratch[:n], scratch[n:2 * n]
        self.send_sems, self.recv_sems = scratch[2 * n:]

    def _to_sibling(self):
        x, y, c, _ = _place()
        return [pltpu.make_async_remote_copy(
            src_ref=self.ins[a].at[2 * q + (1 - c)], dst_ref=self.half[a].at[q], send_sem=self.send_sems.at[a, q],
            recv_sem=self.recv_sems.at[a, q], device_id=(x, y, 1 - c), device_id_type=MESH)
            for a in range(self.n) for q in range(4)]

    def _to_chips(self):
        _, _, c, chips = _place()
        return [pltpu.make_async_remote_copy(
            src_ref=self.half[a].at[2 * chip[0] + chip[1]], dst_ref=self.quarter[a].at[k],
            send_sem=self.send_sems.at[a, 4 + k], recv_sem=self.recv_sems.at[a, 4 + k], device_id=(*chip, c),
            device_id_type=MESH) for a in range(self.n) for k, chip in enumerate(chips)]

    def _rows(self, a, fn):
        m = self.ins[a].shape[1]
        tr = _row_step(m)

        def step(i, carry):
            fn(pl.ds(pl.multiple_of(i * tr, tr), tr))
            return carry
        lax.fori_loop(0, m // tr, step, 0)

    def start(self):
        for cp in self._to_sibling():
            cp.start()

    def middle(self):
        _, _, c, _ = _place()
        for cp in self._to_sibling():
            cp.wait_recv()
        for a in range(self.n):
            for q in range(4):
                def add_half(rows, a=a, q=q):
                    both = self.ins[a][2 * q + c, rows, :].astype(F32) + self.half[a][q, rows, :].astype(F32)
                    self.half[a][q, rows, :] = both.astype(BF16)
                self._rows(a, add_half)
        for cp in self._to_chips():
            cp.start()

    def finish(self):
        x, y, _, _ = _place()
        for cp in self._to_chips():
            cp.wait_recv()
        for a in range(self.n):
            def add_quarters(rows, a=a):
                f = lambda t: t.astype(F32)
                self.outs[a][rows, :] = ((f(self.half[a][2 * x + y, rows, :]) + f(self.quarter[a][0, rows, :]))
                                         + (f(self.quarter[a][1, rows, :]) + f(self.quarter[a][2, rows, :])))
            self._rows(a, add_quarters)
        for cp in self._to_sibling() + self._to_chips():
            cp.wait_send()


def _adamw_math(w, g, m, v):
    m = ADAM_B1 * m + (1.0 - ADAM_B1) * g
    v = ADAM_B2 * v + (1.0 - ADAM_B2) * (g * g)
    m_hat = m / (1.0 - ADAM_B1 ** ADAM_STEP)
    v_hat = v / (1.0 - ADAM_B2 ** ADAM_STEP)
    delta = -ADAM_LR * (m_hat / (jnp.sqrt(v_hat) + ADAM_EPS) + ADAM_WD * w)
    return delta, m, v


def _adamw(w, g, m, v, name):
    R, C = w.shape
    tr = _row_step(R)

    def body(w_ref, g_ref, m_ref, v_ref, d_ref, nm_ref, nv_ref):
        d_ref[...], nm_ref[...], nv_ref[...] = _adamw_math(w_ref[...], g_ref[...], m_ref[...], v_ref[...])

    tile = pl.BlockSpec((tr, C), lambda i: (i, 0))
    out = pltpu.HBM((R, C), F32)
    return pl.pallas_call(
        body, name=name, grid=(R // tr,), in_specs=[tile] * 4, out_specs=[tile] * 3, out_shape=[out] * 3,
        compiler_params=_params(16, ("arbitrary",)),
    )(*_hbm(w, g, m, v))


SMALL = ("norm_gain", "gmlp_v_gain", "gmlp_w_s", "gmlp_b", "attn_q_gain", "attn_k_gain", "mem_norm_gain",
         "mem_q_gain", "mem_k_gain")
WEIGHTS = ("norm_gain", "w_in", "gmlp_v_gain", "gmlp_w_s", "gmlp_b", "attn_q_gain", "attn_k_gain",
           "mem_norm_gain", "w_mem_kv", "mem_q_gain", "mem_k_gain", "w_out")


def _adamw_small(w, m, v, g_all):
    k = len(SMALL)
    half = slice(0, HEAD_DIM), slice(HEAD_DIM, 2 * HEAD_DIM)

    def body(*refs):
        w_refs, m_refs, v_refs = refs[:k], refs[k:2 * k], refs[2 * k:3 * k]
        g_ref = refs[3 * k]
        outs = refs[3 * k + 1:7 * k + 1]
        loss_ref, gsum = refs[7 * k + 1:]

        part = SMALL_ROWS // 4
        for p in range(4):
            acc = g_ref[part * p:part * (p + 1), :]
            for dev in range(1, N_DEV):
                acc = acc + g_ref[dev * SMALL_ROWS + part * p:dev * SMALL_ROWS + part * (p + 1), :]
            gsum[part * p:part * (p + 1), :] = acc

        def update(name, at, g):
            i = SMALL.index(name)
            d, nm, nv = _adamw_math(w_refs[i][at], g, m_refs[i][at], v_refs[i][at])
            outs[i][at], outs[k + i][at], outs[2 * k + i][at], outs[3 * k + i][at] = g, d, nm, nv

        for i in range(8):
            at = (slice(0, 1), slice(128 * i, 128 * (i + 1)))
            update("norm_gain", at, gsum[ROW_NORM + i:ROW_NORM + i + 1, :])
            update("mem_norm_gain", at, gsum[ROW_MEM_NORM + i:ROW_MEM_NORM + i + 1, :])
        for h in range(4):
            row = (0, slice(h, h + 1), slice(None))
            update("gmlp_v_gain", row, gsum[ROW_V_GAIN + h // 2:ROW_V_GAIN + h // 2 + 1, half[h % 2]])
            update("gmlp_b", row, gsum[ROW_B + h:ROW_B + h + 1, :])
            update("gmlp_w_s", (0, h), gsum[ROW_W_S + CHUNK * h:ROW_W_S + CHUNK * (h + 1), :])
        whole = (slice(0, 1), slice(None))
        update("attn_q_gain", whole, gsum[ROW_ATTN_GAINS:ROW_ATTN_GAINS + 1, half[0]])
        update("attn_k_gain", whole, gsum[ROW_ATTN_GAINS:ROW_ATTN_GAINS + 1, half[1]])
        update("mem_q_gain", whole, gsum[ROW_MEM_GAINS:ROW_MEM_GAINS + 1, half[0]])
        update("mem_k_gain", whole, gsum[ROW_MEM_GAINS:ROW_MEM_GAINS + 1, half[1]])
        loss_ref[...] = jnp.sum(gsum[ROW_LOSS:ROW_LOSS + 8, :], keepdims=True) * (0.5 / D_MODEL)

    shapes = [jax.ShapeDtypeStruct(w[name].shape, F32) for name in SMALL]
    res = pl.pallas_call(
        body, name="adamw_small",
        out_shape=shapes * 4 + [jax.ShapeDtypeStruct((1, 1), F32)],
        scratch_shapes=[pltpu.VMEM((SMALL_ROWS, 128), F32)],
        compiler_params=_params(16),
    )(*[w[n] for n in SMALL], *[m[n] for n in SMALL], *[v[n] for n in SMALL], g_all)
    trees = [dict(zip(SMALL, res[j * k:(j + 1) * k])) for j in range(4)]
    return (*trees, res[4 * k])


def _grads(x, mem, tgt, w, win_t, wkv_bf, wout_bf):
    bd128, bd256 = _head_blockdiag(128), _head_blockdiag(256)
    gain = w["norm_gain"].reshape(1, D_MODEL)
    vg = w["gmlp_v_gain"].reshape(1, GMLP_WIDTH)
    w_s = w["gmlp_w_s"].reshape(4, CHUNK, CHUNK)
    b2 = jnp.repeat(w["gmlp_b"].reshape(4, CHUNK).T, HEAD_DIM, axis=1)
    qg2 = jnp.tile(w["attn_q_gain"].reshape(1, HEAD_DIM), (1, 2))
    kg2 = jnp.tile(w["attn_k_gain"].reshape(1, HEAD_DIM), (1, 2))
    mqg4 = jnp.tile(w["mem_q_gain"].reshape(1, HEAD_DIM), (1, 4))
    mkg4 = jnp.tile(w["mem_k_gain"].reshape(1, HEAD_DIM), (1, 4))
    mgain = w["mem_norm_gain"].reshape(1, D_MODEL)

    proj, h_bf = _rms_proj(x, gain, win_t)
    yg = _gmlp_fwd(proj, vg, w_s, b2, bd256)
    ya, att, lse = _attn_fwd(proj, qg2, kg2, bd128)
    hm_bf, kraw, mk, mv = _mem_kv(mem, mgain, wkv_bf, mkg4, bd256)
    ym, om = _mem_fwd(proj, mk, mv, mqg4, bd256)
    dout, dycat, dwout, sq = _out_loss(yg, ya, ym, x, tgt, wout_bf)

    du, dgv, dgg, dws, db2, dvg = _gmlp_bwd(proj, dycat, vg, w_s, b2, bd256)
    dq, dk, dv, dag, dqg, dkg = _attn_bwd(proj, dycat, att, lse, qg2, kg2, bd128)
    dmq, dmg, dmk, dmv, dmqg = _mem_bwd(proj, dycat, om, mk, mv, mqg4, bd256)
    dwkv, dmgain, dmkg = _mem_kv_bwd(dmk, dmv, kraw, mem, mgain, mkg4, wkv_bf, hm_bf, bd256)
    pieces = [du, dgv, dgg, dq, dk, dv, dag, dmq, dmg]
    dwin, g_wkv, g_wout = _in_bwd_dw(pieces, h_bf, [dwkv, dwout])
    grad_x, dgain, g_win = _in_bwd_dx(pieces, x, dout, gain, win_t, dwin)
    return grad_x, g_win, g_wkv, g_wout, (dgain, dmgain, dvg, db2, dqg, dkg, dmqg, dmkg, dws, sq)


def kernel(x, mem, norm_gain, w_in, gmlp_v_gain, gmlp_w_s, gmlp_b, attn_q_gain, attn_k_gain, mem_norm_gain, w_mem_kv, mem_q_gain, mem_k_gain, w_out, loss_target, m_norm_gain, m_w_in, m_gmlp_v_gain, m_gmlp_w_s, m_gmlp_b, m_attn_q_gain, m_attn_k_gain, m_mem_norm_gain, m_w_mem_kv, m_mem_q_gain, m_mem_k_gain, m_w_out, v_norm_gain, v_w_in, v_gmlp_v_gain, v_gmlp_w_s, v_gmlp_b, v_attn_q_gain, v_attn_k_gain, v_mem_norm_gain, v_w_mem_kv, v_mem_q_gain, v_mem_k_gain, v_w_out):
    w = dict(norm_gain=norm_gain, w_in=w_in, gmlp_v_gain=gmlp_v_gain, gmlp_w_s=gmlp_w_s, gmlp_b=gmlp_b,
             attn_q_gain=attn_q_gain, attn_k_gain=attn_k_gain, mem_norm_gain=mem_norm_gain, w_mem_kv=w_mem_kv,
             mem_q_gain=mem_q_gain, mem_k_gain=mem_k_gain, w_out=w_out)
    m = dict(norm_gain=m_norm_gain, w_in=m_w_in, gmlp_v_gain=m_gmlp_v_gain, gmlp_w_s=m_gmlp_w_s, gmlp_b=m_gmlp_b,
             attn_q_gain=m_attn_q_gain, attn_k_gain=m_attn_k_gain, mem_norm_gain=m_mem_norm_gain,
             w_mem_kv=m_w_mem_kv, mem_q_gain=m_mem_q_gain, mem_k_gain=m_mem_k_gain, w_out=m_w_out)
    v = dict(norm_gain=v_norm_gain, w_in=v_w_in, gmlp_v_gain=v_gmlp_v_gain, gmlp_w_s=v_gmlp_w_s, gmlp_b=v_gmlp_b,
             attn_q_gain=v_attn_q_gain, attn_k_gain=v_attn_k_gain, mem_norm_gain=v_mem_norm_gain,
             w_mem_kv=v_w_mem_kv, mem_q_gain=v_mem_q_gain, mem_k_gain=v_mem_k_gain, w_out=v_w_out)
    transposed = lambda t: jnp.transpose(t[0])

    win_t, wkv_bf, wout_bf = _gather_weights([transposed(w_in), w_mem_kv[0], w_out[0]])

    grad_x, g_win, g_wkv, g_wout, small = _grads(x[0], mem[0], loss_target[0], w, win_t, wkv_bf, wout_bf)
    small_all = _gather_small(*small)

    out_g, out_d, out_m, out_v, loss = _adamw_small(w, m, v, small_all)
    d_, m_, v_ = _adamw(transposed(w_in), g_win, transposed(m_w_in), transposed(v_w_in), "adamw_w_in")
    for tree, t in ((out_g, g_win), (out_d, d_), (out_m, m_), (out_v, v_)):
        tree["w_in"] = jnp.transpose(t)[None]
    for name, g in (("w_mem_kv", g_wkv), ("w_out", g_wout)):
        d_, m_, v_ = _adamw(w[name][0], g, m[name][0], v[name][0], "adamw_" + name)
        out_g[name], out_d[name], out_m[name], out_v[name] = g[None], d_[None], m_[None], v_[None]

    return (loss.reshape(()), grad_x[None], *[out_g[k] for k in WEIGHTS], *[out_d[k] for k in WEIGHTS],
            *[out_m[k] for k in WEIGHTS], *[out_v[k] for k in WEIGHTS])
```

```python
import functools
import math

import jax
import jax.numpy as jnp
from jax import lax
from jax.experimental import pallas as pl
from jax.experimental.pallas import tpu as pltpu

F32 = jnp.float32
BF16 = jnp.bfloat16

N_DEV = 8
D_MODEL = 1024
HEAD_DIM = 64
GMLP_WIDTH = 256
ATTN_WIDTH = 512
MEM_WIDTH = 256
MEM_LEN = 256
IN_WIDTH = 3 * GMLP_WIDTH + 4 * ATTN_WIDTH + 2 * MEM_WIDTH
CHUNK = 128
BLOCK = 128
DILATIONS = (1, 4, 16)
EPS = 1e-6
SCALE = 1.0 / math.sqrt(HEAD_DIM)
NEG = -1e30

ADAM_LR = 0.001
ADAM_B1 = 0.9
ADAM_B2 = 0.999
ADAM_EPS = 1e-08
ADAM_WD = 0.01
ADAM_STEP = 10

MIB = 1024 * 1024
MESH = pl.DeviceIdType.MESH

COL_AQ, COL_AK, COL_AV, COL_AG = 6, 10, 14, 18


def _params(vmem_mib, semantics=None):
    kw = dict(vmem_limit_bytes=vmem_mib * MIB)
    if semantics is not None:
        kw["dimension_semantics"] = semantics
    return pltpu.CompilerParams(**kw)


def _hbm(*arrs):
    return [pltpu.with_memory_space_constraint(a, pltpu.HBM) for a in arrs]


def _split_dot(x, sel_bf):
    hi = x.astype(BF16)
    lo = (x - hi.astype(F32)).astype(BF16)
    return jnp.dot(hi, sel_bf, preferred_element_type=F32) + jnp.dot(lo, sel_bf, preferred_element_type=F32)


def _nt(a, b):
    return lax.dot_general(a, b, (((1,), (1,)), ((), ())), preferred_element_type=F32)


def _tn(a, b):
    return lax.dot_general(a, b, (((0,), (0,)), ((), ())), preferred_element_type=F32)


def _silu_parts(g):
    sg = jax.nn.sigmoid(g)
    return g * sg, sg * (1.0 + g * (1.0 - sg))


def _head_index(shape):
    return lax.shift_right_logical(lax.broadcasted_iota(jnp.int32, shape, 1), HEAD_DIM.bit_length() - 1)


def _head_blockdiag(width):
    i = jnp.arange(width) // HEAD_DIM
    return (i[:, None] == i[None, :]).astype(BF16)


def _gmlp_masked_weights(ws_ref, transpose):
    t = lax.broadcasted_iota(jnp.int32, (CHUNK, CHUNK), 0)
    s = lax.broadcasted_iota(jnp.int32, (CHUNK, CHUNK), 1)
    parts = []
    for h in range(4):
        wm = jnp.where(s <= t, ws_ref[h], 0.0)
        parts.append(wm.T if transpose else wm)
    return jnp.concatenate(parts, axis=1).astype(BF16)


def _head_stack(v, head):
    return jnp.concatenate([jnp.where(head == h, v, 0.0) for h in range(4)], axis=0).astype(BF16)


def _gmlp_fwd(proj, vg, w_s, b2, bd):
    S = proj.shape[0]
    tm = 512

    def body(u_ref, v_ref, g_ref, vg_ref, ws_ref, b2_ref, bd_ref, y_ref):
        v = v_ref[...]
        ms = _split_dot(v * v, bd_ref[...]) * (1.0 / HEAD_DIM)
        vn = (v * lax.rsqrt(ms + EPS)) * vg_ref[...]
        wcat = _gmlp_masked_weights(ws_ref, False)
        head = _head_index((CHUNK, GMLP_WIDTH))
        for c in range(tm // CHUNK):
            rows = slice(c * CHUNK, (c + 1) * CHUNK)
            sp = jnp.dot(wcat, _head_stack(vn[rows], head), preferred_element_type=F32) + b2_ref[...]
            silu, _ = _silu_parts(g_ref[rows, :])
            y_ref[rows, :] = ((u_ref[rows, :] * sp) * silu).astype(BF16)

    col = lambda j: pl.BlockSpec((tm, GMLP_WIDTH), lambda i, j=j: (i, j))
    const = lambda shape: pl.BlockSpec(shape, lambda i: (0,) * len(shape))
    return pl.pallas_call(
        body, name="gmlp_fwd", grid=(S // tm,),
        in_specs=[col(0), col(1), col(2), const((1, GMLP_WIDTH)), const((4, CHUNK, CHUNK)),
                  const((CHUNK, GMLP_WIDTH)), const((GMLP_WIDTH, GMLP_WIDTH))],
        out_specs=pl.BlockSpec((tm, GMLP_WIDTH), lambda i: (i, 0)),
        out_shape=pltpu.HBM((S, GMLP_WIDTH), BF16),
        compiler_params=_params(24, ("arbitrary",)),
    )(*_hbm(proj, proj, proj, vg, w_s, b2, bd))


def _gmlp_bwd(proj, dycat, vg, w_s, b2, bd):
    S = proj.shape[0]
    tm = 512
    nsteps = S // tm

    def body(u_ref, v_ref, g_ref, dy_ref, vg_ref, ws_ref, b2_ref, bd_ref,
             du_ref, dv_ref, dg_ref, dws_ref, db2_ref, dvg_ref):
        i = pl.program_id(0)

        @pl.when(i == 0)
        def _():
            dws_ref[...] = jnp.zeros_like(dws_ref)
            db2_ref[...] = jnp.zeros_like(db2_ref)
            dvg_ref[...] = jnp.zeros_like(dvg_ref)

        bdv = bd_ref[...]
        v = v_ref[...]
        ms = _split_dot(v * v, bdv) * (1.0 / HEAD_DIM)
        rv = lax.rsqrt(ms + EPS)
        xhat = v * rv
        vgv = vg_ref[...]
        vn = xhat * vgv
        wcat = _gmlp_masked_weights(ws_ref, False)
        wcat_t = _gmlp_masked_weights(ws_ref, True)
        head = _head_index((CHUNK, GMLP_WIDTH))
        dvg = jnp.zeros((1, GMLP_WIDTH), F32)
        for c in range(tm // CHUNK):
            rows = slice(c * CHUNK, (c + 1) * CHUNK)
            vn_c = vn[rows]
            spb = jnp.dot(wcat, _head_stack(vn_c, head), preferred_element_type=F32) + b2_ref[...]
            silu, dsilu = _silu_parts(g_ref[rows, :])
            dy = dy_ref[rows, :]
            u = u_ref[rows, :]
            du_ref[rows, :] = (dy * spb * silu).astype(BF16)
            dg_ref[rows, :] = (dy * u * spb * dsilu).astype(BF16)
            dsp = dy * u * silu
            db2_ref[...] += dsp
            dstack = _head_stack(dsp, head)
            dvn = jnp.dot(wcat_t, dstack, preferred_element_type=F32)
            dws_ref[...] += _nt(dstack, vn_c.astype(BF16))
            xh = xhat[rows]
            a = dvn * vgv
            mean_ax = _split_dot(a * xh, bdv) * (1.0 / HEAD_DIM)
            dv_ref[rows, :] = (rv[rows] * (a - xh * mean_ax)).astype(BF16)
            dvg = dvg + jnp.sum(dvn * xh, axis=0, keepdims=True)
        dvg_ref[...] += dvg

        @pl.when(i == nsteps - 1)
        def _():
            t = lax.broadcasted_iota(jnp.int32, (4 * CHUNK, CHUNK), 0) % CHUNK
            s = lax.broadcasted_iota(jnp.int32, (4 * CHUNK, CHUNK), 1)
            dws_ref[...] = jnp.where(s <= t, dws_ref[...], 0.0)
            db2_ref[...] = _split_dot(db2_ref[...], bdv)

    col = lambda j: pl.BlockSpec((tm, GMLP_WIDTH), lambda i, j=j: (i, j))
    const = lambda shape: pl.BlockSpec(shape, lambda i: (0,) * len(shape))
    tile = pl.BlockSpec((tm, GMLP_WIDTH), lambda i: (i, 0))
    piece = pltpu.HBM((S, GMLP_WIDTH), BF16)
    return pl.pallas_call(
        body, name="gmlp_bwd", grid=(nsteps,),
        in_specs=[col(0), col(1), col(2), col(0), const((1, GMLP_WIDTH)), const((4, CHUNK, CHUNK)),
                  const((CHUNK, GMLP_WIDTH)), const((GMLP_WIDTH, GMLP_WIDTH))],
        out_specs=[tile, tile, tile, const((4 * CHUNK, CHUNK)), const((CHUNK, GMLP_WIDTH)), const((1, GMLP_WIDTH))],
        out_shape=[piece, piece, piece, pltpu.HBM((4 * CHUNK, CHUNK), F32),
                   pltpu.HBM((CHUNK, GMLP_WIDTH), F32), pltpu.HBM((1, GMLP_WIDTH), F32)],
        compiler_params=_params(32, ("arbitrary",)),
    )(*_hbm(proj, proj, proj, dycat, vg, w_s, b2, bd))


def _band_mask():
    qi = lax.broadcasted_iota(jnp.int32, (2 * BLOCK, 2 * BLOCK), 0) % BLOCK
    ki = lax.broadcasted_iota(jnp.int32, (2 * BLOCK, 2 * BLOCK), 1)
    return ((ki < BLOCK) & (ki >= qi)) | ((ki >= BLOCK) & ((ki - BLOCK) <= qi))


def _first_block_bias(blk, blocks_per_class):
    kcol = lax.broadcasted_iota(jnp.int32, (1, 2 * BLOCK), 1)
    kill = jnp.where((blk & (blocks_per_class - 1)) == 0, NEG, 0.0)
    return jnp.where(kcol < BLOCK, kill, 0.0)


def _two_heads(q, lo):
    zero = jnp.zeros_like(q)
    return jnp.concatenate([jnp.where(lo, q, zero), jnp.where(lo, zero, q)], axis=0)


def _block_tokens(blk, d, S):
    if d == 1:
        return pl.ds(pl.multiple_of(blk * BLOCK, BLOCK), BLOCK)
    blocks_per_class = S // d // BLOCK
    r = lax.shift_right_logical(blk, blocks_per_class.bit_length() - 1)
    n = blk & (blocks_per_class - 1)
    return pl.ds(r + n * (BLOCK * d), BLOCK, stride=d)


def _padded_block(blk):
    return pl.ds(pl.multiple_of((blk + 1) * BLOCK, BLOCK), BLOCK)


def _for_blocks(n_blocks, unroll, fn):
    def group(g, carry):
        for u in range(unroll):
            fn(g * unroll + u)
        return carry
    lax.fori_loop(0, n_blocks // unroll, group, 0)


def _attn_fwd(proj, qg2, kg2, bd):
    S = proj.shape[0]
    npairs = ATTN_WIDTH // 128
    tn = 512

    def body(q_ref, k_ref, v_ref, g_ref, qg_ref, kg_ref, bd_ref, y_ref, att_ref, lse_ref, qn, kn, kc, vc):
        bdv = bd_ref[...]
        lo = lax.broadcasted_iota(jnp.int32, (BLOCK, 128), 1) < HEAD_DIM
        band_mask = _band_mask()
        kc[pl.ds(0, BLOCK), :] = jnp.zeros((BLOCK, 128), BF16)
        vc[pl.ds(0, BLOCK), :] = jnp.zeros((BLOCK, 128), BF16)

        def norm_step(i, carry):
            rows = pl.ds(pl.multiple_of(i * tn, tn), tn)
            qv = q_ref[rows, :]
            kv = k_ref[rows, :]
            qn[rows, :] = (qv * lax.rsqrt(_split_dot(qv * qv, bdv) * (1.0 / HEAD_DIM) + EPS)) * (qg_ref[...] * SCALE)
            kn[rows, :] = (kv * lax.rsqrt(_split_dot(kv * kv, bdv) * (1.0 / HEAD_DIM) + EPS)) * kg_ref[...]
            return carry
        lax.fori_loop(0, S // tn, norm_step, 0)

        def fill(blk, d):
            tokens = _block_tokens(blk, d, S)
            kc[_padded_block(blk), :] = kn[tokens, :].astype(BF16)
            vc[_padded_block(blk), :] = v_ref[tokens, :].astype(BF16)

        def block(blk, d):
            tokens = _block_tokens(blk, d, S)
            keys = pl.ds(pl.multiple_of(blk * BLOCK, BLOCK), 2 * BLOCK)
            q2 = _two_heads(qn[tokens, :].astype(BF16), lo)
            s = jnp.where(band_mask, _nt(q2, kc[keys, :]), NEG) + _first_block_bias(blk, S // d // BLOCK)
            m = jnp.max(s, axis=-1, keepdims=True)
            e = jnp.exp(s - m)
            l = jnp.sum(e, axis=-1, keepdims=True)
            o2 = jnp.dot(e.astype(BF16), vc[keys, :], preferred_element_type=F32) * (1.0 / l)
            lse2 = m + jnp.log(l)
            o = jnp.where(lo, o2[:BLOCK], o2[BLOCK:])
            lse = jnp.where(lo, lse2[:BLOCK], lse2[BLOCK:])
            if d > 1:
                la = lse_ref[tokens, :]
                mx = jnp.maximum(la, lse)
                wa, wb = jnp.exp(la - mx), jnp.exp(lse - mx)
                t = wa + wb
                o = (wa * att_ref[tokens, :] + wb * o) / t
                lse = mx + jnp.log(t)
            att_ref[tokens, :] = o
            lse_ref[tokens, :] = lse

        for d in DILATIONS:
            _for_blocks(S // BLOCK, 4, functools.partial(fill, d=d))
            _for_blocks(S // BLOCK, 8, functools.partial(block, d=d))

        def gate_step(i, carry):
            rows = pl.ds(pl.multiple_of(i * tn, tn), tn)
            silu, _ = _silu_parts(g_ref[rows, :])
            y_ref[rows, :] = (att_ref[rows, :] * silu).astype(BF16)
            return carry
        lax.fori_loop(0, S // tn, gate_step, 0)

    col = lambda j0: pl.BlockSpec((S, 128), lambda p, j0=j0: (0, j0 + p))
    const = lambda shape: pl.BlockSpec(shape, lambda p: (0,) * len(shape))
    out = pl.BlockSpec((S, 128), lambda p: (0, p))
    return pl.pallas_call(
        body, name="attn_fwd", grid=(npairs,),
        in_specs=[col(COL_AQ), col(COL_AK), col(COL_AV), col(COL_AG), const((1, 128)), const((1, 128)),
                  const((128, 128))],
        out_specs=[out, out, out],
        out_shape=[pltpu.HBM((S, ATTN_WIDTH), BF16), pltpu.HBM((S, ATTN_WIDTH), F32),
                   pltpu.HBM((S, ATTN_WIDTH), F32)],
        scratch_shapes=[pltpu.VMEM((S, 128), F32), pltpu.VMEM((S, 128), F32),
                        pltpu.VMEM((S + BLOCK, 128), BF16), pltpu.VMEM((S + BLOCK, 128), BF16)],
        compiler_params=_params(48, ("arbitrary",)),
    )(*_hbm(proj, proj, proj, proj, qg2, kg2, bd))


def _attn_bwd(proj, dycat, att, lse, qg2, kg2, bd):
    S = proj.shape[0]
    npairs = ATTN_WIDTH // 128
    tn = 512

    def body(q_ref, k_ref, v_ref, g_ref, dy_ref, att_ref, lse_ref, qg_ref, kg_ref, bd_ref,
             dq_ref, dk_ref, dv_ref, dg_ref, dqg_ref, dkg_ref,
             qn, kn, rq_s, rk_s, kc, vc, do_s, dd_s, dqa, dka, dva):
        bdv = bd_ref[...]
        lo = lax.broadcasted_iota(jnp.int32, (BLOCK, 128), 1) < HEAD_DIM
        kc[pl.ds(0, BLOCK), :] = jnp.zeros((BLOCK, 128), BF16)
        vc[pl.ds(0, BLOCK), :] = jnp.zeros((BLOCK, 128), BF16)

        def prepare(i, carry):
            rows = pl.ds(pl.multiple_of(i * tn, tn), tn)
            qv = q_ref[rows, :]
            kv = k_ref[rows, :]
            rq = lax.rsqrt(_split_dot(qv * qv, bdv) * (1.0 / HEAD_DIM) + EPS)
            rk = lax.rsqrt(_split_dot(kv * kv, bdv) * (1.0 / HEAD_DIM) + EPS)
            rq_s[rows, :] = rq
            rk_s[rows, :] = rk
            qn[rows, :] = (qv * rq) * (qg_ref[...] * SCALE)
            kn[rows, :] = (kv * rk) * kg_ref[...]
            silu, dsilu = _silu_parts(g_ref[rows, :])
            dy = dy_ref[rows, :]
            at = att_ref[rows, :]
            do = dy * silu
            do_s[rows, :] = do
            dd_s[rows, :] = _split_dot(do * at, bdv)
            dg_ref[rows, :] = (dy * at * dsilu).astype(BF16)
            dka[rows, :] = jnp.zeros((tn, 128), F32)
            dva[rows, :] = jnp.zeros((tn, 128), F32)
            return carry
        lax.fori_loop(0, S // tn, prepare, 0)

        kt = lax.broadcasted_iota(jnp.int32, (2 * BLOCK, 2 * BLOCK), 0)
        qt = lax.broadcasted_iota(jnp.int32, (2 * BLOCK, 2 * BLOCK), 1) % BLOCK
        band_mask_t = ((kt < BLOCK) & (kt >= qt)) | ((kt >= BLOCK) & ((kt - BLOCK) <= qt))

        def per_query_row(t):
            tt = t.T
            return jnp.concatenate([tt[0:1, :], tt[HEAD_DIM:HEAD_DIM + 1, :]], axis=1)

        def fill(blk, d):
            tokens = _block_tokens(blk, d, S)
            kc[_padded_block(blk), :] = kn[tokens, :].astype(BF16)
            vc[_padded_block(blk), :] = v_ref[tokens, :].astype(BF16)

        def block(blk, d):
            tokens = _block_tokens(blk, d, S)
            keys = pl.ds(pl.multiple_of(blk * BLOCK, BLOCK), 2 * BLOCK)
            first = (blk & (S // d // BLOCK - 1)) == 0
            q2 = _two_heads(qn[tokens, :].astype(BF16), lo)
            do2 = _two_heads(do_s[tokens, :].astype(BF16), lo)
            lse_row = per_query_row(lse_ref[tokens, :])
            dd_row = per_query_row(dd_s[tokens, :])
            kb = kc[keys, :]
            vb = vc[keys, :]
            st = jnp.where(band_mask_t, _nt(kb, q2), NEG)
            st = jnp.concatenate([st[:BLOCK] + jnp.where(first, NEG, 0.0), st[BLOCK:]], axis=0)
            pt = jnp.exp(st - lse_row)
            dst = pt * (_nt(vb, do2) - dd_row)
            ptb = pt.astype(BF16)
            dstb = dst.astype(BF16)
            dv_band = jnp.dot(ptb, do2, preferred_element_type=F32)
            dk_band = jnp.dot(dstb, q2, preferred_element_type=F32)
            before = _block_tokens(jnp.where(first, blk, blk - 1), d, S)
            dka[before, :] = dka[before, :] + dk_band[:BLOCK]
            dva[before, :] = dva[before, :] + dv_band[:BLOCK]
            dka[tokens, :] = dka[tokens, :] + dk_band[BLOCK:]
            dva[tokens, :] = dva[tokens, :] + dv_band[BLOCK:]
            dq2 = _tn(dstb, kb)
            dq = jnp.where(lo, dq2[:BLOCK], dq2[BLOCK:])
            dqa[tokens, :] = dq if d == 1 else dqa[tokens, :] + dq

        for d in DILATIONS:
            _for_blocks(S // BLOCK, 4, functools.partial(fill, d=d))
            _for_blocks(S // BLOCK, 8, functools.partial(block, d=d))

        def out_step(i, carry):
            dqg, dkg = carry
            rows = pl.ds(pl.multiple_of(i * tn, tn), tn)
            rq = rq_s[rows, :]
            rk = rk_s[rows, :]
            qh = q_ref[rows, :] * rq
            kh = k_ref[rows, :] * rk
            dqs = dqa[rows, :] * SCALE
            dkn = dka[rows, :]
            aq = dqs * qg_ref[...]
            ak = dkn * kg_ref[...]
            dq_ref[rows, :] = (rq * (aq - qh * (_split_dot(aq * qh, bdv) * (1.0 / HEAD_DIM)))).astype(BF16)
            dk_ref[rows, :] = (rk * (ak - kh * (_split_dot(ak * kh, bdv) * (1.0 / HEAD_DIM)))).astype(BF16)
            dv_ref[rows, :] = dva[rows, :].astype(BF16)
            dqg = dqg + jnp.sum(dqs * qh, axis=0, keepdims=True)
            dkg = dkg + jnp.sum(dkn * kh, axis=0, keepdims=True)
            return dqg, dkg
        zero = jnp.zeros((1, 128), F32)
        dqg, dkg = lax.fori_loop(0, S // tn, out_step, (zero, zero))
        dqg_ref[0] = dqg
        dkg_ref[0] = dkg

    col = lambda j0: pl.BlockSpec((S, 128), lambda p, j0=j0: (0, j0 + p))
    col1 = lambda j0: pl.BlockSpec((S, 128), lambda p, j0=j0: (0, j0 + p), pipeline_mode=pl.Buffered(1))
    const = lambda shape: pl.BlockSpec(shape, lambda p: (0,) * len(shape))
    out = pl.BlockSpec((S, 128), lambda p: (0, p))
    gain_out = pl.BlockSpec((1, 1, 128), lambda p: (p, 0, 0))
    piece = pltpu.HBM((S, ATTN_WIDTH), BF16)
    gains = pltpu.HBM((npairs, 1, 128), F32)
    f32buf = pltpu.VMEM((S, 128), F32)
    bf16pad = pltpu.VMEM((S + BLOCK, 128), BF16)
    return pl.pallas_call(
        body, name="attn_bwd", grid=(npairs,),
        in_specs=[col(COL_AQ), col(COL_AK), col(COL_AV), col1(COL_AG), col1(GMLP_WIDTH // 128), col1(0), col(0),
                  const((1, 128)), const((1, 128)), const((128, 128))],
        out_specs=[out, out, out, out, gain_out, gain_out],
        out_shape=[piece, piece, piece, piece, gains, gains],
        scratch_shapes=[f32buf, f32buf, f32buf, f32buf, bf16pad, bf16pad, f32buf, f32buf, f32buf, f32buf, f32buf],
        compiler_params=_params(60, ("arbitrary",)),
    )(*_hbm(proj, proj, proj, proj, dycat, att, lse, qg2, kg2, bd))


def _mem_kv(mem, gain, wkv_bf, kg4, bd):
    def body(mem_ref, g_ref, w_ref, kg_ref, bd_ref, hm_ref, kraw_ref, mk_ref, mv_ref):
        mv_ = mem_ref[...]
        r = lax.rsqrt(jnp.mean(mv_ * mv_, axis=-1, keepdims=True) + EPS)
        hm = ((mv_ * r) * g_ref[...]).astype(BF16)
        hm_ref[...] = hm
        kv = jnp.dot(hm, w_ref[...], preferred_element_type=F32)
        kraw = kv[:, :MEM_WIDTH]
        kraw_ref[...] = kraw
        ms = _split_dot(kraw * kraw, bd_ref[...]) * (1.0 / HEAD_DIM)
        mk_ref[...] = (kraw * lax.rsqrt(ms + EPS)) * kg_ref[...]
        mv_ref[...] = kv[:, MEM_WIDTH:]

    sq = jax.ShapeDtypeStruct((MEM_LEN, MEM_WIDTH), F32)
    return pl.pallas_call(
        body, name="mem_kv",
        out_shape=[jax.ShapeDtypeStruct((MEM_LEN, D_MODEL), BF16), sq, sq, sq],
        compiler_params=_params(16),
    )(mem, gain, wkv_bf, kg4, bd)


def _mem_fwd(proj, mk, mv, qg4, bd):
    S = proj.shape[0]
    tm = 512

    def body(q_ref, g_ref, mk_ref, mv_ref, qg_ref, bd_ref, y_ref, om_ref):
        qv = q_ref[...]
        ms = _split_dot(qv * qv, bd_ref[...]) * (1.0 / HEAD_DIM)
        qs = (qv * lax.rsqrt(ms + EPS)) * (qg_ref[...] * SCALE)
        mkb = mk_ref[...].astype(BF16)
        mvb = mv_ref[...].astype(BF16)
        head = _head_index((tm, MEM_WIDTH))
        o = jnp.zeros((tm, MEM_WIDTH), F32)
        for h in range(4):
            s = _nt(jnp.where(head == h, qs, 0.0).astype(BF16), mkb)
            e = jnp.exp(s - jnp.max(s, axis=-1, keepdims=True))
            p = e * (1.0 / jnp.sum(e, axis=-1, keepdims=True))
            o = jnp.where(head == h, jnp.dot(p.astype(BF16), mvb, preferred_element_type=F32), o)
        om_ref[...] = o
        silu, _ = _silu_parts(g_ref[...])
        y_ref[...] = (o * silu).astype(BF16)

    col = lambda j: pl.BlockSpec((tm, MEM_WIDTH), lambda i, j=j: (i, j))
    const = lambda shape: pl.BlockSpec(shape, lambda i: (0,) * len(shape))
    tile = pl.BlockSpec((tm, MEM_WIDTH), lambda i: (i, 0))
    return pl.pallas_call(
        body, name="mem_fwd", grid=(S // tm,),
        in_specs=[col(11), col(12), const((MEM_LEN, MEM_WIDTH)), const((MEM_LEN, MEM_WIDTH)), const((1, MEM_WIDTH)),
                  const((MEM_WIDTH, MEM_WIDTH))],
        out_specs=[tile, tile],
        out_shape=[pltpu.HBM((S, MEM_WIDTH), BF16), pltpu.HBM((S, MEM_WIDTH), F32)],
        compiler_params=_params(24, ("arbitrary",)),
    )(*_hbm(proj, proj, mk, mv, qg4, bd))


def _mem_bwd(proj, dycat, om, mk, mv, qg4, bd):
    S = proj.shape[0]
    tm = 512

    def body(q_ref, g_ref, dy_ref, om_ref, mk_ref, mv_ref, qg_ref, bd_ref,
             dq_ref, dg_ref, dmk_ref, dmv_ref, dqg_ref):
        i = pl.program_id(0)

        @pl.when(i == 0)
        def _():
            dmk_ref[...] = jnp.zeros_like(dmk_ref)
            dmv_ref[...] = jnp.zeros_like(dmv_ref)
            dqg_ref[...] = jnp.zeros_like(dqg_ref)

        bdv = bd_ref[...]
        qv = q_ref[...]
        rq = lax.rsqrt(_split_dot(qv * qv, bdv) * (1.0 / HEAD_DIM) + EPS)
        qh = qv * rq
        qs = qh * (qg_ref[...] * SCALE)
        silu, dsilu = _silu_parts(g_ref[...])
        dy = dy_ref[...]
        o = om_ref[...]
        do = dy * silu
        dg_ref[...] = (dy * o * dsilu).astype(BF16)
        dd = _split_dot(do * o, bdv)
        mkb = mk_ref[...].astype(BF16)
        mvb = mv_ref[...].astype(BF16)
        head = _head_index((tm, MEM_WIDTH))
        dqs = jnp.zeros((tm, MEM_WIDTH), F32)
        for h in range(4):
            qhd = jnp.where(head == h, qs, 0.0).astype(BF16)
            doh = jnp.where(head == h, do, 0.0).astype(BF16)
            s = _nt(qhd, mkb)
            e = jnp.exp(s - jnp.max(s, axis=-1, keepdims=True))
            p = e * (1.0 / jnp.sum(e, axis=-1, keepdims=True))
            ds = p * (_nt(doh, mvb) - dd[:, h * HEAD_DIM:h * HEAD_DIM + 1])
            dsb = ds.astype(BF16)
            dmv_ref[...] += _tn(p.astype(BF16), doh)
            dmk_ref[...] += _tn(dsb, qhd)
            dqs = jnp.where(head == h, jnp.dot(dsb, mkb, preferred_element_type=F32), dqs)
        dqs = dqs * SCALE
        a = dqs * qg_ref[...]
        dq_ref[...] = (rq * (a - qh * (_split_dot(a * qh, bdv) * (1.0 / HEAD_DIM)))).astype(BF16)
        dqg_ref[...] += jnp.sum(dqs * qh, axis=0, keepdims=True)

    col = lambda j: pl.BlockSpec((tm, MEM_WIDTH), lambda i, j=j: (i, j))
    const = lambda shape: pl.BlockSpec(shape, lambda i: (0,) * len(shape))
    tile = pl.BlockSpec((tm, MEM_WIDTH), lambda i: (i, 0))
    piece = pltpu.HBM((S, MEM_WIDTH), BF16)
    sq = pltpu.HBM((MEM_LEN, MEM_WIDTH), F32)
    return pl.pallas_call(
        body, name="mem_bwd", grid=(S // tm,),
        in_specs=[col(11), col(12), col(3), tile, const((MEM_LEN, MEM_WIDTH)), const((MEM_LEN, MEM_WIDTH)),
                  const((1, MEM_WIDTH)), const((MEM_WIDTH, MEM_WIDTH))],
        out_specs=[tile, tile, const((MEM_LEN, MEM_WIDTH)), const((MEM_LEN, MEM_WIDTH)), const((1, MEM_WIDTH))],
        out_shape=[piece, piece, sq, sq, pltpu.HBM((1, MEM_WIDTH), F32)],
        compiler_params=_params(32, ("arbitrary",)),
    )(*_hbm(proj, proj, dycat, om, mk, mv, qg4, bd))


def _mem_kv_bwd(dmk, dmv, kraw, mem, gain, kg4, wkv_bf, hm_bf, bd):
    def body(dmk_ref, dmv_ref, kraw_ref, mem_ref, g_ref, kg_ref, w_ref, hm_ref, bd_ref, dw_ref, dg_ref, dkg_ref):
        bdv = bd_ref[...]
        kraw = kraw_ref[...]
        rk = lax.rsqrt(_split_dot(kraw * kraw, bdv) * (1.0 / HEAD_DIM) + EPS)
        kh = kraw * rk
        dmkv = dmk_ref[...]
        a = dmkv * kg_ref[...]
        dkraw = rk * (a - kh * (_split_dot(a * kh, bdv) * (1.0 / HEAD_DIM)))
        dkg_ref[...] = jnp.sum(dmkv * kh, axis=0, keepdims=True)
        dkv = jnp.concatenate([dkraw, dmv_ref[...]], axis=1).astype(BF16)
        dw = _tn(hm_ref[...], dkv).astype(BF16)
        rows_blk = D_MODEL // N_DEV
        for j in range(N_DEV):
            dw_ref[j] = dw[rows_blk * j:rows_blk * (j + 1)]
        dhm = _nt(dkv, w_ref[...])
        mv_ = mem_ref[...]
        r = lax.rsqrt(jnp.mean(mv_ * mv_, axis=-1, keepdims=True) + EPS)
        dg_ref[...] = jnp.sum(dhm * (mv_ * r), axis=0, keepdims=True)

    return pl.pallas_call(
        body, name="mem_kv_bwd",
        out_shape=[jax.ShapeDtypeStruct((N_DEV, D_MODEL // N_DEV, 2 * MEM_WIDTH), BF16),
                   jax.ShapeDtypeStruct((1, D_MODEL), F32), jax.ShapeDtypeStruct((1, MEM_WIDTH), F32)],
        compiler_params=_params(24),
    )(dmk, dmv, kraw, mem, gain, kg4, wkv_bf, hm_bf, bd)


def _out_loss(yg, ya, ym, x, tgt, wout_bf):
    S, D = x.shape
    tm = 256

    nsteps = S // tm
    rows_blk = D // N_DEV

    def body(yg_ref, ya_ref, ym_ref, x_ref, t_ref, w_ref, dout_ref, dycat_ref, dw_ref, loss_ref, acc_ref):
        i = pl.program_id(0)

        @pl.when(i == 0)
        def _():
            acc_ref[...] = jnp.zeros_like(acc_ref)
            loss_ref[...] = jnp.zeros_like(loss_ref)

        ycat = jnp.concatenate([yg_ref[...], ya_ref[...], ym_ref[...]], axis=1)
        w = w_ref[...]
        diff = (x_ref[...] + jnp.dot(ycat, w, preferred_element_type=F32)) - t_ref[...]
        loss_ref[...] += jnp.sum(diff * diff, axis=0, keepdims=True)
        dout = diff * (1.0 / D)
        dout_ref[...] = dout
        db = dout.astype(BF16)
        dycat_ref[...] = _nt(db, w)
        acc_ref[...] += _tn(ycat, db)

        @pl.when(i == nsteps - 1)
        def _():
            for j in range(N_DEV):
                dw_ref[j] = acc_ref[rows_blk * j:rows_blk * (j + 1), :].astype(BF16)

    tile = lambda w: pl.BlockSpec((tm, w), lambda i: (i, 0))
    const = lambda shape: pl.BlockSpec(shape, lambda i: (0,) * len(shape))
    return pl.pallas_call(
        body, name="out_loss", grid=(nsteps,),
        in_specs=[tile(GMLP_WIDTH), tile(ATTN_WIDTH), tile(MEM_WIDTH), tile(D), tile(D), const((D, D))],
        out_specs=[tile(D), tile(D), const((N_DEV, rows_blk, D)), const((1, D))],
        out_shape=[pltpu.HBM((S, D), F32), pltpu.HBM((S, D), F32),
                   pltpu.HBM((N_DEV, rows_blk, D), BF16), pltpu.HBM((1, D), F32)],
        scratch_shapes=[pltpu.VMEM((D, D), F32)],
        compiler_params=_params(40, ("arbitrary",)),
    )(*_hbm(yg, ya, ym, x, tgt, wout_bf))


def _piece_specs(pieces, tm):
    return [pl.BlockSpec((tm, p.shape[1]), lambda i: (i, 0)) for p in pieces]


def _in_bwd_dx(pieces, x, dout, gain, w_t, dw_blocks):
    S, D = x.shape
    N = w_t.shape[0]
    tm = 256
    n = len(pieces)
    nsteps = S // tm
    middle_step = nsteps // 4

    def body(*refs):
        piece_refs = refs[:n]
        x_ref, dout_ref, g_ref, w_ref, dwb_ref, gx_ref, dg_ref, gw_ref = refs[n:n + 8]
        rs = _ReduceScatter([dwb_ref], [gw_ref], *refs[n + 8:])
        i = pl.program_id(0)

        @pl.when(i == 0)
        def _():
            dg_ref[...] = jnp.zeros_like(dg_ref)
            rs.start()

        @pl.when(i == middle_step)
        def _():
            rs.middle()

        dproj = jnp.concatenate([r[...] for r in piece_refs], axis=1)
        dh = jnp.dot(dproj, w_ref[...], preferred_element_type=F32)
        xv = x_ref[...]
        r = lax.rsqrt(jnp.mean(xv * xv, axis=-1, keepdims=True) + EPS)
        xh = xv * r
        a = dh * g_ref[...]
        gx_ref[...] = dout_ref[...] + r * (a - xh * jnp.mean(a * xh, axis=-1, keepdims=True))
        dg_ref[...] += jnp.sum(dh * xh, axis=0, keepdims=True)

        @pl.when(i == nsteps - 1)
        def _():
            rs.finish()

    tile = pl.BlockSpec((tm, D), lambda i: (i, 0))
    const = lambda shape: pl.BlockSpec(shape, lambda i: (0,) * len(shape))
    vmem = pl.BlockSpec(memory_space=pltpu.VMEM)
    return pl.pallas_call(
        body, name="in_bwd_dx", grid=(nsteps,),
        in_specs=_piece_specs(pieces, tm) + [tile, tile, const((1, D)), const((N, D)), vmem],
        out_specs=[tile, const((1, D)), vmem],
        out_shape=[pltpu.HBM((S, D), F32), pltpu.HBM((1, D), F32), jax.ShapeDtypeStruct(dw_blocks.shape[1:], F32)],
        scratch_shapes=_reduce_scatter_scratch([dw_blocks]),
        compiler_params=_params(56, ("arbitrary",)),
    )(*_hbm(*pieces, x, dout, gain, w_t), dw_blocks)


def _in_bwd_dw(pieces, h_bf, others):
    S, D = h_bf.shape
    N = sum(p.shape[1] for p in pieces)
    n_blk = N // N_DEV
    tm = 512
    n = len(pieces)
    k = len(others)
    nsteps = S // tm

    def body(*refs):
        piece_refs = refs[:n]
        h_ref = refs[n]
        other_refs = refs[n + 1:n + 1 + k]
        dw_ref = refs[n + 1 + k]
        sum_refs = refs[n + 2 + k:n + 2 + 2 * k]
        acc_ref = refs[n + 2 + 2 * k]
        rs = _ReduceScatter(other_refs, sum_refs, *refs[n + 3 + 2 * k:])
        i = pl.program_id(0)

        @pl.when(i == 0)
        def _():
            acc_ref[...] = jnp.zeros_like(acc_ref)
            rs.start()

        @pl.when(i == 1)
        def _():
            rs.middle()

        dproj = jnp.concatenate([r[...] for r in piece_refs], axis=1)
        acc_ref[...] += _tn(h_ref[...], dproj)

        @pl.when(i == nsteps - 1)
        def _():
            for j in range(N_DEV):
                dw_ref[j] = acc_ref[:, n_blk * j:n_blk * (j + 1)].T.astype(BF16)
            rs.finish()

    vmem = pl.BlockSpec(memory_space=pltpu.VMEM)
    return pl.pallas_call(
        body, name="in_bwd_dw", grid=(nsteps,),
        in_specs=_piece_specs(pieces, tm) + [pl.BlockSpec((tm, D), lambda i: (i, 0))] + [vmem] * k,
        out_specs=[pl.BlockSpec((N_DEV, n_blk, D), lambda i: (0, 0, 0))] + [vmem] * k,
        out_shape=[pltpu.HBM((N_DEV, n_blk, D), BF16)] + [jax.ShapeDtypeStruct(o.shape[1:], F32) for o in others],
        scratch_shapes=[pltpu.VMEM((D, N), F32)] + _reduce_scatter_scratch(others),
        compiler_params=_params(56, ("arbitrary",)),
    )(*_hbm(*pieces, h_bf), *others)


def _row_step(m):
    return max(t for t in range(16, 257, 16) if m % t == 0)


def _place():
    x, y, c = lax.axis_index("x"), lax.axis_index("y"), lax.axis_index("c")
    chips = [(1 - x, y), (x, 1 - y), (1 - x, 1 - y)]
    return x, y, c, chips


class _AllGather:
    def __init__(self, srcs, outs, send_sems, recv_sems, local_sems):
        self.srcs, self.outs, self.n = srcs, outs, len(srcs)
        self.send_sems, self.recv_sems, self.local_sems = send_sems, recv_sems, local_sems

    def _rows(self, a, px, py, pc):
        m = self.srcs[a].shape[0]
        return self.outs[a].at[pl.ds((4 * px + 2 * py + pc) * m, m), :]

    def _copy(self, a, k, block, to, src=None):
        return pltpu.make_async_remote_copy(
            src_ref=self._rows(a, *block) if src is None else src, dst_ref=self._rows(a, *block),
            send_sem=self.send_sems.at[a, k], recv_sem=self.recv_sems.at[a, k], device_id=to, device_id_type=MESH)

    def _mine(self):
        x, y, c, _ = _place()
        return [pltpu.make_async_copy(self.srcs[a], self._rows(a, x, y, c), self.local_sems.at[a])
                for a in range(self.n)]

    def _first(self):
        x, y, c, chips = _place()
        out = []
        for a in range(self.n):
            out.append(self._copy(a, 0, (x, y, c), (x, y, 1 - c), src=self.srcs[a]))
            out += [self._copy(a, 1 + j, (x, y, c), (*chip, c), src=self.srcs[a]) for j, chip in enumerate(chips)]
        return out

    def _passed(self, j):
        x, y, c, chips = _place()
        return [self._copy(a, 4 + j, (*chips[j], c), (x, y, 1 - c)) for a in range(self.n)]

    def start(self):
        for cp in self._mine() + self._first():
            cp.start()

    def from_chip(self, j):
        x, y, c, chips = _place()
        for a in range(self.n):
            self._copy(a, 1 + j, (*chips[j], c), (x, y, c)).wait_recv()
        for cp in self._passed(j):
            cp.start()

    def from_sibling(self, j=None):
        x, y, c, chips = _place()
        for a in range(self.n):
            block = (x, y, 1 - c) if j is None else (*chips[j], 1 - c)
            self._copy(a, 0 if j is None else 4 + j, block, (x, y, c)).wait_recv()

    def from_self(self):
        for cp in self._mine():
            cp.wait()

    def finish(self):
        for cp in self._first() + self._passed(0) + self._passed(1) + self._passed(2):
            cp.wait_send()

    def run(self):
        self.start()
        self.from_self()
        for j in range(3):
            self.from_chip(j)
        self.from_sibling()
        for j in range(3):
            self.from_sibling(j)
        self.finish()


def _gather_proj(x, gain, shards, xpos):
    S, D = x.shape
    n = len(shards)
    N = N_DEV * shards[0].shape[0]
    half = N // 2
    tm = 256
    nsteps = S // tm

    def body(*refs):
        xpos_ref, x_ref, g_ref = refs[:3]
        ins = refs[3:3 + n]
        proj_ref, h_ref = refs[3 + n:5 + n]
        outs = refs[5 + n:5 + 2 * n]
        casts = refs[5 + 2 * n:5 + 3 * n]
        whole = refs[5 + 3 * n:5 + 4 * n]
        ag = _AllGather(casts, whole, *refs[5 + 4 * n:8 + 4 * n])
        out_sems = refs[8 + 4 * n]
        hh, i = pl.program_id(0), pl.program_id(1)

        @pl.when((hh == 0) & (i == 0))
        def _():
            for a in range(n):
                tr = _row_step(ins[a].shape[0])

                def cast(r, carry, a=a, tr=tr):
                    rows = pl.ds(pl.multiple_of(r * tr, tr), tr)
                    casts[a][rows, :] = ins[a][rows, :].astype(BF16)
                    return carry
                lax.fori_loop(0, ins[a].shape[0] // tr, cast, 0)
            ag.start()
            ag.from_self()
            ag.from_chip(1)
            ag.from_sibling()
            ag.from_sibling(1)

        @pl.when((hh == 1) & (i == 0))
        def _():
            for j in (0, 2):
                ag.from_chip(j)
            for j in (0, 2):
                ag.from_sibling(j)

        xv = x_ref[...]
        r = lax.rsqrt(jnp.mean(xv * xv, axis=-1, keepdims=True) + EPS)
        h = ((xv * r) * g_ref[...]).astype(BF16)

        @pl.when(hh == 0)
        def _():
            h_ref[...] = h

        which = (xpos_ref[0] + hh) % 2
        w_half = whole[0][pl.ds(pl.multiple_of(which * half, half), half), :]
        proj_ref[...] = _nt(h, w_half)

        @pl.when((hh == 1) & (i == nsteps - 1))
        def _():
            ag.finish()
            to_results = [pltpu.make_async_copy(whole[a], outs[a], out_sems.at[a]) for a in range(n)]
            for cp in to_results:
                cp.start()
            for cp in to_results:
                cp.wait()

    vmem = pl.BlockSpec(memory_space=pltpu.VMEM)
    hbm = pl.BlockSpec(memory_space=pl.ANY)
    gathered = [(N_DEV * a.shape[0], a.shape[1]) for a in shards]
    grid_spec = pltpu.PrefetchScalarGridSpec(
        num_scalar_prefetch=1, grid=(2, nsteps),
        in_specs=[pl.BlockSpec((tm, D), lambda hh, i, xp: (i, 0)), pl.BlockSpec((1, D), lambda hh, i, xp: (0, 0))]
        + [vmem] * n,
        out_specs=[pl.BlockSpec((tm, half), lambda hh, i, xp: (i, (xp[0] + hh) % 2)),
                   pl.BlockSpec((tm, D), lambda hh, i, xp: (i * (1 - hh) + (nsteps - 1) * hh, 0))] + [hbm] * n,
        scratch_shapes=[pltpu.VMEM(a.shape, BF16) for a in shards] + [pltpu.VMEM(g, BF16) for g in gathered]
        + [pltpu.SemaphoreType.DMA((n, 7)), pltpu.SemaphoreType.DMA((n, 7)), pltpu.SemaphoreType.DMA((n,)),
           pltpu.SemaphoreType.DMA((n,))])
    return pl.pallas_call(
        body, name="gather_proj", grid_spec=grid_spec,
        out_shape=[pltpu.HBM((S, N), F32), pltpu.HBM((S, D), BF16)] + [pltpu.HBM(g, BF16) for g in gathered],
        compiler_params=_params(48, ("arbitrary", "arbitrary")),
    )(xpos, *_hbm(x, gain), *shards)


ROW_NORM, ROW_MEM_NORM, ROW_V_GAIN, ROW_B, ROW_ATTN_GAINS, ROW_MEM_GAINS, ROW_W_S, ROW_LOSS = 0, 8, 16, 18, 22, 23, 24, 536
SMALL_ROWS = 544


def _gather_small(dgain, dmgain, dvg, db2, dqg, dkg, dmqg, dmkg, dws, sq):
    def body(dgain_ref, dmgain_ref, dvg_ref, db2_ref, dqg_ref, dkg_ref, dmqg_ref, dmkg_ref, dws_ref, sq_ref,
             out_ref, mine, send_sems, recv_sems, local_sems):
        first = lax.broadcasted_iota(jnp.int32, (1, 128), 1) < HEAD_DIM
        for i in range(8):
            cols = slice(128 * i, 128 * (i + 1))
            mine[ROW_NORM + i:ROW_NORM + i + 1, :] = dgain_ref[:, cols]
            mine[ROW_MEM_NORM + i:ROW_MEM_NORM + i + 1, :] = dmgain_ref[:, cols]
            mine[ROW_LOSS + i:ROW_LOSS + i + 1, :] = sq_ref[:, cols]
        mine[ROW_V_GAIN:ROW_V_GAIN + 1, :] = dvg_ref[:, 0:128]
        mine[ROW_V_GAIN + 1:ROW_V_GAIN + 2, :] = dvg_ref[:, 128:256]
        bt = db2_ref[...].T
        for h in range(4):
            mine[ROW_B + h:ROW_B + h + 1, :] = bt[HEAD_DIM * h:HEAD_DIM * h + 1, :]

        def fold_heads(t):
            return t + pltpu.roll(t, HEAD_DIM, axis=1)
        aq = fold_heads(dqg_ref[0] + dqg_ref[1] + dqg_ref[2] + dqg_ref[3])
        ak = fold_heads(dkg_ref[0] + dkg_ref[1] + dkg_ref[2] + dkg_ref[3])
        mine[ROW_ATTN_GAINS:ROW_ATTN_GAINS + 1, :] = jnp.where(first, aq, ak)
        mq = fold_heads(dmqg_ref[:, 0:128] + dmqg_ref[:, 128:256])
        mk = fold_heads(dmkg_ref[:, 0:128] + dmkg_ref[:, 128:256])
        mine[ROW_MEM_GAINS:ROW_MEM_GAINS + 1, :] = jnp.where(first, mq, mk)
        mine[ROW_W_S:ROW_W_S + 4 * CHUNK, :] = dws_ref[...]
        _AllGather([mine], [out_ref], send_sems, recv_sems, local_sems).run()

    return pl.pallas_call(
        body, name="gather_small_grads",
        out_shape=jax.ShapeDtypeStruct((N_DEV * SMALL_ROWS, 128), F32),
        scratch_shapes=[pltpu.VMEM((SMALL_ROWS, 128), F32), pltpu.SemaphoreType.DMA((1, 7)),
                        pltpu.SemaphoreType.DMA((1, 7)), pltpu.SemaphoreType.DMA((1,))],
        compiler_params=_params(16),
    )(dgain, dmgain, dvg, db2, dqg, dkg, dmqg, dmkg, dws, sq)


def _reduce_scatter_scratch(arrs):
    n = len(arrs)
    return ([pltpu.VMEM((4,) + a.shape[1:], BF16) for a in arrs] + [pltpu.VMEM((3,) + a.shape[1:], BF16) for a in arrs]
            + [pltpu.SemaphoreType.DMA((n, 7)), pltpu.SemaphoreType.DMA((n, 7))])


class _ReduceScatter:
    def __init__(self, ins, outs, *scratch):
        n = len(ins)
        self.n, self.ins, self.outs = n, ins, outs
        self.half, self.quarter = scratch[:n], scratch[n:2 * n]
        self.send_sems, self.recv_sems = scratch[2 * n:]

    def _to_sibling(self):
        x, y, c, _ = _place()
        return [pltpu.make_async_remote_copy(
            src_ref=self.ins[a].at[2 * q + (1 - c)], dst_ref=self.half[a].at[q], send_sem=self.send_sems.at[a, q],
            recv_sem=self.recv_sems.at[a, q], device_id=(x, y, 1 - c), device_id_type=MESH)
            for a in range(self.n) for q in range(4)]

    def _to_chips(self):
        _, _, c, chips = _place()
        return [pltpu.make_async_remote_copy(
            src_ref=self.half[a].at[2 * chip[0] + chip[1]], dst_ref=self.quarter[a].at[k],
            send_sem=self.send_sems.at[a, 4 + k], recv_sem=self.recv_sems.at[a, 4 + k], device_id=(*chip, c),
            device_id_type=MESH) for a in range(self.n) for k, chip in enumerate(chips)]

    def _rows(self, a, fn):
        m = self.ins[a].shape[1]
        tr = _row_step(m)

        def step(i, carry):
            fn(pl.ds(pl.multiple_of(i * tr, tr), tr))
            return carry
        lax.fori_loop(0, m // tr, step, 0)

    def start(self):
        for cp in self._to_sibling():
            cp.start()

    def middle(self):
        _, _, c, _ = _place()
        for cp in self._to_sibling():
            cp.wait_recv()
        for a in range(self.n):
            for q in range(4):
                def add_half(rows, a=a, q=q):
                    both = self.ins[a][2 * q + c, rows, :].astype(F32) + self.half[a][q, rows, :].astype(F32)
                    self.half[a][q, rows, :] = both.astype(BF16)
                self._rows(a, add_half)
        for cp in self._to_chips():
            cp.start()

    def finish(self):
        x, y, _, _ = _place()
        for cp in self._to_chips():
            cp.wait_recv()
        for a in range(self.n):
            def add_quarters(rows, a=a):
                f = lambda t: t.astype(F32)
                self.outs[a][rows, :] = ((f(self.half[a][2 * x + y, rows, :]) + f(self.quarter[a][0, rows, :]))
                                         + (f(self.quarter[a][1, rows, :]) + f(self.quarter[a][2, rows, :])))
            self._rows(a, add_quarters)
        for cp in self._to_sibling() + self._to_chips():
            cp.wait_send()


def _adamw_math(w, g, m, v):
    m = ADAM_B1 * m + (1.0 - ADAM_B1) * g
    v = ADAM_B2 * v + (1.0 - ADAM_B2) * (g * g)
    m_hat = m / (1.0 - ADAM_B1 ** ADAM_STEP)
    v_hat = v / (1.0 - ADAM_B2 ** ADAM_STEP)
    delta = -ADAM_LR * (m_hat / (jnp.sqrt(v_hat) + ADAM_EPS) + ADAM_WD * w)
    return delta, m, v


def _adamw(w, g, m, v, name):
    R, C = w.shape
    tr = _row_step(R)

    def body(w_ref, g_ref, m_ref, v_ref, d_ref, nm_ref, nv_ref):
        d_ref[...], nm_ref[...], nv_ref[...] = _adamw_math(w_ref[...], g_ref[...], m_ref[...], v_ref[...])

    tile = pl.BlockSpec((tr, C), lambda i: (i, 0))
    out = pltpu.HBM((R, C), F32)
    return pl.pallas_call(
        body, name=name, grid=(R // tr,), in_specs=[tile] * 4, out_specs=[tile] * 3, out_shape=[out] * 3,
        compiler_params=_params(16, ("arbitrary",)),
    )(*_hbm(w, g, m, v))


SMALL = ("norm_gain", "gmlp_v_gain", "gmlp_w_s", "gmlp_b", "attn_q_gain", "attn_k_gain", "mem_norm_gain",
         "mem_q_gain", "mem_k_gain")
WEIGHTS = ("norm_gain", "w_in", "gmlp_v_gain", "gmlp_w_s", "gmlp_b", "attn_q_gain", "attn_k_gain",
           "mem_norm_gain", "w_mem_kv", "mem_q_gain", "mem_k_gain", "w_out")


def _adamw_small(w, m, v, g_all):
    k = len(SMALL)
    half = slice(0, HEAD_DIM), slice(HEAD_DIM, 2 * HEAD_DIM)

    def body(*refs):
        w_refs, m_refs, v_refs = refs[:k], refs[k:2 * k], refs[2 * k:3 * k]
        g_ref = refs[3 * k]
        outs = refs[3 * k + 1:7 * k + 1]
        loss_ref, gsum = refs[7 * k + 1:]

        part = SMALL_ROWS // 4
        for p in range(4):
            acc = g_ref[part * p:part * (p + 1), :]
            for dev in range(1, N_DEV):
                acc = acc + g_ref[dev * SMALL_ROWS + part * p:dev * SMALL_ROWS + part * (p + 1), :]
            gsum[part * p:part * (p + 1), :] = acc

        def update(name, at, g):
            i = SMALL.index(name)
            d, nm, nv = _adamw_math(w_refs[i][at], g, m_refs[i][at], v_refs[i][at])
            outs[i][at], outs[k + i][at], outs[2 * k + i][at], outs[3 * k + i][at] = g, d, nm, nv

        for i in range(8):
            at = (slice(0, 1), slice(128 * i, 128 * (i + 1)))
            update("norm_gain", at, gsum[ROW_NORM + i:ROW_NORM + i + 1, :])
            update("mem_norm_gain", at, gsum[ROW_MEM_NORM + i:ROW_MEM_NORM + i + 1, :])
        for h in range(4):
            row = (0, slice(h, h + 1), slice(None))
            update("gmlp_v_gain", row, gsum[ROW_V_GAIN + h // 2:ROW_V_GAIN + h // 2 + 1, half[h % 2]])
            update("gmlp_b", row, gsum[ROW_B + h:ROW_B + h + 1, :])
            update("gmlp_w_s", (0, h), gsum[ROW_W_S + CHUNK * h:ROW_W_S + CHUNK * (h + 1), :])
        whole = (slice(0, 1), slice(None))
        update("attn_q_gain", whole, gsum[ROW_ATTN_GAINS:ROW_ATTN_GAINS + 1, half[0]])
        update("attn_k_gain", whole, gsum[ROW_ATTN_GAINS:ROW_ATTN_GAINS + 1, half[1]])
        update("mem_q_gain", whole, gsum[ROW_MEM_GAINS:ROW_MEM_GAINS + 1, half[0]])
        update("mem_k_gain", whole, gsum[ROW_MEM_GAINS:ROW_MEM_GAINS + 1, half[1]])
        loss_ref[...] = jnp.sum(gsum[ROW_LOSS:ROW_LOSS + 8, :], keepdims=True) * (0.5 / D_MODEL)

    shapes = [jax.ShapeDtypeStruct(w[name].shape, F32) for name in SMALL]
    res = pl.pallas_call(
        body, name="adamw_small",
        out_shape=shapes * 4 + [jax.ShapeDtypeStruct((1, 1), F32)],
        scratch_shapes=[pltpu.VMEM((SMALL_ROWS, 128), F32)],
        compiler_params=_params(16),
    )(*[w[n] for n in SMALL], *[m[n] for n in SMALL], *[v[n] for n in SMALL], g_all)
    trees = [dict(zip(SMALL, res[j * k:(j + 1) * k])) for j in range(4)]
    return (*trees, res[4 * k])


def _grads(x, mem, tgt, w, shards):
    bd128, bd256 = _head_blockdiag(128), _head_blockdiag(256)
    gain = w["norm_gain"].reshape(1, D_MODEL)
    vg = w["gmlp_v_gain"].reshape(1, GMLP_WIDTH)
    w_s = w["gmlp_w_s"].reshape(4, CHUNK, CHUNK)
    b2 = jnp.repeat(w["gmlp_b"].reshape(4, CHUNK).T, HEAD_DIM, axis=1)
    qg2 = jnp.tile(w["attn_q_gain"].reshape(1, HEAD_DIM), (1, 2))
    kg2 = jnp.tile(w["attn_k_gain"].reshape(1, HEAD_DIM), (1, 2))
    mqg4 = jnp.tile(w["mem_q_gain"].reshape(1, HEAD_DIM), (1, 4))
    mkg4 = jnp.tile(w["mem_k_gain"].reshape(1, HEAD_DIM), (1, 4))
    mgain = w["mem_norm_gain"].reshape(1, D_MODEL)

    xpos = lax.axis_index("x").astype(jnp.int32).reshape(1)
    proj, h_bf, win_t, wkv_bf, wout_bf = _gather_proj(x, gain, shards, xpos)
    yg = _gmlp_fwd(proj, vg, w_s, b2, bd256)
    ya, att, lse = _attn_fwd(proj, qg2, kg2, bd128)
    hm_bf, kraw, mk, mv = _mem_kv(mem, mgain, wkv_bf, mkg4, bd256)
    ym, om = _mem_fwd(proj, mk, mv, mqg4, bd256)
    dout, dycat, dwout, sq = _out_loss(yg, ya, ym, x, tgt, wout_bf)

    du, dgv, dgg, dws, db2, dvg = _gmlp_bwd(proj, dycat, vg, w_s, b2, bd256)
    dq, dk, dv, dag, dqg, dkg = _attn_bwd(proj, dycat, att, lse, qg2, kg2, bd128)
    dmq, dmg, dmk, dmv, dmqg = _mem_bwd(proj, dycat, om, mk, mv, mqg4, bd256)
    dwkv, dmgain, dmkg = _mem_kv_bwd(dmk, dmv, kraw, mem, mgain, mkg4, wkv_bf, hm_bf, bd256)
    pieces = [du, dgv, dgg, dq, dk, dv, dag, dmq, dmg]
    dwin, g_wkv, g_wout = _in_bwd_dw(pieces, h_bf, [dwkv, dwout])
    grad_x, dgain, g_win = _in_bwd_dx(pieces, x, dout, gain, win_t, dwin)
    return grad_x, g_win, g_wkv, g_wout, (dgain, dmgain, dvg, db2, dqg, dkg, dmqg, dmkg, dws, sq)


def kernel(x, mem, norm_gain, w_in, gmlp_v_gain, gmlp_w_s, gmlp_b, attn_q_gain, attn_k_gain, mem_norm_gain, w_mem_kv, mem_q_gain, mem_k_gain, w_out, loss_target, m_norm_gain, m_w_in, m_gmlp_v_gain, m_gmlp_w_s, m_gmlp_b, m_attn_q_gain, m_attn_k_gain, m_mem_norm_gain, m_w_mem_kv, m_mem_q_gain, m_mem_k_gain, m_w_out, v_norm_gain, v_w_in, v_gmlp_v_gain, v_gmlp_w_s, v_gmlp_b, v_attn_q_gain, v_attn_k_gain, v_mem_norm_gain, v_w_mem_kv, v_mem_q_gain, v_mem_k_gain, v_w_out):
    w = dict(norm_gain=norm_gain, w_in=w_in, gmlp_v_gain=gmlp_v_gain, gmlp_w_s=gmlp_w_s, gmlp_b=gmlp_b,
             attn_q_gain=attn_q_gain, attn_k_gain=attn_k_gain, mem_norm_gain=mem_norm_gain, w_mem_kv=w_mem_kv,
             mem_q_gain=mem_q_gain, mem_k_gain=mem_k_gain, w_out=w_out)
    m = dict(norm_gain=m_norm_gain, w_in=m_w_in, gmlp_v_gain=m_gmlp_v_gain, gmlp_w_s=m_gmlp_w_s, gmlp_b=m_gmlp_b,
             attn_q_gain=m_attn_q_gain, attn_k_gain=m_attn_k_gain, mem_norm_gain=m_mem_norm_gain,
             w_mem_kv=m_w_mem_kv, mem_q_gain=m_mem_q_gain, mem_k_gain=m_mem_k_gain, w_out=m_w_out)
    v = dict(norm_gain=v_norm_gain, w_in=v_w_in, gmlp_v_gain=v_gmlp_v_gain, gmlp_w_s=v_gmlp_w_s, gmlp_b=v_gmlp_b,
             attn_q_gain=v_attn_q_gain, attn_k_gain=v_attn_k_gain, mem_norm_gain=v_mem_norm_gain,
             w_mem_kv=v_w_mem_kv, mem_q_gain=v_mem_q_gain, mem_k_gain=v_mem_k_gain, w_out=v_w_out)
    transposed = lambda t: jnp.transpose(t[0])

    grad_x, g_win, g_wkv, g_wout, small = _grads(
        x[0], mem[0], loss_target[0], w, [transposed(w_in), w_mem_kv[0], w_out[0]])
    small_all = _gather_small(*small)

    out_g, out_d, out_m, out_v, loss = _adamw_small(w, m, v, small_all)
    d_, m_, v_ = _adamw(transposed(w_in), g_win, transposed(m_w_in), transposed(v_w_in), "adamw_w_in")
    for tree, t in ((out_g, g_win), (out_d, d_), (out_m, m_), (out_v, v_)):
        tree["w_in"] = jnp.transpose(t)[None]
    for name, g in (("w_mem_kv", g_wkv), ("w_out", g_wout)):
        d_, m_, v_ = _adamw(w[name][0], g, m[name][0], v[name][0], "adamw_" + name)
        out_g[name], out_d[name], out_m[name], out_v[name] = g[None], d_[None], m_[None], v_[None]

    return (loss.reshape(()), grad_x[None], *[out_g[k] for k in WEIGHTS], *[out_d[k] for k in WEIGHTS],
            *[out_m[k] for k in WEIGHTS], *[out_v[k] for k in WEIGHTS])
```

```python
import functools
import math

import jax
import jax.numpy as jnp
from jax import lax
from jax.experimental import pallas as pl
from jax.experimental.pallas import tpu as pltpu

F32 = jnp.float32
BF16 = jnp.bfloat16

N_DEV = 8
D_MODEL = 1024
HEAD_DIM = 64
GMLP_WIDTH = 256
ATTN_WIDTH = 512
MEM_WIDTH = 256
MEM_LEN = 256
IN_WIDTH = 3 * GMLP_WIDTH + 4 * ATTN_WIDTH + 2 * MEM_WIDTH
CHUNK = 128
BLOCK = 128
DILATIONS = (1, 4, 16)
EPS = 1e-6
SCALE = 1.0 / math.sqrt(HEAD_DIM)
NEG = -1e30

ADAM_LR = 0.001
ADAM_B1 = 0.9
ADAM_B2 = 0.999
ADAM_EPS = 1e-08
ADAM_WD = 0.01
ADAM_STEP = 10

MIB = 1024 * 1024
MESH = pl.DeviceIdType.MESH

COL_AQ, COL_AK, COL_AV, COL_AG = 6, 10, 14, 18


def _params(vmem_mib, semantics=None):
    kw = dict(vmem_limit_bytes=vmem_mib * MIB)
    if semantics is not None:
        kw["dimension_semantics"] = semantics
    return pltpu.CompilerParams(**kw)


def _hbm(*arrs):
    return [pltpu.with_memory_space_constraint(a, pltpu.HBM) for a in arrs]


def _split_dot(x, sel_bf):
    hi = x.astype(BF16)
    lo = (x - hi.astype(F32)).astype(BF16)
    return jnp.dot(hi, sel_bf, preferred_element_type=F32) + jnp.dot(lo, sel_bf, preferred_element_type=F32)


def _nt(a, b):
    return lax.dot_general(a, b, (((1,), (1,)), ((), ())), preferred_element_type=F32)


def _tn(a, b):
    return lax.dot_general(a, b, (((0,), (0,)), ((), ())), preferred_element_type=F32)


def _silu_parts(g):
    sg = jax.nn.sigmoid(g)
    return g * sg, sg * (1.0 + g * (1.0 - sg))


def _head_index(shape):
    return lax.shift_right_logical(lax.broadcasted_iota(jnp.int32, shape, 1), HEAD_DIM.bit_length() - 1)


def _head_blockdiag(width):
    i = jnp.arange(width) // HEAD_DIM
    return (i[:, None] == i[None, :]).astype(BF16)


def _gmlp_masked_weights(ws_ref, transpose):
    t = lax.broadcasted_iota(jnp.int32, (CHUNK, CHUNK), 0)
    s = lax.broadcasted_iota(jnp.int32, (CHUNK, CHUNK), 1)
    parts = []
    for h in range(4):
        wm = jnp.where(s <= t, ws_ref[h], 0.0)
        parts.append(wm.T if transpose else wm)
    return jnp.concatenate(parts, axis=1).astype(BF16)


def _head_stack(v, head):
    return jnp.concatenate([jnp.where(head == h, v, 0.0) for h in range(4)], axis=0).astype(BF16)


def _gmlp_fwd(proj, vg, w_s, b2, bd):
    S = proj.shape[0]
    tm = 512

    def body(u_ref, v_ref, g_ref, vg_ref, ws_ref, b2_ref, bd_ref, y_ref):
        v = v_ref[...]
        ms = _split_dot(v * v, bd_ref[...]) * (1.0 / HEAD_DIM)
        vn = (v * lax.rsqrt(ms + EPS)) * vg_ref[...]
        wcat = _gmlp_masked_weights(ws_ref, False)
        head = _head_index((CHUNK, GMLP_WIDTH))
        for c in range(tm // CHUNK):
            rows = slice(c * CHUNK, (c + 1) * CHUNK)
            sp = jnp.dot(wcat, _head_stack(vn[rows], head), preferred_element_type=F32) + b2_ref[...]
            silu, _ = _silu_parts(g_ref[rows, :])
            y_ref[rows, :] = ((u_ref[rows, :] * sp) * silu).astype(BF16)

    col = lambda j: pl.BlockSpec((tm, GMLP_WIDTH), lambda i, j=j: (i, j))
    const = lambda shape: pl.BlockSpec(shape, lambda i: (0,) * len(shape))
    return pl.pallas_call(
        body, name="gmlp_fwd", grid=(S // tm,),
        in_specs=[col(0), col(1), col(2), const((1, GMLP_WIDTH)), const((4, CHUNK, CHUNK)),
                  const((CHUNK, GMLP_WIDTH)), const((GMLP_WIDTH, GMLP_WIDTH))],
        out_specs=pl.BlockSpec((tm, GMLP_WIDTH), lambda i: (i, 0)),
        out_shape=pltpu.HBM((S, GMLP_WIDTH), BF16),
        compiler_params=_params(24, ("arbitrary",)),
    )(*_hbm(proj, proj, proj, vg, w_s, b2, bd))


def _gmlp_bwd(proj, dycat, vg, w_s, b2, bd):
    S = proj.shape[0]
    tm = 512
    nsteps = S // tm

    def body(u_ref, v_ref, g_ref, dy_ref, vg_ref, ws_ref, b2_ref, bd_ref,
             du_ref, dv_ref, dg_ref, dws_ref, db2_ref, dvg_ref):
        i = pl.program_id(0)

        @pl.when(i == 0)
        def _():
            dws_ref[...] = jnp.zeros_like(dws_ref)
            db2_ref[...] = jnp.zeros_like(db2_ref)
            dvg_ref[...] = jnp.zeros_like(dvg_ref)

        bdv = bd_ref[...]
        v = v_ref[...]
        ms = _split_dot(v * v, bdv) * (1.0 / HEAD_DIM)
        rv = lax.rsqrt(ms + EPS)
        xhat = v * rv
        vgv = vg_ref[...]
        vn = xhat * vgv
        wcat = _gmlp_masked_weights(ws_ref, False)
        wcat_t = _gmlp_masked_weights(ws_ref, True)
        head = _head_index((CHUNK, GMLP_WIDTH))
        dvg = jnp.zeros((1, GMLP_WIDTH), F32)
        for c in range(tm // CHUNK):
            rows = slice(c * CHUNK, (c + 1) * CHUNK)
            vn_c = vn[rows]
            spb = jnp.dot(wcat, _head_stack(vn_c, head), preferred_element_type=F32) + b2_ref[...]
            silu, dsilu = _silu_parts(g_ref[rows, :])
            dy = dy_ref[rows, :]
            u = u_ref[rows, :]
            du_ref[rows, :] = (dy * spb * silu).astype(BF16)
            dg_ref[rows, :] = (dy * u * spb * dsilu).astype(BF16)
            dsp = dy * u * silu
            db2_ref[...] += dsp
            dstack = _head_stack(dsp, head)
            dvn = jnp.dot(wcat_t, dstack, preferred_element_type=F32)
            dws_ref[...] += _nt(dstack, vn_c.astype(BF16))
            xh = xhat[rows]
            a = dvn * vgv
            mean_ax = _split_dot(a * xh, bdv) * (1.0 / HEAD_DIM)
            dv_ref[rows, :] = (rv[rows] * (a - xh * mean_ax)).astype(BF16)
            dvg = dvg + jnp.sum(dvn * xh, axis=0, keepdims=True)
        dvg_ref[...] += dvg

        @pl.when(i == nsteps - 1)
        def _():
            t = lax.broadcasted_iota(jnp.int32, (4 * CHUNK, CHUNK), 0) % CHUNK
            s = lax.broadcasted_iota(jnp.int32, (4 * CHUNK, CHUNK), 1)
            dws_ref[...] = jnp.where(s <= t, dws_ref[...], 0.0)
            db2_ref[...] = _split_dot(db2_ref[...], bdv)

    col = lambda j: pl.BlockSpec((tm, GMLP_WIDTH), lambda i, j=j: (i, j))
    const = lambda shape: pl.BlockSpec(shape, lambda i: (0,) * len(shape))
    tile = pl.BlockSpec((tm, GMLP_WIDTH), lambda i: (i, 0))
    piece = pltpu.HBM((S, GMLP_WIDTH), BF16)
    return pl.pallas_call(
        body, name="gmlp_bwd", grid=(nsteps,),
        in_specs=[col(0), col(1), col(2), col(0), const((1, GMLP_WIDTH)), const((4, CHUNK, CHUNK)),
                  const((CHUNK, GMLP_WIDTH)), const((GMLP_WIDTH, GMLP_WIDTH))],
        out_specs=[tile, tile, tile, const((4 * CHUNK, CHUNK)), const((CHUNK, GMLP_WIDTH)), const((1, GMLP_WIDTH))],
        out_shape=[piece, piece, piece, pltpu.HBM((4 * CHUNK, CHUNK), F32),
                   pltpu.HBM((CHUNK, GMLP_WIDTH), F32), pltpu.HBM((1, GMLP_WIDTH), F32)],
        compiler_params=_params(32, ("arbitrary",)),
    )(*_hbm(proj, proj, proj, dycat, vg, w_s, b2, bd))


def _band_mask():
    qi = lax.broadcasted_iota(jnp.int32, (2 * BLOCK, 2 * BLOCK), 0) % BLOCK
    ki = lax.broadcasted_iota(jnp.int32, (2 * BLOCK, 2 * BLOCK), 1)
    return ((ki < BLOCK) & (ki >= qi)) | ((ki >= BLOCK) & ((ki - BLOCK) <= qi))


def _first_block_bias(blk, blocks_per_class):
    kcol = lax.broadcasted_iota(jnp.int32, (1, 2 * BLOCK), 1)
    kill = jnp.where((blk & (blocks_per_class - 1)) == 0, NEG, 0.0)
    return jnp.where(kcol < BLOCK, kill, 0.0)


def _two_heads(q, lo):
    zero = jnp.zeros_like(q)
    return jnp.concatenate([jnp.where(lo, q, zero), jnp.where(lo, zero, q)], axis=0)


def _block_tokens(blk, d, S):
    if d == 1:
        return pl.ds(pl.multiple_of(blk * BLOCK, BLOCK), BLOCK)
    blocks_per_class = S // d // BLOCK
    r = lax.shift_right_logical(blk, blocks_per_class.bit_length() - 1)
    n = blk & (blocks_per_class - 1)
    return pl.ds(r + n * (BLOCK * d), BLOCK, stride=d)


def _padded_block(blk):
    return pl.ds(pl.multiple_of((blk + 1) * BLOCK, BLOCK), BLOCK)


def _for_blocks(n_blocks, unroll, fn):
    def group(g, carry):
        for u in range(unroll):
            fn(g * unroll + u)
        return carry
    lax.fori_loop(0, n_blocks // unroll, group, 0)


def _attn_fwd(proj, qg2, kg2, bd):
    S = proj.shape[0]
    npairs = ATTN_WIDTH // 128
    tn = 512

    def body(q_ref, k_ref, v_ref, g_ref, qg_ref, kg_ref, bd_ref, y_ref, att_ref, lse_ref, qn, kn, kc, vc):
        bdv = bd_ref[...]
        lo = lax.broadcasted_iota(jnp.int32, (BLOCK, 128), 1) < HEAD_DIM
        band_mask = _band_mask()
        kc[pl.ds(0, BLOCK), :] = jnp.zeros((BLOCK, 128), BF16)
        vc[pl.ds(0, BLOCK), :] = jnp.zeros((BLOCK, 128), BF16)

        def norm_step(i, carry):
            rows = pl.ds(pl.multiple_of(i * tn, tn), tn)
            qv = q_ref[rows, :]
            kv = k_ref[rows, :]
            qn[rows, :] = (qv * lax.rsqrt(_split_dot(qv * qv, bdv) * (1.0 / HEAD_DIM) + EPS)) * (qg_ref[...] * SCALE)
            kn[rows, :] = (kv * lax.rsqrt(_split_dot(kv * kv, bdv) * (1.0 / HEAD_DIM) + EPS)) * kg_ref[...]
            return carry
        lax.fori_loop(0, S // tn, norm_step, 0)

        def fill(blk, d):
            tokens = _block_tokens(blk, d, S)
            kc[_padded_block(blk), :] = kn[tokens, :].astype(BF16)
            vc[_padded_block(blk), :] = v_ref[tokens, :].astype(BF16)

        def block(blk, d):
            tokens = _block_tokens(blk, d, S)
            keys = pl.ds(pl.multiple_of(blk * BLOCK, BLOCK), 2 * BLOCK)
            q2 = _two_heads(qn[tokens, :].astype(BF16), lo)
            s = jnp.where(band_mask, _nt(q2, kc[keys, :]), NEG) + _first_block_bias(blk, S // d // BLOCK)
            m = jnp.max(s, axis=-1, keepdims=True)
            e = jnp.exp(s - m)
            l = jnp.sum(e, axis=-1, keepdims=True)
            o2 = jnp.dot(e.astype(BF16), vc[keys, :], preferred_element_type=F32) * (1.0 / l)
            lse2 = m + jnp.log(l)
            o = jnp.where(lo, o2[:BLOCK], o2[BLOCK:])
            lse = jnp.where(lo, lse2[:BLOCK], lse2[BLOCK:])
            if d > 1:
                la = lse_ref[tokens, :]
                mx = jnp.maximum(la, lse)
                wa, wb = jnp.exp(la - mx), jnp.exp(lse - mx)
                t = wa + wb
                o = (wa * att_ref[tokens, :] + wb * o) / t
                lse = mx + jnp.log(t)
            att_ref[tokens, :] = o
            lse_ref[tokens, :] = lse

        for d in DILATIONS:
            _for_blocks(S // BLOCK, 4, functools.partial(fill, d=d))
            _for_blocks(S // BLOCK, 8, functools.partial(block, d=d))

        def gate_step(i, carry):
            rows = pl.ds(pl.multiple_of(i * tn, tn), tn)
            silu, _ = _silu_parts(g_ref[rows, :])
            y_ref[rows, :] = (att_ref[rows, :] * silu).astype(BF16)
            return carry
        lax.fori_loop(0, S // tn, gate_step, 0)

    col = lambda j0: pl.BlockSpec((S, 128), lambda p, j0=j0: (0, j0 + p))
    const = lambda shape: pl.BlockSpec(shape, lambda p: (0,) * len(shape))
    out = pl.BlockSpec((S, 128), lambda p: (0, p))
    return pl.pallas_call(
        body, name="attn_fwd", grid=(npairs,),
        in_specs=[col(COL_AQ), col(COL_AK), col(COL_AV), col(COL_AG), const((1, 128)), const((1, 128)),
                  const((128, 128))],
        out_specs=[out, out, out],
        out_shape=[pltpu.HBM((S, ATTN_WIDTH), BF16), pltpu.HBM((S, ATTN_WIDTH), F32),
                   pltpu.HBM((S, ATTN_WIDTH), F32)],
        scratch_shapes=[pltpu.VMEM((S, 128), F32), pltpu.VMEM((S, 128), F32),
                        pltpu.VMEM((S + BLOCK, 128), BF16), pltpu.VMEM((S + BLOCK, 128), BF16)],
        compiler_params=_params(48, ("arbitrary",)),
    )(*_hbm(proj, proj, proj, proj, qg2, kg2, bd))


def _attn_bwd(proj, dycat, att, lse, qg2, kg2, bd):
    S = proj.shape[0]
    npairs = ATTN_WIDTH // 128
    tn = 512

    def body(q_ref, k_ref, v_ref, g_ref, dy_ref, att_ref, lse_ref, qg_ref, kg_ref, bd_ref,
             dq_ref, dk_ref, dv_ref, dg_ref, dqg_ref, dkg_ref,
             qn, kn, rq_s, rk_s, kc, vc, do_s, dd_s, dqa, dka, dva):
        bdv = bd_ref[...]
        lo = lax.broadcasted_iota(jnp.int32, (BLOCK, 128), 1) < HEAD_DIM
        kc[pl.ds(0, BLOCK), :] = jnp.zeros((BLOCK, 128), BF16)
        vc[pl.ds(0, BLOCK), :] = jnp.zeros((BLOCK, 128), BF16)

        def prepare(i, carry):
            rows = pl.ds(pl.multiple_of(i * tn, tn), tn)
            qv = q_ref[rows, :]
            kv = k_ref[rows, :]
            rq = lax.rsqrt(_split_dot(qv * qv, bdv) * (1.0 / HEAD_DIM) + EPS)
            rk = lax.rsqrt(_split_dot(kv * kv, bdv) * (1.0 / HEAD_DIM) + EPS)
            rq_s[rows, :] = rq
            rk_s[rows, :] = rk
            qn[rows, :] = (qv * rq) * (qg_ref[...] * SCALE)
            kn[rows, :] = (kv * rk) * kg_ref[...]
            silu, dsilu = _silu_parts(g_ref[rows, :])
            dy = dy_ref[rows, :]
            at = att_ref[rows, :]
            do = dy * silu
            do_s[rows, :] = do
            dd_s[rows, :] = _split_dot(do * at, bdv)
            dg_ref[rows, :] = (dy * at * dsilu).astype(BF16)
            dka[rows, :] = jnp.zeros((tn, 128), F32)
            dva[rows, :] = jnp.zeros((tn, 128), F32)
            return carry
        lax.fori_loop(0, S // tn, prepare, 0)

        kt = lax.broadcasted_iota(jnp.int32, (2 * BLOCK, 2 * BLOCK), 0)
        qt = lax.broadcasted_iota(jnp.int32, (2 * BLOCK, 2 * BLOCK), 1) % BLOCK
        band_mask_t = ((kt < BLOCK) & (kt >= qt)) | ((kt >= BLOCK) & ((kt - BLOCK) <= qt))

        def per_query_row(t):
            tt = t.T
            return jnp.concatenate([tt[0:1, :], tt[HEAD_DIM:HEAD_DIM + 1, :]], axis=1)

        def fill(blk, d):
            tokens = _block_tokens(blk, d, S)
            kc[_padded_block(blk), :] = kn[tokens, :].astype(BF16)
            vc[_padded_block(blk), :] = v_ref[tokens, :].astype(BF16)

        def block(blk, d):
            tokens = _block_tokens(blk, d, S)
            keys = pl.ds(pl.multiple_of(blk * BLOCK, BLOCK), 2 * BLOCK)
            first = (blk & (S // d // BLOCK - 1)) == 0
            q2 = _two_heads(qn[tokens, :].astype(BF16), lo)
            do2 = _two_heads(do_s[tokens, :].astype(BF16), lo)
            lse_row = per_query_row(lse_ref[tokens, :])
            dd_row = per_query_row(dd_s[tokens, :])
            kb = kc[keys, :]
            vb = vc[keys, :]
            st = jnp.where(band_mask_t, _nt(kb, q2), NEG)
            st = jnp.concatenate([st[:BLOCK] + jnp.where(first, NEG, 0.0), st[BLOCK:]], axis=0)
            pt = jnp.exp(st - lse_row)
            dst = pt * (_nt(vb, do2) - dd_row)
            ptb = pt.astype(BF16)
            dstb = dst.astype(BF16)
            dv_band = jnp.dot(ptb, do2, preferred_element_type=F32)
            dk_band = jnp.dot(dstb, q2, preferred_element_type=F32)
            before = _block_tokens(jnp.where(first, blk, blk - 1), d, S)
            dka[before, :] = dka[before, :] + dk_band[:BLOCK]
            dva[before, :] = dva[before, :] + dv_band[:BLOCK]
            dka[tokens, :] = dka[tokens, :] + dk_band[BLOCK:]
            dva[tokens, :] = dva[tokens, :] + dv_band[BLOCK:]
            dq2 = _tn(dstb, kb)
            dq = jnp.where(lo, dq2[:BLOCK], dq2[BLOCK:])
            dqa[tokens, :] = dq if d == 1 else dqa[tokens, :] + dq

        for d in DILATIONS:
            _for_blocks(S // BLOCK, 4, functools.partial(fill, d=d))
            _for_blocks(S // BLOCK, 8, functools.partial(block, d=d))

        def out_step(i, carry):
            dqg, dkg = carry
            rows = pl.ds(pl.multiple_of(i * tn, tn), tn)
            rq = rq_s[rows, :]
            rk = rk_s[rows, :]
            qh = q_ref[rows, :] * rq
            kh = k_ref[rows, :] * rk
            dqs = dqa[rows, :] * SCALE
            dkn = dka[rows, :]
            aq = dqs * qg_ref[...]
            ak = dkn * kg_ref[...]
            dq_ref[rows, :] = (rq * (aq - qh * (_split_dot(aq * qh, bdv) * (1.0 / HEAD_DIM)))).astype(BF16)
            dk_ref[rows, :] = (rk * (ak - kh * (_split_dot(ak * kh, bdv) * (1.0 / HEAD_DIM)))).astype(BF16)
            dv_ref[rows, :] = dva[rows, :].astype(BF16)
            dqg = dqg + jnp.sum(dqs * qh, axis=0, keepdims=True)
            dkg = dkg + jnp.sum(dkn * kh, axis=0, keepdims=True)
            return dqg, dkg
        zero = jnp.zeros((1, 128), F32)
        dqg, dkg = lax.fori_loop(0, S // tn, out_step, (zero, zero))
        dqg_ref[0] = dqg
        dkg_ref[0] = dkg

    col = lambda j0: pl.BlockSpec((S, 128), lambda p, j0=j0: (0, j0 + p))
    col1 = lambda j0: pl.BlockSpec((S, 128), lambda p, j0=j0: (0, j0 + p), pipeline_mode=pl.Buffered(1))
    const = lambda shape: pl.BlockSpec(shape, lambda p: (0,) * len(shape))
    out = pl.BlockSpec((S, 128), lambda p: (0, p))
    gain_out = pl.BlockSpec((1, 1, 128), lambda p: (p, 0, 0))
    piece = pltpu.HBM((S, ATTN_WIDTH), BF16)
    gains = pltpu.HBM((npairs, 1, 128), F32)
    f32buf = pltpu.VMEM((S, 128), F32)
    bf16pad = pltpu.VMEM((S + BLOCK, 128), BF16)
    return pl.pallas_call(
        body, name="attn_bwd", grid=(npairs,),
        in_specs=[col(COL_AQ), col(COL_AK), col(COL_AV), col1(COL_AG), col1(GMLP_WIDTH // 128), col1(0), col(0),
                  const((1, 128)), const((1, 128)), const((128, 128))],
        out_specs=[out, out, out, out, gain_out, gain_out],
        out_shape=[piece, piece, piece, piece, gains, gains],
        scratch_shapes=[f32buf, f32buf, f32buf, f32buf, bf16pad, bf16pad, f32buf, f32buf, f32buf, f32buf, f32buf],
        compiler_params=_params(60, ("arbitrary",)),
    )(*_hbm(proj, proj, proj, proj, dycat, att, lse, qg2, kg2, bd))


def _mem_kv(mem, gain, wkv_bf, kg4, bd):
    def body(mem_ref, g_ref, w_ref, kg_ref, bd_ref, hm_ref, kraw_ref, mk_ref, mv_ref):
        mv_ = mem_ref[...]
        r = lax.rsqrt(jnp.mean(mv_ * mv_, axis=-1, keepdims=True) + EPS)
        hm = ((mv_ * r) * g_ref[...]).astype(BF16)
        hm_ref[...] = hm
        kv = jnp.dot(hm, w_ref[...], preferred_element_type=F32)
        kraw = kv[:, :MEM_WIDTH]
        kraw_ref[...] = kraw
        ms = _split_dot(kraw * kraw, bd_ref[...]) * (1.0 / HEAD_DIM)
        mk_ref[...] = (kraw * lax.rsqrt(ms + EPS)) * kg_ref[...]
        mv_ref[...] = kv[:, MEM_WIDTH:]

    sq = jax.ShapeDtypeStruct((MEM_LEN, MEM_WIDTH), F32)
    return pl.pallas_call(
        body, name="mem_kv",
        out_shape=[jax.ShapeDtypeStruct((MEM_LEN, D_MODEL), BF16), sq, sq, sq],
        compiler_params=_params(16),
    )(mem, gain, wkv_bf, kg4, bd)


def _mem_fwd(proj, mk, mv, qg4, bd):
    S = proj.shape[0]
    tm = 512

    def body(q_ref, g_ref, mk_ref, mv_ref, qg_ref, bd_ref, y_ref, om_ref):
        qv = q_ref[...]
        ms = _split_dot(qv * qv, bd_ref[...]) * (1.0 / HEAD_DIM)
        qs = (qv * lax.rsqrt(ms + EPS)) * (qg_ref[...] * SCALE)
        mkb = mk_ref[...].astype(BF16)
        mvb = mv_ref[...].astype(BF16)
        head = _head_index((tm, MEM_WIDTH))
        o = jnp.zeros((tm, MEM_WIDTH), F32)
        for h in range(4):
            s = _nt(jnp.where(head == h, qs, 0.0).astype(BF16), mkb)
            e = jnp.exp(s - jnp.max(s, axis=-1, keepdims=True))
            p = e * (1.0 / jnp.sum(e, axis=-1, keepdims=True))
            o = jnp.where(head == h, jnp.dot(p.astype(BF16), mvb, preferred_element_type=F32), o)
        om_ref[...] = o
        silu, _ = _silu_parts(g_ref[...])
        y_ref[...] = (o * silu).astype(BF16)

    col = lambda j: pl.BlockSpec((tm, MEM_WIDTH), lambda i, j=j: (i, j))
    const = lambda shape: pl.BlockSpec(shape, lambda i: (0,) * len(shape))
    tile = pl.BlockSpec((tm, MEM_WIDTH), lambda i: (i, 0))
    return pl.pallas_call(
        body, name="mem_fwd", grid=(S // tm,),
        in_specs=[col(11), col(12), const((MEM_LEN, MEM_WIDTH)), const((MEM_LEN, MEM_WIDTH)), const((1, MEM_WIDTH)),
                  const((MEM_WIDTH, MEM_WIDTH))],
        out_specs=[tile, tile],
        out_shape=[pltpu.HBM((S, MEM_WIDTH), BF16), pltpu.HBM((S, MEM_WIDTH), F32)],
        compiler_params=_params(24, ("arbitrary",)),
    )(*_hbm(proj, proj, mk, mv, qg4, bd))


def _mem_bwd(proj, dycat, om, mk, mv, qg4, bd):
    S = proj.shape[0]
    tm = 512

    def body(q_ref, g_ref, dy_ref, om_ref, mk_ref, mv_ref, qg_ref, bd_ref,
             dq_ref, dg_ref, dmk_ref, dmv_ref, dqg_ref):
        i = pl.program_id(0)

        @pl.when(i == 0)
        def _():
            dmk_ref[...] = jnp.zeros_like(dmk_ref)
            dmv_ref[...] = jnp.zeros_like(dmv_ref)
            dqg_ref[...] = jnp.zeros_like(dqg_ref)

        bdv = bd_ref[...]
        qv = q_ref[...]
        rq = lax.rsqrt(_split_dot(qv * qv, bdv) * (1.0 / HEAD_DIM) + EPS)
        qh = qv * rq
        qs = qh * (qg_ref[...] * SCALE)
        silu, dsilu = _silu_parts(g_ref[...])
        dy = dy_ref[...]
        o = om_ref[...]
        do = dy * silu
        dg_ref[...] = (dy * o * dsilu).astype(BF16)
        dd = _split_dot(do * o, bdv)
        mkb = mk_ref[...].astype(BF16)
        mvb = mv_ref[...].astype(BF16)
        head = _head_index((tm, MEM_WIDTH))
        dqs = jnp.zeros((tm, MEM_WIDTH), F32)
        for h in range(4):
            qhd = jnp.where(head == h, qs, 0.0).astype(BF16)
            doh = jnp.where(head == h, do, 0.0).astype(BF16)
            s = _nt(qhd, mkb)
            e = jnp.exp(s - jnp.max(s, axis=-1, keepdims=True))
            p = e * (1.0 / jnp.sum(e, axis=-1, keepdims=True))
            ds = p * (_nt(doh, mvb) - dd[:, h * HEAD_DIM:h * HEAD_DIM + 1])
            dsb = ds.astype(BF16)
            dmv_ref[...] += _tn(p.astype(BF16), doh)
            dmk_ref[...] += _tn(dsb, qhd)
            dqs = jnp.where(head == h, jnp.dot(dsb, mkb, preferred_element_type=F32), dqs)
        dqs = dqs * SCALE
        a = dqs * qg_ref[...]
        dq_ref[...] = (rq * (a - qh * (_split_dot(a * qh, bdv) * (1.0 / HEAD_DIM)))).astype(BF16)
        dqg_ref[...] += jnp.sum(dqs * qh, axis=0, keepdims=True)

    col = lambda j: pl.BlockSpec((tm, MEM_WIDTH), lambda i, j=j: (i, j))
    const = lambda shape: pl.BlockSpec(shape, lambda i: (0,) * len(shape))
    tile = pl.BlockSpec((tm, MEM_WIDTH), lambda i: (i, 0))
    piece = pltpu.HBM((S, MEM_WIDTH), BF16)
    sq = pltpu.HBM((MEM_LEN, MEM_WIDTH), F32)
    return pl.pallas_call(
        body, name="mem_bwd", grid=(S // tm,),
        in_specs=[col(11), col(12), col(3), tile, const((MEM_LEN, MEM_WIDTH)), const((MEM_LEN, MEM_WIDTH)),
                  const((1, MEM_WIDTH)), const((MEM_WIDTH, MEM_WIDTH))],
        out_specs=[tile, tile, const((MEM_LEN, MEM_WIDTH)), const((MEM_LEN, MEM_WIDTH)), const((1, MEM_WIDTH))],
        out_shape=[piece, piece, sq, sq, pltpu.HBM((1, MEM_WIDTH), F32)],
        compiler_params=_params(32, ("arbitrary",)),
    )(*_hbm(proj, proj, dycat, om, mk, mv, qg4, bd))


def _mem_kv_bwd(dmk, dmv, kraw, mem, gain, kg4, wkv_bf, hm_bf, bd):
    def body(dmk_ref, dmv_ref, kraw_ref, mem_ref, g_ref, kg_ref, w_ref, hm_ref, bd_ref, dw_ref, dg_ref, dkg_ref):
        bdv = bd_ref[...]
        kraw = kraw_ref[...]
        rk = lax.rsqrt(_split_dot(kraw * kraw, bdv) * (1.0 / HEAD_DIM) + EPS)
        kh = kraw * rk
        dmkv = dmk_ref[...]
        a = dmkv * kg_ref[...]
        dkraw = rk * (a - kh * (_split_dot(a * kh, bdv) * (1.0 / HEAD_DIM)))
        dkg_ref[...] = jnp.sum(dmkv * kh, axis=0, keepdims=True)
        dkv = jnp.concatenate([dkraw, dmv_ref[...]], axis=1).astype(BF16)
        dw = _tn(hm_ref[...], dkv).astype(BF16)
        rows_blk = D_MODEL // N_DEV
        for j in range(N_DEV):
            dw_ref[j] = dw[rows_blk * j:rows_blk * (j + 1)]
        dhm = _nt(dkv, w_ref[...])
        mv_ = mem_ref[...]
        r = lax.rsqrt(jnp.mean(mv_ * mv_, axis=-1, keepdims=True) + EPS)
        dg_ref[...] = jnp.sum(dhm * (mv_ * r), axis=0, keepdims=True)

    return pl.pallas_call(
        body, name="mem_kv_bwd",
        out_shape=[jax.ShapeDtypeStruct((N_DEV, D_MODEL // N_DEV, 2 * MEM_WIDTH), BF16),
                   jax.ShapeDtypeStruct((1, D_MODEL), F32), jax.ShapeDtypeStruct((1, MEM_WIDTH), F32)],
        compiler_params=_params(24),
    )(dmk, dmv, kraw, mem, gain, kg4, wkv_bf, hm_bf, bd)


def _out_loss(yg, ya, ym, x, tgt, wout_bf):
    S, D = x.shape
    tm = 256

    nsteps = S // tm
    rows_blk = D // N_DEV

    def body(yg_ref, ya_ref, ym_ref, x_ref, t_ref, w_ref, dout_ref, dycat_ref, dw_ref, loss_ref, acc_ref):
        i = pl.program_id(0)

        @pl.when(i == 0)
        def _():
            acc_ref[...] = jnp.zeros_like(acc_ref)
            loss_ref[...] = jnp.zeros_like(loss_ref)

        ycat = jnp.concatenate([yg_ref[...], ya_ref[...], ym_ref[...]], axis=1)
        w = w_ref[...]
        diff = (x_ref[...] + jnp.dot(ycat, w, preferred_element_type=F32)) - t_ref[...]
        loss_ref[...] += jnp.sum(diff * diff, axis=0, keepdims=True)
        dout = diff * (1.0 / D)
        dout_ref[...] = dout
        db = dout.astype(BF16)
        dycat_ref[...] = _nt(db, w)
        acc_ref[...] += _tn(ycat, db)

        @pl.when(i == nsteps - 1)
        def _():
            for j in range(N_DEV):
                dw_ref[j] = acc_ref[rows_blk * j:rows_blk * (j + 1), :].astype(BF16)

    tile = lambda w: pl.BlockSpec((tm, w), lambda i: (i, 0))
    const = lambda shape: pl.BlockSpec(shape, lambda i: (0,) * len(shape))
    return pl.pallas_call(
        body, name="out_loss", grid=(nsteps,),
        in_specs=[tile(GMLP_WIDTH), tile(ATTN_WIDTH), tile(MEM_WIDTH), tile(D), tile(D), const((D, D))],
        out_specs=[tile(D), tile(D), const((N_DEV, rows_blk, D)), const((1, D))],
        out_shape=[pltpu.HBM((S, D), F32), pltpu.HBM((S, D), F32),
                   pltpu.HBM((N_DEV, rows_blk, D), BF16), pltpu.HBM((1, D), F32)],
        scratch_shapes=[pltpu.VMEM((D, D), F32)],
        compiler_params=_params(40, ("arbitrary",)),
    )(*_hbm(yg, ya, ym, x, tgt, wout_bf))


def _piece_specs(pieces, tm):
    return [pl.BlockSpec((tm, p.shape[1]), lambda i: (i, 0)) for p in pieces]


def _in_bwd_dx(pieces, x, dout, gain, w_t, dw_blocks):
    S, D = x.shape
    N = w_t.shape[0]
    tm = 256
    n = len(pieces)
    nsteps = S // tm
    middle_step = nsteps // 4

    def body(*refs):
        piece_refs = refs[:n]
        x_ref, dout_ref, g_ref, w_ref, dwb_ref, gx_ref, dg_ref, gw_ref = refs[n:n + 8]
        rs = _ReduceScatter([dwb_ref], [gw_ref], *refs[n + 8:])
        i = pl.program_id(0)

        @pl.when(i == 0)
        def _():
            dg_ref[...] = jnp.zeros_like(dg_ref)
            rs.start()

        @pl.when(i == middle_step)
        def _():
            rs.middle()

        dproj = jnp.concatenate([r[...] for r in piece_refs], axis=1)
        dh = jnp.dot(dproj, w_ref[...], preferred_element_type=F32)
        xv = x_ref[...]
        r = lax.rsqrt(jnp.mean(xv * xv, axis=-1, keepdims=True) + EPS)
        xh = xv * r
        a = dh * g_ref[...]
        gx_ref[...] = dout_ref[...] + r * (a - xh * jnp.mean(a * xh, axis=-1, keepdims=True))
        dg_ref[...] += jnp.sum(dh * xh, axis=0, keepdims=True)

        @pl.when(i == nsteps - 1)
        def _():
            rs.finish()

    tile = pl.BlockSpec((tm, D), lambda i: (i, 0))
    const = lambda shape: pl.BlockSpec(shape, lambda i: (0,) * len(shape))
    vmem = pl.BlockSpec(memory_space=pltpu.VMEM)
    return pl.pallas_call(
        body, name="in_bwd_dx", grid=(nsteps,),
        in_specs=_piece_specs(pieces, tm) + [tile, tile, const((1, D)), const((N, D)), vmem],
        out_specs=[tile, const((1, D)), vmem],
        out_shape=[pltpu.HBM((S, D), F32), pltpu.HBM((1, D), F32), jax.ShapeDtypeStruct(dw_blocks.shape[1:], F32)],
        scratch_shapes=_reduce_scatter_scratch([dw_blocks]),
        compiler_params=_params(56, ("arbitrary",)),
    )(*_hbm(*pieces, x, dout, gain, w_t), dw_blocks)


def _in_bwd_dw(pieces, h_bf, others):
    S, D = h_bf.shape
    N = sum(p.shape[1] for p in pieces)
    n_blk = N // N_DEV
    tm = 512
    n = len(pieces)
    k = len(others)
    nsteps = S // tm

    def body(*refs):
        piece_refs = refs[:n]
        h_ref = refs[n]
        other_refs = refs[n + 1:n + 1 + k]
        dw_ref = refs[n + 1 + k]
        sum_refs = refs[n + 2 + k:n + 2 + 2 * k]
        acc_ref = refs[n + 2 + 2 * k]
        rs = _ReduceScatter(other_refs, sum_refs, *refs[n + 3 + 2 * k:])
        i = pl.program_id(0)

        @pl.when(i == 0)
        def _():
            acc_ref[...] = jnp.zeros_like(acc_ref)
            rs.start()

        @pl.when(i == 1)
        def _():
            rs.middle()

        dproj = jnp.concatenate([r[...] for r in piece_refs], axis=1)
        acc_ref[...] += _tn(h_ref[...], dproj)

        @pl.when(i == nsteps - 1)
        def _():
            for j in range(N_DEV):
                dw_ref[j] = acc_ref[:, n_blk * j:n_blk * (j + 1)].T.astype(BF16)
            rs.finish()

    vmem = pl.BlockSpec(memory_space=pltpu.VMEM)
    return pl.pallas_call(
        body, name="in_bwd_dw", grid=(nsteps,),
        in_specs=_piece_specs(pieces, tm) + [pl.BlockSpec((tm, D), lambda i: (i, 0))] + [vmem] * k,
        out_specs=[pl.BlockSpec((N_DEV, n_blk, D), lambda i: (0, 0, 0))] + [vmem] * k,
        out_shape=[pltpu.HBM((N_DEV, n_blk, D), BF16)] + [jax.ShapeDtypeStruct(o.shape[1:], F32) for o in others],
        scratch_shapes=[pltpu.VMEM((D, N), F32)] + _reduce_scatter_scratch(others),
        compiler_params=_params(56, ("arbitrary",)),
    )(*_hbm(*pieces, h_bf), *others)


def _row_step(m):
    return max(t for t in range(16, 257, 16) if m % t == 0)


def _place():
    x, y, c = lax.axis_index("x"), lax.axis_index("y"), lax.axis_index("c")
    chips = [(1 - x, y), (x, 1 - y), (1 - x, 1 - y)]
    return x, y, c, chips


class _AllGather:
    def __init__(self, srcs, outs, send_sems, recv_sems, local_sems):
        self.srcs, self.outs, self.n = srcs, outs, len(srcs)
        self.send_sems, self.recv_sems, self.local_sems = send_sems, recv_sems, local_sems

    def _rows(self, a, px, py, pc):
        m = self.srcs[a].shape[0]
        return self.outs[a].at[pl.ds((4 * px + 2 * py + pc) * m, m), :]

    def _copy(self, a, k, block, to, src=None):
        return pltpu.make_async_remote_copy(
            src_ref=self._rows(a, *block) if src is None else src, dst_ref=self._rows(a, *block),
            send_sem=self.send_sems.at[a, k], recv_sem=self.recv_sems.at[a, k], device_id=to, device_id_type=MESH)

    def _mine(self):
        x, y, c, _ = _place()
        return [pltpu.make_async_copy(self.srcs[a], self._rows(a, x, y, c), self.local_sems.at[a])
                for a in range(self.n)]

    def _first(self, far):
        x, y, c, chips = _place()
        out = []
        for a in range(self.n):
            if far:
                out.append(self._copy(a, 3, (x, y, c), (*chips[2], c), src=self.srcs[a]))
            else:
                out.append(self._copy(a, 0, (x, y, c), (x, y, 1 - c), src=self.srcs[a]))
                out += [self._copy(a, 1 + j, (x, y, c), (*chips[j], c), src=self.srcs[a]) for j in (1, 0)]
        return out

    def _passed(self, j):
        x, y, c, chips = _place()
        return [self._copy(a, 4 + j, (*chips[j], c), (x, y, 1 - c)) for a in range(self.n)]

    def start(self):
        for cp in self._mine() + self._first(far=False):
            cp.start()

    def start_far(self):
        for cp in self._first(far=True):
            cp.start()

    def from_chip(self, j):
        x, y, c, chips = _place()
        for a in range(self.n):
            self._copy(a, 1 + j, (*chips[j], c), (x, y, c)).wait_recv()
        for cp in self._passed(j):
            cp.start()

    def from_sibling(self, j=None):
        x, y, c, chips = _place()
        for a in range(self.n):
            block = (x, y, 1 - c) if j is None else (*chips[j], 1 - c)
            self._copy(a, 0 if j is None else 4 + j, block, (x, y, c)).wait_recv()

    def from_self(self):
        for cp in self._mine():
            cp.wait()

    def finish(self):
        for cp in (self._first(far=False) + self._first(far=True)
                   + self._passed(0) + self._passed(1) + self._passed(2)):
            cp.wait_send()

    def run(self):
        self.start()
        self.start_far()
        self.from_self()
        for j in range(3):
            self.from_chip(j)
        self.from_sibling()
        for j in range(3):
            self.from_sibling(j)
        self.finish()


def _gather_proj(x, gain, shards, xpos):
    S, D = x.shape
    n = len(shards)
    N = N_DEV * shards[0].shape[0]
    half = N // 2
    tm = 256
    nsteps = S // tm

    def body(*refs):
        xpos_ref, x_ref, g_ref = refs[:3]
        ins = refs[3:3 + n]
        proj_ref, h_ref = refs[3 + n:5 + n]
        outs = refs[5 + n:5 + 2 * n]
        casts = refs[5 + 2 * n:5 + 3 * n]
        whole = refs[5 + 3 * n:5 + 4 * n]
        ag = _AllGather(casts, whole, *refs[5 + 4 * n:8 + 4 * n])
        out_sems = refs[8 + 4 * n]
        hh, i = pl.program_id(0), pl.program_id(1)

        @pl.when((hh == 0) & (i == 0))
        def _():
            for a in range(n):
                tr = _row_step(ins[a].shape[0])

                def cast(r, carry, a=a, tr=tr):
                    rows = pl.ds(pl.multiple_of(r * tr, tr), tr)
                    casts[a][rows, :] = ins[a][rows, :].astype(BF16)
                    return carry
                lax.fori_loop(0, ins[a].shape[0] // tr, cast, 0)
            ag.start()
            ag.from_self()
            ag.from_chip(1)
            ag.from_sibling()
            ag.from_sibling(1)
            ag.start_far()

        @pl.when((hh == 1) & (i == 0))
        def _():
            for j in (0, 2):
                ag.from_chip(j)
            for j in (0, 2):
                ag.from_sibling(j)

        xv = x_ref[...]
        r = lax.rsqrt(jnp.mean(xv * xv, axis=-1, keepdims=True) + EPS)
        h = ((xv * r) * g_ref[...]).astype(BF16)

        @pl.when(hh == 0)
        def _():
            h_ref[...] = h

        which = (xpos_ref[0] + hh) % 2
        w_half = whole[0][pl.ds(pl.multiple_of(which * half, half), half), :]
        proj_ref[...] = _nt(h, w_half)

        @pl.when((hh == 1) & (i == nsteps - 1))
        def _():
            ag.finish()
            to_results = [pltpu.make_async_copy(whole[a], outs[a], out_sems.at[a]) for a in range(n)]
            for cp in to_results:
                cp.start()
            for cp in to_results:
                cp.wait()

    vmem = pl.BlockSpec(memory_space=pltpu.VMEM)
    hbm = pl.BlockSpec(memory_space=pl.ANY)
    gathered = [(N_DEV * a.shape[0], a.shape[1]) for a in shards]
    grid_spec = pltpu.PrefetchScalarGridSpec(
        num_scalar_prefetch=1, grid=(2, nsteps),
        in_specs=[pl.BlockSpec((tm, D), lambda hh, i, xp: (i, 0)), pl.BlockSpec((1, D), lambda hh, i, xp: (0, 0))]
        + [vmem] * n,
        out_specs=[pl.BlockSpec((tm, half), lambda hh, i, xp: (i, (xp[0] + hh) % 2)),
                   pl.BlockSpec((tm, D), lambda hh, i, xp: (i * (1 - hh) + (nsteps - 1) * hh, 0))] + [hbm] * n,
        scratch_shapes=[pltpu.VMEM(a.shape, BF16) for a in shards] + [pltpu.VMEM(g, BF16) for g in gathered]
        + [pltpu.SemaphoreType.DMA((n, 7)), pltpu.SemaphoreType.DMA((n, 7)), pltpu.SemaphoreType.DMA((n,)),
           pltpu.SemaphoreType.DMA((n,))])
    return pl.pallas_call(
        body, name="gather_proj", grid_spec=grid_spec,
        out_shape=[pltpu.HBM((S, N), F32), pltpu.HBM((S, D), BF16)] + [pltpu.HBM(g, BF16) for g in gathered],
        compiler_params=_params(48, ("arbitrary", "arbitrary")),
    )(xpos, *_hbm(x, gain), *shards)


ROW_NORM, ROW_MEM_NORM, ROW_V_GAIN, ROW_B, ROW_ATTN_GAINS, ROW_MEM_GAINS, ROW_W_S, ROW_LOSS = 0, 8, 16, 18, 22, 23, 24, 536
SMALL_ROWS = 544


def _gather_small(dgain, dmgain, dvg, db2, dqg, dkg, dmqg, dmkg, dws, sq):
    def body(dgain_ref, dmgain_ref, dvg_ref, db2_ref, dqg_ref, dkg_ref, dmqg_ref, dmkg_ref, dws_ref, sq_ref,
             out_ref, mine, send_sems, recv_sems, local_sems):
        first = lax.broadcasted_iota(jnp.int32, (1, 128), 1) < HEAD_DIM
        for i in range(8):
            cols = slice(128 * i, 128 * (i + 1))
            mine[ROW_NORM + i:ROW_NORM + i + 1, :] = dgain_ref[:, cols]
            mine[ROW_MEM_NORM + i:ROW_MEM_NORM + i + 1, :] = dmgain_ref[:, cols]
            mine[ROW_LOSS + i:ROW_LOSS + i + 1, :] = sq_ref[:, cols]
        mine[ROW_V_GAIN:ROW_V_GAIN + 1, :] = dvg_ref[:, 0:128]
        mine[ROW_V_GAIN + 1:ROW_V_GAIN + 2, :] = dvg_ref[:, 128:256]
        bt = db2_ref[...].T
        for h in range(4):
            mine[ROW_B + h:ROW_B + h + 1, :] = bt[HEAD_DIM * h:HEAD_DIM * h + 1, :]

        def fold_heads(t):
            return t + pltpu.roll(t, HEAD_DIM, axis=1)
        aq = fold_heads(dqg_ref[0] + dqg_ref[1] + dqg_ref[2] + dqg_ref[3])
        ak = fold_heads(dkg_ref[0] + dkg_ref[1] + dkg_ref[2] + dkg_ref[3])
        mine[ROW_ATTN_GAINS:ROW_ATTN_GAINS + 1, :] = jnp.where(first, aq, ak)
        mq = fold_heads(dmqg_ref[:, 0:128] + dmqg_ref[:, 128:256])
        mk = fold_heads(dmkg_ref[:, 0:128] + dmkg_ref[:, 128:256])
        mine[ROW_MEM_GAINS:ROW_MEM_GAINS + 1, :] = jnp.where(first, mq, mk)
        mine[ROW_W_S:ROW_W_S + 4 * CHUNK, :] = dws_ref[...]
        _AllGather([mine], [out_ref], send_sems, recv_sems, local_sems).run()

    return pl.pallas_call(
        body, name="gather_small_grads",
        out_shape=jax.ShapeDtypeStruct((N_DEV * SMALL_ROWS, 128), F32),
        scratch_shapes=[pltpu.VMEM((SMALL_ROWS, 128), F32), pltpu.SemaphoreType.DMA((1, 7)),
                        pltpu.SemaphoreType.DMA((1, 7)), pltpu.SemaphoreType.DMA((1,))],
        compiler_params=_params(16),
    )(dgain, dmgain, dvg, db2, dqg, dkg, dmqg, dmkg, dws, sq)


def _reduce_scatter_scratch(arrs):
    n = len(arrs)
    return ([pltpu.VMEM((4,) + a.shape[1:], BF16) for a in arrs] + [pltpu.VMEM((3,) + a.shape[1:], BF16) for a in arrs]
            + [pltpu.SemaphoreType.DMA((n, 7)), pltpu.SemaphoreType.DMA((n, 7))])


class _ReduceScatter:
    def __init__(self, ins, outs, *scratch):
        n = len(ins)
        self.n, self.ins, self.outs = n, ins, outs
        self.half, self.quarter = scratch[:n], scratch[n:2 * n]
        self.send_sems, self.recv_sems = scratch[2 * n:]

    def _to_sibling(self):
        x, y, c, _ = _place()
        return [pltpu.make_async_remote_copy(
            src_ref=self.ins[a].at[2 * q + (1 - c)], dst_ref=self.half[a].at[q], send_sem=self.send_sems.at[a, q],
            recv_sem=self.recv_sems.at[a, q], device_id=(x, y, 1 - c), device_id_type=MESH)
            for a in range(self.n) for q in range(4)]

    def _to_chips(self):
        _, _, c, chips = _place()
        return [pltpu.make_async_remote_copy(
            src_ref=self.half[a].at[2 * chip[0] + chip[1]], dst_ref=self.quarter[a].at[k],
            send_sem=self.send_sems.at[a, 4 + k], recv_sem=self.recv_sems.at[a, 4 + k], device_id=(*chip, c),
            device_id_type=MESH) for a in range(self.n) for k, chip in enumerate(chips)]

    def _rows(self, a, fn):
        m = self.ins[a].shape[1]
        tr = _row_step(m)

        def step(i, carry):
            fn(pl.ds(pl.multiple_of(i * tr, tr), tr))
            return carry
        lax.fori_loop(0, m // tr, step, 0)

    def start(self):
        for cp in self._to_sibling():
            cp.start()

    def middle(self):
        _, _, c, _ = _place()
        for cp in self._to_sibling():
            cp.wait_recv()
        for a in range(self.n):
            for q in range(4):
                def add_half(rows, a=a, q=q):
                    both = self.ins[a][2 * q + c, rows, :].astype(F32) + self.half[a][q, rows, :].astype(F32)
                    self.half[a][q, rows, :] = both.astype(BF16)
                self._rows(a, add_half)
        for cp in self._to_chips():
            cp.start()

    def finish(self):
        x, y, _, _ = _place()
        for cp in self._to_chips():
            cp.wait_recv()
        for a in range(self.n):
            def add_quarters(rows, a=a):
                f = lambda t: t.astype(F32)
                self.outs[a][rows, :] = ((f(self.half[a][2 * x + y, rows, :]) + f(self.quarter[a][0, rows, :]))
                                         + (f(self.quarter[a][1, rows, :]) + f(self.quarter[a][2, rows, :])))
            self._rows(a, add_quarters)
        for cp in self._to_sibling() + self._to_chips():
            cp.wait_send()


def _adamw_math(w, g, m, v):
    m = ADAM_B1 * m + (1.0 - ADAM_B1) * g
    v = ADAM_B2 * v + (1.0 - ADAM_B2) * (g * g)
    m_hat = m / (1.0 - ADAM_B1 ** ADAM_STEP)
    v_hat = v / (1.0 - ADAM_B2 ** ADAM_STEP)
    delta = -ADAM_LR * (m_hat / (jnp.sqrt(v_hat) + ADAM_EPS) + ADAM_WD * w)
    return delta, m, v


def _adamw(w, g, m, v, name):
    R, C = w.shape
    tr = _row_step(R)

    def body(w_ref, g_ref, m_ref, v_ref, d_ref, nm_ref, nv_ref):
        d_ref[...], nm_ref[...], nv_ref[...] = _adamw_math(w_ref[...], g_ref[...], m_ref[...], v_ref[...])

    tile = pl.BlockSpec((tr, C), lambda i: (i, 0))
    out = pltpu.HBM((R, C), F32)
    return pl.pallas_call(
        body, name=name, grid=(R // tr,), in_specs=[tile] * 4, out_specs=[tile] * 3, out_shape=[out] * 3,
        compiler_params=_params(16, ("arbitrary",)),
    )(*_hbm(w, g, m, v))


SMALL = ("norm_gain", "gmlp_v_gain", "gmlp_w_s", "gmlp_b", "attn_q_gain", "attn_k_gain", "mem_norm_gain",
         "mem_q_gain", "mem_k_gain")
WEIGHTS = ("norm_gain", "w_in", "gmlp_v_gain", "gmlp_w_s", "gmlp_b", "attn_q_gain", "attn_k_gain",
           "mem_norm_gain", "w_mem_kv", "mem_q_gain", "mem_k_gain", "w_out")


def _adamw_small(w, m, v, g_all):
    k = len(SMALL)
    half = slice(0, HEAD_DIM), slice(HEAD_DIM, 2 * HEAD_DIM)

    def body(*refs):
        w_refs, m_refs, v_refs = refs[:k], refs[k:2 * k], refs[2 * k:3 * k]
        g_ref = refs[3 * k]
        outs = refs[3 * k + 1:7 * k + 1]
        loss_ref, gsum = refs[7 * k + 1:]

        part = SMALL_ROWS // 4
        for p in range(4):
            acc = g_ref[part * p:part * (p + 1), :]
            for dev in range(1, N_DEV):
                acc = acc + g_ref[dev * SMALL_ROWS + part * p:dev * SMALL_ROWS + part * (p + 1), :]
            gsum[part * p:part * (p + 1), :] = acc

        def update(name, at, g):
            i = SMALL.index(name)
            d, nm, nv = _adamw_math(w_refs[i][at], g, m_refs[i][at], v_refs[i][at])
            outs[i][at], outs[k + i][at], outs[2 * k + i][at], outs[3 * k + i][at] = g, d, nm, nv

        for i in range(8):
            at = (slice(0, 1), slice(128 * i, 128 * (i + 1)))
            update("norm_gain", at, gsum[ROW_NORM + i:ROW_NORM + i + 1, :])
            update("mem_norm_gain", at, gsum[ROW_MEM_NORM + i:ROW_MEM_NORM + i + 1, :])
        for h in range(4):
            row = (0, slice(h, h + 1), slice(None))
            update("gmlp_v_gain", row, gsum[ROW_V_GAIN + h // 2:ROW_V_GAIN + h // 2 + 1, half[h % 2]])
            update("gmlp_b", row, gsum[ROW_B + h:ROW_B + h + 1, :])
            update("gmlp_w_s", (0, h), gsum[ROW_W_S + CHUNK * h:ROW_W_S + CHUNK * (h + 1), :])
        whole = (slice(0, 1), slice(None))
        update("attn_q_gain", whole, gsum[ROW_ATTN_GAINS:ROW_ATTN_GAINS + 1, half[0]])
        update("attn_k_gain", whole, gsum[ROW_ATTN_GAINS:ROW_ATTN_GAINS + 1, half[1]])
        update("mem_q_gain", whole, gsum[ROW_MEM_GAINS:ROW_MEM_GAINS + 1, half[0]])
        update("mem_k_gain", whole, gsum[ROW_MEM_GAINS:ROW_MEM_GAINS + 1, half[1]])
        loss_ref[...] = jnp.sum(gsum[ROW_LOSS:ROW_LOSS + 8, :], keepdims=True) * (0.5 / D_MODEL)

    shapes = [jax.ShapeDtypeStruct(w[name].shape, F32) for name in SMALL]
    res = pl.pallas_call(
        body, name="adamw_small",
        out_shape=shapes * 4 + [jax.ShapeDtypeStruct((1, 1), F32)],
        scratch_shapes=[pltpu.VMEM((SMALL_ROWS, 128), F32)],
        compiler_params=_params(16),
    )(*[w[n] for n in SMALL], *[m[n] for n in SMALL], *[v[n] for n in SMALL], g_all)
    trees = [dict(zip(SMALL, res[j * k:(j + 1) * k])) for j in range(4)]
    return (*trees, res[4 * k])


def _grads(x, mem, tgt, w, shards):
    bd128, bd256 = _head_blockdiag(128), _head_blockdiag(256)
    gain = w["norm_gain"].reshape(1, D_MODEL)
    vg = w["gmlp_v_gain"].reshape(1, GMLP_WIDTH)
    w_s = w["gmlp_w_s"].reshape(4, CHUNK, CHUNK)
    b2 = jnp.repeat(w["gmlp_b"].reshape(4, CHUNK).T, HEAD_DIM, axis=1)
    qg2 = jnp.tile(w["attn_q_gain"].reshape(1, HEAD_DIM), (1, 2))
    kg2 = jnp.tile(w["attn_k_gain"].reshape(1, HEAD_DIM), (1, 2))
    mqg4 = jnp.tile(w["mem_q_gain"].reshape(1, HEAD_DIM), (1, 4))
    mkg4 = jnp.tile(w["mem_k_gain"].reshape(1, HEAD_DIM), (1, 4))
    mgain = w["mem_norm_gain"].reshape(1, D_MODEL)

    xpos = lax.axis_index("x").astype(jnp.int32).reshape(1)
    proj, h_bf, win_t, wkv_bf, wout_bf = _gather_proj(x, gain, shards, xpos)
    yg = _gmlp_fwd(proj, vg, w_s, b2, bd256)
    ya, att, lse = _attn_fwd(proj, qg2, kg2, bd128)
    hm_bf, kraw, mk, mv = _mem_kv(mem, mgain, wkv_bf, mkg4, bd256)
    ym, om = _mem_fwd(proj, mk, mv, mqg4, bd256)
    dout, dycat, dwout, sq = _out_loss(yg, ya, ym, x, tgt, wout_bf)

    du, dgv, dgg, dws, db2, dvg = _gmlp_bwd(proj, dycat, vg, w_s, b2, bd256)
    dq, dk, dv, dag, dqg, dkg = _attn_bwd(proj, dycat, att, lse, qg2, kg2, bd128)
    dmq, dmg, dmk, dmv, dmqg = _mem_bwd(proj, dycat, om, mk, mv, mqg4, bd256)
    dwkv, dmgain, dmkg = _mem_kv_bwd(dmk, dmv, kraw, mem, mgain, mkg4, wkv_bf, hm_bf, bd256)
    pieces = [du, dgv, dgg, dq, dk, dv, dag, dmq, dmg]
    dwin, g_wkv, g_wout = _in_bwd_dw(pieces, h_bf, [dwkv, dwout])
    grad_x, dgain, g_win = _in_bwd_dx(pieces, x, dout, gain, win_t, dwin)
    return grad_x, g_win, g_wkv, g_wout, (dgain, dmgain, dvg, db2, dqg, dkg, dmqg, dmkg, dws, sq)


def kernel(x, mem, norm_gain, w_in, gmlp_v_gain, gmlp_w_s, gmlp_b, attn_q_gain, attn_k_gain, mem_norm_gain, w_mem_kv, mem_q_gain, mem_k_gain, w_out, loss_target, m_norm_gain, m_w_in, m_gmlp_v_gain, m_gmlp_w_s, m_gmlp_b, m_attn_q_gain, m_attn_k_gain, m_mem_norm_gain, m_w_mem_kv, m_mem_q_gain, m_mem_k_gain, m_w_out, v_norm_gain, v_w_in, v_gmlp_v_gain, v_gmlp_w_s, v_gmlp_b, v_attn_q_gain, v_attn_k_gain, v_mem_norm_gain, v_w_mem_kv, v_mem_q_gain, v_mem_k_gain, v_w_out):
    w = dict(norm_gain=norm_gain, w_in=w_in, gmlp_v_gain=gmlp_v_gain, gmlp_w_s=gmlp_w_s, gmlp_b=gmlp_b,
             attn_q_gain=attn_q_gain, attn_k_gain=attn_k_gain, mem_norm_gain=mem_norm_gain, w_mem_kv=w_mem_kv,
             mem_q_gain=mem_q_gain, mem_k_gain=mem_k_gain, w_out=w_out)
    m = dict(norm_gain=m_norm_gain, w_in=m_w_in, gmlp_v_gain=m_gmlp_v_gain, gmlp_w_s=m_gmlp_w_s, gmlp_b=m_gmlp_b,
             attn_q_gain=m_attn_q_gain, attn_k_gain=m_attn_k_gain, mem_norm_gain=m_mem_norm_gain,
             w_mem_kv=m_w_mem_kv, mem_q_gain=m_mem_q_gain, mem_k_gain=m_mem_k_gain, w_out=m_w_out)
    v = dict(norm_gain=v_norm_gain, w_in=v_w_in, gmlp_v_gain=v_gmlp_v_gain, gmlp_w_s=v_gmlp_w_s, gmlp_b=v_gmlp_b,
             attn_q_gain=v_attn_q_gain, attn_k_gain=v_attn_k_gain, mem_norm_gain=v_mem_norm_gain,
             w_mem_kv=v_w_mem_kv, mem_q_gain=v_mem_q_gain, mem_k_gain=v_mem_k_gain, w_out=v_w_out)
    transposed = lambda t: jnp.transpose(t[0])

    grad_x, g_win, g_wkv, g_wout, small = _grads(
        x[0], mem[0], loss_target[0], w, [transposed(w_in), w_mem_kv[0], w_out[0]])
    small_all = _gather_small(*small)

    out_g, out_d, out_m, out_v, loss = _adamw_small(w, m, v, small_all)
    d_, m_, v_ = _adamw(transposed(w_in), g_win, transposed(m_w_in), transposed(v_w_in), "adamw_w_in")
    for tree, t in ((out_g, g_win), (out_d, d_), (out_m, m_), (out_v, v_)):
        tree["w_in"] = jnp.transpose(t)[None]
    for name, g in (("w_mem_kv", g_wkv), ("w_out", g_wout)):
        d_, m_, v_ = _adamw(w[name][0], g, m[name][0], v[name][0], "adamw_" + name)
        out_g[name], out_d[name], out_m[name], out_v[name] = g[None], d_[None], m_[None], v_[None]

    return (loss.reshape(()), grad_x[None], *[out_g[k] for k in WEIGHTS], *[out_d[k] for k in WEIGHTS],
            *[out_m[k] for k in WEIGHTS], *[out_v[k] for k in WEIGHTS])
```

```python
import functools
import math

import jax
import jax.numpy as jnp
from jax import lax
from jax.experimental import pallas as pl
from jax.experimental.pallas import tpu as pltpu

F32 = jnp.float32
BF16 = jnp.bfloat16

N_DEV = 8
D_MODEL = 1024
HEAD_DIM = 64
GMLP_WIDTH = 256
ATTN_WIDTH = 512
MEM_WIDTH = 256
MEM_LEN = 256
IN_WIDTH = 3 * GMLP_WIDTH + 4 * ATTN_WIDTH + 2 * MEM_WIDTH
CHUNK = 128
BLOCK = 128
DILATIONS = (1, 4, 16)
EPS = 1e-6
SCALE = 1.0 / math.sqrt(HEAD_DIM)
NEG = -1e30

ADAM_LR = 0.001
ADAM_B1 = 0.9
ADAM_B2 = 0.999
ADAM_EPS = 1e-08
ADAM_WD = 0.01
ADAM_STEP = 10

MIB = 1024 * 1024
MESH = pl.DeviceIdType.MESH

COL_AQ, COL_AK, COL_AV, COL_AG = 6, 10, 14, 18


def _params(vmem_mib, semantics=None):
    kw = dict(vmem_limit_bytes=vmem_mib * MIB)
    if semantics is not None:
        kw["dimension_semantics"] = semantics
    return pltpu.CompilerParams(**kw)


def _hbm(*arrs):
    return [pltpu.with_memory_space_constraint(a, pltpu.HBM) for a in arrs]


def _split_dot(x, sel_bf):
    hi = x.astype(BF16)
    lo = (x - hi.astype(F32)).astype(BF16)
    return jnp.dot(hi, sel_bf, preferred_element_type=F32) + jnp.dot(lo, sel_bf, preferred_element_type=F32)


def _nt(a, b):
    return lax.dot_general(a, b, (((1,), (1,)), ((), ())), preferred_element_type=F32)


def _tn(a, b):
    return lax.dot_general(a, b, (((0,), (0,)), ((), ())), preferred_element_type=F32)


def _silu_parts(g):
    sg = jax.nn.sigmoid(g)
    return g * sg, sg * (1.0 + g * (1.0 - sg))


def _head_index(shape):
    return lax.shift_right_logical(lax.broadcasted_iota(jnp.int32, shape, 1), HEAD_DIM.bit_length() - 1)


def _head_blockdiag(width):
    i = jnp.arange(width) // HEAD_DIM
    return (i[:, None] == i[None, :]).astype(BF16)


def _gmlp_masked_weights(ws_ref, transpose):
    t = lax.broadcasted_iota(jnp.int32, (CHUNK, CHUNK), 0)
    s = lax.broadcasted_iota(jnp.int32, (CHUNK, CHUNK), 1)
    parts = []
    for h in range(4):
        wm = jnp.where(s <= t, ws_ref[h], 0.0)
        parts.append(wm.T if transpose else wm)
    return jnp.concatenate(parts, axis=1).astype(BF16)


def _head_stack(v, head):
    return jnp.concatenate([jnp.where(head == h, v, 0.0) for h in range(4)], axis=0).astype(BF16)


def _gmlp_fwd(proj, vg, w_s, b2, bd):
    S = proj.shape[0]
    tm = 512

    def body(u_ref, v_ref, g_ref, vg_ref, ws_ref, b2_ref, bd_ref, y_ref):
        v = v_ref[...]
        ms = _split_dot(v * v, bd_ref[...]) * (1.0 / HEAD_DIM)
        vn = (v * lax.rsqrt(ms + EPS)) * vg_ref[...]
        wcat = _gmlp_masked_weights(ws_ref, False)
        head = _head_index((CHUNK, GMLP_WIDTH))
        for c in range(tm // CHUNK):
            rows = slice(c * CHUNK, (c + 1) * CHUNK)
            sp = jnp.dot(wcat, _head_stack(vn[rows], head), preferred_element_type=F32) + b2_ref[...]
            silu, _ = _silu_parts(g_ref[rows, :])
            y_ref[rows, :] = ((u_ref[rows, :] * sp) * silu).astype(BF16)

    col = lambda j: pl.BlockSpec((tm, GMLP_WIDTH), lambda i, j=j: (i, j))
    const = lambda shape: pl.BlockSpec(shape, lambda i: (0,) * len(shape))
    return pl.pallas_call(
        body, name="gmlp_fwd", grid=(S // tm,),
        in_specs=[col(0), col(1), col(2), const((1, GMLP_WIDTH)), const((4, CHUNK, CHUNK)),
                  const((CHUNK, GMLP_WIDTH)), const((GMLP_WIDTH, GMLP_WIDTH))],
        out_specs=pl.BlockSpec((tm, GMLP_WIDTH), lambda i: (i, 0)),
        out_shape=pltpu.HBM((S, GMLP_WIDTH), BF16),
        compiler_params=_params(24, ("arbitrary",)),
    )(*_hbm(proj, proj, proj, vg, w_s, b2, bd))


def _gmlp_bwd(proj, dycat, vg, w_s, b2, bd):
    S = proj.shape[0]
    tm = 512
    nsteps = S // tm

    def body(u_ref, v_ref, g_ref, dy_ref, vg_ref, ws_ref, b2_ref, bd_ref,
             du_ref, dv_ref, dg_ref, dws_ref, db2_ref, dvg_ref):
        i = pl.program_id(0)

        @pl.when(i == 0)
        def _():
            dws_ref[...] = jnp.zeros_like(dws_ref)
            db2_ref[...] = jnp.zeros_like(db2_ref)
            dvg_ref[...] = jnp.zeros_like(dvg_ref)

        bdv = bd_ref[...]
        v = v_ref[...]
        ms = _split_dot(v * v, bdv) * (1.0 / HEAD_DIM)
        rv = lax.rsqrt(ms + EPS)
        xhat = v * rv
        vgv = vg_ref[...]
        vn = xhat * vgv
        wcat = _gmlp_masked_weights(ws_ref, False)
        wcat_t = _gmlp_masked_weights(ws_ref, True)
        head = _head_index((CHUNK, GMLP_WIDTH))
        dvg = jnp.zeros((1, GMLP_WIDTH), F32)
        for c in range(tm // CHUNK):
            rows = slice(c * CHUNK, (c + 1) * CHUNK)
            vn_c = vn[rows]
            spb = jnp.dot(wcat, _head_stack(vn_c, head), preferred_element_type=F32) + b2_ref[...]
            silu, dsilu = _silu_parts(g_ref[rows, :])
            dy = dy_ref[rows, :]
            u = u_ref[rows, :]
            du_ref[rows, :] = (dy * spb * silu).astype(BF16)
            dg_ref[rows, :] = (dy * u * spb * dsilu).astype(BF16)
            dsp = dy * u * silu
            db2_ref[...] += dsp
            dstack = _head_stack(dsp, head)
            dvn = jnp.dot(wcat_t, dstack, preferred_element_type=F32)
            dws_ref[...] += _nt(dstack, vn_c.astype(BF16))
            xh = xhat[rows]
            a = dvn * vgv
            mean_ax = _split_dot(a * xh, bdv) * (1.0 / HEAD_DIM)
            dv_ref[rows, :] = (rv[rows] * (a - xh * mean_ax)).astype(BF16)
            dvg = dvg + jnp.sum(dvn * xh, axis=0, keepdims=True)
        dvg_ref[...] += dvg

        @pl.when(i == nsteps - 1)
        def _():
            t = lax.broadcasted_iota(jnp.int32, (4 * CHUNK, CHUNK), 0) % CHUNK
            s = lax.broadcasted_iota(jnp.int32, (4 * CHUNK, CHUNK), 1)
            dws_ref[...] = jnp.where(s <= t, dws_ref[...], 0.0)
            db2_ref[...] = _split_dot(db2_ref[...], bdv)

    col = lambda j: pl.BlockSpec((tm, GMLP_WIDTH), lambda i, j=j: (i, j))
    const = lambda shape: pl.BlockSpec(shape, lambda i: (0,) * len(shape))
    tile = pl.BlockSpec((tm, GMLP_WIDTH), lambda i: (i, 0))
    piece = pltpu.HBM((S, GMLP_WIDTH), BF16)
    return pl.pallas_call(
        body, name="gmlp_bwd", grid=(nsteps,),
        in_specs=[col(0), col(1), col(2), col(0), const((1, GMLP_WIDTH)), const((4, CHUNK, CHUNK)),
                  const((CHUNK, GMLP_WIDTH)), const((GMLP_WIDTH, GMLP_WIDTH))],
        out_specs=[tile, tile, tile, const((4 * CHUNK, CHUNK)), const((CHUNK, GMLP_WIDTH)), const((1, GMLP_WIDTH))],
        out_shape=[piece, piece, piece, pltpu.HBM((4 * CHUNK, CHUNK), F32),
                   pltpu.HBM((CHUNK, GMLP_WIDTH), F32), pltpu.HBM((1, GMLP_WIDTH), F32)],
        compiler_params=_params(32, ("arbitrary",)),
    )(*_hbm(proj, proj, proj, dycat, vg, w_s, b2, bd))


def _band_mask():
    qi = lax.broadcasted_iota(jnp.int32, (2 * BLOCK, 2 * BLOCK), 0) % BLOCK
    ki = lax.broadcasted_iota(jnp.int32, (2 * BLOCK, 2 * BLOCK), 1)
    return ((ki < BLOCK) & (ki >= qi)) | ((ki >= BLOCK) & ((ki - BLOCK) <= qi))


def _first_block_bias(blk, blocks_per_class):
    kcol = lax.broadcasted_iota(jnp.int32, (1, 2 * BLOCK), 1)
    kill = jnp.where((blk & (blocks_per_class - 1)) == 0, NEG, 0.0)
    return jnp.where(kcol < BLOCK, kill, 0.0)


def _two_heads(q, lo):
    zero = jnp.zeros_like(q)
    return jnp.concatenate([jnp.where(lo, q, zero), jnp.where(lo, zero, q)], axis=0)


def _block_tokens(blk, d, S):
    if d == 1:
        return pl.ds(pl.multiple_of(blk * BLOCK, BLOCK), BLOCK)
    blocks_per_class = S // d // BLOCK
    r = lax.shift_right_logical(blk, blocks_per_class.bit_length() - 1)
    n = blk & (blocks_per_class - 1)
    return pl.ds(r + n * (BLOCK * d), BLOCK, stride=d)


def _padded_block(blk):
    return pl.ds(pl.multiple_of((blk + 1) * BLOCK, BLOCK), BLOCK)


def _for_blocks(n_blocks, unroll, fn):
    def group(g, carry):
        for u in range(unroll):
            fn(g * unroll + u)
        return carry
    lax.fori_loop(0, n_blocks // unroll, group, 0)


def _attn_fwd(proj, qg2, kg2, bd):
    S = proj.shape[0]
    npairs = ATTN_WIDTH // 128
    tn = 512

    def body(q_ref, k_ref, v_ref, g_ref, qg_ref, kg_ref, bd_ref, y_ref, att_ref, lse_ref, qn, kn, kc, vc):
        bdv = bd_ref[...]
        lo = lax.broadcasted_iota(jnp.int32, (BLOCK, 128), 1) < HEAD_DIM
        band_mask = _band_mask()
        kc[pl.ds(0, BLOCK), :] = jnp.zeros((BLOCK, 128), BF16)
        vc[pl.ds(0, BLOCK), :] = jnp.zeros((BLOCK, 128), BF16)

        def norm_step(i, carry):
            rows = pl.ds(pl.multiple_of(i * tn, tn), tn)
            qv = q_ref[rows, :]
            kv = k_ref[rows, :]
            qn[rows, :] = (qv * lax.rsqrt(_split_dot(qv * qv, bdv) * (1.0 / HEAD_DIM) + EPS)) * (qg_ref[...] * SCALE)
            kn[rows, :] = (kv * lax.rsqrt(_split_dot(kv * kv, bdv) * (1.0 / HEAD_DIM) + EPS)) * kg_ref[...]
            return carry
        lax.fori_loop(0, S // tn, norm_step, 0)

        def fill(blk, d):
            tokens = _block_tokens(blk, d, S)
            kc[_padded_block(blk), :] = kn[tokens, :].astype(BF16)
            vc[_padded_block(blk), :] = v_ref[tokens, :].astype(BF16)

        def block(blk, d):
            tokens = _block_tokens(blk, d, S)
            keys = pl.ds(pl.multiple_of(blk * BLOCK, BLOCK), 2 * BLOCK)
            q2 = _two_heads(qn[tokens, :].astype(BF16), lo)
            s = jnp.where(band_mask, _nt(q2, kc[keys, :]), NEG) + _first_block_bias(blk, S // d // BLOCK)
            m = jnp.max(s, axis=-1, keepdims=True)
            e = jnp.exp(s - m)
            l = jnp.sum(e, axis=-1, keepdims=True)
            o2 = jnp.dot(e.astype(BF16), vc[keys, :], preferred_element_type=F32) * (1.0 / l)
            lse2 = m + jnp.log(l)
            o = jnp.where(lo, o2[:BLOCK], o2[BLOCK:])
            lse = jnp.where(lo, lse2[:BLOCK], lse2[BLOCK:])
            if d > 1:
                la = lse_ref[tokens, :]
                mx = jnp.maximum(la, lse)
                wa, wb = jnp.exp(la - mx), jnp.exp(lse - mx)
                t = wa + wb
                o = (wa * att_ref[tokens, :] + wb * o) / t
                lse = mx + jnp.log(t)
            att_ref[tokens, :] = o
            lse_ref[tokens, :] = lse

        for d in DILATIONS:
            _for_blocks(S // BLOCK, 4, functools.partial(fill, d=d))
            _for_blocks(S // BLOCK, 8, functools.partial(block, d=d))

        def gate_step(i, carry):
            rows = pl.ds(pl.multiple_of(i * tn, tn), tn)
            silu, _ = _silu_parts(g_ref[rows, :])
            y_ref[rows, :] = (att_ref[rows, :] * silu).astype(BF16)
            return carry
        lax.fori_loop(0, S // tn, gate_step, 0)

    col = lambda j0: pl.BlockSpec((S, 128), lambda p, j0=j0: (0, j0 + p))
    const = lambda shape: pl.BlockSpec(shape, lambda p: (0,) * len(shape))
    out = pl.BlockSpec((S, 128), lambda p: (0, p))
    return pl.pallas_call(
        body, name="attn_fwd", grid=(npairs,),
        in_specs=[col(COL_AQ), col(COL_AK), col(COL_AV), col(COL_AG), const((1, 128)), const((1, 128)),
                  const((128, 128))],
        out_specs=[out, out, out],
        out_shape=[pltpu.HBM((S, ATTN_WIDTH), BF16), pltpu.HBM((S, ATTN_WIDTH), F32),
                   pltpu.HBM((S, ATTN_WIDTH), F32)],
        scratch_shapes=[pltpu.VMEM((S, 128), F32), pltpu.VMEM((S, 128), F32),
                        pltpu.VMEM((S + BLOCK, 128), BF16), pltpu.VMEM((S + BLOCK, 128), BF16)],
        compiler_params=_params(48, ("arbitrary",)),
    )(*_hbm(proj, proj, proj, proj, qg2, kg2, bd))


def _attn_bwd(proj, dycat, att, lse, qg2, kg2, bd):
    S = proj.shape[0]
    npairs = ATTN_WIDTH // 128
    tn = 512

    def body(q_ref, k_ref, v_ref, g_ref, dy_ref, att_ref, lse_ref, qg_ref, kg_ref, bd_ref,
             dq_ref, dk_ref, dv_ref, dg_ref, dqg_ref, dkg_ref,
             qn, kn, rq_s, rk_s, kc, vc, do_s, dd_s, dqa, dka, dva):
        bdv = bd_ref[...]
        lo = lax.broadcasted_iota(jnp.int32, (BLOCK, 128), 1) < HEAD_DIM
        kc[pl.ds(0, BLOCK), :] = jnp.zeros((BLOCK, 128), BF16)
        vc[pl.ds(0, BLOCK), :] = jnp.zeros((BLOCK, 128), BF16)

        def prepare(i, carry):
            rows = pl.ds(pl.multiple_of(i * tn, tn), tn)
            qv = q_ref[rows, :]
            kv = k_ref[rows, :]
            rq = lax.rsqrt(_split_dot(qv * qv, bdv) * (1.0 / HEAD_DIM) + EPS)
            rk = lax.rsqrt(_split_dot(kv * kv, bdv) * (1.0 / HEAD_DIM) + EPS)
            rq_s[rows, :] = rq
            rk_s[rows, :] = rk
            qn[rows, :] = (qv * rq) * (qg_ref[...] * SCALE)
            kn[rows, :] = (kv * rk) * kg_ref[...]
            silu, dsilu = _silu_parts(g_ref[rows, :])
            dy = dy_ref[rows, :]
            at = att_ref[rows, :]
            do = dy * silu
            do_s[rows, :] = do
            dd_s[rows, :] = _split_dot(do * at, bdv)
            dg_ref[rows, :] = (dy * at * dsilu).astype(BF16)
            dka[rows, :] = jnp.zeros((tn, 128), F32)
            dva[rows, :] = jnp.zeros((tn, 128), F32)
            return carry
        lax.fori_loop(0, S // tn, prepare, 0)

        kt = lax.broadcasted_iota(jnp.int32, (2 * BLOCK, 2 * BLOCK), 0)
        qt = lax.broadcasted_iota(jnp.int32, (2 * BLOCK, 2 * BLOCK), 1) % BLOCK
        band_mask_t = ((kt < BLOCK) & (kt >= qt)) | ((kt >= BLOCK) & ((kt - BLOCK) <= qt))

        def per_query_row(t):
            tt = t.T
            return jnp.concatenate([tt[0:1, :], tt[HEAD_DIM:HEAD_DIM + 1, :]], axis=1)

        def fill(blk, d):
            tokens = _block_tokens(blk, d, S)
            kc[_padded_block(blk), :] = kn[tokens, :].astype(BF16)
            vc[_padded_block(blk), :] = v_ref[tokens, :].astype(BF16)

        def block(blk, d):
            tokens = _block_tokens(blk, d, S)
            keys = pl.ds(pl.multiple_of(blk * BLOCK, BLOCK), 2 * BLOCK)
            first = (blk & (S // d // BLOCK - 1)) == 0
            q2 = _two_heads(qn[tokens, :].astype(BF16), lo)
            do2 = _two_heads(do_s[tokens, :].astype(BF16), lo)
            lse_row = per_query_row(lse_ref[tokens, :])
            dd_row = per_query_row(dd_s[tokens, :])
            kb = kc[keys, :]
            vb = vc[keys, :]
            st = jnp.where(band_mask_t, _nt(kb, q2), NEG)
            st = jnp.concatenate([st[:BLOCK] + jnp.where(first, NEG, 0.0), st[BLOCK:]], axis=0)
            pt = jnp.exp(st - lse_row)
            dst = pt * (_nt(vb, do2) - dd_row)
            ptb = pt.astype(BF16)
            dstb = dst.astype(BF16)
            dv_band = jnp.dot(ptb, do2, preferred_element_type=F32)
            dk_band = jnp.dot(dstb, q2, preferred_element_type=F32)
            before = _block_tokens(jnp.where(first, blk, blk - 1), d, S)
            dka[before, :] = dka[before, :] + dk_band[:BLOCK]
            dva[before, :] = dva[before, :] + dv_band[:BLOCK]
            dka[tokens, :] = dka[tokens, :] + dk_band[BLOCK:]
            dva[tokens, :] = dva[tokens, :] + dv_band[BLOCK:]
            dq2 = _tn(dstb, kb)
            dq = jnp.where(lo, dq2[:BLOCK], dq2[BLOCK:])
            dqa[tokens, :] = dq if d == 1 else dqa[tokens, :] + dq

        for d in DILATIONS:
            _for_blocks(S // BLOCK, 4, functools.partial(fill, d=d))
            _for_blocks(S // BLOCK, 8, functools.partial(block, d=d))

        def out_step(i, carry):
            dqg, dkg = carry
            rows = pl.ds(pl.multiple_of(i * tn, tn), tn)
            rq = rq_s[rows, :]
            rk = rk_s[rows, :]
            qh = q_ref[rows, :] * rq
            kh = k_ref[rows, :] * rk
            dqs = dqa[rows, :] * SCALE
            dkn = dka[rows, :]
            aq = dqs * qg_ref[...]
            ak = dkn * kg_ref[...]
            dq_ref[rows, :] = (rq * (aq - qh * (_split_dot(aq * qh, bdv) * (1.0 / HEAD_DIM)))).astype(BF16)
            dk_ref[rows, :] = (rk * (ak - kh * (_split_dot(ak * kh, bdv) * (1.0 / HEAD_DIM)))).astype(BF16)
            dv_ref[rows, :] = dva[rows, :].astype(BF16)
            dqg = dqg + jnp.sum(dqs * qh, axis=0, keepdims=True)
            dkg = dkg + jnp.sum(dkn * kh, axis=0, keepdims=True)
            return dqg, dkg
        zero = jnp.zeros((1, 128), F32)
        dqg, dkg = lax.fori_loop(0, S // tn, out_step, (zero, zero))
        dqg_ref[0] = dqg
        dkg_ref[0] = dkg

    col = lambda j0: pl.BlockSpec((S, 128), lambda p, j0=j0: (0, j0 + p))
    col1 = lambda j0: pl.BlockSpec((S, 128), lambda p, j0=j0: (0, j0 + p), pipeline_mode=pl.Buffered(1))
    const = lambda shape: pl.BlockSpec(shape, lambda p: (0,) * len(shape))
    out = pl.BlockSpec((S, 128), lambda p: (0, p))
    gain_out = pl.BlockSpec((1, 1, 128), lambda p: (p, 0, 0))
    piece = pltpu.HBM((S, ATTN_WIDTH), BF16)
    gains = pltpu.HBM((npairs, 1, 128), F32)
    f32buf = pltpu.VMEM((S, 128), F32)
    bf16pad = pltpu.VMEM((S + BLOCK, 128), BF16)
    return pl.pallas_call(
        body, name="attn_bwd", grid=(npairs,),
        in_specs=[col(COL_AQ), col(COL_AK), col(COL_AV), col1(COL_AG), col1(GMLP_WIDTH // 128), col1(0), col(0),
                  const((1, 128)), const((1, 128)), const((128, 128))],
        out_specs=[out, out, out, out, gain_out, gain_out],
        out_shape=[piece, piece, piece, piece, gains, gains],
        scratch_shapes=[f32buf, f32buf, f32buf, f32buf, bf16pad, bf16pad, f32buf, f32buf, f32buf, f32buf, f32buf],
        compiler_params=_params(60, ("arbitrary",)),
    )(*_hbm(proj, proj, proj, proj, dycat, att, lse, qg2, kg2, bd))


def _mem_kv(mem, gain, wkv_bf, kg4, bd):
    def body(mem_ref, g_ref, w_ref, kg_ref, bd_ref, hm_ref, kraw_ref, mk_ref, mv_ref):
        mv_ = mem_ref[...]
        r = lax.rsqrt(jnp.mean(mv_ * mv_, axis=-1, keepdims=True) + EPS)
        hm = ((mv_ * r) * g_ref[...]).astype(BF16)
        hm_ref[...] = hm
        kv = jnp.dot(hm, w_ref[...], preferred_element_type=F32)
        kraw = kv[:, :MEM_WIDTH]
        kraw_ref[...] = kraw
        ms = _split_dot(kraw * kraw, bd_ref[...]) * (1.0 / HEAD_DIM)
        mk_ref[...] = (kraw * lax.rsqrt(ms + EPS)) * kg_ref[...]
        mv_ref[...] = kv[:, MEM_WIDTH:]

    sq = jax.ShapeDtypeStruct((MEM_LEN, MEM_WIDTH), F32)
    return pl.pallas_call(
        body, name="mem_kv",
        out_shape=[jax.ShapeDtypeStruct((MEM_LEN, D_MODEL), BF16), sq, sq, sq],
        compiler_params=_params(16),
    )(mem, gain, wkv_bf, kg4, bd)


def _mem_fwd(proj, mk, mv, qg4, bd):
    S = proj.shape[0]
    tm = 512

    def body(q_ref, g_ref, mk_ref, mv_ref, qg_ref, bd_ref, y_ref, om_ref):
        qv = q_ref[...]
        ms = _split_dot(qv * qv, bd_ref[...]) * (1.0 / HEAD_DIM)
        qs = (qv * lax.rsqrt(ms + EPS)) * (qg_ref[...] * SCALE)
        mkb = mk_ref[...].astype(BF16)
        mvb = mv_ref[...].astype(BF16)
        head = _head_index((tm, MEM_WIDTH))
        o = jnp.zeros((tm, MEM_WIDTH), F32)
        for h in range(4):
            s = _nt(jnp.where(head == h, qs, 0.0).astype(BF16), mkb)
            e = jnp.exp(s - jnp.max(s, axis=-1, keepdims=True))
            p = e * (1.0 / jnp.sum(e, axis=-1, keepdims=True))
            o = jnp.where(head == h, jnp.dot(p.astype(BF16), mvb, preferred_element_type=F32), o)
        om_ref[...] = o
        silu, _ = _silu_parts(g_ref[...])
        y_ref[...] = (o * silu).astype(BF16)

    col = lambda j: pl.BlockSpec((tm, MEM_WIDTH), lambda i, j=j: (i, j))
    const = lambda shape: pl.BlockSpec(shape, lambda i: (0,) * len(shape))
    tile = pl.BlockSpec((tm, MEM_WIDTH), lambda i: (i, 0))
    return pl.pallas_call(
        body, name="mem_fwd", grid=(S // tm,),
        in_specs=[col(11), col(12), const((MEM_LEN, MEM_WIDTH)), const((MEM_LEN, MEM_WIDTH)), const((1, MEM_WIDTH)),
                  const((MEM_WIDTH, MEM_WIDTH))],
        out_specs=[tile, tile],
        out_shape=[pltpu.HBM((S, MEM_WIDTH), BF16), pltpu.HBM((S, MEM_WIDTH), F32)],
        compiler_params=_params(24, ("arbitrary",)),
    )(*_hbm(proj, proj, mk, mv, qg4, bd))


def _mem_bwd(proj, dycat, om, mk, mv, qg4, bd):
    S = proj.shape[0]
    tm = 512

    def body(q_ref, g_ref, dy_ref, om_ref, mk_ref, mv_ref, qg_ref, bd_ref,
             dq_ref, dg_ref, dmk_ref, dmv_ref, dqg_ref):
        i = pl.program_id(0)

        @pl.when(i == 0)
        def _():
            dmk_ref[...] = jnp.zeros_like(dmk_ref)
            dmv_ref[...] = jnp.zeros_like(dmv_ref)
            dqg_ref[...] = jnp.zeros_like(dqg_ref)

        bdv = bd_ref[...]
        qv = q_ref[...]
        rq = lax.rsqrt(_split_dot(qv * qv, bdv) * (1.0 / HEAD_DIM) + EPS)
        qh = qv * rq
        qs = qh * (qg_ref[...] * SCALE)
        silu, dsilu = _silu_parts(g_ref[...])
        dy = dy_ref[...]
        o = om_ref[...]
        do = dy * silu
        dg_ref[...] = (dy * o * dsilu).astype(BF16)
        dd = _split_dot(do * o, bdv)
        mkb = mk_ref[...].astype(BF16)
        mvb = mv_ref[...].astype(BF16)
        head = _head_index((tm, MEM_WIDTH))
        dqs = jnp.zeros((tm, MEM_WIDTH), F32)
        for h in range(4):
            qhd = jnp.where(head == h, qs, 0.0).astype(BF16)
            doh = jnp.where(head == h, do, 0.0).astype(BF16)
            s = _nt(qhd, mkb)
            e = jnp.exp(s - jnp.max(s, axis=-1, keepdims=True))
            p = e * (1.0 / jnp.sum(e, axis=-1, keepdims=True))
            ds = p * (_nt(doh, mvb) - dd[:, h * HEAD_DIM:h * HEAD_DIM + 1])
            dsb = ds.astype(BF16)
            dmv_ref[...] += _tn(p.astype(BF16), doh)
            dmk_ref[...] += _tn(dsb, qhd)
            dqs = jnp.where(head == h, jnp.dot(dsb, mkb, preferred_element_type=F32), dqs)
        dqs = dqs * SCALE
        a = dqs * qg_ref[...]
        dq_ref[...] = (rq * (a - qh * (_split_dot(a * qh, bdv) * (1.0 / HEAD_DIM)))).astype(BF16)
        dqg_ref[...] += jnp.sum(dqs * qh, axis=0, keepdims=True)

    col = lambda j: pl.BlockSpec((tm, MEM_WIDTH), lambda i, j=j: (i, j))
    const = lambda shape: pl.BlockSpec(shape, lambda i: (0,) * len(shape))
    tile = pl.BlockSpec((tm, MEM_WIDTH), lambda i: (i, 0))
    piece = pltpu.HBM((S, MEM_WIDTH), BF16)
    sq = pltpu.HBM((MEM_LEN, MEM_WIDTH), F32)
    return pl.pallas_call(
        body, name="mem_bwd", grid=(S // tm,),
        in_specs=[col(11), col(12), col(3), tile, const((MEM_LEN, MEM_WIDTH)), const((MEM_LEN, MEM_WIDTH)),
                  const((1, MEM_WIDTH)), const((MEM_WIDTH, MEM_WIDTH))],
        out_specs=[tile, tile, const((MEM_LEN, MEM_WIDTH)), const((MEM_LEN, MEM_WIDTH)), const((1, MEM_WIDTH))],
        out_shape=[piece, piece, sq, sq, pltpu.HBM((1, MEM_WIDTH), F32)],
        compiler_params=_params(32, ("arbitrary",)),
    )(*_hbm(proj, proj, dycat, om, mk, mv, qg4, bd))


def _mem_kv_bwd(dmk, dmv, kraw, mem, gain, kg4, wkv_bf, hm_bf, bd):
    def body(dmk_ref, dmv_ref, kraw_ref, mem_ref, g_ref, kg_ref, w_ref, hm_ref, bd_ref, dw_ref, dg_ref, dkg_ref):
        bdv = bd_ref[...]
        kraw = kraw_ref[...]
        rk = lax.rsqrt(_split_dot(kraw * kraw, bdv) * (1.0 / HEAD_DIM) + EPS)
        kh = kraw * rk
        dmkv = dmk_ref[...]
        a = dmkv * kg_ref[...]
        dkraw = rk * (a - kh * (_split_dot(a * kh, bdv) * (1.0 / HEAD_DIM)))
        dkg_ref[...] = jnp.sum(dmkv * kh, axis=0, keepdims=True)
        dkv = jnp.concatenate([dkraw, dmv_ref[...]], axis=1).astype(BF16)
        dw = _tn(hm_ref[...], dkv).astype(BF16)
        rows_blk = D_MODEL // N_DEV
        for j in range(N_DEV):
            dw_ref[j] = dw[rows_blk * j:rows_blk * (j + 1)]
        dhm = _nt(dkv, w_ref[...])
        mv_ = mem_ref[...]
        r = lax.rsqrt(jnp.mean(mv_ * mv_, axis=-1, keepdims=True) + EPS)
        dg_ref[...] = jnp.sum(dhm * (mv_ * r), axis=0, keepdims=True)

    return pl.pallas_call(
        body, name="mem_kv_bwd",
        out_shape=[jax.ShapeDtypeStruct((N_DEV, D_MODEL // N_DEV, 2 * MEM_WIDTH), BF16),
                   jax.ShapeDtypeStruct((1, D_MODEL), F32), jax.ShapeDtypeStruct((1, MEM_WIDTH), F32)],
        compiler_params=_params(24),
    )(dmk, dmv, kraw, mem, gain, kg4, wkv_bf, hm_bf, bd)


def _out_loss(yg, ya, ym, x, tgt, wout_bf):
    S, D = x.shape
    tm = 256

    nsteps = S // tm
    rows_blk = D // N_DEV

    def body(yg_ref, ya_ref, ym_ref, x_ref, t_ref, w_ref, dout_ref, dycat_ref, dw_ref, loss_ref, acc_ref):
        i = pl.program_id(0)

        @pl.when(i == 0)
        def _():
            acc_ref[...] = jnp.zeros_like(acc_ref)
            loss_ref[...] = jnp.zeros_like(loss_ref)

        ycat = jnp.concatenate([yg_ref[...], ya_ref[...], ym_ref[...]], axis=1)
        w = w_ref[...]
        diff = (x_ref[...] + jnp.dot(ycat, w, preferred_element_type=F32)) - t_ref[...]
        loss_ref[...] += jnp.sum(diff * diff, axis=0, keepdims=True)
        dout = diff * (1.0 / D)
        dout_ref[...] = dout
        db = dout.astype(BF16)
        dycat_ref[...] = _nt(db, w)
        acc_ref[...] += _tn(ycat, db)

        @pl.when(i == nsteps - 1)
        def _():
            for j in range(N_DEV):
                dw_ref[j] = acc_ref[rows_blk * j:rows_blk * (j + 1), :].astype(BF16)

    tile = lambda w: pl.BlockSpec((tm, w), lambda i: (i, 0))
    const = lambda shape: pl.BlockSpec(shape, lambda i: (0,) * len(shape))
    return pl.pallas_call(
        body, name="out_loss", grid=(nsteps,),
        in_specs=[tile(GMLP_WIDTH), tile(ATTN_WIDTH), tile(MEM_WIDTH), tile(D), tile(D), const((D, D))],
        out_specs=[tile(D), tile(D), const((N_DEV, rows_blk, D)), const((1, D))],
        out_shape=[pltpu.HBM((S, D), F32), pltpu.HBM((S, D), F32),
                   pltpu.HBM((N_DEV, rows_blk, D), BF16), pltpu.HBM((1, D), F32)],
        scratch_shapes=[pltpu.VMEM((D, D), F32)],
        compiler_params=_params(40, ("arbitrary",)),
    )(*_hbm(yg, ya, ym, x, tgt, wout_bf))


def _piece_specs(pieces, tm):
    return [pl.BlockSpec((tm, p.shape[1]), lambda i: (i, 0)) for p in pieces]


def _in_bwd_dx(pieces, x, dout, gain, w_t, dw_blocks):
    S, D = x.shape
    N = w_t.shape[0]
    tm = 256
    n = len(pieces)
    nsteps = S // tm
    middle_step = nsteps // 8

    def body(*refs):
        piece_refs = refs[:n]
        x_ref, dout_ref, g_ref, w_ref, dwb_ref, gx_ref, dg_ref, gw_ref = refs[n:n + 8]
        rs = _ReduceScatter([dwb_ref], [gw_ref], *refs[n + 8:])
        i = pl.program_id(0)

        @pl.when(i == 0)
        def _():
            dg_ref[...] = jnp.zeros_like(dg_ref)
            rs.start()

        @pl.when(i == middle_step)
        def _():
            rs.middle()

        dproj = jnp.concatenate([r[...] for r in piece_refs], axis=1)
        dh = jnp.dot(dproj, w_ref[...], preferred_element_type=F32)
        xv = x_ref[...]
        r = lax.rsqrt(jnp.mean(xv * xv, axis=-1, keepdims=True) + EPS)
        xh = xv * r
        a = dh * g_ref[...]
        gx_ref[...] = dout_ref[...] + r * (a - xh * jnp.mean(a * xh, axis=-1, keepdims=True))
        dg_ref[...] += jnp.sum(dh * xh, axis=0, keepdims=True)

        @pl.when(i == nsteps - 1)
        def _():
            rs.finish()

    tile = pl.BlockSpec((tm, D), lambda i: (i, 0))
    const = lambda shape: pl.BlockSpec(shape, lambda i: (0,) * len(shape))
    vmem = pl.BlockSpec(memory_space=pltpu.VMEM)
    return pl.pallas_call(
        body, name="in_bwd_dx", grid=(nsteps,),
        in_specs=_piece_specs(pieces, tm) + [tile, tile, const((1, D)), const((N, D)), vmem],
        out_specs=[tile, const((1, D)), vmem],
        out_shape=[pltpu.HBM((S, D), F32), pltpu.HBM((1, D), F32), jax.ShapeDtypeStruct(dw_blocks.shape[1:], F32)],
        scratch_shapes=_reduce_scatter_scratch([dw_blocks]),
        compiler_params=_params(56, ("arbitrary",)),
    )(*_hbm(*pieces, x, dout, gain, w_t), dw_blocks)


def _in_bwd_dw(pieces, h_bf, others):
    S, D = h_bf.shape
    N = sum(p.shape[1] for p in pieces)
    n_blk = N // N_DEV
    tm = 512
    n = len(pieces)
    k = len(others)
    nsteps = S // tm

    def body(*refs):
        piece_refs = refs[:n]
        h_ref = refs[n]
        other_refs = refs[n + 1:n + 1 + k]
        dw_ref = refs[n + 1 + k]
        sum_refs = refs[n + 2 + k:n + 2 + 2 * k]
        acc_ref = refs[n + 2 + 2 * k]
        rs = _ReduceScatter(other_refs, sum_refs, *refs[n + 3 + 2 * k:])
        i = pl.program_id(0)

        @pl.when(i == 0)
        def _():
            acc_ref[...] = jnp.zeros_like(acc_ref)
            rs.start()

        @pl.when(i == 1)
        def _():
            rs.middle()

        dproj = jnp.concatenate([r[...] for r in piece_refs], axis=1)
        acc_ref[...] += _tn(h_ref[...], dproj)

        @pl.when(i == nsteps - 1)
        def _():
            for j in range(N_DEV):
                dw_ref[j] = acc_ref[:, n_blk * j:n_blk * (j + 1)].T.astype(BF16)
            rs.finish()

    vmem = pl.BlockSpec(memory_space=pltpu.VMEM)
    return pl.pallas_call(
        body, name="in_bwd_dw", grid=(nsteps,),
        in_specs=_piece_specs(pieces, tm) + [pl.BlockSpec((tm, D), lambda i: (i, 0))] + [vmem] * k,
        out_specs=[pl.BlockSpec((N_DEV, n_blk, D), lambda i: (0, 0, 0))] + [vmem] * k,
        out_shape=[pltpu.HBM((N_DEV, n_blk, D), BF16)] + [jax.ShapeDtypeStruct(o.shape[1:], F32) for o in others],
        scratch_shapes=[pltpu.VMEM((D, N), F32)] + _reduce_scatter_scratch(others),
        compiler_params=_params(56, ("arbitrary",)),
    )(*_hbm(*pieces, h_bf), *others)


def _row_step(m):
    return max(t for t in range(16, 257, 16) if m % t == 0)


def _place():
    x, y, c = lax.axis_index("x"), lax.axis_index("y"), lax.axis_index("c")
    chips = [(1 - x, y), (x, 1 - y), (1 - x, 1 - y)]
    return x, y, c, chips


class _AllGather:
    def __init__(self, srcs, outs, send_sems, recv_sems, local_sems):
        self.srcs, self.outs, self.n = srcs, outs, len(srcs)
        self.send_sems, self.recv_sems, self.local_sems = send_sems, recv_sems, local_sems

    def _rows(self, a, px, py, pc):
        m = self.srcs[a].shape[0]
        return self.outs[a].at[pl.ds((4 * px + 2 * py + pc) * m, m), :]

    def _copy(self, a, k, block, to, src=None):
        return pltpu.make_async_remote_copy(
            src_ref=self._rows(a, *block) if src is None else src, dst_ref=self._rows(a, *block),
            send_sem=self.send_sems.at[a, k], recv_sem=self.recv_sems.at[a, k], device_id=to, device_id_type=MESH)

    def _mine(self):
        x, y, c, _ = _place()
        return [pltpu.make_async_copy(self.srcs[a], self._rows(a, x, y, c), self.local_sems.at[a])
                for a in range(self.n)]

    def _first(self, far):
        x, y, c, chips = _place()
        out = []
        for a in range(self.n):
            if far:
                out.append(self._copy(a, 3, (x, y, c), (*chips[2], c), src=self.srcs[a]))
            else:
                out.append(self._copy(a, 0, (x, y, c), (x, y, 1 - c), src=self.srcs[a]))
                out += [self._copy(a, 1 + j, (x, y, c), (*chips[j], c), src=self.srcs[a]) for j in (1, 0)]
        return out

    def _passed(self, j):
        x, y, c, chips = _place()
        return [self._copy(a, 4 + j, (*chips[j], c), (x, y, 1 - c)) for a in range(self.n)]

    def start(self):
        for cp in self._mine() + self._first(far=False):
            cp.start()

    def start_far(self):
        for cp in self._first(far=True):
            cp.start()

    def from_chip(self, j):
        x, y, c, chips = _place()
        for a in range(self.n):
            self._copy(a, 1 + j, (*chips[j], c), (x, y, c)).wait_recv()
        for cp in self._passed(j):
            cp.start()

    def from_sibling(self, j=None):
        x, y, c, chips = _place()
        for a in range(self.n):
            block = (x, y, 1 - c) if j is None else (*chips[j], 1 - c)
            self._copy(a, 0 if j is None else 4 + j, block, (x, y, c)).wait_recv()

    def from_self(self):
        for cp in self._mine():
            cp.wait()

    def finish(self):
        for cp in (self._first(far=False) + self._first(far=True)
                   + self._passed(0) + self._passed(1) + self._passed(2)):
            cp.wait_send()

    def run(self):
        self.start()
        self.start_far()
        self.from_self()
        for j in range(3):
            self.from_chip(j)
        self.from_sibling()
        for j in range(3):
            self.from_sibling(j)
        self.finish()


def _gather_proj(x, gain, shards, xpos):
    S, D = x.shape
    n = len(shards)
    N = N_DEV * shards[0].shape[0]
    half = N // 2
    tm = 512
    nsteps = S // tm

    def body(*refs):
        xpos_ref, x_ref, g_ref = refs[:3]
        ins = refs[3:3 + n]
        proj_ref, h_ref = refs[3 + n:5 + n]
        outs = refs[5 + n:5 + 2 * n]
        casts = refs[5 + 2 * n:5 + 3 * n]
        whole = refs[5 + 3 * n:5 + 4 * n]
        ag = _AllGather(casts, whole, *refs[5 + 4 * n:8 + 4 * n])
        out_sems, h_all = refs[8 + 4 * n:]
        p, i = pl.program_id(0), pl.program_id(1)
        rows = pl.ds(pl.multiple_of(i * tm, tm), tm)

        @pl.when((p == 0) & (i == 0))
        def _():
            for a in range(n):
                tr = _row_step(ins[a].shape[0])

                def cast(r, carry, a=a, tr=tr):
                    at = pl.ds(pl.multiple_of(r * tr, tr), tr)
                    casts[a][at, :] = ins[a][at, :].astype(BF16)
                    return carry
                lax.fori_loop(0, ins[a].shape[0] // tr, cast, 0)
            ag.start()

        @pl.when(p == 0)
        def _():
            xv = x_ref[...]
            r = lax.rsqrt(jnp.mean(xv * xv, axis=-1, keepdims=True) + EPS)
            h = ((xv * r) * g_ref[...]).astype(BF16)
            h_ref[...] = h
            h_all[rows, :] = h

        @pl.when((p == 1) & (i == 0))
        def _():
            ag.from_self()
            ag.from_chip(1)
            ag.from_sibling()
            ag.from_sibling(1)
            ag.start_far()

        @pl.when((p == 2) & (i == 0))
        def _():
            for j in (0, 2):
                ag.from_chip(j)
            for j in (0, 2):
                ag.from_sibling(j)

        @pl.when(p > 0)
        def _():
            which = (xpos_ref[0] + p - 1) % 2
            w_half = whole[0][pl.ds(pl.multiple_of(which * half, half), half), :]
            proj_ref[...] = _nt(h_all[rows, :], w_half)

        @pl.when((p == 2) & (i == nsteps - 1))
        def _():
            ag.finish()
            to_results = [pltpu.make_async_copy(whole[a], outs[a], out_sems.at[a]) for a in range(n)]
            for cp in to_results:
                cp.start()
            for cp in to_results:
                cp.wait()

    vmem = pl.BlockSpec(memory_space=pltpu.VMEM)
    hbm = pl.BlockSpec(memory_space=pl.ANY)
    gathered = [(N_DEV * a.shape[0], a.shape[1]) for a in shards]
    x_tile = lambda p, i, xp: (jnp.where(p == 0, i, nsteps - 1), 0)
    proj_tile = lambda p, i, xp: (jnp.where(p == 0, 0, i), (xp[0] + jnp.maximum(p - 1, 0)) % 2)
    grid_spec = pltpu.PrefetchScalarGridSpec(
        num_scalar_prefetch=1, grid=(3, nsteps),
        in_specs=[pl.BlockSpec((tm, D), x_tile), pl.BlockSpec((1, D), lambda p, i, xp: (0, 0))] + [vmem] * n,
        out_specs=[pl.BlockSpec((tm, half), proj_tile), pl.BlockSpec((tm, D), x_tile)] + [hbm] * n,
        scratch_shapes=[pltpu.VMEM(a.shape, BF16) for a in shards] + [pltpu.VMEM(g, BF16) for g in gathered]
        + [pltpu.SemaphoreType.DMA((n, 7)), pltpu.SemaphoreType.DMA((n, 7)), pltpu.SemaphoreType.DMA((n,)),
           pltpu.SemaphoreType.DMA((n,)), pltpu.VMEM((S, D), BF16)])
    return pl.pallas_call(
        body, name="gather_proj", grid_spec=grid_spec,
        out_shape=[pltpu.HBM((S, N), F32), pltpu.HBM((S, D), BF16)] + [pltpu.HBM(g, BF16) for g in gathered],
        compiler_params=_params(56, ("arbitrary", "arbitrary")),
    )(xpos, *_hbm(x, gain), *shards)


ROW_NORM, ROW_MEM_NORM, ROW_V_GAIN, ROW_B, ROW_ATTN_GAINS, ROW_MEM_GAINS, ROW_W_S, ROW_LOSS = 0, 8, 16, 18, 22, 23, 24, 536
SMALL_ROWS = 544


def _gather_small(dgain, dmgain, dvg, db2, dqg, dkg, dmqg, dmkg, dws, sq):
    def body(dgain_ref, dmgain_ref, dvg_ref, db2_ref, dqg_ref, dkg_ref, dmqg_ref, dmkg_ref, dws_ref, sq_ref,
             out_ref, mine, send_sems, recv_sems, local_sems):
        first = lax.broadcasted_iota(jnp.int32, (1, 128), 1) < HEAD_DIM
        for i in range(8):
            cols = slice(128 * i, 128 * (i + 1))
            mine[ROW_NORM + i:ROW_NORM + i + 1, :] = dgain_ref[:, cols]
            mine[ROW_MEM_NORM + i:ROW_MEM_NORM + i + 1, :] = dmgain_ref[:, cols]
            mine[ROW_LOSS + i:ROW_LOSS + i + 1, :] = sq_ref[:, cols]
        mine[ROW_V_GAIN:ROW_V_GAIN + 1, :] = dvg_ref[:, 0:128]
        mine[ROW_V_GAIN + 1:ROW_V_GAIN + 2, :] = dvg_ref[:, 128:256]
        bt = db2_ref[...].T
        for h in range(4):
            mine[ROW_B + h:ROW_B + h + 1, :] = bt[HEAD_DIM * h:HEAD_DIM * h + 1, :]

        def fold_heads(t):
            return t + pltpu.roll(t, HEAD_DIM, axis=1)
        aq = fold_heads(dqg_ref[0] + dqg_ref[1] + dqg_ref[2] + dqg_ref[3])
        ak = fold_heads(dkg_ref[0] + dkg_ref[1] + dkg_ref[2] + dkg_ref[3])
        mine[ROW_ATTN_GAINS:ROW_ATTN_GAINS + 1, :] = jnp.where(first, aq, ak)
        mq = fold_heads(dmqg_ref[:, 0:128] + dmqg_ref[:, 128:256])
        mk = fold_heads(dmkg_ref[:, 0:128] + dmkg_ref[:, 128:256])
        mine[ROW_MEM_GAINS:ROW_MEM_GAINS + 1, :] = jnp.where(first, mq, mk)
        mine[ROW_W_S:ROW_W_S + 4 * CHUNK, :] = dws_ref[...]
        _AllGather([mine], [out_ref], send_sems, recv_sems, local_sems).run()

    return pl.pallas_call(
        body, name="gather_small_grads",
        out_shape=jax.ShapeDtypeStruct((N_DEV * SMALL_ROWS, 128), F32),
        scratch_shapes=[pltpu.VMEM((SMALL_ROWS, 128), F32), pltpu.SemaphoreType.DMA((1, 7)),
                        pltpu.SemaphoreType.DMA((1, 7)), pltpu.SemaphoreType.DMA((1,))],
        compiler_params=_params(16),
    )(dgain, dmgain, dvg, db2, dqg, dkg, dmqg, dmkg, dws, sq)


def _reduce_scatter_scratch(arrs):
    n = len(arrs)
    return ([pltpu.VMEM((4,) + a.shape[1:], BF16) for a in arrs] + [pltpu.VMEM((3,) + a.shape[1:], BF16) for a in arrs]
            + [pltpu.SemaphoreType.DMA((n, 7)), pltpu.SemaphoreType.DMA((n, 7))])


class _ReduceScatter:
    def __init__(self, ins, outs, *scratch):
        n = len(ins)
        self.n, self.ins, self.outs = n, ins, outs
        self.half, self.quarter = scratch[:n], scratch[n:2 * n]
        self.send_sems, self.recv_sems = scratch[2 * n:]

    def _to_sibling(self):
        x, y, c, _ = _place()
        return [pltpu.make_async_remote_copy(
            src_ref=self.ins[a].at[2 * q + (1 - c)], dst_ref=self.half[a].at[q], send_sem=self.send_sems.at[a, q],
            recv_sem=self.recv_sems.at[a, q], device_id=(x, y, 1 - c), device_id_type=MESH)
            for a in range(self.n) for q in range(4)]

    def _to_chips(self):
        _, _, c, chips = _place()
        return [pltpu.make_async_remote_copy(
            src_ref=self.half[a].at[2 * chip[0] + chip[1]], dst_ref=self.quarter[a].at[k],
            send_sem=self.send_sems.at[a, 4 + k], recv_sem=self.recv_sems.at[a, 4 + k], device_id=(*chip, c),
            device_id_type=MESH) for a in range(self.n) for k, chip in enumerate(chips)]

    def _rows(self, a, fn):
        m = self.ins[a].shape[1]
        tr = _row_step(m)

        def step(i, carry):
            fn(pl.ds(pl.multiple_of(i * tr, tr), tr))
            return carry
        lax.fori_loop(0, m // tr, step, 0)

    def start(self):
        for cp in self._to_sibling():
            cp.start()

    def middle(self):
        _, _, c, _ = _place()
        for cp in self._to_sibling():
            cp.wait_recv()
        for a in range(self.n):
            for q in range(4):
                def add_half(rows, a=a, q=q):
                    both = self.ins[a][2 * q + c, rows, :].astype(F32) + self.half[a][q, rows, :].astype(F32)
                    self.half[a][q, rows, :] = both.astype(BF16)
                self._rows(a, add_half)
        for cp in self._to_chips():
            cp.start()

    def finish(self):
        x, y, _, _ = _place()
        for cp in self._to_chips():
            cp.wait_recv()
        for a in range(self.n):
            def add_quarters(rows, a=a):
                f = lambda t: t.astype(F32)
                self.outs[a][rows, :] = ((f(self.half[a][2 * x + y, rows, :]) + f(self.quarter[a][0, rows, :]))
                                         + (f(self.quarter[a][1, rows, :]) + f(self.quarter[a][2, rows, :])))
            self._rows(a, add_quarters)
        for cp in self._to_sibling() + self._to_chips():
            cp.wait_send()


def _adamw_math(w, g, m, v):
    m = ADAM_B1 * m + (1.0 - ADAM_B1) * g
    v = ADAM_B2 * v + (1.0 - ADAM_B2) * (g * g)
    m_hat = m / (1.0 - ADAM_B1 ** ADAM_STEP)
    v_hat = v / (1.0 - ADAM_B2 ** ADAM_STEP)
    delta = -ADAM_LR * (m_hat / (jnp.sqrt(v_hat) + ADAM_EPS) + ADAM_WD * w)
    return delta, m, v


def _adamw(w, g, m, v, name):
    R, C = w.shape
    tr = _row_step(R)

    def body(w_ref, g_ref, m_ref, v_ref, d_ref, nm_ref, nv_ref):
        d_ref[...], nm_ref[...], nv_ref[...] = _adamw_math(w_ref[...], g_ref[...], m_ref[...], v_ref[...])

    tile = pl.BlockSpec((tr, C), lambda i: (i, 0))
    out = pltpu.HBM((R, C), F32)
    return pl.pallas_call(
        body, name=name, grid=(R // tr,), in_specs=[tile] * 4, out_specs=[tile] * 3, out_shape=[out] * 3,
        compiler_params=_params(16, ("arbitrary",)),
    )(*_hbm(w, g, m, v))


SMALL = ("norm_gain", "gmlp_v_gain", "gmlp_w_s", "gmlp_b", "attn_q_gain", "attn_k_gain", "mem_norm_gain",
         "mem_q_gain", "mem_k_gain")
WEIGHTS = ("norm_gain", "w_in", "gmlp_v_gain", "gmlp_w_s", "gmlp_b", "attn_q_gain", "attn_k_gain",
           "mem_norm_gain", "w_mem_kv", "mem_q_gain", "mem_k_gain", "w_out")


def _adamw_small(w, m, v, g_all):
    k = len(SMALL)
    half = slice(0, HEAD_DIM), slice(HEAD_DIM, 2 * HEAD_DIM)

    def body(*refs):
        w_refs, m_refs, v_refs = refs[:k], refs[k:2 * k], refs[2 * k:3 * k]
        g_ref = refs[3 * k]
        outs = refs[3 * k + 1:7 * k + 1]
        loss_ref, gsum = refs[7 * k + 1:]

        part = SMALL_ROWS // 4
        for p in range(4):
            acc = g_ref[part * p:part * (p + 1), :]
            for dev in range(1, N_DEV):
                acc = acc + g_ref[dev * SMALL_ROWS + part * p:dev * SMALL_ROWS + part * (p + 1), :]
            gsum[part * p:part * (p + 1), :] = acc

        def update(name, at, g):
            i = SMALL.index(name)
            d, nm, nv = _adamw_math(w_refs[i][at], g, m_refs[i][at], v_refs[i][at])
            outs[i][at], outs[k + i][at], outs[2 * k + i][at], outs[3 * k + i][at] = g, d, nm, nv

        for i in range(8):
            at = (slice(0, 1), slice(128 * i, 128 * (i + 1)))
            update("norm_gain", at, gsum[ROW_NORM + i:ROW_NORM + i + 1, :])
            update("mem_norm_gain", at, gsum[ROW_MEM_NORM + i:ROW_MEM_NORM + i + 1, :])
        for h in range(4):
            row = (0, slice(h, h + 1), slice(None))
            update("gmlp_v_gain", row, gsum[ROW_V_GAIN + h // 2:ROW_V_GAIN + h // 2 + 1, half[h % 2]])
            update("gmlp_b", row, gsum[ROW_B + h:ROW_B + h + 1, :])
            update("gmlp_w_s", (0, h), gsum[ROW_W_S + CHUNK * h:ROW_W_S + CHUNK * (h + 1), :])
        whole = (slice(0, 1), slice(None))
        update("attn_q_gain", whole, gsum[ROW_ATTN_GAINS:ROW_ATTN_GAINS + 1, half[0]])
        update("attn_k_gain", whole, gsum[ROW_ATTN_GAINS:ROW_ATTN_GAINS + 1, half[1]])
        update("mem_q_gain", whole, gsum[ROW_MEM_GAINS:ROW_MEM_GAINS + 1, half[0]])
        update("mem_k_gain", whole, gsum[ROW_MEM_GAINS:ROW_MEM_GAINS + 1, half[1]])
        loss_ref[...] = jnp.sum(gsum[ROW_LOSS:ROW_LOSS + 8, :], keepdims=True) * (0.5 / D_MODEL)

    shapes = [jax.ShapeDtypeStruct(w[name].shape, F32) for name in SMALL]
    res = pl.pallas_call(
        body, name="adamw_small",
        out_shape=shapes * 4 + [jax.ShapeDtypeStruct((1, 1), F32)],
        scratch_shapes=[pltpu.VMEM((SMALL_ROWS, 128), F32)],
        compiler_params=_params(16),
    )(*[w[n] for n in SMALL], *[m[n] for n in SMALL], *[v[n] for n in SMALL], g_all)
    trees = [dict(zip(SMALL, res[j * k:(j + 1) * k])) for j in range(4)]
    return (*trees, res[4 * k])


def _grads(x, mem, tgt, w, shards):
    bd128, bd256 = _head_blockdiag(128), _head_blockdiag(256)
    gain = w["norm_gain"].reshape(1, D_MODEL)
    vg = w["gmlp_v_gain"].reshape(1, GMLP_WIDTH)
    w_s = w["gmlp_w_s"].reshape(4, CHUNK, CHUNK)
    b2 = jnp.repeat(w["gmlp_b"].reshape(4, CHUNK).T, HEAD_DIM, axis=1)
    qg2 = jnp.tile(w["attn_q_gain"].reshape(1, HEAD_DIM), (1, 2))
    kg2 = jnp.tile(w["attn_k_gain"].reshape(1, HEAD_DIM), (1, 2))
    mqg4 = jnp.tile(w["mem_q_gain"].reshape(1, HEAD_DIM), (1, 4))
    mkg4 = jnp.tile(w["mem_k_gain"].reshape(1, HEAD_DIM), (1, 4))
    mgain = w["mem_norm_gain"].reshape(1, D_MODEL)

    xpos = lax.axis_index("x").astype(jnp.int32).reshape(1)
    proj, h_bf, win_t, wkv_bf, wout_bf = _gather_proj(x, gain, shards, xpos)
    yg = _gmlp_fwd(proj, vg, w_s, b2, bd256)
    ya, att, lse = _attn_fwd(proj, qg2, kg2, bd128)
    hm_bf, kraw, mk, mv = _mem_kv(mem, mgain, wkv_bf, mkg4, bd256)
    ym, om = _mem_fwd(proj, mk, mv, mqg4, bd256)
    dout, dycat, dwout, sq = _out_loss(yg, ya, ym, x, tgt, wout_bf)

    du, dgv, dgg, dws, db2, dvg = _gmlp_bwd(proj, dycat, vg, w_s, b2, bd256)
    dq, dk, dv, dag, dqg, dkg = _attn_bwd(proj, dycat, att, lse, qg2, kg2, bd128)
    dmq, dmg, dmk, dmv, dmqg = _mem_bwd(proj, dycat, om, mk, mv, mqg4, bd256)
    dwkv, dmgain, dmkg = _mem_kv_bwd(dmk, dmv, kraw, mem, mgain, mkg4, wkv_bf, hm_bf, bd256)
    pieces = [du, dgv, dgg, dq, dk, dv, dag, dmq, dmg]
    dwin, g_wkv, g_wout = _in_bwd_dw(pieces, h_bf, [dwkv, dwout])
    grad_x, dgain, g_win = _in_bwd_dx(pieces, x, dout, gain, win_t, dwin)
    return grad_x, g_win, g_wkv, g_wout, (dgain, dmgain, dvg, db2, dqg, dkg, dmqg, dmkg, dws, sq)


def kernel(x, mem, norm_gain, w_in, gmlp_v_gain, gmlp_w_s, gmlp_b, attn_q_gain, attn_k_gain, mem_norm_gain, w_mem_kv, mem_q_gain, mem_k_gain, w_out, loss_target, m_norm_gain, m_w_in, m_gmlp_v_gain, m_gmlp_w_s, m_gmlp_b, m_attn_q_gain, m_attn_k_gain, m_mem_norm_gain, m_w_mem_kv, m_mem_q_gain, m_mem_k_gain, m_w_out, v_norm_gain, v_w_in, v_gmlp_v_gain, v_gmlp_w_s, v_gmlp_b, v_attn_q_gain, v_attn_k_gain, v_mem_norm_gain, v_w_mem_kv, v_mem_q_gain, v_mem_k_gain, v_w_out):
    w = dict(norm_gain=norm_gain, w_in=w_in, gmlp_v_gain=gmlp_v_gain, gmlp_w_s=gmlp_w_s, gmlp_b=gmlp_b,
             attn_q_gain=attn_q_gain, attn_k_gain=attn_k_gain, mem_norm_gain=mem_norm_gain, w_mem_kv=w_mem_kv,
             mem_q_gain=mem_q_gain, mem_k_gain=mem_k_gain, w_out=w_out)
    m = dict(norm_gain=m_norm_gain, w_in=m_w_in, gmlp_v_gain=m_gmlp_v_gain, gmlp_w_s=m_gmlp_w_s, gmlp_b=m_gmlp_b,
             attn_q_gain=m_attn_q_gain, attn_k_gain=m_attn_k_gain, mem_norm_gain=m_mem_norm_gain,
             w_mem_kv=m_w_mem_kv, mem_q_gain=m_mem_q_gain, mem_k_gain=m_mem_k_gain, w_out=m_w_out)
    v = dict(norm_gain=v_norm_gain, w_in=v_w_in, gmlp_v_gain=v_gmlp_v_gain, gmlp_w_s=v_gmlp_w_s, gmlp_b=v_gmlp_b,
             attn_q_gain=v_attn_q_gain, attn_k_gain=v_attn_k_gain, mem_norm_gain=v_mem_norm_gain,
             w_mem_kv=v_w_mem_kv, mem_q_gain=v_mem_q_gain, mem_k_gain=v_mem_k_gain, w_out=v_w_out)
    transposed = lambda t: jnp.transpose(t[0])

    grad_x, g_win, g_wkv, g_wout, small = _grads(
        x[0], mem[0], loss_target[0], w, [transposed(w_in), w_mem_kv[0], w_out[0]])
    small_all = _gather_small(*small)

    out_g, out_d, out_m, out_v, loss = _adamw_small(w, m, v, small_all)
    d_, m_, v_ = _adamw(transposed(w_in), g_win, transposed(m_w_in), transposed(v_w_in), "adamw_w_in")
    for tree, t in ((out_g, g_win), (out_d, d_), (out_m, m_), (out_v, v_)):
        tree["w_in"] = jnp.transpose(t)[None]
    for name, g in (("w_mem_kv", g_wkv), ("w_out", g_wout)):
        d_, m_, v_ = _adamw(w[name][0], g, m[name][0], v[name][0], "adamw_" + name)
        out_g[name], out_d[name], out_m[name], out_v[name] = g[None], d_[None], m_[None], v_[None]

    return (loss.reshape(()), grad_x[None], *[out_g[k] for k in WEIGHTS], *[out_d[k] for k in WEIGHTS],
            *[out_m[k] for k in WEIGHTS], *[out_v[k] for k in WEIGHTS])
```

```python
import functools
import math

import jax
import jax.numpy as jnp
from jax import lax
from jax.experimental import pallas as pl
from jax.experimental.pallas import tpu as pltpu

F32 = jnp.float32
BF16 = jnp.bfloat16

N_DEV = 8
D_MODEL = 1024
HEAD_DIM = 64
GMLP_WIDTH = 256
ATTN_WIDTH = 512
MEM_WIDTH = 256
MEM_LEN = 256
IN_WIDTH = 3 * GMLP_WIDTH + 4 * ATTN_WIDTH + 2 * MEM_WIDTH
CHUNK = 128
BLOCK = 128
DILATIONS = (1, 4, 16)
EPS = 1e-6
SCALE = 1.0 / math.sqrt(HEAD_DIM)
NEG = -1e30

ADAM_LR = 0.001
ADAM_B1 = 0.9
ADAM_B2 = 0.999
ADAM_EPS = 1e-08
ADAM_WD = 0.01
ADAM_STEP = 10

MIB = 1024 * 1024
MESH = pl.DeviceIdType.MESH

COL_AQ, COL_AK, COL_AV, COL_AG = 6, 10, 14, 18


def _params(vmem_mib, semantics=None):
    kw = dict(vmem_limit_bytes=vmem_mib * MIB)
    if semantics is not None:
        kw["dimension_semantics"] = semantics
    return pltpu.CompilerParams(**kw)


def _hbm(*arrs):
    return [pltpu.with_memory_space_constraint(a, pltpu.HBM) for a in arrs]


def _split_dot(x, sel_bf):
    hi = x.astype(BF16)
    lo = (x - hi.astype(F32)).astype(BF16)
    return jnp.dot(hi, sel_bf, preferred_element_type=F32) + jnp.dot(lo, sel_bf, preferred_element_type=F32)


def _nt(a, b):
    return lax.dot_general(a, b, (((1,), (1,)), ((), ())), preferred_element_type=F32)


def _tn(a, b):
    return lax.dot_general(a, b, (((0,), (0,)), ((), ())), preferred_element_type=F32)


def _silu_parts(g):
    sg = jax.nn.sigmoid(g)
    return g * sg, sg * (1.0 + g * (1.0 - sg))


def _head_index(shape):
    return lax.shift_right_logical(lax.broadcasted_iota(jnp.int32, shape, 1), HEAD_DIM.bit_length() - 1)


def _head_blockdiag(width):
    i = jnp.arange(width) // HEAD_DIM
    return (i[:, None] == i[None, :]).astype(BF16)


def _gmlp_masked_weights(ws_ref, transpose):
    t = lax.broadcasted_iota(jnp.int32, (CHUNK, CHUNK), 0)
    s = lax.broadcasted_iota(jnp.int32, (CHUNK, CHUNK), 1)
    parts = []
    for h in range(4):
        wm = jnp.where(s <= t, ws_ref[h], 0.0)
        parts.append(wm.T if transpose else wm)
    return jnp.concatenate(parts, axis=1).astype(BF16)


def _head_stack(v, head):
    return jnp.concatenate([jnp.where(head == h, v, 0.0) for h in range(4)], axis=0).astype(BF16)


def _gmlp_fwd(proj, vg, w_s, b2, bd):
    S = proj.shape[0]
    tm = 512

    def body(u_ref, v_ref, g_ref, vg_ref, ws_ref, b2_ref, bd_ref, y_ref):
        v = v_ref[...]
        ms = _split_dot(v * v, bd_ref[...]) * (1.0 / HEAD_DIM)
        vn = (v * lax.rsqrt(ms + EPS)) * vg_ref[...]
        wcat = _gmlp_masked_weights(ws_ref, False)
        head = _head_index((CHUNK, GMLP_WIDTH))
        for c in range(tm // CHUNK):
            rows = slice(c * CHUNK, (c + 1) * CHUNK)
            sp = jnp.dot(wcat, _head_stack(vn[rows], head), preferred_element_type=F32) + b2_ref[...]
            silu, _ = _silu_parts(g_ref[rows, :])
            y_ref[rows, :] = ((u_ref[rows, :] * sp) * silu).astype(BF16)

    col = lambda j: pl.BlockSpec((tm, GMLP_WIDTH), lambda i, j=j: (i, j))
    const = lambda shape: pl.BlockSpec(shape, lambda i: (0,) * len(shape))
    return pl.pallas_call(
        body, name="gmlp_fwd", grid=(S // tm,),
        in_specs=[col(0), col(1), col(2), const((1, GMLP_WIDTH)), const((4, CHUNK, CHUNK)),
                  const((CHUNK, GMLP_WIDTH)), const((GMLP_WIDTH, GMLP_WIDTH))],
        out_specs=pl.BlockSpec((tm, GMLP_WIDTH), lambda i: (i, 0)),
        out_shape=pltpu.HBM((S, GMLP_WIDTH), BF16),
        compiler_params=_params(24, ("arbitrary",)),
    )(*_hbm(proj, proj, proj, vg, w_s, b2, bd))


def _gmlp_bwd(proj, dycat, vg, w_s, b2, bd):
    S = proj.shape[0]
    tm = 512
    nsteps = S // tm

    def body(u_ref, v_ref, g_ref, dy_ref, vg_ref, ws_ref, b2_ref, bd_ref,
             du_ref, dv_ref, dg_ref, dws_ref, db2_ref, dvg_ref):
        i = pl.program_id(0)

        @pl.when(i == 0)
        def _():
            dws_ref[...] = jnp.zeros_like(dws_ref)
            db2_ref[...] = jnp.zeros_like(db2_ref)
            dvg_ref[...] = jnp.zeros_like(dvg_ref)

        bdv = bd_ref[...]
        v = v_ref[...]
        ms = _split_dot(v * v, bdv) * (1.0 / HEAD_DIM)
        rv = lax.rsqrt(ms + EPS)
        xhat = v * rv
        vgv = vg_ref[...]
        vn = xhat * vgv
        wcat = _gmlp_masked_weights(ws_ref, False)
        wcat_t = _gmlp_masked_weights(ws_ref, True)
        head = _head_index((CHUNK, GMLP_WIDTH))
        dvg = jnp.zeros((1, GMLP_WIDTH), F32)
        for c in range(tm // CHUNK):
            rows = slice(c * CHUNK, (c + 1) * CHUNK)
            vn_c = vn[rows]
            spb = jnp.dot(wcat, _head_stack(vn_c, head), preferred_element_type=F32) + b2_ref[...]
            silu, dsilu = _silu_parts(g_ref[rows, :])
            dy = dy_ref[rows, :]
            u = u_ref[rows, :]
            du_ref[rows, :] = (dy * spb * silu).astype(BF16)
            dg_ref[rows, :] = (dy * u * spb * dsilu).astype(BF16)
            dsp = dy * u * silu
            db2_ref[...] += dsp
            dstack = _head_stack(dsp, head)
            dvn = jnp.dot(wcat_t, dstack, preferred_element_type=F32)
            dws_ref[...] += _nt(dstack, vn_c.astype(BF16))
            xh = xhat[rows]
            a = dvn * vgv
            mean_ax = _split_dot(a * xh, bdv) * (1.0 / HEAD_DIM)
            dv_ref[rows, :] = (rv[rows] * (a - xh * mean_ax)).astype(BF16)
            dvg = dvg + jnp.sum(dvn * xh, axis=0, keepdims=True)
        dvg_ref[...] += dvg

        @pl.when(i == nsteps - 1)
        def _():
            t = lax.broadcasted_iota(jnp.int32, (4 * CHUNK, CHUNK), 0) % CHUNK
            s = lax.broadcasted_iota(jnp.int32, (4 * CHUNK, CHUNK), 1)
            dws_ref[...] = jnp.where(s <= t, dws_ref[...], 0.0)
            db2_ref[...] = _split_dot(db2_ref[...], bdv)

    col = lambda j: pl.BlockSpec((tm, GMLP_WIDTH), lambda i, j=j: (i, j))
    const = lambda shape: pl.BlockSpec(shape, lambda i: (0,) * len(shape))
    tile = pl.BlockSpec((tm, GMLP_WIDTH), lambda i: (i, 0))
    piece = pltpu.HBM((S, GMLP_WIDTH), BF16)
    return pl.pallas_call(
        body, name="gmlp_bwd", grid=(nsteps,),
        in_specs=[col(0), col(1), col(2), col(0), const((1, GMLP_WIDTH)), const((4, CHUNK, CHUNK)),
                  const((CHUNK, GMLP_WIDTH)), const((GMLP_WIDTH, GMLP_WIDTH))],
        out_specs=[tile, tile, tile, const((4 * CHUNK, CHUNK)), const((CHUNK, GMLP_WIDTH)), const((1, GMLP_WIDTH))],
        out_shape=[piece, piece, piece, pltpu.HBM((4 * CHUNK, CHUNK), F32),
                   pltpu.HBM((CHUNK, GMLP_WIDTH), F32), pltpu.HBM((1, GMLP_WIDTH), F32)],
        compiler_params=_params(32, ("arbitrary",)),
    )(*_hbm(proj, proj, proj, dycat, vg, w_s, b2, bd))


def _band_mask():
    qi = lax.broadcasted_iota(jnp.int32, (2 * BLOCK, 2 * BLOCK), 0) % BLOCK
    ki = lax.broadcasted_iota(jnp.int32, (2 * BLOCK, 2 * BLOCK), 1)
    return ((ki < BLOCK) & (ki >= qi)) | ((ki >= BLOCK) & ((ki - BLOCK) <= qi))


def _first_block_bias(blk, blocks_per_class):
    kcol = lax.broadcasted_iota(jnp.int32, (1, 2 * BLOCK), 1)
    kill = jnp.where((blk & (blocks_per_class - 1)) == 0, NEG, 0.0)
    return jnp.where(kcol < BLOCK, kill, 0.0)


def _two_heads(q, lo):
    zero = jnp.zeros_like(q)
    return jnp.concatenate([jnp.where(lo, q, zero), jnp.where(lo, zero, q)], axis=0)


def _block_tokens(blk, d, S):
    if d == 1:
        return pl.ds(pl.multiple_of(blk * BLOCK, BLOCK), BLOCK)
    blocks_per_class = S // d // BLOCK
    r = lax.shift_right_logical(blk, blocks_per_class.bit_length() - 1)
    n = blk & (blocks_per_class - 1)
    return pl.ds(r + n * (BLOCK * d), BLOCK, stride=d)


def _padded_block(blk):
    return pl.ds(pl.multiple_of((blk + 1) * BLOCK, BLOCK), BLOCK)


def _for_blocks(n_blocks, unroll, fn):
    def group(g, carry):
        for u in range(unroll):
            fn(g * unroll + u)
        return carry
    lax.fori_loop(0, n_blocks // unroll, group, 0)


def _attn_fwd(proj, qg2, kg2, bd):
    S = proj.shape[0]
    npairs = ATTN_WIDTH // 128
    tn = 512

    def body(q_ref, k_ref, v_ref, g_ref, qg_ref, kg_ref, bd_ref, y_ref, att_ref, lse_ref, qn, kn, kc, vc):
        bdv = bd_ref[...]
        lo = lax.broadcasted_iota(jnp.int32, (BLOCK, 128), 1) < HEAD_DIM
        band_mask = _band_mask()
        kc[pl.ds(0, BLOCK), :] = jnp.zeros((BLOCK, 128), BF16)
        vc[pl.ds(0, BLOCK), :] = jnp.zeros((BLOCK, 128), BF16)

        def norm_step(i, carry):
            rows = pl.ds(pl.multiple_of(i * tn, tn), tn)
            qv = q_ref[rows, :]
            kv = k_ref[rows, :]
            qn[rows, :] = (qv * lax.rsqrt(_split_dot(qv * qv, bdv) * (1.0 / HEAD_DIM) + EPS)) * (qg_ref[...] * SCALE)
            kn[rows, :] = (kv * lax.rsqrt(_split_dot(kv * kv, bdv) * (1.0 / HEAD_DIM) + EPS)) * kg_ref[...]
            return carry
        lax.fori_loop(0, S // tn, norm_step, 0)

        def fill(blk, d):
            tokens = _block_tokens(blk, d, S)
            kc[_padded_block(blk), :] = kn[tokens, :].astype(BF16)
            vc[_padded_block(blk), :] = v_ref[tokens, :].astype(BF16)

        def block(blk, d):
            tokens = _block_tokens(blk, d, S)
            keys = pl.ds(pl.multiple_of(blk * BLOCK, BLOCK), 2 * BLOCK)
            q2 = _two_heads(qn[tokens, :].astype(BF16), lo)
            s = jnp.where(band_mask, _nt(q2, kc[keys, :]), NEG) + _first_block_bias(blk, S // d // BLOCK)
            m = jnp.max(s, axis=-1, keepdims=True)
            e = jnp.exp(s - m)
            l = jnp.sum(e, axis=-1, keepdims=True)
            o2 = jnp.dot(e.astype(BF16), vc[keys, :], preferred_element_type=F32) * (1.0 / l)
            lse2 = m + jnp.log(l)
            o = jnp.where(lo, o2[:BLOCK], o2[BLOCK:])
            lse = jnp.where(lo, lse2[:BLOCK], lse2[BLOCK:])
            if d > 1:
                la = lse_ref[tokens, :]
                mx = jnp.maximum(la, lse)
                wa, wb = jnp.exp(la - mx), jnp.exp(lse - mx)
                t = wa + wb
                o = (wa * att_ref[tokens, :] + wb * o) / t
                lse = mx + jnp.log(t)
            att_ref[tokens, :] = o
            lse_ref[tokens, :] = lse

        for d in DILATIONS:
            _for_blocks(S // BLOCK, 4, functools.partial(fill, d=d))
            _for_blocks(S // BLOCK, 8, functools.partial(block, d=d))

        def gate_step(i, carry):
            rows = pl.ds(pl.multiple_of(i * tn, tn), tn)
            silu, _ = _silu_parts(g_ref[rows, :])
            y_ref[rows, :] = (att_ref[rows, :] * silu).astype(BF16)
            return carry
        lax.fori_loop(0, S // tn, gate_step, 0)

    col = lambda j0: pl.BlockSpec((S, 128), lambda p, j0=j0: (0, j0 + p))
    const = lambda shape: pl.BlockSpec(shape, lambda p: (0,) * len(shape))
    out = pl.BlockSpec((S, 128), lambda p: (0, p))
    return pl.pallas_call(
        body, name="attn_fwd", grid=(npairs,),
        in_specs=[col(COL_AQ), col(COL_AK), col(COL_AV), col(COL_AG), const((1, 128)), const((1, 128)),
                  const((128, 128))],
        out_specs=[out, out, out],
        out_shape=[pltpu.HBM((S, ATTN_WIDTH), BF16), pltpu.HBM((S, ATTN_WIDTH), F32),
                   pltpu.HBM((S, ATTN_WIDTH), F32)],
        scratch_shapes=[pltpu.VMEM((S, 128), F32), pltpu.VMEM((S, 128), F32),
                        pltpu.VMEM((S + BLOCK, 128), BF16), pltpu.VMEM((S + BLOCK, 128), BF16)],
        compiler_params=_params(48, ("arbitrary",)),
    )(*_hbm(proj, proj, proj, proj, qg2, kg2, bd))


def _attn_bwd(proj, dycat, att, lse, qg2, kg2, bd):
    S = proj.shape[0]
    npairs = ATTN_WIDTH // 128
    tn = 512

    def body(q_ref, k_ref, v_ref, g_ref, dy_ref, att_ref, lse_ref, qg_ref, kg_ref, bd_ref,
             dq_ref, dk_ref, dv_ref, dg_ref, dqg_ref, dkg_ref,
             qn, kn, rq_s, rk_s, kc, vc, do_s, dd_s, dqa, dka, dva):
        bdv = bd_ref[...]
        lo = lax.broadcasted_iota(jnp.int32, (BLOCK, 128), 1) < HEAD_DIM
        kc[pl.ds(0, BLOCK), :] = jnp.zeros((BLOCK, 128), BF16)
        vc[pl.ds(0, BLOCK), :] = jnp.zeros((BLOCK, 128), BF16)

        def prepare(i, carry):
            rows = pl.ds(pl.multiple_of(i * tn, tn), tn)
            qv = q_ref[rows, :]
            kv = k_ref[rows, :]
            rq = lax.rsqrt(_split_dot(qv * qv, bdv) * (1.0 / HEAD_DIM) + EPS)
            rk = lax.rsqrt(_split_dot(kv * kv, bdv) * (1.0 / HEAD_DIM) + EPS)
            rq_s[rows, :] = rq
            rk_s[rows, :] = rk
            qn[rows, :] = (qv * rq) * (qg_ref[...] * SCALE)
            kn[rows, :] = (kv * rk) * kg_ref[...]
            silu, dsilu = _silu_parts(g_ref[rows, :])
            dy = dy_ref[rows, :]
            at = att_ref[rows, :]
            do = dy * silu
            do_s[rows, :] = do
            dd_s[rows, :] = _split_dot(do * at, bdv)
            dg_ref[rows, :] = (dy * at * dsilu).astype(BF16)
            dka[rows, :] = jnp.zeros((tn, 128), F32)
            dva[rows, :] = jnp.zeros((tn, 128), F32)
            return carry
        lax.fori_loop(0, S // tn, prepare, 0)

        kt = lax.broadcasted_iota(jnp.int32, (2 * BLOCK, 2 * BLOCK), 0)
        qt = lax.broadcasted_iota(jnp.int32, (2 * BLOCK, 2 * BLOCK), 1) % BLOCK
        band_mask_t = ((kt < BLOCK) & (kt >= qt)) | ((kt >= BLOCK) & ((kt - BLOCK) <= qt))

        def per_query_row(t):
            tt = t.T
            return jnp.concatenate([tt[0:1, :], tt[HEAD_DIM:HEAD_DIM + 1, :]], axis=1)

        def fill(blk, d):
            tokens = _block_tokens(blk, d, S)
            kc[_padded_block(blk), :] = kn[tokens, :].astype(BF16)
            vc[_padded_block(blk), :] = v_ref[tokens, :].astype(BF16)

        def block(blk, d):
            tokens = _block_tokens(blk, d, S)
            keys = pl.ds(pl.multiple_of(blk * BLOCK, BLOCK), 2 * BLOCK)
            first = (blk & (S // d // BLOCK - 1)) == 0
            q2 = _two_heads(qn[tokens, :].astype(BF16), lo)
            do2 = _two_heads(do_s[tokens, :].astype(BF16), lo)
            lse_row = per_query_row(lse_ref[tokens, :])
            dd_row = per_query_row(dd_s[tokens, :])
            kb = kc[keys, :]
            vb = vc[keys, :]
            st = jnp.where(band_mask_t, _nt(kb, q2), NEG)
            st = jnp.concatenate([st[:BLOCK] + jnp.where(first, NEG, 0.0), st[BLOCK:]], axis=0)
            pt = jnp.exp(st - lse_row)
            dst = pt * (_nt(vb, do2) - dd_row)
            ptb = pt.astype(BF16)
            dstb = dst.astype(BF16)
            dv_band = jnp.dot(ptb, do2, preferred_element_type=F32)
            dk_band = jnp.dot(dstb, q2, preferred_element_type=F32)
            before = _block_tokens(jnp.where(first, blk, blk - 1), d, S)
            dka[before, :] = dka[before, :] + dk_band[:BLOCK]
            dva[before, :] = dva[before, :] + dv_band[:BLOCK]
            dka[tokens, :] = dka[tokens, :] + dk_band[BLOCK:]
            dva[tokens, :] = dva[tokens, :] + dv_band[BLOCK:]
            dq2 = _tn(dstb, kb)
            dq = jnp.where(lo, dq2[:BLOCK], dq2[BLOCK:])
            dqa[tokens, :] = dq if d == 1 else dqa[tokens, :] + dq

        for d in DILATIONS:
            _for_blocks(S // BLOCK, 4, functools.partial(fill, d=d))
            _for_blocks(S // BLOCK, 8, functools.partial(block, d=d))

        def out_step(i, carry):
            dqg, dkg = carry
            rows = pl.ds(pl.multiple_of(i * tn, tn), tn)
            rq = rq_s[rows, :]
            rk = rk_s[rows, :]
            qh = q_ref[rows, :] * rq
            kh = k_ref[rows, :] * rk
            dqs = dqa[rows, :] * SCALE
            dkn = dka[rows, :]
            aq = dqs * qg_ref[...]
            ak = dkn * kg_ref[...]
            dq_ref[rows, :] = (rq * (aq - qh * (_split_dot(aq * qh, bdv) * (1.0 / HEAD_DIM)))).astype(BF16)
            dk_ref[rows, :] = (rk * (ak - kh * (_split_dot(ak * kh, bdv) * (1.0 / HEAD_DIM)))).astype(BF16)
            dv_ref[rows, :] = dva[rows, :].astype(BF16)
            dqg = dqg + jnp.sum(dqs * qh, axis=0, keepdims=True)
            dkg = dkg + jnp.sum(dkn * kh, axis=0, keepdims=True)
            return dqg, dkg
        zero = jnp.zeros((1, 128), F32)
        dqg, dkg = lax.fori_loop(0, S // tn, out_step, (zero, zero))
        dqg_ref[0] = dqg
        dkg_ref[0] = dkg

    col = lambda j0: pl.BlockSpec((S, 128), lambda p, j0=j0: (0, j0 + p))
    col1 = lambda j0: pl.BlockSpec((S, 128), lambda p, j0=j0: (0, j0 + p), pipeline_mode=pl.Buffered(1))
    const = lambda shape: pl.BlockSpec(shape, lambda p: (0,) * len(shape))
    out = pl.BlockSpec((S, 128), lambda p: (0, p))
    gain_out = pl.BlockSpec((1, 1, 128), lambda p: (p, 0, 0))
    piece = pltpu.HBM((S, ATTN_WIDTH), BF16)
    gains = pltpu.HBM((npairs, 1, 128), F32)
    f32buf = pltpu.VMEM((S, 128), F32)
    bf16pad = pltpu.VMEM((S + BLOCK, 128), BF16)
    return pl.pallas_call(
        body, name="attn_bwd", grid=(npairs,),
        in_specs=[col(COL_AQ), col(COL_AK), col(COL_AV), col1(COL_AG), col1(GMLP_WIDTH // 128), col1(0), col(0),
                  const((1, 128)), const((1, 128)), const((128, 128))],
        out_specs=[out, out, out, out, gain_out, gain_out],
        out_shape=[piece, piece, piece, piece, gains, gains],
        scratch_shapes=[f32buf, f32buf, f32buf, f32buf, bf16pad, bf16pad, f32buf, f32buf, f32buf, f32buf, f32buf],
        compiler_params=_params(60, ("arbitrary",)),
    )(*_hbm(proj, proj, proj, proj, dycat, att, lse, qg2, kg2, bd))


def _mem_kv(mem, gain, wkv_bf, kg4, bd):
    def body(mem_ref, g_ref, w_ref, kg_ref, bd_ref, hm_ref, kraw_ref, mk_ref, mv_ref):
        mv_ = mem_ref[...]
        r = lax.rsqrt(jnp.mean(mv_ * mv_, axis=-1, keepdims=True) + EPS)
        hm = ((mv_ * r) * g_ref[...]).astype(BF16)
        hm_ref[...] = hm
        kv = jnp.dot(hm, w_ref[...], preferred_element_type=F32)
        kraw = kv[:, :MEM_WIDTH]
        kraw_ref[...] = kraw
        ms = _split_dot(kraw * kraw, bd_ref[...]) * (1.0 / HEAD_DIM)
        mk_ref[...] = (kraw * lax.rsqrt(ms + EPS)) * kg_ref[...]
        mv_ref[...] = kv[:, MEM_WIDTH:]

    sq = jax.ShapeDtypeStruct((MEM_LEN, MEM_WIDTH), F32)
    return pl.pallas_call(
        body, name="mem_kv",
        out_shape=[jax.ShapeDtypeStruct((MEM_LEN, D_MODEL), BF16), sq, sq, sq],
        compiler_params=_params(16),
    )(mem, gain, wkv_bf, kg4, bd)


def _mem_fwd(proj, mk, mv, qg4, bd):
    S = proj.shape[0]
    tm = 512

    def body(q_ref, g_ref, mk_ref, mv_ref, qg_ref, bd_ref, y_ref, om_ref):
        qv = q_ref[...]
        ms = _split_dot(qv * qv, bd_ref[...]) * (1.0 / HEAD_DIM)
        qs = (qv * lax.rsqrt(ms + EPS)) * (qg_ref[...] * SCALE)
        mkb = mk_ref[...].astype(BF16)
        mvb = mv_ref[...].astype(BF16)
        head = _head_index((tm, MEM_WIDTH))
        o = jnp.zeros((tm, MEM_WIDTH), F32)
        for h in range(4):
            s = _nt(jnp.where(head == h, qs, 0.0).astype(BF16), mkb)
            e = jnp.exp(s - jnp.max(s, axis=-1, keepdims=True))
            p = e * (1.0 / jnp.sum(e, axis=-1, keepdims=True))
            o = jnp.where(head == h, jnp.dot(p.astype(BF16), mvb, preferred_element_type=F32), o)
        om_ref[...] = o
        silu, _ = _silu_parts(g_ref[...])
        y_ref[...] = (o * silu).astype(BF16)

    col = lambda j: pl.BlockSpec((tm, MEM_WIDTH), lambda i, j=j: (i, j))
    const = lambda shape: pl.BlockSpec(shape, lambda i: (0,) * len(shape))
    tile = pl.BlockSpec((tm, MEM_WIDTH), lambda i: (i, 0))
    return pl.pallas_call(
        body, name="mem_fwd", grid=(S // tm,),
        in_specs=[col(11), col(12), const((MEM_LEN, MEM_WIDTH)), const((MEM_LEN, MEM_WIDTH)), const((1, MEM_WIDTH)),
                  const((MEM_WIDTH, MEM_WIDTH))],
        out_specs=[tile, tile],
        out_shape=[pltpu.HBM((S, MEM_WIDTH), BF16), pltpu.HBM((S, MEM_WIDTH), F32)],
        compiler_params=_params(24, ("arbitrary",)),
    )(*_hbm(proj, proj, mk, mv, qg4, bd))


def _mem_bwd(proj, dycat, om, mk, mv, qg4, bd):
    S = proj.shape[0]
    tm = 512

    def body(q_ref, g_ref, dy_ref, om_ref, mk_ref, mv_ref, qg_ref, bd_ref,
             dq_ref, dg_ref, dmk_ref, dmv_ref, dqg_ref):
        i = pl.program_id(0)

        @pl.when(i == 0)
        def _():
            dmk_ref[...] = jnp.zeros_like(dmk_ref)
            dmv_ref[...] = jnp.zeros_like(dmv_ref)
            dqg_ref[...] = jnp.zeros_like(dqg_ref)

        bdv = bd_ref[...]
        qv = q_ref[...]
        rq = lax.rsqrt(_split_dot(qv * qv, bdv) * (1.0 / HEAD_DIM) + EPS)
        qh = qv * rq
        qs = qh * (qg_ref[...] * SCALE)
        silu, dsilu = _silu_parts(g_ref[...])
        dy = dy_ref[...]
        o = om_ref[...]
        do = dy * silu
        dg_ref[...] = (dy * o * dsilu).astype(BF16)
        dd = _split_dot(do * o, bdv)
        mkb = mk_ref[...].astype(BF16)
        mvb = mv_ref[...].astype(BF16)
        head = _head_index((tm, MEM_WIDTH))
        dqs = jnp.zeros((tm, MEM_WIDTH), F32)
        for h in range(4):
            qhd = jnp.where(head == h, qs, 0.0).astype(BF16)
            doh = jnp.where(head == h, do, 0.0).astype(BF16)
            s = _nt(qhd, mkb)
            e = jnp.exp(s - jnp.max(s, axis=-1, keepdims=True))
            p = e * (1.0 / jnp.sum(e, axis=-1, keepdims=True))
            ds = p * (_nt(doh, mvb) - dd[:, h * HEAD_DIM:h * HEAD_DIM + 1])
            dsb = ds.astype(BF16)
            dmv_ref[...] += _tn(p.astype(BF16), doh)
            dmk_ref[...] += _tn(dsb, qhd)
            dqs = jnp.where(head == h, jnp.dot(dsb, mkb, preferred_element_type=F32), dqs)
        dqs = dqs * SCALE
        a = dqs * qg_ref[...]
        dq_ref[...] = (rq * (a - qh * (_split_dot(a * qh, bdv) * (1.0 / HEAD_DIM)))).astype(BF16)
        dqg_ref[...] += jnp.sum(dqs * qh, axis=0, keepdims=True)

    col = lambda j: pl.BlockSpec((tm, MEM_WIDTH), lambda i, j=j: (i, j))
    const = lambda shape: pl.BlockSpec(shape, lambda i: (0,) * len(shape))
    tile = pl.BlockSpec((tm, MEM_WIDTH), lambda i: (i, 0))
    piece = pltpu.HBM((S, MEM_WIDTH), BF16)
    sq = pltpu.HBM((MEM_LEN, MEM_WIDTH), F32)
    return pl.pallas_call(
        body, name="mem_bwd", grid=(S // tm,),
        in_specs=[col(11), col(12), col(3), tile, const((MEM_LEN, MEM_WIDTH)), const((MEM_LEN, MEM_WIDTH)),
                  const((1, MEM_WIDTH)), const((MEM_WIDTH, MEM_WIDTH))],
        out_specs=[tile, tile, const((MEM_LEN, MEM_WIDTH)), const((MEM_LEN, MEM_WIDTH)), const((1, MEM_WIDTH))],
        out_shape=[piece, piece, sq, sq, pltpu.HBM((1, MEM_WIDTH), F32)],
        compiler_params=_params(32, ("arbitrary",)),
    )(*_hbm(proj, proj, dycat, om, mk, mv, qg4, bd))


def _mem_kv_bwd(dmk, dmv, kraw, mem, gain, kg4, wkv_bf, hm_bf, bd):
    def body(dmk_ref, dmv_ref, kraw_ref, mem_ref, g_ref, kg_ref, w_ref, hm_ref, bd_ref, dw_ref, dg_ref, dkg_ref):
        bdv = bd_ref[...]
        kraw = kraw_ref[...]
        rk = lax.rsqrt(_split_dot(kraw * kraw, bdv) * (1.0 / HEAD_DIM) + EPS)
        kh = kraw * rk
        dmkv = dmk_ref[...]
        a = dmkv * kg_ref[...]
        dkraw = rk * (a - kh * (_split_dot(a * kh, bdv) * (1.0 / HEAD_DIM)))
        dkg_ref[...] = jnp.sum(dmkv * kh, axis=0, keepdims=True)
        dkv = jnp.concatenate([dkraw, dmv_ref[...]], axis=1).astype(BF16)
        dw = _tn(hm_ref[...], dkv).astype(BF16)
        rows_blk = D_MODEL // N_DEV
        for j in range(N_DEV):
            dw_ref[j] = dw[rows_blk * j:rows_blk * (j + 1)]
        dhm = _nt(dkv, w_ref[...])
        mv_ = mem_ref[...]
        r = lax.rsqrt(jnp.mean(mv_ * mv_, axis=-1, keepdims=True) + EPS)
        dg_ref[...] = jnp.sum(dhm * (mv_ * r), axis=0, keepdims=True)

    return pl.pallas_call(
        body, name="mem_kv_bwd",
        out_shape=[jax.ShapeDtypeStruct((N_DEV, D_MODEL // N_DEV, 2 * MEM_WIDTH), BF16),
                   jax.ShapeDtypeStruct((1, D_MODEL), F32), jax.ShapeDtypeStruct((1, MEM_WIDTH), F32)],
        compiler_params=_params(24),
    )(dmk, dmv, kraw, mem, gain, kg4, wkv_bf, hm_bf, bd)


def _out_loss(yg, ya, ym, x, tgt, wout_bf):
    S, D = x.shape
    tm = 256

    nsteps = S // tm
    rows_blk = D // N_DEV

    def body(yg_ref, ya_ref, ym_ref, x_ref, t_ref, w_ref, dout_ref, dycat_ref, dw_ref, loss_ref, acc_ref):
        i = pl.program_id(0)

        @pl.when(i == 0)
        def _():
            acc_ref[...] = jnp.zeros_like(acc_ref)
            loss_ref[...] = jnp.zeros_like(loss_ref)

        ycat = jnp.concatenate([yg_ref[...], ya_ref[...], ym_ref[...]], axis=1)
        w = w_ref[...]
        diff = (x_ref[...] + jnp.dot(ycat, w, preferred_element_type=F32)) - t_ref[...]
        loss_ref[...] += jnp.sum(diff * diff, axis=0, keepdims=True)
        dout = diff * (1.0 / D)
        dout_ref[...] = dout
        db = dout.astype(BF16)
        dycat_ref[...] = _nt(db, w)
        acc_ref[...] += _tn(ycat, db)

        @pl.when(i == nsteps - 1)
        def _():
            for j in range(N_DEV):
                dw_ref[j] = acc_ref[rows_blk * j:rows_blk * (j + 1), :].astype(BF16)

    tile = lambda w: pl.BlockSpec((tm, w), lambda i: (i, 0))
    const = lambda shape: pl.BlockSpec(shape, lambda i: (0,) * len(shape))
    return pl.pallas_call(
        body, name="out_loss", grid=(nsteps,),
        in_specs=[tile(GMLP_WIDTH), tile(ATTN_WIDTH), tile(MEM_WIDTH), tile(D), tile(D), const((D, D))],
        out_specs=[tile(D), tile(D), const((N_DEV, rows_blk, D)), const((1, D))],
        out_shape=[pltpu.HBM((S, D), F32), pltpu.HBM((S, D), F32),
                   pltpu.HBM((N_DEV, rows_blk, D), BF16), pltpu.HBM((1, D), F32)],
        scratch_shapes=[pltpu.VMEM((D, D), F32)],
        compiler_params=_params(40, ("arbitrary",)),
    )(*_hbm(yg, ya, ym, x, tgt, wout_bf))


def _piece_specs(pieces, tm):
    return [pl.BlockSpec((tm, p.shape[1]), lambda i: (i, 0)) for p in pieces]


def _in_bwd_dx(pieces, x, dout, gain, w_t, dw_blocks):
    S, D = x.shape
    N = w_t.shape[0]
    tm = 256
    n = len(pieces)
    nsteps = S // tm
    middle_step = nsteps // 8

    def body(*refs):
        piece_refs = refs[:n]
        x_ref, dout_ref, g_ref, w_ref, dwb_ref, gx_ref, dg_ref, gw_ref = refs[n:n + 8]
        rs = _ReduceScatter([dwb_ref], [gw_ref], *refs[n + 8:])
        i = pl.program_id(0)

        @pl.when(i == 0)
        def _():
            dg_ref[...] = jnp.zeros_like(dg_ref)
            rs.start()

        @pl.when(i == middle_step)
        def _():
            rs.middle()

        dproj = jnp.concatenate([r[...] for r in piece_refs], axis=1)
        dh = jnp.dot(dproj, w_ref[...], preferred_element_type=F32)
        xv = x_ref[...]
        r = lax.rsqrt(jnp.mean(xv * xv, axis=-1, keepdims=True) + EPS)
        xh = xv * r
        a = dh * g_ref[...]
        gx_ref[...] = dout_ref[...] + r * (a - xh * jnp.mean(a * xh, axis=-1, keepdims=True))
        dg_ref[...] += jnp.sum(dh * xh, axis=0, keepdims=True)

        @pl.when(i == nsteps - 1)
        def _():
            rs.finish()

    tile = pl.BlockSpec((tm, D), lambda i: (i, 0))
    const = lambda shape: pl.BlockSpec(shape, lambda i: (0,) * len(shape))
    vmem = pl.BlockSpec(memory_space=pltpu.VMEM)
    return pl.pallas_call(
        body, name="in_bwd_dx", grid=(nsteps,),
        in_specs=_piece_specs(pieces, tm) + [tile, tile, const((1, D)), const((N, D)), vmem],
        out_specs=[tile, const((1, D)), vmem],
        out_shape=[pltpu.HBM((S, D), F32), pltpu.HBM((1, D), F32), jax.ShapeDtypeStruct(dw_blocks.shape[1:], F32)],
        scratch_shapes=_reduce_scatter_scratch([dw_blocks]),
        compiler_params=_params(56, ("arbitrary",)),
    )(*_hbm(*pieces, x, dout, gain, w_t), dw_blocks)


def _in_bwd_dw(pieces, h_bf, others):
    S, D = h_bf.shape
    N = sum(p.shape[1] for p in pieces)
    n_blk = N // N_DEV
    tm = 512
    n = len(pieces)
    k = len(others)
    nsteps = S // tm

    def body(*refs):
        piece_refs = refs[:n]
        h_ref = refs[n]
        other_refs = refs[n + 1:n + 1 + k]
        dw_ref = refs[n + 1 + k]
        sum_refs = refs[n + 2 + k:n + 2 + 2 * k]
        acc_ref = refs[n + 2 + 2 * k]
        rs = _ReduceScatter(other_refs, sum_refs, *refs[n + 3 + 2 * k:])
        i = pl.program_id(0)

        @pl.when(i == 0)
        def _():
            acc_ref[...] = jnp.zeros_like(acc_ref)
            rs.start()

        @pl.when(i == 1)
        def _():
            rs.middle()

        dproj = jnp.concatenate([r[...] for r in piece_refs], axis=1)
        acc_ref[...] += _tn(h_ref[...], dproj)

        @pl.when(i == nsteps - 1)
        def _():
            for j in range(N_DEV):
                dw_ref[j] = acc_ref[:, n_blk * j:n_blk * (j + 1)].T.astype(BF16)
            rs.finish()

    vmem = pl.BlockSpec(memory_space=pltpu.VMEM)
    return pl.pallas_call(
        body, name="in_bwd_dw", grid=(nsteps,),
        in_specs=_piece_specs(pieces, tm) + [pl.BlockSpec((tm, D), lambda i: (i, 0))] + [vmem] * k,
        out_specs=[pl.BlockSpec((N_DEV, n_blk, D), lambda i: (0, 0, 0))] + [vmem] * k,
        out_shape=[pltpu.HBM((N_DEV, n_blk, D), BF16)] + [jax.ShapeDtypeStruct(o.shape[1:], F32) for o in others],
        scratch_shapes=[pltpu.VMEM((D, N), F32)] + _reduce_scatter_scratch(others),
        compiler_params=_params(56, ("arbitrary",)),
    )(*_hbm(*pieces, h_bf), *others)


def _row_step(m):
    return max(t for t in range(16, 257, 16) if m % t == 0)


def _place():
    x, y, c = lax.axis_index("x"), lax.axis_index("y"), lax.axis_index("c")
    chips = [(1 - x, y), (x, 1 - y), (1 - x, 1 - y)]
    return x, y, c, chips


class _AllGather:
    def __init__(self, srcs, outs, send_sems, recv_sems, local_sems, first_sem=0):
        self.srcs, self.outs, self.n, self.first_sem = srcs, outs, len(srcs), first_sem
        self.send_sems, self.recv_sems, self.local_sems = send_sems, recv_sems, local_sems

    def _rows(self, a, px, py, pc):
        m = self.srcs[a].shape[0]
        return self.outs[a].at[pl.ds((4 * px + 2 * py + pc) * m, m), :]

    def _copy(self, a, k, block, to, src=None):
        row = self.first_sem + a
        return pltpu.make_async_remote_copy(
            src_ref=self._rows(a, *block) if src is None else src, dst_ref=self._rows(a, *block),
            send_sem=self.send_sems.at[row, k], recv_sem=self.recv_sems.at[row, k], device_id=to, device_id_type=MESH)

    def _mine(self):
        x, y, c, _ = _place()
        return [pltpu.make_async_copy(self.srcs[a], self._rows(a, x, y, c), self.local_sems.at[self.first_sem + a])
                for a in range(self.n)]

    def _first(self, far):
        x, y, c, chips = _place()
        out = []
        for a in range(self.n):
            if far:
                out.append(self._copy(a, 3, (x, y, c), (*chips[2], c), src=self.srcs[a]))
            else:
                out.append(self._copy(a, 0, (x, y, c), (x, y, 1 - c), src=self.srcs[a]))
                out += [self._copy(a, 1 + j, (x, y, c), (*chips[j], c), src=self.srcs[a]) for j in (1, 0)]
        return out

    def _passed(self, j):
        x, y, c, chips = _place()
        return [self._copy(a, 4 + j, (*chips[j], c), (x, y, 1 - c)) for a in range(self.n)]

    def start(self):
        for cp in self._mine() + self._first(far=False):
            cp.start()

    def start_far(self):
        for cp in self._first(far=True):
            cp.start()

    def from_chip(self, j):
        x, y, c, chips = _place()
        for a in range(self.n):
            self._copy(a, 1 + j, (*chips[j], c), (x, y, c)).wait_recv()
        for cp in self._passed(j):
            cp.start()

    def from_sibling(self, j=None):
        x, y, c, chips = _place()
        for a in range(self.n):
            block = (x, y, 1 - c) if j is None else (*chips[j], 1 - c)
            self._copy(a, 0 if j is None else 4 + j, block, (x, y, c)).wait_recv()

    def from_self(self):
        for cp in self._mine():
            cp.wait()

    def finish(self):
        for cp in (self._first(far=False) + self._first(far=True)
                   + self._passed(0) + self._passed(1) + self._passed(2)):
            cp.wait_send()

    def run(self):
        self.start()
        self.start_far()
        self.from_self()
        for j in range(3):
            self.from_chip(j)
        self.from_sibling()
        for j in range(3):
            self.from_sibling(j)
        self.finish()


def _gather_proj(x, gain, shards, xpos):
    S, D = x.shape
    n = len(shards)
    N = N_DEV * shards[0].shape[0]
    half = N // 2
    tm = 512
    nsteps = S // tm

    def body(*refs):
        xpos_ref, x_ref, g_ref = refs[:3]
        ins = refs[3:3 + n]
        proj_ref, h_ref = refs[3 + n:5 + n]
        outs = refs[5 + n:5 + 2 * n]
        casts = refs[5 + 2 * n:5 + 3 * n]
        whole = refs[5 + 3 * n:5 + 4 * n]
        sems = refs[5 + 4 * n:8 + 4 * n]
        ag = _AllGather(casts[:1], whole[:1], *sems)
        later = _AllGather(casts[1:], whole[1:], *sems, first_sem=1)
        out_sems, h_all = refs[8 + 4 * n:]
        p, i = pl.program_id(0), pl.program_id(1)
        rows = pl.ds(pl.multiple_of(i * tm, tm), tm)

        @pl.when((p == 0) & (i == 0))
        def _():
            for a in range(n):
                tr = _row_step(ins[a].shape[0])

                def cast(r, carry, a=a, tr=tr):
                    at = pl.ds(pl.multiple_of(r * tr, tr), tr)
                    casts[a][at, :] = ins[a][at, :].astype(BF16)
                    return carry
                lax.fori_loop(0, ins[a].shape[0] // tr, cast, 0)
            ag.start()

        @pl.when(p == 0)
        def _():
            xv = x_ref[...]
            r = lax.rsqrt(jnp.mean(xv * xv, axis=-1, keepdims=True) + EPS)
            h = ((xv * r) * g_ref[...]).astype(BF16)
            h_ref[...] = h
            h_all[rows, :] = h

        @pl.when((p == 1) & (i == 0))
        def _():
            ag.from_self()
            ag.from_chip(1)
            ag.from_sibling()
            ag.from_sibling(1)
            ag.start_far()
            later.start()
            later.start_far()

        @pl.when((p == 2) & (i == 0))
        def _():
            for j in (0, 2):
                ag.from_chip(j)
            for j in (0, 2):
                ag.from_sibling(j)

        @pl.when(p > 0)
        def _():
            which = (xpos_ref[0] + p - 1) % 2
            w_half = whole[0][pl.ds(pl.multiple_of(which * half, half), half), :]
            proj_ref[...] = _nt(h_all[rows, :], w_half)

        @pl.when((p == 2) & (i == nsteps - 1))
        def _():
            ag.finish()
            later.from_self()
            for j in range(3):
                later.from_chip(j)
            later.from_sibling()
            for j in range(3):
                later.from_sibling(j)
            later.finish()
            to_results = [pltpu.make_async_copy(whole[a], outs[a], out_sems.at[a]) for a in range(n)]
            for cp in to_results:
                cp.start()
            for cp in to_results:
                cp.wait()

    vmem = pl.BlockSpec(memory_space=pltpu.VMEM)
    hbm = pl.BlockSpec(memory_space=pl.ANY)
    gathered = [(N_DEV * a.shape[0], a.shape[1]) for a in shards]
    x_tile = lambda p, i, xp: (jnp.where(p == 0, i, nsteps - 1), 0)
    proj_tile = lambda p, i, xp: (jnp.where(p == 0, 0, i), (xp[0] + jnp.maximum(p - 1, 0)) % 2)
    grid_spec = pltpu.PrefetchScalarGridSpec(
        num_scalar_prefetch=1, grid=(3, nsteps),
        in_specs=[pl.BlockSpec((tm, D), x_tile), pl.BlockSpec((1, D), lambda p, i, xp: (0, 0))] + [vmem] * n,
        out_specs=[pl.BlockSpec((tm, half), proj_tile), pl.BlockSpec((tm, D), x_tile)] + [hbm] * n,
        scratch_shapes=[pltpu.VMEM(a.shape, BF16) for a in shards] + [pltpu.VMEM(g, BF16) for g in gathered]
        + [pltpu.SemaphoreType.DMA((n, 7)), pltpu.SemaphoreType.DMA((n, 7)), pltpu.SemaphoreType.DMA((n,)),
           pltpu.SemaphoreType.DMA((n,)), pltpu.VMEM((S, D), BF16)])
    return pl.pallas_call(
        body, name="gather_proj", grid_spec=grid_spec,
        out_shape=[pltpu.HBM((S, N), F32), pltpu.HBM((S, D), BF16)] + [pltpu.HBM(g, BF16) for g in gathered],
        compiler_params=_params(56, ("arbitrary", "arbitrary")),
    )(xpos, *_hbm(x, gain), *shards)


ROW_NORM, ROW_MEM_NORM, ROW_V_GAIN, ROW_B, ROW_ATTN_GAINS, ROW_MEM_GAINS, ROW_W_S, ROW_LOSS = 0, 8, 16, 18, 22, 23, 24, 536
SMALL_ROWS = 544


def _gather_small(dgain, dmgain, dvg, db2, dqg, dkg, dmqg, dmkg, dws, sq):
    def body(dgain_ref, dmgain_ref, dvg_ref, db2_ref, dqg_ref, dkg_ref, dmqg_ref, dmkg_ref, dws_ref, sq_ref,
             out_ref, mine, send_sems, recv_sems, local_sems):
        first = lax.broadcasted_iota(jnp.int32, (1, 128), 1) < HEAD_DIM
        for i in range(8):
            cols = slice(128 * i, 128 * (i + 1))
            mine[ROW_NORM + i:ROW_NORM + i + 1, :] = dgain_ref[:, cols]
            mine[ROW_MEM_NORM + i:ROW_MEM_NORM + i + 1, :] = dmgain_ref[:, cols]
            mine[ROW_LOSS + i:ROW_LOSS + i + 1, :] = sq_ref[:, cols]
        mine[ROW_V_GAIN:ROW_V_GAIN + 1, :] = dvg_ref[:, 0:128]
        mine[ROW_V_GAIN + 1:ROW_V_GAIN + 2, :] = dvg_ref[:, 128:256]
        bt = db2_ref[...].T
        for h in range(4):
            mine[ROW_B + h:ROW_B + h + 1, :] = bt[HEAD_DIM * h:HEAD_DIM * h + 1, :]

        def fold_heads(t):
            return t + pltpu.roll(t, HEAD_DIM, axis=1)
        aq = fold_heads(dqg_ref[0] + dqg_ref[1] + dqg_ref[2] + dqg_ref[3])
        ak = fold_heads(dkg_ref[0] + dkg_ref[1] + dkg_ref[2] + dkg_ref[3])
        mine[ROW_ATTN_GAINS:ROW_ATTN_GAINS + 1, :] = jnp.where(first, aq, ak)
        mq = fold_heads(dmqg_ref[:, 0:128] + dmqg_ref[:, 128:256])
        mk = fold_heads(dmkg_ref[:, 0:128] + dmkg_ref[:, 128:256])
        mine[ROW_MEM_GAINS:ROW_MEM_GAINS + 1, :] = jnp.where(first, mq, mk)
        mine[ROW_W_S:ROW_W_S + 4 * CHUNK, :] = dws_ref[...]
        _AllGather([mine], [out_ref], send_sems, recv_sems, local_sems).run()

    return pl.pallas_call(
        body, name="gather_small_grads",
        out_shape=jax.ShapeDtypeStruct((N_DEV * SMALL_ROWS, 128), F32),
        scratch_shapes=[pltpu.VMEM((SMALL_ROWS, 128), F32), pltpu.SemaphoreType.DMA((1, 7)),
                        pltpu.SemaphoreType.DMA((1, 7)), pltpu.SemaphoreType.DMA((1,))],
        compiler_params=_params(16),
    )(dgain, dmgain, dvg, db2, dqg, dkg, dmqg, dmkg, dws, sq)


def _reduce_scatter_scratch(arrs):
    n = len(arrs)
    return ([pltpu.VMEM((4,) + a.shape[1:], BF16) for a in arrs] + [pltpu.VMEM((3,) + a.shape[1:], BF16) for a in arrs]
            + [pltpu.SemaphoreType.DMA((n, 7)), pltpu.SemaphoreType.DMA((n, 7))])


class _ReduceScatter:
    def __init__(self, ins, outs, *scratch):
        n = len(ins)
        self.n, self.ins, self.outs = n, ins, outs
        self.half, self.quarter = scratch[:n], scratch[n:2 * n]
        self.send_sems, self.recv_sems = scratch[2 * n:]

    def _to_sibling(self):
        x, y, c, _ = _place()
        return [pltpu.make_async_remote_copy(
            src_ref=self.ins[a].at[2 * q + (1 - c)], dst_ref=self.half[a].at[q], send_sem=self.send_sems.at[a, q],
            recv_sem=self.recv_sems.at[a, q], device_id=(x, y, 1 - c), device_id_type=MESH)
            for a in range(self.n) for q in range(4)]

    def _to_chips(self):
        _, _, c, chips = _place()
        return [pltpu.make_async_remote_copy(
            src_ref=self.half[a].at[2 * chip[0] + chip[1]], dst_ref=self.quarter[a].at[k],
            send_sem=self.send_sems.at[a, 4 + k], recv_sem=self.recv_sems.at[a, 4 + k], device_id=(*chip, c),
            device_id_type=MESH) for a in range(self.n) for k, chip in enumerate(chips)]

    def _rows(self, a, fn):
        m = self.ins[a].shape[1]
        tr = _row_step(m)

        def step(i, carry):
            fn(pl.ds(pl.multiple_of(i * tr, tr), tr))
            return carry
        lax.fori_loop(0, m // tr, step, 0)

    def start(self):
        for cp in self._to_sibling():
            cp.start()

    def middle(self):
        _, _, c, _ = _place()
        for cp in self._to_sibling():
            cp.wait_recv()
        for a in range(self.n):
            for q in range(4):
                def add_half(rows, a=a, q=q):
                    both = self.ins[a][2 * q + c, rows, :].astype(F32) + self.half[a][q, rows, :].astype(F32)
                    self.half[a][q, rows, :] = both.astype(BF16)
                self._rows(a, add_half)
        for cp in self._to_chips():
            cp.start()

    def finish(self):
        x, y, _, _ = _place()
        for cp in self._to_chips():
            cp.wait_recv()
        for a in range(self.n):
            def add_quarters(rows, a=a):
                f = lambda t: t.astype(F32)
                self.outs[a][rows, :] = ((f(self.half[a][2 * x + y, rows, :]) + f(self.quarter[a][0, rows, :]))
                                         + (f(self.quarter[a][1, rows, :]) + f(self.quarter[a][2, rows, :])))
            self._rows(a, add_quarters)
        for cp in self._to_sibling() + self._to_chips():
            cp.wait_send()


def _adamw_math(w, g, m, v):
    m = ADAM_B1 * m + (1.0 - ADAM_B1) * g
    v = ADAM_B2 * v + (1.0 - ADAM_B2) * (g * g)
    m_hat = m / (1.0 - ADAM_B1 ** ADAM_STEP)
    v_hat = v / (1.0 - ADAM_B2 ** ADAM_STEP)
    delta = -ADAM_LR * (m_hat / (jnp.sqrt(v_hat) + ADAM_EPS) + ADAM_WD * w)
    return delta, m, v


def _adamw(w, g, m, v, name):
    R, C = w.shape
    tr = _row_step(R)

    def body(w_ref, g_ref, m_ref, v_ref, d_ref, nm_ref, nv_ref):
        d_ref[...], nm_ref[...], nv_ref[...] = _adamw_math(w_ref[...], g_ref[...], m_ref[...], v_ref[...])

    tile = pl.BlockSpec((tr, C), lambda i: (i, 0))
    out = pltpu.HBM((R, C), F32)
    return pl.pallas_call(
        body, name=name, grid=(R // tr,), in_specs=[tile] * 4, out_specs=[tile] * 3, out_shape=[out] * 3,
        compiler_params=_params(16, ("arbitrary",)),
    )(*_hbm(w, g, m, v))


SMALL = ("norm_gain", "gmlp_v_gain", "gmlp_w_s", "gmlp_b", "attn_q_gain", "attn_k_gain", "mem_norm_gain",
         "mem_q_gain", "mem_k_gain")
WEIGHTS = ("norm_gain", "w_in", "gmlp_v_gain", "gmlp_w_s", "gmlp_b", "attn_q_gain", "attn_k_gain",
           "mem_norm_gain", "w_mem_kv", "mem_q_gain", "mem_k_gain", "w_out")


def _adamw_small(w, m, v, g_all):
    k = len(SMALL)
    half = slice(0, HEAD_DIM), slice(HEAD_DIM, 2 * HEAD_DIM)

    def body(*refs):
        w_refs, m_refs, v_refs = refs[:k], refs[k:2 * k], refs[2 * k:3 * k]
        g_ref = refs[3 * k]
        outs = refs[3 * k + 1:7 * k + 1]
        loss_ref, gsum = refs[7 * k + 1:]

        part = SMALL_ROWS // 4
        for p in range(4):
            acc = g_ref[part * p:part * (p + 1), :]
            for dev in range(1, N_DEV):
                acc = acc + g_ref[dev * SMALL_ROWS + part * p:dev * SMALL_ROWS + part * (p + 1), :]
            gsum[part * p:part * (p + 1), :] = acc

        def update(name, at, g):
            i = SMALL.index(name)
            d, nm, nv = _adamw_math(w_refs[i][at], g, m_refs[i][at], v_refs[i][at])
            outs[i][at], outs[k + i][at], outs[2 * k + i][at], outs[3 * k + i][at] = g, d, nm, nv

        for i in range(8):
            at = (slice(0, 1), slice(128 * i, 128 * (i + 1)))
            update("norm_gain", at, gsum[ROW_NORM + i:ROW_NORM + i + 1, :])
            update("mem_norm_gain", at, gsum[ROW_MEM_NORM + i:ROW_MEM_NORM + i + 1, :])
        for h in range(4):
            row = (0, slice(h, h + 1), slice(None))
            update("gmlp_v_gain", row, gsum[ROW_V_GAIN + h // 2:ROW_V_GAIN + h // 2 + 1, half[h % 2]])
            update("gmlp_b", row, gsum[ROW_B + h:ROW_B + h + 1, :])
            update("gmlp_w_s", (0, h), gsum[ROW_W_S + CHUNK * h:ROW_W_S + CHUNK * (h + 1), :])
        whole = (slice(0, 1), slice(None))
        update("attn_q_gain", whole, gsum[ROW_ATTN_GAINS:ROW_ATTN_GAINS + 1, half[0]])
        update("attn_k_gain", whole, gsum[ROW_ATTN_GAINS:ROW_ATTN_GAINS + 1, half[1]])
        update("mem_q_gain", whole, gsum[ROW_MEM_GAINS:ROW_MEM_GAINS + 1, half[0]])
        update("mem_k_gain", whole, gsum[ROW_MEM_GAINS:ROW_MEM_GAINS + 1, half[1]])
        loss_ref[...] = jnp.sum(gsum[ROW_LOSS:ROW_LOSS + 8, :], keepdims=True) * (0.5 / D_MODEL)

    shapes = [jax.ShapeDtypeStruct(w[name].shape, F32) for name in SMALL]
    res = pl.pallas_call(
        body, name="adamw_small",
        out_shape=shapes * 4 + [jax.ShapeDtypeStruct((1, 1), F32)],
        scratch_shapes=[pltpu.VMEM((SMALL_ROWS, 128), F32)],
        compiler_params=_params(16),
    )(*[w[n] for n in SMALL], *[m[n] for n in SMALL], *[v[n] for n in SMALL], g_all)
    trees = [dict(zip(SMALL, res[j * k:(j + 1) * k])) for j in range(4)]
    return (*trees, res[4 * k])


def _grads(x, mem, tgt, w, shards):
    bd128, bd256 = _head_blockdiag(128), _head_blockdiag(256)
    gain = w["norm_gain"].reshape(1, D_MODEL)
    vg = w["gmlp_v_gain"].reshape(1, GMLP_WIDTH)
    w_s = w["gmlp_w_s"].reshape(4, CHUNK, CHUNK)
    b2 = jnp.repeat(w["gmlp_b"].reshape(4, CHUNK).T, HEAD_DIM, axis=1)
    qg2 = jnp.tile(w["attn_q_gain"].reshape(1, HEAD_DIM), (1, 2))
    kg2 = jnp.tile(w["attn_k_gain"].reshape(1, HEAD_DIM), (1, 2))
    mqg4 = jnp.tile(w["mem_q_gain"].reshape(1, HEAD_DIM), (1, 4))
    mkg4 = jnp.tile(w["mem_k_gain"].reshape(1, HEAD_DIM), (1, 4))
    mgain = w["mem_norm_gain"].reshape(1, D_MODEL)

    xpos = lax.axis_index("x").astype(jnp.int32).reshape(1)
    proj, h_bf, win_t, wkv_bf, wout_bf = _gather_proj(x, gain, shards, xpos)
    yg = _gmlp_fwd(proj, vg, w_s, b2, bd256)
    ya, att, lse = _attn_fwd(proj, qg2, kg2, bd128)
    hm_bf, kraw, mk, mv = _mem_kv(mem, mgain, wkv_bf, mkg4, bd256)
    ym, om = _mem_fwd(proj, mk, mv, mqg4, bd256)
    dout, dycat, dwout, sq = _out_loss(yg, ya, ym, x, tgt, wout_bf)

    du, dgv, dgg, dws, db2, dvg = _gmlp_bwd(proj, dycat, vg, w_s, b2, bd256)
    dq, dk, dv, dag, dqg, dkg = _attn_bwd(proj, dycat, att, lse, qg2, kg2, bd128)
    dmq, dmg, dmk, dmv, dmqg = _mem_bwd(proj, dycat, om, mk, mv, mqg4, bd256)
    dwkv, dmgain, dmkg = _mem_kv_bwd(dmk, dmv, kraw, mem, mgain, mkg4, wkv_bf, hm_bf, bd256)
    pieces = [du, dgv, dgg, dq, dk, dv, dag, dmq, dmg]
    dwin, g_wkv, g_wout = _in_bwd_dw(pieces, h_bf, [dwkv, dwout])
    grad_x, dgain, g_win = _in_bwd_dx(pieces, x, dout, gain, win_t, dwin)
    return grad_x, g_win, g_wkv, g_wout, (dgain, dmgain, dvg, db2, dqg, dkg, dmqg, dmkg, dws, sq)


def kernel(x, mem, norm_gain, w_in, gmlp_v_gain, gmlp_w_s, gmlp_b, attn_q_gain, attn_k_gain, mem_norm_gain, w_mem_kv, mem_q_gain, mem_k_gain, w_out, loss_target, m_norm_gain, m_w_in, m_gmlp_v_gain, m_gmlp_w_s, m_gmlp_b, m_attn_q_gain, m_attn_k_gain, m_mem_norm_gain, m_w_mem_kv, m_mem_q_gain, m_mem_k_gain, m_w_out, v_norm_gain, v_w_in, v_gmlp_v_gain, v_gmlp_w_s, v_gmlp_b, v_attn_q_gain, v_attn_k_gain, v_mem_norm_gain, v_w_mem_kv, v_mem_q_gain, v_mem_k_gain, v_w_out):
    w = dict(norm_gain=norm_gain, w_in=w_in, gmlp_v_gain=gmlp_v_gain, gmlp_w_s=gmlp_w_s, gmlp_b=gmlp_b,
             attn_q_gain=attn_q_gain, attn_k_gain=attn_k_gain, mem_norm_gain=mem_norm_gain, w_mem_kv=w_mem_kv,
             mem_q_gain=mem_q_gain, mem_k_gain=mem_k_gain, w_out=w_out)
    m = dict(norm_gain=m_norm_gain, w_in=m_w_in, gmlp_v_gain=m_gmlp_v_gain, gmlp_w_s=m_gmlp_w_s, gmlp_b=m_gmlp_b,
             attn_q_gain=m_attn_q_gain, attn_k_gain=m_attn_k_gain, mem_norm_gain=m_mem_norm_gain,
             w_mem_kv=m_w_mem_kv, mem_q_gain=m_mem_q_gain, mem_k_gain=m_mem_k_gain, w_out=m_w_out)
    v = dict(norm_gain=v_norm_gain, w_in=v_w_in, gmlp_v_gain=v_gmlp_v_gain, gmlp_w_s=v_gmlp_w_s, gmlp_b=v_gmlp_b,
             attn_q_gain=v_attn_q_gain, attn_k_gain=v_attn_k_gain, mem_norm_gain=v_mem_norm_gain,
             w_mem_kv=v_w_mem_kv, mem_q_gain=v_mem_q_gain, mem_k_gain=v_mem_k_gain, w_out=v_w_out)
    transposed = lambda t: jnp.transpose(t[0])

    grad_x, g_win, g_wkv, g_wout, small = _grads(
        x[0], mem[0], loss_target[0], w, [transposed(w_in), w_mem_kv[0], w_out[0]])
    small_all = _gather_small(*small)

    out_g, out_d, out_m, out_v, loss = _adamw_small(w, m, v, small_all)
    d_, m_, v_ = _adamw(transposed(w_in), g_win, transposed(m_w_in), transposed(v_w_in), "adamw_w_in")
    for tree, t in ((out_g, g_win), (out_d, d_), (out_m, m_), (out_v, v_)):
        tree["w_in"] = jnp.transpose(t)[None]
    for name, g in (("w_mem_kv", g_wkv), ("w_out", g_wout)):
        d_, m_, v_ = _adamw(w[name][0], g, m[name][0], v[name][0], "adamw_" + name)
        out_g[name], out_d[name], out_m[name], out_v[name] = g[None], d_[None], m_[None], v_[None]

    return (loss.reshape(()), grad_x[None], *[out_g[k] for k in WEIGHTS], *[out_d[k] for k in WEIGHTS],
            *[out_m[k] for k in WEIGHTS], *[out_v[k] for k in WEIGHTS])
```

```python
import functools
import math

import jax
import jax.numpy as jnp
from jax import lax
from jax.experimental import pallas as pl
from jax.experimental.pallas import tpu as pltpu

F32 = jnp.float32
BF16 = jnp.bfloat16

N_DEV = 8
D_MODEL = 1024
HEAD_DIM = 64
GMLP_WIDTH = 256
ATTN_WIDTH = 512
MEM_WIDTH = 256
MEM_LEN = 256
IN_WIDTH = 3 * GMLP_WIDTH + 4 * ATTN_WIDTH + 2 * MEM_WIDTH
CHUNK = 128
BLOCK = 128
DILATIONS = (1, 4, 16)
EPS = 1e-6
SCALE = 1.0 / math.sqrt(HEAD_DIM)
NEG = -1e30

ADAM_LR = 0.001
ADAM_B1 = 0.9
ADAM_B2 = 0.999
ADAM_EPS = 1e-08
ADAM_WD = 0.01
ADAM_STEP = 10

MIB = 1024 * 1024
MESH = pl.DeviceIdType.MESH

COL_AQ, COL_AK, COL_AV, COL_AG = 6, 10, 14, 18


def _params(vmem_mib, semantics=None):
    kw = dict(vmem_limit_bytes=vmem_mib * MIB)
    if semantics is not None:
        kw["dimension_semantics"] = semantics
    return pltpu.CompilerParams(**kw)


def _hbm(*arrs):
    return [pltpu.with_memory_space_constraint(a, pltpu.HBM) for a in arrs]


def _split_dot(x, sel_bf):
    hi = x.astype(BF16)
    lo = (x - hi.astype(F32)).astype(BF16)
    return jnp.dot(hi, sel_bf, preferred_element_type=F32) + jnp.dot(lo, sel_bf, preferred_element_type=F32)


def _nt(a, b):
    return lax.dot_general(a, b, (((1,), (1,)), ((), ())), preferred_element_type=F32)


def _tn(a, b):
    return lax.dot_general(a, b, (((0,), (0,)), ((), ())), preferred_element_type=F32)


def _silu_parts(g):
    sg = jax.nn.sigmoid(g)
    return g * sg, sg * (1.0 + g * (1.0 - sg))


def _head_index(shape):
    return lax.shift_right_logical(lax.broadcasted_iota(jnp.int32, shape, 1), HEAD_DIM.bit_length() - 1)


def _head_blockdiag(width):
    i = jnp.arange(width) // HEAD_DIM
    return (i[:, None] == i[None, :]).astype(BF16)


def _gmlp_masked_weights(ws_ref, transpose):
    t = lax.broadcasted_iota(jnp.int32, (CHUNK, CHUNK), 0)
    s = lax.broadcasted_iota(jnp.int32, (CHUNK, CHUNK), 1)
    parts = []
    for h in range(4):
        wm = jnp.where(s <= t, ws_ref[h], 0.0)
        parts.append(wm.T if transpose else wm)
    return jnp.concatenate(parts, axis=1).astype(BF16)


def _head_stack(v, head):
    return jnp.concatenate([jnp.where(head == h, v, 0.0) for h in range(4)], axis=0).astype(BF16)


def _gmlp_fwd(proj, vg, w_s, b2, bd):
    S = proj.shape[0]
    tm = 512

    def body(u_ref, v_ref, g_ref, vg_ref, ws_ref, b2_ref, bd_ref, y_ref):
        v = v_ref[...]
        ms = _split_dot(v * v, bd_ref[...]) * (1.0 / HEAD_DIM)
        vn = (v * lax.rsqrt(ms + EPS)) * vg_ref[...]
        wcat = _gmlp_masked_weights(ws_ref, False)
        head = _head_index((CHUNK, GMLP_WIDTH))
        for c in range(tm // CHUNK):
            rows = slice(c * CHUNK, (c + 1) * CHUNK)
            sp = jnp.dot(wcat, _head_stack(vn[rows], head), preferred_element_type=F32) + b2_ref[...]
            silu, _ = _silu_parts(g_ref[rows, :])
            y_ref[rows, :] = ((u_ref[rows, :] * sp) * silu).astype(BF16)

    col = lambda j: pl.BlockSpec((tm, GMLP_WIDTH), lambda i, j=j: (i, j))
    const = lambda shape: pl.BlockSpec(shape, lambda i: (0,) * len(shape))
    return pl.pallas_call(
        body, name="gmlp_fwd", grid=(S // tm,),
        in_specs=[col(0), col(1), col(2), const((1, GMLP_WIDTH)), const((4, CHUNK, CHUNK)),
                  const((CHUNK, GMLP_WIDTH)), const((GMLP_WIDTH, GMLP_WIDTH))],
        out_specs=pl.BlockSpec((tm, GMLP_WIDTH), lambda i: (i, 0)),
        out_shape=pltpu.HBM((S, GMLP_WIDTH), BF16),
        compiler_params=_params(24, ("arbitrary",)),
    )(*_hbm(proj, proj, proj, vg, w_s, b2, bd))


def _gmlp_bwd(proj, dycat, vg, w_s, b2, bd):
    S = proj.shape[0]
    tm = 512
    nsteps = S // tm

    def body(u_ref, v_ref, g_ref, dy_ref, vg_ref, ws_ref, b2_ref, bd_ref,
             du_ref, dv_ref, dg_ref, dws_ref, db2_ref, dvg_ref):
        i = pl.program_id(0)

        @pl.when(i == 0)
        def _():
            dws_ref[...] = jnp.zeros_like(dws_ref)
            db2_ref[...] = jnp.zeros_like(db2_ref)
            dvg_ref[...] = jnp.zeros_like(dvg_ref)

        bdv = bd_ref[...]
        v = v_ref[...]
        ms = _split_dot(v * v, bdv) * (1.0 / HEAD_DIM)
        rv = lax.rsqrt(ms + EPS)
        xhat = v * rv
        vgv = vg_ref[...]
        vn = xhat * vgv
        wcat = _gmlp_masked_weights(ws_ref, False)
        wcat_t = _gmlp_masked_weights(ws_ref, True)
        head = _head_index((CHUNK, GMLP_WIDTH))
        dvg = jnp.zeros((1, GMLP_WIDTH), F32)
        for c in range(tm // CHUNK):
            rows = slice(c * CHUNK, (c + 1) * CHUNK)
            vn_c = vn[rows]
            spb = jnp.dot(wcat, _head_stack(vn_c, head), preferred_element_type=F32) + b2_ref[...]
            silu, dsilu = _silu_parts(g_ref[rows, :])
            dy = dy_ref[rows, :]
            u = u_ref[rows, :]
            du_ref[rows, :] = (dy * spb * silu).astype(BF16)
            dg_ref[rows, :] = (dy * u * spb * dsilu).astype(BF16)
            dsp = dy * u * silu
            db2_ref[...] += dsp
            dstack = _head_stack(dsp, head)
            dvn = jnp.dot(wcat_t, dstack, preferred_element_type=F32)
            dws_ref[...] += _nt(dstack, vn_c.astype(BF16))
            xh = xhat[rows]
            a = dvn * vgv
            mean_ax = _split_dot(a * xh, bdv) * (1.0 / HEAD_DIM)
            dv_ref[rows, :] = (rv[rows] * (a - xh * mean_ax)).astype(BF16)
            dvg = dvg + jnp.sum(dvn * xh, axis=0, keepdims=True)
        dvg_ref[...] += dvg

        @pl.when(i == nsteps - 1)
        def _():
            t = lax.broadcasted_iota(jnp.int32, (4 * CHUNK, CHUNK), 0) % CHUNK
            s = lax.broadcasted_iota(jnp.int32, (4 * CHUNK, CHUNK), 1)
            dws_ref[...] = jnp.where(s <= t, dws_ref[...], 0.0)
            db2_ref[...] = _split_dot(db2_ref[...], bdv)

    col = lambda j: pl.BlockSpec((tm, GMLP_WIDTH), lambda i, j=j: (i, j))
    const = lambda shape: pl.BlockSpec(shape, lambda i: (0,) * len(shape))
    tile = pl.BlockSpec((tm, GMLP_WIDTH), lambda i: (i, 0))
    piece = pltpu.HBM((S, GMLP_WIDTH), BF16)
    return pl.pallas_call(
        body, name="gmlp_bwd", grid=(nsteps,),
        in_specs=[col(0), col(1), col(2), col(0), const((1, GMLP_WIDTH)), const((4, CHUNK, CHUNK)),
                  const((CHUNK, GMLP_WIDTH)), const((GMLP_WIDTH, GMLP_WIDTH))],
        out_specs=[tile, tile, tile, const((4 * CHUNK, CHUNK)), const((CHUNK, GMLP_WIDTH)), const((1, GMLP_WIDTH))],
        out_shape=[piece, piece, piece, pltpu.HBM((4 * CHUNK, CHUNK), F32),
                   pltpu.HBM((CHUNK, GMLP_WIDTH), F32), pltpu.HBM((1, GMLP_WIDTH), F32)],
        compiler_params=_params(32, ("arbitrary",)),
    )(*_hbm(proj, proj, proj, dycat, vg, w_s, b2, bd))


def _band_mask():
    qi = lax.broadcasted_iota(jnp.int32, (2 * BLOCK, 2 * BLOCK), 0) % BLOCK
    ki = lax.broadcasted_iota(jnp.int32, (2 * BLOCK, 2 * BLOCK), 1)
    return ((ki < BLOCK) & (ki >= qi)) | ((ki >= BLOCK) & ((ki - BLOCK) <= qi))


def _first_block_bias(blk, blocks_per_class):
    kcol = lax.broadcasted_iota(jnp.int32, (1, 2 * BLOCK), 1)
    kill = jnp.where((blk & (blocks_per_class - 1)) == 0, NEG, 0.0)
    return jnp.where(kcol < BLOCK, kill, 0.0)


def _two_heads(q, lo):
    zero = jnp.zeros_like(q)
    return jnp.concatenate([jnp.where(lo, q, zero), jnp.where(lo, zero, q)], axis=0)


def _block_tokens(blk, d, S):
    if d == 1:
        return pl.ds(pl.multiple_of(blk * BLOCK, BLOCK), BLOCK)
    blocks_per_class = S // d // BLOCK
    r = lax.shift_right_logical(blk, blocks_per_class.bit_length() - 1)
    n = blk & (blocks_per_class - 1)
    return pl.ds(r + n * (BLOCK * d), BLOCK, stride=d)


def _padded_block(blk):
    return pl.ds(pl.multiple_of((blk + 1) * BLOCK, BLOCK), BLOCK)


def _for_blocks(n_blocks, unroll, fn):
    def group(g, carry):
        for u in range(unroll):
            fn(g * unroll + u)
        return carry
    lax.fori_loop(0, n_blocks // unroll, group, 0)


def _attn_fwd(proj, qg2, kg2, bd):
    S = proj.shape[0]
    npairs = ATTN_WIDTH // 128
    tn = 512

    def body(q_ref, k_ref, v_ref, g_ref, qg_ref, kg_ref, bd_ref, y_ref, att_ref, lse_ref, qn, kn, kc, vc):
        bdv = bd_ref[...]
        lo = lax.broadcasted_iota(jnp.int32, (BLOCK, 128), 1) < HEAD_DIM
        band_mask = _band_mask()
        kc[pl.ds(0, BLOCK), :] = jnp.zeros((BLOCK, 128), BF16)
        vc[pl.ds(0, BLOCK), :] = jnp.zeros((BLOCK, 128), BF16)

        def norm_step(i, carry):
            rows = pl.ds(pl.multiple_of(i * tn, tn), tn)
            qv = q_ref[rows, :]
            kv = k_ref[rows, :]
            qn[rows, :] = (qv * lax.rsqrt(_split_dot(qv * qv, bdv) * (1.0 / HEAD_DIM) + EPS)) * (qg_ref[...] * SCALE)
            kn[rows, :] = (kv * lax.rsqrt(_split_dot(kv * kv, bdv) * (1.0 / HEAD_DIM) + EPS)) * kg_ref[...]
            return carry
        lax.fori_loop(0, S // tn, norm_step, 0)

        def fill(blk, d):
            tokens = _block_tokens(blk, d, S)
            kc[_padded_block(blk), :] = kn[tokens, :].astype(BF16)
            vc[_padded_block(blk), :] = v_ref[tokens, :].astype(BF16)

        def block(blk, d):
            tokens = _block_tokens(blk, d, S)
            keys = pl.ds(pl.multiple_of(blk * BLOCK, BLOCK), 2 * BLOCK)
            q2 = _two_heads(qn[tokens, :].astype(BF16), lo)
            s = jnp.where(band_mask, _nt(q2, kc[keys, :]), NEG) + _first_block_bias(blk, S // d // BLOCK)
            m = jnp.max(s, axis=-1, keepdims=True)
            e = jnp.exp(s - m)
            l = jnp.sum(e, axis=-1, keepdims=True)
            o2 = jnp.dot(e.astype(BF16), vc[keys, :], preferred_element_type=F32) * (1.0 / l)
            lse2 = m + jnp.log(l)
            o = jnp.where(lo, o2[:BLOCK], o2[BLOCK:])
            lse = jnp.where(lo, lse2[:BLOCK], lse2[BLOCK:])
            if d > 1:
                la = lse_ref[tokens, :]
                mx = jnp.maximum(la, lse)
                wa, wb = jnp.exp(la - mx), jnp.exp(lse - mx)
                t = wa + wb
                o = (wa * att_ref[tokens, :] + wb * o) / t
                lse = mx + jnp.log(t)
            att_ref[tokens, :] = o
            lse_ref[tokens, :] = lse

        for d in DILATIONS:
            _for_blocks(S // BLOCK, 4, functools.partial(fill, d=d))
            _for_blocks(S // BLOCK, 8, functools.partial(block, d=d))

        def gate_step(i, carry):
            rows = pl.ds(pl.multiple_of(i * tn, tn), tn)
            silu, _ = _silu_parts(g_ref[rows, :])
            y_ref[rows, :] = (att_ref[rows, :] * silu).astype(BF16)
            return carry
        lax.fori_loop(0, S // tn, gate_step, 0)

    col = lambda j0: pl.BlockSpec((S, 128), lambda p, j0=j0: (0, j0 + p))
    const = lambda shape: pl.BlockSpec(shape, lambda p: (0,) * len(shape))
    out = pl.BlockSpec((S, 128), lambda p: (0, p))
    return pl.pallas_call(
        body, name="attn_fwd", grid=(npairs,),
        in_specs=[col(COL_AQ), col(COL_AK), col(COL_AV), col(COL_AG), const((1, 128)), const((1, 128)),
                  const((128, 128))],
        out_specs=[out, out, out],
        out_shape=[pltpu.HBM((S, ATTN_WIDTH), BF16), pltpu.HBM((S, ATTN_WIDTH), F32),
                   pltpu.HBM((S, ATTN_WIDTH), F32)],
        scratch_shapes=[pltpu.VMEM((S, 128), F32), pltpu.VMEM((S, 128), F32),
                        pltpu.VMEM((S + BLOCK, 128), BF16), pltpu.VMEM((S + BLOCK, 128), BF16)],
        compiler_params=_params(48, ("arbitrary",)),
    )(*_hbm(proj, proj, proj, proj, qg2, kg2, bd))


def _attn_bwd(proj, dycat, att, lse, qg2, kg2, bd):
    S = proj.shape[0]
    npairs = ATTN_WIDTH // 128
    tn = 512

    def body(q_ref, k_ref, v_ref, g_ref, dy_ref, att_ref, lse_ref, qg_ref, kg_ref, bd_ref,
             dq_ref, dk_ref, dv_ref, dg_ref, dqg_ref, dkg_ref,
             qn, kn, rq_s, rk_s, kc, vc, do_s, dd_s, dqa, dka, dva):
        bdv = bd_ref[...]
        lo = lax.broadcasted_iota(jnp.int32, (BLOCK, 128), 1) < HEAD_DIM
        kc[pl.ds(0, BLOCK), :] = jnp.zeros((BLOCK, 128), BF16)
        vc[pl.ds(0, BLOCK), :] = jnp.zeros((BLOCK, 128), BF16)

        def prepare(i, carry):
            rows = pl.ds(pl.multiple_of(i * tn, tn), tn)
            qv = q_ref[rows, :]
            kv = k_ref[rows, :]
            rq = lax.rsqrt(_split_dot(qv * qv, bdv) * (1.0 / HEAD_DIM) + EPS)
            rk = lax.rsqrt(_split_dot(kv * kv, bdv) * (1.0 / HEAD_DIM) + EPS)
            rq_s[rows, :] = rq
            rk_s[rows, :] = rk
            qn[rows, :] = (qv * rq) * (qg_ref[...] * SCALE)
            kn[rows, :] = (kv * rk) * kg_ref[...]
            silu, dsilu = _silu_parts(g_ref[rows, :])
            dy = dy_ref[rows, :]
            at = att_ref[rows, :]
            do = dy * silu
            do_s[rows, :] = do
            dd_s[rows, :] = _split_dot(do * at, bdv)
            dg_ref[rows, :] = (dy * at * dsilu).astype(BF16)
            dka[rows, :] = jnp.zeros((tn, 128), F32)
            dva[rows, :] = jnp.zeros((tn, 128), F32)
            return carry
        lax.fori_loop(0, S // tn, prepare, 0)

        kt = lax.broadcasted_iota(jnp.int32, (2 * BLOCK, 2 * BLOCK), 0)
        qt = lax.broadcasted_iota(jnp.int32, (2 * BLOCK, 2 * BLOCK), 1) % BLOCK
        band_mask_t = ((kt < BLOCK) & (kt >= qt)) | ((kt >= BLOCK) & ((kt - BLOCK) <= qt))

        def per_query_row(t):
            tt = t.T
            return jnp.concatenate([tt[0:1, :], tt[HEAD_DIM:HEAD_DIM + 1, :]], axis=1)

        def fill(blk, d):
            tokens = _block_tokens(blk, d, S)
            kc[_padded_block(blk), :] = kn[tokens, :].astype(BF16)
            vc[_padded_block(blk), :] = v_ref[tokens, :].astype(BF16)

        def block(blk, d):
            tokens = _block_tokens(blk, d, S)
            keys = pl.ds(pl.multiple_of(blk * BLOCK, BLOCK), 2 * BLOCK)
            first = (blk & (S // d // BLOCK - 1)) == 0
            q2 = _two_heads(qn[tokens, :].astype(BF16), lo)
            do2 = _two_heads(do_s[tokens, :].astype(BF16), lo)
            lse_row = per_query_row(lse_ref[tokens, :])
            dd_row = per_query_row(dd_s[tokens, :])
            kb = kc[keys, :]
            vb = vc[keys, :]
            st = jnp.where(band_mask_t, _nt(kb, q2), NEG)
            st = jnp.concatenate([st[:BLOCK] + jnp.where(first, NEG, 0.0), st[BLOCK:]], axis=0)
            pt = jnp.exp(st - lse_row)
            dst = pt * (_nt(vb, do2) - dd_row)
            ptb = pt.astype(BF16)
            dstb = dst.astype(BF16)
            dv_band = jnp.dot(ptb, do2, preferred_element_type=F32)
            dk_band = jnp.dot(dstb, q2, preferred_element_type=F32)
            before = _block_tokens(jnp.where(first, blk, blk - 1), d, S)
            dka[before, :] = dka[before, :] + dk_band[:BLOCK]
            dva[before, :] = dva[before, :] + dv_band[:BLOCK]
            dka[tokens, :] = dka[tokens, :] + dk_band[BLOCK:]
            dva[tokens, :] = dva[tokens, :] + dv_band[BLOCK:]
            dq2 = _tn(dstb, kb)
            dq = jnp.where(lo, dq2[:BLOCK], dq2[BLOCK:])
            dqa[tokens, :] = dq if d == 1 else dqa[tokens, :] + dq

        for d in DILATIONS:
            _for_blocks(S // BLOCK, 4, functools.partial(fill, d=d))
            _for_blocks(S // BLOCK, 8, functools.partial(block, d=d))

        def out_step(i, carry):
            dqg, dkg = carry
            rows = pl.ds(pl.multiple_of(i * tn, tn), tn)
            rq = rq_s[rows, :]
            rk = rk_s[rows, :]
            qh = q_ref[rows, :] * rq
            kh = k_ref[rows, :] * rk
            dqs = dqa[rows, :] * SCALE
            dkn = dka[rows, :]
            aq = dqs * qg_ref[...]
            ak = dkn * kg_ref[...]
            dq_ref[rows, :] = (rq * (aq - qh * (_split_dot(aq * qh, bdv) * (1.0 / HEAD_DIM)))).astype(BF16)
            dk_ref[rows, :] = (rk * (ak - kh * (_split_dot(ak * kh, bdv) * (1.0 / HEAD_DIM)))).astype(BF16)
            dv_ref[rows, :] = dva[rows, :].astype(BF16)
            dqg = dqg + jnp.sum(dqs * qh, axis=0, keepdims=True)
            dkg = dkg + jnp.sum(dkn * kh, axis=0, keepdims=True)
            return dqg, dkg
        zero = jnp.zeros((1, 128), F32)
        dqg, dkg = lax.fori_loop(0, S // tn, out_step, (zero, zero))
        dqg_ref[0] = dqg
        dkg_ref[0] = dkg

    col = lambda j0: pl.BlockSpec((S, 128), lambda p, j0=j0: (0, j0 + p))
    col1 = lambda j0: pl.BlockSpec((S, 128), lambda p, j0=j0: (0, j0 + p), pipeline_mode=pl.Buffered(1))
    const = lambda shape: pl.BlockSpec(shape, lambda p: (0,) * len(shape))
    out = pl.BlockSpec((S, 128), lambda p: (0, p))
    gain_out = pl.BlockSpec((1, 1, 128), lambda p: (p, 0, 0))
    piece = pltpu.HBM((S, ATTN_WIDTH), BF16)
    gains = pltpu.HBM((npairs, 1, 128), F32)
    f32buf = pltpu.VMEM((S, 128), F32)
    bf16pad = pltpu.VMEM((S + BLOCK, 128), BF16)
    return pl.pallas_call(
        body, name="attn_bwd", grid=(npairs,),
        in_specs=[col(COL_AQ), col(COL_AK), col(COL_AV), col1(COL_AG), col1(GMLP_WIDTH // 128), col1(0), col(0),
                  const((1, 128)), const((1, 128)), const((128, 128))],
        out_specs=[out, out, out, out, gain_out, gain_out],
        out_shape=[piece, piece, piece, piece, gains, gains],
        scratch_shapes=[f32buf, f32buf, f32buf, f32buf, bf16pad, bf16pad, f32buf, f32buf, f32buf, f32buf, f32buf],
        compiler_params=_params(60, ("arbitrary",)),
    )(*_hbm(proj, proj, proj, proj, dycat, att, lse, qg2, kg2, bd))


def _mem_kv(mem, gain, wkv_bf, kg4, bd):
    def body(mem_ref, g_ref, w_ref, kg_ref, bd_ref, hm_ref, kraw_ref, mk_ref, mv_ref):
        mv_ = mem_ref[...]
        r = lax.rsqrt(jnp.mean(mv_ * mv_, axis=-1, keepdims=True) + EPS)
        hm = ((mv_ * r) * g_ref[...]).astype(BF16)
        hm_ref[...] = hm
        kv = jnp.dot(hm, w_ref[...], preferred_element_type=F32)
        kraw = kv[:, :MEM_WIDTH]
        kraw_ref[...] = kraw
        ms = _split_dot(kraw * kraw, bd_ref[...]) * (1.0 / HEAD_DIM)
        mk_ref[...] = (kraw * lax.rsqrt(ms + EPS)) * kg_ref[...]
        mv_ref[...] = kv[:, MEM_WIDTH:]

    sq = jax.ShapeDtypeStruct((MEM_LEN, MEM_WIDTH), F32)
    return pl.pallas_call(
        body, name="mem_kv",
        out_shape=[jax.ShapeDtypeStruct((MEM_LEN, D_MODEL), BF16), sq, sq, sq],
        compiler_params=_params(16),
    )(mem, gain, wkv_bf, kg4, bd)


def _mem_fwd(proj, mk, mv, qg4, bd):
    S = proj.shape[0]
    tm = 512

    def body(q_ref, g_ref, mk_ref, mv_ref, qg_ref, bd_ref, y_ref, om_ref):
        qv = q_ref[...]
        ms = _split_dot(qv * qv, bd_ref[...]) * (1.0 / HEAD_DIM)
        qs = (qv * lax.rsqrt(ms + EPS)) * (qg_ref[...] * SCALE)
        mkb = mk_ref[...].astype(BF16)
        mvb = mv_ref[...].astype(BF16)
        head = _head_index((tm, MEM_WIDTH))
        o = jnp.zeros((tm, MEM_WIDTH), F32)
        for h in range(4):
            s = _nt(jnp.where(head == h, qs, 0.0).astype(BF16), mkb)
            e = jnp.exp(s - jnp.max(s, axis=-1, keepdims=True))
            p = e * (1.0 / jnp.sum(e, axis=-1, keepdims=True))
            o = jnp.where(head == h, jnp.dot(p.astype(BF16), mvb, preferred_element_type=F32), o)
        om_ref[...] = o
        silu, _ = _silu_parts(g_ref[...])
        y_ref[...] = (o * silu).astype(BF16)

    col = lambda j: pl.BlockSpec((tm, MEM_WIDTH), lambda i, j=j: (i, j))
    const = lambda shape: pl.BlockSpec(shape, lambda i: (0,) * len(shape))
    tile = pl.BlockSpec((tm, MEM_WIDTH), lambda i: (i, 0))
    return pl.pallas_call(
        body, name="mem_fwd", grid=(S // tm,),
        in_specs=[col(11), col(12), const((MEM_LEN, MEM_WIDTH)), const((MEM_LEN, MEM_WIDTH)), const((1, MEM_WIDTH)),
                  const((MEM_WIDTH, MEM_WIDTH))],
        out_specs=[tile, tile],
        out_shape=[pltpu.HBM((S, MEM_WIDTH), BF16), pltpu.HBM((S, MEM_WIDTH), F32)],
        compiler_params=_params(24, ("arbitrary",)),
    )(*_hbm(proj, proj, mk, mv, qg4, bd))


def _mem_bwd(proj, dycat, om, mk, mv, qg4, bd):
    S = proj.shape[0]
    tm = 512

    def body(q_ref, g_ref, dy_ref, om_ref, mk_ref, mv_ref, qg_ref, bd_ref,
             dq_ref, dg_ref, dmk_ref, dmv_ref, dqg_ref):
        i = pl.program_id(0)

        @pl.when(i == 0)
        def _():
            dmk_ref[...] = jnp.zeros_like(dmk_ref)
            dmv_ref[...] = jnp.zeros_like(dmv_ref)
            dqg_ref[...] = jnp.zeros_like(dqg_ref)

        bdv = bd_ref[...]
        qv = q_ref[...]
        rq = lax.rsqrt(_split_dot(qv * qv, bdv) * (1.0 / HEAD_DIM) + EPS)
        qh = qv * rq
        qs = qh * (qg_ref[...] * SCALE)
        silu, dsilu = _silu_parts(g_ref[...])
        dy = dy_ref[...]
        o = om_ref[...]
        do = dy * silu
        dg_ref[...] = (dy * o * dsilu).astype(BF16)
        dd = _split_dot(do * o, bdv)
        mkb = mk_ref[...].astype(BF16)
        mvb = mv_ref[...].astype(BF16)
        head = _head_index((tm, MEM_WIDTH))
        dqs = jnp.zeros((tm, MEM_WIDTH), F32)
        for h in range(4):
            qhd = jnp.where(head == h, qs, 0.0).astype(BF16)
            doh = jnp.where(head == h, do, 0.0).astype(BF16)
            s = _nt(qhd, mkb)
            e = jnp.exp(s - jnp.max(s, axis=-1, keepdims=True))
            p = e * (1.0 / jnp.sum(e, axis=-1, keepdims=True))
            ds = p * (_nt(doh, mvb) - dd[:, h * HEAD_DIM:h * HEAD_DIM + 1])
            dsb = ds.astype(BF16)
            dmv_ref[...] += _tn(p.astype(BF16), doh)
            dmk_ref[...] += _tn(dsb, qhd)
            dqs = jnp.where(head == h, jnp.dot(dsb, mkb, preferred_element_type=F32), dqs)
        dqs = dqs * SCALE
        a = dqs * qg_ref[...]
        dq_ref[...] = (rq * (a - qh * (_split_dot(a * qh, bdv) * (1.0 / HEAD_DIM)))).astype(BF16)
        dqg_ref[...] += jnp.sum(dqs * qh, axis=0, keepdims=True)

    col = lambda j: pl.BlockSpec((tm, MEM_WIDTH), lambda i, j=j: (i, j))
    const = lambda shape: pl.BlockSpec(shape, lambda i: (0,) * len(shape))
    tile = pl.BlockSpec((tm, MEM_WIDTH), lambda i: (i, 0))
    piece = pltpu.HBM((S, MEM_WIDTH), BF16)
    sq = pltpu.HBM((MEM_LEN, MEM_WIDTH), F32)
    return pl.pallas_call(
        body, name="mem_bwd", grid=(S // tm,),
        in_specs=[col(11), col(12), col(3), tile, const((MEM_LEN, MEM_WIDTH)), const((MEM_LEN, MEM_WIDTH)),
                  const((1, MEM_WIDTH)), const((MEM_WIDTH, MEM_WIDTH))],
        out_specs=[tile, tile, const((MEM_LEN, MEM_WIDTH)), const((MEM_LEN, MEM_WIDTH)), const((1, MEM_WIDTH))],
        out_shape=[piece, piece, sq, sq, pltpu.HBM((1, MEM_WIDTH), F32)],
        compiler_params=_params(32, ("arbitrary",)),
    )(*_hbm(proj, proj, dycat, om, mk, mv, qg4, bd))


def _mem_kv_bwd(dmk, dmv, kraw, mem, gain, kg4, wkv_bf, hm_bf, bd):
    def body(dmk_ref, dmv_ref, kraw_ref, mem_ref, g_ref, kg_ref, w_ref, hm_ref, bd_ref, dw_ref, dg_ref, dkg_ref):
        bdv = bd_ref[...]
        kraw = kraw_ref[...]
        rk = lax.rsqrt(_split_dot(kraw * kraw, bdv) * (1.0 / HEAD_DIM) + EPS)
        kh = kraw * rk
        dmkv = dmk_ref[...]
        a = dmkv * kg_ref[...]
        dkraw = rk * (a - kh * (_split_dot(a * kh, bdv) * (1.0 / HEAD_DIM)))
        dkg_ref[...] = jnp.sum(dmkv * kh, axis=0, keepdims=True)
        dkv = jnp.concatenate([dkraw, dmv_ref[...]], axis=1).astype(BF16)
        dw = _tn(hm_ref[...], dkv).astype(BF16)
        rows_blk = D_MODEL // N_DEV
        for j in range(N_DEV):
            dw_ref[j] = dw[rows_blk * j:rows_blk * (j + 1)]
        dhm = _nt(dkv, w_ref[...])
        mv_ = mem_ref[...]
        r = lax.rsqrt(jnp.mean(mv_ * mv_, axis=-1, keepdims=True) + EPS)
        dg_ref[...] = jnp.sum(dhm * (mv_ * r), axis=0, keepdims=True)

    return pl.pallas_call(
        body, name="mem_kv_bwd",
        out_shape=[jax.ShapeDtypeStruct((N_DEV, D_MODEL // N_DEV, 2 * MEM_WIDTH), BF16),
                   jax.ShapeDtypeStruct((1, D_MODEL), F32), jax.ShapeDtypeStruct((1, MEM_WIDTH), F32)],
        compiler_params=_params(24),
    )(dmk, dmv, kraw, mem, gain, kg4, wkv_bf, hm_bf, bd)


def _out_loss(yg, ya, ym, x, tgt, wout_bf):
    S, D = x.shape
    tm = 512
    nsteps = S // tm
    rows_blk = D // N_DEV

    def body(yg_ref, ya_ref, ym_ref, x_ref, t_ref, w_ref, dout_ref, dycat_ref, dw_ref, loss_ref, acc_ref):
        i = pl.program_id(0)

        @pl.when(i == 0)
        def _():
            acc_ref[...] = jnp.zeros_like(acc_ref)
            loss_ref[...] = jnp.zeros_like(loss_ref)

        ycat = jnp.concatenate([yg_ref[...], ya_ref[...], ym_ref[...]], axis=1)
        w = w_ref[...]
        diff = (x_ref[...] + jnp.dot(ycat, w, preferred_element_type=F32)) - t_ref[...]
        loss_ref[...] += jnp.sum(diff * diff, axis=0, keepdims=True)
        dout = diff * (1.0 / D)
        dout_ref[...] = dout
        db = dout.astype(BF16)
        dycat_ref[...] = _nt(db, w)
        acc_ref[...] += _tn(ycat, db)

        @pl.when(i == nsteps - 1)
        def _():
            for j in range(N_DEV):
                dw_ref[j] = acc_ref[rows_blk * j:rows_blk * (j + 1), :].astype(BF16)

    tile = lambda w: pl.BlockSpec((tm, w), lambda i: (i, 0))
    const = lambda shape: pl.BlockSpec(shape, lambda i: (0,) * len(shape))
    return pl.pallas_call(
        body, name="out_loss", grid=(nsteps,),
        in_specs=[tile(GMLP_WIDTH), tile(ATTN_WIDTH), tile(MEM_WIDTH), tile(D), tile(D), const((D, D))],
        out_specs=[tile(D), tile(D), const((N_DEV, rows_blk, D)), const((1, D))],
        out_shape=[pltpu.HBM((S, D), F32), pltpu.HBM((S, D), F32),
                   pltpu.HBM((N_DEV, rows_blk, D), BF16), pltpu.HBM((1, D), F32)],
        scratch_shapes=[pltpu.VMEM((D, D), F32)],
        compiler_params=_params(40, ("arbitrary",)),
    )(*_hbm(yg, ya, ym, x, tgt, wout_bf))


def _piece_specs(pieces, tm):
    return [pl.BlockSpec((tm, p.shape[1]), lambda i: (i, 0)) for p in pieces]


def _in_bwd_dx(pieces, x, dout, gain, w_t, dw_blocks):
    S, D = x.shape
    N = w_t.shape[0]
    tm = 512
    n = len(pieces)
    nsteps = S // tm
    middle_step = nsteps // 8

    def body(*refs):
        piece_refs = refs[:n]
        x_ref, dout_ref, g_ref, w_ref, dwb_ref, gx_ref, dg_ref, gw_ref = refs[n:n + 8]
        rs = _ReduceScatter([dwb_ref], [gw_ref], *refs[n + 8:])
        i = pl.program_id(0)

        @pl.when(i == 0)
        def _():
            dg_ref[...] = jnp.zeros_like(dg_ref)
            rs.start()

        @pl.when(i == middle_step)
        def _():
            rs.middle()

        dproj = jnp.concatenate([r[...] for r in piece_refs], axis=1)
        dh = jnp.dot(dproj, w_ref[...], preferred_element_type=F32)
        xv = x_ref[...]
        r = lax.rsqrt(jnp.mean(xv * xv, axis=-1, keepdims=True) + EPS)
        xh = xv * r
        a = dh * g_ref[...]
        gx_ref[...] = dout_ref[...] + r * (a - xh * jnp.mean(a * xh, axis=-1, keepdims=True))
        dg_ref[...] += jnp.sum(dh * xh, axis=0, keepdims=True)

        @pl.when(i == nsteps - 1)
        def _():
            rs.finish()

    tile = pl.BlockSpec((tm, D), lambda i: (i, 0))
    const = lambda shape: pl.BlockSpec(shape, lambda i: (0,) * len(shape))
    vmem = pl.BlockSpec(memory_space=pltpu.VMEM)
    return pl.pallas_call(
        body, name="in_bwd_dx", grid=(nsteps,),
        in_specs=_piece_specs(pieces, tm)
        + [tile, tile, const((1, D)), pl.BlockSpec((N, D), lambda i: (0, 0), pipeline_mode=pl.Buffered(1)), vmem],
        out_specs=[tile, const((1, D)), vmem],
        out_shape=[pltpu.HBM((S, D), F32), pltpu.HBM((1, D), F32), jax.ShapeDtypeStruct(dw_blocks.shape[1:], F32)],
        scratch_shapes=_reduce_scatter_scratch([dw_blocks]),
        compiler_params=_params(56, ("arbitrary",)),
    )(*_hbm(*pieces, x, dout, gain, w_t), dw_blocks)


def _in_bwd_dw(pieces, h_bf, others):
    S, D = h_bf.shape
    N = sum(p.shape[1] for p in pieces)
    n_blk = N // N_DEV
    tm = 512
    n = len(pieces)
    k = len(others)
    nsteps = S // tm

    def body(*refs):
        piece_refs = refs[:n]
        h_ref = refs[n]
        other_refs = refs[n + 1:n + 1 + k]
        dw_ref = refs[n + 1 + k]
        sum_refs = refs[n + 2 + k:n + 2 + 2 * k]
        acc_ref = refs[n + 2 + 2 * k]
        rs = _ReduceScatter(other_refs, sum_refs, *refs[n + 3 + 2 * k:])
        i = pl.program_id(0)

        @pl.when(i == 0)
        def _():
            acc_ref[...] = jnp.zeros_like(acc_ref)
            rs.start()

        @pl.when(i == 1)
        def _():
            rs.middle()

        dproj = jnp.concatenate([r[...] for r in piece_refs], axis=1)
        acc_ref[...] += _tn(h_ref[...], dproj)

        @pl.when(i == nsteps - 1)
        def _():
            for j in range(N_DEV):
                dw_ref[j] = acc_ref[:, n_blk * j:n_blk * (j + 1)].T.astype(BF16)
            rs.finish()

    vmem = pl.BlockSpec(memory_space=pltpu.VMEM)
    return pl.pallas_call(
        body, name="in_bwd_dw", grid=(nsteps,),
        in_specs=_piece_specs(pieces, tm) + [pl.BlockSpec((tm, D), lambda i: (i, 0))] + [vmem] * k,
        out_specs=[pl.BlockSpec((N_DEV, n_blk, D), lambda i: (0, 0, 0))] + [vmem] * k,
        out_shape=[pltpu.HBM((N_DEV, n_blk, D), BF16)] + [jax.ShapeDtypeStruct(o.shape[1:], F32) for o in others],
        scratch_shapes=[pltpu.VMEM((D, N), F32)] + _reduce_scatter_scratch(others),
        compiler_params=_params(56, ("arbitrary",)),
    )(*_hbm(*pieces, h_bf), *others)


def _row_step(m):
    return max(t for t in range(16, 257, 16) if m % t == 0)


def _place():
    x, y, c = lax.axis_index("x"), lax.axis_index("y"), lax.axis_index("c")
    chips = [(1 - x, y), (x, 1 - y), (1 - x, 1 - y)]
    return x, y, c, chips


class _AllGather:
    def __init__(self, srcs, outs, send_sems, recv_sems, local_sems, first_sem=0):
        self.srcs, self.outs, self.n, self.first_sem = srcs, outs, len(srcs), first_sem
        self.send_sems, self.recv_sems, self.local_sems = send_sems, recv_sems, local_sems

    def _rows(self, a, px, py, pc):
        m = self.srcs[a].shape[0]
        return self.outs[a].at[pl.ds((4 * px + 2 * py + pc) * m, m), :]

    def _copy(self, a, k, block, to, src=None):
        row = self.first_sem + a
        return pltpu.make_async_remote_copy(
            src_ref=self._rows(a, *block) if src is None else src, dst_ref=self._rows(a, *block),
            send_sem=self.send_sems.at[row, k], recv_sem=self.recv_sems.at[row, k], device_id=to, device_id_type=MESH)

    def _mine(self):
        x, y, c, _ = _place()
        return [pltpu.make_async_copy(self.srcs[a], self._rows(a, x, y, c), self.local_sems.at[self.first_sem + a])
                for a in range(self.n)]

    def _first(self, far):
        x, y, c, chips = _place()
        out = []
        for a in range(self.n):
            if far:
                out.append(self._copy(a, 3, (x, y, c), (*chips[2], c), src=self.srcs[a]))
            else:
                out.append(self._copy(a, 0, (x, y, c), (x, y, 1 - c), src=self.srcs[a]))
                out += [self._copy(a, 1 + j, (x, y, c), (*chips[j], c), src=self.srcs[a]) for j in (1, 0)]
        return out

    def _passed(self, j):
        x, y, c, chips = _place()
        return [self._copy(a, 4 + j, (*chips[j], c), (x, y, 1 - c)) for a in range(self.n)]

    def start(self):
        for cp in self._mine() + self._first(far=False):
            cp.start()

    def start_far(self):
        for cp in self._first(far=True):
            cp.start()

    def from_chip(self, j):
        x, y, c, chips = _place()
        for a in range(self.n):
            self._copy(a, 1 + j, (*chips[j], c), (x, y, c)).wait_recv()
        for cp in self._passed(j):
            cp.start()

    def from_sibling(self, j=None):
        x, y, c, chips = _place()
        for a in range(self.n):
            block = (x, y, 1 - c) if j is None else (*chips[j], 1 - c)
            self._copy(a, 0 if j is None else 4 + j, block, (x, y, c)).wait_recv()

    def from_self(self):
        for cp in self._mine():
            cp.wait()

    def finish(self):
        for cp in (self._first(far=False) + self._first(far=True)
                   + self._passed(0) + self._passed(1) + self._passed(2)):
            cp.wait_send()

    def run(self):
        self.start()
        self.start_far()
        self.from_self()
        for j in range(3):
            self.from_chip(j)
        self.from_sibling()
        for j in range(3):
            self.from_sibling(j)
        self.finish()


def _gather_proj(x, gain, shards, xpos):
    S, D = x.shape
    n = len(shards)
    N = N_DEV * shards[0].shape[0]
    half = N // 2
    tm = 512
    nsteps = S // tm

    def body(*refs):
        xpos_ref, x_ref, g_ref = refs[:3]
        ins = refs[3:3 + n]
        proj_ref, h_ref = refs[3 + n:5 + n]
        outs = refs[5 + n:5 + 2 * n]
        casts = refs[5 + 2 * n:5 + 3 * n]
        whole = refs[5 + 3 * n:5 + 4 * n]
        sems = refs[5 + 4 * n:8 + 4 * n]
        ag = _AllGather(casts[:1], whole[:1], *sems)
        later = _AllGather(casts[1:], whole[1:], *sems, first_sem=1)
        out_sems, h_all = refs[8 + 4 * n:]
        p, i = pl.program_id(0), pl.program_id(1)
        rows = pl.ds(pl.multiple_of(i * tm, tm), tm)

        @pl.when((p == 0) & (i == 0))
        def _():
            for a in range(n):
                tr = _row_step(ins[a].shape[0])

                def cast(r, carry, a=a, tr=tr):
                    at = pl.ds(pl.multiple_of(r * tr, tr), tr)
                    casts[a][at, :] = ins[a][at, :].astype(BF16)
                    return carry
                lax.fori_loop(0, ins[a].shape[0] // tr, cast, 0)
            ag.start()

        @pl.when(p == 0)
        def _():
            xv = x_ref[...]
            r = lax.rsqrt(jnp.mean(xv * xv, axis=-1, keepdims=True) + EPS)
            h = ((xv * r) * g_ref[...]).astype(BF16)
            h_ref[...] = h
            h_all[rows, :] = h

        @pl.when((p == 1) & (i == 0))
        def _():
            ag.from_self()
            ag.from_chip(1)
            ag.from_sibling()
            ag.from_sibling(1)
            ag.start_far()
            later.start()
            later.start_far()

        @pl.when((p == 2) & (i == 0))
        def _():
            for j in (0, 2):
                ag.from_chip(j)
            for j in (0, 2):
                ag.from_sibling(j)

        @pl.when(p > 0)
        def _():
            which = (xpos_ref[0] + p - 1) % 2
            w_half = whole[0][pl.ds(pl.multiple_of(which * half, half), half), :]
            proj_ref[...] = _nt(h_all[rows, :], w_half)

        @pl.when((p == 2) & (i == nsteps - 1))
        def _():
            ag.finish()
            later.from_self()
            for j in range(3):
                later.from_chip(j)
            later.from_sibling()
            for j in range(3):
                later.from_sibling(j)
            later.finish()
            to_results = [pltpu.make_async_copy(whole[a], outs[a], out_sems.at[a]) for a in range(n)]
            for cp in to_results:
                cp.start()
            for cp in to_results:
                cp.wait()

    vmem = pl.BlockSpec(memory_space=pltpu.VMEM)
    hbm = pl.BlockSpec(memory_space=pl.ANY)
    gathered = [(N_DEV * a.shape[0], a.shape[1]) for a in shards]
    x_tile = lambda p, i, xp: (jnp.where(p == 0, i, nsteps - 1), 0)
    proj_tile = lambda p, i, xp: (jnp.where(p == 0, 0, i), (xp[0] + jnp.maximum(p - 1, 0)) % 2)
    grid_spec = pltpu.PrefetchScalarGridSpec(
        num_scalar_prefetch=1, grid=(3, nsteps),
        in_specs=[pl.BlockSpec((tm, D), x_tile), pl.BlockSpec((1, D), lambda p, i, xp: (0, 0))] + [vmem] * n,
        out_specs=[pl.BlockSpec((tm, half), proj_tile), pl.BlockSpec((tm, D), x_tile)] + [hbm] * n,
        scratch_shapes=[pltpu.VMEM(a.shape, BF16) for a in shards] + [pltpu.VMEM(g, BF16) for g in gathered]
        + [pltpu.SemaphoreType.DMA((n, 7)), pltpu.SemaphoreType.DMA((n, 7)), pltpu.SemaphoreType.DMA((n,)),
           pltpu.SemaphoreType.DMA((n,)), pltpu.VMEM((S, D), BF16)])
    return pl.pallas_call(
        body, name="gather_proj", grid_spec=grid_spec,
        out_shape=[pltpu.HBM((S, N), F32), pltpu.HBM((S, D), BF16)] + [pltpu.HBM(g, BF16) for g in gathered],
        compiler_params=_params(56, ("arbitrary", "arbitrary")),
    )(xpos, *_hbm(x, gain), *shards)


ROW_NORM, ROW_MEM_NORM, ROW_V_GAIN, ROW_B, ROW_ATTN_GAINS, ROW_MEM_GAINS, ROW_W_S, ROW_LOSS = 0, 8, 16, 18, 22, 23, 24, 536
SMALL_ROWS = 544


def _gather_small(dgain, dmgain, dvg, db2, dqg, dkg, dmqg, dmkg, dws, sq):
    def body(dgain_ref, dmgain_ref, dvg_ref, db2_ref, dqg_ref, dkg_ref, dmqg_ref, dmkg_ref, dws_ref, sq_ref,
             out_ref, mine, send_sems, recv_sems, local_sems):
        first = lax.broadcasted_iota(jnp.int32, (1, 128), 1) < HEAD_DIM
        for i in range(8):
            cols = slice(128 * i, 128 * (i + 1))
            mine[ROW_NORM + i:ROW_NORM + i + 1, :] = dgain_ref[:, cols]
            mine[ROW_MEM_NORM + i:ROW_MEM_NORM + i + 1, :] = dmgain_ref[:, cols]
            mine[ROW_LOSS + i:ROW_LOSS + i + 1, :] = sq_ref[:, cols]
        mine[ROW_V_GAIN:ROW_V_GAIN + 1, :] = dvg_ref[:, 0:128]
        mine[ROW_V_GAIN + 1:ROW_V_GAIN + 2, :] = dvg_ref[:, 128:256]
        bt = db2_ref[...].T
        for h in range(4):
            mine[ROW_B + h:ROW_B + h + 1, :] = bt[HEAD_DIM * h:HEAD_DIM * h + 1, :]

        def fold_heads(t):
            return t + pltpu.roll(t, HEAD_DIM, axis=1)
        aq = fold_heads(dqg_ref[0] + dqg_ref[1] + dqg_ref[2] + dqg_ref[3])
        ak = fold_heads(dkg_ref[0] + dkg_ref[1] + dkg_ref[2] + dkg_ref[3])
        mine[ROW_ATTN_GAINS:ROW_ATTN_GAINS + 1, :] = jnp.where(first, aq, ak)
        mq = fold_heads(dmqg_ref[:, 0:128] + dmqg_ref[:, 128:256])
        mk = fold_heads(dmkg_ref[:, 0:128] + dmkg_ref[:, 128:256])
        mine[ROW_MEM_GAINS:ROW_MEM_GAINS + 1, :] = jnp.where(first, mq, mk)
        mine[ROW_W_S:ROW_W_S + 4 * CHUNK, :] = dws_ref[...]
        _AllGather([mine], [out_ref], send_sems, recv_sems, local_sems).run()

    return pl.pallas_call(
        body, name="gather_small_grads",
        out_shape=jax.ShapeDtypeStruct((N_DEV * SMALL_ROWS, 128), F32),
        scratch_shapes=[pltpu.VMEM((SMALL_ROWS, 128), F32), pltpu.SemaphoreType.DMA((1, 7)),
                        pltpu.SemaphoreType.DMA((1, 7)), pltpu.SemaphoreType.DMA((1,))],
        compiler_params=_params(16),
    )(dgain, dmgain, dvg, db2, dqg, dkg, dmqg, dmkg, dws, sq)


def _reduce_scatter_scratch(arrs):
    n = len(arrs)
    return ([pltpu.VMEM((4,) + a.shape[1:], BF16) for a in arrs] + [pltpu.VMEM((3,) + a.shape[1:], BF16) for a in arrs]
            + [pltpu.SemaphoreType.DMA((n, 7)), pltpu.SemaphoreType.DMA((n, 7))])


class _ReduceScatter:
    def __init__(self, ins, outs, *scratch):
        n = len(ins)
        self.n, self.ins, self.outs = n, ins, outs
        self.half, self.quarter = scratch[:n], scratch[n:2 * n]
        self.send_sems, self.recv_sems = scratch[2 * n:]

    def _to_sibling(self):
        x, y, c, _ = _place()
        return [pltpu.make_async_remote_copy(
            src_ref=self.ins[a].at[2 * q + (1 - c)], dst_ref=self.half[a].at[q], send_sem=self.send_sems.at[a, q],
            recv_sem=self.recv_sems.at[a, q], device_id=(x, y, 1 - c), device_id_type=MESH)
            for a in range(self.n) for q in range(4)]

    def _to_chips(self):
        _, _, c, chips = _place()
        return [pltpu.make_async_remote_copy(
            src_ref=self.half[a].at[2 * chip[0] + chip[1]], dst_ref=self.quarter[a].at[k],
            send_sem=self.send_sems.at[a, 4 + k], recv_sem=self.recv_sems.at[a, 4 + k], device_id=(*chip, c),
            device_id_type=MESH) for a in range(self.n) for k, chip in enumerate(chips)]

    def _rows(self, a, fn):
        m = self.ins[a].shape[1]
        tr = _row_step(m)

        def step(i, carry):
            fn(pl.ds(pl.multiple_of(i * tr, tr), tr))
            return carry
        lax.fori_loop(0, m // tr, step, 0)

    def start(self):
        for cp in self._to_sibling():
            cp.start()

    def middle(self):
        _, _, c, _ = _place()
        for cp in self._to_sibling():
            cp.wait_recv()
        for a in range(self.n):
            for q in range(4):
                def add_half(rows, a=a, q=q):
                    both = self.ins[a][2 * q + c, rows, :].astype(F32) + self.half[a][q, rows, :].astype(F32)
                    self.half[a][q, rows, :] = both.astype(BF16)
                self._rows(a, add_half)
        for cp in self._to_chips():
            cp.start()

    def finish(self):
        x, y, _, _ = _place()
        for cp in self._to_chips():
            cp.wait_recv()
        for a in range(self.n):
            def add_quarters(rows, a=a):
                f = lambda t: t.astype(F32)
                self.outs[a][rows, :] = ((f(self.half[a][2 * x + y, rows, :]) + f(self.quarter[a][0, rows, :]))
                                         + (f(self.quarter[a][1, rows, :]) + f(self.quarter[a][2, rows, :])))
            self._rows(a, add_quarters)
        for cp in self._to_sibling() + self._to_chips():
            cp.wait_send()


def _adamw_math(w, g, m, v):
    m = ADAM_B1 * m + (1.0 - ADAM_B1) * g
    v = ADAM_B2 * v + (1.0 - ADAM_B2) * (g * g)
    m_hat = m / (1.0 - ADAM_B1 ** ADAM_STEP)
    v_hat = v / (1.0 - ADAM_B2 ** ADAM_STEP)
    delta = -ADAM_LR * (m_hat / (jnp.sqrt(v_hat) + ADAM_EPS) + ADAM_WD * w)
    return delta, m, v


def _adamw(w, g, m, v, name):
    R, C = w.shape
    tr = _row_step(R)

    def body(w_ref, g_ref, m_ref, v_ref, d_ref, nm_ref, nv_ref):
        d_ref[...], nm_ref[...], nv_ref[...] = _adamw_math(w_ref[...], g_ref[...], m_ref[...], v_ref[...])

    tile = pl.BlockSpec((tr, C), lambda i: (i, 0))
    out = pltpu.HBM((R, C), F32)
    return pl.pallas_call(
        body, name=name, grid=(R // tr,), in_specs=[tile] * 4, out_specs=[tile] * 3, out_shape=[out] * 3,
        compiler_params=_params(16, ("arbitrary",)),
    )(*_hbm(w, g, m, v))


SMALL = ("norm_gain", "gmlp_v_gain", "gmlp_w_s", "gmlp_b", "attn_q_gain", "attn_k_gain", "mem_norm_gain",
         "mem_q_gain", "mem_k_gain")
WEIGHTS = ("norm_gain", "w_in", "gmlp_v_gain", "gmlp_w_s", "gmlp_b", "attn_q_gain", "attn_k_gain",
           "mem_norm_gain", "w_mem_kv", "mem_q_gain", "mem_k_gain", "w_out")


def _adamw_small(w, m, v, g_all):
    k = len(SMALL)
    half = slice(0, HEAD_DIM), slice(HEAD_DIM, 2 * HEAD_DIM)

    def body(*refs):
        w_refs, m_refs, v_refs = refs[:k], refs[k:2 * k], refs[2 * k:3 * k]
        g_ref = refs[3 * k]
        outs = refs[3 * k + 1:7 * k + 1]
        loss_ref, gsum = refs[7 * k + 1:]

        part = SMALL_ROWS // 4
        for p in range(4):
            acc = g_ref[part * p:part * (p + 1), :]
            for dev in range(1, N_DEV):
                acc = acc + g_ref[dev * SMALL_ROWS + part * p:dev * SMALL_ROWS + part * (p + 1), :]
            gsum[part * p:part * (p + 1), :] = acc

        def update(name, at, g):
            i = SMALL.index(name)
            d, nm, nv = _adamw_math(w_refs[i][at], g, m_refs[i][at], v_refs[i][at])
            outs[i][at], outs[k + i][at], outs[2 * k + i][at], outs[3 * k + i][at] = g, d, nm, nv

        for i in range(8):
            at = (slice(0, 1), slice(128 * i, 128 * (i + 1)))
            update("norm_gain", at, gsum[ROW_NORM + i:ROW_NORM + i + 1, :])
            update("mem_norm_gain", at, gsum[ROW_MEM_NORM + i:ROW_MEM_NORM + i + 1, :])
        for h in range(4):
            row = (0, slice(h, h + 1), slice(None))
            update("gmlp_v_gain", row, gsum[ROW_V_GAIN + h // 2:ROW_V_GAIN + h // 2 + 1, half[h % 2]])
            update("gmlp_b", row, gsum[ROW_B + h:ROW_B + h + 1, :])
            update("gmlp_w_s", (0, h), gsum[ROW_W_S + CHUNK * h:ROW_W_S + CHUNK * (h + 1), :])
        whole = (slice(0, 1), slice(None))
        update("attn_q_gain", whole, gsum[ROW_ATTN_GAINS:ROW_ATTN_GAINS + 1, half[0]])
        update("attn_k_gain", whole, gsum[ROW_ATTN_GAINS:ROW_ATTN_GAINS + 1, half[1]])
        update("mem_q_gain", whole, gsum[ROW_MEM_GAINS:ROW_MEM_GAINS + 1, half[0]])
        update("mem_k_gain", whole, gsum[ROW_MEM_GAINS:ROW_MEM_GAINS + 1, half[1]])
        loss_ref[...] = jnp.sum(gsum[ROW_LOSS:ROW_LOSS + 8, :], keepdims=True) * (0.5 / D_MODEL)

    shapes = [jax.ShapeDtypeStruct(w[name].shape, F32) for name in SMALL]
    res = pl.pallas_call(
        body, name="adamw_small",
        out_shape=shapes * 4 + [jax.ShapeDtypeStruct((1, 1), F32)],
        scratch_shapes=[pltpu.VMEM((SMALL_ROWS, 128), F32)],
        compiler_params=_params(16),
    )(*[w[n] for n in SMALL], *[m[n] for n in SMALL], *[v[n] for n in SMALL], g_all)
    trees = [dict(zip(SMALL, res[j * k:(j + 1) * k])) for j in range(4)]
    return (*trees, res[4 * k])


def _grads(x, mem, tgt, w, shards):
    bd128, bd256 = _head_blockdiag(128), _head_blockdiag(256)
    gain = w["norm_gain"].reshape(1, D_MODEL)
    vg = w["gmlp_v_gain"].reshape(1, GMLP_WIDTH)
    w_s = w["gmlp_w_s"].reshape(4, CHUNK, CHUNK)
    b2 = jnp.repeat(w["gmlp_b"].reshape(4, CHUNK).T, HEAD_DIM, axis=1)
    qg2 = jnp.tile(w["attn_q_gain"].reshape(1, HEAD_DIM), (1, 2))
    kg2 = jnp.tile(w["attn_k_gain"].reshape(1, HEAD_DIM), (1, 2))
    mqg4 = jnp.tile(w["mem_q_gain"].reshape(1, HEAD_DIM), (1, 4))
    mkg4 = jnp.tile(w["mem_k_gain"].reshape(1, HEAD_DIM), (1, 4))
    mgain = w["mem_norm_gain"].reshape(1, D_MODEL)

    xpos = lax.axis_index("x").astype(jnp.int32).reshape(1)
    proj, h_bf, win_t, wkv_bf, wout_bf = _gather_proj(x, gain, shards, xpos)
    yg = _gmlp_fwd(proj, vg, w_s, b2, bd256)
    ya, att, lse = _attn_fwd(proj, qg2, kg2, bd128)
    hm_bf, kraw, mk, mv = _mem_kv(mem, mgain, wkv_bf, mkg4, bd256)
    ym, om = _mem_fwd(proj, mk, mv, mqg4, bd256)
    dout, dycat, dwout, sq = _out_loss(yg, ya, ym, x, tgt, wout_bf)

    du, dgv, dgg, dws, db2, dvg = _gmlp_bwd(proj, dycat, vg, w_s, b2, bd256)
    dq, dk, dv, dag, dqg, dkg = _attn_bwd(proj, dycat, att, lse, qg2, kg2, bd128)
    dmq, dmg, dmk, dmv, dmqg = _mem_bwd(proj, dycat, om, mk, mv, mqg4, bd256)
    dwkv, dmgain, dmkg = _mem_kv_bwd(dmk, dmv, kraw, mem, mgain, mkg4, wkv_bf, hm_bf, bd256)
    pieces = [du, dgv, dgg, dq, dk, dv, dag, dmq, dmg]
    dwin, g_wkv, g_wout = _in_bwd_dw(pieces, h_bf, [dwkv, dwout])
    grad_x, dgain, g_win = _in_bwd_dx(pieces, x, dout, gain, win_t, dwin)
    return grad_x, g_win, g_wkv, g_wout, (dgain, dmgain, dvg, db2, dqg, dkg, dmqg, dmkg, dws, sq)


def kernel(x, mem, norm_gain, w_in, gmlp_v_gain, gmlp_w_s, gmlp_b, attn_q_gain, attn_k_gain, mem_norm_gain, w_mem_kv, mem_q_gain, mem_k_gain, w_out, loss_target, m_norm_gain, m_w_in, m_gmlp_v_gain, m_gmlp_w_s, m_gmlp_b, m_attn_q_gain, m_attn_k_gain, m_mem_norm_gain, m_w_mem_kv, m_mem_q_gain, m_mem_k_gain, m_w_out, v_norm_gain, v_w_in, v_gmlp_v_gain, v_gmlp_w_s, v_gmlp_b, v_attn_q_gain, v_attn_k_gain, v_mem_norm_gain, v_w_mem_kv, v_mem_q_gain, v_mem_k_gain, v_w_out):
    w = dict(norm_gain=norm_gain, w_in=w_in, gmlp_v_gain=gmlp_v_gain, gmlp_w_s=gmlp_w_s, gmlp_b=gmlp_b,
             attn_q_gain=attn_q_gain, attn_k_gain=attn_k_gain, mem_norm_gain=mem_norm_gain, w_mem_kv=w_mem_kv,
             mem_q_gain=mem_q_gain, mem_k_gain=mem_k_gain, w_out=w_out)
    m = dict(norm_gain=m_norm_gain, w_in=m_w_in, gmlp_v_gain=m_gmlp_v_gain, gmlp_w_s=m_gmlp_w_s, gmlp_b=m_gmlp_b,
             attn_q_gain=m_attn_q_gain, attn_k_gain=m_attn_k_gain, mem_norm_gain=m_mem_norm_gain,
             w_mem_kv=m_w_mem_kv, mem_q_gain=m_mem_q_gain, mem_k_gain=m_mem_k_gain, w_out=m_w_out)
    v = dict(norm_gain=v_norm_gain, w_in=v_w_in, gmlp_v_gain=v_gmlp_v_gain, gmlp_w_s=v_gmlp_w_s, gmlp_b=v_gmlp_b,
             attn_q_gain=v_attn_q_gain, attn_k_gain=v_attn_k_gain, mem_norm_gain=v_mem_norm_gain,
             w_mem_kv=v_w_mem_kv, mem_q_gain=v_mem_q_gain, mem_k_gain=v_mem_k_gain, w_out=v_w_out)
    transposed = lambda t: jnp.transpose(t[0])

    grad_x, g_win, g_wkv, g_wout, small = _grads(
        x[0], mem[0], loss_target[0], w, [transposed(w_in), w_mem_kv[0], w_out[0]])
    small_all = _gather_small(*small)

    out_g, out_d, out_m, out_v, loss = _adamw_small(w, m, v, small_all)
    d_, m_, v_ = _adamw(transposed(w_in), g_win, transposed(m_w_in), transposed(v_w_in), "adamw_w_in")
    for tree, t in ((out_g, g_win), (out_d, d_), (out_m, m_), (out_v, v_)):
        tree["w_in"] = jnp.transpose(t)[None]
    for name, g in (("w_mem_kv", g_wkv), ("w_out", g_wout)):
        d_, m_, v_ = _adamw(w[name][0], g, m[name][0], v[name][0], "adamw_" + name)
        out_g[name], out_d[name], out_m[name], out_v[name] = g[None], d_[None], m_[None], v_[None]

    return (loss.reshape(()), grad_x[None], *[out_g[k] for k in WEIGHTS], *[out_d[k] for k in WEIGHTS],
            *[out_m[k] for k in WEIGHTS], *[out_v[k] for k in WEIGHTS])
```

```python
import functools
import math

import jax
import jax.numpy as jnp
from jax import lax
from jax.experimental import pallas as pl
from jax.experimental.pallas import tpu as pltpu

F32 = jnp.float32
BF16 = jnp.bfloat16

N_DEV = 8
D_MODEL = 1024
HEAD_DIM = 64
GMLP_WIDTH = 256
ATTN_WIDTH = 512
MEM_WIDTH = 256
MEM_LEN = 256
IN_WIDTH = 3 * GMLP_WIDTH + 4 * ATTN_WIDTH + 2 * MEM_WIDTH
CHUNK = 128
BLOCK = 128
DILATIONS = (1, 4, 16)
EPS = 1e-6
SCALE = 1.0 / math.sqrt(HEAD_DIM)
NEG = -1e30

ADAM_LR = 0.001
ADAM_B1 = 0.9
ADAM_B2 = 0.999
ADAM_EPS = 1e-08
ADAM_WD = 0.01
ADAM_STEP = 10

MIB = 1024 * 1024
MESH = pl.DeviceIdType.MESH

COL_AQ, COL_AK, COL_AV, COL_AG = 6, 10, 14, 18


def _params(vmem_mib, semantics=None):
    kw = dict(vmem_limit_bytes=vmem_mib * MIB)
    if semantics is not None:
        kw["dimension_semantics"] = semantics
    return pltpu.CompilerParams(**kw)


def _hbm(*arrs):
    return [pltpu.with_memory_space_constraint(a, pltpu.HBM) for a in arrs]


def _split_dot(x, sel_bf):
    hi = x.astype(BF16)
    lo = (x - hi.astype(F32)).astype(BF16)
    return jnp.dot(hi, sel_bf, preferred_element_type=F32) + jnp.dot(lo, sel_bf, preferred_element_type=F32)


def _nt(a, b):
    return lax.dot_general(a, b, (((1,), (1,)), ((), ())), preferred_element_type=F32)


def _tn(a, b):
    return lax.dot_general(a, b, (((0,), (0,)), ((), ())), preferred_element_type=F32)


def _silu_parts(g):
    sg = jax.nn.sigmoid(g)
    return g * sg, sg * (1.0 + g * (1.0 - sg))


def _head_index(shape):
    return lax.shift_right_logical(lax.broadcasted_iota(jnp.int32, shape, 1), HEAD_DIM.bit_length() - 1)


def _head_blockdiag(width):
    i = jnp.arange(width) // HEAD_DIM
    return (i[:, None] == i[None, :]).astype(BF16)


def _gmlp_masked_weights(ws_ref, transpose):
    t = lax.broadcasted_iota(jnp.int32, (CHUNK, CHUNK), 0)
    s = lax.broadcasted_iota(jnp.int32, (CHUNK, CHUNK), 1)
    parts = []
    for h in range(4):
        wm = jnp.where(s <= t, ws_ref[h], 0.0)
        parts.append(wm.T if transpose else wm)
    return jnp.concatenate(parts, axis=1).astype(BF16)


def _head_stack(v, head):
    return jnp.concatenate([jnp.where(head == h, v, 0.0) for h in range(4)], axis=0).astype(BF16)


def _gmlp_fwd(proj, vg, w_s, b2, bd):
    S = proj.shape[0]
    tm = 512

    def body(u_ref, v_ref, g_ref, vg_ref, ws_ref, b2_ref, bd_ref, y_ref):
        v = v_ref[...]
        ms = _split_dot(v * v, bd_ref[...]) * (1.0 / HEAD_DIM)
        vn = (v * lax.rsqrt(ms + EPS)) * vg_ref[...]
        wcat = _gmlp_masked_weights(ws_ref, False)
        head = _head_index((CHUNK, GMLP_WIDTH))
        for c in range(tm // CHUNK):
            rows = slice(c * CHUNK, (c + 1) * CHUNK)
            sp = jnp.dot(wcat, _head_stack(vn[rows], head), preferred_element_type=F32) + b2_ref[...]
            silu, _ = _silu_parts(g_ref[rows, :])
            y_ref[rows, :] = ((u_ref[rows, :] * sp) * silu).astype(BF16)

    col = lambda j: pl.BlockSpec((tm, GMLP_WIDTH), lambda i, j=j: (i, j))
    const = lambda shape: pl.BlockSpec(shape, lambda i: (0,) * len(shape))
    return pl.pallas_call(
        body, name="gmlp_fwd", grid=(S // tm,),
        in_specs=[col(0), col(1), col(2), const((1, GMLP_WIDTH)), const((4, CHUNK, CHUNK)),
                  const((CHUNK, GMLP_WIDTH)), const((GMLP_WIDTH, GMLP_WIDTH))],
        out_specs=pl.BlockSpec((tm, GMLP_WIDTH), lambda i: (i, 0)),
        out_shape=pltpu.HBM((S, GMLP_WIDTH), BF16),
        compiler_params=_params(24, ("arbitrary",)),
    )(*_hbm(proj, proj, proj, vg, w_s, b2, bd))


def _gmlp_bwd(proj, dycat, vg, w_s, b2, bd):
    S = proj.shape[0]
    tm = 512
    nsteps = S // tm

    def body(u_ref, v_ref, g_ref, dy_ref, vg_ref, ws_ref, b2_ref, bd_ref,
             du_ref, dv_ref, dg_ref, dws_ref, db2_ref, dvg_ref):
        i = pl.program_id(0)

        @pl.when(i == 0)
        def _():
            dws_ref[...] = jnp.zeros_like(dws_ref)
            db2_ref[...] = jnp.zeros_like(db2_ref)
            dvg_ref[...] = jnp.zeros_like(dvg_ref)

        bdv = bd_ref[...]
        v = v_ref[...]
        ms = _split_dot(v * v, bdv) * (1.0 / HEAD_DIM)
        rv = lax.rsqrt(ms + EPS)
        xhat = v * rv
        vgv = vg_ref[...]
        vn = xhat * vgv
        wcat = _gmlp_masked_weights(ws_ref, False)
        wcat_t = _gmlp_masked_weights(ws_ref, True)
        head = _head_index((CHUNK, GMLP_WIDTH))
        dvg = jnp.zeros((1, GMLP_WIDTH), F32)
        for c in range(tm // CHUNK):
            rows = slice(c * CHUNK, (c + 1) * CHUNK)
            vn_c = vn[rows]
            spb = jnp.dot(wcat, _head_stack(vn_c, head), preferred_element_type=F32) + b2_ref[...]
            silu, dsilu = _silu_parts(g_ref[rows, :])
            dy = dy_ref[rows, :]
            u = u_ref[rows, :]
            du_ref[rows, :] = (dy * spb * silu).astype(BF16)
            dg_ref[rows, :] = (dy * u * spb * dsilu).astype(BF16)
            dsp = dy * u * silu
            db2_ref[...] += dsp
            dstack = _head_stack(dsp, head)
            dvn = jnp.dot(wcat_t, dstack, preferred_element_type=F32)
            dws_ref[...] += _nt(dstack, vn_c.astype(BF16))
            xh = xhat[rows]
            a = dvn * vgv
            mean_ax = _split_dot(a * xh, bdv) * (1.0 / HEAD_DIM)
            dv_ref[rows, :] = (rv[rows] * (a - xh * mean_ax)).astype(BF16)
            dvg = dvg + jnp.sum(dvn * xh, axis=0, keepdims=True)
        dvg_ref[...] += dvg

        @pl.when(i == nsteps - 1)
        def _():
            t = lax.broadcasted_iota(jnp.int32, (4 * CHUNK, CHUNK), 0) % CHUNK
            s = lax.broadcasted_iota(jnp.int32, (4 * CHUNK, CHUNK), 1)
            dws_ref[...] = jnp.where(s <= t, dws_ref[...], 0.0)
            db2_ref[...] = _split_dot(db2_ref[...], bdv)

    col = lambda j: pl.BlockSpec((tm, GMLP_WIDTH), lambda i, j=j: (i, j))
    const = lambda shape: pl.BlockSpec(shape, lambda i: (0,) * len(shape))
    tile = pl.BlockSpec((tm, GMLP_WIDTH), lambda i: (i, 0))
    piece = pltpu.HBM((S, GMLP_WIDTH), BF16)
    return pl.pallas_call(
        body, name="gmlp_bwd", grid=(nsteps,),
        in_specs=[col(0), col(1), col(2), col(0), const((1, GMLP_WIDTH)), const((4, CHUNK, CHUNK)),
                  const((CHUNK, GMLP_WIDTH)), const((GMLP_WIDTH, GMLP_WIDTH))],
        out_specs=[tile, tile, tile, const((4 * CHUNK, CHUNK)), const((CHUNK, GMLP_WIDTH)), const((1, GMLP_WIDTH))],
        out_shape=[piece, piece, piece, pltpu.HBM((4 * CHUNK, CHUNK), F32),
                   pltpu.HBM((CHUNK, GMLP_WIDTH), F32), pltpu.HBM((1, GMLP_WIDTH), F32)],
        compiler_params=_params(32, ("arbitrary",)),
    )(*_hbm(proj, proj, proj, dycat, vg, w_s, b2, bd))


def _band_mask():
    qi = lax.broadcasted_iota(jnp.int32, (2 * BLOCK, 2 * BLOCK), 0) % BLOCK
    ki = lax.broadcasted_iota(jnp.int32, (2 * BLOCK, 2 * BLOCK), 1)
    return ((ki < BLOCK) & (ki >= qi)) | ((ki >= BLOCK) & ((ki - BLOCK) <= qi))


def _first_block_bias(blk, blocks_per_class):
    kcol = lax.broadcasted_iota(jnp.int32, (1, 2 * BLOCK), 1)
    kill = jnp.where((blk & (blocks_per_class - 1)) == 0, NEG, 0.0)
    return jnp.where(kcol < BLOCK, kill, 0.0)


def _two_heads(q, lo):
    zero = jnp.zeros_like(q)
    return jnp.concatenate([jnp.where(lo, q, zero), jnp.where(lo, zero, q)], axis=0)


def _block_tokens(blk, d, S):
    if d == 1:
        return pl.ds(pl.multiple_of(blk * BLOCK, BLOCK), BLOCK)
    blocks_per_class = S // d // BLOCK
    r = lax.shift_right_logical(blk, blocks_per_class.bit_length() - 1)
    n = blk & (blocks_per_class - 1)
    return pl.ds(r + n * (BLOCK * d), BLOCK, stride=d)


def _padded_block(blk):
    return pl.ds(pl.multiple_of((blk + 1) * BLOCK, BLOCK), BLOCK)


def _for_blocks(n_blocks, unroll, fn):
    def group(g, carry):
        for u in range(unroll):
            fn(g * unroll + u)
        return carry
    lax.fori_loop(0, n_blocks // unroll, group, 0)


def _attn_fwd(proj, qg2, kg2, bd):
    S = proj.shape[0]
    npairs = ATTN_WIDTH // 128
    tn = 512

    def body(q_ref, k_ref, v_ref, g_ref, qg_ref, kg_ref, bd_ref, y_ref, att_ref, lse_ref, qn, kn, kc, vc):
        bdv = bd_ref[...]
        lo = lax.broadcasted_iota(jnp.int32, (BLOCK, 128), 1) < HEAD_DIM
        band_mask = _band_mask()
        kc[pl.ds(0, BLOCK), :] = jnp.zeros((BLOCK, 128), BF16)
        vc[pl.ds(0, BLOCK), :] = jnp.zeros((BLOCK, 128), BF16)

        def norm_step(i, carry):
            rows = pl.ds(pl.multiple_of(i * tn, tn), tn)
            qv = q_ref[rows, :]
            kv = k_ref[rows, :]
            qn[rows, :] = (qv * lax.rsqrt(_split_dot(qv * qv, bdv) * (1.0 / HEAD_DIM) + EPS)) * (qg_ref[...] * SCALE)
            kn[rows, :] = (kv * lax.rsqrt(_split_dot(kv * kv, bdv) * (1.0 / HEAD_DIM) + EPS)) * kg_ref[...]
            return carry
        lax.fori_loop(0, S // tn, norm_step, 0)

        def fill(blk, d):
            tokens = _block_tokens(blk, d, S)
            kc[_padded_block(blk), :] = kn[tokens, :].astype(BF16)
            vc[_padded_block(blk), :] = v_ref[tokens, :].astype(BF16)

        ones_bf = jnp.ones((2 * BLOCK, 128), BF16)

        def block(blk, d):
            tokens = _block_tokens(blk, d, S)
            keys = pl.ds(pl.multiple_of(blk * BLOCK, BLOCK), 2 * BLOCK)
            q2 = _two_heads(qn[tokens, :].astype(BF16), lo)
            s = jnp.where(band_mask, _nt(q2, kc[keys, :]), NEG) + _first_block_bias(blk, S // d // BLOCK)
            m = jnp.max(s, axis=-1, keepdims=True)
            e = jnp.exp((s - m).astype(BF16))
            ol = jnp.dot(e, jnp.concatenate([vc[keys, :], ones_bf], axis=1), preferred_element_type=F32)
            l = ol[:, 128:]
            o2 = ol[:, :128] * (1.0 / l)
            lse2 = m + jnp.log(l)
            o = jnp.where(lo, o2[:BLOCK], o2[BLOCK:])
            lse = jnp.where(lo, lse2[:BLOCK], lse2[BLOCK:])
            if d > 1:
                la = lse_ref[tokens, :]
                mx = jnp.maximum(la, lse)
                wa, wb = jnp.exp(la - mx), jnp.exp(lse - mx)
                t = wa + wb
                o = (wa * att_ref[tokens, :] + wb * o) / t
                lse = mx + jnp.log(t)
            att_ref[tokens, :] = o
            lse_ref[tokens, :] = lse

        for d in DILATIONS:
            _for_blocks(S // BLOCK, 4, functools.partial(fill, d=d))
            _for_blocks(S // BLOCK, 8, functools.partial(block, d=d))

        def gate_step(i, carry):
            rows = pl.ds(pl.multiple_of(i * tn, tn), tn)
            silu, _ = _silu_parts(g_ref[rows, :])
            y_ref[rows, :] = (att_ref[rows, :] * silu).astype(BF16)
            return carry
        lax.fori_loop(0, S // tn, gate_step, 0)

    col = lambda j0: pl.BlockSpec((S, 128), lambda p, j0=j0: (0, j0 + p))
    const = lambda shape: pl.BlockSpec(shape, lambda p: (0,) * len(shape))
    out = pl.BlockSpec((S, 128), lambda p: (0, p))
    return pl.pallas_call(
        body, name="attn_fwd", grid=(npairs,),
        in_specs=[col(COL_AQ), col(COL_AK), col(COL_AV), col(COL_AG), const((1, 128)), const((1, 128)),
                  const((128, 128))],
        out_specs=[out, out, out],
        out_shape=[pltpu.HBM((S, ATTN_WIDTH), BF16), pltpu.HBM((S, ATTN_WIDTH), F32),
                   pltpu.HBM((S, ATTN_WIDTH), F32)],
        scratch_shapes=[pltpu.VMEM((S, 128), F32), pltpu.VMEM((S, 128), F32),
                        pltpu.VMEM((S + BLOCK, 128), BF16), pltpu.VMEM((S + BLOCK, 128), BF16)],
        compiler_params=_params(48, ("arbitrary",)),
    )(*_hbm(proj, proj, proj, proj, qg2, kg2, bd))


def _attn_bwd(proj, dycat, att, lse, qg2, kg2, bd):
    S = proj.shape[0]
    npairs = ATTN_WIDTH // 128
    tn = 512

    def body(q_ref, k_ref, v_ref, g_ref, dy_ref, att_ref, lse_ref, qg_ref, kg_ref, bd_ref,
             dq_ref, dk_ref, dv_ref, dg_ref, dqg_ref, dkg_ref,
             qn, kn, rq_s, rk_s, kc, vc, do_s, dd_s, dqa, dka, dva):
        bdv = bd_ref[...]
        lo = lax.broadcasted_iota(jnp.int32, (BLOCK, 128), 1) < HEAD_DIM
        kc[pl.ds(0, BLOCK), :] = jnp.zeros((BLOCK, 128), BF16)
        vc[pl.ds(0, BLOCK), :] = jnp.zeros((BLOCK, 128), BF16)

        def prepare(i, carry):
            rows = pl.ds(pl.multiple_of(i * tn, tn), tn)
            qv = q_ref[rows, :]
            kv = k_ref[rows, :]
            rq = lax.rsqrt(_split_dot(qv * qv, bdv) * (1.0 / HEAD_DIM) + EPS)
            rk = lax.rsqrt(_split_dot(kv * kv, bdv) * (1.0 / HEAD_DIM) + EPS)
            rq_s[rows, :] = rq
            rk_s[rows, :] = rk
            qn[rows, :] = (qv * rq) * (qg_ref[...] * SCALE)
            kn[rows, :] = (kv * rk) * kg_ref[...]
            silu, dsilu = _silu_parts(g_ref[rows, :])
            dy = dy_ref[rows, :]
            at = att_ref[rows, :]
            do = dy * silu
            do_s[rows, :] = do
            dd_s[rows, :] = _split_dot(do * at, bdv)
            dg_ref[rows, :] = (dy * at * dsilu).astype(BF16)
            dka[rows, :] = jnp.zeros((tn, 128), F32)
            dva[rows, :] = jnp.zeros((tn, 128), F32)
            return carry
        lax.fori_loop(0, S // tn, prepare, 0)

        kt = lax.broadcasted_iota(jnp.int32, (2 * BLOCK, 2 * BLOCK), 0)
        qt = lax.broadcasted_iota(jnp.int32, (2 * BLOCK, 2 * BLOCK), 1) % BLOCK
        band_mask_t = ((kt < BLOCK) & (kt >= qt)) | ((kt >= BLOCK) & ((kt - BLOCK) <= qt))

        def per_query_row(t):
            tt = t.T
            return jnp.concatenate([tt[0:1, :], tt[HEAD_DIM:HEAD_DIM + 1, :]], axis=1)

        def fill(blk, d):
            tokens = _block_tokens(blk, d, S)
            kc[_padded_block(blk), :] = kn[tokens, :].astype(BF16)
            vc[_padded_block(blk), :] = v_ref[tokens, :].astype(BF16)

        def block(blk, d):
            tokens = _block_tokens(blk, d, S)
            keys = pl.ds(pl.multiple_of(blk * BLOCK, BLOCK), 2 * BLOCK)
            first = (blk & (S // d // BLOCK - 1)) == 0
            q2 = _two_heads(qn[tokens, :].astype(BF16), lo)
            do2 = _two_heads(do_s[tokens, :].astype(BF16), lo)
            lse_row = per_query_row(lse_ref[tokens, :])
            dd_row = per_query_row(dd_s[tokens, :])
            kb = kc[keys, :]
            vb = vc[keys, :]
            st = jnp.where(band_mask_t, _nt(kb, q2), NEG)
            st = jnp.concatenate([st[:BLOCK] + jnp.where(first, NEG, 0.0), st[BLOCK:]], axis=0)
            ptb = jnp.exp((st - lse_row).astype(BF16))
            dstb = ptb * (_nt(vb, do2) - dd_row).astype(BF16)
            dv_band = jnp.dot(ptb, do2, preferred_element_type=F32)
            dk_band = jnp.dot(dstb, q2, preferred_element_type=F32)
            before = _block_tokens(jnp.where(first, blk, blk - 1), d, S)
            dka[before, :] = dka[before, :] + dk_band[:BLOCK]
            dva[before, :] = dva[before, :] + dv_band[:BLOCK]
            dka[tokens, :] = dka[tokens, :] + dk_band[BLOCK:]
            dva[tokens, :] = dva[tokens, :] + dv_band[BLOCK:]
            dq2 = _tn(dstb, kb)
            dq = jnp.where(lo, dq2[:BLOCK], dq2[BLOCK:])
            dqa[tokens, :] = dq if d == 1 else dqa[tokens, :] + dq

        for d in DILATIONS:
            _for_blocks(S // BLOCK, 4, functools.partial(fill, d=d))
            _for_blocks(S // BLOCK, 8, functools.partial(block, d=d))

        def out_step(i, carry):
            dqg, dkg = carry
            rows = pl.ds(pl.multiple_of(i * tn, tn), tn)
            rq = rq_s[rows, :]
            rk = rk_s[rows, :]
            qh = q_ref[rows, :] * rq
            kh = k_ref[rows, :] * rk
            dqs = dqa[rows, :] * SCALE
            dkn = dka[rows, :]
            aq = dqs * qg_ref[...]
            ak = dkn * kg_ref[...]
            dq_ref[rows, :] = (rq * (aq - qh * (_split_dot(aq * qh, bdv) * (1.0 / HEAD_DIM)))).astype(BF16)
            dk_ref[rows, :] = (rk * (ak - kh * (_split_dot(ak * kh, bdv) * (1.0 / HEAD_DIM)))).astype(BF16)
            dv_ref[rows, :] = dva[rows, :].astype(BF16)
            dqg = dqg + jnp.sum(dqs * qh, axis=0, keepdims=True)
            dkg = dkg + jnp.sum(dkn * kh, axis=0, keepdims=True)
            return dqg, dkg
        zero = jnp.zeros((1, 128), F32)
        dqg, dkg = lax.fori_loop(0, S // tn, out_step, (zero, zero))
        dqg_ref[0] = dqg
        dkg_ref[0] = dkg

    col = lambda j0: pl.BlockSpec((S, 128), lambda p, j0=j0: (0, j0 + p))
    col1 = lambda j0: pl.BlockSpec((S, 128), lambda p, j0=j0: (0, j0 + p), pipeline_mode=pl.Buffered(1))
    const = lambda shape: pl.BlockSpec(shape, lambda p: (0,) * len(shape))
    out = pl.BlockSpec((S, 128), lambda p: (0, p))
    gain_out = pl.BlockSpec((1, 1, 128), lambda p: (p, 0, 0))
    piece = pltpu.HBM((S, ATTN_WIDTH), BF16)
    gains = pltpu.HBM((npairs, 1, 128), F32)
    f32buf = pltpu.VMEM((S, 128), F32)
    bf16pad = pltpu.VMEM((S + BLOCK, 128), BF16)
    return pl.pallas_call(
        body, name="attn_bwd", grid=(npairs,),
        in_specs=[col(COL_AQ), col(COL_AK), col(COL_AV), col1(COL_AG), col1(GMLP_WIDTH // 128), col1(0), col(0),
                  const((1, 128)), const((1, 128)), const((128, 128))],
        out_specs=[out, out, out, out, gain_out, gain_out],
        out_shape=[piece, piece, piece, piece, gains, gains],
        scratch_shapes=[f32buf, f32buf, f32buf, f32buf, bf16pad, bf16pad, f32buf, f32buf, f32buf, f32buf, f32buf],
        compiler_params=_params(60, ("arbitrary",)),
    )(*_hbm(proj, proj, proj, proj, dycat, att, lse, qg2, kg2, bd))


def _mem_kv(mem, gain, wkv_bf, kg4, bd):
    def body(mem_ref, g_ref, w_ref, kg_ref, bd_ref, hm_ref, kraw_ref, mk_ref, mv_ref):
        mv_ = mem_ref[...]
        r = lax.rsqrt(jnp.mean(mv_ * mv_, axis=-1, keepdims=True) + EPS)
        hm = ((mv_ * r) * g_ref[...]).astype(BF16)
        hm_ref[...] = hm
        kv = jnp.dot(hm, w_ref[...], preferred_element_type=F32)
        kraw = kv[:, :MEM_WIDTH]
        kraw_ref[...] = kraw
        ms = _split_dot(kraw * kraw, bd_ref[...]) * (1.0 / HEAD_DIM)
        mk_ref[...] = (kraw * lax.rsqrt(ms + EPS)) * kg_ref[...]
        mv_ref[...] = kv[:, MEM_WIDTH:]

    sq = jax.ShapeDtypeStruct((MEM_LEN, MEM_WIDTH), F32)
    return pl.pallas_call(
        body, name="mem_kv",
        out_shape=[jax.ShapeDtypeStruct((MEM_LEN, D_MODEL), BF16), sq, sq, sq],
        compiler_params=_params(16),
    )(mem, gain, wkv_bf, kg4, bd)


def _mem_fwd(proj, mk, mv, qg4, bd):
    S = proj.shape[0]
    tm = 512

    def body(q_ref, g_ref, mk_ref, mv_ref, qg_ref, bd_ref, y_ref, om_ref):
        qv = q_ref[...]
        ms = _split_dot(qv * qv, bd_ref[...]) * (1.0 / HEAD_DIM)
        qs = (qv * lax.rsqrt(ms + EPS)) * (qg_ref[...] * SCALE)
        mkb = mk_ref[...].astype(BF16)
        mvb = mv_ref[...].astype(BF16)
        head = _head_index((tm, MEM_WIDTH))
        o = jnp.zeros((tm, MEM_WIDTH), F32)
        for h in range(4):
            s = _nt(jnp.where(head == h, qs, 0.0).astype(BF16), mkb)
            e = jnp.exp(s - jnp.max(s, axis=-1, keepdims=True))
            p = e * (1.0 / jnp.sum(e, axis=-1, keepdims=True))
            o = jnp.where(head == h, jnp.dot(p.astype(BF16), mvb, preferred_element_type=F32), o)
        om_ref[...] = o
        silu, _ = _silu_parts(g_ref[...])
        y_ref[...] = (o * silu).astype(BF16)

    col = lambda j: pl.BlockSpec((tm, MEM_WIDTH), lambda i, j=j: (i, j))
    const = lambda shape: pl.BlockSpec(shape, lambda i: (0,) * len(shape))
    tile = pl.BlockSpec((tm, MEM_WIDTH), lambda i: (i, 0))
    return pl.pallas_call(
        body, name="mem_fwd", grid=(S // tm,),
        in_specs=[col(11), col(12), const((MEM_LEN, MEM_WIDTH)), const((MEM_LEN, MEM_WIDTH)), const((1, MEM_WIDTH)),
                  const((MEM_WIDTH, MEM_WIDTH))],
        out_specs=[tile, tile],
        out_shape=[pltpu.HBM((S, MEM_WIDTH), BF16), pltpu.HBM((S, MEM_WIDTH), F32)],
        compiler_params=_params(24, ("arbitrary",)),
    )(*_hbm(proj, proj, mk, mv, qg4, bd))


def _mem_bwd(proj, dycat, om, mk, mv, qg4, bd):
    S = proj.shape[0]
    tm = 512

    def body(q_ref, g_ref, dy_ref, om_ref, mk_ref, mv_ref, qg_ref, bd_ref,
             dq_ref, dg_ref, dmk_ref, dmv_ref, dqg_ref):
        i = pl.program_id(0)

        @pl.when(i == 0)
        def _():
            dmk_ref[...] = jnp.zeros_like(dmk_ref)
            dmv_ref[...] = jnp.zeros_like(dmv_ref)
            dqg_ref[...] = jnp.zeros_like(dqg_ref)

        bdv = bd_ref[...]
        qv = q_ref[...]
        rq = lax.rsqrt(_split_dot(qv * qv, bdv) * (1.0 / HEAD_DIM) + EPS)
        qh = qv * rq
        qs = qh * (qg_ref[...] * SCALE)
        silu, dsilu = _silu_parts(g_ref[...])
        dy = dy_ref[...]
        o = om_ref[...]
        do = dy * silu
        dg_ref[...] = (dy * o * dsilu).astype(BF16)
        dd = _split_dot(do * o, bdv)
        mkb = mk_ref[...].astype(BF16)
        mvb = mv_ref[...].astype(BF16)
        head = _head_index((tm, MEM_WIDTH))
        dqs = jnp.zeros((tm, MEM_WIDTH), F32)
        for h in range(4):
            qhd = jnp.where(head == h, qs, 0.0).astype(BF16)
            doh = jnp.where(head == h, do, 0.0).astype(BF16)
            s = _nt(qhd, mkb)
            e = jnp.exp(s - jnp.max(s, axis=-1, keepdims=True))
            p = e * (1.0 / jnp.sum(e, axis=-1, keepdims=True))
            ds = p * (_nt(doh, mvb) - dd[:, h * HEAD_DIM:h * HEAD_DIM + 1])
            dsb = ds.astype(BF16)
            dmv_ref[...] += _tn(p.astype(BF16), doh)
            dmk_ref[...] += _tn(dsb, qhd)
            dqs = jnp.where(head == h, jnp.dot(dsb, mkb, preferred_element_type=F32), dqs)
        dqs = dqs * SCALE
        a = dqs * qg_ref[...]
        dq_ref[...] = (rq * (a - qh * (_split_dot(a * qh, bdv) * (1.0 / HEAD_DIM)))).astype(BF16)
        dqg_ref[...] += jnp.sum(dqs * qh, axis=0, keepdims=True)

    col = lambda j: pl.BlockSpec((tm, MEM_WIDTH), lambda i, j=j: (i, j))
    const = lambda shape: pl.BlockSpec(shape, lambda i: (0,) * len(shape))
    tile = pl.BlockSpec((tm, MEM_WIDTH), lambda i: (i, 0))
    piece = pltpu.HBM((S, MEM_WIDTH), BF16)
    sq = pltpu.HBM((MEM_LEN, MEM_WIDTH), F32)
    return pl.pallas_call(
        body, name="mem_bwd", grid=(S // tm,),
        in_specs=[col(11), col(12), col(3), tile, const((MEM_LEN, MEM_WIDTH)), const((MEM_LEN, MEM_WIDTH)),
                  const((1, MEM_WIDTH)), const((MEM_WIDTH, MEM_WIDTH))],
        out_specs=[tile, tile, const((MEM_LEN, MEM_WIDTH)), const((MEM_LEN, MEM_WIDTH)), const((1, MEM_WIDTH))],
        out_shape=[piece, piece, sq, sq, pltpu.HBM((1, MEM_WIDTH), F32)],
        compiler_params=_params(32, ("arbitrary",)),
    )(*_hbm(proj, proj, dycat, om, mk, mv, qg4, bd))


def _mem_kv_bwd(dmk, dmv, kraw, mem, gain, kg4, wkv_bf, hm_bf, bd):
    def body(dmk_ref, dmv_ref, kraw_ref, mem_ref, g_ref, kg_ref, w_ref, hm_ref, bd_ref, dw_ref, dg_ref, dkg_ref):
        bdv = bd_ref[...]
        kraw = kraw_ref[...]
        rk = lax.rsqrt(_split_dot(kraw * kraw, bdv) * (1.0 / HEAD_DIM) + EPS)
        kh = kraw * rk
        dmkv = dmk_ref[...]
        a = dmkv * kg_ref[...]
        dkraw = rk * (a - kh * (_split_dot(a * kh, bdv) * (1.0 / HEAD_DIM)))
        dkg_ref[...] = jnp.sum(dmkv * kh, axis=0, keepdims=True)
        dkv = jnp.concatenate([dkraw, dmv_ref[...]], axis=1).astype(BF16)
        dw = _tn(hm_ref[...], dkv).astype(BF16)
        rows_blk = D_MODEL // N_DEV
        for j in range(N_DEV):
            dw_ref[j] = dw[rows_blk * j:rows_blk * (j + 1)]
        dhm = _nt(dkv, w_ref[...])
        mv_ = mem_ref[...]
        r = lax.rsqrt(jnp.mean(mv_ * mv_, axis=-1, keepdims=True) + EPS)
        dg_ref[...] = jnp.sum(dhm * (mv_ * r), axis=0, keepdims=True)

    return pl.pallas_call(
        body, name="mem_kv_bwd",
        out_shape=[jax.ShapeDtypeStruct((N_DEV, D_MODEL // N_DEV, 2 * MEM_WIDTH), BF16),
                   jax.ShapeDtypeStruct((1, D_MODEL), F32), jax.ShapeDtypeStruct((1, MEM_WIDTH), F32)],
        compiler_params=_params(24),
    )(dmk, dmv, kraw, mem, gain, kg4, wkv_bf, hm_bf, bd)


def _out_loss(yg, ya, ym, x, tgt, wout_bf):
    S, D = x.shape
    tm = 512
    nsteps = S // tm
    rows_blk = D // N_DEV

    def body(yg_ref, ya_ref, ym_ref, x_ref, t_ref, w_ref, dout_ref, dycat_ref, dw_ref, loss_ref, acc_ref):
        i = pl.program_id(0)

        @pl.when(i == 0)
        def _():
            acc_ref[...] = jnp.zeros_like(acc_ref)
            loss_ref[...] = jnp.zeros_like(loss_ref)

        ycat = jnp.concatenate([yg_ref[...], ya_ref[...], ym_ref[...]], axis=1)
        w = w_ref[...]
        diff = (x_ref[...] + jnp.dot(ycat, w, preferred_element_type=F32)) - t_ref[...]
        loss_ref[...] += jnp.sum(diff * diff, axis=0, keepdims=True)
        dout = diff * (1.0 / D)
        dout_ref[...] = dout
        db = dout.astype(BF16)
        dycat_ref[...] = _nt(db, w)
        acc_ref[...] += _tn(ycat, db)

        @pl.when(i == nsteps - 1)
        def _():
            for j in range(N_DEV):
                dw_ref[j] = acc_ref[rows_blk * j:rows_blk * (j + 1), :].astype(BF16)

    tile = lambda w: pl.BlockSpec((tm, w), lambda i: (i, 0))
    const = lambda shape: pl.BlockSpec(shape, lambda i: (0,) * len(shape))
    return pl.pallas_call(
        body, name="out_loss", grid=(nsteps,),
        in_specs=[tile(GMLP_WIDTH), tile(ATTN_WIDTH), tile(MEM_WIDTH), tile(D), tile(D), const((D, D))],
        out_specs=[tile(D), tile(D), const((N_DEV, rows_blk, D)), const((1, D))],
        out_shape=[pltpu.HBM((S, D), F32), pltpu.HBM((S, D), F32),
                   pltpu.HBM((N_DEV, rows_blk, D), BF16), pltpu.HBM((1, D), F32)],
        scratch_shapes=[pltpu.VMEM((D, D), F32)],
        compiler_params=_params(40, ("arbitrary",)),
    )(*_hbm(yg, ya, ym, x, tgt, wout_bf))


def _piece_specs(pieces, tm):
    return [pl.BlockSpec((tm, p.shape[1]), lambda i: (i, 0)) for p in pieces]


def _in_bwd_dx(pieces, x, dout, gain, w_t, dw_blocks):
    S, D = x.shape
    N = w_t.shape[0]
    tm = 512
    n = len(pieces)
    nsteps = S // tm
    middle_step = nsteps // 8

    def body(*refs):
        piece_refs = refs[:n]
        x_ref, dout_ref, g_ref, w_ref, dwb_ref, gx_ref, dg_ref, gw_ref = refs[n:n + 8]
        rs = _ReduceScatter([dwb_ref], [gw_ref], *refs[n + 8:])
        i = pl.program_id(0)

        @pl.when(i == 0)
        def _():
            dg_ref[...] = jnp.zeros_like(dg_ref)
            rs.start()

        @pl.when(i == middle_step)
        def _():
            rs.middle()

        dproj = jnp.concatenate([r[...] for r in piece_refs], axis=1)
        dh = jnp.dot(dproj, w_ref[...], preferred_element_type=F32)
        xv = x_ref[...]
        r = lax.rsqrt(jnp.mean(xv * xv, axis=-1, keepdims=True) + EPS)
        xh = xv * r
        a = dh * g_ref[...]
        gx_ref[...] = dout_ref[...] + r * (a - xh * jnp.mean(a * xh, axis=-1, keepdims=True))
        dg_ref[...] += jnp.sum(dh * xh, axis=0, keepdims=True)

        @pl.when(i == nsteps - 1)
        def _():
            rs.finish()

    tile = pl.BlockSpec((tm, D), lambda i: (i, 0))
    const = lambda shape: pl.BlockSpec(shape, lambda i: (0,) * len(shape))
    vmem = pl.BlockSpec(memory_space=pltpu.VMEM)
    return pl.pallas_call(
        body, name="in_bwd_dx", grid=(nsteps,),
        in_specs=_piece_specs(pieces, tm)
        + [tile, tile, const((1, D)), pl.BlockSpec((N, D), lambda i: (0, 0), pipeline_mode=pl.Buffered(1)), vmem],
        out_specs=[tile, const((1, D)), vmem],
        out_shape=[pltpu.HBM((S, D), F32), pltpu.HBM((1, D), F32), jax.ShapeDtypeStruct(dw_blocks.shape[1:], F32)],
        scratch_shapes=_reduce_scatter_scratch([dw_blocks]),
        compiler_params=_params(56, ("arbitrary",)),
    )(*_hbm(*pieces, x, dout, gain, w_t), dw_blocks)


def _in_bwd_dw(pieces, h_bf, others):
    S, D = h_bf.shape
    N = sum(p.shape[1] for p in pieces)
    n_blk = N // N_DEV
    tm = 512
    n = len(pieces)
    k = len(others)
    nsteps = S // tm

    def body(*refs):
        piece_refs = refs[:n]
        h_ref = refs[n]
        other_refs = refs[n + 1:n + 1 + k]
        dw_ref = refs[n + 1 + k]
        sum_refs = refs[n + 2 + k:n + 2 + 2 * k]
        acc_ref = refs[n + 2 + 2 * k]
        rs = _ReduceScatter(other_refs, sum_refs, *refs[n + 3 + 2 * k:])
        i = pl.program_id(0)

        @pl.when(i == 0)
        def _():
            acc_ref[...] = jnp.zeros_like(acc_ref)
            rs.start()

        @pl.when(i == 1)
        def _():
            rs.middle()

        dproj = jnp.concatenate([r[...] for r in piece_refs], axis=1)
        acc_ref[...] += _tn(h_ref[...], dproj)

        @pl.when(i == nsteps - 1)
        def _():
            for j in range(N_DEV):
                dw_ref[j] = acc_ref[:, n_blk * j:n_blk * (j + 1)].T.astype(BF16)
            rs.finish()

    vmem = pl.BlockSpec(memory_space=pltpu.VMEM)
    return pl.pallas_call(
        body, name="in_bwd_dw", grid=(nsteps,),
        in_specs=_piece_specs(pieces, tm) + [pl.BlockSpec((tm, D), lambda i: (i, 0))] + [vmem] * k,
        out_specs=[pl.BlockSpec((N_DEV, n_blk, D), lambda i: (0, 0, 0))] + [vmem] * k,
        out_shape=[pltpu.HBM((N_DEV, n_blk, D), BF16)] + [jax.ShapeDtypeStruct(o.shape[1:], F32) for o in others],
        scratch_shapes=[pltpu.VMEM((D, N), F32)] + _reduce_scatter_scratch(others),
        compiler_params=_params(56, ("arbitrary",)),
    )(*_hbm(*pieces, h_bf), *others)


def _row_step(m):
    return max(t for t in range(16, 257, 16) if m % t == 0)


def _place():
    x, y, c = lax.axis_index("x"), lax.axis_index("y"), lax.axis_index("c")
    chips = [(1 - x, y), (x, 1 - y), (1 - x, 1 - y)]
    return x, y, c, chips


class _AllGather:
    def __init__(self, srcs, outs, send_sems, recv_sems, local_sems, first_sem=0):
        self.srcs, self.outs, self.n, self.first_sem = srcs, outs, len(srcs), first_sem
        self.send_sems, self.recv_sems, self.local_sems = send_sems, recv_sems, local_sems

    def _rows(self, a, px, py, pc):
        m = self.srcs[a].shape[0]
        return self.outs[a].at[pl.ds((4 * px + 2 * py + pc) * m, m), :]

    def _copy(self, a, k, block, to, src=None):
        row = self.first_sem + a
        return pltpu.make_async_remote_copy(
            src_ref=self._rows(a, *block) if src is None else src, dst_ref=self._rows(a, *block),
            send_sem=self.send_sems.at[row, k], recv_sem=self.recv_sems.at[row, k], device_id=to, device_id_type=MESH)

    def _mine(self):
        x, y, c, _ = _place()
        return [pltpu.make_async_copy(self.srcs[a], self._rows(a, x, y, c), self.local_sems.at[self.first_sem + a])
                for a in range(self.n)]

    def _first(self, far):
        x, y, c, chips = _place()
        out = []
        for a in range(self.n):
            if far:
                out.append(self._copy(a, 3, (x, y, c), (*chips[2], c), src=self.srcs[a]))
            else:
                out.append(self._copy(a, 0, (x, y, c), (x, y, 1 - c), src=self.srcs[a]))
                out += [self._copy(a, 1 + j, (x, y, c), (*chips[j], c), src=self.srcs[a]) for j in (1, 0)]
        return out

    def _passed(self, j):
        x, y, c, chips = _place()
        return [self._copy(a, 4 + j, (*chips[j], c), (x, y, 1 - c)) for a in range(self.n)]

    def start(self):
        for cp in self._mine() + self._first(far=False):
            cp.start()

    def start_far(self):
        for cp in self._first(far=True):
            cp.start()

    def from_chip(self, j):
        x, y, c, chips = _place()
        for a in range(self.n):
            self._copy(a, 1 + j, (*chips[j], c), (x, y, c)).wait_recv()
        for cp in self._passed(j):
            cp.start()

    def from_sibling(self, j=None):
        x, y, c, chips = _place()
        for a in range(self.n):
            block = (x, y, 1 - c) if j is None else (*chips[j], 1 - c)
            self._copy(a, 0 if j is None else 4 + j, block, (x, y, c)).wait_recv()

    def from_self(self):
        for cp in self._mine():
            cp.wait()

    def finish(self):
        for cp in (self._first(far=False) + self._first(far=True)
                   + self._passed(0) + self._passed(1) + self._passed(2)):
            cp.wait_send()

    def run(self):
        self.start()
        self.start_far()
        self.from_self()
        for j in range(3):
            self.from_chip(j)
        self.from_sibling()
        for j in range(3):
            self.from_sibling(j)
        self.finish()


def _gather_proj(x, gain, shards, xpos):
    S, D = x.shape
    n = len(shards)
    N = N_DEV * shards[0].shape[0]
    half = N // 2
    tm = 512
    nsteps = S // tm

    def body(*refs):
        xpos_ref, x_ref, g_ref = refs[:3]
        ins = refs[3:3 + n]
        proj_ref, h_ref = refs[3 + n:5 + n]
        outs = refs[5 + n:5 + 2 * n]
        casts = refs[5 + 2 * n:5 + 3 * n]
        whole = refs[5 + 3 * n:5 + 4 * n]
        sems = refs[5 + 4 * n:8 + 4 * n]
        ag = _AllGather(casts[:1], whole[:1], *sems)
        later = _AllGather(casts[1:], whole[1:], *sems, first_sem=1)
        out_sems, h_all = refs[8 + 4 * n:]
        p, i = pl.program_id(0), pl.program_id(1)
        rows = pl.ds(pl.multiple_of(i * tm, tm), tm)

        @pl.when((p == 0) & (i == 0))
        def _():
            for a in range(n):
                tr = _row_step(ins[a].shape[0])

                def cast(r, carry, a=a, tr=tr):
                    at = pl.ds(pl.multiple_of(r * tr, tr), tr)
                    casts[a][at, :] = ins[a][at, :].astype(BF16)
                    return carry
                lax.fori_loop(0, ins[a].shape[0] // tr, cast, 0)
            ag.start()

        @pl.when(p == 0)
        def _():
            xv = x_ref[...]
            r = lax.rsqrt(jnp.mean(xv * xv, axis=-1, keepdims=True) + EPS)
            h = ((xv * r) * g_ref[...]).astype(BF16)
            h_ref[...] = h
            h_all[rows, :] = h

        @pl.when((p == 1) & (i == 0))
        def _():
            ag.from_self()
            ag.from_chip(1)
            ag.from_sibling()
            ag.from_sibling(1)
            ag.start_far()
            later.start()
            later.start_far()

        @pl.when((p == 2) & (i == 0))
        def _():
            for j in (0, 2):
                ag.from_chip(j)
            for j in (0, 2):
                ag.from_sibling(j)

        @pl.when(p > 0)
        def _():
            which = (xpos_ref[0] + p - 1) % 2
            w_half = whole[0][pl.ds(pl.multiple_of(which * half, half), half), :]
            proj_ref[...] = _nt(h_all[rows, :], w_half)

        @pl.when((p == 2) & (i == nsteps - 1))
        def _():
            ag.finish()
            later.from_self()
            for j in range(3):
                later.from_chip(j)
            later.from_sibling()
            for j in range(3):
                later.from_sibling(j)
            later.finish()
            to_results = [pltpu.make_async_copy(whole[a], outs[a], out_sems.at[a]) for a in range(n)]
            for cp in to_results:
                cp.start()
            for cp in to_results:
                cp.wait()

    vmem = pl.BlockSpec(memory_space=pltpu.VMEM)
    hbm = pl.BlockSpec(memory_space=pl.ANY)
    gathered = [(N_DEV * a.shape[0], a.shape[1]) for a in shards]
    x_tile = lambda p, i, xp: (jnp.where(p == 0, i, nsteps - 1), 0)
    proj_tile = lambda p, i, xp: (jnp.where(p == 0, 0, i), (xp[0] + jnp.maximum(p - 1, 0)) % 2)
    grid_spec = pltpu.PrefetchScalarGridSpec(
        num_scalar_prefetch=1, grid=(3, nsteps),
        in_specs=[pl.BlockSpec((tm, D), x_tile), pl.BlockSpec((1, D), lambda p, i, xp: (0, 0))] + [vmem] * n,
        out_specs=[pl.BlockSpec((tm, half), proj_tile), pl.BlockSpec((tm, D), x_tile)] + [hbm] * n,
        scratch_shapes=[pltpu.VMEM(a.shape, BF16) for a in shards] + [pltpu.VMEM(g, BF16) for g in gathered]
        + [pltpu.SemaphoreType.DMA((n, 7)), pltpu.SemaphoreType.DMA((n, 7)), pltpu.SemaphoreType.DMA((n,)),
           pltpu.SemaphoreType.DMA((n,)), pltpu.VMEM((S, D), BF16)])
    return pl.pallas_call(
        body, name="gather_proj", grid_spec=grid_spec,
        out_shape=[pltpu.HBM((S, N), F32), pltpu.HBM((S, D), BF16)] + [pltpu.HBM(g, BF16) for g in gathered],
        compiler_params=_params(56, ("arbitrary", "arbitrary")),
    )(xpos, *_hbm(x, gain), *shards)


ROW_NORM, ROW_MEM_NORM, ROW_V_GAIN, ROW_B, ROW_ATTN_GAINS, ROW_MEM_GAINS, ROW_W_S, ROW_LOSS = 0, 8, 16, 18, 22, 23, 24, 536
SMALL_ROWS = 544


def _gather_small(dgain, dmgain, dvg, db2, dqg, dkg, dmqg, dmkg, dws, sq):
    def body(dgain_ref, dmgain_ref, dvg_ref, db2_ref, dqg_ref, dkg_ref, dmqg_ref, dmkg_ref, dws_ref, sq_ref,
             out_ref, mine, send_sems, recv_sems, local_sems):
        first = lax.broadcasted_iota(jnp.int32, (1, 128), 1) < HEAD_DIM
        for i in range(8):
            cols = slice(128 * i, 128 * (i + 1))
            mine[ROW_NORM + i:ROW_NORM + i + 1, :] = dgain_ref[:, cols]
            mine[ROW_MEM_NORM + i:ROW_MEM_NORM + i + 1, :] = dmgain_ref[:, cols]
            mine[ROW_LOSS + i:ROW_LOSS + i + 1, :] = sq_ref[:, cols]
        mine[ROW_V_GAIN:ROW_V_GAIN + 1, :] = dvg_ref[:, 0:128]
        mine[ROW_V_GAIN + 1:ROW_V_GAIN + 2, :] = dvg_ref[:, 128:256]
        bt = db2_ref[...].T
        for h in range(4):
            mine[ROW_B + h:ROW_B + h + 1, :] = bt[HEAD_DIM * h:HEAD_DIM * h + 1, :]

        def fold_heads(t):
            return t + pltpu.roll(t, HEAD_DIM, axis=1)
        aq = fold_heads(dqg_ref[0] + dqg_ref[1] + dqg_ref[2] + dqg_ref[3])
        ak = fold_heads(dkg_ref[0] + dkg_ref[1] + dkg_ref[2] + dkg_ref[3])
        mine[ROW_ATTN_GAINS:ROW_ATTN_GAINS + 1, :] = jnp.where(first, aq, ak)
        mq = fold_heads(dmqg_ref[:, 0:128] + dmqg_ref[:, 128:256])
        mk = fold_heads(dmkg_ref[:, 0:128] + dmkg_ref[:, 128:256])
        mine[ROW_MEM_GAINS:ROW_MEM_GAINS + 1, :] = jnp.where(first, mq, mk)
        mine[ROW_W_S:ROW_W_S + 4 * CHUNK, :] = dws_ref[...]
        _AllGather([mine], [out_ref], send_sems, recv_sems, local_sems).run()

    return pl.pallas_call(
        body, name="gather_small_grads",
        out_shape=jax.ShapeDtypeStruct((N_DEV * SMALL_ROWS, 128), F32),
        scratch_shapes=[pltpu.VMEM((SMALL_ROWS, 128), F32), pltpu.SemaphoreType.DMA((1, 7)),
                        pltpu.SemaphoreType.DMA((1, 7)), pltpu.SemaphoreType.DMA((1,))],
        compiler_params=_params(16),
    )(dgain, dmgain, dvg, db2, dqg, dkg, dmqg, dmkg, dws, sq)


def _reduce_scatter_scratch(arrs):
    n = len(arrs)
    return ([pltpu.VMEM((4,) + a.shape[1:], BF16) for a in arrs] + [pltpu.VMEM((3,) + a.shape[1:], BF16) for a in arrs]
            + [pltpu.SemaphoreType.DMA((n, 7)), pltpu.SemaphoreType.DMA((n, 7))])


class _ReduceScatter:
    def __init__(self, ins, outs, *scratch):
        n = len(ins)
        self.n, self.ins, self.outs = n, ins, outs
        self.half, self.quarter = scratch[:n], scratch[n:2 * n]
        self.send_sems, self.recv_sems = scratch[2 * n:]

    def _to_sibling(self):
        x, y, c, _ = _place()
        return [pltpu.make_async_remote_copy(
            src_ref=self.ins[a].at[2 * q + (1 - c)], dst_ref=self.half[a].at[q], send_sem=self.send_sems.at[a, q],
            recv_sem=self.recv_sems.at[a, q], device_id=(x, y, 1 - c), device_id_type=MESH)
            for a in range(self.n) for q in range(4)]

    def _to_chips(self):
        _, _, c, chips = _place()
        return [pltpu.make_async_remote_copy(
            src_ref=self.half[a].at[2 * chip[0] + chip[1]], dst_ref=self.quarter[a].at[k],
            send_sem=self.send_sems.at[a, 4 + k], recv_sem=self.recv_sems.at[a, 4 + k], device_id=(*chip, c),
            device_id_type=MESH) for a in range(self.n) for k, chip in enumerate(chips)]

    def _rows(self, a, fn):
        m = self.ins[a].shape[1]
        tr = _row_step(m)

        def step(i, carry):
            fn(pl.ds(pl.multiple_of(i * tr, tr), tr))
            return carry
        lax.fori_loop(0, m // tr, step, 0)

    def start(self):
        for cp in self._to_sibling():
            cp.start()

    def middle(self):
        _, _, c, _ = _place()
        for cp in self._to_sibling():
            cp.wait_recv()
        for a in range(self.n):
            for q in range(4):
                def add_half(rows, a=a, q=q):
                    both = self.ins[a][2 * q + c, rows, :].astype(F32) + self.half[a][q, rows, :].astype(F32)
                    self.half[a][q, rows, :] = both.astype(BF16)
                self._rows(a, add_half)
        for cp in self._to_chips():
            cp.start()

    def finish(self):
        x, y, _, _ = _place()
        for cp in self._to_chips():
            cp.wait_recv()
        for a in range(self.n):
            def add_quarters(rows, a=a):
                f = lambda t: t.astype(F32)
                self.outs[a][rows, :] = ((f(self.half[a][2 * x + y, rows, :]) + f(self.quarter[a][0, rows, :]))
                                         + (f(self.quarter[a][1, rows, :]) + f(self.quarter[a][2, rows, :])))
            self._rows(a, add_quarters)
        for cp in self._to_sibling() + self._to_chips():
            cp.wait_send()


def _adamw_math(w, g, m, v):
    m = ADAM_B1 * m + (1.0 - ADAM_B1) * g
    v = ADAM_B2 * v + (1.0 - ADAM_B2) * (g * g)
    m_hat = m / (1.0 - ADAM_B1 ** ADAM_STEP)
    v_hat = v / (1.0 - ADAM_B2 ** ADAM_STEP)
    delta = -ADAM_LR * (m_hat / (jnp.sqrt(v_hat) + ADAM_EPS) + ADAM_WD * w)
    return delta, m, v


def _adamw(w, g, m, v, name):
    R, C = w.shape
    tr = _row_step(R)

    def body(w_ref, g_ref, m_ref, v_ref, d_ref, nm_ref, nv_ref):
        d_ref[...], nm_ref[...], nv_ref[...] = _adamw_math(w_ref[...], g_ref[...], m_ref[...], v_ref[...])

    tile = pl.BlockSpec((tr, C), lambda i: (i, 0))
    out = pltpu.HBM((R, C), F32)
    return pl.pallas_call(
        body, name=name, grid=(R // tr,), in_specs=[tile] * 4, out_specs=[tile] * 3, out_shape=[out] * 3,
        compiler_params=_params(16, ("arbitrary",)),
    )(*_hbm(w, g, m, v))


SMALL = ("norm_gain", "gmlp_v_gain", "gmlp_w_s", "gmlp_b", "attn_q_gain", "attn_k_gain", "mem_norm_gain",
         "mem_q_gain", "mem_k_gain")
WEIGHTS = ("norm_gain", "w_in", "gmlp_v_gain", "gmlp_w_s", "gmlp_b", "attn_q_gain", "attn_k_gain",
           "mem_norm_gain", "w_mem_kv", "mem_q_gain", "mem_k_gain", "w_out")


def _adamw_small(w, m, v, g_all):
    k = len(SMALL)
    half = slice(0, HEAD_DIM), slice(HEAD_DIM, 2 * HEAD_DIM)

    def body(*refs):
        w_refs, m_refs, v_refs = refs[:k], refs[k:2 * k], refs[2 * k:3 * k]
        g_ref = refs[3 * k]
        outs = refs[3 * k + 1:7 * k + 1]
        loss_ref, gsum = refs[7 * k + 1:]

        part = SMALL_ROWS // 4
        for p in range(4):
            acc = g_ref[part * p:part * (p + 1), :]
            for dev in range(1, N_DEV):
                acc = acc + g_ref[dev * SMALL_ROWS + part * p:dev * SMALL_ROWS + part * (p + 1), :]
            gsum[part * p:part * (p + 1), :] = acc

        def update(name, at, g):
            i = SMALL.index(name)
            d, nm, nv = _adamw_math(w_refs[i][at], g, m_refs[i][at], v_refs[i][at])
            outs[i][at], outs[k + i][at], outs[2 * k + i][at], outs[3 * k + i][at] = g, d, nm, nv

        for i in range(8):
            at = (slice(0, 1), slice(128 * i, 128 * (i + 1)))
            update("norm_gain", at, gsum[ROW_NORM + i:ROW_NORM + i + 1, :])
            update("mem_norm_gain", at, gsum[ROW_MEM_NORM + i:ROW_MEM_NORM + i + 1, :])
        for h in range(4):
            row = (0, slice(h, h + 1), slice(None))
            update("gmlp_v_gain", row, gsum[ROW_V_GAIN + h // 2:ROW_V_GAIN + h // 2 + 1, half[h % 2]])
            update("gmlp_b", row, gsum[ROW_B + h:ROW_B + h + 1, :])
            update("gmlp_w_s", (0, h), gsum[ROW_W_S + CHUNK * h:ROW_W_S + CHUNK * (h + 1), :])
        whole = (slice(0, 1), slice(None))
        update("attn_q_gain", whole, gsum[ROW_ATTN_GAINS:ROW_ATTN_GAINS + 1, half[0]])
        update("attn_k_gain", whole, gsum[ROW_ATTN_GAINS:ROW_ATTN_GAINS + 1, half[1]])
        update("mem_q_gain", whole, gsum[ROW_MEM_GAINS:ROW_MEM_GAINS + 1, half[0]])
        update("mem_k_gain", whole, gsum[ROW_MEM_GAINS:ROW_MEM_GAINS + 1, half[1]])
        loss_ref[...] = jnp.sum(gsum[ROW_LOSS:ROW_LOSS + 8, :], keepdims=True) * (0.5 / D_MODEL)

    shapes = [jax.ShapeDtypeStruct(w[name].shape, F32) for name in SMALL]
    res = pl.pallas_call(
        body, name="adamw_small",
        out_shape=shapes * 4 + [jax.ShapeDtypeStruct((1, 1), F32)],
        scratch_shapes=[pltpu.VMEM((SMALL_ROWS, 128), F32)],
        compiler_params=_params(16),
    )(*[w[n] for n in SMALL], *[m[n] for n in SMALL], *[v[n] for n in SMALL], g_all)
    trees = [dict(zip(SMALL, res[j * k:(j + 1) * k])) for j in range(4)]
    return (*trees, res[4 * k])


def _grads(x, mem, tgt, w, shards):
    bd128, bd256 = _head_blockdiag(128), _head_blockdiag(256)
    gain = w["norm_gain"].reshape(1, D_MODEL)
    vg = w["gmlp_v_gain"].reshape(1, GMLP_WIDTH)
    w_s = w["gmlp_w_s"].reshape(4, CHUNK, CHUNK)
    b2 = jnp.repeat(w["gmlp_b"].reshape(4, CHUNK).T, HEAD_DIM, axis=1)
    qg2 = jnp.tile(w["attn_q_gain"].reshape(1, HEAD_DIM), (1, 2))
    kg2 = jnp.tile(w["attn_k_gain"].reshape(1, HEAD_DIM), (1, 2))
    mqg4 = jnp.tile(w["mem_q_gain"].reshape(1, HEAD_DIM), (1, 4))
    mkg4 = jnp.tile(w["mem_k_gain"].reshape(1, HEAD_DIM), (1, 4))
    mgain = w["mem_norm_gain"].reshape(1, D_MODEL)

    xpos = lax.axis_index("x").astype(jnp.int32).reshape(1)
    proj, h_bf, win_t, wkv_bf, wout_bf = _gather_proj(x, gain, shards, xpos)
    yg = _gmlp_fwd(proj, vg, w_s, b2, bd256)
    ya, att, lse = _attn_fwd(proj, qg2, kg2, bd128)
    hm_bf, kraw, mk, mv = _mem_kv(mem, mgain, wkv_bf, mkg4, bd256)
    ym, om = _mem_fwd(proj, mk, mv, mqg4, bd256)
    dout, dycat, dwout, sq = _out_loss(yg, ya, ym, x, tgt, wout_bf)

    du, dgv, dgg, dws, db2, dvg = _gmlp_bwd(proj, dycat, vg, w_s, b2, bd256)
    dq, dk, dv, dag, dqg, dkg = _attn_bwd(proj, dycat, att, lse, qg2, kg2, bd128)
    dmq, dmg, dmk, dmv, dmqg = _mem_bwd(proj, dycat, om, mk, mv, mqg4, bd256)
    dwkv, dmgain, dmkg = _mem_kv_bwd(dmk, dmv, kraw, mem, mgain, mkg4, wkv_bf, hm_bf, bd256)
    pieces = [du, dgv, dgg, dq, dk, dv, dag, dmq, dmg]
    dwin, g_wkv, g_wout = _in_bwd_dw(pieces, h_bf, [dwkv, dwout])
    grad_x, dgain, g_win = _in_bwd_dx(pieces, x, dout, gain, win_t, dwin)
    return grad_x, g_win, g_wkv, g_wout, (dgain, dmgain, dvg, db2, dqg, dkg, dmqg, dmkg, dws, sq)


def kernel(x, mem, norm_gain, w_in, gmlp_v_gain, gmlp_w_s, gmlp_b, attn_q_gain, attn_k_gain, mem_norm_gain, w_mem_kv, mem_q_gain, mem_k_gain, w_out, loss_target, m_norm_gain, m_w_in, m_gmlp_v_gain, m_gmlp_w_s, m_gmlp_b, m_attn_q_gain, m_attn_k_gain, m_mem_norm_gain, m_w_mem_kv, m_mem_q_gain, m_mem_k_gain, m_w_out, v_norm_gain, v_w_in, v_gmlp_v_gain, v_gmlp_w_s, v_gmlp_b, v_attn_q_gain, v_attn_k_gain, v_mem_norm_gain, v_w_mem_kv, v_mem_q_gain, v_mem_k_gain, v_w_out):
    w = dict(norm_gain=norm_gain, w_in=w_in, gmlp_v_gain=gmlp_v_gain, gmlp_w_s=gmlp_w_s, gmlp_b=gmlp_b,
             attn_q_gain=attn_q_gain, attn_k_gain=attn_k_gain, mem_norm_gain=mem_norm_gain, w_mem_kv=w_mem_kv,
             mem_q_gain=mem_q_gain, mem_k_gain=mem_k_gain, w_out=w_out)
    m = dict(norm_gain=m_norm_gain, w_in=m_w_in, gmlp_v_gain=m_gmlp_v_gain, gmlp_w_s=m_gmlp_w_s, gmlp_b=m_gmlp_b,
             attn_q_gain=m_attn_q_gain, attn_k_gain=m_attn_k_gain, mem_norm_gain=m_mem_norm_gain,
             w_mem_kv=m_w_mem_kv, mem_q_gain=m_mem_q_gain, mem_k_gain=m_mem_k_gain, w_out=m_w_out)
    v = dict(norm_gain=v_norm_gain, w_in=v_w_in, gmlp_v_gain=v_gmlp_v_gain, gmlp_w_s=v_gmlp_w_s, gmlp_b=v_gmlp_b,
             attn_q_gain=v_attn_q_gain, attn_k_gain=v_attn_k_gain, mem_norm_gain=v_mem_norm_gain,
             w_mem_kv=v_w_mem_kv, mem_q_gain=v_mem_q_gain, mem_k_gain=v_mem_k_gain, w_out=v_w_out)
    transposed = lambda t: jnp.transpose(t[0])

    grad_x, g_win, g_wkv, g_wout, small = _grads(
        x[0], mem[0], loss_target[0], w, [transposed(w_in), w_mem_kv[0], w_out[0]])
    small_all = _gather_small(*small)

    out_g, out_d, out_m, out_v, loss = _adamw_small(w, m, v, small_all)
    d_, m_, v_ = _adamw(transposed(w_in), g_win, transposed(m_w_in), transposed(v_w_in), "adamw_w_in")
    for tree, t in ((out_g, g_win), (out_d, d_), (out_m, m_), (out_v, v_)):
        tree["w_in"] = jnp.transpose(t)[None]
    for name, g in (("w_mem_kv", g_wkv), ("w_out", g_wout)):
        d_, m_, v_ = _adamw(w[name][0], g, m[name][0], v[name][0], "adamw_" + name)
        out_g[name], out_d[name], out_m[name], out_v[name] = g[None], d_[None], m_[None], v_[None]

    return (loss.reshape(()), grad_x[None], *[out_g[k] for k in WEIGHTS], *[out_d[k] for k in WEIGHTS],
            *[out_m[k] for k in WEIGHTS], *[out_v[k] for k in WEIGHTS])
```

```python
import functools
import math

import jax
import jax.numpy as jnp
from jax import lax
from jax.experimental import pallas as pl
from jax.experimental.pallas import tpu as pltpu

F32 = jnp.float32
BF16 = jnp.bfloat16

N_DEV = 8
D_MODEL = 1024
HEAD_DIM = 64
GMLP_WIDTH = 256
ATTN_WIDTH = 512
MEM_WIDTH = 256
MEM_LEN = 256
IN_WIDTH = 3 * GMLP_WIDTH + 4 * ATTN_WIDTH + 2 * MEM_WIDTH
CHUNK = 128
BLOCK = 128
DILATIONS = (1, 4, 16)
EPS = 1e-6
SCALE = 1.0 / math.sqrt(HEAD_DIM)
NEG = -1e30

ADAM_LR = 0.001
ADAM_B1 = 0.9
ADAM_B2 = 0.999
ADAM_EPS = 1e-08
ADAM_WD = 0.01
ADAM_STEP = 10

MIB = 1024 * 1024
MESH = pl.DeviceIdType.MESH

COL_AQ, COL_AK, COL_AV, COL_AG = 6, 10, 14, 18


def _params(vmem_mib, semantics=None):
    kw = dict(vmem_limit_bytes=vmem_mib * MIB)
    if semantics is not None:
        kw["dimension_semantics"] = semantics
    return pltpu.CompilerParams(**kw)


def _hbm(*arrs):
    return [pltpu.with_memory_space_constraint(a, pltpu.HBM) for a in arrs]


def _split_dot(x, sel_bf):
    hi = x.astype(BF16)
    lo = (x - hi.astype(F32)).astype(BF16)
    return jnp.dot(hi, sel_bf, preferred_element_type=F32) + jnp.dot(lo, sel_bf, preferred_element_type=F32)


def _nt(a, b):
    return lax.dot_general(a, b, (((1,), (1,)), ((), ())), preferred_element_type=F32)


def _tn(a, b):
    return lax.dot_general(a, b, (((0,), (0,)), ((), ())), preferred_element_type=F32)


def _silu_parts(g):
    sg = jax.nn.sigmoid(g)
    return g * sg, sg * (1.0 + g * (1.0 - sg))


def _head_index(shape):
    return lax.shift_right_logical(lax.broadcasted_iota(jnp.int32, shape, 1), HEAD_DIM.bit_length() - 1)


def _head_blockdiag(width):
    i = jnp.arange(width) // HEAD_DIM
    return (i[:, None] == i[None, :]).astype(BF16)


def _gmlp_masked_weights(ws_ref, transpose):
    t = lax.broadcasted_iota(jnp.int32, (CHUNK, CHUNK), 0)
    s = lax.broadcasted_iota(jnp.int32, (CHUNK, CHUNK), 1)
    parts = []
    for h in range(4):
        wm = jnp.where(s <= t, ws_ref[h], 0.0)
        parts.append(wm.T if transpose else wm)
    return jnp.concatenate(parts, axis=1).astype(BF16)


def _head_stack(v, head):
    return jnp.concatenate([jnp.where(head == h, v, 0.0) for h in range(4)], axis=0).astype(BF16)


def _gmlp_fwd(proj, vg, w_s, b2, bd):
    S = proj.shape[0]
    tm = 512

    def body(u_ref, v_ref, g_ref, vg_ref, ws_ref, b2_ref, bd_ref, y_ref):
        v = v_ref[...]
        ms = _split_dot(v * v, bd_ref[...]) * (1.0 / HEAD_DIM)
        vn = (v * lax.rsqrt(ms + EPS)) * vg_ref[...]
        wcat = _gmlp_masked_weights(ws_ref, False)
        head = _head_index((CHUNK, GMLP_WIDTH))
        for c in range(tm // CHUNK):
            rows = slice(c * CHUNK, (c + 1) * CHUNK)
            sp = jnp.dot(wcat, _head_stack(vn[rows], head), preferred_element_type=F32) + b2_ref[...]
            silu, _ = _silu_parts(g_ref[rows, :])
            y_ref[rows, :] = ((u_ref[rows, :] * sp) * silu).astype(BF16)

    col = lambda j: pl.BlockSpec((tm, GMLP_WIDTH), lambda i, j=j: (i, j))
    const = lambda shape: pl.BlockSpec(shape, lambda i: (0,) * len(shape))
    return pl.pallas_call(
        body, name="gmlp_fwd", grid=(S // tm,),
        in_specs=[col(0), col(1), col(2), const((1, GMLP_WIDTH)), const((4, CHUNK, CHUNK)),
                  const((CHUNK, GMLP_WIDTH)), const((GMLP_WIDTH, GMLP_WIDTH))],
        out_specs=pl.BlockSpec((tm, GMLP_WIDTH), lambda i: (i, 0)),
        out_shape=pltpu.HBM((S, GMLP_WIDTH), BF16),
        compiler_params=_params(24, ("arbitrary",)),
    )(*_hbm(proj, proj, proj, vg, w_s, b2, bd))


def _gmlp_bwd(proj, dycat, vg, w_s, b2, bd):
    S = proj.shape[0]
    tm = 512
    nsteps = S // tm

    def body(u_ref, v_ref, g_ref, dy_ref, vg_ref, ws_ref, b2_ref, bd_ref,
             du_ref, dv_ref, dg_ref, dws_ref, db2_ref, dvg_ref):
        i = pl.program_id(0)

        @pl.when(i == 0)
        def _():
            dws_ref[...] = jnp.zeros_like(dws_ref)
            db2_ref[...] = jnp.zeros_like(db2_ref)
            dvg_ref[...] = jnp.zeros_like(dvg_ref)

        bdv = bd_ref[...]
        v = v_ref[...]
        ms = _split_dot(v * v, bdv) * (1.0 / HEAD_DIM)
        rv = lax.rsqrt(ms + EPS)
        xhat = v * rv
        vgv = vg_ref[...]
        vn = xhat * vgv
        wcat = _gmlp_masked_weights(ws_ref, False)
        wcat_t = _gmlp_masked_weights(ws_ref, True)
        head = _head_index((CHUNK, GMLP_WIDTH))
        dvg = jnp.zeros((1, GMLP_WIDTH), F32)
        for c in range(tm // CHUNK):
            rows = slice(c * CHUNK, (c + 1) * CHUNK)
            vn_c = vn[rows]
            spb = jnp.dot(wcat, _head_stack(vn_c, head), preferred_element_type=F32) + b2_ref[...]
            silu, dsilu = _silu_parts(g_ref[rows, :])
            dy = dy_ref[rows, :]
            u = u_ref[rows, :]
            du_ref[rows, :] = (dy * spb * silu).astype(BF16)
            dg_ref[rows, :] = (dy * u * spb * dsilu).astype(BF16)
            dsp = dy * u * silu
            db2_ref[...] += dsp
            dstack = _head_stack(dsp, head)
            dvn = jnp.dot(wcat_t, dstack, preferred_element_type=F32)
            dws_ref[...] += _nt(dstack, vn_c.astype(BF16))
            xh = xhat[rows]
            a = dvn * vgv
            mean_ax = _split_dot(a * xh, bdv) * (1.0 / HEAD_DIM)
            dv_ref[rows, :] = (rv[rows] * (a - xh * mean_ax)).astype(BF16)
            dvg = dvg + jnp.sum(dvn * xh, axis=0, keepdims=True)
        dvg_ref[...] += dvg

        @pl.when(i == nsteps - 1)
        def _():
            t = lax.broadcasted_iota(jnp.int32, (4 * CHUNK, CHUNK), 0) % CHUNK
            s = lax.broadcasted_iota(jnp.int32, (4 * CHUNK, CHUNK), 1)
            dws_ref[...] = jnp.where(s <= t, dws_ref[...], 0.0)
            db2_ref[...] = _split_dot(db2_ref[...], bdv)

    col = lambda j: pl.BlockSpec((tm, GMLP_WIDTH), lambda i, j=j: (i, j))
    const = lambda shape: pl.BlockSpec(shape, lambda i: (0,) * len(shape))
    tile = pl.BlockSpec((tm, GMLP_WIDTH), lambda i: (i, 0))
    piece = pltpu.HBM((S, GMLP_WIDTH), BF16)
    return pl.pallas_call(
        body, name="gmlp_bwd", grid=(nsteps,),
        in_specs=[col(0), col(1), col(2), col(0), const((1, GMLP_WIDTH)), const((4, CHUNK, CHUNK)),
                  const((CHUNK, GMLP_WIDTH)), const((GMLP_WIDTH, GMLP_WIDTH))],
        out_specs=[tile, tile, tile, const((4 * CHUNK, CHUNK)), const((CHUNK, GMLP_WIDTH)), const((1, GMLP_WIDTH))],
        out_shape=[piece, piece, piece, pltpu.HBM((4 * CHUNK, CHUNK), F32),
                   pltpu.HBM((CHUNK, GMLP_WIDTH), F32), pltpu.HBM((1, GMLP_WIDTH), F32)],
        compiler_params=_params(32, ("arbitrary",)),
    )(*_hbm(proj, proj, proj, dycat, vg, w_s, b2, bd))


def _band_mask():
    qi = lax.broadcasted_iota(jnp.int32, (2 * BLOCK, 2 * BLOCK), 0) % BLOCK
    ki = lax.broadcasted_iota(jnp.int32, (2 * BLOCK, 2 * BLOCK), 1)
    return ((ki < BLOCK) & (ki >= qi)) | ((ki >= BLOCK) & ((ki - BLOCK) <= qi))


def _first_block_bias(blk, blocks_per_class):
    kcol = lax.broadcasted_iota(jnp.int32, (1, 2 * BLOCK), 1)
    kill = jnp.where((blk & (blocks_per_class - 1)) == 0, NEG, 0.0)
    return jnp.where(kcol < BLOCK, kill, 0.0)


def _two_heads(q, lo):
    zero = jnp.zeros_like(q)
    return jnp.concatenate([jnp.where(lo, q, zero), jnp.where(lo, zero, q)], axis=0)


def _block_tokens(blk, d, S):
    if d == 1:
        return pl.ds(pl.multiple_of(blk * BLOCK, BLOCK), BLOCK)
    blocks_per_class = S // d // BLOCK
    r = lax.shift_right_logical(blk, blocks_per_class.bit_length() - 1)
    n = blk & (blocks_per_class - 1)
    return pl.ds(r + n * (BLOCK * d), BLOCK, stride=d)


def _padded_block(blk):
    return pl.ds(pl.multiple_of((blk + 1) * BLOCK, BLOCK), BLOCK)


def _for_blocks(n_blocks, unroll, fn):
    def group(g, carry):
        for u in range(unroll):
            fn(g * unroll + u)
        return carry
    lax.fori_loop(0, n_blocks // unroll, group, 0)


def _attn_fwd(proj, qg2, kg2, bd):
    S = proj.shape[0]
    npairs = ATTN_WIDTH // 128
    tn = 512

    def body(q_ref, k_ref, v_ref, g_ref, qg_ref, kg_ref, bd_ref, y_ref, att_ref, lse_ref, qn, kn, kc, vc):
        bdv = bd_ref[...]
        lo = lax.broadcasted_iota(jnp.int32, (BLOCK, 128), 1) < HEAD_DIM
        band_mask = _band_mask()
        kc[pl.ds(0, BLOCK), :] = jnp.zeros((BLOCK, 128), BF16)
        vc[pl.ds(0, BLOCK), :] = jnp.zeros((BLOCK, 128), BF16)

        def norm_step(i, carry):
            rows = pl.ds(pl.multiple_of(i * tn, tn), tn)
            qv = q_ref[rows, :]
            kv = k_ref[rows, :]
            qn[rows, :] = (qv * lax.rsqrt(_split_dot(qv * qv, bdv) * (1.0 / HEAD_DIM) + EPS)) * (qg_ref[...] * SCALE)
            kn[rows, :] = (kv * lax.rsqrt(_split_dot(kv * kv, bdv) * (1.0 / HEAD_DIM) + EPS)) * kg_ref[...]
            return carry
        lax.fori_loop(0, S // tn, norm_step, 0)

        def fill(blk, d):
            tokens = _block_tokens(blk, d, S)
            kc[_padded_block(blk), :] = kn[tokens, :].astype(BF16)
            vc[_padded_block(blk), :] = v_ref[tokens, :].astype(BF16)

        ones_bf = jnp.ones((2 * BLOCK, 128), BF16)

        def block(blk, d):
            tokens = _block_tokens(blk, d, S)
            keys = pl.ds(pl.multiple_of(blk * BLOCK, BLOCK), 2 * BLOCK)
            q2 = _two_heads(qn[tokens, :].astype(BF16), lo)
            s = jnp.where(band_mask, _nt(q2, kc[keys, :]), NEG) + _first_block_bias(blk, S // d // BLOCK)
            m = jnp.max(s, axis=-1, keepdims=True)
            e = jnp.exp((s - m).astype(BF16))
            ol = jnp.dot(e, jnp.concatenate([vc[keys, :], ones_bf], axis=1), preferred_element_type=F32)
            l = ol[:, 128:]
            o2 = ol[:, :128] * (1.0 / l)
            lse2 = m + jnp.log(l)
            o = jnp.where(lo, o2[:BLOCK], o2[BLOCK:])
            lse = jnp.where(lo, lse2[:BLOCK], lse2[BLOCK:])
            if d > 1:
                la = lse_ref[tokens, :]
                mx = jnp.maximum(la, lse)
                wa, wb = jnp.exp(la - mx), jnp.exp(lse - mx)
                t = wa + wb
                o = (wa * att_ref[tokens, :] + wb * o) / t
                lse = mx + jnp.log(t)
            att_ref[tokens, :] = o
            lse_ref[tokens, :] = lse

        for d in DILATIONS:
            _for_blocks(S // BLOCK, 4, functools.partial(fill, d=d))
            _for_blocks(S // BLOCK, 16, functools.partial(block, d=d))

        def gate_step(i, carry):
            rows = pl.ds(pl.multiple_of(i * tn, tn), tn)
            silu, _ = _silu_parts(g_ref[rows, :])
            y_ref[rows, :] = (att_ref[rows, :] * silu).astype(BF16)
            return carry
        lax.fori_loop(0, S // tn, gate_step, 0)

    col = lambda j0: pl.BlockSpec((S, 128), lambda p, j0=j0: (0, j0 + p))
    const = lambda shape: pl.BlockSpec(shape, lambda p: (0,) * len(shape))
    out = pl.BlockSpec((S, 128), lambda p: (0, p))
    return pl.pallas_call(
        body, name="attn_fwd", grid=(npairs,),
        in_specs=[col(COL_AQ), col(COL_AK), col(COL_AV), col(COL_AG), const((1, 128)), const((1, 128)),
                  const((128, 128))],
        out_specs=[out, out, out],
        out_shape=[pltpu.HBM((S, ATTN_WIDTH), BF16), pltpu.HBM((S, ATTN_WIDTH), F32),
                   pltpu.HBM((S, ATTN_WIDTH), F32)],
        scratch_shapes=[pltpu.VMEM((S, 128), F32), pltpu.VMEM((S, 128), F32),
                        pltpu.VMEM((S + BLOCK, 128), BF16), pltpu.VMEM((S + BLOCK, 128), BF16)],
        compiler_params=_params(48, ("arbitrary",)),
    )(*_hbm(proj, proj, proj, proj, qg2, kg2, bd))


def _attn_bwd(proj, dycat, att, lse, qg2, kg2, bd):
    S = proj.shape[0]
    npairs = ATTN_WIDTH // 128
    tn = 512

    def body(q_ref, k_ref, v_ref, g_ref, dy_ref, att_ref, lse_ref, qg_ref, kg_ref, bd_ref,
             dq_ref, dk_ref, dv_ref, dg_ref, dqg_ref, dkg_ref,
             qn, kn, rq_s, rk_s, kc, vc, do_s, dd_s, dqa, dka, dva):
        bdv = bd_ref[...]
        lo = lax.broadcasted_iota(jnp.int32, (BLOCK, 128), 1) < HEAD_DIM
        kc[pl.ds(0, BLOCK), :] = jnp.zeros((BLOCK, 128), BF16)
        vc[pl.ds(0, BLOCK), :] = jnp.zeros((BLOCK, 128), BF16)

        def prepare(i, carry):
            rows = pl.ds(pl.multiple_of(i * tn, tn), tn)
            qv = q_ref[rows, :]
            kv = k_ref[rows, :]
            rq = lax.rsqrt(_split_dot(qv * qv, bdv) * (1.0 / HEAD_DIM) + EPS)
            rk = lax.rsqrt(_split_dot(kv * kv, bdv) * (1.0 / HEAD_DIM) + EPS)
            rq_s[rows, :] = rq
            rk_s[rows, :] = rk
            qn[rows, :] = (qv * rq) * (qg_ref[...] * SCALE)
            kn[rows, :] = (kv * rk) * kg_ref[...]
            silu, dsilu = _silu_parts(g_ref[rows, :])
            dy = dy_ref[rows, :]
            at = att_ref[rows, :]
            do = dy * silu
            do_s[rows, :] = do
            dd_s[rows, :] = _split_dot(do * at, bdv)
            dg_ref[rows, :] = (dy * at * dsilu).astype(BF16)
            dka[rows, :] = jnp.zeros((tn, 128), F32)
            dva[rows, :] = jnp.zeros((tn, 128), F32)
            return carry
        lax.fori_loop(0, S // tn, prepare, 0)

        kt = lax.broadcasted_iota(jnp.int32, (2 * BLOCK, 2 * BLOCK), 0)
        qt = lax.broadcasted_iota(jnp.int32, (2 * BLOCK, 2 * BLOCK), 1) % BLOCK
        band_mask_t = ((kt < BLOCK) & (kt >= qt)) | ((kt >= BLOCK) & ((kt - BLOCK) <= qt))

        def per_query_row(t):
            tt = t.T
            return jnp.concatenate([tt[0:1, :], tt[HEAD_DIM:HEAD_DIM + 1, :]], axis=1)

        def fill(blk, d):
            tokens = _block_tokens(blk, d, S)
            kc[_padded_block(blk), :] = kn[tokens, :].astype(BF16)
            vc[_padded_block(blk), :] = v_ref[tokens, :].astype(BF16)

        def block(blk, d):
            tokens = _block_tokens(blk, d, S)
            keys = pl.ds(pl.multiple_of(blk * BLOCK, BLOCK), 2 * BLOCK)
            first = (blk & (S // d // BLOCK - 1)) == 0
            q2 = _two_heads(qn[tokens, :].astype(BF16), lo)
            do2 = _two_heads(do_s[tokens, :].astype(BF16), lo)
            lse_row = per_query_row(lse_ref[tokens, :])
            dd_row = per_query_row(dd_s[tokens, :])
            kb = kc[keys, :]
            vb = vc[keys, :]
            st = jnp.where(band_mask_t, _nt(kb, q2), NEG)
            st = jnp.concatenate([st[:BLOCK] + jnp.where(first, NEG, 0.0), st[BLOCK:]], axis=0)
            pt = jnp.exp(st - lse_row)
            dst = pt * (_nt(vb, do2) - dd_row)
            ptb = pt.astype(BF16)
            dstb = dst.astype(BF16)
            dv_band = jnp.dot(ptb, do2, preferred_element_type=F32)
            dk_band = jnp.dot(dstb, q2, preferred_element_type=F32)
            before = _block_tokens(jnp.where(first, blk, blk - 1), d, S)
            dka[before, :] = dka[before, :] + dk_band[:BLOCK]
            dva[before, :] = dva[before, :] + dv_band[:BLOCK]
            dka[tokens, :] = dka[tokens, :] + dk_band[BLOCK:]
            dva[tokens, :] = dva[tokens, :] + dv_band[BLOCK:]
            dq2 = _tn(dstb, kb)
            dq = jnp.where(lo, dq2[:BLOCK], dq2[BLOCK:])
            dqa[tokens, :] = dq if d == 1 else dqa[tokens, :] + dq

        for d in DILATIONS:
            _for_blocks(S // BLOCK, 4, functools.partial(fill, d=d))
            _for_blocks(S // BLOCK, 8, functools.partial(block, d=d))

        def out_step(i, carry):
            dqg, dkg = carry
            rows = pl.ds(pl.multiple_of(i * tn, tn), tn)
            rq = rq_s[rows, :]
            rk = rk_s[rows, :]
            qh = q_ref[rows, :] * rq
            kh = k_ref[rows, :] * rk
            dqs = dqa[rows, :] * SCALE
            dkn = dka[rows, :]
            aq = dqs * qg_ref[...]
            ak = dkn * kg_ref[...]
            dq_ref[rows, :] = (rq * (aq - qh * (_split_dot(aq * qh, bdv) * (1.0 / HEAD_DIM)))).astype(BF16)
            dk_ref[rows, :] = (rk * (ak - kh * (_split_dot(ak * kh, bdv) * (1.0 / HEAD_DIM)))).astype(BF16)
            dv_ref[rows, :] = dva[rows, :].astype(BF16)
            dqg = dqg + jnp.sum(dqs * qh, axis=0, keepdims=True)
            dkg = dkg + jnp.sum(dkn * kh, axis=0, keepdims=True)
            return dqg, dkg
        zero = jnp.zeros((1, 128), F32)
        dqg, dkg = lax.fori_loop(0, S // tn, out_step, (zero, zero))
        dqg_ref[0] = dqg
        dkg_ref[0] = dkg

    col = lambda j0: pl.BlockSpec((S, 128), lambda p, j0=j0: (0, j0 + p))
    col1 = lambda j0: pl.BlockSpec((S, 128), lambda p, j0=j0: (0, j0 + p), pipeline_mode=pl.Buffered(1))
    const = lambda shape: pl.BlockSpec(shape, lambda p: (0,) * len(shape))
    out = pl.BlockSpec((S, 128), lambda p: (0, p))
    gain_out = pl.BlockSpec((1, 1, 128), lambda p: (p, 0, 0))
    piece = pltpu.HBM((S, ATTN_WIDTH), BF16)
    gains = pltpu.HBM((npairs, 1, 128), F32)
    f32buf = pltpu.VMEM((S, 128), F32)
    bf16pad = pltpu.VMEM((S + BLOCK, 128), BF16)
    return pl.pallas_call(
        body, name="attn_bwd", grid=(npairs,),
        in_specs=[col(COL_AQ), col(COL_AK), col(COL_AV), col1(COL_AG), col1(GMLP_WIDTH // 128), col1(0), col(0),
                  const((1, 128)), const((1, 128)), const((128, 128))],
        out_specs=[out, out, out, out, gain_out, gain_out],
        out_shape=[piece, piece, piece, piece, gains, gains],
        scratch_shapes=[f32buf, f32buf, f32buf, f32buf, bf16pad, bf16pad, f32buf, f32buf, f32buf, f32buf, f32buf],
        compiler_params=_params(60, ("arbitrary",)),
    )(*_hbm(proj, proj, proj, proj, dycat, att, lse, qg2, kg2, bd))


def _mem_kv(mem, gain, wkv_bf, kg4, bd):
    def body(mem_ref, g_ref, w_ref, kg_ref, bd_ref, hm_ref, kraw_ref, mk_ref, mv_ref):
        mv_ = mem_ref[...]
        r = lax.rsqrt(jnp.mean(mv_ * mv_, axis=-1, keepdims=True) + EPS)
        hm = ((mv_ * r) * g_ref[...]).astype(BF16)
        hm_ref[...] = hm
        kv = jnp.dot(hm, w_ref[...], preferred_element_type=F32)
        kraw = kv[:, :MEM_WIDTH]
        kraw_ref[...] = kraw
        ms = _split_dot(kraw * kraw, bd_ref[...]) * (1.0 / HEAD_DIM)
        mk_ref[...] = (kraw * lax.rsqrt(ms + EPS)) * kg_ref[...]
        mv_ref[...] = kv[:, MEM_WIDTH:]

    sq = jax.ShapeDtypeStruct((MEM_LEN, MEM_WIDTH), F32)
    return pl.pallas_call(
        body, name="mem_kv",
        out_shape=[jax.ShapeDtypeStruct((MEM_LEN, D_MODEL), BF16), sq, sq, sq],
        compiler_params=_params(16),
    )(mem, gain, wkv_bf, kg4, bd)


def _mem_fwd(proj, mk, mv, qg4, bd):
    S = proj.shape[0]
    tm = 512

    def body(q_ref, g_ref, mk_ref, mv_ref, qg_ref, bd_ref, y_ref, om_ref):
        qv = q_ref[...]
        ms = _split_dot(qv * qv, bd_ref[...]) * (1.0 / HEAD_DIM)
        qs = (qv * lax.rsqrt(ms + EPS)) * (qg_ref[...] * SCALE)
        mkb = mk_ref[...].astype(BF16)
        mvb = mv_ref[...].astype(BF16)
        head = _head_index((tm, MEM_WIDTH))
        o = jnp.zeros((tm, MEM_WIDTH), F32)
        for h in range(4):
            s = _nt(jnp.where(head == h, qs, 0.0).astype(BF16), mkb)
            e = jnp.exp(s - jnp.max(s, axis=-1, keepdims=True))
            p = e * (1.0 / jnp.sum(e, axis=-1, keepdims=True))
            o = jnp.where(head == h, jnp.dot(p.astype(BF16), mvb, preferred_element_type=F32), o)
        om_ref[...] = o
        silu, _ = _silu_parts(g_ref[...])
        y_ref[...] = (o * silu).astype(BF16)

    col = lambda j: pl.BlockSpec((tm, MEM_WIDTH), lambda i, j=j: (i, j))
    const = lambda shape: pl.BlockSpec(shape, lambda i: (0,) * len(shape))
    tile = pl.BlockSpec((tm, MEM_WIDTH), lambda i: (i, 0))
    return pl.pallas_call(
        body, name="mem_fwd", grid=(S // tm,),
        in_specs=[col(11), col(12), const((MEM_LEN, MEM_WIDTH)), const((MEM_LEN, MEM_WIDTH)), const((1, MEM_WIDTH)),
                  const((MEM_WIDTH, MEM_WIDTH))],
        out_specs=[tile, tile],
        out_shape=[pltpu.HBM((S, MEM_WIDTH), BF16), pltpu.HBM((S, MEM_WIDTH), F32)],
        compiler_params=_params(24, ("arbitrary",)),
    )(*_hbm(proj, proj, mk, mv, qg4, bd))


def _mem_bwd(proj, dycat, om, mk, mv, qg4, bd):
    S = proj.shape[0]
    tm = 512

    def body(q_ref, g_ref, dy_ref, om_ref, mk_ref, mv_ref, qg_ref, bd_ref,
             dq_ref, dg_ref, dmk_ref, dmv_ref, dqg_ref):
        i = pl.program_id(0)

        @pl.when(i == 0)
        def _():
            dmk_ref[...] = jnp.zeros_like(dmk_ref)
            dmv_ref[...] = jnp.zeros_like(dmv_ref)
            dqg_ref[...] = jnp.zeros_like(dqg_ref)

        bdv = bd_ref[...]
        qv = q_ref[...]
        rq = lax.rsqrt(_split_dot(qv * qv, bdv) * (1.0 / HEAD_DIM) + EPS)
        qh = qv * rq
        qs = qh * (qg_ref[...] * SCALE)
        silu, dsilu = _silu_parts(g_ref[...])
        dy = dy_ref[...]
        o = om_ref[...]
        do = dy * silu
        dg_ref[...] = (dy * o * dsilu).astype(BF16)
        dd = _split_dot(do * o, bdv)
        mkb = mk_ref[...].astype(BF16)
        mvb = mv_ref[...].astype(BF16)
        head = _head_index((tm, MEM_WIDTH))
        dqs = jnp.zeros((tm, MEM_WIDTH), F32)
        for h in range(4):
            qhd = jnp.where(head == h, qs, 0.0).astype(BF16)
            doh = jnp.where(head == h, do, 0.0).astype(BF16)
            s = _nt(qhd, mkb)
            e = jnp.exp(s - jnp.max(s, axis=-1, keepdims=True))
            p = e * (1.0 / jnp.sum(e, axis=-1, keepdims=True))
            ds = p * (_nt(doh, mvb) - dd[:, h * HEAD_DIM:h * HEAD_DIM + 1])
            dsb = ds.astype(BF16)
            dmv_ref[...] += _tn(p.astype(BF16), doh)
            dmk_ref[...] += _tn(dsb, qhd)
            dqs = jnp.where(head == h, jnp.dot(dsb, mkb, preferred_element_type=F32), dqs)
        dqs = dqs * SCALE
        a = dqs * qg_ref[...]
        dq_ref[...] = (rq * (a - qh * (_split_dot(a * qh, bdv) * (1.0 / HEAD_DIM)))).astype(BF16)
        dqg_ref[...] += jnp.sum(dqs * qh, axis=0, keepdims=True)

    col = lambda j: pl.BlockSpec((tm, MEM_WIDTH), lambda i, j=j: (i, j))
    const = lambda shape: pl.BlockSpec(shape, lambda i: (0,) * len(shape))
    tile = pl.BlockSpec((tm, MEM_WIDTH), lambda i: (i, 0))
    piece = pltpu.HBM((S, MEM_WIDTH), BF16)
    sq = pltpu.HBM((MEM_LEN, MEM_WIDTH), F32)
    return pl.pallas_call(
        body, name="mem_bwd", grid=(S // tm,),
        in_specs=[col(11), col(12), col(3), tile, const((MEM_LEN, MEM_WIDTH)), const((MEM_LEN, MEM_WIDTH)),
                  const((1, MEM_WIDTH)), const((MEM_WIDTH, MEM_WIDTH))],
        out_specs=[tile, tile, const((MEM_LEN, MEM_WIDTH)), const((MEM_LEN, MEM_WIDTH)), const((1, MEM_WIDTH))],
        out_shape=[piece, piece, sq, sq, pltpu.HBM((1, MEM_WIDTH), F32)],
        compiler_params=_params(32, ("arbitrary",)),
    )(*_hbm(proj, proj, dycat, om, mk, mv, qg4, bd))


def _mem_kv_bwd(dmk, dmv, kraw, mem, gain, kg4, wkv_bf, hm_bf, bd):
    def body(dmk_ref, dmv_ref, kraw_ref, mem_ref, g_ref, kg_ref, w_ref, hm_ref, bd_ref, dw_ref, dg_ref, dkg_ref):
        bdv = bd_ref[...]
        kraw = kraw_ref[...]
        rk = lax.rsqrt(_split_dot(kraw * kraw, bdv) * (1.0 / HEAD_DIM) + EPS)
        kh = kraw * rk
        dmkv = dmk_ref[...]
        a = dmkv * kg_ref[...]
        dkraw = rk * (a - kh * (_split_dot(a * kh, bdv) * (1.0 / HEAD_DIM)))
        dkg_ref[...] = jnp.sum(dmkv * kh, axis=0, keepdims=True)
        dkv = jnp.concatenate([dkraw, dmv_ref[...]], axis=1).astype(BF16)
        dw = _tn(hm_ref[...], dkv).astype(BF16)
        rows_blk = D_MODEL // N_DEV
        for j in range(N_DEV):
            dw_ref[j] = dw[rows_blk * j:rows_blk * (j + 1)]
        dhm = _nt(dkv, w_ref[...])
        mv_ = mem_ref[...]
        r = lax.rsqrt(jnp.mean(mv_ * mv_, axis=-1, keepdims=True) + EPS)
        dg_ref[...] = jnp.sum(dhm * (mv_ * r), axis=0, keepdims=True)

    return pl.pallas_call(
        body, name="mem_kv_bwd",
        out_shape=[jax.ShapeDtypeStruct((N_DEV, D_MODEL // N_DEV, 2 * MEM_WIDTH), BF16),
                   jax.ShapeDtypeStruct((1, D_MODEL), F32), jax.ShapeDtypeStruct((1, MEM_WIDTH), F32)],
        compiler_params=_params(24),
    )(dmk, dmv, kraw, mem, gain, kg4, wkv_bf, hm_bf, bd)


def _out_loss(yg, ya, ym, x, tgt, wout_bf):
    S, D = x.shape
    tm = 512
    nsteps = S // tm
    rows_blk = D // N_DEV

    def body(yg_ref, ya_ref, ym_ref, x_ref, t_ref, w_ref, dout_ref, dycat_ref, dw_ref, loss_ref, acc_ref):
        i = pl.program_id(0)

        @pl.when(i == 0)
        def _():
            acc_ref[...] = jnp.zeros_like(acc_ref)
            loss_ref[...] = jnp.zeros_like(loss_ref)

        ycat = jnp.concatenate([yg_ref[...], ya_ref[...], ym_ref[...]], axis=1)
        w = w_ref[...]
        diff = (x_ref[...] + jnp.dot(ycat, w, preferred_element_type=F32)) - t_ref[...]
        loss_ref[...] += jnp.sum(diff * diff, axis=0, keepdims=True)
        dout = diff * (1.0 / D)
        dout_ref[...] = dout
        db = dout.astype(BF16)
        dycat_ref[...] = _nt(db, w)
        acc_ref[...] += _tn(ycat, db)

        @pl.when(i == nsteps - 1)
        def _():
            for j in range(N_DEV):
                dw_ref[j] = acc_ref[rows_blk * j:rows_blk * (j + 1), :].astype(BF16)

    tile = lambda w: pl.BlockSpec((tm, w), lambda i: (i, 0))
    const = lambda shape: pl.BlockSpec(shape, lambda i: (0,) * len(shape))
    return pl.pallas_call(
        body, name="out_loss", grid=(nsteps,),
        in_specs=[tile(GMLP_WIDTH), tile(ATTN_WIDTH), tile(MEM_WIDTH), tile(D), tile(D), const((D, D))],
        out_specs=[tile(D), tile(D), const((N_DEV, rows_blk, D)), const((1, D))],
        out_shape=[pltpu.HBM((S, D), F32), pltpu.HBM((S, D), F32),
                   pltpu.HBM((N_DEV, rows_blk, D), BF16), pltpu.HBM((1, D), F32)],
        scratch_shapes=[pltpu.VMEM((D, D), F32)],
        compiler_params=_params(40, ("arbitrary",)),
    )(*_hbm(yg, ya, ym, x, tgt, wout_bf))


def _piece_specs(pieces, tm):
    return [pl.BlockSpec((tm, p.shape[1]), lambda i: (i, 0)) for p in pieces]


def _in_bwd_dx(pieces, x, dout, gain, w_t, dw_blocks):
    S, D = x.shape
    N = w_t.shape[0]
    tm = 512
    n = len(pieces)
    nsteps = S // tm
    middle_step = nsteps // 8

    def body(*refs):
        piece_refs = refs[:n]
        x_ref, dout_ref, g_ref, w_ref, dwb_ref, gx_ref, dg_ref, gw_ref = refs[n:n + 8]
        rs = _ReduceScatter([dwb_ref], [gw_ref], *refs[n + 8:])
        i = pl.program_id(0)

        @pl.when(i == 0)
        def _():
            dg_ref[...] = jnp.zeros_like(dg_ref)
            rs.start()

        @pl.when(i == middle_step)
        def _():
            rs.middle()

        dproj = jnp.concatenate([r[...] for r in piece_refs], axis=1)
        dh = jnp.dot(dproj, w_ref[...], preferred_element_type=F32)
        xv = x_ref[...]
        r = lax.rsqrt(jnp.mean(xv * xv, axis=-1, keepdims=True) + EPS)
        xh = xv * r
        a = dh * g_ref[...]
        gx_ref[...] = dout_ref[...] + r * (a - xh * jnp.mean(a * xh, axis=-1, keepdims=True))
        dg_ref[...] += jnp.sum(dh * xh, axis=0, keepdims=True)

        @pl.when(i == nsteps - 1)
        def _():
            rs.finish()

    tile = pl.BlockSpec((tm, D), lambda i: (i, 0))
    const = lambda shape: pl.BlockSpec(shape, lambda i: (0,) * len(shape))
    vmem = pl.BlockSpec(memory_space=pltpu.VMEM)
    return pl.pallas_call(
        body, name="in_bwd_dx", grid=(nsteps,),
        in_specs=_piece_specs(pieces, tm)
        + [tile, tile, const((1, D)), pl.BlockSpec((N, D), lambda i: (0, 0), pipeline_mode=pl.Buffered(1)), vmem],
        out_specs=[tile, const((1, D)), vmem],
        out_shape=[pltpu.HBM((S, D), F32), pltpu.HBM((1, D), F32), jax.ShapeDtypeStruct(dw_blocks.shape[1:], F32)],
        scratch_shapes=_reduce_scatter_scratch([dw_blocks]),
        compiler_params=_params(56, ("arbitrary",)),
    )(*_hbm(*pieces, x, dout, gain, w_t), dw_blocks)


def _in_bwd_dw(pieces, h_bf, others):
    S, D = h_bf.shape
    N = sum(p.shape[1] for p in pieces)
    n_blk = N // N_DEV
    tm = 512
    n = len(pieces)
    k = len(others)
    nsteps = S // tm

    def body(*refs):
        piece_refs = refs[:n]
        h_ref = refs[n]
        other_refs = refs[n + 1:n + 1 + k]
        dw_ref = refs[n + 1 + k]
        sum_refs = refs[n + 2 + k:n + 2 + 2 * k]
        acc_ref = refs[n + 2 + 2 * k]
        rs = _ReduceScatter(other_refs, sum_refs, *refs[n + 3 + 2 * k:])
        i = pl.program_id(0)

        @pl.when(i == 0)
        def _():
            acc_ref[...] = jnp.zeros_like(acc_ref)
            rs.start()

        @pl.when(i == 1)
        def _():
            rs.middle()

        dproj = jnp.concatenate([r[...] for r in piece_refs], axis=1)
        acc_ref[...] += _tn(h_ref[...], dproj)

        @pl.when(i == nsteps - 1)
        def _():
            for j in range(N_DEV):
                dw_ref[j] = acc_ref[:, n_blk * j:n_blk * (j + 1)].T.astype(BF16)
            rs.finish()

    vmem = pl.BlockSpec(memory_space=pltpu.VMEM)
    return pl.pallas_call(
        body, name="in_bwd_dw", grid=(nsteps,),
        in_specs=_piece_specs(pieces, tm) + [pl.BlockSpec((tm, D), lambda i: (i, 0))] + [vmem] * k,
        out_specs=[pl.BlockSpec((N_DEV, n_blk, D), lambda i: (0, 0, 0))] + [vmem] * k,
        out_shape=[pltpu.HBM((N_DEV, n_blk, D), BF16)] + [jax.ShapeDtypeStruct(o.shape[1:], F32) for o in others],
        scratch_shapes=[pltpu.VMEM((D, N), F32)] + _reduce_scatter_scratch(others),
        compiler_params=_params(56, ("arbitrary",)),
    )(*_hbm(*pieces, h_bf), *others)


def _row_step(m):
    return max(t for t in range(16, 257, 16) if m % t == 0)


def _place():
    x, y, c = lax.axis_index("x"), lax.axis_index("y"), lax.axis_index("c")
    chips = [(1 - x, y), (x, 1 - y), (1 - x, 1 - y)]
    return x, y, c, chips


class _AllGather:
    def __init__(self, srcs, outs, send_sems, recv_sems, local_sems, first_sem=0):
        self.srcs, self.outs, self.n, self.first_sem = srcs, outs, len(srcs), first_sem
        self.send_sems, self.recv_sems, self.local_sems = send_sems, recv_sems, local_sems

    def _rows(self, a, px, py, pc):
        m = self.srcs[a].shape[0]
        return self.outs[a].at[pl.ds((4 * px + 2 * py + pc) * m, m), :]

    def _copy(self, a, k, block, to, src=None):
        row = self.first_sem + a
        return pltpu.make_async_remote_copy(
            src_ref=self._rows(a, *block) if src is None else src, dst_ref=self._rows(a, *block),
            send_sem=self.send_sems.at[row, k], recv_sem=self.recv_sems.at[row, k], device_id=to, device_id_type=MESH)

    def _mine(self):
        x, y, c, _ = _place()
        return [pltpu.make_async_copy(self.srcs[a], self._rows(a, x, y, c), self.local_sems.at[self.first_sem + a])
                for a in range(self.n)]

    def _first(self, far):
        x, y, c, chips = _place()
        out = []
        for a in range(self.n):
            if far:
                out.append(self._copy(a, 3, (x, y, c), (*chips[2], c), src=self.srcs[a]))
            else:
                out.append(self._copy(a, 0, (x, y, c), (x, y, 1 - c), src=self.srcs[a]))
                out += [self._copy(a, 1 + j, (x, y, c), (*chips[j], c), src=self.srcs[a]) for j in (1, 0)]
        return out

    def _passed(self, j):
        x, y, c, chips = _place()
        return [self._copy(a, 4 + j, (*chips[j], c), (x, y, 1 - c)) for a in range(self.n)]

    def start(self):
        for cp in self._mine() + self._first(far=False):
            cp.start()

    def start_far(self):
        for cp in self._first(far=True):
            cp.start()

    def from_chip(self, j):
        x, y, c, chips = _place()
        for a in range(self.n):
            self._copy(a, 1 + j, (*chips[j], c), (x, y, c)).wait_recv()
        for cp in self._passed(j):
            cp.start()

    def from_sibling(self, j=None):
        x, y, c, chips = _place()
        for a in range(self.n):
            block = (x, y, 1 - c) if j is None else (*chips[j], 1 - c)
            self._copy(a, 0 if j is None else 4 + j, block, (x, y, c)).wait_recv()

    def from_self(self):
        for cp in self._mine():
            cp.wait()

    def finish(self):
        for cp in (self._first(far=False) + self._first(far=True)
                   + self._passed(0) + self._passed(1) + self._passed(2)):
            cp.wait_send()

    def run(self):
        self.start()
        self.start_far()
        self.from_self()
        for j in range(3):
            self.from_chip(j)
        self.from_sibling()
        for j in range(3):
            self.from_sibling(j)
        self.finish()


def _gather_proj(x, gain, shards, xpos):
    S, D = x.shape
    n = len(shards)
    N = N_DEV * shards[0].shape[0]
    half = N // 2
    tm = 512
    nsteps = S // tm

    def body(*refs):
        xpos_ref, x_ref, g_ref = refs[:3]
        ins = refs[3:3 + n]
        proj_ref, h_ref = refs[3 + n:5 + n]
        outs = refs[5 + n:5 + 2 * n]
        casts = refs[5 + 2 * n:5 + 3 * n]
        whole = refs[5 + 3 * n:5 + 4 * n]
        sems = refs[5 + 4 * n:8 + 4 * n]
        ag = _AllGather(casts[:1], whole[:1], *sems)
        later = _AllGather(casts[1:], whole[1:], *sems, first_sem=1)
        out_sems, h_all = refs[8 + 4 * n:]
        p, i = pl.program_id(0), pl.program_id(1)
        rows = pl.ds(pl.multiple_of(i * tm, tm), tm)

        @pl.when((p == 0) & (i == 0))
        def _():
            for a in range(n):
                tr = _row_step(ins[a].shape[0])

                def cast(r, carry, a=a, tr=tr):
                    at = pl.ds(pl.multiple_of(r * tr, tr), tr)
                    casts[a][at, :] = ins[a][at, :].astype(BF16)
                    return carry
                lax.fori_loop(0, ins[a].shape[0] // tr, cast, 0)
            ag.start()

        @pl.when(p == 0)
        def _():
            xv = x_ref[...]
            r = lax.rsqrt(jnp.mean(xv * xv, axis=-1, keepdims=True) + EPS)
            h = ((xv * r) * g_ref[...]).astype(BF16)
            h_ref[...] = h
            h_all[rows, :] = h

        @pl.when((p == 1) & (i == 0))
        def _():
            ag.from_self()
            ag.from_chip(1)
            ag.from_sibling()
            ag.from_sibling(1)
            ag.start_far()
            later.start()
            later.start_far()

        @pl.when((p == 2) & (i == 0))
        def _():
            for j in (0, 2):
                ag.from_chip(j)
            for j in (0, 2):
                ag.from_sibling(j)

        @pl.when(p > 0)
        def _():
            which = (xpos_ref[0] + p - 1) % 2
            w_half = whole[0][pl.ds(pl.multiple_of(which * half, half), half), :]
            proj_ref[...] = _nt(h_all[rows, :], w_half)

        @pl.when((p == 2) & (i == nsteps - 1))
        def _():
            ag.finish()
            later.from_self()
            for j in range(3):
                later.from_chip(j)
            later.from_sibling()
            for j in range(3):
                later.from_sibling(j)
            later.finish()
            to_results = [pltpu.make_async_copy(whole[a], outs[a], out_sems.at[a]) for a in range(n)]
            for cp in to_results:
                cp.start()
            for cp in to_results:
                cp.wait()

    vmem = pl.BlockSpec(memory_space=pltpu.VMEM)
    hbm = pl.BlockSpec(memory_space=pl.ANY)
    gathered = [(N_DEV * a.shape[0], a.shape[1]) for a in shards]
    x_tile = lambda p, i, xp: (jnp.where(p == 0, i, nsteps - 1), 0)
    proj_tile = lambda p, i, xp: (jnp.where(p == 0, 0, i), (xp[0] + jnp.maximum(p - 1, 0)) % 2)
    grid_spec = pltpu.PrefetchScalarGridSpec(
        num_scalar_prefetch=1, grid=(3, nsteps),
        in_specs=[pl.BlockSpec((tm, D), x_tile), pl.BlockSpec((1, D), lambda p, i, xp: (0, 0))] + [vmem] * n,
        out_specs=[pl.BlockSpec((tm, half), proj_tile), pl.BlockSpec((tm, D), x_tile)] + [hbm] * n,
        scratch_shapes=[pltpu.VMEM(a.shape, BF16) for a in shards] + [pltpu.VMEM(g, BF16) for g in gathered]
        + [pltpu.SemaphoreType.DMA((n, 7)), pltpu.SemaphoreType.DMA((n, 7)), pltpu.SemaphoreType.DMA((n,)),
           pltpu.SemaphoreType.DMA((n,)), pltpu.VMEM((S, D), BF16)])
    return pl.pallas_call(
        body, name="gather_proj", grid_spec=grid_spec,
        out_shape=[pltpu.HBM((S, N), F32), pltpu.HBM((S, D), BF16)] + [pltpu.HBM(g, BF16) for g in gathered],
        compiler_params=_params(56, ("arbitrary", "arbitrary")),
    )(xpos, *_hbm(x, gain), *shards)


ROW_NORM, ROW_MEM_NORM, ROW_V_GAIN, ROW_B, ROW_ATTN_GAINS, ROW_MEM_GAINS, ROW_W_S, ROW_LOSS = 0, 8, 16, 18, 22, 23, 24, 536
SMALL_ROWS = 544


def _gather_small(dgain, dmgain, dvg, db2, dqg, dkg, dmqg, dmkg, dws, sq):
    def body(dgain_ref, dmgain_ref, dvg_ref, db2_ref, dqg_ref, dkg_ref, dmqg_ref, dmkg_ref, dws_ref, sq_ref,
             out_ref, mine, send_sems, recv_sems, local_sems):
        first = lax.broadcasted_iota(jnp.int32, (1, 128), 1) < HEAD_DIM
        for i in range(8):
            cols = slice(128 * i, 128 * (i + 1))
            mine[ROW_NORM + i:ROW_NORM + i + 1, :] = dgain_ref[:, cols]
            mine[ROW_MEM_NORM + i:ROW_MEM_NORM + i + 1, :] = dmgain_ref[:, cols]
            mine[ROW_LOSS + i:ROW_LOSS + i + 1, :] = sq_ref[:, cols]
        mine[ROW_V_GAIN:ROW_V_GAIN + 1, :] = dvg_ref[:, 0:128]
        mine[ROW_V_GAIN + 1:ROW_V_GAIN + 2, :] = dvg_ref[:, 128:256]
        bt = db2_ref[...].T
        for h in range(4):
            mine[ROW_B + h:ROW_B + h + 1, :] = bt[HEAD_DIM * h:HEAD_DIM * h + 1, :]

        def fold_heads(t):
            return t + pltpu.roll(t, HEAD_DIM, axis=1)
        aq = fold_heads(dqg_ref[0] + dqg_ref[1] + dqg_ref[2] + dqg_ref[3])
        ak = fold_heads(dkg_ref[0] + dkg_ref[1] + dkg_ref[2] + dkg_ref[3])
        mine[ROW_ATTN_GAINS:ROW_ATTN_GAINS + 1, :] = jnp.where(first, aq, ak)
        mq = fold_heads(dmqg_ref[:, 0:128] + dmqg_ref[:, 128:256])
        mk = fold_heads(dmkg_ref[:, 0:128] + dmkg_ref[:, 128:256])
        mine[ROW_MEM_GAINS:ROW_MEM_GAINS + 1, :] = jnp.where(first, mq, mk)
        mine[ROW_W_S:ROW_W_S + 4 * CHUNK, :] = dws_ref[...]
        _AllGather([mine], [out_ref], send_sems, recv_sems, local_sems).run()

    return pl.pallas_call(
        body, name="gather_small_grads",
        out_shape=jax.ShapeDtypeStruct((N_DEV * SMALL_ROWS, 128), F32),
        scratch_shapes=[pltpu.VMEM((SMALL_ROWS, 128), F32), pltpu.SemaphoreType.DMA((1, 7)),
                        pltpu.SemaphoreType.DMA((1, 7)), pltpu.SemaphoreType.DMA((1,))],
        compiler_params=_params(16),
    )(dgain, dmgain, dvg, db2, dqg, dkg, dmqg, dmkg, dws, sq)


def _reduce_scatter_scratch(arrs):
    n = len(arrs)
    return ([pltpu.VMEM((4,) + a.shape[1:], BF16) for a in arrs] + [pltpu.VMEM((3,) + a.shape[1:], BF16) for a in arrs]
            + [pltpu.SemaphoreType.DMA((n, 7)), pltpu.SemaphoreType.DMA((n, 7))])


class _ReduceScatter:
    def __init__(self, ins, outs, *scratch):
        n = len(ins)
        self.n, self.ins, self.outs = n, ins, outs
        self.half, self.quarter = scratch[:n], scratch[n:2 * n]
        self.send_sems, self.recv_sems = scratch[2 * n:]

    def _to_sibling(self):
        x, y, c, _ = _place()
        return [pltpu.make_async_remote_copy(
            src_ref=self.ins[a].at[2 * q + (1 - c)], dst_ref=self.half[a].at[q], send_sem=self.send_sems.at[a, q],
            recv_sem=self.recv_sems.at[a, q], device_id=(x, y, 1 - c), device_id_type=MESH)
            for a in range(self.n) for q in range(4)]

    def _to_chips(self):
        _, _, c, chips = _place()
        return [pltpu.make_async_remote_copy(
            src_ref=self.half[a].at[2 * chip[0] + chip[1]], dst_ref=self.quarter[a].at[k],
            send_sem=self.send_sems.at[a, 4 + k], recv_sem=self.recv_sems.at[a, 4 + k], device_id=(*chip, c),
            device_id_type=MESH) for a in range(self.n) for k, chip in enumerate(chips)]

    def _rows(self, a, fn):
        m = self.ins[a].shape[1]
        tr = _row_step(m)

        def step(i, carry):
            fn(pl.ds(pl.multiple_of(i * tr, tr), tr))
            return carry
        lax.fori_loop(0, m // tr, step, 0)

    def start(self):
        for cp in self._to_sibling():
            cp.start()

    def middle(self):
        _, _, c, _ = _place()
        for cp in self._to_sibling():
            cp.wait_recv()
        for a in range(self.n):
            for q in range(4):
                def add_half(rows, a=a, q=q):
                    both = self.ins[a][2 * q + c, rows, :].astype(F32) + self.half[a][q, rows, :].astype(F32)
                    self.half[a][q, rows, :] = both.astype(BF16)
                self._rows(a, add_half)
        for cp in self._to_chips():
            cp.start()

    def finish(self):
        x, y, _, _ = _place()
        for cp in self._to_chips():
            cp.wait_recv()
        for a in range(self.n):
            def add_quarters(rows, a=a):
                f = lambda t: t.astype(F32)
                self.outs[a][rows, :] = ((f(self.half[a][2 * x + y, rows, :]) + f(self.quarter[a][0, rows, :]))
                                         + (f(self.quarter[a][1, rows, :]) + f(self.quarter[a][2, rows, :])))
            self._rows(a, add_quarters)
        for cp in self._to_sibling() + self._to_chips():
            cp.wait_send()


def _adamw_math(w, g, m, v):
    m = ADAM_B1 * m + (1.0 - ADAM_B1) * g
    v = ADAM_B2 * v + (1.0 - ADAM_B2) * (g * g)
    m_hat = m / (1.0 - ADAM_B1 ** ADAM_STEP)
    v_hat = v / (1.0 - ADAM_B2 ** ADAM_STEP)
    delta = -ADAM_LR * (m_hat / (jnp.sqrt(v_hat) + ADAM_EPS) + ADAM_WD * w)
    return delta, m, v


def _adamw(w, g, m, v, name):
    R, C = w.shape
    tr = _row_step(R)

    def body(w_ref, g_ref, m_ref, v_ref, d_ref, nm_ref, nv_ref):
        d_ref[...], nm_ref[...], nv_ref[...] = _adamw_math(w_ref[...], g_ref[...], m_ref[...], v_ref[...])

    tile = pl.BlockSpec((tr, C), lambda i: (i, 0))
    out = pltpu.HBM((R, C), F32)
    return pl.pallas_call(
        body, name=name, grid=(R // tr,), in_specs=[tile] * 4, out_specs=[tile] * 3, out_shape=[out] * 3,
        compiler_params=_params(16, ("arbitrary",)),
    )(*_hbm(w, g, m, v))


SMALL = ("norm_gain", "gmlp_v_gain", "gmlp_w_s", "gmlp_b", "attn_q_gain", "attn_k_gain", "mem_norm_gain",
         "mem_q_gain", "mem_k_gain")
WEIGHTS = ("norm_gain", "w_in", "gmlp_v_gain", "gmlp_w_s", "gmlp_b", "attn_q_gain", "attn_k_gain",
           "mem_norm_gain", "w_mem_kv", "mem_q_gain", "mem_k_gain", "w_out")


def _adamw_small(w, m, v, g_all):
    k = len(SMALL)
    half = slice(0, HEAD_DIM), slice(HEAD_DIM, 2 * HEAD_DIM)

    def body(*refs):
        w_refs, m_refs, v_refs = refs[:k], refs[k:2 * k], refs[2 * k:3 * k]
        g_ref = refs[3 * k]
        outs = refs[3 * k + 1:7 * k + 1]
        loss_ref, gsum = refs[7 * k + 1:]

        part = SMALL_ROWS // 4
        for p in range(4):
            acc = g_ref[part * p:part * (p + 1), :]
            for dev in range(1, N_DEV):
                acc = acc + g_ref[dev * SMALL_ROWS + part * p:dev * SMALL_ROWS + part * (p + 1), :]
            gsum[part * p:part * (p + 1), :] = acc

        def update(name, at, g):
            i = SMALL.index(name)
            d, nm, nv = _adamw_math(w_refs[i][at], g, m_refs[i][at], v_refs[i][at])
            outs[i][at], outs[k + i][at], outs[2 * k + i][at], outs[3 * k + i][at] = g, d, nm, nv

        for i in range(8):
            at = (slice(0, 1), slice(128 * i, 128 * (i + 1)))
            update("norm_gain", at, gsum[ROW_NORM + i:ROW_NORM + i + 1, :])
            update("mem_norm_gain", at, gsum[ROW_MEM_NORM + i:ROW_MEM_NORM + i + 1, :])
        for h in range(4):
            row = (0, slice(h, h + 1), slice(None))
            update("gmlp_v_gain", row, gsum[ROW_V_GAIN + h // 2:ROW_V_GAIN + h // 2 + 1, half[h % 2]])
            update("gmlp_b", row, gsum[ROW_B + h:ROW_B + h + 1, :])
            update("gmlp_w_s", (0, h), gsum[ROW_W_S + CHUNK * h:ROW_W_S + CHUNK * (h + 1), :])
        whole = (slice(0, 1), slice(None))
        update("attn_q_gain", whole, gsum[ROW_ATTN_GAINS:ROW_ATTN_GAINS + 1, half[0]])
        update("attn_k_gain", whole, gsum[ROW_ATTN_GAINS:ROW_ATTN_GAINS + 1, half[1]])
        update("mem_q_gain", whole, gsum[ROW_MEM_GAINS:ROW_MEM_GAINS + 1, half[0]])
        update("mem_k_gain", whole, gsum[ROW_MEM_GAINS:ROW_MEM_GAINS + 1, half[1]])
        loss_ref[...] = jnp.sum(gsum[ROW_LOSS:ROW_LOSS + 8, :], keepdims=True) * (0.5 / D_MODEL)

    shapes = [jax.ShapeDtypeStruct(w[name].shape, F32) for name in SMALL]
    res = pl.pallas_call(
        body, name="adamw_small",
        out_shape=shapes * 4 + [jax.ShapeDtypeStruct((1, 1), F32)],
        scratch_shapes=[pltpu.VMEM((SMALL_ROWS, 128), F32)],
        compiler_params=_params(16),
    )(*[w[n] for n in SMALL], *[m[n] for n in SMALL], *[v[n] for n in SMALL], g_all)
    trees = [dict(zip(SMALL, res[j * k:(j + 1) * k])) for j in range(4)]
    return (*trees, res[4 * k])


def _grads(x, mem, tgt, w, shards):
    bd128, bd256 = _head_blockdiag(128), _head_blockdiag(256)
    gain = w["norm_gain"].reshape(1, D_MODEL)
    vg = w["gmlp_v_gain"].reshape(1, GMLP_WIDTH)
    w_s = w["gmlp_w_s"].reshape(4, CHUNK, CHUNK)
    b2 = jnp.repeat(w["gmlp_b"].reshape(4, CHUNK).T, HEAD_DIM, axis=1)
    qg2 = jnp.tile(w["attn_q_gain"].reshape(1, HEAD_DIM), (1, 2))
    kg2 = jnp.tile(w["attn_k_gain"].reshape(1, HEAD_DIM), (1, 2))
    mqg4 = jnp.tile(w["mem_q_gain"].reshape(1, HEAD_DIM), (1, 4))
    mkg4 = jnp.tile(w["mem_k_gain"].reshape(1, HEAD_DIM), (1, 4))
    mgain = w["mem_norm_gain"].reshape(1, D_MODEL)

    xpos = lax.axis_index("x").astype(jnp.int32).reshape(1)
    proj, h_bf, win_t, wkv_bf, wout_bf = _gather_proj(x, gain, shards, xpos)
    yg = _gmlp_fwd(proj, vg, w_s, b2, bd256)
    ya, att, lse = _attn_fwd(proj, qg2, kg2, bd128)
    hm_bf, kraw, mk, mv = _mem_kv(mem, mgain, wkv_bf, mkg4, bd256)
    ym, om = _mem_fwd(proj, mk, mv, mqg4, bd256)
    dout, dycat, dwout, sq = _out_loss(yg, ya, ym, x, tgt, wout_bf)

    du, dgv, dgg, dws, db2, dvg = _gmlp_bwd(proj, dycat, vg, w_s, b2, bd256)
    dq, dk, dv, dag, dqg, dkg = _attn_bwd(proj, dycat, att, lse, qg2, kg2, bd128)
    dmq, dmg, dmk, dmv, dmqg = _mem_bwd(proj, dycat, om, mk, mv, mqg4, bd256)
    dwkv, dmgain, dmkg = _mem_kv_bwd(dmk, dmv, kraw, mem, mgain, mkg4, wkv_bf, hm_bf, bd256)
    pieces = [du, dgv, dgg, dq, dk, dv, dag, dmq, dmg]
    dwin, g_wkv, g_wout = _in_bwd_dw(pieces, h_bf, [dwkv, dwout])
    grad_x, dgain, g_win = _in_bwd_dx(pieces, x, dout, gain, win_t, dwin)
    return grad_x, g_win, g_wkv, g_wout, (dgain, dmgain, dvg, db2, dqg, dkg, dmqg, dmkg, dws, sq)


def kernel(x, mem, norm_gain, w_in, gmlp_v_gain, gmlp_w_s, gmlp_b, attn_q_gain, attn_k_gain, mem_norm_gain, w_mem_kv, mem_q_gain, mem_k_gain, w_out, loss_target, m_norm_gain, m_w_in, m_gmlp_v_gain, m_gmlp_w_s, m_gmlp_b, m_attn_q_gain, m_attn_k_gain, m_mem_norm_gain, m_w_mem_kv, m_mem_q_gain, m_mem_k_gain, m_w_out, v_norm_gain, v_w_in, v_gmlp_v_gain, v_gmlp_w_s, v_gmlp_b, v_attn_q_gain, v_attn_k_gain, v_mem_norm_gain, v_w_mem_kv, v_mem_q_gain, v_mem_k_gain, v_w_out):
    w = dict(norm_gain=norm_gain, w_in=w_in, gmlp_v_gain=gmlp_v_gain, gmlp_w_s=gmlp_w_s, gmlp_b=gmlp_b,
             attn_q_gain=attn_q_gain, attn_k_gain=attn_k_gain, mem_norm_gain=mem_norm_gain, w_mem_kv=w_mem_kv,
             mem_q_gain=mem_q_gain, mem_k_gain=mem_k_gain, w_out=w_out)
    m = dict(norm_gain=m_norm_gain, w_in=m_w_in, gmlp_v_gain=m_gmlp_v_gain, gmlp_w_s=m_gmlp_w_s, gmlp_b=m_gmlp_b,
             attn_q_gain=m_attn_q_gain, attn_k_gain=m_attn_k_gain, mem_norm_gain=m_mem_norm_gain,
             w_mem_kv=m_w_mem_kv, mem_q_gain=m_mem_q_gain, mem_k_gain=m_mem_k_gain, w_out=m_w_out)
    v = dict(norm_gain=v_norm_gain, w_in=v_w_in, gmlp_v_gain=v_gmlp_v_gain, gmlp_w_s=v_gmlp_w_s, gmlp_b=v_gmlp_b,
             attn_q_gain=v_attn_q_gain, attn_k_gain=v_attn_k_gain, mem_norm_gain=v_mem_norm_gain,
             w_mem_kv=v_w_mem_kv, mem_q_gain=v_mem_q_gain, mem_k_gain=v_mem_k_gain, w_out=v_w_out)
    transposed = lambda t: jnp.transpose(t[0])

    grad_x, g_win, g_wkv, g_wout, small = _grads(
        x[0], mem[0], loss_target[0], w, [transposed(w_in), w_mem_kv[0], w_out[0]])
    small_all = _gather_small(*small)

    out_g, out_d, out_m, out_v, loss = _adamw_small(w, m, v, small_all)
    d_, m_, v_ = _adamw(transposed(w_in), g_win, transposed(m_w_in), transposed(v_w_in), "adamw_w_in")
    for tree, t in ((out_g, g_win), (out_d, d_), (out_m, m_), (out_v, v_)):
        tree["w_in"] = jnp.transpose(t)[None]
    for name, g in (("w_mem_kv", g_wkv), ("w_out", g_wout)):
        d_, m_, v_ = _adamw(w[name][0], g, m[name][0], v[name][0], "adamw_" + name)
        out_g[name], out_d[name], out_m[name], out_v[name] = g[None], d_[None], m_[None], v_[None]

    return (loss.reshape(()), grad_x[None], *[out_g[k] for k in WEIGHTS], *[out_d[k] for k in WEIGHTS],
            *[out_m[k] for k in WEIGHTS], *[out_v[k] for k in WEIGHTS])
```

```python
import functools
import math

import jax
import jax.numpy as jnp
from jax import lax
from jax.experimental import pallas as pl
from jax.experimental.pallas import tpu as pltpu

F32 = jnp.float32
BF16 = jnp.bfloat16

N_DEV = 8
D_MODEL = 1024
HEAD_DIM = 64
GMLP_WIDTH = 256
ATTN_WIDTH = 512
MEM_WIDTH = 256
MEM_LEN = 256
CHUNK = 128
BLOCK = 128
DILATIONS = (1, 4, 16)
EPS = 1e-6
SCALE = 1.0 / math.sqrt(HEAD_DIM)
NEG = -1e30

ADAM_LR = 0.001
ADAM_B1 = 0.9
ADAM_B2 = 0.999
ADAM_EPS = 1e-08
ADAM_WD = 0.01
ADAM_STEP = 10

MIB = 1024 * 1024
MESH = pl.DeviceIdType.MESH

COL_AQ, COL_AK, COL_AV, COL_AG = 6, 10, 14, 18


def _params(vmem_mib, semantics=None):
    kw = dict(vmem_limit_bytes=vmem_mib * MIB)
    if semantics is not None:
        kw["dimension_semantics"] = semantics
    return pltpu.CompilerParams(**kw)


def _hbm(*arrs):
    return [pltpu.with_memory_space_constraint(a, pltpu.HBM) for a in arrs]


def _split_dot(x, sel_bf):
    hi = x.astype(BF16)
    lo = (x - hi.astype(F32)).astype(BF16)
    return jnp.dot(hi, sel_bf, preferred_element_type=F32) + jnp.dot(lo, sel_bf, preferred_element_type=F32)


def _nt(a, b):
    return lax.dot_general(a, b, (((1,), (1,)), ((), ())), preferred_element_type=F32)


def _tn(a, b):
    return lax.dot_general(a, b, (((0,), (0,)), ((), ())), preferred_element_type=F32)


def _silu_parts(g):
    sg = jax.nn.sigmoid(g)
    return g * sg, sg * (1.0 + g * (1.0 - sg))


def _head_index(shape):
    return lax.shift_right_logical(lax.broadcasted_iota(jnp.int32, shape, 1), HEAD_DIM.bit_length() - 1)


def _head_blockdiag(width):
    i = jnp.arange(width) // HEAD_DIM
    return (i[:, None] == i[None, :]).astype(BF16)


def _gmlp_masked_weights(ws_ref, transpose):
    t = lax.broadcasted_iota(jnp.int32, (CHUNK, CHUNK), 0)
    s = lax.broadcasted_iota(jnp.int32, (CHUNK, CHUNK), 1)
    parts = []
    for h in range(4):
        wm = jnp.where(s <= t, ws_ref[h], 0.0)
        parts.append(wm.T if transpose else wm)
    return jnp.concatenate(parts, axis=1).astype(BF16)


def _head_stack(v, head):
    return jnp.concatenate([jnp.where(head == h, v, 0.0) for h in range(4)], axis=0).astype(BF16)


def _gmlp_fwd(proj, vg, w_s, b2, bd):
    S = proj.shape[0]
    tm = 512

    def body(u_ref, v_ref, g_ref, vg_ref, ws_ref, b2_ref, bd_ref, y_ref):
        v = v_ref[...]
        ms = _split_dot(v * v, bd_ref[...]) * (1.0 / HEAD_DIM)
        vn = (v * lax.rsqrt(ms + EPS)) * vg_ref[...]
        wcat = _gmlp_masked_weights(ws_ref, False)
        head = _head_index((CHUNK, GMLP_WIDTH))
        for c in range(tm // CHUNK):
            rows = slice(c * CHUNK, (c + 1) * CHUNK)
            sp = jnp.dot(wcat, _head_stack(vn[rows], head), preferred_element_type=F32) + b2_ref[...]
            silu, _ = _silu_parts(g_ref[rows, :])
            y_ref[rows, :] = ((u_ref[rows, :] * sp) * silu).astype(BF16)

    col = lambda j: pl.BlockSpec((tm, GMLP_WIDTH), lambda i, j=j: (i, j))
    const = lambda shape: pl.BlockSpec(shape, lambda i: (0,) * len(shape))
    return pl.pallas_call(
        body, name="gmlp_fwd", grid=(S // tm,),
        in_specs=[col(0), col(1), col(2), const((1, GMLP_WIDTH)), const((4, CHUNK, CHUNK)),
                  const((CHUNK, GMLP_WIDTH)), const((GMLP_WIDTH, GMLP_WIDTH))],
        out_specs=pl.BlockSpec((tm, GMLP_WIDTH), lambda i: (i, 0)),
        out_shape=pltpu.HBM((S, GMLP_WIDTH), BF16),
        compiler_params=_params(24, ("arbitrary",)),
    )(*_hbm(proj, proj, proj, vg, w_s, b2, bd))


def _gmlp_bwd(proj, dycat, vg, w_s, b2, bd):
    S = proj.shape[0]
    tm = 512
    nsteps = S // tm

    def body(u_ref, v_ref, g_ref, dy_ref, vg_ref, ws_ref, b2_ref, bd_ref,
             du_ref, dv_ref, dg_ref, dws_ref, db2_ref, dvg_ref):
        i = pl.program_id(0)

        @pl.when(i == 0)
        def _():
            dws_ref[...] = jnp.zeros_like(dws_ref)
            db2_ref[...] = jnp.zeros_like(db2_ref)
            dvg_ref[...] = jnp.zeros_like(dvg_ref)

        bdv = bd_ref[...]
        v = v_ref[...]
        ms = _split_dot(v * v, bdv) * (1.0 / HEAD_DIM)
        rv = lax.rsqrt(ms + EPS)
        xhat = v * rv
        vgv = vg_ref[...]
        vn = xhat * vgv
        wcat = _gmlp_masked_weights(ws_ref, False)
        wcat_t = _gmlp_masked_weights(ws_ref, True)
        head = _head_index((CHUNK, GMLP_WIDTH))
        dvg = jnp.zeros((1, GMLP_WIDTH), F32)
        for c in range(tm // CHUNK):
            rows = slice(c * CHUNK, (c + 1) * CHUNK)
            vn_c = vn[rows]
            spb = jnp.dot(wcat, _head_stack(vn_c, head), preferred_element_type=F32) + b2_ref[...]
            silu, dsilu = _silu_parts(g_ref[rows, :])
            dy = dy_ref[rows, :]
            u = u_ref[rows, :]
            du_ref[rows, :] = (dy * spb * silu).astype(BF16)
            dg_ref[rows, :] = (dy * u * spb * dsilu).astype(BF16)
            dsp = dy * u * silu
            db2_ref[...] += dsp
            dstack = _head_stack(dsp, head)
            dvn = jnp.dot(wcat_t, dstack, preferred_element_type=F32)
            dws_ref[...] += _nt(dstack, vn_c.astype(BF16))
            xh = xhat[rows]
            a = dvn * vgv
            mean_ax = _split_dot(a * xh, bdv) * (1.0 / HEAD_DIM)
            dv_ref[rows, :] = (rv[rows] * (a - xh * mean_ax)).astype(BF16)
            dvg = dvg + jnp.sum(dvn * xh, axis=0, keepdims=True)
        dvg_ref[...] += dvg

        @pl.when(i == nsteps - 1)
        def _():
            t = lax.broadcasted_iota(jnp.int32, (4 * CHUNK, CHUNK), 0) % CHUNK
            s = lax.broadcasted_iota(jnp.int32, (4 * CHUNK, CHUNK), 1)
            dws_ref[...] = jnp.where(s <= t, dws_ref[...], 0.0)
            db2_ref[...] = _split_dot(db2_ref[...], bdv)

    col = lambda j: pl.BlockSpec((tm, GMLP_WIDTH), lambda i, j=j: (i, j))
    const = lambda shape: pl.BlockSpec(shape, lambda i: (0,) * len(shape))
    tile = pl.BlockSpec((tm, GMLP_WIDTH), lambda i: (i, 0))
    piece = pltpu.HBM((S, GMLP_WIDTH), BF16)
    return pl.pallas_call(
        body, name="gmlp_bwd", grid=(nsteps,),
        in_specs=[col(0), col(1), col(2), col(0), const((1, GMLP_WIDTH)), const((4, CHUNK, CHUNK)),
                  const((CHUNK, GMLP_WIDTH)), const((GMLP_WIDTH, GMLP_WIDTH))],
        out_specs=[tile, tile, tile, const((4 * CHUNK, CHUNK)), const((CHUNK, GMLP_WIDTH)), const((1, GMLP_WIDTH))],
        out_shape=[piece, piece, piece, pltpu.HBM((4 * CHUNK, CHUNK), F32),
                   pltpu.HBM((CHUNK, GMLP_WIDTH), F32), pltpu.HBM((1, GMLP_WIDTH), F32)],
        compiler_params=_params(32, ("arbitrary",)),
    )(*_hbm(proj, proj, proj, dycat, vg, w_s, b2, bd))


def _band_mask():
    qi = lax.broadcasted_iota(jnp.int32, (2 * BLOCK, 2 * BLOCK), 0) % BLOCK
    ki = lax.broadcasted_iota(jnp.int32, (2 * BLOCK, 2 * BLOCK), 1)
    return ((ki < BLOCK) & (ki >= qi)) | ((ki >= BLOCK) & ((ki - BLOCK) <= qi))


def _first_block_bias(blk, blocks_per_class):
    kcol = lax.broadcasted_iota(jnp.int32, (1, 2 * BLOCK), 1)
    kill = jnp.where((blk & (blocks_per_class - 1)) == 0, NEG, 0.0)
    return jnp.where(kcol < BLOCK, kill, 0.0)


def _two_heads(q, lo):
    zero = jnp.zeros_like(q)
    return jnp.concatenate([jnp.where(lo, q, zero), jnp.where(lo, zero, q)], axis=0)


def _block_tokens(blk, d, S):
    if d == 1:
        return pl.ds(pl.multiple_of(blk * BLOCK, BLOCK), BLOCK)
    blocks_per_class = S // d // BLOCK
    r = lax.shift_right_logical(blk, blocks_per_class.bit_length() - 1)
    n = blk & (blocks_per_class - 1)
    return pl.ds(r + n * (BLOCK * d), BLOCK, stride=d)


def _padded_block(blk):
    return pl.ds(pl.multiple_of((blk + 1) * BLOCK, BLOCK), BLOCK)


def _for_blocks(n_blocks, unroll, fn):
    def group(g, carry):
        for u in range(unroll):
            fn(g * unroll + u)
        return carry
    lax.fori_loop(0, n_blocks // unroll, group, 0)


def _attn_fwd(proj, qg2, kg2, bd):
    S = proj.shape[0]
    npairs = ATTN_WIDTH // 128
    tn = 512

    def body(q_ref, k_ref, v_ref, g_ref, qg_ref, kg_ref, bd_ref, y_ref, att_ref, lse_ref, qn, kn, kc, vc):
        bdv = bd_ref[...]
        lo = lax.broadcasted_iota(jnp.int32, (BLOCK, 128), 1) < HEAD_DIM
        band_mask = _band_mask()
        kc[pl.ds(0, BLOCK), :] = jnp.zeros((BLOCK, 128), BF16)
        vc[pl.ds(0, BLOCK), :] = jnp.zeros((BLOCK, 128), BF16)

        def norm_step(i, carry):
            rows = pl.ds(pl.multiple_of(i * tn, tn), tn)
            qv = q_ref[rows, :]
            kv = k_ref[rows, :]
            qn[rows, :] = (qv * lax.rsqrt(_split_dot(qv * qv, bdv) * (1.0 / HEAD_DIM) + EPS)) * (qg_ref[...] * SCALE)
            kn[rows, :] = (kv * lax.rsqrt(_split_dot(kv * kv, bdv) * (1.0 / HEAD_DIM) + EPS)) * kg_ref[...]
            return carry
        lax.fori_loop(0, S // tn, norm_step, 0)

        def fill(blk, d):
            tokens = _block_tokens(blk, d, S)
            kc[_padded_block(blk), :] = kn[tokens, :].astype(BF16)
            vc[_padded_block(blk), :] = v_ref[tokens, :].astype(BF16)

        ones_bf = jnp.ones((2 * BLOCK, 128), BF16)

        def block(blk, d):
            tokens = _block_tokens(blk, d, S)
            keys = pl.ds(pl.multiple_of(blk * BLOCK, BLOCK), 2 * BLOCK)
            q2 = _two_heads(qn[tokens, :].astype(BF16), lo)
            s = jnp.where(band_mask, _nt(q2, kc[keys, :]), NEG) + _first_block_bias(blk, S // d // BLOCK)
            m = jnp.max(s, axis=-1, keepdims=True)
            e = jnp.exp((s - m).astype(BF16))
            ol = jnp.dot(e, jnp.concatenate([vc[keys, :], ones_bf], axis=1), preferred_element_type=F32)
            l = ol[:, 128:]
            o2 = ol[:, :128] * (1.0 / l)
            lse2 = m + jnp.log(l)
            o = jnp.where(lo, o2[:BLOCK], o2[BLOCK:])
            lse = jnp.where(lo, lse2[:BLOCK], lse2[BLOCK:])
            if d > 1:
                la = lse_ref[tokens, :]
                mx = jnp.maximum(la, lse)
                wa, wb = jnp.exp(la - mx), jnp.exp(lse - mx)
                t = wa + wb
                o = (wa * att_ref[tokens, :] + wb * o) / t
                lse = mx + jnp.log(t)
            att_ref[tokens, :] = o
            lse_ref[tokens, :] = lse

        for d in DILATIONS:
            _for_blocks(S // BLOCK, 4, functools.partial(fill, d=d))
            _for_blocks(S // BLOCK, 16, functools.partial(block, d=d))

        def gate_step(i, carry):
            rows = pl.ds(pl.multiple_of(i * tn, tn), tn)
            silu, _ = _silu_parts(g_ref[rows, :])
            y_ref[rows, :] = (att_ref[rows, :] * silu).astype(BF16)
            return carry
        lax.fori_loop(0, S // tn, gate_step, 0)

    col = lambda j0: pl.BlockSpec((S, 128), lambda p, j0=j0: (0, j0 + p))
    const = lambda shape: pl.BlockSpec(shape, lambda p: (0,) * len(shape))
    out = pl.BlockSpec((S, 128), lambda p: (0, p))
    return pl.pallas_call(
        body, name="attn_fwd", grid=(npairs,),
        in_specs=[col(COL_AQ), col(COL_AK), col(COL_AV), col(COL_AG), const((1, 128)), const((1, 128)),
                  const((128, 128))],
        out_specs=[out, out, out],
        out_shape=[pltpu.HBM((S, ATTN_WIDTH), BF16), pltpu.HBM((S, ATTN_WIDTH), F32),
                   pltpu.HBM((S, ATTN_WIDTH), F32)],
        scratch_shapes=[pltpu.VMEM((S, 128), F32), pltpu.VMEM((S, 128), F32),
                        pltpu.VMEM((S + BLOCK, 128), BF16), pltpu.VMEM((S + BLOCK, 128), BF16)],
        compiler_params=_params(48, ("arbitrary",)),
    )(*_hbm(proj, proj, proj, proj, qg2, kg2, bd))


def _attn_bwd(proj, dycat, att, lse, qg2, kg2, bd):
    S = proj.shape[0]
    npairs = ATTN_WIDTH // 128
    tn = 512

    def body(q_ref, k_ref, v_ref, g_ref, dy_ref, att_ref, lse_ref, qg_ref, kg_ref, bd_ref,
             dq_ref, dk_ref, dv_ref, dg_ref, dqg_ref, dkg_ref,
             qn, kn, rq_s, rk_s, kc, vc, do_s, dd_s, dqa, dka, dva):
        bdv = bd_ref[...]
        lo = lax.broadcasted_iota(jnp.int32, (BLOCK, 128), 1) < HEAD_DIM
        kc[pl.ds(0, BLOCK), :] = jnp.zeros((BLOCK, 128), BF16)
        vc[pl.ds(0, BLOCK), :] = jnp.zeros((BLOCK, 128), BF16)

        def prepare(i, carry):
            rows = pl.ds(pl.multiple_of(i * tn, tn), tn)
            qv = q_ref[rows, :]
            kv = k_ref[rows, :]
            rq = lax.rsqrt(_split_dot(qv * qv, bdv) * (1.0 / HEAD_DIM) + EPS)
            rk = lax.rsqrt(_split_dot(kv * kv, bdv) * (1.0 / HEAD_DIM) + EPS)
            rq_s[rows, :] = rq
            rk_s[rows, :] = rk
            qn[rows, :] = (qv * rq) * (qg_ref[...] * SCALE)
            kn[rows, :] = (kv * rk) * kg_ref[...]
            silu, dsilu = _silu_parts(g_ref[rows, :])
            dy = dy_ref[rows, :]
            at = att_ref[rows, :]
            do = dy * silu
            do_s[rows, :] = do
            dd_s[rows, :] = _split_dot(do * at, bdv)
            dg_ref[rows, :] = (dy * at * dsilu).astype(BF16)
            dka[rows, :] = jnp.zeros((tn, 128), F32)
            dva[rows, :] = jnp.zeros((tn, 128), F32)
            return carry
        lax.fori_loop(0, S // tn, prepare, 0)

        kt = lax.broadcasted_iota(jnp.int32, (2 * BLOCK, 2 * BLOCK), 0)
        qt = lax.broadcasted_iota(jnp.int32, (2 * BLOCK, 2 * BLOCK), 1) % BLOCK
        band_mask_t = ((kt < BLOCK) & (kt >= qt)) | ((kt >= BLOCK) & ((kt - BLOCK) <= qt))

        def per_query_row(t):
            tt = t.T
            return jnp.concatenate([tt[0:1, :], tt[HEAD_DIM:HEAD_DIM + 1, :]], axis=1)

        def fill(blk, d):
            tokens = _block_tokens(blk, d, S)
            kc[_padded_block(blk), :] = kn[tokens, :].astype(BF16)
            vc[_padded_block(blk), :] = v_ref[tokens, :].astype(BF16)

        def block(blk, d):
            tokens = _block_tokens(blk, d, S)
            keys = pl.ds(pl.multiple_of(blk * BLOCK, BLOCK), 2 * BLOCK)
            first = (blk & (S // d // BLOCK - 1)) == 0
            q2 = _two_heads(qn[tokens, :].astype(BF16), lo)
            do2 = _two_heads(do_s[tokens, :].astype(BF16), lo)
            lse_row = per_query_row(lse_ref[tokens, :])
            dd_row = per_query_row(dd_s[tokens, :])
            kb = kc[keys, :]
            vb = vc[keys, :]
            st = jnp.where(band_mask_t, _nt(kb, q2), NEG)
            st = jnp.concatenate([st[:BLOCK] + jnp.where(first, NEG, 0.0), st[BLOCK:]], axis=0)
            pt = jnp.exp(st - lse_row)
            dst = pt * (_nt(vb, do2) - dd_row)
            ptb = pt.astype(BF16)
            dstb = dst.astype(BF16)
            dv_band = jnp.dot(ptb, do2, preferred_element_type=F32)
            dk_band = jnp.dot(dstb, q2, preferred_element_type=F32)
            before = _block_tokens(jnp.where(first, blk, blk - 1), d, S)
            dka[before, :] = dka[before, :] + dk_band[:BLOCK]
            dva[before, :] = dva[before, :] + dv_band[:BLOCK]
            dka[tokens, :] = dka[tokens, :] + dk_band[BLOCK:]
            dva[tokens, :] = dva[tokens, :] + dv_band[BLOCK:]
            dq2 = _tn(dstb, kb)
            dq = jnp.where(lo, dq2[:BLOCK], dq2[BLOCK:])
            dqa[tokens, :] = dq if d == 1 else dqa[tokens, :] + dq

        for d in DILATIONS:
            _for_blocks(S // BLOCK, 4, functools.partial(fill, d=d))
            _for_blocks(S // BLOCK, 8, functools.partial(block, d=d))

        def out_step(i, carry):
            dqg, dkg = carry
            rows = pl.ds(pl.multiple_of(i * tn, tn), tn)
            rq = rq_s[rows, :]
            rk = rk_s[rows, :]
            qh = q_ref[rows, :] * rq
            kh = k_ref[rows, :] * rk
            dqs = dqa[rows, :] * SCALE
            dkn = dka[rows, :]
            aq = dqs * qg_ref[...]
            ak = dkn * kg_ref[...]
            dq_ref[rows, :] = (rq * (aq - qh * (_split_dot(aq * qh, bdv) * (1.0 / HEAD_DIM)))).astype(BF16)
            dk_ref[rows, :] = (rk * (ak - kh * (_split_dot(ak * kh, bdv) * (1.0 / HEAD_DIM)))).astype(BF16)
            dv_ref[rows, :] = dva[rows, :].astype(BF16)
            dqg = dqg + jnp.sum(dqs * qh, axis=0, keepdims=True)
            dkg = dkg + jnp.sum(dkn * kh, axis=0, keepdims=True)
            return dqg, dkg
        zero = jnp.zeros((1, 128), F32)
        dqg, dkg = lax.fori_loop(0, S // tn, out_step, (zero, zero))
        dqg_ref[0] = dqg
        dkg_ref[0] = dkg

    col = lambda j0: pl.BlockSpec((S, 128), lambda p, j0=j0: (0, j0 + p))
    col1 = lambda j0: pl.BlockSpec((S, 128), lambda p, j0=j0: (0, j0 + p), pipeline_mode=pl.Buffered(1))
    const = lambda shape: pl.BlockSpec(shape, lambda p: (0,) * len(shape))
    out = pl.BlockSpec((S, 128), lambda p: (0, p))
    gain_out = pl.BlockSpec((1, 1, 128), lambda p: (p, 0, 0))
    piece = pltpu.HBM((S, ATTN_WIDTH), BF16)
    gains = pltpu.HBM((npairs, 1, 128), F32)
    f32buf = pltpu.VMEM((S, 128), F32)
    bf16pad = pltpu.VMEM((S + BLOCK, 128), BF16)
    return pl.pallas_call(
        body, name="attn_bwd", grid=(npairs,),
        in_specs=[col(COL_AQ), col(COL_AK), col(COL_AV), col1(COL_AG), col1(GMLP_WIDTH // 128), col1(0), col(0),
                  const((1, 128)), const((1, 128)), const((128, 128))],
        out_specs=[out, out, out, out, gain_out, gain_out],
        out_shape=[piece, piece, piece, piece, gains, gains],
        scratch_shapes=[f32buf, f32buf, f32buf, f32buf, bf16pad, bf16pad, f32buf, f32buf, f32buf, f32buf, f32buf],
        compiler_params=_params(60, ("arbitrary",)),
    )(*_hbm(proj, proj, proj, proj, dycat, att, lse, qg2, kg2, bd))


def _mem_kv(mem, gain, wkv_bf, kg4, bd):
    def body(mem_ref, g_ref, w_ref, kg_ref, bd_ref, hm_ref, kraw_ref, mk_ref, mv_ref):
        mv_ = mem_ref[...]
        r = lax.rsqrt(jnp.mean(mv_ * mv_, axis=-1, keepdims=True) + EPS)
        hm = ((mv_ * r) * g_ref[...]).astype(BF16)
        hm_ref[...] = hm
        kv = jnp.dot(hm, w_ref[...], preferred_element_type=F32)
        kraw = kv[:, :MEM_WIDTH]
        kraw_ref[...] = kraw
        ms = _split_dot(kraw * kraw, bd_ref[...]) * (1.0 / HEAD_DIM)
        mk_ref[...] = (kraw * lax.rsqrt(ms + EPS)) * kg_ref[...]
        mv_ref[...] = kv[:, MEM_WIDTH:]

    sq = jax.ShapeDtypeStruct((MEM_LEN, MEM_WIDTH), F32)
    return pl.pallas_call(
        body, name="mem_kv",
        out_shape=[jax.ShapeDtypeStruct((MEM_LEN, D_MODEL), BF16), sq, sq, sq],
        compiler_params=_params(16),
    )(mem, gain, wkv_bf, kg4, bd)


def _mem_fwd(proj, mk, mv, qg4, bd):
    S = proj.shape[0]
    tm = 512

    def body(q_ref, g_ref, mk_ref, mv_ref, qg_ref, bd_ref, y_ref, om_ref):
        qv = q_ref[...]
        ms = _split_dot(qv * qv, bd_ref[...]) * (1.0 / HEAD_DIM)
        qs = (qv * lax.rsqrt(ms + EPS)) * (qg_ref[...] * SCALE)
        mkb = mk_ref[...].astype(BF16)
        mvb = mv_ref[...].astype(BF16)
        head = _head_index((tm, MEM_WIDTH))
        o = jnp.zeros((tm, MEM_WIDTH), F32)
        for h in range(4):
            s = _nt(jnp.where(head == h, qs, 0.0).astype(BF16), mkb)
            e = jnp.exp(s - jnp.max(s, axis=-1, keepdims=True))
            p = e * (1.0 / jnp.sum(e, axis=-1, keepdims=True))
            o = jnp.where(head == h, jnp.dot(p.astype(BF16), mvb, preferred_element_type=F32), o)
        om_ref[...] = o
        silu, _ = _silu_parts(g_ref[...])
        y_ref[...] = (o * silu).astype(BF16)

    col = lambda j: pl.BlockSpec((tm, MEM_WIDTH), lambda i, j=j: (i, j))
    const = lambda shape: pl.BlockSpec(shape, lambda i: (0,) * len(shape))
    tile = pl.BlockSpec((tm, MEM_WIDTH), lambda i: (i, 0))
    return pl.pallas_call(
        body, name="mem_fwd", grid=(S // tm,),
        in_specs=[col(11), col(12), const((MEM_LEN, MEM_WIDTH)), const((MEM_LEN, MEM_WIDTH)), const((1, MEM_WIDTH)),
                  const((MEM_WIDTH, MEM_WIDTH))],
        out_specs=[tile, tile],
        out_shape=[pltpu.HBM((S, MEM_WIDTH), BF16), pltpu.HBM((S, MEM_WIDTH), F32)],
        compiler_params=_params(24, ("arbitrary",)),
    )(*_hbm(proj, proj, mk, mv, qg4, bd))


def _mem_bwd(proj, dycat, om, mk, mv, qg4, bd):
    S = proj.shape[0]
    tm = 512

    def body(q_ref, g_ref, dy_ref, om_ref, mk_ref, mv_ref, qg_ref, bd_ref,
             dq_ref, dg_ref, dmk_ref, dmv_ref, dqg_ref):
        i = pl.program_id(0)

        @pl.when(i == 0)
        def _():
            dmk_ref[...] = jnp.zeros_like(dmk_ref)
            dmv_ref[...] = jnp.zeros_like(dmv_ref)
            dqg_ref[...] = jnp.zeros_like(dqg_ref)

        bdv = bd_ref[...]
        qv = q_ref[...]
        rq = lax.rsqrt(_split_dot(qv * qv, bdv) * (1.0 / HEAD_DIM) + EPS)
        qh = qv * rq
        qs = qh * (qg_ref[...] * SCALE)
        silu, dsilu = _silu_parts(g_ref[...])
        dy = dy_ref[...]
        o = om_ref[...]
        do = dy * silu
        dg_ref[...] = (dy * o * dsilu).astype(BF16)
        dd = _split_dot(do * o, bdv)
        mkb = mk_ref[...].astype(BF16)
        mvb = mv_ref[...].astype(BF16)
        head = _head_index((tm, MEM_WIDTH))
        dqs = jnp.zeros((tm, MEM_WIDTH), F32)
        for h in range(4):
            qhd = jnp.where(head == h, qs, 0.0).astype(BF16)
            doh = jnp.where(head == h, do, 0.0).astype(BF16)
            s = _nt(qhd, mkb)
            e = jnp.exp(s - jnp.max(s, axis=-1, keepdims=True))
            p = e * (1.0 / jnp.sum(e, axis=-1, keepdims=True))
            ds = p * (_nt(doh, mvb) - dd[:, h * HEAD_DIM:h * HEAD_DIM + 1])
            dsb = ds.astype(BF16)
            dmv_ref[...] += _tn(p.astype(BF16), doh)
            dmk_ref[...] += _tn(dsb, qhd)
            dqs = jnp.where(head == h, jnp.dot(dsb, mkb, preferred_element_type=F32), dqs)
        dqs = dqs * SCALE
        a = dqs * qg_ref[...]
        dq_ref[...] = (rq * (a - qh * (_split_dot(a * qh, bdv) * (1.0 / HEAD_DIM)))).astype(BF16)
        dqg_ref[...] += jnp.sum(dqs * qh, axis=0, keepdims=True)

    col = lambda j: pl.BlockSpec((tm, MEM_WIDTH), lambda i, j=j: (i, j))
    const = lambda shape: pl.BlockSpec(shape, lambda i: (0,) * len(shape))
    tile = pl.BlockSpec((tm, MEM_WIDTH), lambda i: (i, 0))
    piece = pltpu.HBM((S, MEM_WIDTH), BF16)
    sq = pltpu.HBM((MEM_LEN, MEM_WIDTH), F32)
    return pl.pallas_call(
        body, name="mem_bwd", grid=(S // tm,),
        in_specs=[col(11), col(12), col(3), tile, const((MEM_LEN, MEM_WIDTH)), const((MEM_LEN, MEM_WIDTH)),
                  const((1, MEM_WIDTH)), const((MEM_WIDTH, MEM_WIDTH))],
        out_specs=[tile, tile, const((MEM_LEN, MEM_WIDTH)), const((MEM_LEN, MEM_WIDTH)), const((1, MEM_WIDTH))],
        out_shape=[piece, piece, sq, sq, pltpu.HBM((1, MEM_WIDTH), F32)],
        compiler_params=_params(32, ("arbitrary",)),
    )(*_hbm(proj, proj, dycat, om, mk, mv, qg4, bd))


def _mem_kv_bwd(dmk, dmv, kraw, mem, gain, kg4, wkv_bf, hm_bf, bd):
    def body(dmk_ref, dmv_ref, kraw_ref, mem_ref, g_ref, kg_ref, w_ref, hm_ref, bd_ref, dw_ref, dg_ref, dkg_ref):
        bdv = bd_ref[...]
        kraw = kraw_ref[...]
        rk = lax.rsqrt(_split_dot(kraw * kraw, bdv) * (1.0 / HEAD_DIM) + EPS)
        kh = kraw * rk
        dmkv = dmk_ref[...]
        a = dmkv * kg_ref[...]
        dkraw = rk * (a - kh * (_split_dot(a * kh, bdv) * (1.0 / HEAD_DIM)))
        dkg_ref[...] = jnp.sum(dmkv * kh, axis=0, keepdims=True)
        dkv = jnp.concatenate([dkraw, dmv_ref[...]], axis=1).astype(BF16)
        dw = _tn(hm_ref[...], dkv).astype(BF16)
        rows_blk = D_MODEL // N_DEV
        for j in range(N_DEV):
            dw_ref[j] = dw[rows_blk * j:rows_blk * (j + 1)]
        dhm = _nt(dkv, w_ref[...])
        mv_ = mem_ref[...]
        r = lax.rsqrt(jnp.mean(mv_ * mv_, axis=-1, keepdims=True) + EPS)
        dg_ref[...] = jnp.sum(dhm * (mv_ * r), axis=0, keepdims=True)

    return pl.pallas_call(
        body, name="mem_kv_bwd",
        out_shape=[jax.ShapeDtypeStruct((N_DEV, D_MODEL // N_DEV, 2 * MEM_WIDTH), BF16),
                   jax.ShapeDtypeStruct((1, D_MODEL), F32), jax.ShapeDtypeStruct((1, MEM_WIDTH), F32)],
        compiler_params=_params(24),
    )(dmk, dmv, kraw, mem, gain, kg4, wkv_bf, hm_bf, bd)


def _out_loss(yg, ya, ym, x, tgt, wout_bf):
    S, D = x.shape
    tm = 512
    nsteps = S // tm
    rows_blk = D // N_DEV

    def body(yg_ref, ya_ref, ym_ref, x_ref, t_ref, w_ref, dout_ref, dycat_ref, dw_ref, loss_ref, acc_ref):
        i = pl.program_id(0)

        @pl.when(i == 0)
        def _():
            acc_ref[...] = jnp.zeros_like(acc_ref)
            loss_ref[...] = jnp.zeros_like(loss_ref)

        ycat = jnp.concatenate([yg_ref[...], ya_ref[...], ym_ref[...]], axis=1)
        w = w_ref[...]
        diff = (x_ref[...] + jnp.dot(ycat, w, preferred_element_type=F32)) - t_ref[...]
        loss_ref[...] += jnp.sum(diff * diff, axis=0, keepdims=True)
        dout = diff * (1.0 / D)
        dout_ref[...] = dout
        db = dout.astype(BF16)
        dycat_ref[...] = _nt(db, w)
        acc_ref[...] += _tn(ycat, db)

        @pl.when(i == nsteps - 1)
        def _():
            for j in range(N_DEV):
                dw_ref[j] = acc_ref[rows_blk * j:rows_blk * (j + 1), :].astype(BF16)

    tile = lambda w: pl.BlockSpec((tm, w), lambda i: (i, 0))
    const = lambda shape: pl.BlockSpec(shape, lambda i: (0,) * len(shape))
    return pl.pallas_call(
        body, name="out_loss", grid=(nsteps,),
        in_specs=[tile(GMLP_WIDTH), tile(ATTN_WIDTH), tile(MEM_WIDTH), tile(D), tile(D), const((D, D))],
        out_specs=[tile(D), tile(D), const((N_DEV, rows_blk, D)), const((1, D))],
        out_shape=[pltpu.HBM((S, D), F32), pltpu.HBM((S, D), F32),
                   pltpu.HBM((N_DEV, rows_blk, D), BF16), pltpu.HBM((1, D), F32)],
        scratch_shapes=[pltpu.VMEM((D, D), F32)],
        compiler_params=_params(40, ("arbitrary",)),
    )(*_hbm(yg, ya, ym, x, tgt, wout_bf))


def _piece_specs(pieces, tm):
    return [pl.BlockSpec((tm, p.shape[1]), lambda i: (i, 0)) for p in pieces]


def _in_bwd_dx(pieces, x, dout, gain, w_t, dw_blocks):
    S, D = x.shape
    N = w_t.shape[0]
    tm = 256
    n = len(pieces)
    nsteps = S // tm
    middle_step = nsteps // 8

    def body(*refs):
        piece_refs = refs[:n]
        x_ref, dout_ref, g_ref, w_ref, dwb_ref, gx_ref, dg_ref, gw_ref = refs[n:n + 8]
        rs = _ReduceScatter([dwb_ref], [gw_ref], *refs[n + 8:])
        i = pl.program_id(0)

        @pl.when(i == 0)
        def _():
            dg_ref[...] = jnp.zeros_like(dg_ref)
            rs.start()

        @pl.when(i == middle_step)
        def _():
            rs.middle()

        dproj = jnp.concatenate([r[...] for r in piece_refs], axis=1)
        dh = jnp.dot(dproj, w_ref[...], preferred_element_type=F32)
        xv = x_ref[...]
        r = lax.rsqrt(jnp.mean(xv * xv, axis=-1, keepdims=True) + EPS)
        xh = xv * r
        a = dh * g_ref[...]
        gx_ref[...] = dout_ref[...] + r * (a - xh * jnp.mean(a * xh, axis=-1, keepdims=True))
        dg_ref[...] += jnp.sum(dh * xh, axis=0, keepdims=True)

        @pl.when(i == nsteps - 1)
        def _():
            rs.finish()

    tile = pl.BlockSpec((tm, D), lambda i: (i, 0))
    const = lambda shape: pl.BlockSpec(shape, lambda i: (0,) * len(shape))
    vmem = pl.BlockSpec(memory_space=pltpu.VMEM)
    return pl.pallas_call(
        body, name="in_bwd_dx", grid=(nsteps,),
        in_specs=_piece_specs(pieces, tm)
        + [tile, tile, const((1, D)), pl.BlockSpec((N, D), lambda i: (0, 0), pipeline_mode=pl.Buffered(1)), vmem],
        out_specs=[tile, const((1, D)), vmem],
        out_shape=[pltpu.HBM((S, D), F32), pltpu.HBM((1, D), F32), jax.ShapeDtypeStruct(dw_blocks.shape[1:], F32)],
        scratch_shapes=_reduce_scatter_scratch([dw_blocks]),
        compiler_params=_params(56, ("arbitrary",)),
    )(*_hbm(*pieces, x, dout, gain, w_t), dw_blocks)


def _in_bwd_dw(pieces, h_bf, others):
    S, D = h_bf.shape
    N = sum(p.shape[1] for p in pieces)
    n_blk = N // N_DEV
    tm = 512
    n = len(pieces)
    k = len(others)
    nsteps = S // tm

    def body(*refs):
        piece_refs = refs[:n]
        h_ref = refs[n]
        other_refs = refs[n + 1:n + 1 + k]
        dw_ref = refs[n + 1 + k]
        sum_refs = refs[n + 2 + k:n + 2 + 2 * k]
        acc_ref = refs[n + 2 + 2 * k]
        rs = _ReduceScatter(other_refs, sum_refs, *refs[n + 3 + 2 * k:])
        i = pl.program_id(0)

        @pl.when(i == 0)
        def _():
            acc_ref[...] = jnp.zeros_like(acc_ref)
            rs.start()

        @pl.when(i == 1)
        def _():
            rs.middle()

        dproj = jnp.concatenate([r[...] for r in piece_refs], axis=1)
        acc_ref[...] += _tn(h_ref[...], dproj)

        @pl.when(i == nsteps - 1)
        def _():
            for j in range(N_DEV):
                dw_ref[j] = acc_ref[:, n_blk * j:n_blk * (j + 1)].T.astype(BF16)
            rs.finish()

    vmem = pl.BlockSpec(memory_space=pltpu.VMEM)
    return pl.pallas_call(
        body, name="in_bwd_dw", grid=(nsteps,),
        in_specs=_piece_specs(pieces, tm) + [pl.BlockSpec((tm, D), lambda i: (i, 0))] + [vmem] * k,
        out_specs=[pl.BlockSpec((N_DEV, n_blk, D), lambda i: (0, 0, 0))] + [vmem] * k,
        out_shape=[pltpu.HBM((N_DEV, n_blk, D), BF16)] + [jax.ShapeDtypeStruct(o.shape[1:], F32) for o in others],
        scratch_shapes=[pltpu.VMEM((D, N), F32)] + _reduce_scatter_scratch(others),
        compiler_params=_params(56, ("arbitrary",)),
    )(*_hbm(*pieces, h_bf), *others)


def _row_step(m):
    return max(t for t in range(16, 257, 16) if m % t == 0)


def _place():
    x, y, c = lax.axis_index("x"), lax.axis_index("y"), lax.axis_index("c")
    chips = [(1 - x, y), (x, 1 - y), (1 - x, 1 - y)]
    return x, y, c, chips


class _AllGather:
    def __init__(self, srcs, outs, send_sems, recv_sems, local_sems, first_sem=0):
        self.srcs, self.outs, self.n, self.first_sem = srcs, outs, len(srcs), first_sem
        self.send_sems, self.recv_sems, self.local_sems = send_sems, recv_sems, local_sems

    def _rows(self, a, px, py, pc):
        m = self.srcs[a].shape[0]
        return self.outs[a].at[pl.ds((4 * px + 2 * py + pc) * m, m), :]

    def _copy(self, a, k, block, to, src=None):
        row = self.first_sem + a
        return pltpu.make_async_remote_copy(
            src_ref=self._rows(a, *block) if src is None else src, dst_ref=self._rows(a, *block),
            send_sem=self.send_sems.at[row, k], recv_sem=self.recv_sems.at[row, k], device_id=to, device_id_type=MESH)

    def _mine(self):
        x, y, c, _ = _place()
        return [pltpu.make_async_copy(self.srcs[a], self._rows(a, x, y, c), self.local_sems.at[self.first_sem + a])
                for a in range(self.n)]

    def _first(self, far):
        x, y, c, chips = _place()
        out = []
        for a in range(self.n):
            if far:
                out.append(self._copy(a, 3, (x, y, c), (*chips[2], c), src=self.srcs[a]))
            else:
                out.append(self._copy(a, 0, (x, y, c), (x, y, 1 - c), src=self.srcs[a]))
                out += [self._copy(a, 1 + j, (x, y, c), (*chips[j], c), src=self.srcs[a]) for j in (1, 0)]
        return out

    def _passed(self, j):
        x, y, c, chips = _place()
        return [self._copy(a, 4 + j, (*chips[j], c), (x, y, 1 - c)) for a in range(self.n)]

    def start(self):
        for cp in self._mine() + self._first(far=False):
            cp.start()

    def start_far(self):
        for cp in self._first(far=True):
            cp.start()

    def from_chip(self, j):
        x, y, c, chips = _place()
        for a in range(self.n):
            self._copy(a, 1 + j, (*chips[j], c), (x, y, c)).wait_recv()
        for cp in self._passed(j):
            cp.start()

    def from_sibling(self, j=None):
        x, y, c, chips = _place()
        for a in range(self.n):
            block = (x, y, 1 - c) if j is None else (*chips[j], 1 - c)
            self._copy(a, 0 if j is None else 4 + j, block, (x, y, c)).wait_recv()

    def from_self(self):
        for cp in self._mine():
            cp.wait()

    def finish(self):
        for cp in (self._first(far=False) + self._first(far=True)
                   + self._passed(0) + self._passed(1) + self._passed(2)):
            cp.wait_send()

    def run(self):
        self.start()
        self.start_far()
        self.from_self()
        for j in range(3):
            self.from_chip(j)
        self.from_sibling()
        for j in range(3):
            self.from_sibling(j)
        self.finish()


def _gather_proj(x, gain, shards, xpos):
    S, D = x.shape
    n = len(shards)
    N = N_DEV * shards[0].shape[0]
    half = N // 2
    tm = 512
    nsteps = S // tm

    def body(*refs):
        xpos_ref, x_ref, g_ref = refs[:3]
        ins = refs[3:3 + n]
        proj_ref, h_ref = refs[3 + n:5 + n]
        outs = refs[5 + n:5 + 2 * n]
        casts = refs[5 + 2 * n:5 + 3 * n]
        whole = refs[5 + 3 * n:5 + 4 * n]
        sems = refs[5 + 4 * n:8 + 4 * n]
        ag = _AllGather(casts[:1], whole[:1], *sems)
        later = _AllGather(casts[1:], whole[1:], *sems, first_sem=1)
        out_sems, h_all = refs[8 + 4 * n:]
        p, i = pl.program_id(0), pl.program_id(1)
        rows = pl.ds(pl.multiple_of(i * tm, tm), tm)

        @pl.when((p == 0) & (i == 0))
        def _():
            for a in range(n):
                tr = _row_step(ins[a].shape[0])

                def cast(r, carry, a=a, tr=tr):
                    at = pl.ds(pl.multiple_of(r * tr, tr), tr)
                    casts[a][at, :] = ins[a][at, :].astype(BF16)
                    return carry
                lax.fori_loop(0, ins[a].shape[0] // tr, cast, 0)
            ag.start()

        @pl.when(p == 0)
        def _():
            xv = x_ref[...]
            r = lax.rsqrt(jnp.mean(xv * xv, axis=-1, keepdims=True) + EPS)
            h = ((xv * r) * g_ref[...]).astype(BF16)
            h_ref[...] = h
            h_all[rows, :] = h

        @pl.when((p == 1) & (i == 0))
        def _():
            ag.from_self()
            ag.from_chip(1)
            ag.start_far()
            later.start()
            later.start_far()
            ag.from_sibling()
            ag.from_sibling(1)

        @pl.when((p == 2) & (i == 0))
        def _():
            for j in (0, 2):
                ag.from_chip(j)
            for j in (0, 2):
                ag.from_sibling(j)

        @pl.when(p > 0)
        def _():
            which = (xpos_ref[0] + p - 1) % 2
            w_half = whole[0][pl.ds(pl.multiple_of(which * half, half), half), :]
            proj_ref[...] = _nt(h_all[rows, :], w_half)

        @pl.when((p == 2) & (i == nsteps - 1))
        def _():
            ag.finish()
            later.from_self()
            for j in range(3):
                later.from_chip(j)
            later.from_sibling()
            for j in range(3):
                later.from_sibling(j)
            later.finish()
            to_results = [pltpu.make_async_copy(whole[a], outs[a], out_sems.at[a]) for a in range(n)]
            for cp in to_results:
                cp.start()
            for cp in to_results:
                cp.wait()

    vmem = pl.BlockSpec(memory_space=pltpu.VMEM)
    hbm = pl.BlockSpec(memory_space=pl.ANY)
    gathered = [(N_DEV * a.shape[0], a.shape[1]) for a in shards]
    x_tile = lambda p, i, xp: (jnp.where(p == 0, i, nsteps - 1), 0)
    proj_tile = lambda p, i, xp: (jnp.where(p == 0, 0, i), (xp[0] + jnp.maximum(p - 1, 0)) % 2)
    grid_spec = pltpu.PrefetchScalarGridSpec(
        num_scalar_prefetch=1, grid=(3, nsteps),
        in_specs=[pl.BlockSpec((tm, D), x_tile), pl.BlockSpec((1, D), lambda p, i, xp: (0, 0))] + [vmem] * n,
        out_specs=[pl.BlockSpec((tm, half), proj_tile), pl.BlockSpec((tm, D), x_tile)] + [hbm] * n,
        scratch_shapes=[pltpu.VMEM(a.shape, BF16) for a in shards] + [pltpu.VMEM(g, BF16) for g in gathered]
        + [pltpu.SemaphoreType.DMA((n, 7)), pltpu.SemaphoreType.DMA((n, 7)), pltpu.SemaphoreType.DMA((n,)),
           pltpu.SemaphoreType.DMA((n,)), pltpu.VMEM((S, D), BF16)])
    return pl.pallas_call(
        body, name="gather_proj", grid_spec=grid_spec,
        out_shape=[pltpu.HBM((S, N), F32), pltpu.HBM((S, D), BF16)] + [pltpu.HBM(g, BF16) for g in gathered],
        compiler_params=_params(56, ("arbitrary", "arbitrary")),
    )(xpos, *_hbm(x, gain), *shards)


ROW_NORM, ROW_MEM_NORM, ROW_V_GAIN, ROW_B, ROW_ATTN_GAINS, ROW_MEM_GAINS, ROW_W_S, ROW_LOSS = 0, 8, 16, 18, 22, 23, 24, 536
SMALL_ROWS = 544


def _gather_small(dgain, dmgain, dvg, db2, dqg, dkg, dmqg, dmkg, dws, sq):
    def body(dgain_ref, dmgain_ref, dvg_ref, db2_ref, dqg_ref, dkg_ref, dmqg_ref, dmkg_ref, dws_ref, sq_ref,
             out_ref, mine, send_sems, recv_sems, local_sems):
        first = lax.broadcasted_iota(jnp.int32, (1, 128), 1) < HEAD_DIM
        for i in range(8):
            cols = slice(128 * i, 128 * (i + 1))
            mine[ROW_NORM + i:ROW_NORM + i + 1, :] = dgain_ref[:, cols]
            mine[ROW_MEM_NORM + i:ROW_MEM_NORM + i + 1, :] = dmgain_ref[:, cols]
            mine[ROW_LOSS + i:ROW_LOSS + i + 1, :] = sq_ref[:, cols]
        mine[ROW_V_GAIN:ROW_V_GAIN + 1, :] = dvg_ref[:, 0:128]
        mine[ROW_V_GAIN + 1:ROW_V_GAIN + 2, :] = dvg_ref[:, 128:256]
        bt = db2_ref[...].T
        for h in range(4):
            mine[ROW_B + h:ROW_B + h + 1, :] = bt[HEAD_DIM * h:HEAD_DIM * h + 1, :]

        def fold_heads(t):
            return t + pltpu.roll(t, HEAD_DIM, axis=1)
        aq = fold_heads(dqg_ref[0] + dqg_ref[1] + dqg_ref[2] + dqg_ref[3])
        ak = fold_heads(dkg_ref[0] + dkg_ref[1] + dkg_ref[2] + dkg_ref[3])
        mine[ROW_ATTN_GAINS:ROW_ATTN_GAINS + 1, :] = jnp.where(first, aq, ak)
        mq = fold_heads(dmqg_ref[:, 0:128] + dmqg_ref[:, 128:256])
        mk = fold_heads(dmkg_ref[:, 0:128] + dmkg_ref[:, 128:256])
        mine[ROW_MEM_GAINS:ROW_MEM_GAINS + 1, :] = jnp.where(first, mq, mk)
        mine[ROW_W_S:ROW_W_S + 4 * CHUNK, :] = dws_ref[...]
        _AllGather([mine], [out_ref], send_sems, recv_sems, local_sems).run()

    return pl.pallas_call(
        body, name="gather_small_grads",
        out_shape=jax.ShapeDtypeStruct((N_DEV * SMALL_ROWS, 128), F32),
        scratch_shapes=[pltpu.VMEM((SMALL_ROWS, 128), F32), pltpu.SemaphoreType.DMA((1, 7)),
                        pltpu.SemaphoreType.DMA((1, 7)), pltpu.SemaphoreType.DMA((1,))],
        compiler_params=_params(16),
    )(dgain, dmgain, dvg, db2, dqg, dkg, dmqg, dmkg, dws, sq)


def _reduce_scatter_scratch(arrs):
    n = len(arrs)
    return ([pltpu.VMEM((4,) + a.shape[1:], BF16) for a in arrs] + [pltpu.VMEM((3,) + a.shape[1:], BF16) for a in arrs]
            + [pltpu.SemaphoreType.DMA((n, 7)), pltpu.SemaphoreType.DMA((n, 7))])


class _ReduceScatter:
    def __init__(self, ins, outs, *scratch):
        n = len(ins)
        self.n, self.ins, self.outs = n, ins, outs
        self.half, self.quarter = scratch[:n], scratch[n:2 * n]
        self.send_sems, self.recv_sems = scratch[2 * n:]

    def _to_sibling(self):
        x, y, c, _ = _place()
        return [pltpu.make_async_remote_copy(
            src_ref=self.ins[a].at[2 * q + (1 - c)], dst_ref=self.half[a].at[q], send_sem=self.send_sems.at[a, q],
            recv_sem=self.recv_sems.at[a, q], device_id=(x, y, 1 - c), device_id_type=MESH)
            for a in range(self.n) for q in range(4)]

    def _to_chips(self):
        _, _, c, chips = _place()
        return [pltpu.make_async_remote_copy(
            src_ref=self.half[a].at[2 * chip[0] + chip[1]], dst_ref=self.quarter[a].at[k],
            send_sem=self.send_sems.at[a, 4 + k], recv_sem=self.recv_sems.at[a, 4 + k], device_id=(*chip, c),
            device_id_type=MESH) for a in range(self.n) for k, chip in enumerate(chips)]

    def _rows(self, a, fn):
        m = self.ins[a].shape[1]
        tr = _row_step(m)

        def step(i, carry):
            fn(pl.ds(pl.multiple_of(i * tr, tr), tr))
            return carry
        lax.fori_loop(0, m // tr, step, 0)

    def start(self):
        for cp in self._to_sibling():
            cp.start()

    def middle(self):
        _, _, c, _ = _place()
        for cp in self._to_sibling():
            cp.wait_recv()
        for a in range(self.n):
            for q in range(4):
                def add_half(rows, a=a, q=q):
                    both = self.ins[a][2 * q + c, rows, :].astype(F32) + self.half[a][q, rows, :].astype(F32)
                    self.half[a][q, rows, :] = both.astype(BF16)
                self._rows(a, add_half)
        for cp in self._to_chips():
            cp.start()

    def finish(self):
        x, y, _, _ = _place()
        for cp in self._to_chips():
            cp.wait_recv()
        for a in range(self.n):
            def add_quarters(rows, a=a):
                f = lambda t: t.astype(F32)
                self.outs[a][rows, :] = ((f(self.half[a][2 * x + y, rows, :]) + f(self.quarter[a][0, rows, :]))
                                         + (f(self.quarter[a][1, rows, :]) + f(self.quarter[a][2, rows, :])))
            self._rows(a, add_quarters)
        for cp in self._to_sibling() + self._to_chips():
            cp.wait_send()


def _adamw_math(w, g, m, v):
    m = ADAM_B1 * m + (1.0 - ADAM_B1) * g
    v = ADAM_B2 * v + (1.0 - ADAM_B2) * (g * g)
    m_hat = m / (1.0 - ADAM_B1 ** ADAM_STEP)
    v_hat = v / (1.0 - ADAM_B2 ** ADAM_STEP)
    delta = -ADAM_LR * (m_hat / (jnp.sqrt(v_hat) + ADAM_EPS) + ADAM_WD * w)
    return delta, m, v


def _adamw(w, g, m, v, name):
    R, C = w.shape
    tr = _row_step(R)

    def body(w_ref, g_ref, m_ref, v_ref, d_ref, nm_ref, nv_ref):
        d_ref[...], nm_ref[...], nv_ref[...] = _adamw_math(w_ref[...], g_ref[...], m_ref[...], v_ref[...])

    tile = pl.BlockSpec((tr, C), lambda i: (i, 0))
    out = pltpu.HBM((R, C), F32)
    return pl.pallas_call(
        body, name=name, grid=(R // tr,), in_specs=[tile] * 4, out_specs=[tile] * 3, out_shape=[out] * 3,
        compiler_params=_params(16, ("arbitrary",)),
    )(*_hbm(w, g, m, v))


SMALL = ("norm_gain", "gmlp_v_gain", "gmlp_w_s", "gmlp_b", "attn_q_gain", "attn_k_gain", "mem_norm_gain",
         "mem_q_gain", "mem_k_gain")
WEIGHTS = ("norm_gain", "w_in", "gmlp_v_gain", "gmlp_w_s", "gmlp_b", "attn_q_gain", "attn_k_gain",
           "mem_norm_gain", "w_mem_kv", "mem_q_gain", "mem_k_gain", "w_out")


def _adamw_small(w, m, v, g_all):
    k = len(SMALL)
    half = slice(0, HEAD_DIM), slice(HEAD_DIM, 2 * HEAD_DIM)

    def body(*refs):
        w_refs, m_refs, v_refs = refs[:k], refs[k:2 * k], refs[2 * k:3 * k]
        g_ref = refs[3 * k]
        outs = refs[3 * k + 1:7 * k + 1]
        loss_ref, gsum = refs[7 * k + 1:]

        part = SMALL_ROWS // 4
        for p in range(4):
            acc = g_ref[part * p:part * (p + 1), :]
            for dev in range(1, N_DEV):
                acc = acc + g_ref[dev * SMALL_ROWS + part * p:dev * SMALL_ROWS + part * (p + 1), :]
            gsum[part * p:part * (p + 1), :] = acc

        def update(name, at, g):
            i = SMALL.index(name)
            d, nm, nv = _adamw_math(w_refs[i][at], g, m_refs[i][at], v_refs[i][at])
            outs[i][at], outs[k + i][at], outs[2 * k + i][at], outs[3 * k + i][at] = g, d, nm, nv

        for i in range(8):
            at = (slice(0, 1), slice(128 * i, 128 * (i + 1)))
            update("norm_gain", at, gsum[ROW_NORM + i:ROW_NORM + i + 1, :])
            update("mem_norm_gain", at, gsum[ROW_MEM_NORM + i:ROW_MEM_NORM + i + 1, :])
        for h in range(4):
            row = (0, slice(h, h + 1), slice(None))
            update("gmlp_v_gain", row, gsum[ROW_V_GAIN + h // 2:ROW_V_GAIN + h // 2 + 1, half[h % 2]])
            update("gmlp_b", row, gsum[ROW_B + h:ROW_B + h + 1, :])
            update("gmlp_w_s", (0, h), gsum[ROW_W_S + CHUNK * h:ROW_W_S + CHUNK * (h + 1), :])
        whole = (slice(0, 1), slice(None))
        update("attn_q_gain", whole, gsum[ROW_ATTN_GAINS:ROW_ATTN_GAINS + 1, half[0]])
        update("attn_k_gain", whole, gsum[ROW_ATTN_GAINS:ROW_ATTN_GAINS + 1, half[1]])
        update("mem_q_gain", whole, gsum[ROW_MEM_GAINS:ROW_MEM_GAINS + 1, half[0]])
        update("mem_k_gain", whole, gsum[ROW_MEM_GAINS:ROW_MEM_GAINS + 1, half[1]])
        loss_ref[...] = jnp.sum(gsum[ROW_LOSS:ROW_LOSS + 8, :], keepdims=True) * (0.5 / D_MODEL)

    shapes = [jax.ShapeDtypeStruct(w[name].shape, F32) for name in SMALL]
    res = pl.pallas_call(
        body, name="adamw_small",
        out_shape=shapes * 4 + [jax.ShapeDtypeStruct((1, 1), F32)],
        scratch_shapes=[pltpu.VMEM((SMALL_ROWS, 128), F32)],
        compiler_params=_params(16),
    )(*[w[n] for n in SMALL], *[m[n] for n in SMALL], *[v[n] for n in SMALL], g_all)
    trees = [dict(zip(SMALL, res[j * k:(j + 1) * k])) for j in range(4)]
    return (*trees, res[4 * k])


def _grads(x, mem, tgt, w, shards):
    bd128, bd256 = _head_blockdiag(128), _head_blockdiag(256)
    gain = w["norm_gain"].reshape(1, D_MODEL)
    vg = w["gmlp_v_gain"].reshape(1, GMLP_WIDTH)
    w_s = w["gmlp_w_s"].reshape(4, CHUNK, CHUNK)
    b2 = jnp.repeat(w["gmlp_b"].reshape(4, CHUNK).T, HEAD_DIM, axis=1)
    qg2 = jnp.tile(w["attn_q_gain"].reshape(1, HEAD_DIM), (1, 2))
    kg2 = jnp.tile(w["attn_k_gain"].reshape(1, HEAD_DIM), (1, 2))
    mqg4 = jnp.tile(w["mem_q_gain"].reshape(1, HEAD_DIM), (1, 4))
    mkg4 = jnp.tile(w["mem_k_gain"].reshape(1, HEAD_DIM), (1, 4))
    mgain = w["mem_norm_gain"].reshape(1, D_MODEL)

    xpos = lax.axis_index("x").astype(jnp.int32).reshape(1)
    proj, h_bf, win_t, wkv_bf, wout_bf = _gather_proj(x, gain, shards, xpos)
    yg = _gmlp_fwd(proj, vg, w_s, b2, bd256)
    ya, att, lse = _attn_fwd(proj, qg2, kg2, bd128)
    hm_bf, kraw, mk, mv = _mem_kv(mem, mgain, wkv_bf, mkg4, bd256)
    ym, om = _mem_fwd(proj, mk, mv, mqg4, bd256)
    dout, dycat, dwout, sq = _out_loss(yg, ya, ym, x, tgt, wout_bf)

    du, dgv, dgg, dws, db2, dvg = _gmlp_bwd(proj, dycat, vg, w_s, b2, bd256)
    dq, dk, dv, dag, dqg, dkg = _attn_bwd(proj, dycat, att, lse, qg2, kg2, bd128)
    dmq, dmg, dmk, dmv, dmqg = _mem_bwd(proj, dycat, om, mk, mv, mqg4, bd256)
    dwkv, dmgain, dmkg = _mem_kv_bwd(dmk, dmv, kraw, mem, mgain, mkg4, wkv_bf, hm_bf, bd256)
    pieces = [du, dgv, dgg, dq, dk, dv, dag, dmq, dmg]
    dwin, g_wkv, g_wout = _in_bwd_dw(pieces, h_bf, [dwkv, dwout])
    grad_x, dgain, g_win = _in_bwd_dx(pieces, x, dout, gain, win_t, dwin)
    return grad_x, g_win, g_wkv, g_wout, (dgain, dmgain, dvg, db2, dqg, dkg, dmqg, dmkg, dws, sq)


def kernel(x, mem, norm_gain, w_in, gmlp_v_gain, gmlp_w_s, gmlp_b, attn_q_gain, attn_k_gain, mem_norm_gain, w_mem_kv, mem_q_gain, mem_k_gain, w_out, loss_target, m_norm_gain, m_w_in, m_gmlp_v_gain, m_gmlp_w_s, m_gmlp_b, m_attn_q_gain, m_attn_k_gain, m_mem_norm_gain, m_w_mem_kv, m_mem_q_gain, m_mem_k_gain, m_w_out, v_norm_gain, v_w_in, v_gmlp_v_gain, v_gmlp_w_s, v_gmlp_b, v_attn_q_gain, v_attn_k_gain, v_mem_norm_gain, v_w_mem_kv, v_mem_q_gain, v_mem_k_gain, v_w_out):
    w = dict(norm_gain=norm_gain, w_in=w_in, gmlp_v_gain=gmlp_v_gain, gmlp_w_s=gmlp_w_s, gmlp_b=gmlp_b,
             attn_q_gain=attn_q_gain, attn_k_gain=attn_k_gain, mem_norm_gain=mem_norm_gain, w_mem_kv=w_mem_kv,
             mem_q_gain=mem_q_gain, mem_k_gain=mem_k_gain, w_out=w_out)
    m = dict(norm_gain=m_norm_gain, w_in=m_w_in, gmlp_v_gain=m_gmlp_v_gain, gmlp_w_s=m_gmlp_w_s, gmlp_b=m_gmlp_b,
             attn_q_gain=m_attn_q_gain, attn_k_gain=m_attn_k_gain, mem_norm_gain=m_mem_norm_gain,
             w_mem_kv=m_w_mem_kv, mem_q_gain=m_mem_q_gain, mem_k_gain=m_mem_k_gain, w_out=m_w_out)
    v = dict(norm_gain=v_norm_gain, w_in=v_w_in, gmlp_v_gain=v_gmlp_v_gain, gmlp_w_s=v_gmlp_w_s, gmlp_b=v_gmlp_b,
             attn_q_gain=v_attn_q_gain, attn_k_gain=v_attn_k_gain, mem_norm_gain=v_mem_norm_gain,
             w_mem_kv=v_w_mem_kv, mem_q_gain=v_mem_q_gain, mem_k_gain=v_mem_k_gain, w_out=v_w_out)
    transposed = lambda t: jnp.transpose(t[0])

    grad_x, g_win, g_wkv, g_wout, small = _grads(
        x[0], mem[0], loss_target[0], w, [transposed(w_in), w_mem_kv[0], w_out[0]])
    small_all = _gather_small(*small)

    out_g, out_d, out_m, out_v, loss = _adamw_small(w, m, v, small_all)
    d_, m_, v_ = _adamw(transposed(w_in), g_win, transposed(m_w_in), transposed(v_w_in), "adamw_w_in")
    for tree, t in ((out_g, g_win), (out_d, d_), (out_m, m_), (out_v, v_)):
        tree["w_in"] = jnp.transpose(t)[None]
    for name, g in (("w_mem_kv", g_wkv), ("w_out", g_wout)):
        d_, m_, v_ = _adamw(w[name][0], g, m[name][0], v[name][0], "adamw_" + name)
        out_g[name], out_d[name], out_m[name], out_v[name] = g[None], d_[None], m_[None], v_[None]

    return (loss.reshape(()), grad_x[None], *[out_g[k] for k in WEIGHTS], *[out_d[k] for k in WEIGHTS],
            *[out_m[k] for k in WEIGHTS], *[out_v[k] for k in WEIGHTS])
```

```python
import functools
import math

import jax
import jax.numpy as jnp
from jax import lax
from jax.experimental import pallas as pl
from jax.experimental.pallas import tpu as pltpu

F32 = jnp.float32
BF16 = jnp.bfloat16

N_DEV = 8
D_MODEL = 1024
HEAD_DIM = 64
GMLP_WIDTH = 256
ATTN_WIDTH = 512
MEM_WIDTH = 256
MEM_LEN = 256
CHUNK = 128
BLOCK = 128
DILATIONS = (1, 4, 16)
CONFIG_ORDER = tuple(reversed(DILATIONS))
EPS = 1e-6
SCALE = 1.0 / math.sqrt(HEAD_DIM)
NEG = -1e30

ADAM_LR = 0.001
ADAM_B1 = 0.9
ADAM_B2 = 0.999
ADAM_EPS = 1e-08
ADAM_WD = 0.01
ADAM_STEP = 10

MIB = 1024 * 1024
MESH = pl.DeviceIdType.MESH

COL_AQ, COL_AK, COL_AV, COL_AG = 6, 10, 14, 18


def _params(vmem_mib, semantics=None):
    kw = dict(vmem_limit_bytes=vmem_mib * MIB)
    if semantics is not None:
        kw["dimension_semantics"] = semantics
    return pltpu.CompilerParams(**kw)


def _hbm(*arrs):
    return [pltpu.with_memory_space_constraint(a, pltpu.HBM) for a in arrs]


def _split_dot(x, sel_bf):
    hi = x.astype(BF16)
    lo = (x - hi.astype(F32)).astype(BF16)
    return jnp.dot(hi, sel_bf, preferred_element_type=F32) + jnp.dot(lo, sel_bf, preferred_element_type=F32)


def _nt(a, b):
    return lax.dot_general(a, b, (((1,), (1,)), ((), ())), preferred_element_type=F32)


def _tn(a, b):
    return lax.dot_general(a, b, (((0,), (0,)), ((), ())), preferred_element_type=F32)


def _silu_parts(g):
    sg = jax.nn.sigmoid(g)
    return g * sg, sg * (1.0 + g * (1.0 - sg))


def _head_index(shape):
    return lax.shift_right_logical(lax.broadcasted_iota(jnp.int32, shape, 1), HEAD_DIM.bit_length() - 1)


def _head_blockdiag(width):
    i = jnp.arange(width) // HEAD_DIM
    return (i[:, None] == i[None, :]).astype(BF16)


def _gmlp_masked_weights(ws_ref, transpose):
    t = lax.broadcasted_iota(jnp.int32, (CHUNK, CHUNK), 0)
    s = lax.broadcasted_iota(jnp.int32, (CHUNK, CHUNK), 1)
    parts = []
    for h in range(4):
        wm = jnp.where(s <= t, ws_ref[h], 0.0)
        parts.append(wm.T if transpose else wm)
    return jnp.concatenate(parts, axis=1).astype(BF16)


def _head_stack(v, head):
    return jnp.concatenate([jnp.where(head == h, v, 0.0) for h in range(4)], axis=0).astype(BF16)


def _gmlp_fwd(proj, vg, w_s, b2, bd):
    S = proj.shape[0]
    tm = 512

    def body(u_ref, v_ref, g_ref, vg_ref, ws_ref, b2_ref, bd_ref, y_ref):
        v = v_ref[...]
        ms = _split_dot(v * v, bd_ref[...]) * (1.0 / HEAD_DIM)
        vn = (v * lax.rsqrt(ms + EPS)) * vg_ref[...]
        wcat = _gmlp_masked_weights(ws_ref, False)
        head = _head_index((CHUNK, GMLP_WIDTH))
        for c in range(tm // CHUNK):
            rows = slice(c * CHUNK, (c + 1) * CHUNK)
            sp = jnp.dot(wcat, _head_stack(vn[rows], head), preferred_element_type=F32) + b2_ref[...]
            silu, _ = _silu_parts(g_ref[rows, :])
            y_ref[rows, :] = ((u_ref[rows, :] * sp) * silu).astype(BF16)

    col = lambda j: pl.BlockSpec((tm, GMLP_WIDTH), lambda i, j=j: (i, j))
    const = lambda shape: pl.BlockSpec(shape, lambda i: (0,) * len(shape))
    return pl.pallas_call(
        body, name="gmlp_fwd", grid=(S // tm,),
        in_specs=[col(0), col(1), col(2), const((1, GMLP_WIDTH)), const((4, CHUNK, CHUNK)),
                  const((CHUNK, GMLP_WIDTH)), const((GMLP_WIDTH, GMLP_WIDTH))],
        out_specs=pl.BlockSpec((tm, GMLP_WIDTH), lambda i: (i, 0)),
        out_shape=pltpu.HBM((S, GMLP_WIDTH), BF16),
        compiler_params=_params(24, ("arbitrary",)),
    )(*_hbm(proj, proj, proj, vg, w_s, b2, bd))


def _gmlp_bwd(proj, dycat, vg, w_s, b2, bd):
    S = proj.shape[0]
    tm = 512
    nsteps = S // tm

    def body(u_ref, v_ref, g_ref, dy_ref, vg_ref, ws_ref, b2_ref, bd_ref,
             du_ref, dv_ref, dg_ref, dws_ref, db2_ref, dvg_ref):
        i = pl.program_id(0)

        @pl.when(i == 0)
        def _():
            dws_ref[...] = jnp.zeros_like(dws_ref)
            db2_ref[...] = jnp.zeros_like(db2_ref)
            dvg_ref[...] = jnp.zeros_like(dvg_ref)

        bdv = bd_ref[...]
        v = v_ref[...]
        ms = _split_dot(v * v, bdv) * (1.0 / HEAD_DIM)
        rv = lax.rsqrt(ms + EPS)
        xhat = v * rv
        vgv = vg_ref[...]
        vn = xhat * vgv
        wcat = _gmlp_masked_weights(ws_ref, False)
        wcat_t = _gmlp_masked_weights(ws_ref, True)
        head = _head_index((CHUNK, GMLP_WIDTH))
        dvg = jnp.zeros((1, GMLP_WIDTH), F32)
        for c in range(tm // CHUNK):
            rows = slice(c * CHUNK, (c + 1) * CHUNK)
            vn_c = vn[rows]
            spb = jnp.dot(wcat, _head_stack(vn_c, head), preferred_element_type=F32) + b2_ref[...]
            silu, dsilu = _silu_parts(g_ref[rows, :])
            dy = dy_ref[rows, :]
            u = u_ref[rows, :]
            du_ref[rows, :] = (dy * spb * silu).astype(BF16)
            dg_ref[rows, :] = (dy * u * spb * dsilu).astype(BF16)
            dsp = dy * u * silu
            db2_ref[...] += dsp
            dstack = _head_stack(dsp, head)
            dvn = jnp.dot(wcat_t, dstack, preferred_element_type=F32)
            dws_ref[...] += _nt(dstack, vn_c.astype(BF16))
            xh = xhat[rows]
            a = dvn * vgv
            mean_ax = _split_dot(a * xh, bdv) * (1.0 / HEAD_DIM)
            dv_ref[rows, :] = (rv[rows] * (a - xh * mean_ax)).astype(BF16)
            dvg = dvg + jnp.sum(dvn * xh, axis=0, keepdims=True)
        dvg_ref[...] += dvg

        @pl.when(i == nsteps - 1)
        def _():
            t = lax.broadcasted_iota(jnp.int32, (4 * CHUNK, CHUNK), 0) % CHUNK
            s = lax.broadcasted_iota(jnp.int32, (4 * CHUNK, CHUNK), 1)
            dws_ref[...] = jnp.where(s <= t, dws_ref[...], 0.0)
            db2_ref[...] = _split_dot(db2_ref[...], bdv)

    col = lambda j: pl.BlockSpec((tm, GMLP_WIDTH), lambda i, j=j: (i, j))
    const = lambda shape: pl.BlockSpec(shape, lambda i: (0,) * len(shape))
    tile = pl.BlockSpec((tm, GMLP_WIDTH), lambda i: (i, 0))
    piece = pltpu.HBM((S, GMLP_WIDTH), BF16)
    return pl.pallas_call(
        body, name="gmlp_bwd", grid=(nsteps,),
        in_specs=[col(0), col(1), col(2), col(0), const((1, GMLP_WIDTH)), const((4, CHUNK, CHUNK)),
                  const((CHUNK, GMLP_WIDTH)), const((GMLP_WIDTH, GMLP_WIDTH))],
        out_specs=[tile, tile, tile, const((4 * CHUNK, CHUNK)), const((CHUNK, GMLP_WIDTH)), const((1, GMLP_WIDTH))],
        out_shape=[piece, piece, piece, pltpu.HBM((4 * CHUNK, CHUNK), F32),
                   pltpu.HBM((CHUNK, GMLP_WIDTH), F32), pltpu.HBM((1, GMLP_WIDTH), F32)],
        compiler_params=_params(32, ("arbitrary",)),
    )(*_hbm(proj, proj, proj, dycat, vg, w_s, b2, bd))


def _band_mask():
    qi = lax.broadcasted_iota(jnp.int32, (2 * BLOCK, 2 * BLOCK), 0) % BLOCK
    ki = lax.broadcasted_iota(jnp.int32, (2 * BLOCK, 2 * BLOCK), 1)
    return ((ki < BLOCK) & (ki >= qi)) | ((ki >= BLOCK) & ((ki - BLOCK) <= qi))


def _first_block_bias(blk, blocks_per_class):
    kcol = lax.broadcasted_iota(jnp.int32, (1, 2 * BLOCK), 1)
    kill = jnp.where((blk & (blocks_per_class - 1)) == 0, NEG, 0.0)
    return jnp.where(kcol < BLOCK, kill, 0.0)


def _two_heads(q, lo):
    zero = jnp.zeros_like(q)
    return jnp.concatenate([jnp.where(lo, q, zero), jnp.where(lo, zero, q)], axis=0)


def _block_tokens(blk, d, S):
    if d == 1:
        return pl.ds(pl.multiple_of(blk * BLOCK, BLOCK), BLOCK)
    blocks_per_class = S // d // BLOCK
    r = lax.shift_right_logical(blk, blocks_per_class.bit_length() - 1)
    n = blk & (blocks_per_class - 1)
    return pl.ds(r + n * (BLOCK * d), BLOCK, stride=d)


def _padded_block(blk):
    return pl.ds(pl.multiple_of((blk + 1) * BLOCK, BLOCK), BLOCK)


def _for_blocks(n_blocks, unroll, fn):
    def group(g, carry):
        for u in range(unroll):
            fn(g * unroll + u)
        return carry
    lax.fori_loop(0, n_blocks // unroll, group, 0)


def _attn_fwd(proj, qg2, kg2, bd):
    S = proj.shape[0]
    npairs = ATTN_WIDTH // 128
    tn = 512

    def body(q_ref, k_ref, v_ref, g_ref, qg_ref, kg_ref, bd_ref, y_ref, att_ref, lse_ref, qn, kn, kc, vc):
        bdv = bd_ref[...]
        lo = lax.broadcasted_iota(jnp.int32, (BLOCK, 128), 1) < HEAD_DIM
        band_mask = _band_mask()
        kc[pl.ds(0, BLOCK), :] = jnp.zeros((BLOCK, 128), BF16)
        vc[pl.ds(0, BLOCK), :] = jnp.zeros((BLOCK, 128), BF16)

        def norm_step(i, carry):
            rows = pl.ds(pl.multiple_of(i * tn, tn), tn)
            qv = q_ref[rows, :]
            kv = k_ref[rows, :]
            qn[rows, :] = (qv * lax.rsqrt(_split_dot(qv * qv, bdv) * (1.0 / HEAD_DIM) + EPS)) * (qg_ref[...] * SCALE)
            kn[rows, :] = (kv * lax.rsqrt(_split_dot(kv * kv, bdv) * (1.0 / HEAD_DIM) + EPS)) * kg_ref[...]
            return carry
        lax.fori_loop(0, S // tn, norm_step, 0)

        def fill(blk, d):
            tokens = _block_tokens(blk, d, S)
            kc[_padded_block(blk), :] = kn[tokens, :].astype(BF16)
            vc[_padded_block(blk), :] = v_ref[tokens, :].astype(BF16)

        ones_bf = jnp.ones((2 * BLOCK, 128), BF16)

        def block(blk, d):
            tokens = _block_tokens(blk, d, S)
            keys = pl.ds(pl.multiple_of(blk * BLOCK, BLOCK), 2 * BLOCK)
            q2 = _two_heads(qn[tokens, :].astype(BF16), lo)
            s = jnp.where(band_mask, _nt(q2, kc[keys, :]), NEG) + _first_block_bias(blk, S // d // BLOCK)
            m = jnp.max(s, axis=-1, keepdims=True)
            e = jnp.exp((s - m).astype(BF16))
            ol = jnp.dot(e, jnp.concatenate([vc[keys, :], ones_bf], axis=1), preferred_element_type=F32)
            l = ol[:, 128:]
            o2 = ol[:, :128] * (1.0 / l)
            lse2 = m + jnp.log(l)
            o = jnp.where(lo, o2[:BLOCK], o2[BLOCK:])
            lse = jnp.where(lo, lse2[:BLOCK], lse2[BLOCK:])
            if d != CONFIG_ORDER[0]:
                la = lse_ref[tokens, :]
                mx = jnp.maximum(la, lse)
                wa, wb = jnp.exp(la - mx), jnp.exp(lse - mx)
                t = wa + wb
                o = (wa * att_ref[tokens, :] + wb * o) / t
                lse = mx + jnp.log(t)
            att_ref[tokens, :] = o
            lse_ref[tokens, :] = lse

        for d in CONFIG_ORDER:
            _for_blocks(S // BLOCK, 4, functools.partial(fill, d=d))
            _for_blocks(S // BLOCK, 16, functools.partial(block, d=d))

        def gate_step(i, carry):
            rows = pl.ds(pl.multiple_of(i * tn, tn), tn)
            silu, _ = _silu_parts(g_ref[rows, :])
            y_ref[rows, :] = (att_ref[rows, :] * silu).astype(BF16)
            return carry
        lax.fori_loop(0, S // tn, gate_step, 0)

    col = lambda j0: pl.BlockSpec((S, 128), lambda p, j0=j0: (0, j0 + p))
    const = lambda shape: pl.BlockSpec(shape, lambda p: (0,) * len(shape))
    out = pl.BlockSpec((S, 128), lambda p: (0, p))
    return pl.pallas_call(
        body, name="attn_fwd", grid=(npairs,),
        in_specs=[col(COL_AQ), col(COL_AK), col(COL_AV), col(COL_AG), const((1, 128)), const((1, 128)),
                  const((128, 128))],
        out_specs=[out, out, out],
        out_shape=[pltpu.HBM((S, ATTN_WIDTH), BF16), pltpu.HBM((S, ATTN_WIDTH), F32),
                   pltpu.HBM((S, ATTN_WIDTH), F32)],
        scratch_shapes=[pltpu.VMEM((S, 128), F32), pltpu.VMEM((S, 128), F32),
                        pltpu.VMEM((S + BLOCK, 128), BF16), pltpu.VMEM((S + BLOCK, 128), BF16)],
        compiler_params=_params(48, ("arbitrary",)),
    )(*_hbm(proj, proj, proj, proj, qg2, kg2, bd))


def _attn_bwd(proj, dycat, att, lse, qg2, kg2, bd):
    S = proj.shape[0]
    npairs = ATTN_WIDTH // 128
    tn = 512

    def body(q_ref, k_ref, v_ref, g_ref, dy_ref, att_ref, lse_ref, qg_ref, kg_ref, bd_ref,
             dq_ref, dk_ref, dv_ref, dg_ref, dqg_ref, dkg_ref,
             qn, kn, rq_s, rk_s, kc, vc, do_s, dd_s, dqa, dka, dva):
        bdv = bd_ref[...]
        lo = lax.broadcasted_iota(jnp.int32, (BLOCK, 128), 1) < HEAD_DIM
        kc[pl.ds(0, BLOCK), :] = jnp.zeros((BLOCK, 128), BF16)
        vc[pl.ds(0, BLOCK), :] = jnp.zeros((BLOCK, 128), BF16)

        def prepare(i, carry):
            rows = pl.ds(pl.multiple_of(i * tn, tn), tn)
            qv = q_ref[rows, :]
            kv = k_ref[rows, :]
            rq = lax.rsqrt(_split_dot(qv * qv, bdv) * (1.0 / HEAD_DIM) + EPS)
            rk = lax.rsqrt(_split_dot(kv * kv, bdv) * (1.0 / HEAD_DIM) + EPS)
            rq_s[rows, :] = rq
            rk_s[rows, :] = rk
            qn[rows, :] = (qv * rq) * (qg_ref[...] * SCALE)
            kn[rows, :] = (kv * rk) * kg_ref[...]
            silu, dsilu = _silu_parts(g_ref[rows, :])
            dy = dy_ref[rows, :]
            at = att_ref[rows, :]
            do = dy * silu
            do_s[rows, :] = do
            dd_s[rows, :] = _split_dot(do * at, bdv)
            dg_ref[rows, :] = (dy * at * dsilu).astype(BF16)
            dka[rows, :] = jnp.zeros((tn, 128), F32)
            dva[rows, :] = jnp.zeros((tn, 128), F32)
            return carry
        lax.fori_loop(0, S // tn, prepare, 0)

        kt = lax.broadcasted_iota(jnp.int32, (2 * BLOCK, 2 * BLOCK), 0)
        qt = lax.broadcasted_iota(jnp.int32, (2 * BLOCK, 2 * BLOCK), 1) % BLOCK
        band_mask_t = ((kt < BLOCK) & (kt >= qt)) | ((kt >= BLOCK) & ((kt - BLOCK) <= qt))

        def per_query_row(t):
            tt = t.T
            return jnp.concatenate([tt[0:1, :], tt[HEAD_DIM:HEAD_DIM + 1, :]], axis=1)

        def fill(blk, d):
            tokens = _block_tokens(blk, d, S)
            kc[_padded_block(blk), :] = kn[tokens, :].astype(BF16)
            vc[_padded_block(blk), :] = v_ref[tokens, :].astype(BF16)

        def block(blk, d):
            tokens = _block_tokens(blk, d, S)
            keys = pl.ds(pl.multiple_of(blk * BLOCK, BLOCK), 2 * BLOCK)
            first = (blk & (S // d // BLOCK - 1)) == 0
            q2 = _two_heads(qn[tokens, :].astype(BF16), lo)
            do2 = _two_heads(do_s[tokens, :].astype(BF16), lo)
            lse_row = per_query_row(lse_ref[tokens, :])
            dd_row = per_query_row(dd_s[tokens, :])
            kb = kc[keys, :]
            vb = vc[keys, :]
            st = jnp.where(band_mask_t, _nt(kb, q2), NEG)
            st = jnp.concatenate([st[:BLOCK] + jnp.where(first, NEG, 0.0), st[BLOCK:]], axis=0)
            pt = jnp.exp(st - lse_row)
            dst = pt * (_nt(vb, do2) - dd_row)
            ptb = pt.astype(BF16)
            dstb = dst.astype(BF16)
            dv_band = jnp.dot(ptb, do2, preferred_element_type=F32)
            dk_band = jnp.dot(dstb, q2, preferred_element_type=F32)
            before = _block_tokens(jnp.where(first, blk, blk - 1), d, S)
            dka[before, :] = dka[before, :] + dk_band[:BLOCK]
            dva[before, :] = dva[before, :] + dv_band[:BLOCK]
            dka[tokens, :] = dka[tokens, :] + dk_band[BLOCK:]
            dva[tokens, :] = dva[tokens, :] + dv_band[BLOCK:]
            dq2 = _tn(dstb, kb)
            dq = jnp.where(lo, dq2[:BLOCK], dq2[BLOCK:])
            dqa[tokens, :] = dq if d == CONFIG_ORDER[0] else dqa[tokens, :] + dq

        for d in CONFIG_ORDER:
            _for_blocks(S // BLOCK, 4, functools.partial(fill, d=d))
            _for_blocks(S // BLOCK, 8, functools.partial(block, d=d))

        def out_step(i, carry):
            dqg, dkg = carry
            rows = pl.ds(pl.multiple_of(i * tn, tn), tn)
            rq = rq_s[rows, :]
            rk = rk_s[rows, :]
            qh = q_ref[rows, :] * rq
            kh = k_ref[rows, :] * rk
            dqs = dqa[rows, :] * SCALE
            dkn = dka[rows, :]
            aq = dqs * qg_ref[...]
            ak = dkn * kg_ref[...]
            dq_ref[rows, :] = (rq * (aq - qh * (_split_dot(aq * qh, bdv) * (1.0 / HEAD_DIM)))).astype(BF16)
            dk_ref[rows, :] = (rk * (ak - kh * (_split_dot(ak * kh, bdv) * (1.0 / HEAD_DIM)))).astype(BF16)
            dv_ref[rows, :] = dva[rows, :].astype(BF16)
            dqg = dqg + jnp.sum(dqs * qh, axis=0, keepdims=True)
            dkg = dkg + jnp.sum(dkn * kh, axis=0, keepdims=True)
            return dqg, dkg
        zero = jnp.zeros((1, 128), F32)
        dqg, dkg = lax.fori_loop(0, S // tn, out_step, (zero, zero))
        dqg_ref[0] = dqg
        dkg_ref[0] = dkg

    col = lambda j0: pl.BlockSpec((S, 128), lambda p, j0=j0: (0, j0 + p))
    col1 = lambda j0: pl.BlockSpec((S, 128), lambda p, j0=j0: (0, j0 + p), pipeline_mode=pl.Buffered(1))
    const = lambda shape: pl.BlockSpec(shape, lambda p: (0,) * len(shape))
    out = pl.BlockSpec((S, 128), lambda p: (0, p))
    gain_out = pl.BlockSpec((1, 1, 128), lambda p: (p, 0, 0))
    piece = pltpu.HBM((S, ATTN_WIDTH), BF16)
    gains = pltpu.HBM((npairs, 1, 128), F32)
    f32buf = pltpu.VMEM((S, 128), F32)
    bf16pad = pltpu.VMEM((S + BLOCK, 128), BF16)
    return pl.pallas_call(
        body, name="attn_bwd", grid=(npairs,),
        in_specs=[col(COL_AQ), col(COL_AK), col(COL_AV), col1(COL_AG), col1(GMLP_WIDTH // 128), col1(0), col(0),
                  const((1, 128)), const((1, 128)), const((128, 128))],
        out_specs=[out, out, out, out, gain_out, gain_out],
        out_shape=[piece, piece, piece, piece, gains, gains],
        scratch_shapes=[f32buf, f32buf, f32buf, f32buf, bf16pad, bf16pad, f32buf, f32buf, f32buf, f32buf, f32buf],
        compiler_params=_params(60, ("arbitrary",)),
    )(*_hbm(proj, proj, proj, proj, dycat, att, lse, qg2, kg2, bd))


def _mem_kv(mem, gain, wkv_bf, kg4, bd):
    def body(mem_ref, g_ref, w_ref, kg_ref, bd_ref, hm_ref, kraw_ref, mk_ref, mv_ref):
        mv_ = mem_ref[...]
        r = lax.rsqrt(jnp.mean(mv_ * mv_, axis=-1, keepdims=True) + EPS)
        hm = ((mv_ * r) * g_ref[...]).astype(BF16)
        hm_ref[...] = hm
        kv = jnp.dot(hm, w_ref[...], preferred_element_type=F32)
        kraw = kv[:, :MEM_WIDTH]
        kraw_ref[...] = kraw
        ms = _split_dot(kraw * kraw, bd_ref[...]) * (1.0 / HEAD_DIM)
        mk_ref[...] = (kraw * lax.rsqrt(ms + EPS)) * kg_ref[...]
        mv_ref[...] = kv[:, MEM_WIDTH:]

    sq = jax.ShapeDtypeStruct((MEM_LEN, MEM_WIDTH), F32)
    return pl.pallas_call(
        body, name="mem_kv",
        out_shape=[jax.ShapeDtypeStruct((MEM_LEN, D_MODEL), BF16), sq, sq, sq],
        compiler_params=_params(16),
    )(mem, gain, wkv_bf, kg4, bd)


def _mem_fwd(proj, mk, mv, qg4, bd):
    S = proj.shape[0]
    tm = 512

    def body(q_ref, g_ref, mk_ref, mv_ref, qg_ref, bd_ref, y_ref, om_ref):
        qv = q_ref[...]
        ms = _split_dot(qv * qv, bd_ref[...]) * (1.0 / HEAD_DIM)
        qs = (qv * lax.rsqrt(ms + EPS)) * (qg_ref[...] * SCALE)
        mkb = mk_ref[...].astype(BF16)
        mvb = mv_ref[...].astype(BF16)
        head = _head_index((tm, MEM_WIDTH))
        o = jnp.zeros((tm, MEM_WIDTH), F32)
        for h in range(4):
            s = _nt(jnp.where(head == h, qs, 0.0).astype(BF16), mkb)
            e = jnp.exp(s - jnp.max(s, axis=-1, keepdims=True))
            p = e * (1.0 / jnp.sum(e, axis=-1, keepdims=True))
            o = jnp.where(head == h, jnp.dot(p.astype(BF16), mvb, preferred_element_type=F32), o)
        om_ref[...] = o
        silu, _ = _silu_parts(g_ref[...])
        y_ref[...] = (o * silu).astype(BF16)

    col = lambda j: pl.BlockSpec((tm, MEM_WIDTH), lambda i, j=j: (i, j))
    const = lambda shape: pl.BlockSpec(shape, lambda i: (0,) * len(shape))
    tile = pl.BlockSpec((tm, MEM_WIDTH), lambda i: (i, 0))
    return pl.pallas_call(
        body, name="mem_fwd", grid=(S // tm,),
        in_specs=[col(11), col(12), const((MEM_LEN, MEM_WIDTH)), const((MEM_LEN, MEM_WIDTH)), const((1, MEM_WIDTH)),
                  const((MEM_WIDTH, MEM_WIDTH))],
        out_specs=[tile, tile],
        out_shape=[pltpu.HBM((S, MEM_WIDTH), BF16), pltpu.HBM((S, MEM_WIDTH), F32)],
        compiler_params=_params(24, ("arbitrary",)),
    )(*_hbm(proj, proj, mk, mv, qg4, bd))


def _mem_bwd(proj, dycat, om, mk, mv, qg4, bd):
    S = proj.shape[0]
    tm = 512

    def body(q_ref, g_ref, dy_ref, om_ref, mk_ref, mv_ref, qg_ref, bd_ref,
             dq_ref, dg_ref, dmk_ref, dmv_ref, dqg_ref):
        i = pl.program_id(0)

        @pl.when(i == 0)
        def _():
            dmk_ref[...] = jnp.zeros_like(dmk_ref)
            dmv_ref[...] = jnp.zeros_like(dmv_ref)
            dqg_ref[...] = jnp.zeros_like(dqg_ref)

        bdv = bd_ref[...]
        qv = q_ref[...]
        rq = lax.rsqrt(_split_dot(qv * qv, bdv) * (1.0 / HEAD_DIM) + EPS)
        qh = qv * rq
        qs = qh * (qg_ref[...] * SCALE)
        silu, dsilu = _silu_parts(g_ref[...])
        dy = dy_ref[...]
        o = om_ref[...]
        do = dy * silu
        dg_ref[...] = (dy * o * dsilu).astype(BF16)
        dd = _split_dot(do * o, bdv)
        mkb = mk_ref[...].astype(BF16)
        mvb = mv_ref[...].astype(BF16)
        head = _head_index((tm, MEM_WIDTH))
        dqs = jnp.zeros((tm, MEM_WIDTH), F32)
        for h in range(4):
            qhd = jnp.where(head == h, qs, 0.0).astype(BF16)
            doh = jnp.where(head == h, do, 0.0).astype(BF16)
            s = _nt(qhd, mkb)
            e = jnp.exp(s - jnp.max(s, axis=-1, keepdims=True))
            p = e * (1.0 / jnp.sum(e, axis=-1, keepdims=True))
            ds = p * (_nt(doh, mvb) - dd[:, h * HEAD_DIM:h * HEAD_DIM + 1])
            dsb = ds.astype(BF16)
            dmv_ref[...] += _tn(p.astype(BF16), doh)
            dmk_ref[...] += _tn(dsb, qhd)
            dqs = jnp.where(head == h, jnp.dot(dsb, mkb, preferred_element_type=F32), dqs)
        dqs = dqs * SCALE
        a = dqs * qg_ref[...]
        dq_ref[...] = (rq * (a - qh * (_split_dot(a * qh, bdv) * (1.0 / HEAD_DIM)))).astype(BF16)
        dqg_ref[...] += jnp.sum(dqs * qh, axis=0, keepdims=True)

    col = lambda j: pl.BlockSpec((tm, MEM_WIDTH), lambda i, j=j: (i, j))
    const = lambda shape: pl.BlockSpec(shape, lambda i: (0,) * len(shape))
    tile = pl.BlockSpec((tm, MEM_WIDTH), lambda i: (i, 0))
    piece = pltpu.HBM((S, MEM_WIDTH), BF16)
    sq = pltpu.HBM((MEM_LEN, MEM_WIDTH), F32)
    return pl.pallas_call(
        body, name="mem_bwd", grid=(S // tm,),
        in_specs=[col(11), col(12), col(3), tile, const((MEM_LEN, MEM_WIDTH)), const((MEM_LEN, MEM_WIDTH)),
                  const((1, MEM_WIDTH)), const((MEM_WIDTH, MEM_WIDTH))],
        out_specs=[tile, tile, const((MEM_LEN, MEM_WIDTH)), const((MEM_LEN, MEM_WIDTH)), const((1, MEM_WIDTH))],
        out_shape=[piece, piece, sq, sq, pltpu.HBM((1, MEM_WIDTH), F32)],
        compiler_params=_params(32, ("arbitrary",)),
    )(*_hbm(proj, proj, dycat, om, mk, mv, qg4, bd))


def _mem_kv_bwd(dmk, dmv, kraw, mem, gain, kg4, wkv_bf, hm_bf, bd):
    def body(dmk_ref, dmv_ref, kraw_ref, mem_ref, g_ref, kg_ref, w_ref, hm_ref, bd_ref, dw_ref, dg_ref, dkg_ref):
        bdv = bd_ref[...]
        kraw = kraw_ref[...]
        rk = lax.rsqrt(_split_dot(kraw * kraw, bdv) * (1.0 / HEAD_DIM) + EPS)
        kh = kraw * rk
        dmkv = dmk_ref[...]
        a = dmkv * kg_ref[...]
        dkraw = rk * (a - kh * (_split_dot(a * kh, bdv) * (1.0 / HEAD_DIM)))
        dkg_ref[...] = jnp.sum(dmkv * kh, axis=0, keepdims=True)
        dkv = jnp.concatenate([dkraw, dmv_ref[...]], axis=1).astype(BF16)
        dw = _tn(hm_ref[...], dkv).astype(BF16)
        rows_blk = D_MODEL // N_DEV
        for j in range(N_DEV):
            dw_ref[j] = dw[rows_blk * j:rows_blk * (j + 1)]
        dhm = _nt(dkv, w_ref[...])
        mv_ = mem_ref[...]
        r = lax.rsqrt(jnp.mean(mv_ * mv_, axis=-1, keepdims=True) + EPS)
        dg_ref[...] = jnp.sum(dhm * (mv_ * r), axis=0, keepdims=True)

    return pl.pallas_call(
        body, name="mem_kv_bwd",
        out_shape=[jax.ShapeDtypeStruct((N_DEV, D_MODEL // N_DEV, 2 * MEM_WIDTH), BF16),
                   jax.ShapeDtypeStruct((1, D_MODEL), F32), jax.ShapeDtypeStruct((1, MEM_WIDTH), F32)],
        compiler_params=_params(24),
    )(dmk, dmv, kraw, mem, gain, kg4, wkv_bf, hm_bf, bd)


def _out_loss(yg, ya, ym, x, tgt, wout_bf):
    S, D = x.shape
    tm = 512
    nsteps = S // tm
    rows_blk = D // N_DEV

    def body(yg_ref, ya_ref, ym_ref, x_ref, t_ref, w_ref, dout_ref, dycat_ref, dw_ref, loss_ref, acc_ref):
        i = pl.program_id(0)

        @pl.when(i == 0)
        def _():
            acc_ref[...] = jnp.zeros_like(acc_ref)
            loss_ref[...] = jnp.zeros_like(loss_ref)

        ycat = jnp.concatenate([yg_ref[...], ya_ref[...], ym_ref[...]], axis=1)
        w = w_ref[...]
        diff = (x_ref[...] + jnp.dot(ycat, w, preferred_element_type=F32)) - t_ref[...]
        loss_ref[...] += jnp.sum(diff * diff, axis=0, keepdims=True)
        dout = diff * (1.0 / D)
        dout_ref[...] = dout
        db = dout.astype(BF16)
        dycat_ref[...] = _nt(db, w)
        acc_ref[...] += _tn(ycat, db)

        @pl.when(i == nsteps - 1)
        def _():
            for j in range(N_DEV):
                dw_ref[j] = acc_ref[rows_blk * j:rows_blk * (j + 1), :].astype(BF16)

    tile = lambda w: pl.BlockSpec((tm, w), lambda i: (i, 0))
    const = lambda shape: pl.BlockSpec(shape, lambda i: (0,) * len(shape))
    return pl.pallas_call(
        body, name="out_loss", grid=(nsteps,),
        in_specs=[tile(GMLP_WIDTH), tile(ATTN_WIDTH), tile(MEM_WIDTH), tile(D), tile(D), const((D, D))],
        out_specs=[tile(D), tile(D), const((N_DEV, rows_blk, D)), const((1, D))],
        out_shape=[pltpu.HBM((S, D), F32), pltpu.HBM((S, D), F32),
                   pltpu.HBM((N_DEV, rows_blk, D), BF16), pltpu.HBM((1, D), F32)],
        scratch_shapes=[pltpu.VMEM((D, D), F32)],
        compiler_params=_params(40, ("arbitrary",)),
    )(*_hbm(yg, ya, ym, x, tgt, wout_bf))


def _piece_specs(pieces, tm):
    return [pl.BlockSpec((tm, p.shape[1]), lambda i: (i, 0)) for p in pieces]


def _in_bwd_dx(pieces, x, dout, gain, w_t, dw_blocks):
    S, D = x.shape
    N = w_t.shape[0]
    tm = 256
    n = len(pieces)
    nsteps = S // tm
    middle_step = nsteps // 8

    def body(*refs):
        piece_refs = refs[:n]
        x_ref, dout_ref, g_ref, w_ref, dwb_ref, gx_ref, dg_ref, gw_ref = refs[n:n + 8]
        rs = _ReduceScatter([dwb_ref], [gw_ref], *refs[n + 8:])
        i = pl.program_id(0)

        @pl.when(i == 0)
        def _():
            dg_ref[...] = jnp.zeros_like(dg_ref)
            rs.start()

        @pl.when(i == middle_step)
        def _():
            rs.middle()

        dproj = jnp.concatenate([r[...] for r in piece_refs], axis=1)
        dh = jnp.dot(dproj, w_ref[...], preferred_element_type=F32)
        xv = x_ref[...]
        r = lax.rsqrt(jnp.mean(xv * xv, axis=-1, keepdims=True) + EPS)
        xh = xv * r
        a = dh * g_ref[...]
        gx_ref[...] = dout_ref[...] + r * (a - xh * jnp.mean(a * xh, axis=-1, keepdims=True))
        dg_ref[...] += jnp.sum(dh * xh, axis=0, keepdims=True)

        @pl.when(i == nsteps - 1)
        def _():
            rs.finish()

    tile = pl.BlockSpec((tm, D), lambda i: (i, 0))
    const = lambda shape: pl.BlockSpec(shape, lambda i: (0,) * len(shape))
    vmem = pl.BlockSpec(memory_space=pltpu.VMEM)
    return pl.pallas_call(
        body, name="in_bwd_dx", grid=(nsteps,),
        in_specs=_piece_specs(pieces, tm)
        + [tile, tile, const((1, D)), pl.BlockSpec((N, D), lambda i: (0, 0), pipeline_mode=pl.Buffered(1)), vmem],
        out_specs=[tile, const((1, D)), vmem],
        out_shape=[pltpu.HBM((S, D), F32), pltpu.HBM((1, D), F32), jax.ShapeDtypeStruct(dw_blocks.shape[1:], F32)],
        scratch_shapes=_reduce_scatter_scratch([dw_blocks]),
        compiler_params=_params(56, ("arbitrary",)),
    )(*_hbm(*pieces, x, dout, gain, w_t), dw_blocks)


def _in_bwd_dw(pieces, h_bf, others):
    S, D = h_bf.shape
    N = sum(p.shape[1] for p in pieces)
    n_blk = N // N_DEV
    tm = 512
    n = len(pieces)
    k = len(others)
    nsteps = S // tm

    def body(*refs):
        piece_refs = refs[:n]
        h_ref = refs[n]
        other_refs = refs[n + 1:n + 1 + k]
        dw_ref = refs[n + 1 + k]
        sum_refs = refs[n + 2 + k:n + 2 + 2 * k]
        acc_ref = refs[n + 2 + 2 * k]
        rs = _ReduceScatter(other_refs, sum_refs, *refs[n + 3 + 2 * k:])
        i = pl.program_id(0)

        @pl.when(i == 0)
        def _():
            acc_ref[...] = jnp.zeros_like(acc_ref)
            rs.start()

        @pl.when(i == 1)
        def _():
            rs.middle()

        dproj = jnp.concatenate([r[...] for r in piece_refs], axis=1)
        acc_ref[...] += _tn(h_ref[...], dproj)

        @pl.when(i == nsteps - 1)
        def _():
            for j in range(N_DEV):
                dw_ref[j] = acc_ref[:, n_blk * j:n_blk * (j + 1)].T.astype(BF16)
            rs.finish()

    vmem = pl.BlockSpec(memory_space=pltpu.VMEM)
    return pl.pallas_call(
        body, name="in_bwd_dw", grid=(nsteps,),
        in_specs=_piece_specs(pieces, tm) + [pl.BlockSpec((tm, D), lambda i: (i, 0))] + [vmem] * k,
        out_specs=[pl.BlockSpec((N_DEV, n_blk, D), lambda i: (0, 0, 0))] + [vmem] * k,
        out_shape=[pltpu.HBM((N_DEV, n_blk, D), BF16)] + [jax.ShapeDtypeStruct(o.shape[1:], F32) for o in others],
        scratch_shapes=[pltpu.VMEM((D, N), F32)] + _reduce_scatter_scratch(others),
        compiler_params=_params(56, ("arbitrary",)),
    )(*_hbm(*pieces, h_bf), *others)


def _row_step(m):
    return max(t for t in range(16, 257, 16) if m % t == 0)


def _place():
    x, y, c = lax.axis_index("x"), lax.axis_index("y"), lax.axis_index("c")
    chips = [(1 - x, y), (x, 1 - y), (1 - x, 1 - y)]
    return x, y, c, chips


class _AllGather:
    def __init__(self, srcs, outs, send_sems, recv_sems, local_sems, first_sem=0):
        self.srcs, self.outs, self.n, self.first_sem = srcs, outs, len(srcs), first_sem
        self.send_sems, self.recv_sems, self.local_sems = send_sems, recv_sems, local_sems

    def _rows(self, a, px, py, pc):
        m = self.srcs[a].shape[0]
        return self.outs[a].at[pl.ds((4 * px + 2 * py + pc) * m, m), :]

    def _copy(self, a, k, block, to, src=None):
        row = self.first_sem + a
        return pltpu.make_async_remote_copy(
            src_ref=self._rows(a, *block) if src is None else src, dst_ref=self._rows(a, *block),
            send_sem=self.send_sems.at[row, k], recv_sem=self.recv_sems.at[row, k], device_id=to, device_id_type=MESH)

    def _mine(self):
        x, y, c, _ = _place()
        return [pltpu.make_async_copy(self.srcs[a], self._rows(a, x, y, c), self.local_sems.at[self.first_sem + a])
                for a in range(self.n)]

    def _first(self, far):
        x, y, c, chips = _place()
        out = []
        for a in range(self.n):
            if far:
                out.append(self._copy(a, 3, (x, y, c), (*chips[2], c), src=self.srcs[a]))
            else:
                out.append(self._copy(a, 0, (x, y, c), (x, y, 1 - c), src=self.srcs[a]))
                out += [self._copy(a, 1 + j, (x, y, c), (*chips[j], c), src=self.srcs[a]) for j in (1, 0)]
        return out

    def _passed(self, j):
        x, y, c, chips = _place()
        return [self._copy(a, 4 + j, (*chips[j], c), (x, y, 1 - c)) for a in range(self.n)]

    def start(self):
        for cp in self._mine() + self._first(far=False):
            cp.start()

    def start_far(self):
        for cp in self._first(far=True):
            cp.start()

    def from_chip(self, j):
        x, y, c, chips = _place()
        for a in range(self.n):
            self._copy(a, 1 + j, (*chips[j], c), (x, y, c)).wait_recv()
        for cp in self._passed(j):
            cp.start()

    def from_sibling(self, j=None):
        x, y, c, chips = _place()
        for a in range(self.n):
            block = (x, y, 1 - c) if j is None else (*chips[j], 1 - c)
            self._copy(a, 0 if j is None else 4 + j, block, (x, y, c)).wait_recv()

    def from_self(self):
        for cp in self._mine():
            cp.wait()

    def finish(self):
        for cp in (self._first(far=False) + self._first(far=True)
                   + self._passed(0) + self._passed(1) + self._passed(2)):
            cp.wait_send()

    def run(self):
        self.start()
        self.start_far()
        self.from_self()
        for j in range(3):
            self.from_chip(j)
        self.from_sibling()
        for j in range(3):
            self.from_sibling(j)
        self.finish()


def _gather_proj(x, gain, shards, xpos):
    S, D = x.shape
    n = len(shards)
    N = N_DEV * shards[0].shape[0]
    half = N // 2
    tm = 512
    nsteps = S // tm

    def body(*refs):
        xpos_ref, x_ref, g_ref = refs[:3]
        ins = refs[3:3 + n]
        proj_ref, h_ref = refs[3 + n:5 + n]
        outs = refs[5 + n:5 + 2 * n]
        casts = refs[5 + 2 * n:5 + 3 * n]
        whole = refs[5 + 3 * n:5 + 4 * n]
        sems = refs[5 + 4 * n:8 + 4 * n]
        ag = _AllGather(casts[:1], whole[:1], *sems)
        later = _AllGather(casts[1:], whole[1:], *sems, first_sem=1)
        out_sems, h_all = refs[8 + 4 * n:]
        p, i = pl.program_id(0), pl.program_id(1)
        rows = pl.ds(pl.multiple_of(i * tm, tm), tm)

        @pl.when((p == 0) & (i == 0))
        def _():
            for a in range(n):
                tr = _row_step(ins[a].shape[0])

                def cast(r, carry, a=a, tr=tr):
                    at = pl.ds(pl.multiple_of(r * tr, tr), tr)
                    casts[a][at, :] = ins[a][at, :].astype(BF16)
                    return carry
                lax.fori_loop(0, ins[a].shape[0] // tr, cast, 0)
            ag.start()

        @pl.when(p == 0)
        def _():
            xv = x_ref[...]
            r = lax.rsqrt(jnp.mean(xv * xv, axis=-1, keepdims=True) + EPS)
            h = ((xv * r) * g_ref[...]).astype(BF16)
            h_ref[...] = h
            h_all[rows, :] = h

        @pl.when((p == 1) & (i == 0))
        def _():
            ag.from_self()
            ag.from_chip(1)
            ag.start_far()
            later.start()
            later.start_far()
            ag.from_sibling()
            ag.from_sibling(1)

        @pl.when((p == 2) & (i == 0))
        def _():
            for j in (0, 2):
                ag.from_chip(j)
            for j in (0, 2):
                ag.from_sibling(j)

        @pl.when(p > 0)
        def _():
            which = (xpos_ref[0] + p - 1) % 2
            w_half = whole[0][pl.ds(pl.multiple_of(which * half, half), half), :]
            proj_ref[...] = _nt(h_all[rows, :], w_half)

        @pl.when((p == 2) & (i == nsteps - 1))
        def _():
            ag.finish()
            later.from_self()
            for j in range(3):
                later.from_chip(j)
            later.from_sibling()
            for j in range(3):
                later.from_sibling(j)
            later.finish()
            to_results = [pltpu.make_async_copy(whole[a], outs[a], out_sems.at[a]) for a in range(n)]
            for cp in to_results:
                cp.start()
            for cp in to_results:
                cp.wait()

    vmem = pl.BlockSpec(memory_space=pltpu.VMEM)
    hbm = pl.BlockSpec(memory_space=pl.ANY)
    gathered = [(N_DEV * a.shape[0], a.shape[1]) for a in shards]
    x_tile = lambda p, i, xp: (jnp.where(p == 0, i, nsteps - 1), 0)
    proj_tile = lambda p, i, xp: (jnp.where(p == 0, 0, i), (xp[0] + jnp.maximum(p - 1, 0)) % 2)
    grid_spec = pltpu.PrefetchScalarGridSpec(
        num_scalar_prefetch=1, grid=(3, nsteps),
        in_specs=[pl.BlockSpec((tm, D), x_tile), pl.BlockSpec((1, D), lambda p, i, xp: (0, 0))] + [vmem] * n,
        out_specs=[pl.BlockSpec((tm, half), proj_tile), pl.BlockSpec((tm, D), x_tile)] + [hbm] * n,
        scratch_shapes=[pltpu.VMEM(a.shape, BF16) for a in shards] + [pltpu.VMEM(g, BF16) for g in gathered]
        + [pltpu.SemaphoreType.DMA((n, 7)), pltpu.SemaphoreType.DMA((n, 7)), pltpu.SemaphoreType.DMA((n,)),
           pltpu.SemaphoreType.DMA((n,)), pltpu.VMEM((S, D), BF16)])
    return pl.pallas_call(
        body, name="gather_proj", grid_spec=grid_spec,
        out_shape=[pltpu.HBM((S, N), F32), pltpu.HBM((S, D), BF16)] + [pltpu.HBM(g, BF16) for g in gathered],
        compiler_params=_params(56, ("arbitrary", "arbitrary")),
    )(xpos, *_hbm(x, gain), *shards)


ROW_NORM, ROW_MEM_NORM, ROW_V_GAIN, ROW_B, ROW_ATTN_GAINS, ROW_MEM_GAINS, ROW_W_S, ROW_LOSS = 0, 8, 16, 18, 22, 23, 24, 536
SMALL_ROWS = 544


def _gather_small(dgain, dmgain, dvg, db2, dqg, dkg, dmqg, dmkg, dws, sq):
    def body(dgain_ref, dmgain_ref, dvg_ref, db2_ref, dqg_ref, dkg_ref, dmqg_ref, dmkg_ref, dws_ref, sq_ref,
             out_ref, mine, send_sems, recv_sems, local_sems):
        first = lax.broadcasted_iota(jnp.int32, (1, 128), 1) < HEAD_DIM
        for i in range(8):
            cols = slice(128 * i, 128 * (i + 1))
            mine[ROW_NORM + i:ROW_NORM + i + 1, :] = dgain_ref[:, cols]
            mine[ROW_MEM_NORM + i:ROW_MEM_NORM + i + 1, :] = dmgain_ref[:, cols]
            mine[ROW_LOSS + i:ROW_LOSS + i + 1, :] = sq_ref[:, cols]
        mine[ROW_V_GAIN:ROW_V_GAIN + 1, :] = dvg_ref[:, 0:128]
        mine[ROW_V_GAIN + 1:ROW_V_GAIN + 2, :] = dvg_ref[:, 128:256]
        bt = db2_ref[...].T
        for h in range(4):
            mine[ROW_B + h:ROW_B + h + 1, :] = bt[HEAD_DIM * h:HEAD_DIM * h + 1, :]

        def fold_heads(t):
            return t + pltpu.roll(t, HEAD_DIM, axis=1)
        aq = fold_heads(dqg_ref[0] + dqg_ref[1] + dqg_ref[2] + dqg_ref[3])
        ak = fold_heads(dkg_ref[0] + dkg_ref[1] + dkg_ref[2] + dkg_ref[3])
        mine[ROW_ATTN_GAINS:ROW_ATTN_GAINS + 1, :] = jnp.where(first, aq, ak)
        mq = fold_heads(dmqg_ref[:, 0:128] + dmqg_ref[:, 128:256])
        mk = fold_heads(dmkg_ref[:, 0:128] + dmkg_ref[:, 128:256])
        mine[ROW_MEM_GAINS:ROW_MEM_GAINS + 1, :] = jnp.where(first, mq, mk)
        mine[ROW_W_S:ROW_W_S + 4 * CHUNK, :] = dws_ref[...]
        _AllGather([mine], [out_ref], send_sems, recv_sems, local_sems).run()

    return pl.pallas_call(
        body, name="gather_small_grads",
        out_shape=jax.ShapeDtypeStruct((N_DEV * SMALL_ROWS, 128), F32),
        scratch_shapes=[pltpu.VMEM((SMALL_ROWS, 128), F32), pltpu.SemaphoreType.DMA((1, 7)),
                        pltpu.SemaphoreType.DMA((1, 7)), pltpu.SemaphoreType.DMA((1,))],
        compiler_params=_params(16),
    )(dgain, dmgain, dvg, db2, dqg, dkg, dmqg, dmkg, dws, sq)


def _reduce_scatter_scratch(arrs):
    n = len(arrs)
    return ([pltpu.VMEM((4,) + a.shape[1:], BF16) for a in arrs] + [pltpu.VMEM((3,) + a.shape[1:], BF16) for a in arrs]
            + [pltpu.SemaphoreType.DMA((n, 7)), pltpu.SemaphoreType.DMA((n, 7))])


class _ReduceScatter:
    def __init__(self, ins, outs, *scratch):
        n = len(ins)
        self.n, self.ins, self.outs = n, ins, outs
        self.half, self.quarter = scratch[:n], scratch[n:2 * n]
        self.send_sems, self.recv_sems = scratch[2 * n:]

    def _to_sibling(self):
        x, y, c, _ = _place()
        return [pltpu.make_async_remote_copy(
            src_ref=self.ins[a].at[2 * q + (1 - c)], dst_ref=self.half[a].at[q], send_sem=self.send_sems.at[a, q],
            recv_sem=self.recv_sems.at[a, q], device_id=(x, y, 1 - c), device_id_type=MESH)
            for a in range(self.n) for q in range(4)]

    def _to_chips(self):
        _, _, c, chips = _place()
        return [pltpu.make_async_remote_copy(
            src_ref=self.half[a].at[2 * chip[0] + chip[1]], dst_ref=self.quarter[a].at[k],
            send_sem=self.send_sems.at[a, 4 + k], recv_sem=self.recv_sems.at[a, 4 + k], device_id=(*chip, c),
            device_id_type=MESH) for a in range(self.n) for k, chip in enumerate(chips)]

    def _rows(self, a, fn):
        m = self.ins[a].shape[1]
        tr = _row_step(m)

        def step(i, carry):
            fn(pl.ds(pl.multiple_of(i * tr, tr), tr))
            return carry
        lax.fori_loop(0, m // tr, step, 0)

    def start(self):
        for cp in self._to_sibling():
            cp.start()

    def middle(self):
        _, _, c, _ = _place()
        for cp in self._to_sibling():
            cp.wait_recv()
        for a in range(self.n):
            for q in range(4):
                def add_half(rows, a=a, q=q):
                    both = self.ins[a][2 * q + c, rows, :].astype(F32) + self.half[a][q, rows, :].astype(F32)
                    self.half[a][q, rows, :] = both.astype(BF16)
                self._rows(a, add_half)
        for cp in self._to_chips():
            cp.start()

    def finish(self):
        x, y, _, _ = _place()
        for cp in self._to_chips():
            cp.wait_recv()
        for a in range(self.n):
            def add_quarters(rows, a=a):
                f = lambda t: t.astype(F32)
                self.outs[a][rows, :] = ((f(self.half[a][2 * x + y, rows, :]) + f(self.quarter[a][0, rows, :]))
                                         + (f(self.quarter[a][1, rows, :]) + f(self.quarter[a][2, rows, :])))
            self._rows(a, add_quarters)
        for cp in self._to_sibling() + self._to_chips():
            cp.wait_send()


def _adamw_math(w, g, m, v):
    m = ADAM_B1 * m + (1.0 - ADAM_B1) * g
    v = ADAM_B2 * v + (1.0 - ADAM_B2) * (g * g)
    m_hat = m / (1.0 - ADAM_B1 ** ADAM_STEP)
    v_hat = v / (1.0 - ADAM_B2 ** ADAM_STEP)
    delta = -ADAM_LR * (m_hat / (jnp.sqrt(v_hat) + ADAM_EPS) + ADAM_WD * w)
    return delta, m, v


def _adamw(w, g, m, v, name):
    R, C = w.shape
    tr = _row_step(R)

    def body(w_ref, g_ref, m_ref, v_ref, d_ref, nm_ref, nv_ref):
        d_ref[...], nm_ref[...], nv_ref[...] = _adamw_math(w_ref[...], g_ref[...], m_ref[...], v_ref[...])

    tile = pl.BlockSpec((tr, C), lambda i: (i, 0))
    out = pltpu.HBM((R, C), F32)
    return pl.pallas_call(
        body, name=name, grid=(R // tr,), in_specs=[tile] * 4, out_specs=[tile] * 3, out_shape=[out] * 3,
        compiler_params=_params(16, ("arbitrary",)),
    )(*_hbm(w, g, m, v))


SMALL = ("norm_gain", "gmlp_v_gain", "gmlp_w_s", "gmlp_b", "attn_q_gain", "attn_k_gain", "mem_norm_gain",
         "mem_q_gain", "mem_k_gain")
WEIGHTS = ("norm_gain", "w_in", "gmlp_v_gain", "gmlp_w_s", "gmlp_b", "attn_q_gain", "attn_k_gain",
           "mem_norm_gain", "w_mem_kv", "mem_q_gain", "mem_k_gain", "w_out")


def _adamw_small(w, m, v, g_all):
    k = len(SMALL)
    half = slice(0, HEAD_DIM), slice(HEAD_DIM, 2 * HEAD_DIM)

    def body(*refs):
        w_refs, m_refs, v_refs = refs[:k], refs[k:2 * k], refs[2 * k:3 * k]
        g_ref = refs[3 * k]
        outs = refs[3 * k + 1:7 * k + 1]
        loss_ref, gsum = refs[7 * k + 1:]

        part = SMALL_ROWS // 4
        for p in range(4):
            acc = g_ref[part * p:part * (p + 1), :]
            for dev in range(1, N_DEV):
                acc = acc + g_ref[dev * SMALL_ROWS + part * p:dev * SMALL_ROWS + part * (p + 1), :]
            gsum[part * p:part * (p + 1), :] = acc

        def update(name, at, g):
            i = SMALL.index(name)
            d, nm, nv = _adamw_math(w_refs[i][at], g, m_refs[i][at], v_refs[i][at])
            outs[i][at], outs[k + i][at], outs[2 * k + i][at], outs[3 * k + i][at] = g, d, nm, nv

        for i in range(8):
            at = (slice(0, 1), slice(128 * i, 128 * (i + 1)))
            update("norm_gain", at, gsum[ROW_NORM + i:ROW_NORM + i + 1, :])
            update("mem_norm_gain", at, gsum[ROW_MEM_NORM + i:ROW_MEM_NORM + i + 1, :])
        for h in range(4):
            row = (0, slice(h, h + 1), slice(None))
            update("gmlp_v_gain", row, gsum[ROW_V_GAIN + h // 2:ROW_V_GAIN + h // 2 + 1, half[h % 2]])
            update("gmlp_b", row, gsum[ROW_B + h:ROW_B + h + 1, :])
            update("gmlp_w_s", (0, h), gsum[ROW_W_S + CHUNK * h:ROW_W_S + CHUNK * (h + 1), :])
        whole = (slice(0, 1), slice(None))
        update("attn_q_gain", whole, gsum[ROW_ATTN_GAINS:ROW_ATTN_GAINS + 1, half[0]])
        update("attn_k_gain", whole, gsum[ROW_ATTN_GAINS:ROW_ATTN_GAINS + 1, half[1]])
        update("mem_q_gain", whole, gsum[ROW_MEM_GAINS:ROW_MEM_GAINS + 1, half[0]])
        update("mem_k_gain", whole, gsum[ROW_MEM_GAINS:ROW_MEM_GAINS + 1, half[1]])
        loss_ref[...] = jnp.sum(gsum[ROW_LOSS:ROW_LOSS + 8, :], keepdims=True) * (0.5 / D_MODEL)

    shapes = [jax.ShapeDtypeStruct(w[name].shape, F32) for name in SMALL]
    res = pl.pallas_call(
        body, name="adamw_small",
        out_shape=shapes * 4 + [jax.ShapeDtypeStruct((1, 1), F32)],
        scratch_shapes=[pltpu.VMEM((SMALL_ROWS, 128), F32)],
        compiler_params=_params(16),
    )(*[w[n] for n in SMALL], *[m[n] for n in SMALL], *[v[n] for n in SMALL], g_all)
    trees = [dict(zip(SMALL, res[j * k:(j + 1) * k])) for j in range(4)]
    return (*trees, res[4 * k])


def _grads(x, mem, tgt, w, shards):
    bd128, bd256 = _head_blockdiag(128), _head_blockdiag(256)
    gain = w["norm_gain"].reshape(1, D_MODEL)
    vg = w["gmlp_v_gain"].reshape(1, GMLP_WIDTH)
    w_s = w["gmlp_w_s"].reshape(4, CHUNK, CHUNK)
    b2 = jnp.repeat(w["gmlp_b"].reshape(4, CHUNK).T, HEAD_DIM, axis=1)
    qg2 = jnp.tile(w["attn_q_gain"].reshape(1, HEAD_DIM), (1, 2))
    kg2 = jnp.tile(w["attn_k_gain"].reshape(1, HEAD_DIM), (1, 2))
    mqg4 = jnp.tile(w["mem_q_gain"].reshape(1, HEAD_DIM), (1, 4))
    mkg4 = jnp.tile(w["mem_k_gain"].reshape(1, HEAD_DIM), (1, 4))
    mgain = w["mem_norm_gain"].reshape(1, D_MODEL)

    xpos = lax.axis_index("x").astype(jnp.int32).reshape(1)
    proj, h_bf, win_t, wkv_bf, wout_bf = _gather_proj(x, gain, shards, xpos)
    yg = _gmlp_fwd(proj, vg, w_s, b2, bd256)
    ya, att, lse = _attn_fwd(proj, qg2, kg2, bd128)
    hm_bf, kraw, mk, mv = _mem_kv(mem, mgain, wkv_bf, mkg4, bd256)
    ym, om = _mem_fwd(proj, mk, mv, mqg4, bd256)
    dout, dycat, dwout, sq = _out_loss(yg, ya, ym, x, tgt, wout_bf)

    du, dgv, dgg, dws, db2, dvg = _gmlp_bwd(proj, dycat, vg, w_s, b2, bd256)
    dq, dk, dv, dag, dqg, dkg = _attn_bwd(proj, dycat, att, lse, qg2, kg2, bd128)
    dmq, dmg, dmk, dmv, dmqg = _mem_bwd(proj, dycat, om, mk, mv, mqg4, bd256)
    dwkv, dmgain, dmkg = _mem_kv_bwd(dmk, dmv, kraw, mem, mgain, mkg4, wkv_bf, hm_bf, bd256)
    pieces = [du, dgv, dgg, dq, dk, dv, dag, dmq, dmg]
    dwin, g_wkv, g_wout = _in_bwd_dw(pieces, h_bf, [dwkv, dwout])
    grad_x, dgain, g_win = _in_bwd_dx(pieces, x, dout, gain, win_t, dwin)
    return grad_x, g_win, g_wkv, g_wout, (dgain, dmgain, dvg, db2, dqg, dkg, dmqg, dmkg, dws, sq)


def kernel(x, mem, norm_gain, w_in, gmlp_v_gain, gmlp_w_s, gmlp_b, attn_q_gain, attn_k_gain, mem_norm_gain, w_mem_kv, mem_q_gain, mem_k_gain, w_out, loss_target, m_norm_gain, m_w_in, m_gmlp_v_gain, m_gmlp_w_s, m_gmlp_b, m_attn_q_gain, m_attn_k_gain, m_mem_norm_gain, m_w_mem_kv, m_mem_q_gain, m_mem_k_gain, m_w_out, v_norm_gain, v_w_in, v_gmlp_v_gain, v_gmlp_w_s, v_gmlp_b, v_attn_q_gain, v_attn_k_gain, v_mem_norm_gain, v_w_mem_kv, v_mem_q_gain, v_mem_k_gain, v_w_out):
    w = dict(norm_gain=norm_gain, w_in=w_in, gmlp_v_gain=gmlp_v_gain, gmlp_w_s=gmlp_w_s, gmlp_b=gmlp_b,
             attn_q_gain=attn_q_gain, attn_k_gain=attn_k_gain, mem_norm_gain=mem_norm_gain, w_mem_kv=w_mem_kv,
             mem_q_gain=mem_q_gain, mem_k_gain=mem_k_gain, w_out=w_out)
    m = dict(norm_gain=m_norm_gain, w_in=m_w_in, gmlp_v_gain=m_gmlp_v_gain, gmlp_w_s=m_gmlp_w_s, gmlp_b=m_gmlp_b,
             attn_q_gain=m_attn_q_gain, attn_k_gain=m_attn_k_gain, mem_norm_gain=m_mem_norm_gain,
             w_mem_kv=m_w_mem_kv, mem_q_gain=m_mem_q_gain, mem_k_gain=m_mem_k_gain, w_out=m_w_out)
    v = dict(norm_gain=v_norm_gain, w_in=v_w_in, gmlp_v_gain=v_gmlp_v_gain, gmlp_w_s=v_gmlp_w_s, gmlp_b=v_gmlp_b,
             attn_q_gain=v_attn_q_gain, attn_k_gain=v_attn_k_gain, mem_norm_gain=v_mem_norm_gain,
             w_mem_kv=v_w_mem_kv, mem_q_gain=v_mem_q_gain, mem_k_gain=v_mem_k_gain, w_out=v_w_out)
    transposed = lambda t: jnp.transpose(t[0])

    grad_x, g_win, g_wkv, g_wout, small = _grads(
        x[0], mem[0], loss_target[0], w, [transposed(w_in), w_mem_kv[0], w_out[0]])
    small_all = _gather_small(*small)

    out_g, out_d, out_m, out_v, loss = _adamw_small(w, m, v, small_all)
    d_, m_, v_ = _adamw(transposed(w_in), g_win, transposed(m_w_in), transposed(v_w_in), "adamw_w_in")
    for tree, t in ((out_g, g_win), (out_d, d_), (out_m, m_), (out_v, v_)):
        tree["w_in"] = jnp.transpose(t)[None]
    for name, g in (("w_mem_kv", g_wkv), ("w_out", g_wout)):
        d_, m_, v_ = _adamw(w[name][0], g, m[name][0], v[name][0], "adamw_" + name)
        out_g[name], out_d[name], out_m[name], out_v[name] = g[None], d_[None], m_[None], v_[None]

    return (loss.reshape(()), grad_x[None], *[out_g[k] for k in WEIGHTS], *[out_d[k] for k in WEIGHTS],
            *[out_m[k] for k in WEIGHTS], *[out_v[k] for k in WEIGHTS])
```

```python
import functools
import math

import jax
import jax.numpy as jnp
from jax import lax
from jax.experimental import pallas as pl
from jax.experimental.pallas import tpu as pltpu

F32 = jnp.float32
BF16 = jnp.bfloat16

N_DEV = 8
D_MODEL = 1024
HEAD_DIM = 64
GMLP_WIDTH = 256
ATTN_WIDTH = 512
MEM_WIDTH = 256
MEM_LEN = 256
CHUNK = 128
BLOCK = 128
DILATIONS = (1, 4, 16)
CONFIG_ORDER = tuple(reversed(DILATIONS))
EPS = 1e-6
SCALE = 1.0 / math.sqrt(HEAD_DIM)
NEG = -1e30

ADAM_LR = 0.001
ADAM_B1 = 0.9
ADAM_B2 = 0.999
ADAM_EPS = 1e-08
ADAM_WD = 0.01
ADAM_STEP = 10

MIB = 1024 * 1024
MESH = pl.DeviceIdType.MESH

COL_AQ, COL_AK, COL_AV, COL_AG = 6, 10, 14, 18


def _params(vmem_mib, semantics=None):
    kw = dict(vmem_limit_bytes=vmem_mib * MIB)
    if semantics is not None:
        kw["dimension_semantics"] = semantics
    return pltpu.CompilerParams(**kw)


def _hbm(*arrs):
    return [pltpu.with_memory_space_constraint(a, pltpu.HBM) for a in arrs]


def _split_dot(x, sel_bf):
    hi = x.astype(BF16)
    lo = (x - hi.astype(F32)).astype(BF16)
    return jnp.dot(hi, sel_bf, preferred_element_type=F32) + jnp.dot(lo, sel_bf, preferred_element_type=F32)


def _nt(a, b):
    return lax.dot_general(a, b, (((1,), (1,)), ((), ())), preferred_element_type=F32)


def _tn(a, b):
    return lax.dot_general(a, b, (((0,), (0,)), ((), ())), preferred_element_type=F32)


def _silu_parts(g):
    sg = jax.nn.sigmoid(g)
    return g * sg, sg * (1.0 + g * (1.0 - sg))


def _head_index(shape):
    return lax.shift_right_logical(lax.broadcasted_iota(jnp.int32, shape, 1), HEAD_DIM.bit_length() - 1)


def _head_blockdiag(width):
    i = jnp.arange(width) // HEAD_DIM
    return (i[:, None] == i[None, :]).astype(BF16)


def _gmlp_masked_weights(ws_ref, transpose):
    t = lax.broadcasted_iota(jnp.int32, (CHUNK, CHUNK), 0)
    s = lax.broadcasted_iota(jnp.int32, (CHUNK, CHUNK), 1)
    parts = []
    for h in range(4):
        wm = jnp.where(s <= t, ws_ref[h], 0.0)
        parts.append(wm.T if transpose else wm)
    return jnp.concatenate(parts, axis=1).astype(BF16)


def _head_stack(v, head):
    return jnp.concatenate([jnp.where(head == h, v, 0.0) for h in range(4)], axis=0).astype(BF16)


def _gmlp_fwd(proj, vg, w_s, b2, bd):
    S = proj.shape[0]
    tm = 512

    def body(u_ref, v_ref, g_ref, vg_ref, ws_ref, b2_ref, bd_ref, y_ref):
        v = v_ref[...]
        ms = _split_dot(v * v, bd_ref[...]) * (1.0 / HEAD_DIM)
        vn = (v * lax.rsqrt(ms + EPS)) * vg_ref[...]
        wcat = _gmlp_masked_weights(ws_ref, False)
        head = _head_index((CHUNK, GMLP_WIDTH))
        for c in range(tm // CHUNK):
            rows = slice(c * CHUNK, (c + 1) * CHUNK)
            sp = jnp.dot(wcat, _head_stack(vn[rows], head), preferred_element_type=F32) + b2_ref[...]
            silu, _ = _silu_parts(g_ref[rows, :])
            y_ref[rows, :] = ((u_ref[rows, :] * sp) * silu).astype(BF16)

    col = lambda j: pl.BlockSpec((tm, GMLP_WIDTH), lambda i, j=j: (i, j))
    const = lambda shape: pl.BlockSpec(shape, lambda i: (0,) * len(shape))
    return pl.pallas_call(
        body, name="gmlp_fwd", grid=(S // tm,),
        in_specs=[col(0), col(1), col(2), const((1, GMLP_WIDTH)), const((4, CHUNK, CHUNK)),
                  const((CHUNK, GMLP_WIDTH)), const((GMLP_WIDTH, GMLP_WIDTH))],
        out_specs=pl.BlockSpec((tm, GMLP_WIDTH), lambda i: (i, 0)),
        out_shape=pltpu.HBM((S, GMLP_WIDTH), BF16),
        compiler_params=_params(24, ("arbitrary",)),
    )(*_hbm(proj, proj, proj, vg, w_s, b2, bd))


def _gmlp_bwd(proj, dycat, vg, w_s, b2, bd):
    S = proj.shape[0]
    tm = 512
    nsteps = S // tm

    def body(u_ref, v_ref, g_ref, dy_ref, vg_ref, ws_ref, b2_ref, bd_ref,
             du_ref, dv_ref, dg_ref, dws_ref, db2_ref, dvg_ref):
        i = pl.program_id(0)

        @pl.when(i == 0)
        def _():
            dws_ref[...] = jnp.zeros_like(dws_ref)
            db2_ref[...] = jnp.zeros_like(db2_ref)
            dvg_ref[...] = jnp.zeros_like(dvg_ref)

        bdv = bd_ref[...]
        v = v_ref[...]
        ms = _split_dot(v * v, bdv) * (1.0 / HEAD_DIM)
        rv = lax.rsqrt(ms + EPS)
        xhat = v * rv
        vgv = vg_ref[...]
        vn = xhat * vgv
        wcat = _gmlp_masked_weights(ws_ref, False)
        wcat_t = _gmlp_masked_weights(ws_ref, True)
        head = _head_index((CHUNK, GMLP_WIDTH))
        dvg = jnp.zeros((1, GMLP_WIDTH), F32)
        for c in range(tm // CHUNK):
            rows = slice(c * CHUNK, (c + 1) * CHUNK)
            vn_c = vn[rows]
            spb = jnp.dot(wcat, _head_stack(vn_c, head), preferred_element_type=F32) + b2_ref[...]
            silu, dsilu = _silu_parts(g_ref[rows, :])
            dy = dy_ref[rows, :]
            u = u_ref[rows, :]
            du_ref[rows, :] = (dy * spb * silu).astype(BF16)
            dg_ref[rows, :] = (dy * u * spb * dsilu).astype(BF16)
            dsp = dy * u * silu
            db2_ref[...] += dsp
            dstack = _head_stack(dsp, head)
            dvn = jnp.dot(wcat_t, dstack, preferred_element_type=F32)
            dws_ref[...] += _nt(dstack, vn_c.astype(BF16))
            xh = xhat[rows]
            a = dvn * vgv
            mean_ax = _split_dot(a * xh, bdv) * (1.0 / HEAD_DIM)
            dv_ref[rows, :] = (rv[rows] * (a - xh * mean_ax)).astype(BF16)
            dvg = dvg + jnp.sum(dvn * xh, axis=0, keepdims=True)
        dvg_ref[...] += dvg

        @pl.when(i == nsteps - 1)
        def _():
            t = lax.broadcasted_iota(jnp.int32, (4 * CHUNK, CHUNK), 0) % CHUNK
            s = lax.broadcasted_iota(jnp.int32, (4 * CHUNK, CHUNK), 1)
            dws_ref[...] = jnp.where(s <= t, dws_ref[...], 0.0)
            db2_ref[...] = _split_dot(db2_ref[...], bdv)

    col = lambda j: pl.BlockSpec((tm, GMLP_WIDTH), lambda i, j=j: (i, j))
    const = lambda shape: pl.BlockSpec(shape, lambda i: (0,) * len(shape))
    tile = pl.BlockSpec((tm, GMLP_WIDTH), lambda i: (i, 0))
    piece = pltpu.HBM((S, GMLP_WIDTH), BF16)
    return pl.pallas_call(
        body, name="gmlp_bwd", grid=(nsteps,),
        in_specs=[col(0), col(1), col(2), col(0), const((1, GMLP_WIDTH)), const((4, CHUNK, CHUNK)),
                  const((CHUNK, GMLP_WIDTH)), const((GMLP_WIDTH, GMLP_WIDTH))],
        out_specs=[tile, tile, tile, const((4 * CHUNK, CHUNK)), const((CHUNK, GMLP_WIDTH)), const((1, GMLP_WIDTH))],
        out_shape=[piece, piece, piece, pltpu.HBM((4 * CHUNK, CHUNK), F32),
                   pltpu.HBM((CHUNK, GMLP_WIDTH), F32), pltpu.HBM((1, GMLP_WIDTH), F32)],
        compiler_params=_params(32, ("arbitrary",)),
    )(*_hbm(proj, proj, proj, dycat, vg, w_s, b2, bd))


def _band_mask():
    qi = lax.broadcasted_iota(jnp.int32, (2 * BLOCK, 2 * BLOCK), 0) % BLOCK
    ki = lax.broadcasted_iota(jnp.int32, (2 * BLOCK, 2 * BLOCK), 1)
    return ((ki < BLOCK) & (ki >= qi)) | ((ki >= BLOCK) & ((ki - BLOCK) <= qi))


def _first_block_bias(blk, blocks_per_class):
    kcol = lax.broadcasted_iota(jnp.int32, (1, 2 * BLOCK), 1)
    kill = jnp.where((blk & (blocks_per_class - 1)) == 0, NEG, 0.0)
    return jnp.where(kcol < BLOCK, kill, 0.0)


def _two_heads(q, lo):
    zero = jnp.zeros_like(q)
    return jnp.concatenate([jnp.where(lo, q, zero), jnp.where(lo, zero, q)], axis=0)


def _block_tokens(blk, d, S):
    if d == 1:
        return pl.ds(pl.multiple_of(blk * BLOCK, BLOCK), BLOCK)
    blocks_per_class = S // d // BLOCK
    r = lax.shift_right_logical(blk, blocks_per_class.bit_length() - 1)
    n = blk & (blocks_per_class - 1)
    return pl.ds(r + n * (BLOCK * d), BLOCK, stride=d)


def _padded_block(blk):
    return pl.ds(pl.multiple_of((blk + 1) * BLOCK, BLOCK), BLOCK)


def _for_blocks(n_blocks, unroll, fn):
    def group(g, carry):
        for u in range(unroll):
            fn(g * unroll + u)
        return carry
    lax.fori_loop(0, n_blocks // unroll, group, 0)


def _attn_fwd(proj, qg2, kg2, bd):
    S = proj.shape[0]
    npairs = ATTN_WIDTH // 128
    tn = 512

    def body(q_ref, k_ref, v_ref, g_ref, qg_ref, kg_ref, bd_ref, y_ref, att_ref, lse_ref, qn, kn, kc, vc):
        bdv = bd_ref[...]
        lo = lax.broadcasted_iota(jnp.int32, (BLOCK, 128), 1) < HEAD_DIM
        band_mask = _band_mask()
        kc[pl.ds(0, BLOCK), :] = jnp.zeros((BLOCK, 128), BF16)
        vc[pl.ds(0, BLOCK), :] = jnp.zeros((BLOCK, 128), BF16)

        def norm_step(i, carry):
            rows = pl.ds(pl.multiple_of(i * tn, tn), tn)
            qv = q_ref[rows, :]
            kv = k_ref[rows, :]
            qn[rows, :] = (qv * lax.rsqrt(_split_dot(qv * qv, bdv) * (1.0 / HEAD_DIM) + EPS)) * (qg_ref[...] * SCALE)
            kn[rows, :] = (kv * lax.rsqrt(_split_dot(kv * kv, bdv) * (1.0 / HEAD_DIM) + EPS)) * kg_ref[...]
            return carry
        lax.fori_loop(0, S // tn, norm_step, 0)

        def fill(blk, d):
            tokens = _block_tokens(blk, d, S)
            kc[_padded_block(blk), :] = kn[tokens, :].astype(BF16)
            vc[_padded_block(blk), :] = v_ref[tokens, :].astype(BF16)

        ones_bf = jnp.ones((2 * BLOCK, 128), BF16)

        def block(blk, d):
            tokens = _block_tokens(blk, d, S)
            keys = pl.ds(pl.multiple_of(blk * BLOCK, BLOCK), 2 * BLOCK)
            q2 = _two_heads(qn[tokens, :].astype(BF16), lo)
            s = jnp.where(band_mask, _nt(q2, kc[keys, :]), NEG) + _first_block_bias(blk, S // d // BLOCK)
            m = jnp.max(s, axis=-1, keepdims=True)
            e = jnp.exp((s - m).astype(BF16))
            ol = jnp.dot(e, jnp.concatenate([vc[keys, :], ones_bf], axis=1), preferred_element_type=F32)
            l = ol[:, 128:]
            o2 = ol[:, :128] * (1.0 / l)
            lse2 = m + jnp.log(l)
            o = jnp.where(lo, o2[:BLOCK], o2[BLOCK:])
            lse = jnp.where(lo, lse2[:BLOCK], lse2[BLOCK:])
            if d != CONFIG_ORDER[0]:
                la = lse_ref[tokens, :]
                mx = jnp.maximum(la, lse)
                wa, wb = jnp.exp(la - mx), jnp.exp(lse - mx)
                t = wa + wb
                o = (wa * att_ref[tokens, :] + wb * o) / t
                lse = mx + jnp.log(t)
            att_ref[tokens, :] = o
            lse_ref[tokens, :] = lse

        for d in CONFIG_ORDER:
            _for_blocks(S // BLOCK, 4, functools.partial(fill, d=d))
            _for_blocks(S // BLOCK, 16, functools.partial(block, d=d))

        def gate_step(i, carry):
            rows = pl.ds(pl.multiple_of(i * tn, tn), tn)
            silu, _ = _silu_parts(g_ref[rows, :])
            y_ref[rows, :] = (att_ref[rows, :] * silu).astype(BF16)
            return carry
        lax.fori_loop(0, S // tn, gate_step, 0)

    col = lambda j0: pl.BlockSpec((S, 128), lambda p, j0=j0: (0, j0 + p))
    const = lambda shape: pl.BlockSpec(shape, lambda p: (0,) * len(shape))
    out = pl.BlockSpec((S, 128), lambda p: (0, p))
    return pl.pallas_call(
        body, name="attn_fwd", grid=(npairs,),
        in_specs=[col(COL_AQ), col(COL_AK), col(COL_AV), col(COL_AG), const((1, 128)), const((1, 128)),
                  const((128, 128))],
        out_specs=[out, out, out],
        out_shape=[pltpu.HBM((S, ATTN_WIDTH), BF16), pltpu.HBM((S, ATTN_WIDTH), F32),
                   pltpu.HBM((S, ATTN_WIDTH), F32)],
        scratch_shapes=[pltpu.VMEM((S, 128), F32), pltpu.VMEM((S, 128), F32),
                        pltpu.VMEM((S + BLOCK, 128), BF16), pltpu.VMEM((S + BLOCK, 128), BF16)],
        compiler_params=_params(48, ("arbitrary",)),
    )(*_hbm(proj, proj, proj, proj, qg2, kg2, bd))


def _attn_bwd(proj, dycat, att, lse, qg2, kg2, bd):
    S = proj.shape[0]
    npairs = ATTN_WIDTH // 128
    tn = 512

    def body(q_ref, k_ref, v_ref, g_ref, dy_ref, att_ref, lse_ref, qg_ref, kg_ref, bd_ref,
             dq_ref, dk_ref, dv_ref, dg_ref, dqg_ref, dkg_ref,
             qn, kn, rq_s, rk_s, kc, vc, do_s, dd_s, dqa, dka, dva):
        bdv = bd_ref[...]
        lo = lax.broadcasted_iota(jnp.int32, (BLOCK, 128), 1) < HEAD_DIM
        kc[pl.ds(0, BLOCK), :] = jnp.zeros((BLOCK, 128), BF16)
        vc[pl.ds(0, BLOCK), :] = jnp.zeros((BLOCK, 128), BF16)

        def prepare(i, carry):
            rows = pl.ds(pl.multiple_of(i * tn, tn), tn)
            qv = q_ref[rows, :]
            kv = k_ref[rows, :]
            rq = lax.rsqrt(_split_dot(qv * qv, bdv) * (1.0 / HEAD_DIM) + EPS)
            rk = lax.rsqrt(_split_dot(kv * kv, bdv) * (1.0 / HEAD_DIM) + EPS)
            rq_s[rows, :] = rq
            rk_s[rows, :] = rk
            qn[rows, :] = (qv * rq) * (qg_ref[...] * SCALE)
            kn[rows, :] = (kv * rk) * kg_ref[...]
            silu, dsilu = _silu_parts(g_ref[rows, :])
            dy = dy_ref[rows, :]
            at = att_ref[rows, :]
            do = dy * silu
            do_s[rows, :] = do
            dd_s[rows, :] = _split_dot(do * at, bdv)
            dg_ref[rows, :] = (dy * at * dsilu).astype(BF16)
            dka[rows, :] = jnp.zeros((tn, 128), F32)
            dva[rows, :] = jnp.zeros((tn, 128), F32)
            return carry
        lax.fori_loop(0, S // tn, prepare, 0)

        kt = lax.broadcasted_iota(jnp.int32, (2 * BLOCK, 2 * BLOCK), 0)
        qt = lax.broadcasted_iota(jnp.int32, (2 * BLOCK, 2 * BLOCK), 1) % BLOCK
        band_mask_t = ((kt < BLOCK) & (kt >= qt)) | ((kt >= BLOCK) & ((kt - BLOCK) <= qt))

        def per_query_row(t):
            tt = t.T
            return jnp.concatenate([tt[0:1, :], tt[HEAD_DIM:HEAD_DIM + 1, :]], axis=1)

        def fill(blk, d):
            tokens = _block_tokens(blk, d, S)
            kc[_padded_block(blk), :] = kn[tokens, :].astype(BF16)
            vc[_padded_block(blk), :] = v_ref[tokens, :].astype(BF16)

        def block(blk, d):
            tokens = _block_tokens(blk, d, S)
            keys = pl.ds(pl.multiple_of(blk * BLOCK, BLOCK), 2 * BLOCK)
            first = (blk & (S // d // BLOCK - 1)) == 0
            q2 = _two_heads(qn[tokens, :].astype(BF16), lo)
            do2 = _two_heads(do_s[tokens, :].astype(BF16), lo)
            lse_row = per_query_row(lse_ref[tokens, :])
            dd_row = per_query_row(dd_s[tokens, :])
            kb = kc[keys, :]
            vb = vc[keys, :]
            st = jnp.where(band_mask_t, _nt(kb, q2), NEG)
            st = jnp.concatenate([st[:BLOCK] + jnp.where(first, NEG, 0.0), st[BLOCK:]], axis=0)
            pt = jnp.exp(st - lse_row)
            dst = pt * (_nt(vb, do2) - dd_row)
            ptb = pt.astype(BF16)
            dstb = dst.astype(BF16)
            dv_band = jnp.dot(ptb, do2, preferred_element_type=F32)
            dk_band = jnp.dot(dstb, q2, preferred_element_type=F32)
            before = _block_tokens(jnp.where(first, blk, blk - 1), d, S)
            dka[before, :] = dka[before, :] + dk_band[:BLOCK]
            dva[before, :] = dva[before, :] + dv_band[:BLOCK]
            dka[tokens, :] = dka[tokens, :] + dk_band[BLOCK:]
            dva[tokens, :] = dva[tokens, :] + dv_band[BLOCK:]
            dq2 = _tn(dstb, kb)
            dq = jnp.where(lo, dq2[:BLOCK], dq2[BLOCK:])
            dqa[tokens, :] = dq if d == CONFIG_ORDER[0] else dqa[tokens, :] + dq

        for d in CONFIG_ORDER:
            _for_blocks(S // BLOCK, 4, functools.partial(fill, d=d))
            _for_blocks(S // BLOCK, 16, functools.partial(block, d=d))

        def out_step(i, carry):
            dqg, dkg = carry
            rows = pl.ds(pl.multiple_of(i * tn, tn), tn)
            rq = rq_s[rows, :]
            rk = rk_s[rows, :]
            qh = q_ref[rows, :] * rq
            kh = k_ref[rows, :] * rk
            dqs = dqa[rows, :] * SCALE
            dkn = dka[rows, :]
            aq = dqs * qg_ref[...]
            ak = dkn * kg_ref[...]
            dq_ref[rows, :] = (rq * (aq - qh * (_split_dot(aq * qh, bdv) * (1.0 / HEAD_DIM)))).astype(BF16)
            dk_ref[rows, :] = (rk * (ak - kh * (_split_dot(ak * kh, bdv) * (1.0 / HEAD_DIM)))).astype(BF16)
            dv_ref[rows, :] = dva[rows, :].astype(BF16)
            dqg = dqg + jnp.sum(dqs * qh, axis=0, keepdims=True)
            dkg = dkg + jnp.sum(dkn * kh, axis=0, keepdims=True)
            return dqg, dkg
        zero = jnp.zeros((1, 128), F32)
        dqg, dkg = lax.fori_loop(0, S // tn, out_step, (zero, zero))
        dqg_ref[0] = dqg
        dkg_ref[0] = dkg

    col = lambda j0: pl.BlockSpec((S, 128), lambda p, j0=j0: (0, j0 + p))
    col1 = lambda j0: pl.BlockSpec((S, 128), lambda p, j0=j0: (0, j0 + p), pipeline_mode=pl.Buffered(1))
    const = lambda shape: pl.BlockSpec(shape, lambda p: (0,) * len(shape))
    out = pl.BlockSpec((S, 128), lambda p: (0, p))
    gain_out = pl.BlockSpec((1, 1, 128), lambda p: (p, 0, 0))
    piece = pltpu.HBM((S, ATTN_WIDTH), BF16)
    gains = pltpu.HBM((npairs, 1, 128), F32)
    f32buf = pltpu.VMEM((S, 128), F32)
    bf16pad = pltpu.VMEM((S + BLOCK, 128), BF16)
    return pl.pallas_call(
        body, name="attn_bwd", grid=(npairs,),
        in_specs=[col(COL_AQ), col(COL_AK), col(COL_AV), col1(COL_AG), col1(GMLP_WIDTH // 128), col1(0), col(0),
                  const((1, 128)), const((1, 128)), const((128, 128))],
        out_specs=[out, out, out, out, gain_out, gain_out],
        out_shape=[piece, piece, piece, piece, gains, gains],
        scratch_shapes=[f32buf, f32buf, f32buf, f32buf, bf16pad, bf16pad, f32buf, f32buf, f32buf, f32buf, f32buf],
        compiler_params=_params(60, ("arbitrary",)),
    )(*_hbm(proj, proj, proj, proj, dycat, att, lse, qg2, kg2, bd))


def _mem_kv(mem, gain, wkv_bf, kg4, bd):
    def body(mem_ref, g_ref, w_ref, kg_ref, bd_ref, hm_ref, kraw_ref, mk_ref, mv_ref):
        mv_ = mem_ref[...]
        r = lax.rsqrt(jnp.mean(mv_ * mv_, axis=-1, keepdims=True) + EPS)
        hm = ((mv_ * r) * g_ref[...]).astype(BF16)
        hm_ref[...] = hm
        kv = jnp.dot(hm, w_ref[...], preferred_element_type=F32)
        kraw = kv[:, :MEM_WIDTH]
        kraw_ref[...] = kraw
        ms = _split_dot(kraw * kraw, bd_ref[...]) * (1.0 / HEAD_DIM)
        mk_ref[...] = (kraw * lax.rsqrt(ms + EPS)) * kg_ref[...]
        mv_ref[...] = kv[:, MEM_WIDTH:]

    sq = jax.ShapeDtypeStruct((MEM_LEN, MEM_WIDTH), F32)
    return pl.pallas_call(
        body, name="mem_kv",
        out_shape=[jax.ShapeDtypeStruct((MEM_LEN, D_MODEL), BF16), sq, sq, sq],
        compiler_params=_params(16),
    )(mem, gain, wkv_bf, kg4, bd)


def _mem_fwd(proj, mk, mv, qg4, bd):
    S = proj.shape[0]
    tm = 512

    def body(q_ref, g_ref, mk_ref, mv_ref, qg_ref, bd_ref, y_ref, om_ref):
        qv = q_ref[...]
        ms = _split_dot(qv * qv, bd_ref[...]) * (1.0 / HEAD_DIM)
        qs = (qv * lax.rsqrt(ms + EPS)) * (qg_ref[...] * SCALE)
        mkb = mk_ref[...].astype(BF16)
        mvb = mv_ref[...].astype(BF16)
        head = _head_index((tm, MEM_WIDTH))
        o = jnp.zeros((tm, MEM_WIDTH), F32)
        for h in range(4):
            s = _nt(jnp.where(head == h, qs, 0.0).astype(BF16), mkb)
            e = jnp.exp(s - jnp.max(s, axis=-1, keepdims=True))
            p = e * (1.0 / jnp.sum(e, axis=-1, keepdims=True))
            o = jnp.where(head == h, jnp.dot(p.astype(BF16), mvb, preferred_element_type=F32), o)
        om_ref[...] = o
        silu, _ = _silu_parts(g_ref[...])
        y_ref[...] = (o * silu).astype(BF16)

    col = lambda j: pl.BlockSpec((tm, MEM_WIDTH), lambda i, j=j: (i, j))
    const = lambda shape: pl.BlockSpec(shape, lambda i: (0,) * len(shape))
    tile = pl.BlockSpec((tm, MEM_WIDTH), lambda i: (i, 0))
    return pl.pallas_call(
        body, name="mem_fwd", grid=(S // tm,),
        in_specs=[col(11), col(12), const((MEM_LEN, MEM_WIDTH)), const((MEM_LEN, MEM_WIDTH)), const((1, MEM_WIDTH)),
                  const((MEM_WIDTH, MEM_WIDTH))],
        out_specs=[tile, tile],
        out_shape=[pltpu.HBM((S, MEM_WIDTH), BF16), pltpu.HBM((S, MEM_WIDTH), F32)],
        compiler_params=_params(24, ("arbitrary",)),
    )(*_hbm(proj, proj, mk, mv, qg4, bd))


def _mem_bwd(proj, dycat, om, mk, mv, qg4, bd):
    S = proj.shape[0]
    tm = 512

    def body(q_ref, g_ref, dy_ref, om_ref, mk_ref, mv_ref, qg_ref, bd_ref,
             dq_ref, dg_ref, dmk_ref, dmv_ref, dqg_ref):
        i = pl.program_id(0)

        @pl.when(i == 0)
        def _():
            dmk_ref[...] = jnp.zeros_like(dmk_ref)
            dmv_ref[...] = jnp.zeros_like(dmv_ref)
            dqg_ref[...] = jnp.zeros_like(dqg_ref)

        bdv = bd_ref[...]
        qv = q_ref[...]
        rq = lax.rsqrt(_split_dot(qv * qv, bdv) * (1.0 / HEAD_DIM) + EPS)
        qh = qv * rq
        qs = qh * (qg_ref[...] * SCALE)
        silu, dsilu = _silu_parts(g_ref[...])
        dy = dy_ref[...]
        o = om_ref[...]
        do = dy * silu
        dg_ref[...] = (dy * o * dsilu).astype(BF16)
        dd = _split_dot(do * o, bdv)
        mkb = mk_ref[...].astype(BF16)
        mvb = mv_ref[...].astype(BF16)
        head = _head_index((tm, MEM_WIDTH))
        dqs = jnp.zeros((tm, MEM_WIDTH), F32)
        for h in range(4):
            qhd = jnp.where(head == h, qs, 0.0).astype(BF16)
            doh = jnp.where(head == h, do, 0.0).astype(BF16)
            s = _nt(qhd, mkb)
            e = jnp.exp(s - jnp.max(s, axis=-1, keepdims=True))
            p = e * (1.0 / jnp.sum(e, axis=-1, keepdims=True))
            ds = p * (_nt(doh, mvb) - dd[:, h * HEAD_DIM:h * HEAD_DIM + 1])
            dsb = ds.astype(BF16)
            dmv_ref[...] += _tn(p.astype(BF16), doh)
            dmk_ref[...] += _tn(dsb, qhd)
            dqs = jnp.where(head == h, jnp.dot(dsb, mkb, preferred_element_type=F32), dqs)
        dqs = dqs * SCALE
        a = dqs * qg_ref[...]
        dq_ref[...] = (rq * (a - qh * (_split_dot(a * qh, bdv) * (1.0 / HEAD_DIM)))).astype(BF16)
        dqg_ref[...] += jnp.sum(dqs * qh, axis=0, keepdims=True)

    col = lambda j: pl.BlockSpec((tm, MEM_WIDTH), lambda i, j=j: (i, j))
    const = lambda shape: pl.BlockSpec(shape, lambda i: (0,) * len(shape))
    tile = pl.BlockSpec((tm, MEM_WIDTH), lambda i: (i, 0))
    piece = pltpu.HBM((S, MEM_WIDTH), BF16)
    sq = pltpu.HBM((MEM_LEN, MEM_WIDTH), F32)
    return pl.pallas_call(
        body, name="mem_bwd", grid=(S // tm,),
        in_specs=[col(11), col(12), col(3), tile, const((MEM_LEN, MEM_WIDTH)), const((MEM_LEN, MEM_WIDTH)),
                  const((1, MEM_WIDTH)), const((MEM_WIDTH, MEM_WIDTH))],
        out_specs=[tile, tile, const((MEM_LEN, MEM_WIDTH)), const((MEM_LEN, MEM_WIDTH)), const((1, MEM_WIDTH))],
        out_shape=[piece, piece, sq, sq, pltpu.HBM((1, MEM_WIDTH), F32)],
        compiler_params=_params(32, ("arbitrary",)),
    )(*_hbm(proj, proj, dycat, om, mk, mv, qg4, bd))


def _mem_kv_bwd(dmk, dmv, kraw, mem, gain, kg4, wkv_bf, hm_bf, bd):
    def body(dmk_ref, dmv_ref, kraw_ref, mem_ref, g_ref, kg_ref, w_ref, hm_ref, bd_ref, dw_ref, dg_ref, dkg_ref):
        bdv = bd_ref[...]
        kraw = kraw_ref[...]
        rk = lax.rsqrt(_split_dot(kraw * kraw, bdv) * (1.0 / HEAD_DIM) + EPS)
        kh = kraw * rk
        dmkv = dmk_ref[...]
        a = dmkv * kg_ref[...]
        dkraw = rk * (a - kh * (_split_dot(a * kh, bdv) * (1.0 / HEAD_DIM)))
        dkg_ref[...] = jnp.sum(dmkv * kh, axis=0, keepdims=True)
        dkv = jnp.concatenate([dkraw, dmv_ref[...]], axis=1).astype(BF16)
        dw = _tn(hm_ref[...], dkv).astype(BF16)
        rows_blk = D_MODEL // N_DEV
        for j in range(N_DEV):
            dw_ref[j] = dw[rows_blk * j:rows_blk * (j + 1)]
        dhm = _nt(dkv, w_ref[...])
        mv_ = mem_ref[...]
        r = lax.rsqrt(jnp.mean(mv_ * mv_, axis=-1, keepdims=True) + EPS)
        dg_ref[...] = jnp.sum(dhm * (mv_ * r), axis=0, keepdims=True)

    return pl.pallas_call(
        body, name="mem_kv_bwd",
        out_shape=[jax.ShapeDtypeStruct((N_DEV, D_MODEL // N_DEV, 2 * MEM_WIDTH), BF16),
                   jax.ShapeDtypeStruct((1, D_MODEL), F32), jax.ShapeDtypeStruct((1, MEM_WIDTH), F32)],
        compiler_params=_params(24),
    )(dmk, dmv, kraw, mem, gain, kg4, wkv_bf, hm_bf, bd)


def _out_loss(yg, ya, ym, x, tgt, wout_bf):
    S, D = x.shape
    tm = 512
    nsteps = S // tm
    rows_blk = D // N_DEV

    def body(yg_ref, ya_ref, ym_ref, x_ref, t_ref, w_ref, dout_ref, dycat_ref, dw_ref, loss_ref, acc_ref):
        i = pl.program_id(0)

        @pl.when(i == 0)
        def _():
            acc_ref[...] = jnp.zeros_like(acc_ref)
            loss_ref[...] = jnp.zeros_like(loss_ref)

        ycat = jnp.concatenate([yg_ref[...], ya_ref[...], ym_ref[...]], axis=1)
        w = w_ref[...]
        diff = (x_ref[...] + jnp.dot(ycat, w, preferred_element_type=F32)) - t_ref[...]
        loss_ref[...] += jnp.sum(diff * diff, axis=0, keepdims=True)
        dout = diff * (1.0 / D)
        dout_ref[...] = dout
        db = dout.astype(BF16)
        dycat_ref[...] = _nt(db, w)
        acc_ref[...] += _tn(ycat, db)

        @pl.when(i == nsteps - 1)
        def _():
            for j in range(N_DEV):
                dw_ref[j] = acc_ref[rows_blk * j:rows_blk * (j + 1), :].astype(BF16)

    tile = lambda w: pl.BlockSpec((tm, w), lambda i: (i, 0))
    const = lambda shape: pl.BlockSpec(shape, lambda i: (0,) * len(shape))
    return pl.pallas_call(
        body, name="out_loss", grid=(nsteps,),
        in_specs=[tile(GMLP_WIDTH), tile(ATTN_WIDTH), tile(MEM_WIDTH), tile(D), tile(D), const((D, D))],
        out_specs=[tile(D), tile(D), const((N_DEV, rows_blk, D)), const((1, D))],
        out_shape=[pltpu.HBM((S, D), F32), pltpu.HBM((S, D), F32),
                   pltpu.HBM((N_DEV, rows_blk, D), BF16), pltpu.HBM((1, D), F32)],
        scratch_shapes=[pltpu.VMEM((D, D), F32)],
        compiler_params=_params(40, ("arbitrary",)),
    )(*_hbm(yg, ya, ym, x, tgt, wout_bf))


def _piece_specs(pieces, tm):
    return [pl.BlockSpec((tm, p.shape[1]), lambda i: (i, 0)) for p in pieces]


def _in_bwd_dx(pieces, x, dout, gain, w_t, dw_blocks):
    S, D = x.shape
    N = w_t.shape[0]
    tm = 256
    n = len(pieces)
    nsteps = S // tm
    middle_step = nsteps // 8

    def body(*refs):
        piece_refs = refs[:n]
        x_ref, dout_ref, g_ref, w_ref, dwb_ref, gx_ref, dg_ref, gw_ref = refs[n:n + 8]
        rs = _ReduceScatter([dwb_ref], [gw_ref], *refs[n + 8:])
        i = pl.program_id(0)

        @pl.when(i == 0)
        def _():
            dg_ref[...] = jnp.zeros_like(dg_ref)
            rs.start()

        @pl.when(i == middle_step)
        def _():
            rs.middle()

        dproj = jnp.concatenate([r[...] for r in piece_refs], axis=1)
        dh = jnp.dot(dproj, w_ref[...], preferred_element_type=F32)
        xv = x_ref[...]
        r = lax.rsqrt(jnp.mean(xv * xv, axis=-1, keepdims=True) + EPS)
        xh = xv * r
        a = dh * g_ref[...]
        gx_ref[...] = dout_ref[...] + r * (a - xh * jnp.mean(a * xh, axis=-1, keepdims=True))
        dg_ref[...] += jnp.sum(dh * xh, axis=0, keepdims=True)

        @pl.when(i == nsteps - 1)
        def _():
            rs.finish()

    tile = pl.BlockSpec((tm, D), lambda i: (i, 0))
    const = lambda shape: pl.BlockSpec(shape, lambda i: (0,) * len(shape))
    vmem = pl.BlockSpec(memory_space=pltpu.VMEM)
    return pl.pallas_call(
        body, name="in_bwd_dx", grid=(nsteps,),
        in_specs=_piece_specs(pieces, tm)
        + [tile, tile, const((1, D)), pl.BlockSpec((N, D), lambda i: (0, 0), pipeline_mode=pl.Buffered(1)), vmem],
        out_specs=[tile, const((1, D)), vmem],
        out_shape=[pltpu.HBM((S, D), F32), pltpu.HBM((1, D), F32), jax.ShapeDtypeStruct(dw_blocks.shape[1:], F32)],
        scratch_shapes=_reduce_scatter_scratch([dw_blocks]),
        compiler_params=_params(56, ("arbitrary",)),
    )(*_hbm(*pieces, x, dout, gain, w_t), dw_blocks)


def _in_bwd_dw(pieces, h_bf, others):
    S, D = h_bf.shape
    N = sum(p.shape[1] for p in pieces)
    n_blk = N // N_DEV
    tm = 512
    n = len(pieces)
    k = len(others)
    nsteps = S // tm

    def body(*refs):
        piece_refs = refs[:n]
        h_ref = refs[n]
        other_refs = refs[n + 1:n + 1 + k]
        dw_ref = refs[n + 1 + k]
        sum_refs = refs[n + 2 + k:n + 2 + 2 * k]
        acc_ref = refs[n + 2 + 2 * k]
        rs = _ReduceScatter(other_refs, sum_refs, *refs[n + 3 + 2 * k:])
        i = pl.program_id(0)

        @pl.when(i == 0)
        def _():
            acc_ref[...] = jnp.zeros_like(acc_ref)
            rs.start()

        @pl.when(i == 1)
        def _():
            rs.middle()

        dproj = jnp.concatenate([r[...] for r in piece_refs], axis=1)
        acc_ref[...] += _tn(h_ref[...], dproj)

        @pl.when(i == nsteps - 1)
        def _():
            for j in range(N_DEV):
                dw_ref[j] = acc_ref[:, n_blk * j:n_blk * (j + 1)].T.astype(BF16)
            rs.finish()

    vmem = pl.BlockSpec(memory_space=pltpu.VMEM)
    return pl.pallas_call(
        body, name="in_bwd_dw", grid=(nsteps,),
        in_specs=_piece_specs(pieces, tm) + [pl.BlockSpec((tm, D), lambda i: (i, 0))] + [vmem] * k,
        out_specs=[pl.BlockSpec((N_DEV, n_blk, D), lambda i: (0, 0, 0))] + [vmem] * k,
        out_shape=[pltpu.HBM((N_DEV, n_blk, D), BF16)] + [jax.ShapeDtypeStruct(o.shape[1:], F32) for o in others],
        scratch_shapes=[pltpu.VMEM((D, N), F32)] + _reduce_scatter_scratch(others),
        compiler_params=_params(56, ("arbitrary",)),
    )(*_hbm(*pieces, h_bf), *others)


def _row_step(m):
    return max(t for t in range(16, 257, 16) if m % t == 0)


def _place():
    x, y, c = lax.axis_index("x"), lax.axis_index("y"), lax.axis_index("c")
    chips = [(1 - x, y), (x, 1 - y), (1 - x, 1 - y)]
    return x, y, c, chips


class _AllGather:
    def __init__(self, srcs, outs, send_sems, recv_sems, local_sems, first_sem=0):
        self.srcs, self.outs, self.n, self.first_sem = srcs, outs, len(srcs), first_sem
        self.send_sems, self.recv_sems, self.local_sems = send_sems, recv_sems, local_sems

    def _rows(self, a, px, py, pc):
        m = self.srcs[a].shape[0]
        return self.outs[a].at[pl.ds((4 * px + 2 * py + pc) * m, m), :]

    def _copy(self, a, k, block, to, src=None):
        row = self.first_sem + a
        return pltpu.make_async_remote_copy(
            src_ref=self._rows(a, *block) if src is None else src, dst_ref=self._rows(a, *block),
            send_sem=self.send_sems.at[row, k], recv_sem=self.recv_sems.at[row, k], device_id=to, device_id_type=MESH)

    def _mine(self):
        x, y, c, _ = _place()
        return [pltpu.make_async_copy(self.srcs[a], self._rows(a, x, y, c), self.local_sems.at[self.first_sem + a])
                for a in range(self.n)]

    def _first(self, far):
        x, y, c, chips = _place()
        out = []
        for a in range(self.n):
            if far:
                out.append(self._copy(a, 3, (x, y, c), (*chips[2], c), src=self.srcs[a]))
            else:
                out.append(self._copy(a, 0, (x, y, c), (x, y, 1 - c), src=self.srcs[a]))
                out += [self._copy(a, 1 + j, (x, y, c), (*chips[j], c), src=self.srcs[a]) for j in (1, 0)]
        return out

    def _passed(self, j):
        x, y, c, chips = _place()
        return [self._copy(a, 4 + j, (*chips[j], c), (x, y, 1 - c)) for a in range(self.n)]

    def start(self):
        for cp in self._mine() + self._first(far=False):
            cp.start()

    def start_far(self):
        for cp in self._first(far=True):
            cp.start()

    def from_chip(self, j):
        x, y, c, chips = _place()
        for a in range(self.n):
            self._copy(a, 1 + j, (*chips[j], c), (x, y, c)).wait_recv()
        for cp in self._passed(j):
            cp.start()

    def from_sibling(self, j=None):
        x, y, c, chips = _place()
        for a in range(self.n):
            block = (x, y, 1 - c) if j is None else (*chips[j], 1 - c)
            self._copy(a, 0 if j is None else 4 + j, block, (x, y, c)).wait_recv()

    def from_self(self):
        for cp in self._mine():
            cp.wait()

    def finish(self):
        for cp in (self._first(far=False) + self._first(far=True)
                   + self._passed(0) + self._passed(1) + self._passed(2)):
            cp.wait_send()

    def run(self):
        self.start()
        self.start_far()
        self.from_self()
        for j in range(3):
            self.from_chip(j)
        self.from_sibling()
        for j in range(3):
            self.from_sibling(j)
        self.finish()


def _gather_proj(x, gain, shards, xpos):
    S, D = x.shape
    n = len(shards)
    N = N_DEV * shards[0].shape[0]
    half = N // 2
    tm = 512
    nsteps = S // tm

    def body(*refs):
        xpos_ref, x_ref, g_ref = refs[:3]
        ins = refs[3:3 + n]
        proj_ref, h_ref = refs[3 + n:5 + n]
        outs = refs[5 + n:5 + 2 * n]
        casts = refs[5 + 2 * n:5 + 3 * n]
        whole = refs[5 + 3 * n:5 + 4 * n]
        sems = refs[5 + 4 * n:8 + 4 * n]
        ag = _AllGather(casts[:1], whole[:1], *sems)
        later = _AllGather(casts[1:], whole[1:], *sems, first_sem=1)
        out_sems, h_all = refs[8 + 4 * n:]
        p, i = pl.program_id(0), pl.program_id(1)
        rows = pl.ds(pl.multiple_of(i * tm, tm), tm)

        @pl.when((p == 0) & (i == 0))
        def _():
            for a in range(n):
                tr = _row_step(ins[a].shape[0])

                def cast(r, carry, a=a, tr=tr):
                    at = pl.ds(pl.multiple_of(r * tr, tr), tr)
                    casts[a][at, :] = ins[a][at, :].astype(BF16)
                    return carry
                lax.fori_loop(0, ins[a].shape[0] // tr, cast, 0)
            ag.start()

        @pl.when(p == 0)
        def _():
            xv = x_ref[...]
            r = lax.rsqrt(jnp.mean(xv * xv, axis=-1, keepdims=True) + EPS)
            h = ((xv * r) * g_ref[...]).astype(BF16)
            h_ref[...] = h
            h_all[rows, :] = h

        @pl.when((p == 1) & (i == 0))
        def _():
            ag.from_self()
            ag.from_chip(1)
            ag.start_far()
            later.start()
            later.start_far()
            ag.from_sibling()
            ag.from_sibling(1)

        @pl.when((p == 2) & (i == 0))
        def _():
            for j in (0, 2):
                ag.from_chip(j)
            for j in (0, 2):
                ag.from_sibling(j)

        @pl.when(p > 0)
        def _():
            which = (xpos_ref[0] + p - 1) % 2
            w_half = whole[0][pl.ds(pl.multiple_of(which * half, half), half), :]
            proj_ref[...] = _nt(h_all[rows, :], w_half)

        @pl.when((p == 2) & (i == nsteps - 1))
        def _():
            ag.finish()
            later.from_self()
            for j in range(3):
                later.from_chip(j)
            later.from_sibling()
            for j in range(3):
                later.from_sibling(j)
            later.finish()
            to_results = [pltpu.make_async_copy(whole[a], outs[a], out_sems.at[a]) for a in range(n)]
            for cp in to_results:
                cp.start()
            for cp in to_results:
                cp.wait()

    vmem = pl.BlockSpec(memory_space=pltpu.VMEM)
    hbm = pl.BlockSpec(memory_space=pl.ANY)
    gathered = [(N_DEV * a.shape[0], a.shape[1]) for a in shards]
    x_tile = lambda p, i, xp: (jnp.where(p == 0, i, nsteps - 1), 0)
    proj_tile = lambda p, i, xp: (jnp.where(p == 0, 0, i), (xp[0] + jnp.maximum(p - 1, 0)) % 2)
    grid_spec = pltpu.PrefetchScalarGridSpec(
        num_scalar_prefetch=1, grid=(3, nsteps),
        in_specs=[pl.BlockSpec((tm, D), x_tile), pl.BlockSpec((1, D), lambda p, i, xp: (0, 0))] + [vmem] * n,
        out_specs=[pl.BlockSpec((tm, half), proj_tile), pl.BlockSpec((tm, D), x_tile)] + [hbm] * n,
        scratch_shapes=[pltpu.VMEM(a.shape, BF16) for a in shards] + [pltpu.VMEM(g, BF16) for g in gathered]
        + [pltpu.SemaphoreType.DMA((n, 7)), pltpu.SemaphoreType.DMA((n, 7)), pltpu.SemaphoreType.DMA((n,)),
           pltpu.SemaphoreType.DMA((n,)), pltpu.VMEM((S, D), BF16)])
    return pl.pallas_call(
        body, name="gather_proj", grid_spec=grid_spec,
        out_shape=[pltpu.HBM((S, N), F32), pltpu.HBM((S, D), BF16)] + [pltpu.HBM(g, BF16) for g in gathered],
        compiler_params=_params(56, ("arbitrary", "arbitrary")),
    )(xpos, *_hbm(x, gain), *shards)


ROW_NORM, ROW_MEM_NORM, ROW_V_GAIN, ROW_B, ROW_ATTN_GAINS, ROW_MEM_GAINS, ROW_W_S, ROW_LOSS = 0, 8, 16, 18, 22, 23, 24, 536
SMALL_ROWS = 544


def _gather_small(dgain, dmgain, dvg, db2, dqg, dkg, dmqg, dmkg, dws, sq):
    def body(dgain_ref, dmgain_ref, dvg_ref, db2_ref, dqg_ref, dkg_ref, dmqg_ref, dmkg_ref, dws_ref, sq_ref,
             out_ref, mine, send_sems, recv_sems, local_sems):
        first = lax.broadcasted_iota(jnp.int32, (1, 128), 1) < HEAD_DIM
        for i in range(8):
            cols = slice(128 * i, 128 * (i + 1))
            mine[ROW_NORM + i:ROW_NORM + i + 1, :] = dgain_ref[:, cols]
            mine[ROW_MEM_NORM + i:ROW_MEM_NORM + i + 1, :] = dmgain_ref[:, cols]
            mine[ROW_LOSS + i:ROW_LOSS + i + 1, :] = sq_ref[:, cols]
        mine[ROW_V_GAIN:ROW_V_GAIN + 1, :] = dvg_ref[:, 0:128]
        mine[ROW_V_GAIN + 1:ROW_V_GAIN + 2, :] = dvg_ref[:, 128:256]
        bt = db2_ref[...].T
        for h in range(4):
            mine[ROW_B + h:ROW_B + h + 1, :] = bt[HEAD_DIM * h:HEAD_DIM * h + 1, :]

        def fold_heads(t):
            return t + pltpu.roll(t, HEAD_DIM, axis=1)
        aq = fold_heads(dqg_ref[0] + dqg_ref[1] + dqg_ref[2] + dqg_ref[3])
        ak = fold_heads(dkg_ref[0] + dkg_ref[1] + dkg_ref[2] + dkg_ref[3])
        mine[ROW_ATTN_GAINS:ROW_ATTN_GAINS + 1, :] = jnp.where(first, aq, ak)
        mq = fold_heads(dmqg_ref[:, 0:128] + dmqg_ref[:, 128:256])
        mk = fold_heads(dmkg_ref[:, 0:128] + dmkg_ref[:, 128:256])
        mine[ROW_MEM_GAINS:ROW_MEM_GAINS + 1, :] = jnp.where(first, mq, mk)
        mine[ROW_W_S:ROW_W_S + 4 * CHUNK, :] = dws_ref[...]
        _AllGather([mine], [out_ref], send_sems, recv_sems, local_sems).run()

    return pl.pallas_call(
        body, name="gather_small_grads",
        out_shape=jax.ShapeDtypeStruct((N_DEV * SMALL_ROWS, 128), F32),
        scratch_shapes=[pltpu.VMEM((SMALL_ROWS, 128), F32), pltpu.SemaphoreType.DMA((1, 7)),
                        pltpu.SemaphoreType.DMA((1, 7)), pltpu.SemaphoreType.DMA((1,))],
        compiler_params=_params(16),
    )(dgain, dmgain, dvg, db2, dqg, dkg, dmqg, dmkg, dws, sq)


def _reduce_scatter_scratch(arrs):
    n = len(arrs)
    return ([pltpu.VMEM((4,) + a.shape[1:], BF16) for a in arrs] + [pltpu.VMEM((3,) + a.shape[1:], BF16) for a in arrs]
            + [pltpu.SemaphoreType.DMA((n, 7)), pltpu.SemaphoreType.DMA((n, 7))])


class _ReduceScatter:
    def __init__(self, ins, outs, *scratch):
        n = len(ins)
        self.n, self.ins, self.outs = n, ins, outs
        self.half, self.quarter = scratch[:n], scratch[n:2 * n]
        self.send_sems, self.recv_sems = scratch[2 * n:]

    def _to_sibling(self):
        x, y, c, _ = _place()
        return [pltpu.make_async_remote_copy(
            src_ref=self.ins[a].at[2 * q + (1 - c)], dst_ref=self.half[a].at[q], send_sem=self.send_sems.at[a, q],
            recv_sem=self.recv_sems.at[a, q], device_id=(x, y, 1 - c), device_id_type=MESH)
            for a in range(self.n) for q in range(4)]

    def _to_chips(self):
        _, _, c, chips = _place()
        return [pltpu.make_async_remote_copy(
            src_ref=self.half[a].at[2 * chip[0] + chip[1]], dst_ref=self.quarter[a].at[k],
            send_sem=self.send_sems.at[a, 4 + k], recv_sem=self.recv_sems.at[a, 4 + k], device_id=(*chip, c),
            device_id_type=MESH) for a in range(self.n) for k, chip in enumerate(chips)]

    def _rows(self, a, fn):
        m = self.ins[a].shape[1]
        tr = _row_step(m)

        def step(i, carry):
            fn(pl.ds(pl.multiple_of(i * tr, tr), tr))
            return carry
        lax.fori_loop(0, m // tr, step, 0)

    def start(self):
        for cp in self._to_sibling():
            cp.start()

    def middle(self):
        _, _, c, _ = _place()
        for cp in self._to_sibling():
            cp.wait_recv()
        for a in range(self.n):
            for q in range(4):
                def add_half(rows, a=a, q=q):
                    both = self.ins[a][2 * q + c, rows, :].astype(F32) + self.half[a][q, rows, :].astype(F32)
                    self.half[a][q, rows, :] = both.astype(BF16)
                self._rows(a, add_half)
        for cp in self._to_chips():
            cp.start()

    def finish(self):
        x, y, _, _ = _place()
        for cp in self._to_chips():
            cp.wait_recv()
        for a in range(self.n):
            def add_quarters(rows, a=a):
                f = lambda t: t.astype(F32)
                self.outs[a][rows, :] = ((f(self.half[a][2 * x + y, rows, :]) + f(self.quarter[a][0, rows, :]))
                                         + (f(self.quarter[a][1, rows, :]) + f(self.quarter[a][2, rows, :])))
            self._rows(a, add_quarters)
        for cp in self._to_sibling() + self._to_chips():
            cp.wait_send()


def _adamw_math(w, g, m, v):
    m = ADAM_B1 * m + (1.0 - ADAM_B1) * g
    v = ADAM_B2 * v + (1.0 - ADAM_B2) * (g * g)
    m_hat = m / (1.0 - ADAM_B1 ** ADAM_STEP)
    v_hat = v / (1.0 - ADAM_B2 ** ADAM_STEP)
    delta = -ADAM_LR * (m_hat / (jnp.sqrt(v_hat) + ADAM_EPS) + ADAM_WD * w)
    return delta, m, v


def _adamw(w, g, m, v, name):
    R, C = w.shape
    tr = _row_step(R)

    def body(w_ref, g_ref, m_ref, v_ref, d_ref, nm_ref, nv_ref):
        d_ref[...], nm_ref[...], nv_ref[...] = _adamw_math(w_ref[...], g_ref[...], m_ref[...], v_ref[...])

    tile = pl.BlockSpec((tr, C), lambda i: (i, 0))
    out = pltpu.HBM((R, C), F32)
    return pl.pallas_call(
        body, name=name, grid=(R // tr,), in_specs=[tile] * 4, out_specs=[tile] * 3, out_shape=[out] * 3,
        compiler_params=_params(16, ("arbitrary",)),
    )(*_hbm(w, g, m, v))


SMALL = ("norm_gain", "gmlp_v_gain", "gmlp_w_s", "gmlp_b", "attn_q_gain", "attn_k_gain", "mem_norm_gain",
         "mem_q_gain", "mem_k_gain")
WEIGHTS = ("norm_gain", "w_in", "gmlp_v_gain", "gmlp_w_s", "gmlp_b", "attn_q_gain", "attn_k_gain",
           "mem_norm_gain", "w_mem_kv", "mem_q_gain", "mem_k_gain", "w_out")


def _adamw_small(w, m, v, g_all):
    k = len(SMALL)
    half = slice(0, HEAD_DIM), slice(HEAD_DIM, 2 * HEAD_DIM)

    def body(*refs):
        w_refs, m_refs, v_refs = refs[:k], refs[k:2 * k], refs[2 * k:3 * k]
        g_ref = refs[3 * k]
        outs = refs[3 * k + 1:7 * k + 1]
        loss_ref, gsum = refs[7 * k + 1:]

        part = SMALL_ROWS // 4
        for p in range(4):
            acc = g_ref[part * p:part * (p + 1), :]
            for dev in range(1, N_DEV):
                acc = acc + g_ref[dev * SMALL_ROWS + part * p:dev * SMALL_ROWS + part * (p + 1), :]
            gsum[part * p:part * (p + 1), :] = acc

        def update(name, at, g):
            i = SMALL.index(name)
            d, nm, nv = _adamw_math(w_refs[i][at], g, m_refs[i][at], v_refs[i][at])
            outs[i][at], outs[k + i][at], outs[2 * k + i][at], outs[3 * k + i][at] = g, d, nm, nv

        for i in range(8):
            at = (slice(0, 1), slice(128 * i, 128 * (i + 1)))
            update("norm_gain", at, gsum[ROW_NORM + i:ROW_NORM + i + 1, :])
            update("mem_norm_gain", at, gsum[ROW_MEM_NORM + i:ROW_MEM_NORM + i + 1, :])
        for h in range(4):
            row = (0, slice(h, h + 1), slice(None))
            update("gmlp_v_gain", row, gsum[ROW_V_GAIN + h // 2:ROW_V_GAIN + h // 2 + 1, half[h % 2]])
            update("gmlp_b", row, gsum[ROW_B + h:ROW_B + h + 1, :])
            update("gmlp_w_s", (0, h), gsum[ROW_W_S + CHUNK * h:ROW_W_S + CHUNK * (h + 1), :])
        whole = (slice(0, 1), slice(None))
        update("attn_q_gain", whole, gsum[ROW_ATTN_GAINS:ROW_ATTN_GAINS + 1, half[0]])
        update("attn_k_gain", whole, gsum[ROW_ATTN_GAINS:ROW_ATTN_GAINS + 1, half[1]])
        update("mem_q_gain", whole, gsum[ROW_MEM_GAINS:ROW_MEM_GAINS + 1, half[0]])
        update("mem_k_gain", whole, gsum[ROW_MEM_GAINS:ROW_MEM_GAINS + 1, half[1]])
        loss_ref[...] = jnp.sum(gsum[ROW_LOSS:ROW_LOSS + 8, :], keepdims=True) * (0.5 / D_MODEL)

    shapes = [jax.ShapeDtypeStruct(w[name].shape, F32) for name in SMALL]
    res = pl.pallas_call(
        body, name="adamw_small",
        out_shape=shapes * 4 + [jax.ShapeDtypeStruct((1, 1), F32)],
        scratch_shapes=[pltpu.VMEM((SMALL_ROWS, 128), F32)],
        compiler_params=_params(16),
    )(*[w[n] for n in SMALL], *[m[n] for n in SMALL], *[v[n] for n in SMALL], g_all)
    trees = [dict(zip(SMALL, res[j * k:(j + 1) * k])) for j in range(4)]
    return (*trees, res[4 * k])


def _grads(x, mem, tgt, w, shards):
    bd128, bd256 = _head_blockdiag(128), _head_blockdiag(256)
    gain = w["norm_gain"].reshape(1, D_MODEL)
    vg = w["gmlp_v_gain"].reshape(1, GMLP_WIDTH)
    w_s = w["gmlp_w_s"].reshape(4, CHUNK, CHUNK)
    b2 = jnp.repeat(w["gmlp_b"].reshape(4, CHUNK).T, HEAD_DIM, axis=1)
    qg2 = jnp.tile(w["attn_q_gain"].reshape(1, HEAD_DIM), (1, 2))
    kg2 = jnp.tile(w["attn_k_gain"].reshape(1, HEAD_DIM), (1, 2))
    mqg4 = jnp.tile(w["mem_q_gain"].reshape(1, HEAD_DIM), (1, 4))
    mkg4 = jnp.tile(w["mem_k_gain"].reshape(1, HEAD_DIM), (1, 4))
    mgain = w["mem_norm_gain"].reshape(1, D_MODEL)

    xpos = lax.axis_index("x").astype(jnp.int32).reshape(1)
    proj, h_bf, win_t, wkv_bf, wout_bf = _gather_proj(x, gain, shards, xpos)
    yg = _gmlp_fwd(proj, vg, w_s, b2, bd256)
    ya, att, lse = _attn_fwd(proj, qg2, kg2, bd128)
    hm_bf, kraw, mk, mv = _mem_kv(mem, mgain, wkv_bf, mkg4, bd256)
    ym, om = _mem_fwd(proj, mk, mv, mqg4, bd256)
    dout, dycat, dwout, sq = _out_loss(yg, ya, ym, x, tgt, wout_bf)

    du, dgv, dgg, dws, db2, dvg = _gmlp_bwd(proj, dycat, vg, w_s, b2, bd256)
    dq, dk, dv, dag, dqg, dkg = _attn_bwd(proj, dycat, att, lse, qg2, kg2, bd128)
    dmq, dmg, dmk, dmv, dmqg = _mem_bwd(proj, dycat, om, mk, mv, mqg4, bd256)
    dwkv, dmgain, dmkg = _mem_kv_bwd(dmk, dmv, kraw, mem, mgain, mkg4, wkv_bf, hm_bf, bd256)
    pieces = [du, dgv, dgg, dq, dk, dv, dag, dmq, dmg]
    dwin, g_wkv, g_wout = _in_bwd_dw(pieces, h_bf, [dwkv, dwout])
    grad_x, dgain, g_win = _in_bwd_dx(pieces, x, dout, gain, win_t, dwin)
    return grad_x, g_win, g_wkv, g_wout, (dgain, dmgain, dvg, db2, dqg, dkg, dmqg, dmkg, dws, sq)


def kernel(x, mem, norm_gain, w_in, gmlp_v_gain, gmlp_w_s, gmlp_b, attn_q_gain, attn_k_gain, mem_norm_gain, w_mem_kv, mem_q_gain, mem_k_gain, w_out, loss_target, m_norm_gain, m_w_in, m_gmlp_v_gain, m_gmlp_w_s, m_gmlp_b, m_attn_q_gain, m_attn_k_gain, m_mem_norm_gain, m_w_mem_kv, m_mem_q_gain, m_mem_k_gain, m_w_out, v_norm_gain, v_w_in, v_gmlp_v_gain, v_gmlp_w_s, v_gmlp_b, v_attn_q_gain, v_attn_k_gain, v_mem_norm_gain, v_w_mem_kv, v_mem_q_gain, v_mem_k_gain, v_w_out):
    w = dict(norm_gain=norm_gain, w_in=w_in, gmlp_v_gain=gmlp_v_gain, gmlp_w_s=gmlp_w_s, gmlp_b=gmlp_b,
             attn_q_gain=attn_q_gain, attn_k_gain=attn_k_gain, mem_norm_gain=mem_norm_gain, w_mem_kv=w_mem_kv,
             mem_q_gain=mem_q_gain, mem_k_gain=mem_k_gain, w_out=w_out)
    m = dict(norm_gain=m_norm_gain, w_in=m_w_in, gmlp_v_gain=m_gmlp_v_gain, gmlp_w_s=m_gmlp_w_s, gmlp_b=m_gmlp_b,
             attn_q_gain=m_attn_q_gain, attn_k_gain=m_attn_k_gain, mem_norm_gain=m_mem_norm_gain,
             w_mem_kv=m_w_mem_kv, mem_q_gain=m_mem_q_gain, mem_k_gain=m_mem_k_gain, w_out=m_w_out)
    v = dict(norm_gain=v_norm_gain, w_in=v_w_in, gmlp_v_gain=v_gmlp_v_gain, gmlp_w_s=v_gmlp_w_s, gmlp_b=v_gmlp_b,
             attn_q_gain=v_attn_q_gain, attn_k_gain=v_attn_k_gain, mem_norm_gain=v_mem_norm_gain,
             w_mem_kv=v_w_mem_kv, mem_q_gain=v_mem_q_gain, mem_k_gain=v_mem_k_gain, w_out=v_w_out)
    transposed = lambda t: jnp.transpose(t[0])

    grad_x, g_win, g_wkv, g_wout, small = _grads(
        x[0], mem[0], loss_target[0], w, [transposed(w_in), w_mem_kv[0], w_out[0]])
    small_all = _gather_small(*small)

    out_g, out_d, out_m, out_v, loss = _adamw_small(w, m, v, small_all)
    d_, m_, v_ = _adamw(transposed(w_in), g_win, transposed(m_w_in), transposed(v_w_in), "adamw_w_in")
    for tree, t in ((out_g, g_win), (out_d, d_), (out_m, m_), (out_v, v_)):
        tree["w_in"] = jnp.transpose(t)[None]
    for name, g in (("w_mem_kv", g_wkv), ("w_out", g_wout)):
        d_, m_, v_ = _adamw(w[name][0], g, m[name][0], v[name][0], "adamw_" + name)
        out_g[name], out_d[name], out_m[name], out_v[name] = g[None], d_[None], m_[None], v_[None]

    return (loss.reshape(()), grad_x[None], *[out_g[k] for k in WEIGHTS], *[out_d[k] for k in WEIGHTS],
            *[out_m[k] for k in WEIGHTS], *[out_v[k] for k in WEIGHTS])
```

```python
import functools
import math

import jax
import jax.numpy as jnp
from jax import lax
from jax.experimental import pallas as pl
from jax.experimental.pallas import tpu as pltpu

F32 = jnp.float32
BF16 = jnp.bfloat16

N_DEV = 8
D_MODEL = 1024
HEAD_DIM = 64
GMLP_WIDTH = 256
ATTN_WIDTH = 512
MEM_WIDTH = 256
MEM_LEN = 256
CHUNK = 128
BLOCK = 128
DILATIONS = (1, 4, 16)
CONFIG_ORDER = tuple(reversed(DILATIONS))
EPS = 1e-6
SCALE = 1.0 / math.sqrt(HEAD_DIM)
NEG = -1e30

ADAM_LR = 0.001
ADAM_B1 = 0.9
ADAM_B2 = 0.999
ADAM_EPS = 1e-08
ADAM_WD = 0.01
ADAM_STEP = 10

MIB = 1024 * 1024
MESH = pl.DeviceIdType.MESH

COL_AQ, COL_AK, COL_AV, COL_AG = 6, 10, 14, 18


def _params(vmem_mib, semantics=None):
    kw = dict(vmem_limit_bytes=vmem_mib * MIB)
    if semantics is not None:
        kw["dimension_semantics"] = semantics
    return pltpu.CompilerParams(**kw)


def _hbm(*arrs):
    return [pltpu.with_memory_space_constraint(a, pltpu.HBM) for a in arrs]


def _split_dot(x, sel_bf):
    hi = x.astype(BF16)
    lo = (x - hi.astype(F32)).astype(BF16)
    return jnp.dot(hi, sel_bf, preferred_element_type=F32) + jnp.dot(lo, sel_bf, preferred_element_type=F32)


def _nt(a, b):
    return lax.dot_general(a, b, (((1,), (1,)), ((), ())), preferred_element_type=F32)


def _tn(a, b):
    return lax.dot_general(a, b, (((0,), (0,)), ((), ())), preferred_element_type=F32)


def _silu_parts(g):
    sg = jax.nn.sigmoid(g)
    return g * sg, sg * (1.0 + g * (1.0 - sg))


def _head_index(shape):
    return lax.shift_right_logical(lax.broadcasted_iota(jnp.int32, shape, 1), HEAD_DIM.bit_length() - 1)


def _head_blockdiag(width):
    i = jnp.arange(width) // HEAD_DIM
    return (i[:, None] == i[None, :]).astype(BF16)


def _gmlp_masked_weights(ws_ref, transpose):
    t = lax.broadcasted_iota(jnp.int32, (CHUNK, CHUNK), 0)
    s = lax.broadcasted_iota(jnp.int32, (CHUNK, CHUNK), 1)
    parts = []
    for h in range(4):
        wm = jnp.where(s <= t, ws_ref[h], 0.0)
        parts.append(wm.T if transpose else wm)
    return jnp.concatenate(parts, axis=1).astype(BF16)


def _head_stack(v, head):
    return jnp.concatenate([jnp.where(head == h, v, 0.0) for h in range(4)], axis=0).astype(BF16)


def _gmlp_fwd(proj, vg, w_s, b2, bd):
    S = proj.shape[0]
    tm = 512

    def body(u_ref, v_ref, g_ref, vg_ref, ws_ref, b2_ref, bd_ref, y_ref):
        v = v_ref[...]
        ms = _split_dot(v * v, bd_ref[...]) * (1.0 / HEAD_DIM)
        vn = (v * lax.rsqrt(ms + EPS)) * vg_ref[...]
        wcat = _gmlp_masked_weights(ws_ref, False)
        head = _head_index((CHUNK, GMLP_WIDTH))
        for c in range(tm // CHUNK):
            rows = slice(c * CHUNK, (c + 1) * CHUNK)
            sp = jnp.dot(wcat, _head_stack(vn[rows], head), preferred_element_type=F32) + b2_ref[...]
            silu, _ = _silu_parts(g_ref[rows, :])
            y_ref[rows, :] = ((u_ref[rows, :] * sp) * silu).astype(BF16)

    col = lambda j: pl.BlockSpec((tm, GMLP_WIDTH), lambda i, j=j: (i, j))
    const = lambda shape: pl.BlockSpec(shape, lambda i: (0,) * len(shape))
    return pl.pallas_call(
        body, name="gmlp_fwd", grid=(S // tm,),
        in_specs=[col(0), col(1), col(2), const((1, GMLP_WIDTH)), const((4, CHUNK, CHUNK)),
                  const((CHUNK, GMLP_WIDTH)), const((GMLP_WIDTH, GMLP_WIDTH))],
        out_specs=pl.BlockSpec((tm, GMLP_WIDTH), lambda i: (i, 0)),
        out_shape=pltpu.HBM((S, GMLP_WIDTH), BF16),
        compiler_params=_params(24, ("arbitrary",)),
    )(*_hbm(proj, proj, proj, vg, w_s, b2, bd))


def _gmlp_bwd(proj, dycat, vg, w_s, b2, bd):
    S = proj.shape[0]
    tm = 512
    nsteps = S // tm

    def body(u_ref, v_ref, g_ref, dy_ref, vg_ref, ws_ref, b2_ref, bd_ref,
             du_ref, dv_ref, dg_ref, dws_ref, db2_ref, dvg_ref):
        i = pl.program_id(0)

        @pl.when(i == 0)
        def _():
            dws_ref[...] = jnp.zeros_like(dws_ref)
            db2_ref[...] = jnp.zeros_like(db2_ref)
            dvg_ref[...] = jnp.zeros_like(dvg_ref)

        bdv = bd_ref[...]
        v = v_ref[...]
        ms = _split_dot(v * v, bdv) * (1.0 / HEAD_DIM)
        rv = lax.rsqrt(ms + EPS)
        xhat = v * rv
        vgv = vg_ref[...]
        vn = xhat * vgv
        wcat = _gmlp_masked_weights(ws_ref, False)
        wcat_t = _gmlp_masked_weights(ws_ref, True)
        head = _head_index((CHUNK, GMLP_WIDTH))
        dvg = jnp.zeros((1, GMLP_WIDTH), F32)
        for c in range(tm // CHUNK):
            rows = slice(c * CHUNK, (c + 1) * CHUNK)
            vn_c = vn[rows]
            spb = jnp.dot(wcat, _head_stack(vn_c, head), preferred_element_type=F32) + b2_ref[...]
            silu, dsilu = _silu_parts(g_ref[rows, :])
            dy = dy_ref[rows, :]
            u = u_ref[rows, :]
            du_ref[rows, :] = (dy * spb * silu).astype(BF16)
            dg_ref[rows, :] = (dy * u * spb * dsilu).astype(BF16)
            dsp = dy * u * silu
            db2_ref[...] += dsp
            dstack = _head_stack(dsp, head)
            dvn = jnp.dot(wcat_t, dstack, preferred_element_type=F32)
            dws_ref[...] += _nt(dstack, vn_c.astype(BF16))
            xh = xhat[rows]
            a = dvn * vgv
            mean_ax = _split_dot(a * xh, bdv) * (1.0 / HEAD_DIM)
            dv_ref[rows, :] = (rv[rows] * (a - xh * mean_ax)).astype(BF16)
            dvg = dvg + jnp.sum(dvn * xh, axis=0, keepdims=True)
        dvg_ref[...] += dvg

        @pl.when(i == nsteps - 1)
        def _():
            t = lax.broadcasted_iota(jnp.int32, (4 * CHUNK, CHUNK), 0) % CHUNK
            s = lax.broadcasted_iota(jnp.int32, (4 * CHUNK, CHUNK), 1)
            dws_ref[...] = jnp.where(s <= t, dws_ref[...], 0.0)
            db2_ref[...] = _split_dot(db2_ref[...], bdv)

    col = lambda j: pl.BlockSpec((tm, GMLP_WIDTH), lambda i, j=j: (i, j))
    const = lambda shape: pl.BlockSpec(shape, lambda i: (0,) * len(shape))
    tile = pl.BlockSpec((tm, GMLP_WIDTH), lambda i: (i, 0))
    piece = pltpu.HBM((S, GMLP_WIDTH), BF16)
    return pl.pallas_call(
        body, name="gmlp_bwd", grid=(nsteps,),
        in_specs=[col(0), col(1), col(2), col(0), const((1, GMLP_WIDTH)), const((4, CHUNK, CHUNK)),
                  const((CHUNK, GMLP_WIDTH)), const((GMLP_WIDTH, GMLP_WIDTH))],
        out_specs=[tile, tile, tile, const((4 * CHUNK, CHUNK)), const((CHUNK, GMLP_WIDTH)), const((1, GMLP_WIDTH))],
        out_shape=[piece, piece, piece, pltpu.HBM((4 * CHUNK, CHUNK), F32),
                   pltpu.HBM((CHUNK, GMLP_WIDTH), F32), pltpu.HBM((1, GMLP_WIDTH), F32)],
        compiler_params=_params(32, ("arbitrary",)),
    )(*_hbm(proj, proj, proj, dycat, vg, w_s, b2, bd))


def _band_mask():
    qi = lax.broadcasted_iota(jnp.int32, (2 * BLOCK, 2 * BLOCK), 0) % BLOCK
    ki = lax.broadcasted_iota(jnp.int32, (2 * BLOCK, 2 * BLOCK), 1)
    return ((ki < BLOCK) & (ki >= qi)) | ((ki >= BLOCK) & ((ki - BLOCK) <= qi))


def _first_block_bias(blk, blocks_per_class):
    kcol = lax.broadcasted_iota(jnp.int32, (1, 2 * BLOCK), 1)
    kill = jnp.where((blk & (blocks_per_class - 1)) == 0, NEG, 0.0)
    return jnp.where(kcol < BLOCK, kill, 0.0)


def _two_heads(q, lo):
    zero = jnp.zeros_like(q)
    return jnp.concatenate([jnp.where(lo, q, zero), jnp.where(lo, zero, q)], axis=0)


def _block_tokens(blk, d, S):
    if d == 1:
        return pl.ds(pl.multiple_of(blk * BLOCK, BLOCK), BLOCK)
    blocks_per_class = S // d // BLOCK
    r = lax.shift_right_logical(blk, blocks_per_class.bit_length() - 1)
    n = blk & (blocks_per_class - 1)
    return pl.ds(r + n * (BLOCK * d), BLOCK, stride=d)


def _padded_block(blk):
    return pl.ds(pl.multiple_of((blk + 1) * BLOCK, BLOCK), BLOCK)


def _for_blocks(n_blocks, unroll, fn):
    def group(g, carry):
        for u in range(unroll):
            fn(g * unroll + u)
        return carry
    lax.fori_loop(0, n_blocks // unroll, group, 0)


def _attn_fwd(proj, qg2, kg2, bd):
    S = proj.shape[0]
    npairs = ATTN_WIDTH // 128
    tn = 512

    def body(q_ref, k_ref, v_ref, g_ref, qg_ref, kg_ref, bd_ref, y_ref, att_ref, lse_ref, qn, kn, kc, vc):
        bdv = bd_ref[...]
        lo = lax.broadcasted_iota(jnp.int32, (BLOCK, 128), 1) < HEAD_DIM
        band_mask = _band_mask()
        kc[pl.ds(0, BLOCK), :] = jnp.zeros((BLOCK, 128), BF16)
        vc[pl.ds(0, BLOCK), :] = jnp.zeros((BLOCK, 128), BF16)

        def norm_step(i, carry):
            rows = pl.ds(pl.multiple_of(i * tn, tn), tn)
            qv = q_ref[rows, :]
            kv = k_ref[rows, :]
            qn[rows, :] = (qv * lax.rsqrt(_split_dot(qv * qv, bdv) * (1.0 / HEAD_DIM) + EPS)) * (qg_ref[...] * SCALE)
            kn[rows, :] = (kv * lax.rsqrt(_split_dot(kv * kv, bdv) * (1.0 / HEAD_DIM) + EPS)) * kg_ref[...]
            return carry
        lax.fori_loop(0, S // tn, norm_step, 0)

        def fill(blk, d):
            tokens = _block_tokens(blk, d, S)
            kc[_padded_block(blk), :] = kn[tokens, :].astype(BF16)
            vc[_padded_block(blk), :] = v_ref[tokens, :].astype(BF16)

        ones_bf = jnp.ones((2 * BLOCK, 128), BF16)

        def block(blk, d):
            tokens = _block_tokens(blk, d, S)
            keys = pl.ds(pl.multiple_of(blk * BLOCK, BLOCK), 2 * BLOCK)
            q2 = _two_heads(qn[tokens, :].astype(BF16), lo)
            s = jnp.where(band_mask, _nt(q2, kc[keys, :]), NEG) + _first_block_bias(blk, S // d // BLOCK)
            m = jnp.max(s, axis=-1, keepdims=True)
            e = jnp.exp((s - m).astype(BF16))
            ol = jnp.dot(e, jnp.concatenate([vc[keys, :], ones_bf], axis=1), preferred_element_type=F32)
            l = ol[:, 128:]
            o2 = ol[:, :128] * (1.0 / l)
            lse2 = m + jnp.log(l)
            o = jnp.where(lo, o2[:BLOCK], o2[BLOCK:])
            lse = jnp.where(lo, lse2[:BLOCK], lse2[BLOCK:])
            if d != CONFIG_ORDER[0]:
                la = lse_ref[tokens, :]
                mx = jnp.maximum(la, lse)
                wa, wb = jnp.exp(la - mx), jnp.exp(lse - mx)
                t = wa + wb
                o = (wa * att_ref[tokens, :] + wb * o) / t
                lse = mx + jnp.log(t)
            att_ref[tokens, :] = o
            lse_ref[tokens, :] = lse
            if d == CONFIG_ORDER[-1]:
                silu, _ = _silu_parts(g_ref[tokens, :])
                y_ref[tokens, :] = (o * silu).astype(BF16)

        for d in CONFIG_ORDER:
            _for_blocks(S // BLOCK, 4, functools.partial(fill, d=d))
            _for_blocks(S // BLOCK, 16, functools.partial(block, d=d))

    col = lambda j0: pl.BlockSpec((S, 128), lambda p, j0=j0: (0, j0 + p))
    const = lambda shape: pl.BlockSpec(shape, lambda p: (0,) * len(shape))
    out = pl.BlockSpec((S, 128), lambda p: (0, p))
    return pl.pallas_call(
        body, name="attn_fwd", grid=(npairs,),
        in_specs=[col(COL_AQ), col(COL_AK), col(COL_AV), col(COL_AG), const((1, 128)), const((1, 128)),
                  const((128, 128))],
        out_specs=[out, out, out],
        out_shape=[pltpu.HBM((S, ATTN_WIDTH), BF16), pltpu.HBM((S, ATTN_WIDTH), F32),
                   pltpu.HBM((S, ATTN_WIDTH), F32)],
        scratch_shapes=[pltpu.VMEM((S, 128), F32), pltpu.VMEM((S, 128), F32),
                        pltpu.VMEM((S + BLOCK, 128), BF16), pltpu.VMEM((S + BLOCK, 128), BF16)],
        compiler_params=_params(48, ("arbitrary",)),
    )(*_hbm(proj, proj, proj, proj, qg2, kg2, bd))


def _attn_bwd(proj, dycat, att, lse, qg2, kg2, bd):
    S = proj.shape[0]
    npairs = ATTN_WIDTH // 128
    tn = 512

    def body(q_ref, k_ref, v_ref, g_ref, dy_ref, att_ref, lse_ref, qg_ref, kg_ref, bd_ref,
             dq_ref, dk_ref, dv_ref, dg_ref, dqg_ref, dkg_ref,
             qn, kn, rq_s, rk_s, kc, vc, do_s, dd_s, dqa, dka, dva):
        bdv = bd_ref[...]
        lo = lax.broadcasted_iota(jnp.int32, (BLOCK, 128), 1) < HEAD_DIM
        kc[pl.ds(0, BLOCK), :] = jnp.zeros((BLOCK, 128), BF16)
        vc[pl.ds(0, BLOCK), :] = jnp.zeros((BLOCK, 128), BF16)

        def prepare(i, carry):
            rows = pl.ds(pl.multiple_of(i * tn, tn), tn)
            qv = q_ref[rows, :]
            kv = k_ref[rows, :]
            rq = lax.rsqrt(_split_dot(qv * qv, bdv) * (1.0 / HEAD_DIM) + EPS)
            rk = lax.rsqrt(_split_dot(kv * kv, bdv) * (1.0 / HEAD_DIM) + EPS)
            rq_s[rows, :] = rq
            rk_s[rows, :] = rk
            qn[rows, :] = (qv * rq) * (qg_ref[...] * SCALE)
            kn[rows, :] = (kv * rk) * kg_ref[...]
            silu, dsilu = _silu_parts(g_ref[rows, :])
            dy = dy_ref[rows, :]
            at = att_ref[rows, :]
            do = dy * silu
            do_s[rows, :] = do
            dd_s[rows, :] = _split_dot(do * at, bdv)
            dg_ref[rows, :] = (dy * at * dsilu).astype(BF16)
            dka[rows, :] = jnp.zeros((tn, 128), F32)
            dva[rows, :] = jnp.zeros((tn, 128), F32)
            return carry
        lax.fori_loop(0, S // tn, prepare, 0)

        kt = lax.broadcasted_iota(jnp.int32, (2 * BLOCK, 2 * BLOCK), 0)
        qt = lax.broadcasted_iota(jnp.int32, (2 * BLOCK, 2 * BLOCK), 1) % BLOCK
        band_mask_t = ((kt < BLOCK) & (kt >= qt)) | ((kt >= BLOCK) & ((kt - BLOCK) <= qt))

        def per_query_row(t):
            tt = t.T
            return jnp.concatenate([tt[0:1, :], tt[HEAD_DIM:HEAD_DIM + 1, :]], axis=1)

        def fill(blk, d):
            tokens = _block_tokens(blk, d, S)
            kc[_padded_block(blk), :] = kn[tokens, :].astype(BF16)
            vc[_padded_block(blk), :] = v_ref[tokens, :].astype(BF16)

        def block(blk, d):
            tokens = _block_tokens(blk, d, S)
            keys = pl.ds(pl.multiple_of(blk * BLOCK, BLOCK), 2 * BLOCK)
            first = (blk & (S // d // BLOCK - 1)) == 0
            q2 = _two_heads(qn[tokens, :].astype(BF16), lo)
            do2 = _two_heads(do_s[tokens, :].astype(BF16), lo)
            lse_row = per_query_row(lse_ref[tokens, :])
            dd_row = per_query_row(dd_s[tokens, :])
            kb = kc[keys, :]
            vb = vc[keys, :]
            st = jnp.where(band_mask_t, _nt(kb, q2), NEG)
            st = jnp.concatenate([st[:BLOCK] + jnp.where(first, NEG, 0.0), st[BLOCK:]], axis=0)
            pt = jnp.exp(st - lse_row)
            dst = pt * (_nt(vb, do2) - dd_row)
            ptb = pt.astype(BF16)
            dstb = dst.astype(BF16)
            dv_band = jnp.dot(ptb, do2, preferred_element_type=F32)
            dk_band = jnp.dot(dstb, q2, preferred_element_type=F32)
            before = _block_tokens(jnp.where(first, blk, blk - 1), d, S)
            dka[before, :] = dka[before, :] + dk_band[:BLOCK]
            dva[before, :] = dva[before, :] + dv_band[:BLOCK]
            dka[tokens, :] = dka[tokens, :] + dk_band[BLOCK:]
            dva[tokens, :] = dva[tokens, :] + dv_band[BLOCK:]
            dq2 = _tn(dstb, kb)
            dq = jnp.where(lo, dq2[:BLOCK], dq2[BLOCK:])
            dqa[tokens, :] = dq if d == CONFIG_ORDER[0] else dqa[tokens, :] + dq

        for d in CONFIG_ORDER:
            _for_blocks(S // BLOCK, 4, functools.partial(fill, d=d))
            _for_blocks(S // BLOCK, 8, functools.partial(block, d=d))

        def out_step(i, carry):
            dqg, dkg = carry
            rows = pl.ds(pl.multiple_of(i * tn, tn), tn)
            rq = rq_s[rows, :]
            rk = rk_s[rows, :]
            qh = q_ref[rows, :] * rq
            kh = k_ref[rows, :] * rk
            dqs = dqa[rows, :] * SCALE
            dkn = dka[rows, :]
            aq = dqs * qg_ref[...]
            ak = dkn * kg_ref[...]
            dq_ref[rows, :] = (rq * (aq - qh * (_split_dot(aq * qh, bdv) * (1.0 / HEAD_DIM)))).astype(BF16)
            dk_ref[rows, :] = (rk * (ak - kh * (_split_dot(ak * kh, bdv) * (1.0 / HEAD_DIM)))).astype(BF16)
            dv_ref[rows, :] = dva[rows, :].astype(BF16)
            dqg = dqg + jnp.sum(dqs * qh, axis=0, keepdims=True)
            dkg = dkg + jnp.sum(dkn * kh, axis=0, keepdims=True)
            return dqg, dkg
        zero = jnp.zeros((1, 128), F32)
        dqg, dkg = lax.fori_loop(0, S // tn, out_step, (zero, zero))
        dqg_ref[0] = dqg
        dkg_ref[0] = dkg

    col = lambda j0: pl.BlockSpec((S, 128), lambda p, j0=j0: (0, j0 + p))
    col1 = lambda j0: pl.BlockSpec((S, 128), lambda p, j0=j0: (0, j0 + p), pipeline_mode=pl.Buffered(1))
    const = lambda shape: pl.BlockSpec(shape, lambda p: (0,) * len(shape))
    out = pl.BlockSpec((S, 128), lambda p: (0, p))
    gain_out = pl.BlockSpec((1, 1, 128), lambda p: (p, 0, 0))
    piece = pltpu.HBM((S, ATTN_WIDTH), BF16)
    gains = pltpu.HBM((npairs, 1, 128), F32)
    f32buf = pltpu.VMEM((S, 128), F32)
    bf16pad = pltpu.VMEM((S + BLOCK, 128), BF16)
    return pl.pallas_call(
        body, name="attn_bwd", grid=(npairs,),
        in_specs=[col(COL_AQ), col(COL_AK), col(COL_AV), col1(COL_AG), col1(GMLP_WIDTH // 128), col1(0), col(0),
                  const((1, 128)), const((1, 128)), const((128, 128))],
        out_specs=[out, out, out, out, gain_out, gain_out],
        out_shape=[piece, piece, piece, piece, gains, gains],
        scratch_shapes=[f32buf, f32buf, f32buf, f32buf, bf16pad, bf16pad, f32buf, f32buf, f32buf, f32buf, f32buf],
        compiler_params=_params(60, ("arbitrary",)),
    )(*_hbm(proj, proj, proj, proj, dycat, att, lse, qg2, kg2, bd))


def _mem_kv(mem, gain, wkv_bf, kg4, bd):
    def body(mem_ref, g_ref, w_ref, kg_ref, bd_ref, hm_ref, kraw_ref, mk_ref, mv_ref):
        mv_ = mem_ref[...]
        r = lax.rsqrt(jnp.mean(mv_ * mv_, axis=-1, keepdims=True) + EPS)
        hm = ((mv_ * r) * g_ref[...]).astype(BF16)
        hm_ref[...] = hm
        kv = jnp.dot(hm, w_ref[...], preferred_element_type=F32)
        kraw = kv[:, :MEM_WIDTH]
        kraw_ref[...] = kraw
        ms = _split_dot(kraw * kraw, bd_ref[...]) * (1.0 / HEAD_DIM)
        mk_ref[...] = (kraw * lax.rsqrt(ms + EPS)) * kg_ref[...]
        mv_ref[...] = kv[:, MEM_WIDTH:]

    sq = jax.ShapeDtypeStruct((MEM_LEN, MEM_WIDTH), F32)
    return pl.pallas_call(
        body, name="mem_kv",
        out_shape=[jax.ShapeDtypeStruct((MEM_LEN, D_MODEL), BF16), sq, sq, sq],
        compiler_params=_params(16),
    )(mem, gain, wkv_bf, kg4, bd)


def _mem_fwd(proj, mk, mv, qg4, bd):
    S = proj.shape[0]
    tm = 512

    def body(q_ref, g_ref, mk_ref, mv_ref, qg_ref, bd_ref, y_ref, om_ref):
        qv = q_ref[...]
        ms = _split_dot(qv * qv, bd_ref[...]) * (1.0 / HEAD_DIM)
        qs = (qv * lax.rsqrt(ms + EPS)) * (qg_ref[...] * SCALE)
        mkb = mk_ref[...].astype(BF16)
        mvb = mv_ref[...].astype(BF16)
        head = _head_index((tm, MEM_WIDTH))
        o = jnp.zeros((tm, MEM_WIDTH), F32)
        for h in range(4):
            s = _nt(jnp.where(head == h, qs, 0.0).astype(BF16), mkb)
            e = jnp.exp(s - jnp.max(s, axis=-1, keepdims=True))
            p = e * (1.0 / jnp.sum(e, axis=-1, keepdims=True))
            o = jnp.where(head == h, jnp.dot(p.astype(BF16), mvb, preferred_element_type=F32), o)
        om_ref[...] = o
        silu, _ = _silu_parts(g_ref[...])
        y_ref[...] = (o * silu).astype(BF16)

    col = lambda j: pl.BlockSpec((tm, MEM_WIDTH), lambda i, j=j: (i, j))
    const = lambda shape: pl.BlockSpec(shape, lambda i: (0,) * len(shape))
    tile = pl.BlockSpec((tm, MEM_WIDTH), lambda i: (i, 0))
    return pl.pallas_call(
        body, name="mem_fwd", grid=(S // tm,),
        in_specs=[col(11), col(12), const((MEM_LEN, MEM_WIDTH)), const((MEM_LEN, MEM_WIDTH)), const((1, MEM_WIDTH)),
                  const((MEM_WIDTH, MEM_WIDTH))],
        out_specs=[tile, tile],
        out_shape=[pltpu.HBM((S, MEM_WIDTH), BF16), pltpu.HBM((S, MEM_WIDTH), F32)],
        compiler_params=_params(24, ("arbitrary",)),
    )(*_hbm(proj, proj, mk, mv, qg4, bd))


def _mem_bwd(proj, dycat, om, mk, mv, qg4, bd):
    S = proj.shape[0]
    tm = 512

    def body(q_ref, g_ref, dy_ref, om_ref, mk_ref, mv_ref, qg_ref, bd_ref,
             dq_ref, dg_ref, dmk_ref, dmv_ref, dqg_ref):
        i = pl.program_id(0)

        @pl.when(i == 0)
        def _():
            dmk_ref[...] = jnp.zeros_like(dmk_ref)
            dmv_ref[...] = jnp.zeros_like(dmv_ref)
            dqg_ref[...] = jnp.zeros_like(dqg_ref)

        bdv = bd_ref[...]
        qv = q_ref[...]
        rq = lax.rsqrt(_split_dot(qv * qv, bdv) * (1.0 / HEAD_DIM) + EPS)
        qh = qv * rq
        qs = qh * (qg_ref[...] * SCALE)
        silu, dsilu = _silu_parts(g_ref[...])
        dy = dy_ref[...]
        o = om_ref[...]
        do = dy * silu
        dg_ref[...] = (dy * o * dsilu).astype(BF16)
        dd = _split_dot(do * o, bdv)
        mkb = mk_ref[...].astype(BF16)
        mvb = mv_ref[...].astype(BF16)
        head = _head_index((tm, MEM_WIDTH))
        dqs = jnp.zeros((tm, MEM_WIDTH), F32)
        for h in range(4):
            qhd = jnp.where(head == h, qs, 0.0).astype(BF16)
            doh = jnp.where(head == h, do, 0.0).astype(BF16)
            s = _nt(qhd, mkb)
            e = jnp.exp(s - jnp.max(s, axis=-1, keepdims=True))
            p = e * (1.0 / jnp.sum(e, axis=-1, keepdims=True))
            ds = p * (_nt(doh, mvb) - dd[:, h * HEAD_DIM:h * HEAD_DIM + 1])
            dsb = ds.astype(BF16)
            dmv_ref[...] += _tn(p.astype(BF16), doh)
            dmk_ref[...] += _tn(dsb, qhd)
            dqs = jnp.where(head == h, jnp.dot(dsb, mkb, preferred_element_type=F32), dqs)
        dqs = dqs * SCALE
        a = dqs * qg_ref[...]
        dq_ref[...] = (rq * (a - qh * (_split_dot(a * qh, bdv) * (1.0 / HEAD_DIM)))).astype(BF16)
        dqg_ref[...] += jnp.sum(dqs * qh, axis=0, keepdims=True)

    col = lambda j: pl.BlockSpec((tm, MEM_WIDTH), lambda i, j=j: (i, j))
    const = lambda shape: pl.BlockSpec(shape, lambda i: (0,) * len(shape))
    tile = pl.BlockSpec((tm, MEM_WIDTH), lambda i: (i, 0))
    piece = pltpu.HBM((S, MEM_WIDTH), BF16)
    sq = pltpu.HBM((MEM_LEN, MEM_WIDTH), F32)
    return pl.pallas_call(
        body, name="mem_bwd", grid=(S // tm,),
        in_specs=[col(11), col(12), col(3), tile, const((MEM_LEN, MEM_WIDTH)), const((MEM_LEN, MEM_WIDTH)),
                  const((1, MEM_WIDTH)), const((MEM_WIDTH, MEM_WIDTH))],
        out_specs=[tile, tile, const((MEM_LEN, MEM_WIDTH)), const((MEM_LEN, MEM_WIDTH)), const((1, MEM_WIDTH))],
        out_shape=[piece, piece, sq, sq, pltpu.HBM((1, MEM_WIDTH), F32)],
        compiler_params=_params(32, ("arbitrary",)),
    )(*_hbm(proj, proj, dycat, om, mk, mv, qg4, bd))


def _mem_kv_bwd(dmk, dmv, kraw, mem, gain, kg4, wkv_bf, hm_bf, bd):
    def body(dmk_ref, dmv_ref, kraw_ref, mem_ref, g_ref, kg_ref, w_ref, hm_ref, bd_ref, dw_ref, dg_ref, dkg_ref):
        bdv = bd_ref[...]
        kraw = kraw_ref[...]
        rk = lax.rsqrt(_split_dot(kraw * kraw, bdv) * (1.0 / HEAD_DIM) + EPS)
        kh = kraw * rk
        dmkv = dmk_ref[...]
        a = dmkv * kg_ref[...]
        dkraw = rk * (a - kh * (_split_dot(a * kh, bdv) * (1.0 / HEAD_DIM)))
        dkg_ref[...] = jnp.sum(dmkv * kh, axis=0, keepdims=True)
        dkv = jnp.concatenate([dkraw, dmv_ref[...]], axis=1).astype(BF16)
        dw = _tn(hm_ref[...], dkv).astype(BF16)
        rows_blk = D_MODEL // N_DEV
        for j in range(N_DEV):
            dw_ref[j] = dw[rows_blk * j:rows_blk * (j + 1)]
        dhm = _nt(dkv, w_ref[...])
        mv_ = mem_ref[...]
        r = lax.rsqrt(jnp.mean(mv_ * mv_, axis=-1, keepdims=True) + EPS)
        dg_ref[...] = jnp.sum(dhm * (mv_ * r), axis=0, keepdims=True)

    return pl.pallas_call(
        body, name="mem_kv_bwd",
        out_shape=[jax.ShapeDtypeStruct((N_DEV, D_MODEL // N_DEV, 2 * MEM_WIDTH), BF16),
                   jax.ShapeDtypeStruct((1, D_MODEL), F32), jax.ShapeDtypeStruct((1, MEM_WIDTH), F32)],
        compiler_params=_params(24),
    )(dmk, dmv, kraw, mem, gain, kg4, wkv_bf, hm_bf, bd)


def _out_loss(yg, ya, ym, x, tgt, wout_bf):
    S, D = x.shape
    tm = 512
    nsteps = S // tm
    rows_blk = D // N_DEV

    def body(yg_ref, ya_ref, ym_ref, x_ref, t_ref, w_ref, dout_ref, dycat_ref, dw_ref, loss_ref, acc_ref):
        i = pl.program_id(0)

        @pl.when(i == 0)
        def _():
            acc_ref[...] = jnp.zeros_like(acc_ref)
            loss_ref[...] = jnp.zeros_like(loss_ref)

        ycat = jnp.concatenate([yg_ref[...], ya_ref[...], ym_ref[...]], axis=1)
        w = w_ref[...]
        diff = (x_ref[...] + jnp.dot(ycat, w, preferred_element_type=F32)) - t_ref[...]
        loss_ref[...] += jnp.sum(diff * diff, axis=0, keepdims=True)
        dout = diff * (1.0 / D)
        dout_ref[...] = dout
        db = dout.astype(BF16)
        dycat_ref[...] = _nt(db, w)
        acc_ref[...] += _tn(ycat, db)

        @pl.when(i == nsteps - 1)
        def _():
            for j in range(N_DEV):
                dw_ref[j] = acc_ref[rows_blk * j:rows_blk * (j + 1), :].astype(BF16)

    tile = lambda w: pl.BlockSpec((tm, w), lambda i: (i, 0))
    const = lambda shape: pl.BlockSpec(shape, lambda i: (0,) * len(shape))
    return pl.pallas_call(
        body, name="out_loss", grid=(nsteps,),
        in_specs=[tile(GMLP_WIDTH), tile(ATTN_WIDTH), tile(MEM_WIDTH), tile(D), tile(D), const((D, D))],
        out_specs=[tile(D), tile(D), const((N_DEV, rows_blk, D)), const((1, D))],
        out_shape=[pltpu.HBM((S, D), F32), pltpu.HBM((S, D), F32),
                   pltpu.HBM((N_DEV, rows_blk, D), BF16), pltpu.HBM((1, D), F32)],
        scratch_shapes=[pltpu.VMEM((D, D), F32)],
        compiler_params=_params(40, ("arbitrary",)),
    )(*_hbm(yg, ya, ym, x, tgt, wout_bf))


def _piece_specs(pieces, tm):
    return [pl.BlockSpec((tm, p.shape[1]), lambda i: (i, 0)) for p in pieces]


def _in_bwd_dx(pieces, x, dout, gain, w_t, dw_blocks):
    S, D = x.shape
    N = w_t.shape[0]
    tm = 256
    n = len(pieces)
    nsteps = S // tm
    middle_step = nsteps // 8

    def body(*refs):
        piece_refs = refs[:n]
        x_ref, dout_ref, g_ref, w_ref, dwb_ref, gx_ref, dg_ref, gw_ref = refs[n:n + 8]
        rs = _ReduceScatter([dwb_ref], [gw_ref], *refs[n + 8:])
        i = pl.program_id(0)

        @pl.when(i == 0)
        def _():
            dg_ref[...] = jnp.zeros_like(dg_ref)
            rs.start()

        @pl.when(i == middle_step)
        def _():
            rs.middle()

        dproj = jnp.concatenate([r[...] for r in piece_refs], axis=1)
        dh = jnp.dot(dproj, w_ref[...], preferred_element_type=F32)
        xv = x_ref[...]
        r = lax.rsqrt(jnp.mean(xv * xv, axis=-1, keepdims=True) + EPS)
        xh = xv * r
        a = dh * g_ref[...]
        gx_ref[...] = dout_ref[...] + r * (a - xh * jnp.mean(a * xh, axis=-1, keepdims=True))
        dg_ref[...] += jnp.sum(dh * xh, axis=0, keepdims=True)

        @pl.when(i == nsteps - 1)
        def _():
            rs.finish()

    tile = pl.BlockSpec((tm, D), lambda i: (i, 0))
    const = lambda shape: pl.BlockSpec(shape, lambda i: (0,) * len(shape))
    vmem = pl.BlockSpec(memory_space=pltpu.VMEM)
    return pl.pallas_call(
        body, name="in_bwd_dx", grid=(nsteps,),
        in_specs=_piece_specs(pieces, tm)
        + [tile, tile, const((1, D)), pl.BlockSpec((N, D), lambda i: (0, 0), pipeline_mode=pl.Buffered(1)), vmem],
        out_specs=[tile, const((1, D)), vmem],
        out_shape=[pltpu.HBM((S, D), F32), pltpu.HBM((1, D), F32), jax.ShapeDtypeStruct(dw_blocks.shape[1:], F32)],
        scratch_shapes=_reduce_scatter_scratch([dw_blocks]),
        compiler_params=_params(56, ("arbitrary",)),
    )(*_hbm(*pieces, x, dout, gain, w_t), dw_blocks)


def _in_bwd_dw(pieces, h_bf, others):
    S, D = h_bf.shape
    N = sum(p.shape[1] for p in pieces)
    n_blk = N // N_DEV
    tm = 512
    n = len(pieces)
    k = len(others)
    nsteps = S // tm

    def body(*refs):
        piece_refs = refs[:n]
        h_ref = refs[n]
        other_refs = refs[n + 1:n + 1 + k]
        dw_ref = refs[n + 1 + k]
        sum_refs = refs[n + 2 + k:n + 2 + 2 * k]
        acc_ref = refs[n + 2 + 2 * k]
        rs = _ReduceScatter(other_refs, sum_refs, *refs[n + 3 + 2 * k:])
        i = pl.program_id(0)

        @pl.when(i == 0)
        def _():
            acc_ref[...] = jnp.zeros_like(acc_ref)
            rs.start()

        @pl.when(i == 1)
        def _():
            rs.middle()

        dproj = jnp.concatenate([r[...] for r in piece_refs], axis=1)
        acc_ref[...] += _tn(h_ref[...], dproj)

        @pl.when(i == nsteps - 1)
        def _():
            for j in range(N_DEV):
                dw_ref[j] = acc_ref[:, n_blk * j:n_blk * (j + 1)].T.astype(BF16)
            rs.finish()

    vmem = pl.BlockSpec(memory_space=pltpu.VMEM)
    return pl.pallas_call(
        body, name="in_bwd_dw", grid=(nsteps,),
        in_specs=_piece_specs(pieces, tm) + [pl.BlockSpec((tm, D), lambda i: (i, 0))] + [vmem] * k,
        out_specs=[pl.BlockSpec((N_DEV, n_blk, D), lambda i: (0, 0, 0))] + [vmem] * k,
        out_shape=[pltpu.HBM((N_DEV, n_blk, D), BF16)] + [jax.ShapeDtypeStruct(o.shape[1:], F32) for o in others],
        scratch_shapes=[pltpu.VMEM((D, N), F32)] + _reduce_scatter_scratch(others),
        compiler_params=_params(56, ("arbitrary",)),
    )(*_hbm(*pieces, h_bf), *others)


def _row_step(m):
    return max(t for t in range(16, 257, 16) if m % t == 0)


def _place():
    x, y, c = lax.axis_index("x"), lax.axis_index("y"), lax.axis_index("c")
    chips = [(1 - x, y), (x, 1 - y), (1 - x, 1 - y)]
    return x, y, c, chips


class _AllGather:
    def __init__(self, srcs, outs, send_sems, recv_sems, local_sems, first_sem=0):
        self.srcs, self.outs, self.n, self.first_sem = srcs, outs, len(srcs), first_sem
        self.send_sems, self.recv_sems, self.local_sems = send_sems, recv_sems, local_sems

    def _rows(self, a, px, py, pc):
        m = self.srcs[a].shape[0]
        return self.outs[a].at[pl.ds((4 * px + 2 * py + pc) * m, m), :]

    def _copy(self, a, k, block, to, src=None):
        row = self.first_sem + a
        return pltpu.make_async_remote_copy(
            src_ref=self._rows(a, *block) if src is None else src, dst_ref=self._rows(a, *block),
            send_sem=self.send_sems.at[row, k], recv_sem=self.recv_sems.at[row, k], device_id=to, device_id_type=MESH)

    def _mine(self):
        x, y, c, _ = _place()
        return [pltpu.make_async_copy(self.srcs[a], self._rows(a, x, y, c), self.local_sems.at[self.first_sem + a])
                for a in range(self.n)]

    def _first(self, far):
        x, y, c, chips = _place()
        out = []
        for a in range(self.n):
            if far:
                out.append(self._copy(a, 3, (x, y, c), (*chips[2], c), src=self.srcs[a]))
            else:
                out.append(self._copy(a, 0, (x, y, c), (x, y, 1 - c), src=self.srcs[a]))
                out += [self._copy(a, 1 + j, (x, y, c), (*chips[j], c), src=self.srcs[a]) for j in (1, 0)]
        return out

    def _passed(self, j):
        x, y, c, chips = _place()
        return [self._copy(a, 4 + j, (*chips[j], c), (x, y, 1 - c)) for a in range(self.n)]

    def start(self):
        for cp in self._mine() + self._first(far=False):
            cp.start()

    def start_far(self):
        for cp in self._first(far=True):
            cp.start()

    def from_chip(self, j):
        x, y, c, chips = _place()
        for a in range(self.n):
            self._copy(a, 1 + j, (*chips[j], c), (x, y, c)).wait_recv()
        for cp in self._passed(j):
            cp.start()

    def from_sibling(self, j=None):
        x, y, c, chips = _place()
        for a in range(self.n):
            block = (x, y, 1 - c) if j is None else (*chips[j], 1 - c)
            self._copy(a, 0 if j is None else 4 + j, block, (x, y, c)).wait_recv()

    def from_self(self):
        for cp in self._mine():
            cp.wait()

    def finish(self):
        for cp in (self._first(far=False) + self._first(far=True)
                   + self._passed(0) + self._passed(1) + self._passed(2)):
            cp.wait_send()

    def run(self):
        self.start()
        self.start_far()
        self.from_self()
        for j in range(3):
            self.from_chip(j)
        self.from_sibling()
        for j in range(3):
            self.from_sibling(j)
        self.finish()


def _gather_proj(x, gain, shards, xpos):
    S, D = x.shape
    n = len(shards)
    N = N_DEV * shards[0].shape[0]
    half = N // 2
    tm = 512
    nsteps = S // tm

    def body(*refs):
        xpos_ref, x_ref, g_ref = refs[:3]
        ins = refs[3:3 + n]
        proj_ref, h_ref = refs[3 + n:5 + n]
        outs = refs[5 + n:5 + 2 * n]
        casts = refs[5 + 2 * n:5 + 3 * n]
        whole = refs[5 + 3 * n:5 + 4 * n]
        sems = refs[5 + 4 * n:8 + 4 * n]
        ag = _AllGather(casts[:1], whole[:1], *sems)
        later = _AllGather(casts[1:], whole[1:], *sems, first_sem=1)
        out_sems, h_all = refs[8 + 4 * n:]
        p, i = pl.program_id(0), pl.program_id(1)
        rows = pl.ds(pl.multiple_of(i * tm, tm), tm)

        @pl.when((p == 0) & (i == 0))
        def _():
            for a in range(n):
                tr = _row_step(ins[a].shape[0])

                def cast(r, carry, a=a, tr=tr):
                    at = pl.ds(pl.multiple_of(r * tr, tr), tr)
                    casts[a][at, :] = ins[a][at, :].astype(BF16)
                    return carry
                lax.fori_loop(0, ins[a].shape[0] // tr, cast, 0)
            ag.start()

        @pl.when(p == 0)
        def _():
            xv = x_ref[...]
            r = lax.rsqrt(jnp.mean(xv * xv, axis=-1, keepdims=True) + EPS)
            h = ((xv * r) * g_ref[...]).astype(BF16)
            h_ref[...] = h
            h_all[rows, :] = h

        @pl.when((p == 1) & (i == 0))
        def _():
            ag.from_self()
            ag.from_chip(1)
            ag.start_far()
            later.start()
            later.start_far()
            ag.from_sibling()
            ag.from_sibling(1)

        @pl.when((p == 2) & (i == 0))
        def _():
            for j in (0, 2):
                ag.from_chip(j)
            for j in (0, 2):
                ag.from_sibling(j)

        @pl.when(p > 0)
        def _():
            which = (xpos_ref[0] + p - 1) % 2
            w_half = whole[0][pl.ds(pl.multiple_of(which * half, half), half), :]
            proj_ref[...] = _nt(h_all[rows, :], w_half)

        @pl.when((p == 2) & (i == nsteps - 1))
        def _():
            ag.finish()
            later.from_self()
            for j in range(3):
                later.from_chip(j)
            later.from_sibling()
            for j in range(3):
                later.from_sibling(j)
            later.finish()
            to_results = [pltpu.make_async_copy(whole[a], outs[a], out_sems.at[a]) for a in range(n)]
            for cp in to_results:
                cp.start()
            for cp in to_results:
                cp.wait()

    vmem = pl.BlockSpec(memory_space=pltpu.VMEM)
    hbm = pl.BlockSpec(memory_space=pl.ANY)
    gathered = [(N_DEV * a.shape[0], a.shape[1]) for a in shards]
    x_tile = lambda p, i, xp: (jnp.where(p == 0, i, nsteps - 1), 0)
    proj_tile = lambda p, i, xp: (jnp.where(p == 0, 0, i), (xp[0] + jnp.maximum(p - 1, 0)) % 2)
    grid_spec = pltpu.PrefetchScalarGridSpec(
        num_scalar_prefetch=1, grid=(3, nsteps),
        in_specs=[pl.BlockSpec((tm, D), x_tile), pl.BlockSpec((1, D), lambda p, i, xp: (0, 0))] + [vmem] * n,
        out_specs=[pl.BlockSpec((tm, half), proj_tile), pl.BlockSpec((tm, D), x_tile)] + [hbm] * n,
        scratch_shapes=[pltpu.VMEM(a.shape, BF16) for a in shards] + [pltpu.VMEM(g, BF16) for g in gathered]
        + [pltpu.SemaphoreType.DMA((n, 7)), pltpu.SemaphoreType.DMA((n, 7)), pltpu.SemaphoreType.DMA((n,)),
           pltpu.SemaphoreType.DMA((n,)), pltpu.VMEM((S, D), BF16)])
    return pl.pallas_call(
        body, name="gather_proj", grid_spec=grid_spec,
        out_shape=[pltpu.HBM((S, N), F32), pltpu.HBM((S, D), BF16)] + [pltpu.HBM(g, BF16) for g in gathered],
        compiler_params=_params(56, ("arbitrary", "arbitrary")),
    )(xpos, *_hbm(x, gain), *shards)


ROW_NORM, ROW_MEM_NORM, ROW_V_GAIN, ROW_B, ROW_ATTN_GAINS, ROW_MEM_GAINS, ROW_W_S, ROW_LOSS = 0, 8, 16, 18, 22, 23, 24, 536
SMALL_ROWS = 544


def _gather_small(dgain, dmgain, dvg, db2, dqg, dkg, dmqg, dmkg, dws, sq):
    def body(dgain_ref, dmgain_ref, dvg_ref, db2_ref, dqg_ref, dkg_ref, dmqg_ref, dmkg_ref, dws_ref, sq_ref,
             out_ref, mine, send_sems, recv_sems, local_sems):
        first = lax.broadcasted_iota(jnp.int32, (1, 128), 1) < HEAD_DIM
        for i in range(8):
            cols = slice(128 * i, 128 * (i + 1))
            mine[ROW_NORM + i:ROW_NORM + i + 1, :] = dgain_ref[:, cols]
            mine[ROW_MEM_NORM + i:ROW_MEM_NORM + i + 1, :] = dmgain_ref[:, cols]
            mine[ROW_LOSS + i:ROW_LOSS + i + 1, :] = sq_ref[:, cols]
        mine[ROW_V_GAIN:ROW_V_GAIN + 1, :] = dvg_ref[:, 0:128]
        mine[ROW_V_GAIN + 1:ROW_V_GAIN + 2, :] = dvg_ref[:, 128:256]
        bt = db2_ref[...].T
        for h in range(4):
            mine[ROW_B + h:ROW_B + h + 1, :] = bt[HEAD_DIM * h:HEAD_DIM * h + 1, :]

        def fold_heads(t):
            return t + pltpu.roll(t, HEAD_DIM, axis=1)
        aq = fold_heads(dqg_ref[0] + dqg_ref[1] + dqg_ref[2] + dqg_ref[3])
        ak = fold_heads(dkg_ref[0] + dkg_ref[1] + dkg_ref[2] + dkg_ref[3])
        mine[ROW_ATTN_GAINS:ROW_ATTN_GAINS + 1, :] = jnp.where(first, aq, ak)
        mq = fold_heads(dmqg_ref[:, 0:128] + dmqg_ref[:, 128:256])
        mk = fold_heads(dmkg_ref[:, 0:128] + dmkg_ref[:, 128:256])
        mine[ROW_MEM_GAINS:ROW_MEM_GAINS + 1, :] = jnp.where(first, mq, mk)
        mine[ROW_W_S:ROW_W_S + 4 * CHUNK, :] = dws_ref[...]
        _AllGather([mine], [out_ref], send_sems, recv_sems, local_sems).run()

    return pl.pallas_call(
        body, name="gather_small_grads",
        out_shape=jax.ShapeDtypeStruct((N_DEV * SMALL_ROWS, 128), F32),
        scratch_shapes=[pltpu.VMEM((SMALL_ROWS, 128), F32), pltpu.SemaphoreType.DMA((1, 7)),
                        pltpu.SemaphoreType.DMA((1, 7)), pltpu.SemaphoreType.DMA((1,))],
        compiler_params=_params(16),
    )(dgain, dmgain, dvg, db2, dqg, dkg, dmqg, dmkg, dws, sq)


def _reduce_scatter_scratch(arrs):
    n = len(arrs)
    return ([pltpu.VMEM((4,) + a.shape[1:], BF16) for a in arrs] + [pltpu.VMEM((3,) + a.shape[1:], BF16) for a in arrs]
            + [pltpu.SemaphoreType.DMA((n, 7)), pltpu.SemaphoreType.DMA((n, 7))])


class _ReduceScatter:
    def __init__(self, ins, outs, *scratch):
        n = len(ins)
        self.n, self.ins, self.outs = n, ins, outs
        self.half, self.quarter = scratch[:n], scratch[n:2 * n]
        self.send_sems, self.recv_sems = scratch[2 * n:]

    def _to_sibling(self):
        x, y, c, _ = _place()
        return [pltpu.make_async_remote_copy(
            src_ref=self.ins[a].at[2 * q + (1 - c)], dst_ref=self.half[a].at[q], send_sem=self.send_sems.at[a, q],
            recv_sem=self.recv_sems.at[a, q], device_id=(x, y, 1 - c), device_id_type=MESH)
            for a in range(self.n) for q in range(4)]

    def _to_chips(self):
        _, _, c, chips = _place()
        return [pltpu.make_async_remote_copy(
            src_ref=self.half[a].at[2 * chip[0] + chip[1]], dst_ref=self.quarter[a].at[k],
            send_sem=self.send_sems.at[a, 4 + k], recv_sem=self.recv_sems.at[a, 4 + k], device_id=(*chip, c),
            device_id_type=MESH) for a in range(self.n) for k, chip in enumerate(chips)]

    def _rows(self, a, fn):
        m = self.ins[a].shape[1]
        tr = _row_step(m)

        def step(i, carry):
            fn(pl.ds(pl.multiple_of(i * tr, tr), tr))
            return carry
        lax.fori_loop(0, m // tr, step, 0)

    def start(self):
        for cp in self._to_sibling():
            cp.start()

    def middle(self):
        _, _, c, _ = _place()
        for cp in self._to_sibling():
            cp.wait_recv()
        for a in range(self.n):
            for q in range(4):
                def add_half(rows, a=a, q=q):
                    both = self.ins[a][2 * q + c, rows, :].astype(F32) + self.half[a][q, rows, :].astype(F32)
                    self.half[a][q, rows, :] = both.astype(BF16)
                self._rows(a, add_half)
        for cp in self._to_chips():
            cp.start()

    def finish(self):
        x, y, _, _ = _place()
        for cp in self._to_chips():
            cp.wait_recv()
        for a in range(self.n):
            def add_quarters(rows, a=a):
                f = lambda t: t.astype(F32)
                self.outs[a][rows, :] = ((f(self.half[a][2 * x + y, rows, :]) + f(self.quarter[a][0, rows, :]))
                                         + (f(self.quarter[a][1, rows, :]) + f(self.quarter[a][2, rows, :])))
            self._rows(a, add_quarters)
        for cp in self._to_sibling() + self._to_chips():
            cp.wait_send()


def _adamw_math(w, g, m, v):
    m = ADAM_B1 * m + (1.0 - ADAM_B1) * g
    v = ADAM_B2 * v + (1.0 - ADAM_B2) * (g * g)
    m_hat = m / (1.0 - ADAM_B1 ** ADAM_STEP)
    v_hat = v / (1.0 - ADAM_B2 ** ADAM_STEP)
    delta = -ADAM_LR * (m_hat / (jnp.sqrt(v_hat) + ADAM_EPS) + ADAM_WD * w)
    return delta, m, v


def _adamw(w, g, m, v, name):
    R, C = w.shape
    tr = _row_step(R)

    def body(w_ref, g_ref, m_ref, v_ref, d_ref, nm_ref, nv_ref):
        d_ref[...], nm_ref[...], nv_ref[...] = _adamw_math(w_ref[...], g_ref[...], m_ref[...], v_ref[...])

    tile = pl.BlockSpec((tr, C), lambda i: (i, 0))
    out = pltpu.HBM((R, C), F32)
    return pl.pallas_call(
        body, name=name, grid=(R // tr,), in_specs=[tile] * 4, out_specs=[tile] * 3, out_shape=[out] * 3,
        compiler_params=_params(16, ("arbitrary",)),
    )(*_hbm(w, g, m, v))


SMALL = ("norm_gain", "gmlp_v_gain", "gmlp_w_s", "gmlp_b", "attn_q_gain", "attn_k_gain", "mem_norm_gain",
         "mem_q_gain", "mem_k_gain")
WEIGHTS = ("norm_gain", "w_in", "gmlp_v_gain", "gmlp_w_s", "gmlp_b", "attn_q_gain", "attn_k_gain",
           "mem_norm_gain", "w_mem_kv", "mem_q_gain", "mem_k_gain", "w_out")


def _adamw_small(w, m, v, g_all):
    k = len(SMALL)
    half = slice(0, HEAD_DIM), slice(HEAD_DIM, 2 * HEAD_DIM)

    def body(*refs):
        w_refs, m_refs, v_refs = refs[:k], refs[k:2 * k], refs[2 * k:3 * k]
        g_ref = refs[3 * k]
        outs = refs[3 * k + 1:7 * k + 1]
        loss_ref, gsum = refs[7 * k + 1:]

        part = SMALL_ROWS // 4
        for p in range(4):
            acc = g_ref[part * p:part * (p + 1), :]
            for dev in range(1, N_DEV):
                acc = acc + g_ref[dev * SMALL_ROWS + part * p:dev * SMALL_ROWS + part * (p + 1), :]
            gsum[part * p:part * (p + 1), :] = acc

        def update(name, at, g):
            i = SMALL.index(name)
            d, nm, nv = _adamw_math(w_refs[i][at], g, m_refs[i][at], v_refs[i][at])
            outs[i][at], outs[k + i][at], outs[2 * k + i][at], outs[3 * k + i][at] = g, d, nm, nv

        for i in range(8):
            at = (slice(0, 1), slice(128 * i, 128 * (i + 1)))
            update("norm_gain", at, gsum[ROW_NORM + i:ROW_NORM + i + 1, :])
            update("mem_norm_gain", at, gsum[ROW_MEM_NORM + i:ROW_MEM_NORM + i + 1, :])
        for h in range(4):
            row = (0, slice(h, h + 1), slice(None))
            update("gmlp_v_gain", row, gsum[ROW_V_GAIN + h // 2:ROW_V_GAIN + h // 2 + 1, half[h % 2]])
            update("gmlp_b", row, gsum[ROW_B + h:ROW_B + h + 1, :])
            update("gmlp_w_s", (0, h), gsum[ROW_W_S + CHUNK * h:ROW_W_S + CHUNK * (h + 1), :])
        whole = (slice(0, 1), slice(None))
        update("attn_q_gain", whole, gsum[ROW_ATTN_GAINS:ROW_ATTN_GAINS + 1, half[0]])
        update("attn_k_gain", whole, gsum[ROW_ATTN_GAINS:ROW_ATTN_GAINS + 1, half[1]])
        update("mem_q_gain", whole, gsum[ROW_MEM_GAINS:ROW_MEM_GAINS + 1, half[0]])
        update("mem_k_gain", whole, gsum[ROW_MEM_GAINS:ROW_MEM_GAINS + 1, half[1]])
        loss_ref[...] = jnp.sum(gsum[ROW_LOSS:ROW_LOSS + 8, :], keepdims=True) * (0.5 / D_MODEL)

    shapes = [jax.ShapeDtypeStruct(w[name].shape, F32) for name in SMALL]
    res = pl.pallas_call(
        body, name="adamw_small",
        out_shape=shapes * 4 + [jax.ShapeDtypeStruct((1, 1), F32)],
        scratch_shapes=[pltpu.VMEM((SMALL_ROWS, 128), F32)],
        compiler_params=_params(16),
    )(*[w[n] for n in SMALL], *[m[n] for n in SMALL], *[v[n] for n in SMALL], g_all)
    trees = [dict(zip(SMALL, res[j * k:(j + 1) * k])) for j in range(4)]
    return (*trees, res[4 * k])


def _grads(x, mem, tgt, w, shards):
    bd128, bd256 = _head_blockdiag(128), _head_blockdiag(256)
    gain = w["norm_gain"].reshape(1, D_MODEL)
    vg = w["gmlp_v_gain"].reshape(1, GMLP_WIDTH)
    w_s = w["gmlp_w_s"].reshape(4, CHUNK, CHUNK)
    b2 = jnp.repeat(w["gmlp_b"].reshape(4, CHUNK).T, HEAD_DIM, axis=1)
    qg2 = jnp.tile(w["attn_q_gain"].reshape(1, HEAD_DIM), (1, 2))
    kg2 = jnp.tile(w["attn_k_gain"].reshape(1, HEAD_DIM), (1, 2))
    mqg4 = jnp.tile(w["mem_q_gain"].reshape(1, HEAD_DIM), (1, 4))
    mkg4 = jnp.tile(w["mem_k_gain"].reshape(1, HEAD_DIM), (1, 4))
    mgain = w["mem_norm_gain"].reshape(1, D_MODEL)

    xpos = lax.axis_index("x").astype(jnp.int32).reshape(1)
    proj, h_bf, win_t, wkv_bf, wout_bf = _gather_proj(x, gain, shards, xpos)
    yg = _gmlp_fwd(proj, vg, w_s, b2, bd256)
    ya, att, lse = _attn_fwd(proj, qg2, kg2, bd128)
    hm_bf, kraw, mk, mv = _mem_kv(mem, mgain, wkv_bf, mkg4, bd256)
    ym, om = _mem_fwd(proj, mk, mv, mqg4, bd256)
    dout, dycat, dwout, sq = _out_loss(yg, ya, ym, x, tgt, wout_bf)

    du, dgv, dgg, dws, db2, dvg = _gmlp_bwd(proj, dycat, vg, w_s, b2, bd256)
    dq, dk, dv, dag, dqg, dkg = _attn_bwd(proj, dycat, att, lse, qg2, kg2, bd128)
    dmq, dmg, dmk, dmv, dmqg = _mem_bwd(proj, dycat, om, mk, mv, mqg4, bd256)
    dwkv, dmgain, dmkg = _mem_kv_bwd(dmk, dmv, kraw, mem, mgain, mkg4, wkv_bf, hm_bf, bd256)
    pieces = [du, dgv, dgg, dq, dk, dv, dag, dmq, dmg]
    dwin, g_wkv, g_wout = _in_bwd_dw(pieces, h_bf, [dwkv, dwout])
    grad_x, dgain, g_win = _in_bwd_dx(pieces, x, dout, gain, win_t, dwin)
    return grad_x, g_win, g_wkv, g_wout, (dgain, dmgain, dvg, db2, dqg, dkg, dmqg, dmkg, dws, sq)


def kernel(x, mem, norm_gain, w_in, gmlp_v_gain, gmlp_w_s, gmlp_b, attn_q_gain, attn_k_gain, mem_norm_gain, w_mem_kv, mem_q_gain, mem_k_gain, w_out, loss_target, m_norm_gain, m_w_in, m_gmlp_v_gain, m_gmlp_w_s, m_gmlp_b, m_attn_q_gain, m_attn_k_gain, m_mem_norm_gain, m_w_mem_kv, m_mem_q_gain, m_mem_k_gain, m_w_out, v_norm_gain, v_w_in, v_gmlp_v_gain, v_gmlp_w_s, v_gmlp_b, v_attn_q_gain, v_attn_k_gain, v_mem_norm_gain, v_w_mem_kv, v_mem_q_gain, v_mem_k_gain, v_w_out):
    w = dict(norm_gain=norm_gain, w_in=w_in, gmlp_v_gain=gmlp_v_gain, gmlp_w_s=gmlp_w_s, gmlp_b=gmlp_b,
             attn_q_gain=attn_q_gain, attn_k_gain=attn_k_gain, mem_norm_gain=mem_norm_gain, w_mem_kv=w_mem_kv,
             mem_q_gain=mem_q_gain, mem_k_gain=mem_k_gain, w_out=w_out)
    m = dict(norm_gain=m_norm_gain, w_in=m_w_in, gmlp_v_gain=m_gmlp_v_gain, gmlp_w_s=m_gmlp_w_s, gmlp_b=m_gmlp_b,
             attn_q_gain=m_attn_q_gain, attn_k_gain=m_attn_k_gain, mem_norm_gain=m_mem_norm_gain,
             w_mem_kv=m_w_mem_kv, mem_q_gain=m_mem_q_gain, mem_k_gain=m_mem_k_gain, w_out=m_w_out)
    v = dict(norm_gain=v_norm_gain, w_in=v_w_in, gmlp_v_gain=v_gmlp_v_gain, gmlp_w_s=v_gmlp_w_s, gmlp_b=v_gmlp_b,
             attn_q_gain=v_attn_q_gain, attn_k_gain=v_attn_k_gain, mem_norm_gain=v_mem_norm_gain,
             w_mem_kv=v_w_mem_kv, mem_q_gain=v_mem_q_gain, mem_k_gain=v_mem_k_gain, w_out=v_w_out)
    transposed = lambda t: jnp.transpose(t[0])

    grad_x, g_win, g_wkv, g_wout, small = _grads(
        x[0], mem[0], loss_target[0], w, [transposed(w_in), w_mem_kv[0], w_out[0]])
    small_all = _gather_small(*small)

    out_g, out_d, out_m, out_v, loss = _adamw_small(w, m, v, small_all)
    d_, m_, v_ = _adamw(transposed(w_in), g_win, transposed(m_w_in), transposed(v_w_in), "adamw_w_in")
    for tree, t in ((out_g, g_win), (out_d, d_), (out_m, m_), (out_v, v_)):
        tree["w_in"] = jnp.transpose(t)[None]
    for name, g in (("w_mem_kv", g_wkv), ("w_out", g_wout)):
        d_, m_, v_ = _adamw(w[name][0], g, m[name][0], v[name][0], "adamw_" + name)
        out_g[name], out_d[name], out_m[name], out_v[name] = g[None], d_[None], m_[None], v_[None]

    return (loss.reshape(()), grad_x[None], *[out_g[k] for k in WEIGHTS], *[out_d[k] for k in WEIGHTS],
            *[out_m[k] for k in WEIGHTS], *[out_v[k] for k in WEIGHTS])
```

```python
import functools
import math

import jax
import jax.numpy as jnp
from jax import lax
from jax.experimental import pallas as pl
from jax.experimental.pallas import tpu as pltpu

F32 = jnp.float32
BF16 = jnp.bfloat16

N_DEV = 8
D_MODEL = 1024
HEAD_DIM = 64
GMLP_WIDTH = 256
ATTN_WIDTH = 512
MEM_WIDTH = 256
MEM_LEN = 256
CHUNK = 128
BLOCK = 128
DILATIONS = (1, 4, 16)
CONFIG_ORDER = tuple(reversed(DILATIONS))
EPS = 1e-6
SCALE = 1.0 / math.sqrt(HEAD_DIM)
NEG = -1e30

ADAM_LR = 0.001
ADAM_B1 = 0.9
ADAM_B2 = 0.999
ADAM_EPS = 1e-08
ADAM_WD = 0.01
ADAM_STEP = 10

MIB = 1024 * 1024
MESH = pl.DeviceIdType.MESH

COL_AQ, COL_AK, COL_AV, COL_AG = 6, 10, 14, 18


def _params(vmem_mib, semantics=None):
    kw = dict(vmem_limit_bytes=vmem_mib * MIB)
    if semantics is not None:
        kw["dimension_semantics"] = semantics
    return pltpu.CompilerParams(**kw)


def _hbm(*arrs):
    return [pltpu.with_memory_space_constraint(a, pltpu.HBM) for a in arrs]


def _split_dot(x, sel_bf):
    hi = x.astype(BF16)
    lo = (x - hi.astype(F32)).astype(BF16)
    return jnp.dot(hi, sel_bf, preferred_element_type=F32) + jnp.dot(lo, sel_bf, preferred_element_type=F32)


def _nt(a, b):
    return lax.dot_general(a, b, (((1,), (1,)), ((), ())), preferred_element_type=F32)


def _tn(a, b):
    return lax.dot_general(a, b, (((0,), (0,)), ((), ())), preferred_element_type=F32)


def _silu_parts(g):
    sg = jax.nn.sigmoid(g)
    return g * sg, sg * (1.0 + g * (1.0 - sg))


def _head_index(shape):
    return lax.shift_right_logical(lax.broadcasted_iota(jnp.int32, shape, 1), HEAD_DIM.bit_length() - 1)


def _head_blockdiag(width):
    i = jnp.arange(width) // HEAD_DIM
    return (i[:, None] == i[None, :]).astype(BF16)


def _gmlp_masked_weights(ws_ref, transpose):
    t = lax.broadcasted_iota(jnp.int32, (CHUNK, CHUNK), 0)
    s = lax.broadcasted_iota(jnp.int32, (CHUNK, CHUNK), 1)
    parts = []
    for h in range(4):
        wm = jnp.where(s <= t, ws_ref[h], 0.0)
        parts.append(wm.T if transpose else wm)
    return jnp.concatenate(parts, axis=1).astype(BF16)


def _head_stack(v, head):
    return jnp.concatenate([jnp.where(head == h, v, 0.0) for h in range(4)], axis=0).astype(BF16)


def _gmlp_fwd(proj, vg, w_s, b2, bd):
    S = proj.shape[0]
    tm = 512

    def body(u_ref, v_ref, g_ref, vg_ref, ws_ref, b2_ref, bd_ref, y_ref):
        v = v_ref[...]
        ms = _split_dot(v * v, bd_ref[...]) * (1.0 / HEAD_DIM)
        vn = (v * lax.rsqrt(ms + EPS)) * vg_ref[...]
        wcat = _gmlp_masked_weights(ws_ref, False)
        head = _head_index((CHUNK, GMLP_WIDTH))
        for c in range(tm // CHUNK):
            rows = slice(c * CHUNK, (c + 1) * CHUNK)
            sp = jnp.dot(wcat, _head_stack(vn[rows], head), preferred_element_type=F32) + b2_ref[...]
            silu, _ = _silu_parts(g_ref[rows, :])
            y_ref[rows, :] = ((u_ref[rows, :] * sp) * silu).astype(BF16)

    col = lambda j: pl.BlockSpec((tm, GMLP_WIDTH), lambda i, j=j: (i, j))
    const = lambda shape: pl.BlockSpec(shape, lambda i: (0,) * len(shape))
    return pl.pallas_call(
        body, name="gmlp_fwd", grid=(S // tm,),
        in_specs=[col(0), col(1), col(2), const((1, GMLP_WIDTH)), const((4, CHUNK, CHUNK)),
                  const((CHUNK, GMLP_WIDTH)), const((GMLP_WIDTH, GMLP_WIDTH))],
        out_specs=pl.BlockSpec((tm, GMLP_WIDTH), lambda i: (i, 0)),
        out_shape=pltpu.HBM((S, GMLP_WIDTH), BF16),
        compiler_params=_params(24, ("arbitrary",)),
    )(*_hbm(proj, proj, proj, vg, w_s, b2, bd))


def _gmlp_bwd(proj, dycat, vg, w_s, b2, bd):
    S = proj.shape[0]
    tm = 512
    nsteps = S // tm

    def body(u_ref, v_ref, g_ref, dy_ref, vg_ref, ws_ref, b2_ref, bd_ref,
             du_ref, dv_ref, dg_ref, dws_ref, db2_ref, dvg_ref):
        i = pl.program_id(0)

        @pl.when(i == 0)
        def _():
            dws_ref[...] = jnp.zeros_like(dws_ref)
            db2_ref[...] = jnp.zeros_like(db2_ref)
            dvg_ref[...] = jnp.zeros_like(dvg_ref)

        bdv = bd_ref[...]
        v = v_ref[...]
        ms = _split_dot(v * v, bdv) * (1.0 / HEAD_DIM)
        rv = lax.rsqrt(ms + EPS)
        xhat = v * rv
        vgv = vg_ref[...]
        vn = xhat * vgv
        wcat = _gmlp_masked_weights(ws_ref, False)
        wcat_t = _gmlp_masked_weights(ws_ref, True)
        head = _head_index((CHUNK, GMLP_WIDTH))
        dvg = jnp.zeros((1, GMLP_WIDTH), F32)
        for c in range(tm // CHUNK):
            rows = slice(c * CHUNK, (c + 1) * CHUNK)
            vn_c = vn[rows]
            spb = jnp.dot(wcat, _head_stack(vn_c, head), preferred_element_type=F32) + b2_ref[...]
            silu, dsilu = _silu_parts(g_ref[rows, :])
            dy = dy_ref[rows, :]
            u = u_ref[rows, :]
            du_ref[rows, :] = (dy * spb * silu).astype(BF16)
            dg_ref[rows, :] = (dy * u * spb * dsilu).astype(BF16)
            dsp = dy * u * silu
            db2_ref[...] += dsp
            dstack = _head_stack(dsp, head)
            dvn = jnp.dot(wcat_t, dstack, preferred_element_type=F32)
            dws_ref[...] += _nt(dstack, vn_c.astype(BF16))
            xh = xhat[rows]
            a = dvn * vgv
            mean_ax = _split_dot(a * xh, bdv) * (1.0 / HEAD_DIM)
            dv_ref[rows, :] = (rv[rows] * (a - xh * mean_ax)).astype(BF16)
            dvg = dvg + jnp.sum(dvn * xh, axis=0, keepdims=True)
        dvg_ref[...] += dvg

        @pl.when(i == nsteps - 1)
        def _():
            t = lax.broadcasted_iota(jnp.int32, (4 * CHUNK, CHUNK), 0) % CHUNK
            s = lax.broadcasted_iota(jnp.int32, (4 * CHUNK, CHUNK), 1)
            dws_ref[...] = jnp.where(s <= t, dws_ref[...], 0.0)
            db2_ref[...] = _split_dot(db2_ref[...], bdv)

    col = lambda j: pl.BlockSpec((tm, GMLP_WIDTH), lambda i, j=j: (i, j))
    const = lambda shape: pl.BlockSpec(shape, lambda i: (0,) * len(shape))
    tile = pl.BlockSpec((tm, GMLP_WIDTH), lambda i: (i, 0))
    piece = pltpu.HBM((S, GMLP_WIDTH), BF16)
    return pl.pallas_call(
        body, name="gmlp_bwd", grid=(nsteps,),
        in_specs=[col(0), col(1), col(2), col(0), const((1, GMLP_WIDTH)), const((4, CHUNK, CHUNK)),
                  const((CHUNK, GMLP_WIDTH)), const((GMLP_WIDTH, GMLP_WIDTH))],
        out_specs=[tile, tile, tile, const((4 * CHUNK, CHUNK)), const((CHUNK, GMLP_WIDTH)), const((1, GMLP_WIDTH))],
        out_shape=[piece, piece, piece, pltpu.HBM((4 * CHUNK, CHUNK), F32),
                   pltpu.HBM((CHUNK, GMLP_WIDTH), F32), pltpu.HBM((1, GMLP_WIDTH), F32)],
        compiler_params=_params(32, ("arbitrary",)),
    )(*_hbm(proj, proj, proj, dycat, vg, w_s, b2, bd))


def _band_mask():
    qi = lax.broadcasted_iota(jnp.int32, (2 * BLOCK, 2 * BLOCK), 0) % BLOCK
    ki = lax.broadcasted_iota(jnp.int32, (2 * BLOCK, 2 * BLOCK), 1)
    return ((ki < BLOCK) & (ki >= qi)) | ((ki >= BLOCK) & ((ki - BLOCK) <= qi))


def _first_block_bias(blk, blocks_per_class):
    kcol = lax.broadcasted_iota(jnp.int32, (1, 2 * BLOCK), 1)
    kill = jnp.where((blk & (blocks_per_class - 1)) == 0, NEG, 0.0)
    return jnp.where(kcol < BLOCK, kill, 0.0)


def _two_heads(q, lo):
    zero = jnp.zeros_like(q)
    return jnp.concatenate([jnp.where(lo, q, zero), jnp.where(lo, zero, q)], axis=0)


def _block_tokens(blk, d, S):
    if d == 1:
        return pl.ds(pl.multiple_of(blk * BLOCK, BLOCK), BLOCK)
    blocks_per_class = S // d // BLOCK
    r = lax.shift_right_logical(blk, blocks_per_class.bit_length() - 1)
    n = blk & (blocks_per_class - 1)
    return pl.ds(r + n * (BLOCK * d), BLOCK, stride=d)


def _padded_block(blk):
    return pl.ds(pl.multiple_of((blk + 1) * BLOCK, BLOCK), BLOCK)


def _for_blocks(n_blocks, unroll, fn):
    def group(g, carry):
        for u in range(unroll):
            fn(g * unroll + u)
        return carry
    lax.fori_loop(0, n_blocks // unroll, group, 0)


def _attn_fwd(proj, qg2, kg2, bd):
    S = proj.shape[0]
    npairs = ATTN_WIDTH // 128
    tn = 512

    def body(q_ref, k_ref, v_ref, g_ref, qg_ref, kg_ref, bd_ref, y_ref, att_ref, lse_ref, qn, kn, kc, vc):
        bdv = bd_ref[...]
        lo = lax.broadcasted_iota(jnp.int32, (BLOCK, 128), 1) < HEAD_DIM
        band_mask = _band_mask()
        kc[pl.ds(0, BLOCK), :] = jnp.zeros((BLOCK, 128), BF16)
        vc[pl.ds(0, BLOCK), :] = jnp.zeros((BLOCK, 128), BF16)

        def norm_step(i, carry):
            rows = pl.ds(pl.multiple_of(i * tn, tn), tn)
            qv = q_ref[rows, :]
            kv = k_ref[rows, :]
            qn[rows, :] = (qv * lax.rsqrt(_split_dot(qv * qv, bdv) * (1.0 / HEAD_DIM) + EPS)) * (qg_ref[...] * SCALE)
            kn[rows, :] = (kv * lax.rsqrt(_split_dot(kv * kv, bdv) * (1.0 / HEAD_DIM) + EPS)) * kg_ref[...]
            return carry
        lax.fori_loop(0, S // tn, norm_step, 0)

        def fill(blk, d):
            tokens = _block_tokens(blk, d, S)
            kc[_padded_block(blk), :] = kn[tokens, :].astype(BF16)
            vc[_padded_block(blk), :] = v_ref[tokens, :].astype(BF16)

        ones_bf = jnp.ones((2 * BLOCK, 128), BF16)

        def block(blk, d):
            tokens = _block_tokens(blk, d, S)
            keys = pl.ds(pl.multiple_of(blk * BLOCK, BLOCK), 2 * BLOCK)
            q2 = _two_heads(qn[tokens, :].astype(BF16), lo)
            s = jnp.where(band_mask, _nt(q2, kc[keys, :]), NEG) + _first_block_bias(blk, S // d // BLOCK)
            m = jnp.max(s, axis=-1, keepdims=True)
            e = jnp.exp((s - m).astype(BF16))
            ol = jnp.dot(e, jnp.concatenate([vc[keys, :], ones_bf], axis=1), preferred_element_type=F32)
            l = ol[:, 128:]
            o2 = ol[:, :128] * (1.0 / l)
            lse2 = m + jnp.log(l)
            o = jnp.where(lo, o2[:BLOCK], o2[BLOCK:])
            lse = jnp.where(lo, lse2[:BLOCK], lse2[BLOCK:])
            if d != CONFIG_ORDER[0]:
                la = lse_ref[tokens, :]
                mx = jnp.maximum(la, lse)
                wa, wb = jnp.exp(la - mx), jnp.exp(lse - mx)
                t = wa + wb
                o = (wa * att_ref[tokens, :] + wb * o) / t
                lse = mx + jnp.log(t)
            att_ref[tokens, :] = o
            lse_ref[tokens, :] = lse

        for d in CONFIG_ORDER:
            _for_blocks(S // BLOCK, 4, functools.partial(fill, d=d))
            _for_blocks(S // BLOCK, 16, functools.partial(block, d=d))

        def gate_step(i, carry):
            rows = pl.ds(pl.multiple_of(i * tn, tn), tn)
            silu, _ = _silu_parts(g_ref[rows, :])
            y_ref[rows, :] = (att_ref[rows, :] * silu).astype(BF16)
            return carry
        lax.fori_loop(0, S // tn, gate_step, 0)

    col = lambda j0: pl.BlockSpec((S, 128), lambda p, j0=j0: (0, j0 + p))
    const = lambda shape: pl.BlockSpec(shape, lambda p: (0,) * len(shape))
    out = pl.BlockSpec((S, 128), lambda p: (0, p))
    return pl.pallas_call(
        body, name="attn_fwd", grid=(npairs,),
        in_specs=[col(COL_AQ), col(COL_AK), col(COL_AV), col(COL_AG), const((1, 128)), const((1, 128)),
                  const((128, 128))],
        out_specs=[out, out, out],
        out_shape=[pltpu.HBM((S, ATTN_WIDTH), BF16), pltpu.HBM((S, ATTN_WIDTH), F32),
                   pltpu.HBM((S, ATTN_WIDTH), F32)],
        scratch_shapes=[pltpu.VMEM((S, 128), F32), pltpu.VMEM((S, 128), F32),
                        pltpu.VMEM((S + BLOCK, 128), BF16), pltpu.VMEM((S + BLOCK, 128), BF16)],
        compiler_params=_params(48, ("arbitrary",)),
    )(*_hbm(proj, proj, proj, proj, qg2, kg2, bd))


def _attn_bwd(proj, dycat, att, lse, qg2, kg2, bd):
    S = proj.shape[0]
    npairs = ATTN_WIDTH // 128
    tn = 512

    def body(q_ref, k_ref, v_ref, g_ref, dy_ref, att_ref, lse_ref, qg_ref, kg_ref, bd_ref,
             dq_ref, dk_ref, dv_ref, dg_ref, dqg_ref, dkg_ref,
             qn, kn, rq_s, rk_s, kc, vc, do_s, dd_s, dqa, dka, dva):
        bdv = bd_ref[...]
        lo = lax.broadcasted_iota(jnp.int32, (BLOCK, 128), 1) < HEAD_DIM
        kc[pl.ds(0, BLOCK), :] = jnp.zeros((BLOCK, 128), BF16)
        vc[pl.ds(0, BLOCK), :] = jnp.zeros((BLOCK, 128), BF16)

        def prepare(i, carry):
            rows = pl.ds(pl.multiple_of(i * tn, tn), tn)
            qv = q_ref[rows, :]
            kv = k_ref[rows, :]
            rq = lax.rsqrt(_split_dot(qv * qv, bdv) * (1.0 / HEAD_DIM) + EPS)
            rk = lax.rsqrt(_split_dot(kv * kv, bdv) * (1.0 / HEAD_DIM) + EPS)
            rq_s[rows, :] = rq
            rk_s[rows, :] = rk
            qn[rows, :] = (qv * rq) * (qg_ref[...] * SCALE)
            kn[rows, :] = (kv * rk) * kg_ref[...]
            silu, dsilu = _silu_parts(g_ref[rows, :])
            dy = dy_ref[rows, :]
            at = att_ref[rows, :]
            do = dy * silu
            do_s[rows, :] = do
            dd_s[rows, :] = _split_dot(do * at, bdv)
            dg_ref[rows, :] = (dy * at * dsilu).astype(BF16)
            dka[rows, :] = jnp.zeros((tn, 128), F32)
            dva[rows, :] = jnp.zeros((tn, 128), F32)
            return carry
        lax.fori_loop(0, S // tn, prepare, 0)

        kt = lax.broadcasted_iota(jnp.int32, (2 * BLOCK, 2 * BLOCK), 0)
        qt = lax.broadcasted_iota(jnp.int32, (2 * BLOCK, 2 * BLOCK), 1) % BLOCK
        band_mask_t = ((kt < BLOCK) & (kt >= qt)) | ((kt >= BLOCK) & ((kt - BLOCK) <= qt))

        def per_query_row(t):
            tt = t.T
            return jnp.concatenate([tt[0:1, :], tt[HEAD_DIM:HEAD_DIM + 1, :]], axis=1)

        def fill(blk, d):
            tokens = _block_tokens(blk, d, S)
            kc[_padded_block(blk), :] = kn[tokens, :].astype(BF16)
            vc[_padded_block(blk), :] = v_ref[tokens, :].astype(BF16)

        def block(blk, d):
            tokens = _block_tokens(blk, d, S)
            keys = pl.ds(pl.multiple_of(blk * BLOCK, BLOCK), 2 * BLOCK)
            first = (blk & (S // d // BLOCK - 1)) == 0
            q2 = _two_heads(qn[tokens, :].astype(BF16), lo)
            do2 = _two_heads(do_s[tokens, :].astype(BF16), lo)
            lse_row = per_query_row(lse_ref[tokens, :])
            dd_row = per_query_row(dd_s[tokens, :])
            kb = kc[keys, :]
            vb = vc[keys, :]
            st = jnp.where(band_mask_t, _nt(kb, q2), NEG)
            st = jnp.concatenate([st[:BLOCK] + jnp.where(first, NEG, 0.0), st[BLOCK:]], axis=0)
            pt = jnp.exp(st - lse_row)
            dst = pt * (_nt(vb, do2) - dd_row)
            ptb = pt.astype(BF16)
            dstb = dst.astype(BF16)
            dv_band = jnp.dot(ptb, do2, preferred_element_type=F32)
            dk_band = jnp.dot(dstb, q2, preferred_element_type=F32)
            before = _block_tokens(jnp.where(first, blk, blk - 1), d, S)
            dka[before, :] = dka[before, :] + dk_band[:BLOCK]
            dva[before, :] = dva[before, :] + dv_band[:BLOCK]
            dka[tokens, :] = dka[tokens, :] + dk_band[BLOCK:]
            dva[tokens, :] = dva[tokens, :] + dv_band[BLOCK:]
            dq2 = _tn(dstb, kb)
            dq = jnp.where(lo, dq2[:BLOCK], dq2[BLOCK:])
            dqa[tokens, :] = dq if d == CONFIG_ORDER[0] else dqa[tokens, :] + dq

        for d in CONFIG_ORDER:
            _for_blocks(S // BLOCK, 4, functools.partial(fill, d=d))
            _for_blocks(S // BLOCK, 8, functools.partial(block, d=d))

        def out_step(i, carry):
            dqg, dkg = carry
            rows = pl.ds(pl.multiple_of(i * tn, tn), tn)
            rq = rq_s[rows, :]
            rk = rk_s[rows, :]
            qh = q_ref[rows, :] * rq
            kh = k_ref[rows, :] * rk
            dqs = dqa[rows, :] * SCALE
            dkn = dka[rows, :]
            aq = dqs * qg_ref[...]
            ak = dkn * kg_ref[...]
            dq_ref[rows, :] = (rq * (aq - qh * (_split_dot(aq * qh, bdv) * (1.0 / HEAD_DIM)))).astype(BF16)
            dk_ref[rows, :] = (rk * (ak - kh * (_split_dot(ak * kh, bdv) * (1.0 / HEAD_DIM)))).astype(BF16)
            dv_ref[rows, :] = dva[rows, :].astype(BF16)
            dqg = dqg + jnp.sum(dqs * qh, axis=0, keepdims=True)
            dkg = dkg + jnp.sum(dkn * kh, axis=0, keepdims=True)
            return dqg, dkg
        zero = jnp.zeros((1, 128), F32)
        dqg, dkg = lax.fori_loop(0, S // tn, out_step, (zero, zero))
        dqg_ref[0] = dqg
        dkg_ref[0] = dkg

    col = lambda j0: pl.BlockSpec((S, 128), lambda p, j0=j0: (0, j0 + p))
    col1 = lambda j0: pl.BlockSpec((S, 128), lambda p, j0=j0: (0, j0 + p), pipeline_mode=pl.Buffered(1))
    const = lambda shape: pl.BlockSpec(shape, lambda p: (0,) * len(shape))
    out = pl.BlockSpec((S, 128), lambda p: (0, p))
    gain_out = pl.BlockSpec((1, 1, 128), lambda p: (p, 0, 0))
    piece = pltpu.HBM((S, ATTN_WIDTH), BF16)
    gains = pltpu.HBM((npairs, 1, 128), F32)
    f32buf = pltpu.VMEM((S, 128), F32)
    bf16pad = pltpu.VMEM((S + BLOCK, 128), BF16)
    return pl.pallas_call(
        body, name="attn_bwd", grid=(npairs,),
        in_specs=[col(COL_AQ), col(COL_AK), col(COL_AV), col1(COL_AG), col1(GMLP_WIDTH // 128), col1(0), col(0),
                  const((1, 128)), const((1, 128)), const((128, 128))],
        out_specs=[out, out, out, out, gain_out, gain_out],
        out_shape=[piece, piece, piece, piece, gains, gains],
        scratch_shapes=[f32buf, f32buf, f32buf, f32buf, bf16pad, bf16pad, f32buf, f32buf, f32buf, f32buf, f32buf],
        compiler_params=_params(60, ("arbitrary",)),
    )(*_hbm(proj, proj, proj, proj, dycat, att, lse, qg2, kg2, bd))


def _mem_kv(mem, gain, wkv_bf, kg4, bd):
    def body(mem_ref, g_ref, w_ref, kg_ref, bd_ref, hm_ref, kraw_ref, mk_ref, mv_ref):
        mv_ = mem_ref[...]
        r = lax.rsqrt(jnp.mean(mv_ * mv_, axis=-1, keepdims=True) + EPS)
        hm = ((mv_ * r) * g_ref[...]).astype(BF16)
        hm_ref[...] = hm
        kv = jnp.dot(hm, w_ref[...], preferred_element_type=F32)
        kraw = kv[:, :MEM_WIDTH]
        kraw_ref[...] = kraw
        ms = _split_dot(kraw * kraw, bd_ref[...]) * (1.0 / HEAD_DIM)
        mk_ref[...] = (kraw * lax.rsqrt(ms + EPS)) * kg_ref[...]
        mv_ref[...] = kv[:, MEM_WIDTH:]

    sq = jax.ShapeDtypeStruct((MEM_LEN, MEM_WIDTH), F32)
    return pl.pallas_call(
        body, name="mem_kv",
        out_shape=[jax.ShapeDtypeStruct((MEM_LEN, D_MODEL), BF16), sq, sq, sq],
        compiler_params=_params(16),
    )(mem, gain, wkv_bf, kg4, bd)


def _mem_fwd(proj, mk, mv, qg4, bd):
    S = proj.shape[0]
    tm = 512

    def body(q_ref, g_ref, mk_ref, mv_ref, qg_ref, bd_ref, y_ref, om_ref):
        qv = q_ref[...]
        ms = _split_dot(qv * qv, bd_ref[...]) * (1.0 / HEAD_DIM)
        qs = (qv * lax.rsqrt(ms + EPS)) * (qg_ref[...] * SCALE)
        mkb = mk_ref[...].astype(BF16)
        mvb = mv_ref[...].astype(BF16)
        head = _head_index((tm, MEM_WIDTH))
        o = jnp.zeros((tm, MEM_WIDTH), F32)
        for h in range(4):
            s = _nt(jnp.where(head == h, qs, 0.0).astype(BF16), mkb)
            e = jnp.exp(s - jnp.max(s, axis=-1, keepdims=True))
            p = e * (1.0 / jnp.sum(e, axis=-1, keepdims=True))
            o = jnp.where(head == h, jnp.dot(p.astype(BF16), mvb, preferred_element_type=F32), o)
        om_ref[...] = o
        silu, _ = _silu_parts(g_ref[...])
        y_ref[...] = (o * silu).astype(BF16)

    col = lambda j: pl.BlockSpec((tm, MEM_WIDTH), lambda i, j=j: (i, j))
    const = lambda shape: pl.BlockSpec(shape, lambda i: (0,) * len(shape))
    tile = pl.BlockSpec((tm, MEM_WIDTH), lambda i: (i, 0))
    return pl.pallas_call(
        body, name="mem_fwd", grid=(S // tm,),
        in_specs=[col(11), col(12), const((MEM_LEN, MEM_WIDTH)), const((MEM_LEN, MEM_WIDTH)), const((1, MEM_WIDTH)),
                  const((MEM_WIDTH, MEM_WIDTH))],
        out_specs=[tile, tile],
        out_shape=[pltpu.HBM((S, MEM_WIDTH), BF16), pltpu.HBM((S, MEM_WIDTH), F32)],
        compiler_params=_params(24, ("arbitrary",)),
    )(*_hbm(proj, proj, mk, mv, qg4, bd))


def _mem_bwd(proj, dycat, om, mk, mv, qg4, bd):
    S = proj.shape[0]
    tm = 512

    def body(q_ref, g_ref, dy_ref, om_ref, mk_ref, mv_ref, qg_ref, bd_ref,
             dq_ref, dg_ref, dmk_ref, dmv_ref, dqg_ref):
        i = pl.program_id(0)

        @pl.when(i == 0)
        def _():
            dmk_ref[...] = jnp.zeros_like(dmk_ref)
            dmv_ref[...] = jnp.zeros_like(dmv_ref)
            dqg_ref[...] = jnp.zeros_like(dqg_ref)

        bdv = bd_ref[...]
        qv = q_ref[...]
        rq = lax.rsqrt(_split_dot(qv * qv, bdv) * (1.0 / HEAD_DIM) + EPS)
        qh = qv * rq
        qs = qh * (qg_ref[...] * SCALE)
        silu, dsilu = _silu_parts(g_ref[...])
        dy = dy_ref[...]
        o = om_ref[...]
        do = dy * silu
        dg_ref[...] = (dy * o * dsilu).astype(BF16)
        dd = _split_dot(do * o, bdv)
        mkb = mk_ref[...].astype(BF16)
        mvb = mv_ref[...].astype(BF16)
        head = _head_index((tm, MEM_WIDTH))
        dqs = jnp.zeros((tm, MEM_WIDTH), F32)
        for h in range(4):
            qhd = jnp.where(head == h, qs, 0.0).astype(BF16)
            doh = jnp.where(head == h, do, 0.0).astype(BF16)
            s = _nt(qhd, mkb)
            e = jnp.exp(s - jnp.max(s, axis=-1, keepdims=True))
            p = e * (1.0 / jnp.sum(e, axis=-1, keepdims=True))
            ds = p * (_nt(doh, mvb) - dd[:, h * HEAD_DIM:h * HEAD_DIM + 1])
            dsb = ds.astype(BF16)
            dmv_ref[...] += _tn(p.astype(BF16), doh)
            dmk_ref[...] += _tn(dsb, qhd)
            dqs = jnp.where(head == h, jnp.dot(dsb, mkb, preferred_element_type=F32), dqs)
        dqs = dqs * SCALE
        a = dqs * qg_ref[...]
        dq_ref[...] = (rq * (a - qh * (_split_dot(a * qh, bdv) * (1.0 / HEAD_DIM)))).astype(BF16)
        dqg_ref[...] += jnp.sum(dqs * qh, axis=0, keepdims=True)

    col = lambda j: pl.BlockSpec((tm, MEM_WIDTH), lambda i, j=j: (i, j))
    const = lambda shape: pl.BlockSpec(shape, lambda i: (0,) * len(shape))
    tile = pl.BlockSpec((tm, MEM_WIDTH), lambda i: (i, 0))
    piece = pltpu.HBM((S, MEM_WIDTH), BF16)
    sq = pltpu.HBM((MEM_LEN, MEM_WIDTH), F32)
    return pl.pallas_call(
        body, name="mem_bwd", grid=(S // tm,),
        in_specs=[col(11), col(12), col(3), tile, const((MEM_LEN, MEM_WIDTH)), const((MEM_LEN, MEM_WIDTH)),
                  const((1, MEM_WIDTH)), const((MEM_WIDTH, MEM_WIDTH))],
        out_specs=[tile, tile, const((MEM_LEN, MEM_WIDTH)), const((MEM_LEN, MEM_WIDTH)), const((1, MEM_WIDTH))],
        out_shape=[piece, piece, sq, sq, pltpu.HBM((1, MEM_WIDTH), F32)],
        compiler_params=_params(32, ("arbitrary",)),
    )(*_hbm(proj, proj, dycat, om, mk, mv, qg4, bd))


def _mem_kv_bwd(dmk, dmv, kraw, mem, gain, kg4, wkv_bf, hm_bf, bd):
    def body(dmk_ref, dmv_ref, kraw_ref, mem_ref, g_ref, kg_ref, w_ref, hm_ref, bd_ref, dw_ref, dg_ref, dkg_ref):
        bdv = bd_ref[...]
        kraw = kraw_ref[...]
        rk = lax.rsqrt(_split_dot(kraw * kraw, bdv) * (1.0 / HEAD_DIM) + EPS)
        kh = kraw * rk
        dmkv = dmk_ref[...]
        a = dmkv * kg_ref[...]
        dkraw = rk * (a - kh * (_split_dot(a * kh, bdv) * (1.0 / HEAD_DIM)))
        dkg_ref[...] = jnp.sum(dmkv * kh, axis=0, keepdims=True)
        dkv = jnp.concatenate([dkraw, dmv_ref[...]], axis=1).astype(BF16)
        dw = _tn(hm_ref[...], dkv).astype(BF16)
        rows_blk = D_MODEL // N_DEV
        for j in range(N_DEV):
            dw_ref[j] = dw[rows_blk * j:rows_blk * (j + 1)]
        dhm = _nt(dkv, w_ref[...])
        mv_ = mem_ref[...]
        r = lax.rsqrt(jnp.mean(mv_ * mv_, axis=-1, keepdims=True) + EPS)
        dg_ref[...] = jnp.sum(dhm * (mv_ * r), axis=0, keepdims=True)

    return pl.pallas_call(
        body, name="mem_kv_bwd",
        out_shape=[jax.ShapeDtypeStruct((N_DEV, D_MODEL // N_DEV, 2 * MEM_WIDTH), BF16),
                   jax.ShapeDtypeStruct((1, D_MODEL), F32), jax.ShapeDtypeStruct((1, MEM_WIDTH), F32)],
        compiler_params=_params(24),
    )(dmk, dmv, kraw, mem, gain, kg4, wkv_bf, hm_bf, bd)


def _out_loss(yg, ya, ym, x, tgt, wout_bf):
    S, D = x.shape
    tm = 512
    nsteps = S // tm
    rows_blk = D // N_DEV

    def body(yg_ref, ya_ref, ym_ref, x_ref, t_ref, w_ref, dout_ref, dycat_ref, dw_ref, loss_ref, acc_ref):
        i = pl.program_id(0)

        @pl.when(i == 0)
        def _():
            acc_ref[...] = jnp.zeros_like(acc_ref)
            loss_ref[...] = jnp.zeros_like(loss_ref)

        ycat = jnp.concatenate([yg_ref[...], ya_ref[...], ym_ref[...]], axis=1)
        w = w_ref[...]
        diff = (x_ref[...] + jnp.dot(ycat, w, preferred_element_type=F32)) - t_ref[...]
        loss_ref[...] += jnp.sum(diff * diff, axis=0, keepdims=True)
        dout = diff * (1.0 / D)
        dout_ref[...] = dout
        db = dout.astype(BF16)
        dycat_ref[...] = _nt(db, w)
        acc_ref[...] += _tn(ycat, db)

        @pl.when(i == nsteps - 1)
        def _():
            for j in range(N_DEV):
                dw_ref[j] = acc_ref[rows_blk * j:rows_blk * (j + 1), :].astype(BF16)

    tile = lambda w: pl.BlockSpec((tm, w), lambda i: (i, 0))
    const = lambda shape: pl.BlockSpec(shape, lambda i: (0,) * len(shape))
    return pl.pallas_call(
        body, name="out_loss", grid=(nsteps,),
        in_specs=[tile(GMLP_WIDTH), tile(ATTN_WIDTH), tile(MEM_WIDTH), tile(D), tile(D), const((D, D))],
        out_specs=[tile(D), tile(D), const((N_DEV, rows_blk, D)), const((1, D))],
        out_shape=[pltpu.HBM((S, D), F32), pltpu.HBM((S, D), F32),
                   pltpu.HBM((N_DEV, rows_blk, D), BF16), pltpu.HBM((1, D), F32)],
        scratch_shapes=[pltpu.VMEM((D, D), F32)],
        compiler_params=_params(40, ("arbitrary",)),
    )(*_hbm(yg, ya, ym, x, tgt, wout_bf))


def _piece_specs(pieces, tm):
    return [pl.BlockSpec((tm, p.shape[1]), lambda i: (i, 0)) for p in pieces]


def _in_bwd_dx(pieces, x, dout, gain, w_t, dw_blocks):
    S, D = x.shape
    N = w_t.shape[0]
    tm = 256
    n = len(pieces)
    nsteps = S // tm
    middle_step = nsteps // 8

    def body(*refs):
        piece_refs = refs[:n]
        x_ref, dout_ref, g_ref, w_ref, dwb_ref, gx_ref, dg_ref, gw_ref = refs[n:n + 8]
        rs = _ReduceScatter([dwb_ref], [gw_ref], *refs[n + 8:])
        i = pl.program_id(0)

        @pl.when(i == 0)
        def _():
            dg_ref[...] = jnp.zeros_like(dg_ref)
            rs.start()

        @pl.when(i == middle_step)
        def _():
            rs.middle()

        dproj = jnp.concatenate([r[...] for r in piece_refs], axis=1)
        dh = jnp.dot(dproj, w_ref[...], preferred_element_type=F32)
        xv = x_ref[...]
        r = lax.rsqrt(jnp.mean(xv * xv, axis=-1, keepdims=True) + EPS)
        xh = xv * r
        a = dh * g_ref[...]
        gx_ref[...] = dout_ref[...] + r * (a - xh * jnp.mean(a * xh, axis=-1, keepdims=True))
        dg_ref[...] += jnp.sum(dh * xh, axis=0, keepdims=True)

        @pl.when(i == nsteps - 1)
        def _():
            rs.finish()

    tile = pl.BlockSpec((tm, D), lambda i: (i, 0))
    const = lambda shape: pl.BlockSpec(shape, lambda i: (0,) * len(shape))
    vmem = pl.BlockSpec(memory_space=pltpu.VMEM)
    return pl.pallas_call(
        body, name="in_bwd_dx", grid=(nsteps,),
        in_specs=_piece_specs(pieces, tm)
        + [tile, tile, const((1, D)), pl.BlockSpec((N, D), lambda i: (0, 0), pipeline_mode=pl.Buffered(1)), vmem],
        out_specs=[tile, const((1, D)), vmem],
        out_shape=[pltpu.HBM((S, D), F32), pltpu.HBM((1, D), F32), jax.ShapeDtypeStruct(dw_blocks.shape[1:], F32)],
        scratch_shapes=_reduce_scatter_scratch([dw_blocks]),
        compiler_params=_params(56, ("arbitrary",)),
    )(*_hbm(*pieces, x, dout, gain, w_t), dw_blocks)


def _in_bwd_dw(pieces, h_bf, others):
    S, D = h_bf.shape
    N = sum(p.shape[1] for p in pieces)
    n_blk = N // N_DEV
    tm = 512
    n = len(pieces)
    k = len(others)
    nsteps = S // tm

    def body(*refs):
        piece_refs = refs[:n]
        h_ref = refs[n]
        other_refs = refs[n + 1:n + 1 + k]
        dw_ref = refs[n + 1 + k]
        sum_refs = refs[n + 2 + k:n + 2 + 2 * k]
        acc_ref = refs[n + 2 + 2 * k]
        rs = _ReduceScatter(other_refs, sum_refs, *refs[n + 3 + 2 * k:])
        i = pl.program_id(0)

        @pl.when(i == 0)
        def _():
            acc_ref[...] = jnp.zeros_like(acc_ref)
            rs.start()

        @pl.when(i == 1)
        def _():
            rs.middle()

        dproj = jnp.concatenate([r[...] for r in piece_refs], axis=1)
        acc_ref[...] += _tn(h_ref[...], dproj)

        @pl.when(i == nsteps - 1)
        def _():
            for j in range(N_DEV):
                dw_ref[j] = acc_ref[:, n_blk * j:n_blk * (j + 1)].T.astype(BF16)
            rs.finish()

    vmem = pl.BlockSpec(memory_space=pltpu.VMEM)
    return pl.pallas_call(
        body, name="in_bwd_dw", grid=(nsteps,),
        in_specs=_piece_specs(pieces, tm) + [pl.BlockSpec((tm, D), lambda i: (i, 0))] + [vmem] * k,
        out_specs=[pl.BlockSpec((N_DEV, n_blk, D), lambda i: (0, 0, 0))] + [vmem] * k,
        out_shape=[pltpu.HBM((N_DEV, n_blk, D), BF16)] + [jax.ShapeDtypeStruct(o.shape[1:], F32) for o in others],
        scratch_shapes=[pltpu.VMEM((D, N), F32)] + _reduce_scatter_scratch(others),
        compiler_params=_params(56, ("arbitrary",)),
    )(*_hbm(*pieces, h_bf), *others)


def _row_step(m):
    return max(t for t in range(16, 257, 16) if m % t == 0)


def _place():
    x, y, c = lax.axis_index("x"), lax.axis_index("y"), lax.axis_index("c")
    chips = [(1 - x, y), (x, 1 - y), (1 - x, 1 - y)]
    return x, y, c, chips


class _AllGather:
    def __init__(self, srcs, outs, send_sems, recv_sems, local_sems, first_sem=0):
        self.srcs, self.outs, self.n, self.first_sem = srcs, outs, len(srcs), first_sem
        self.send_sems, self.recv_sems, self.local_sems = send_sems, recv_sems, local_sems

    def _rows(self, a, px, py, pc):
        m = self.srcs[a].shape[0]
        return self.outs[a].at[pl.ds((4 * px + 2 * py + pc) * m, m), :]

    def _copy(self, a, k, block, to, src=None):
        row = self.first_sem + a
        return pltpu.make_async_remote_copy(
            src_ref=self._rows(a, *block) if src is None else src, dst_ref=self._rows(a, *block),
            send_sem=self.send_sems.at[row, k], recv_sem=self.recv_sems.at[row, k], device_id=to, device_id_type=MESH)

    def _mine(self):
        x, y, c, _ = _place()
        return [pltpu.make_async_copy(self.srcs[a], self._rows(a, x, y, c), self.local_sems.at[self.first_sem + a])
                for a in range(self.n)]

    def _first(self, far):
        x, y, c, chips = _place()
        out = []
        for a in range(self.n):
            if far:
                out.append(self._copy(a, 3, (x, y, c), (*chips[2], c), src=self.srcs[a]))
            else:
                out.append(self._copy(a, 0, (x, y, c), (x, y, 1 - c), src=self.srcs[a]))
                out += [self._copy(a, 1 + j, (x, y, c), (*chips[j], c), src=self.srcs[a]) for j in (1, 0)]
        return out

    def _passed(self, j):
        x, y, c, chips = _place()
        return [self._copy(a, 4 + j, (*chips[j], c), (x, y, 1 - c)) for a in range(self.n)]

    def start(self):
        for cp in self._mine() + self._first(far=False):
            cp.start()

    def start_far(self):
        for cp in self._first(far=True):
            cp.start()

    def from_chip(self, j):
        x, y, c, chips = _place()
        for a in range(self.n):
            self._copy(a, 1 + j, (*chips[j], c), (x, y, c)).wait_recv()
        for cp in self._passed(j):
            cp.start()

    def from_sibling(self, j=None):
        x, y, c, chips = _place()
        for a in range(self.n):
            block = (x, y, 1 - c) if j is None else (*chips[j], 1 - c)
            self._copy(a, 0 if j is None else 4 + j, block, (x, y, c)).wait_recv()

    def from_self(self):
        for cp in self._mine():
            cp.wait()

    def finish(self):
        for cp in (self._first(far=False) + self._first(far=True)
                   + self._passed(0) + self._passed(1) + self._passed(2)):
            cp.wait_send()

    def run(self):
        self.start()
        self.start_far()
        self.from_self()
        for j in range(3):
            self.from_chip(j)
        self.from_sibling()
        for j in range(3):
            self.from_sibling(j)
        self.finish()


def _gather_proj(x, gain, shards, xpos):
    S, D = x.shape
    n = len(shards)
    N = N_DEV * shards[0].shape[0]
    half = N // 2
    tm = 1024
    nsteps = S // tm

    def body(*refs):
        xpos_ref, x_ref, g_ref = refs[:3]
        ins = refs[3:3 + n]
        proj_ref, h_ref = refs[3 + n:5 + n]
        outs = refs[5 + n:5 + 2 * n]
        casts = refs[5 + 2 * n:5 + 3 * n]
        whole = refs[5 + 3 * n:5 + 4 * n]
        sems = refs[5 + 4 * n:8 + 4 * n]
        ag = _AllGather(casts[:1], whole[:1], *sems)
        later = _AllGather(casts[1:], whole[1:], *sems, first_sem=1)
        out_sems, h_all = refs[8 + 4 * n:]
        p, i = pl.program_id(0), pl.program_id(1)
        rows = pl.ds(pl.multiple_of(i * tm, tm), tm)

        @pl.when((p == 0) & (i == 0))
        def _():
            for a in range(n):
                tr = _row_step(ins[a].shape[0])

                def cast(r, carry, a=a, tr=tr):
                    at = pl.ds(pl.multiple_of(r * tr, tr), tr)
                    casts[a][at, :] = ins[a][at, :].astype(BF16)
                    return carry
                lax.fori_loop(0, ins[a].shape[0] // tr, cast, 0)
            ag.start()

        @pl.when(p == 0)
        def _():
            xv = x_ref[...]
            r = lax.rsqrt(jnp.mean(xv * xv, axis=-1, keepdims=True) + EPS)
            h = ((xv * r) * g_ref[...]).astype(BF16)
            h_ref[...] = h
            h_all[rows, :] = h

        @pl.when((p == 1) & (i == 0))
        def _():
            ag.from_self()
            ag.from_chip(1)
            ag.start_far()
            later.start()
            later.start_far()
            ag.from_sibling()
            ag.from_sibling(1)

        @pl.when((p == 2) & (i == 0))
        def _():
            for j in (0, 2):
                ag.from_chip(j)
            for j in (0, 2):
                ag.from_sibling(j)

        @pl.when(p > 0)
        def _():
            which = (xpos_ref[0] + p - 1) % 2
            w_half = whole[0][pl.ds(pl.multiple_of(which * half, half), half), :]
            proj_ref[...] = _nt(h_all[rows, :], w_half)

        @pl.when((p == 2) & (i == nsteps - 1))
        def _():
            ag.finish()
            later.from_self()
            for j in range(3):
                later.from_chip(j)
            later.from_sibling()
            for j in range(3):
                later.from_sibling(j)
            later.finish()
            to_results = [pltpu.make_async_copy(whole[a], outs[a], out_sems.at[a]) for a in range(n)]
            for cp in to_results:
                cp.start()
            for cp in to_results:
                cp.wait()

    vmem = pl.BlockSpec(memory_space=pltpu.VMEM)
    hbm = pl.BlockSpec(memory_space=pl.ANY)
    gathered = [(N_DEV * a.shape[0], a.shape[1]) for a in shards]
    x_tile = lambda p, i, xp: (jnp.where(p == 0, i, nsteps - 1), 0)
    proj_tile = lambda p, i, xp: (jnp.where(p == 0, 0, i), (xp[0] + jnp.maximum(p - 1, 0)) % 2)
    grid_spec = pltpu.PrefetchScalarGridSpec(
        num_scalar_prefetch=1, grid=(3, nsteps),
        in_specs=[pl.BlockSpec((tm, D), x_tile), pl.BlockSpec((1, D), lambda p, i, xp: (0, 0))] + [vmem] * n,
        out_specs=[pl.BlockSpec((tm, half), proj_tile), pl.BlockSpec((tm, D), x_tile)] + [hbm] * n,
        scratch_shapes=[pltpu.VMEM(a.shape, BF16) for a in shards] + [pltpu.VMEM(g, BF16) for g in gathered]
        + [pltpu.SemaphoreType.DMA((n, 7)), pltpu.SemaphoreType.DMA((n, 7)), pltpu.SemaphoreType.DMA((n,)),
           pltpu.SemaphoreType.DMA((n,)), pltpu.VMEM((S, D), BF16)])
    return pl.pallas_call(
        body, name="gather_proj", grid_spec=grid_spec,
        out_shape=[pltpu.HBM((S, N), F32), pltpu.HBM((S, D), BF16)] + [pltpu.HBM(g, BF16) for g in gathered],
        compiler_params=_params(56, ("arbitrary", "arbitrary")),
    )(xpos, *_hbm(x, gain), *shards)


ROW_NORM, ROW_MEM_NORM, ROW_V_GAIN, ROW_B, ROW_ATTN_GAINS, ROW_MEM_GAINS, ROW_W_S, ROW_LOSS = 0, 8, 16, 18, 22, 23, 24, 536
SMALL_ROWS = 544


def _gather_small(dgain, dmgain, dvg, db2, dqg, dkg, dmqg, dmkg, dws, sq):
    def body(dgain_ref, dmgain_ref, dvg_ref, db2_ref, dqg_ref, dkg_ref, dmqg_ref, dmkg_ref, dws_ref, sq_ref,
             out_ref, mine, send_sems, recv_sems, local_sems):
        first = lax.broadcasted_iota(jnp.int32, (1, 128), 1) < HEAD_DIM
        for i in range(8):
            cols = slice(128 * i, 128 * (i + 1))
            mine[ROW_NORM + i:ROW_NORM + i + 1, :] = dgain_ref[:, cols]
            mine[ROW_MEM_NORM + i:ROW_MEM_NORM + i + 1, :] = dmgain_ref[:, cols]
            mine[ROW_LOSS + i:ROW_LOSS + i + 1, :] = sq_ref[:, cols]
        mine[ROW_V_GAIN:ROW_V_GAIN + 1, :] = dvg_ref[:, 0:128]
        mine[ROW_V_GAIN + 1:ROW_V_GAIN + 2, :] = dvg_ref[:, 128:256]
        bt = db2_ref[...].T
        for h in range(4):
            mine[ROW_B + h:ROW_B + h + 1, :] = bt[HEAD_DIM * h:HEAD_DIM * h + 1, :]

        def fold_heads(t):
            return t + pltpu.roll(t, HEAD_DIM, axis=1)
        aq = fold_heads(dqg_ref[0] + dqg_ref[1] + dqg_ref[2] + dqg_ref[3])
        ak = fold_heads(dkg_ref[0] + dkg_ref[1] + dkg_ref[2] + dkg_ref[3])
        mine[ROW_ATTN_GAINS:ROW_ATTN_GAINS + 1, :] = jnp.where(first, aq, ak)
        mq = fold_heads(dmqg_ref[:, 0:128] + dmqg_ref[:, 128:256])
        mk = fold_heads(dmkg_ref[:, 0:128] + dmkg_ref[:, 128:256])
        mine[ROW_MEM_GAINS:ROW_MEM_GAINS + 1, :] = jnp.where(first, mq, mk)
        mine[ROW_W_S:ROW_W_S + 4 * CHUNK, :] = dws_ref[...]
        _AllGather([mine], [out_ref], send_sems, recv_sems, local_sems).run()

    return pl.pallas_call(
        body, name="gather_small_grads",
        out_shape=jax.ShapeDtypeStruct((N_DEV * SMALL_ROWS, 128), F32),
        scratch_shapes=[pltpu.VMEM((SMALL_ROWS, 128), F32), pltpu.SemaphoreType.DMA((1, 7)),
                        pltpu.SemaphoreType.DMA((1, 7)), pltpu.SemaphoreType.DMA((1,))],
        compiler_params=_params(16),
    )(dgain, dmgain, dvg, db2, dqg, dkg, dmqg, dmkg, dws, sq)


def _reduce_scatter_scratch(arrs):
    n = len(arrs)
    return ([pltpu.VMEM((4,) + a.shape[1:], BF16) for a in arrs] + [pltpu.VMEM((3,) + a.shape[1:], BF16) for a in arrs]
            + [pltpu.SemaphoreType.DMA((n, 7)), pltpu.SemaphoreType.DMA((n, 7))])


class _ReduceScatter:
    def __init__(self, ins, outs, *scratch):
        n = len(ins)
        self.n, self.ins, self.outs = n, ins, outs
        self.half, self.quarter = scratch[:n], scratch[n:2 * n]
        self.send_sems, self.recv_sems = scratch[2 * n:]

    def _to_sibling(self):
        x, y, c, _ = _place()
        return [pltpu.make_async_remote_copy(
            src_ref=self.ins[a].at[2 * q + (1 - c)], dst_ref=self.half[a].at[q], send_sem=self.send_sems.at[a, q],
            recv_sem=self.recv_sems.at[a, q], device_id=(x, y, 1 - c), device_id_type=MESH)
            for a in range(self.n) for q in range(4)]

    def _to_chips(self):
        _, _, c, chips = _place()
        return [pltpu.make_async_remote_copy(
            src_ref=self.half[a].at[2 * chip[0] + chip[1]], dst_ref=self.quarter[a].at[k],
            send_sem=self.send_sems.at[a, 4 + k], recv_sem=self.recv_sems.at[a, 4 + k], device_id=(*chip, c),
            device_id_type=MESH) for a in range(self.n) for k, chip in enumerate(chips)]

    def _rows(self, a, fn):
        m = self.ins[a].shape[1]
        tr = _row_step(m)

        def step(i, carry):
            fn(pl.ds(pl.multiple_of(i * tr, tr), tr))
            return carry
        lax.fori_loop(0, m // tr, step, 0)

    def start(self):
        for cp in self._to_sibling():
            cp.start()

    def middle(self):
        _, _, c, _ = _place()
        for cp in self._to_sibling():
            cp.wait_recv()
        for a in range(self.n):
            for q in range(4):
                def add_half(rows, a=a, q=q):
                    both = self.ins[a][2 * q + c, rows, :].astype(F32) + self.half[a][q, rows, :].astype(F32)
                    self.half[a][q, rows, :] = both.astype(BF16)
                self._rows(a, add_half)
        for cp in self._to_chips():
            cp.start()

    def finish(self):
        x, y, _, _ = _place()
        for cp in self._to_chips():
            cp.wait_recv()
        for a in range(self.n):
            def add_quarters(rows, a=a):
                f = lambda t: t.astype(F32)
                self.outs[a][rows, :] = ((f(self.half[a][2 * x + y, rows, :]) + f(self.quarter[a][0, rows, :]))
                                         + (f(self.quarter[a][1, rows, :]) + f(self.quarter[a][2, rows, :])))
            self._rows(a, add_quarters)
        for cp in self._to_sibling() + self._to_chips():
            cp.wait_send()


def _adamw_math(w, g, m, v):
    m = ADAM_B1 * m + (1.0 - ADAM_B1) * g
    v = ADAM_B2 * v + (1.0 - ADAM_B2) * (g * g)
    m_hat = m / (1.0 - ADAM_B1 ** ADAM_STEP)
    v_hat = v / (1.0 - ADAM_B2 ** ADAM_STEP)
    delta = -ADAM_LR * (m_hat / (jnp.sqrt(v_hat) + ADAM_EPS) + ADAM_WD * w)
    return delta, m, v


def _adamw(w, g, m, v, name):
    R, C = w.shape
    tr = _row_step(R)

    def body(w_ref, g_ref, m_ref, v_ref, d_ref, nm_ref, nv_ref):
        d_ref[...], nm_ref[...], nv_ref[...] = _adamw_math(w_ref[...], g_ref[...], m_ref[...], v_ref[...])

    tile = pl.BlockSpec((tr, C), lambda i: (i, 0))
    out = pltpu.HBM((R, C), F32)
    return pl.pallas_call(
        body, name=name, grid=(R // tr,), in_specs=[tile] * 4, out_specs=[tile] * 3, out_shape=[out] * 3,
        compiler_params=_params(16, ("arbitrary",)),
    )(*_hbm(w, g, m, v))


SMALL = ("norm_gain", "gmlp_v_gain", "gmlp_w_s", "gmlp_b", "attn_q_gain", "attn_k_gain", "mem_norm_gain",
         "mem_q_gain", "mem_k_gain")
WEIGHTS = ("norm_gain", "w_in", "gmlp_v_gain", "gmlp_w_s", "gmlp_b", "attn_q_gain", "attn_k_gain",
           "mem_norm_gain", "w_mem_kv", "mem_q_gain", "mem_k_gain", "w_out")


def _adamw_small(w, m, v, g_all):
    k = len(SMALL)
    half = slice(0, HEAD_DIM), slice(HEAD_DIM, 2 * HEAD_DIM)

    def body(*refs):
        w_refs, m_refs, v_refs = refs[:k], refs[k:2 * k], refs[2 * k:3 * k]
        g_ref = refs[3 * k]
        outs = refs[3 * k + 1:7 * k + 1]
        loss_ref, gsum = refs[7 * k + 1:]

        part = SMALL_ROWS // 4
        for p in range(4):
            acc = g_ref[part * p:part * (p + 1), :]
            for dev in range(1, N_DEV):
                acc = acc + g_ref[dev * SMALL_ROWS + part * p:dev * SMALL_ROWS + part * (p + 1), :]
            gsum[part * p:part * (p + 1), :] = acc

        def update(name, at, g):
            i = SMALL.index(name)
            d, nm, nv = _adamw_math(w_refs[i][at], g, m_refs[i][at], v_refs[i][at])
            outs[i][at], outs[k + i][at], outs[2 * k + i][at], outs[3 * k + i][at] = g, d, nm, nv

        for i in range(8):
            at = (slice(0, 1), slice(128 * i, 128 * (i + 1)))
            update("norm_gain", at, gsum[ROW_NORM + i:ROW_NORM + i + 1, :])
            update("mem_norm_gain", at, gsum[ROW_MEM_NORM + i:ROW_MEM_NORM + i + 1, :])
        for h in range(4):
            row = (0, slice(h, h + 1), slice(None))
            update("gmlp_v_gain", row, gsum[ROW_V_GAIN + h // 2:ROW_V_GAIN + h // 2 + 1, half[h % 2]])
            update("gmlp_b", row, gsum[ROW_B + h:ROW_B + h + 1, :])
            update("gmlp_w_s", (0, h), gsum[ROW_W_S + CHUNK * h:ROW_W_S + CHUNK * (h + 1), :])
        whole = (slice(0, 1), slice(None))
        update("attn_q_gain", whole, gsum[ROW_ATTN_GAINS:ROW_ATTN_GAINS + 1, half[0]])
        update("attn_k_gain", whole, gsum[ROW_ATTN_GAINS:ROW_ATTN_GAINS + 1, half[1]])
        update("mem_q_gain", whole, gsum[ROW_MEM_GAINS:ROW_MEM_GAINS + 1, half[0]])
        update("mem_k_gain", whole, gsum[ROW_MEM_GAINS:ROW_MEM_GAINS + 1, half[1]])
        loss_ref[...] = jnp.sum(gsum[ROW_LOSS:ROW_LOSS + 8, :], keepdims=True) * (0.5 / D_MODEL)

    shapes = [jax.ShapeDtypeStruct(w[name].shape, F32) for name in SMALL]
    res = pl.pallas_call(
        body, name="adamw_small",
        out_shape=shapes * 4 + [jax.ShapeDtypeStruct((1, 1), F32)],
        scratch_shapes=[pltpu.VMEM((SMALL_ROWS, 128), F32)],
        compiler_params=_params(16),
    )(*[w[n] for n in SMALL], *[m[n] for n in SMALL], *[v[n] for n in SMALL], g_all)
    trees = [dict(zip(SMALL, res[j * k:(j + 1) * k])) for j in range(4)]
    return (*trees, res[4 * k])


def _grads(x, mem, tgt, w, shards):
    bd128, bd256 = _head_blockdiag(128), _head_blockdiag(256)
    gain = w["norm_gain"].reshape(1, D_MODEL)
    vg = w["gmlp_v_gain"].reshape(1, GMLP_WIDTH)
    w_s = w["gmlp_w_s"].reshape(4, CHUNK, CHUNK)
    b2 = jnp.repeat(w["gmlp_b"].reshape(4, CHUNK).T, HEAD_DIM, axis=1)
    qg2 = jnp.tile(w["attn_q_gain"].reshape(1, HEAD_DIM), (1, 2))
    kg2 = jnp.tile(w["attn_k_gain"].reshape(1, HEAD_DIM), (1, 2))
    mqg4 = jnp.tile(w["mem_q_gain"].reshape(1, HEAD_DIM), (1, 4))
    mkg4 = jnp.tile(w["mem_k_gain"].reshape(1, HEAD_DIM), (1, 4))
    mgain = w["mem_norm_gain"].reshape(1, D_MODEL)

    xpos = lax.axis_index("x").astype(jnp.int32).reshape(1)
    proj, h_bf, win_t, wkv_bf, wout_bf = _gather_proj(x, gain, shards, xpos)
    yg = _gmlp_fwd(proj, vg, w_s, b2, bd256)
    ya, att, lse = _attn_fwd(proj, qg2, kg2, bd128)
    hm_bf, kraw, mk, mv = _mem_kv(mem, mgain, wkv_bf, mkg4, bd256)
    ym, om = _mem_fwd(proj, mk, mv, mqg4, bd256)
    dout, dycat, dwout, sq = _out_loss(yg, ya, ym, x, tgt, wout_bf)

    du, dgv, dgg, dws, db2, dvg = _gmlp_bwd(proj, dycat, vg, w_s, b2, bd256)
    dq, dk, dv, dag, dqg, dkg = _attn_bwd(proj, dycat, att, lse, qg2, kg2, bd128)
    dmq, dmg, dmk, dmv, dmqg = _mem_bwd(proj, dycat, om, mk, mv, mqg4, bd256)
    dwkv, dmgain, dmkg = _mem_kv_bwd(dmk, dmv, kraw, mem, mgain, mkg4, wkv_bf, hm_bf, bd256)
    pieces = [du, dgv, dgg, dq, dk, dv, dag, dmq, dmg]
    dwin, g_wkv, g_wout = _in_bwd_dw(pieces, h_bf, [dwkv, dwout])
    grad_x, dgain, g_win = _in_bwd_dx(pieces, x, dout, gain, win_t, dwin)
    return grad_x, g_win, g_wkv, g_wout, (dgain, dmgain, dvg, db2, dqg, dkg, dmqg, dmkg, dws, sq)


def kernel(x, mem, norm_gain, w_in, gmlp_v_gain, gmlp_w_s, gmlp_b, attn_q_gain, attn_k_gain, mem_norm_gain, w_mem_kv, mem_q_gain, mem_k_gain, w_out, loss_target, m_norm_gain, m_w_in, m_gmlp_v_gain, m_gmlp_w_s, m_gmlp_b, m_attn_q_gain, m_attn_k_gain, m_mem_norm_gain, m_w_mem_kv, m_mem_q_gain, m_mem_k_gain, m_w_out, v_norm_gain, v_w_in, v_gmlp_v_gain, v_gmlp_w_s, v_gmlp_b, v_attn_q_gain, v_attn_k_gain, v_mem_norm_gain, v_w_mem_kv, v_mem_q_gain, v_mem_k_gain, v_w_out):
    w = dict(norm_gain=norm_gain, w_in=w_in, gmlp_v_gain=gmlp_v_gain, gmlp_w_s=gmlp_w_s, gmlp_b=gmlp_b,
             attn_q_gain=attn_q_gain, attn_k_gain=attn_k_gain, mem_norm_gain=mem_norm_gain, w_mem_kv=w_mem_kv,
             mem_q_gain=mem_q_gain, mem_k_gain=mem_k_gain, w_out=w_out)
    m = dict(norm_gain=m_norm_gain, w_in=m_w_in, gmlp_v_gain=m_gmlp_v_gain, gmlp_w_s=m_gmlp_w_s, gmlp_b=m_gmlp_b,
             attn_q_gain=m_attn_q_gain, attn_k_gain=m_attn_k_gain, mem_norm_gain=m_mem_norm_gain,
             w_mem_kv=m_w_mem_kv, mem_q_gain=m_mem_q_gain, mem_k_gain=m_mem_k_gain, w_out=m_w_out)
    v = dict(norm_gain=v_norm_gain, w_in=v_w_in, gmlp_v_gain=v_gmlp_v_gain, gmlp_w_s=v_gmlp_w_s, gmlp_b=v_gmlp_b,
             attn_q_gain=v_attn_q_gain, attn_k_gain=v_attn_k_gain, mem_norm_gain=v_mem_norm_gain,
             w_mem_kv=v_w_mem_kv, mem_q_gain=v_mem_q_gain, mem_k_gain=v_mem_k_gain, w_out=v_w_out)
    transposed = lambda t: jnp.transpose(t[0])

    grad_x, g_win, g_wkv, g_wout, small = _grads(
        x[0], mem[0], loss_target[0], w, [transposed(w_in), w_mem_kv[0], w_out[0]])
    small_all = _gather_small(*small)

    out_g, out_d, out_m, out_v, loss = _adamw_small(w, m, v, small_all)
    d_, m_, v_ = _adamw(transposed(w_in), g_win, transposed(m_w_in), transposed(v_w_in), "adamw_w_in")
    for tree, t in ((out_g, g_win), (out_d, d_), (out_m, m_), (out_v, v_)):
        tree["w_in"] = jnp.transpose(t)[None]
    for name, g in (("w_mem_kv", g_wkv), ("w_out", g_wout)):
        d_, m_, v_ = _adamw(w[name][0], g, m[name][0], v[name][0], "adamw_" + name)
        out_g[name], out_d[name], out_m[name], out_v[name] = g[None], d_[None], m_[None], v_[None]

    return (loss.reshape(()), grad_x[None], *[out_g[k] for k in WEIGHTS], *[out_d[k] for k in WEIGHTS],
            *[out_m[k] for k in WEIGHTS], *[out_v[k] for k in WEIGHTS])
```

```python
import functools
import math

import jax
import jax.numpy as jnp
from jax import lax
from jax.experimental import pallas as pl
from jax.experimental.pallas import tpu as pltpu

F32 = jnp.float32
BF16 = jnp.bfloat16

N_DEV = 8
D_MODEL = 1024
HEAD_DIM = 64
GMLP_WIDTH = 256
ATTN_WIDTH = 512
MEM_WIDTH = 256
MEM_LEN = 256
CHUNK = 128
BLOCK = 128
DILATIONS = (1, 4, 16)
CONFIG_ORDER = tuple(reversed(DILATIONS))
EPS = 1e-6
SCALE = 1.0 / math.sqrt(HEAD_DIM)
NEG = -1e30

ADAM_LR = 0.001
ADAM_B1 = 0.9
ADAM_B2 = 0.999
ADAM_EPS = 1e-08
ADAM_WD = 0.01
ADAM_STEP = 10

MIB = 1024 * 1024
MESH = pl.DeviceIdType.MESH

COL_AQ, COL_AK, COL_AV, COL_AG = 6, 10, 14, 18


def _params(vmem_mib, semantics=None):
    kw = dict(vmem_limit_bytes=vmem_mib * MIB)
    if semantics is not None:
        kw["dimension_semantics"] = semantics
    return pltpu.CompilerParams(**kw)


def _hbm(*arrs):
    return [pltpu.with_memory_space_constraint(a, pltpu.HBM) for a in arrs]


def _split_dot(x, sel_bf):
    hi = x.astype(BF16)
    lo = (x - hi.astype(F32)).astype(BF16)
    return jnp.dot(hi, sel_bf, preferred_element_type=F32) + jnp.dot(lo, sel_bf, preferred_element_type=F32)


def _nt(a, b):
    return lax.dot_general(a, b, (((1,), (1,)), ((), ())), preferred_element_type=F32)


def _tn(a, b):
    return lax.dot_general(a, b, (((0,), (0,)), ((), ())), preferred_element_type=F32)


def _silu_parts(g):
    sg = jax.nn.sigmoid(g)
    return g * sg, sg * (1.0 + g * (1.0 - sg))


def _head_index(shape):
    return lax.shift_right_logical(lax.broadcasted_iota(jnp.int32, shape, 1), HEAD_DIM.bit_length() - 1)


def _head_blockdiag(width):
    i = jnp.arange(width) // HEAD_DIM
    return (i[:, None] == i[None, :]).astype(BF16)


def _gmlp_masked_weights(ws_ref, transpose):
    t = lax.broadcasted_iota(jnp.int32, (CHUNK, CHUNK), 0)
    s = lax.broadcasted_iota(jnp.int32, (CHUNK, CHUNK), 1)
    parts = []
    for h in range(4):
        wm = jnp.where(s <= t, ws_ref[h], 0.0)
        parts.append(wm.T if transpose else wm)
    return jnp.concatenate(parts, axis=1).astype(BF16)


def _head_stack(v, head):
    return jnp.concatenate([jnp.where(head == h, v, 0.0) for h in range(4)], axis=0).astype(BF16)


def _gmlp_fwd(proj, vg, w_s, b2, bd):
    S = proj.shape[0]
    tm = 512

    def body(u_ref, v_ref, g_ref, vg_ref, ws_ref, b2_ref, bd_ref, y_ref):
        v = v_ref[...]
        ms = _split_dot(v * v, bd_ref[...]) * (1.0 / HEAD_DIM)
        vn = (v * lax.rsqrt(ms + EPS)) * vg_ref[...]
        wcat = _gmlp_masked_weights(ws_ref, False)
        head = _head_index((CHUNK, GMLP_WIDTH))
        for c in range(tm // CHUNK):
            rows = slice(c * CHUNK, (c + 1) * CHUNK)
            sp = jnp.dot(wcat, _head_stack(vn[rows], head), preferred_element_type=F32) + b2_ref[...]
            silu, _ = _silu_parts(g_ref[rows, :])
            y_ref[rows, :] = ((u_ref[rows, :] * sp) * silu).astype(BF16)

    col = lambda j: pl.BlockSpec((tm, GMLP_WIDTH), lambda i, j=j: (i, j))
    const = lambda shape: pl.BlockSpec(shape, lambda i: (0,) * len(shape))
    return pl.pallas_call(
        body, name="gmlp_fwd", grid=(S // tm,),
        in_specs=[col(0), col(1), col(2), const((1, GMLP_WIDTH)), const((4, CHUNK, CHUNK)),
                  const((CHUNK, GMLP_WIDTH)), const((GMLP_WIDTH, GMLP_WIDTH))],
        out_specs=pl.BlockSpec((tm, GMLP_WIDTH), lambda i: (i, 0)),
        out_shape=pltpu.HBM((S, GMLP_WIDTH), BF16),
        compiler_params=_params(24, ("arbitrary",)),
    )(*_hbm(proj, proj, proj, vg, w_s, b2, bd))


def _gmlp_bwd(proj, dycat, vg, w_s, b2, bd):
    S = proj.shape[0]
    tm = 512
    nsteps = S // tm

    def body(u_ref, v_ref, g_ref, dy_ref, vg_ref, ws_ref, b2_ref, bd_ref,
             du_ref, dv_ref, dg_ref, dws_ref, db2_ref, dvg_ref):
        i = pl.program_id(0)

        @pl.when(i == 0)
        def _():
            dws_ref[...] = jnp.zeros_like(dws_ref)
            db2_ref[...] = jnp.zeros_like(db2_ref)
            dvg_ref[...] = jnp.zeros_like(dvg_ref)

        bdv = bd_ref[...]
        v = v_ref[...]
        ms = _split_dot(v * v, bdv) * (1.0 / HEAD_DIM)
        rv = lax.rsqrt(ms + EPS)
        xhat = v * rv
        vgv = vg_ref[...]
        vn = xhat * vgv
        wcat = _gmlp_masked_weights(ws_ref, False)
        wcat_t = _gmlp_masked_weights(ws_ref, True)
        head = _head_index((CHUNK, GMLP_WIDTH))
        dvg = jnp.zeros((1, GMLP_WIDTH), F32)
        for c in range(tm // CHUNK):
            rows = slice(c * CHUNK, (c + 1) * CHUNK)
            vn_c = vn[rows]
            spb = jnp.dot(wcat, _head_stack(vn_c, head), preferred_element_type=F32) + b2_ref[...]
            silu, dsilu = _silu_parts(g_ref[rows, :])
            dy = dy_ref[rows, :]
            u = u_ref[rows, :]
            du_ref[rows, :] = (dy * spb * silu).astype(BF16)
            dg_ref[rows, :] = (dy * u * spb * dsilu).astype(BF16)
            dsp = dy * u * silu
            db2_ref[...] += dsp
            dstack = _head_stack(dsp, head)
            dvn = jnp.dot(wcat_t, dstack, preferred_element_type=F32)
            dws_ref[...] += _nt(dstack, vn_c.astype(BF16))
            xh = xhat[rows]
            a = dvn * vgv
            mean_ax = _split_dot(a * xh, bdv) * (1.0 / HEAD_DIM)
            dv_ref[rows, :] = (rv[rows] * (a - xh * mean_ax)).astype(BF16)
            dvg = dvg + jnp.sum(dvn * xh, axis=0, keepdims=True)
        dvg_ref[...] += dvg

        @pl.when(i == nsteps - 1)
        def _():
            t = lax.broadcasted_iota(jnp.int32, (4 * CHUNK, CHUNK), 0) % CHUNK
            s = lax.broadcasted_iota(jnp.int32, (4 * CHUNK, CHUNK), 1)
            dws_ref[...] = jnp.where(s <= t, dws_ref[...], 0.0)
            db2_ref[...] = _split_dot(db2_ref[...], bdv)

    col = lambda j: pl.BlockSpec((tm, GMLP_WIDTH), lambda i, j=j: (i, j))
    const = lambda shape: pl.BlockSpec(shape, lambda i: (0,) * len(shape))
    tile = pl.BlockSpec((tm, GMLP_WIDTH), lambda i: (i, 0))
    piece = pltpu.HBM((S, GMLP_WIDTH), BF16)
    return pl.pallas_call(
        body, name="gmlp_bwd", grid=(nsteps,),
        in_specs=[col(0), col(1), col(2), col(0), const((1, GMLP_WIDTH)), const((4, CHUNK, CHUNK)),
                  const((CHUNK, GMLP_WIDTH)), const((GMLP_WIDTH, GMLP_WIDTH))],
        out_specs=[tile, tile, tile, const((4 * CHUNK, CHUNK)), const((CHUNK, GMLP_WIDTH)), const((1, GMLP_WIDTH))],
        out_shape=[piece, piece, piece, pltpu.HBM((4 * CHUNK, CHUNK), F32),
                   pltpu.HBM((CHUNK, GMLP_WIDTH), F32), pltpu.HBM((1, GMLP_WIDTH), F32)],
        compiler_params=_params(32, ("arbitrary",)),
    )(*_hbm(proj, proj, proj, dycat, vg, w_s, b2, bd))


def _band_mask():
    qi = lax.broadcasted_iota(jnp.int32, (2 * BLOCK, 2 * BLOCK), 0) % BLOCK
    ki = lax.broadcasted_iota(jnp.int32, (2 * BLOCK, 2 * BLOCK), 1)
    return ((ki < BLOCK) & (ki >= qi)) | ((ki >= BLOCK) & ((ki - BLOCK) <= qi))


def _first_block_bias(blk, blocks_per_class):
    kcol = lax.broadcasted_iota(jnp.int32, (1, 2 * BLOCK), 1)
    kill = jnp.where((blk & (blocks_per_class - 1)) == 0, NEG, 0.0)
    return jnp.where(kcol < BLOCK, kill, 0.0)


def _two_heads(q, lo):
    zero = jnp.zeros_like(q)
    return jnp.concatenate([jnp.where(lo, q, zero), jnp.where(lo, zero, q)], axis=0)


def _block_tokens(blk, d, S):
    if d == 1:
        return pl.ds(pl.multiple_of(blk * BLOCK, BLOCK), BLOCK)
    blocks_per_class = S // d // BLOCK
    r = lax.shift_right_logical(blk, blocks_per_class.bit_length() - 1)
    n = blk & (blocks_per_class - 1)
    return pl.ds(r + n * (BLOCK * d), BLOCK, stride=d)


def _padded_block(blk):
    return pl.ds(pl.multiple_of((blk + 1) * BLOCK, BLOCK), BLOCK)


def _for_blocks(n_blocks, unroll, fn):
    def group(g, carry):
        for u in range(unroll):
            fn(g * unroll + u)
        return carry
    lax.fori_loop(0, n_blocks // unroll, group, 0)


def _attn_fwd(proj, qg2, kg2, bd):
    S = proj.shape[0]
    npairs = ATTN_WIDTH // 128
    tn = 512

    def body(q_ref, k_ref, v_ref, g_ref, qg_ref, kg_ref, bd_ref, y_ref, att_ref, lse_ref, qn, kn, kc, vc):
        bdv = bd_ref[...]
        lo = lax.broadcasted_iota(jnp.int32, (BLOCK, 128), 1) < HEAD_DIM
        band_mask = _band_mask()
        kc[pl.ds(0, BLOCK), :] = jnp.zeros((BLOCK, 128), BF16)
        vc[pl.ds(0, BLOCK), :] = jnp.zeros((BLOCK, 128), BF16)

        def norm_step(i, carry):
            rows = pl.ds(pl.multiple_of(i * tn, tn), tn)
            qv = q_ref[rows, :]
            kv = k_ref[rows, :]
            qn[rows, :] = (qv * lax.rsqrt(_split_dot(qv * qv, bdv) * (1.0 / HEAD_DIM) + EPS)) * (qg_ref[...] * SCALE)
            kn[rows, :] = (kv * lax.rsqrt(_split_dot(kv * kv, bdv) * (1.0 / HEAD_DIM) + EPS)) * kg_ref[...]
            return carry
        lax.fori_loop(0, S // tn, norm_step, 0)

        def fill(blk, d):
            tokens = _block_tokens(blk, d, S)
            kc[_padded_block(blk), :] = kn[tokens, :].astype(BF16)
            vc[_padded_block(blk), :] = v_ref[tokens, :].astype(BF16)

        ones_bf = jnp.ones((2 * BLOCK, 128), BF16)

        def block(blk, d):
            tokens = _block_tokens(blk, d, S)
            keys = pl.ds(pl.multiple_of(blk * BLOCK, BLOCK), 2 * BLOCK)
            q2 = _two_heads(qn[tokens, :].astype(BF16), lo)
            s = jnp.where(band_mask, _nt(q2, kc[keys, :]), NEG) + _first_block_bias(blk, S // d // BLOCK)
            m = jnp.max(s, axis=-1, keepdims=True)
            e = jnp.exp((s - m).astype(BF16))
            ol = jnp.dot(e, jnp.concatenate([vc[keys, :], ones_bf], axis=1), preferred_element_type=F32)
            l = ol[:, 128:]
            o2 = ol[:, :128] * (1.0 / l)
            lse2 = m + jnp.log(l)
            o = jnp.where(lo, o2[:BLOCK], o2[BLOCK:])
            lse = jnp.where(lo, lse2[:BLOCK], lse2[BLOCK:])
            if d != CONFIG_ORDER[0]:
                la = lse_ref[tokens, :]
                mx = jnp.maximum(la, lse)
                wa, wb = jnp.exp(la - mx), jnp.exp(lse - mx)
                t = wa + wb
                o = (wa * att_ref[tokens, :] + wb * o) / t
                lse = mx + jnp.log(t)
            att_ref[tokens, :] = o
            lse_ref[tokens, :] = lse

        for d in CONFIG_ORDER:
            _for_blocks(S // BLOCK, 4, functools.partial(fill, d=d))
            _for_blocks(S // BLOCK, 16, functools.partial(block, d=d))

        def gate_step(i, carry):
            rows = pl.ds(pl.multiple_of(i * tn, tn), tn)
            silu, _ = _silu_parts(g_ref[rows, :])
            y_ref[rows, :] = (att_ref[rows, :] * silu).astype(BF16)
            return carry
        lax.fori_loop(0, S // tn, gate_step, 0)

    col = lambda j0: pl.BlockSpec((S, 128), lambda p, j0=j0: (0, j0 + p))
    const = lambda shape: pl.BlockSpec(shape, lambda p: (0,) * len(shape))
    out = pl.BlockSpec((S, 128), lambda p: (0, p))
    return pl.pallas_call(
        body, name="attn_fwd", grid=(npairs,),
        in_specs=[col(COL_AQ), col(COL_AK), col(COL_AV), col(COL_AG), const((1, 128)), const((1, 128)),
                  const((128, 128))],
        out_specs=[out, out, out],
        out_shape=[pltpu.HBM((S, ATTN_WIDTH), BF16), pltpu.HBM((S, ATTN_WIDTH), F32),
                   pltpu.HBM((S, ATTN_WIDTH), F32)],
        scratch_shapes=[pltpu.VMEM((S, 128), F32), pltpu.VMEM((S, 128), F32),
                        pltpu.VMEM((S + BLOCK, 128), BF16), pltpu.VMEM((S + BLOCK, 128), BF16)],
        compiler_params=_params(48, ("arbitrary",)),
    )(*_hbm(proj, proj, proj, proj, qg2, kg2, bd))


def _attn_bwd(proj, dycat, att, lse, qg2, kg2, bd):
    S = proj.shape[0]
    npairs = ATTN_WIDTH // 128
    tn = 512

    def body(q_ref, k_ref, v_ref, g_ref, dy_ref, att_ref, lse_ref, qg_ref, kg_ref, bd_ref,
             dq_ref, dk_ref, dv_ref, dg_ref, dqg_ref, dkg_ref,
             qn, kn, rq_s, rk_s, kc, vc, do_s, dd_s, dqa, dka, dva):
        bdv = bd_ref[...]
        lo = lax.broadcasted_iota(jnp.int32, (BLOCK, 128), 1) < HEAD_DIM
        kc[pl.ds(0, BLOCK), :] = jnp.zeros((BLOCK, 128), BF16)
        vc[pl.ds(0, BLOCK), :] = jnp.zeros((BLOCK, 128), BF16)

        def prepare(i, carry):
            rows = pl.ds(pl.multiple_of(i * tn, tn), tn)
            qv = q_ref[rows, :]
            kv = k_ref[rows, :]
            rq = lax.rsqrt(_split_dot(qv * qv, bdv) * (1.0 / HEAD_DIM) + EPS)
            rk = lax.rsqrt(_split_dot(kv * kv, bdv) * (1.0 / HEAD_DIM) + EPS)
            rq_s[rows, :] = rq
            rk_s[rows, :] = rk
            qn[rows, :] = (qv * rq) * (qg_ref[...] * SCALE)
            kn[rows, :] = (kv * rk) * kg_ref[...]
            silu, dsilu = _silu_parts(g_ref[rows, :])
            dy = dy_ref[rows, :]
            at = att_ref[rows, :]
            do = dy * silu
            do_s[rows, :] = do
            dd_s[rows, :] = _split_dot(do * at, bdv)
            dg_ref[rows, :] = (dy * at * dsilu).astype(BF16)
            dka[rows, :] = jnp.zeros((tn, 128), F32)
            dva[rows, :] = jnp.zeros((tn, 128), F32)
            return carry
        lax.fori_loop(0, S // tn, prepare, 0)

        kt = lax.broadcasted_iota(jnp.int32, (2 * BLOCK, 2 * BLOCK), 0)
        qt = lax.broadcasted_iota(jnp.int32, (2 * BLOCK, 2 * BLOCK), 1) % BLOCK
        band_mask_t = ((kt < BLOCK) & (kt >= qt)) | ((kt >= BLOCK) & ((kt - BLOCK) <= qt))

        def per_query_row(t):
            tt = t.T
            return jnp.concatenate([tt[0:1, :], tt[HEAD_DIM:HEAD_DIM + 1, :]], axis=1)

        def fill(blk, d):
            tokens = _block_tokens(blk, d, S)
            kc[_padded_block(blk), :] = kn[tokens, :].astype(BF16)
            vc[_padded_block(blk), :] = v_ref[tokens, :].astype(BF16)

        def block(blk, d):
            tokens = _block_tokens(blk, d, S)
            keys = pl.ds(pl.multiple_of(blk * BLOCK, BLOCK), 2 * BLOCK)
            first = (blk & (S // d // BLOCK - 1)) == 0
            q2 = _two_heads(qn[tokens, :].astype(BF16), lo)
            do2 = _two_heads(do_s[tokens, :].astype(BF16), lo)
            lse_row = per_query_row(lse_ref[tokens, :])
            dd_row = per_query_row(dd_s[tokens, :])
            kb = kc[keys, :]
            vb = vc[keys, :]
            st = jnp.where(band_mask_t, _nt(kb, q2), NEG)
            st = jnp.concatenate([st[:BLOCK] + jnp.where(first, NEG, 0.0), st[BLOCK:]], axis=0)
            pt = jnp.exp(st - lse_row)
            dst = pt * (_nt(vb, do2) - dd_row)
            ptb = pt.astype(BF16)
            dstb = dst.astype(BF16)
            dv_band = jnp.dot(ptb, do2, preferred_element_type=F32)
            dk_band = jnp.dot(dstb, q2, preferred_element_type=F32)
            before = _block_tokens(jnp.where(first, blk, blk - 1), d, S)
            dka[before, :] = dka[before, :] + dk_band[:BLOCK]
            dva[before, :] = dva[before, :] + dv_band[:BLOCK]
            dka[tokens, :] = dka[tokens, :] + dk_band[BLOCK:]
            dva[tokens, :] = dva[tokens, :] + dv_band[BLOCK:]
            dq2 = _tn(dstb, kb)
            dq = jnp.where(lo, dq2[:BLOCK], dq2[BLOCK:])
            dqa[tokens, :] = dq if d == CONFIG_ORDER[0] else dqa[tokens, :] + dq

        for d in CONFIG_ORDER:
            _for_blocks(S // BLOCK, 4, functools.partial(fill, d=d))
            _for_blocks(S // BLOCK, 8, functools.partial(block, d=d))

        def out_step(i, carry):
            dqg, dkg = carry
            rows = pl.ds(pl.multiple_of(i * tn, tn), tn)
            rq = rq_s[rows, :]
            rk = rk_s[rows, :]
            qh = q_ref[rows, :] * rq
            kh = k_ref[rows, :] * rk
            dqs = dqa[rows, :] * SCALE
            dkn = dka[rows, :]
            aq = dqs * qg_ref[...]
            ak = dkn * kg_ref[...]
            dq_ref[rows, :] = (rq * (aq - qh * (_split_dot(aq * qh, bdv) * (1.0 / HEAD_DIM)))).astype(BF16)
            dk_ref[rows, :] = (rk * (ak - kh * (_split_dot(ak * kh, bdv) * (1.0 / HEAD_DIM)))).astype(BF16)
            dv_ref[rows, :] = dva[rows, :].astype(BF16)
            dqg = dqg + jnp.sum(dqs * qh, axis=0, keepdims=True)
            dkg = dkg + jnp.sum(dkn * kh, axis=0, keepdims=True)
            return dqg, dkg
        zero = jnp.zeros((1, 128), F32)
        dqg, dkg = lax.fori_loop(0, S // tn, out_step, (zero, zero))
        dqg_ref[0] = dqg
        dkg_ref[0] = dkg

    col = lambda j0: pl.BlockSpec((S, 128), lambda p, j0=j0: (0, j0 + p))
    col1 = lambda j0: pl.BlockSpec((S, 128), lambda p, j0=j0: (0, j0 + p), pipeline_mode=pl.Buffered(1))
    const = lambda shape: pl.BlockSpec(shape, lambda p: (0,) * len(shape))
    out = pl.BlockSpec((S, 128), lambda p: (0, p))
    gain_out = pl.BlockSpec((1, 1, 128), lambda p: (p, 0, 0))
    piece = pltpu.HBM((S, ATTN_WIDTH), BF16)
    gains = pltpu.HBM((npairs, 1, 128), F32)
    f32buf = pltpu.VMEM((S, 128), F32)
    bf16pad = pltpu.VMEM((S + BLOCK, 128), BF16)
    return pl.pallas_call(
        body, name="attn_bwd", grid=(npairs,),
        in_specs=[col(COL_AQ), col(COL_AK), col(COL_AV), col1(COL_AG), col1(GMLP_WIDTH // 128), col1(0), col(0),
                  const((1, 128)), const((1, 128)), const((128, 128))],
        out_specs=[out, out, out, out, gain_out, gain_out],
        out_shape=[piece, piece, piece, piece, gains, gains],
        scratch_shapes=[f32buf, f32buf, f32buf, f32buf, bf16pad, bf16pad, f32buf, f32buf, f32buf, f32buf, f32buf],
        compiler_params=_params(60, ("arbitrary",)),
    )(*_hbm(proj, proj, proj, proj, dycat, att, lse, qg2, kg2, bd))


def _mem_kv(mem, gain, wkv_bf, kg4, bd):
    def body(mem_ref, g_ref, w_ref, kg_ref, bd_ref, hm_ref, kraw_ref, mk_ref, mv_ref):
        mv_ = mem_ref[...]
        r = lax.rsqrt(jnp.mean(mv_ * mv_, axis=-1, keepdims=True) + EPS)
        hm = ((mv_ * r) * g_ref[...]).astype(BF16)
        hm_ref[...] = hm
        kv = jnp.dot(hm, w_ref[...], preferred_element_type=F32)
        kraw = kv[:, :MEM_WIDTH]
        kraw_ref[...] = kraw
        ms = _split_dot(kraw * kraw, bd_ref[...]) * (1.0 / HEAD_DIM)
        mk_ref[...] = (kraw * lax.rsqrt(ms + EPS)) * kg_ref[...]
        mv_ref[...] = kv[:, MEM_WIDTH:]

    sq = jax.ShapeDtypeStruct((MEM_LEN, MEM_WIDTH), F32)
    return pl.pallas_call(
        body, name="mem_kv",
        out_shape=[jax.ShapeDtypeStruct((MEM_LEN, D_MODEL), BF16), sq, sq, sq],
        compiler_params=_params(16),
    )(mem, gain, wkv_bf, kg4, bd)


def _mem_fwd(proj, mk, mv, qg4, bd):
    S = proj.shape[0]
    tm = 1024

    def body(q_ref, g_ref, mk_ref, mv_ref, qg_ref, bd_ref, y_ref, om_ref):
        qv = q_ref[...]
        ms = _split_dot(qv * qv, bd_ref[...]) * (1.0 / HEAD_DIM)
        qs = (qv * lax.rsqrt(ms + EPS)) * (qg_ref[...] * SCALE)
        mkb = mk_ref[...].astype(BF16)
        mvb = mv_ref[...].astype(BF16)
        head = _head_index((tm, MEM_WIDTH))
        o = jnp.zeros((tm, MEM_WIDTH), F32)
        for h in range(4):
            s = _nt(jnp.where(head == h, qs, 0.0).astype(BF16), mkb)
            e = jnp.exp(s - jnp.max(s, axis=-1, keepdims=True))
            p = e * (1.0 / jnp.sum(e, axis=-1, keepdims=True))
            o = jnp.where(head == h, jnp.dot(p.astype(BF16), mvb, preferred_element_type=F32), o)
        om_ref[...] = o
        silu, _ = _silu_parts(g_ref[...])
        y_ref[...] = (o * silu).astype(BF16)

    col = lambda j: pl.BlockSpec((tm, MEM_WIDTH), lambda i, j=j: (i, j))
    const = lambda shape: pl.BlockSpec(shape, lambda i: (0,) * len(shape))
    tile = pl.BlockSpec((tm, MEM_WIDTH), lambda i: (i, 0))
    return pl.pallas_call(
        body, name="mem_fwd", grid=(S // tm,),
        in_specs=[col(11), col(12), const((MEM_LEN, MEM_WIDTH)), const((MEM_LEN, MEM_WIDTH)), const((1, MEM_WIDTH)),
                  const((MEM_WIDTH, MEM_WIDTH))],
        out_specs=[tile, tile],
        out_shape=[pltpu.HBM((S, MEM_WIDTH), BF16), pltpu.HBM((S, MEM_WIDTH), F32)],
        compiler_params=_params(24, ("arbitrary",)),
    )(*_hbm(proj, proj, mk, mv, qg4, bd))


def _mem_bwd(proj, dycat, om, mk, mv, qg4, bd):
    S = proj.shape[0]
    tm = 1024

    def body(q_ref, g_ref, dy_ref, om_ref, mk_ref, mv_ref, qg_ref, bd_ref,
             dq_ref, dg_ref, dmk_ref, dmv_ref, dqg_ref):
        i = pl.program_id(0)

        @pl.when(i == 0)
        def _():
            dmk_ref[...] = jnp.zeros_like(dmk_ref)
            dmv_ref[...] = jnp.zeros_like(dmv_ref)
            dqg_ref[...] = jnp.zeros_like(dqg_ref)

        bdv = bd_ref[...]
        qv = q_ref[...]
        rq = lax.rsqrt(_split_dot(qv * qv, bdv) * (1.0 / HEAD_DIM) + EPS)
        qh = qv * rq
        qs = qh * (qg_ref[...] * SCALE)
        silu, dsilu = _silu_parts(g_ref[...])
        dy = dy_ref[...]
        o = om_ref[...]
        do = dy * silu
        dg_ref[...] = (dy * o * dsilu).astype(BF16)
        dd = _split_dot(do * o, bdv)
        mkb = mk_ref[...].astype(BF16)
        mvb = mv_ref[...].astype(BF16)
        head = _head_index((tm, MEM_WIDTH))
        dqs = jnp.zeros((tm, MEM_WIDTH), F32)
        for h in range(4):
            qhd = jnp.where(head == h, qs, 0.0).astype(BF16)
            doh = jnp.where(head == h, do, 0.0).astype(BF16)
            s = _nt(qhd, mkb)
            e = jnp.exp(s - jnp.max(s, axis=-1, keepdims=True))
            p = e * (1.0 / jnp.sum(e, axis=-1, keepdims=True))
            ds = p * (_nt(doh, mvb) - dd[:, h * HEAD_DIM:h * HEAD_DIM + 1])
            dsb = ds.astype(BF16)
            dmv_ref[...] += _tn(p.astype(BF16), doh)
            dmk_ref[...] += _tn(dsb, qhd)
            dqs = jnp.where(head == h, jnp.dot(dsb, mkb, preferred_element_type=F32), dqs)
        dqs = dqs * SCALE
        a = dqs * qg_ref[...]
        dq_ref[...] = (rq * (a - qh * (_split_dot(a * qh, bdv) * (1.0 / HEAD_DIM)))).astype(BF16)
        dqg_ref[...] += jnp.sum(dqs * qh, axis=0, keepdims=True)

    col = lambda j: pl.BlockSpec((tm, MEM_WIDTH), lambda i, j=j: (i, j))
    const = lambda shape: pl.BlockSpec(shape, lambda i: (0,) * len(shape))
    tile = pl.BlockSpec((tm, MEM_WIDTH), lambda i: (i, 0))
    piece = pltpu.HBM((S, MEM_WIDTH), BF16)
    sq = pltpu.HBM((MEM_LEN, MEM_WIDTH), F32)
    return pl.pallas_call(
        body, name="mem_bwd", grid=(S // tm,),
        in_specs=[col(11), col(12), col(3), tile, const((MEM_LEN, MEM_WIDTH)), const((MEM_LEN, MEM_WIDTH)),
                  const((1, MEM_WIDTH)), const((MEM_WIDTH, MEM_WIDTH))],
        out_specs=[tile, tile, const((MEM_LEN, MEM_WIDTH)), const((MEM_LEN, MEM_WIDTH)), const((1, MEM_WIDTH))],
        out_shape=[piece, piece, sq, sq, pltpu.HBM((1, MEM_WIDTH), F32)],
        compiler_params=_params(32, ("arbitrary",)),
    )(*_hbm(proj, proj, dycat, om, mk, mv, qg4, bd))


def _mem_kv_bwd(dmk, dmv, kraw, mem, gain, kg4, wkv_bf, hm_bf, bd):
    def body(dmk_ref, dmv_ref, kraw_ref, mem_ref, g_ref, kg_ref, w_ref, hm_ref, bd_ref, dw_ref, dg_ref, dkg_ref):
        bdv = bd_ref[...]
        kraw = kraw_ref[...]
        rk = lax.rsqrt(_split_dot(kraw * kraw, bdv) * (1.0 / HEAD_DIM) + EPS)
        kh = kraw * rk
        dmkv = dmk_ref[...]
        a = dmkv * kg_ref[...]
        dkraw = rk * (a - kh * (_split_dot(a * kh, bdv) * (1.0 / HEAD_DIM)))
        dkg_ref[...] = jnp.sum(dmkv * kh, axis=0, keepdims=True)
        dkv = jnp.concatenate([dkraw, dmv_ref[...]], axis=1).astype(BF16)
        dw = _tn(hm_ref[...], dkv).astype(BF16)
        rows_blk = D_MODEL // N_DEV
        for j in range(N_DEV):
            dw_ref[j] = dw[rows_blk * j:rows_blk * (j + 1)]
        dhm = _nt(dkv, w_ref[...])
        mv_ = mem_ref[...]
        r = lax.rsqrt(jnp.mean(mv_ * mv_, axis=-1, keepdims=True) + EPS)
        dg_ref[...] = jnp.sum(dhm * (mv_ * r), axis=0, keepdims=True)

    return pl.pallas_call(
        body, name="mem_kv_bwd",
        out_shape=[jax.ShapeDtypeStruct((N_DEV, D_MODEL // N_DEV, 2 * MEM_WIDTH), BF16),
                   jax.ShapeDtypeStruct((1, D_MODEL), F32), jax.ShapeDtypeStruct((1, MEM_WIDTH), F32)],
        compiler_params=_params(24),
    )(dmk, dmv, kraw, mem, gain, kg4, wkv_bf, hm_bf, bd)


def _out_loss(yg, ya, ym, x, tgt, wout_bf):
    S, D = x.shape
    tm = 512
    nsteps = S // tm
    rows_blk = D // N_DEV

    def body(yg_ref, ya_ref, ym_ref, x_ref, t_ref, w_ref, dout_ref, dycat_ref, dw_ref, loss_ref, acc_ref):
        i = pl.program_id(0)

        @pl.when(i == 0)
        def _():
            acc_ref[...] = jnp.zeros_like(acc_ref)
            loss_ref[...] = jnp.zeros_like(loss_ref)

        ycat = jnp.concatenate([yg_ref[...], ya_ref[...], ym_ref[...]], axis=1)
        w = w_ref[...]
        diff = (x_ref[...] + jnp.dot(ycat, w, preferred_element_type=F32)) - t_ref[...]
        loss_ref[...] += jnp.sum(diff * diff, axis=0, keepdims=True)
        dout = diff * (1.0 / D)
        dout_ref[...] = dout
        db = dout.astype(BF16)
        dycat_ref[...] = _nt(db, w)
        acc_ref[...] += _tn(ycat, db)

        @pl.when(i == nsteps - 1)
        def _():
            for j in range(N_DEV):
                dw_ref[j] = acc_ref[rows_blk * j:rows_blk * (j + 1), :].astype(BF16)

    tile = lambda w: pl.BlockSpec((tm, w), lambda i: (i, 0))
    const = lambda shape: pl.BlockSpec(shape, lambda i: (0,) * len(shape))
    return pl.pallas_call(
        body, name="out_loss", grid=(nsteps,),
        in_specs=[tile(GMLP_WIDTH), tile(ATTN_WIDTH), tile(MEM_WIDTH), tile(D), tile(D), const((D, D))],
        out_specs=[tile(D), tile(D), const((N_DEV, rows_blk, D)), const((1, D))],
        out_shape=[pltpu.HBM((S, D), F32), pltpu.HBM((S, D), F32),
                   pltpu.HBM((N_DEV, rows_blk, D), BF16), pltpu.HBM((1, D), F32)],
        scratch_shapes=[pltpu.VMEM((D, D), F32)],
        compiler_params=_params(40, ("arbitrary",)),
    )(*_hbm(yg, ya, ym, x, tgt, wout_bf))


def _piece_specs(pieces, tm):
    return [pl.BlockSpec((tm, p.shape[1]), lambda i: (i, 0)) for p in pieces]


def _in_bwd_dx(pieces, x, dout, gain, w_t, dw_blocks):
    S, D = x.shape
    N = w_t.shape[0]
    tm = 256
    n = len(pieces)
    nsteps = S // tm
    middle_step = nsteps // 8

    def body(*refs):
        piece_refs = refs[:n]
        x_ref, dout_ref, g_ref, w_ref, dwb_ref, gx_ref, dg_ref, gw_ref = refs[n:n + 8]
        rs = _ReduceScatter([dwb_ref], [gw_ref], *refs[n + 8:])
        i = pl.program_id(0)

        @pl.when(i == 0)
        def _():
            dg_ref[...] = jnp.zeros_like(dg_ref)
            rs.start()

        @pl.when(i == middle_step)
        def _():
            rs.middle()

        dproj = jnp.concatenate([r[...] for r in piece_refs], axis=1)
        dh = jnp.dot(dproj, w_ref[...], preferred_element_type=F32)
        xv = x_ref[...]
        r = lax.rsqrt(jnp.mean(xv * xv, axis=-1, keepdims=True) + EPS)
        xh = xv * r
        a = dh * g_ref[...]
        gx_ref[...] = dout_ref[...] + r * (a - xh * jnp.mean(a * xh, axis=-1, keepdims=True))
        dg_ref[...] += jnp.sum(dh * xh, axis=0, keepdims=True)

        @pl.when(i == nsteps - 1)
        def _():
            rs.finish()

    tile = pl.BlockSpec((tm, D), lambda i: (i, 0))
    const = lambda shape: pl.BlockSpec(shape, lambda i: (0,) * len(shape))
    vmem = pl.BlockSpec(memory_space=pltpu.VMEM)
    return pl.pallas_call(
        body, name="in_bwd_dx", grid=(nsteps,),
        in_specs=_piece_specs(pieces, tm)
        + [tile, tile, const((1, D)), pl.BlockSpec((N, D), lambda i: (0, 0), pipeline_mode=pl.Buffered(1)), vmem],
        out_specs=[tile, const((1, D)), vmem],
        out_shape=[pltpu.HBM((S, D), F32), pltpu.HBM((1, D), F32), jax.ShapeDtypeStruct(dw_blocks.shape[1:], F32)],
        scratch_shapes=_reduce_scatter_scratch([dw_blocks]),
        compiler_params=_params(56, ("arbitrary",)),
    )(*_hbm(*pieces, x, dout, gain, w_t), dw_blocks)


def _in_bwd_dw(pieces, h_bf, others):
    S, D = h_bf.shape
    N = sum(p.shape[1] for p in pieces)
    n_blk = N // N_DEV
    tm = 512
    n = len(pieces)
    k = len(others)
    nsteps = S // tm

    def body(*refs):
        piece_refs = refs[:n]
        h_ref = refs[n]
        other_refs = refs[n + 1:n + 1 + k]
        dw_ref = refs[n + 1 + k]
        sum_refs = refs[n + 2 + k:n + 2 + 2 * k]
        acc_ref = refs[n + 2 + 2 * k]
        rs = _ReduceScatter(other_refs, sum_refs, *refs[n + 3 + 2 * k:])
        i = pl.program_id(0)

        @pl.when(i == 0)
        def _():
            acc_ref[...] = jnp.zeros_like(acc_ref)
            rs.start()

        @pl.when(i == 1)
        def _():
            rs.middle()

        dproj = jnp.concatenate([r[...] for r in piece_refs], axis=1)
        acc_ref[...] += _tn(h_ref[...], dproj)

        @pl.when(i == nsteps - 1)
        def _():
            for j in range(N_DEV):
                dw_ref[j] = acc_ref[:, n_blk * j:n_blk * (j + 1)].T.astype(BF16)
            rs.finish()

    vmem = pl.BlockSpec(memory_space=pltpu.VMEM)
    return pl.pallas_call(
        body, name="in_bwd_dw", grid=(nsteps,),
        in_specs=_piece_specs(pieces, tm) + [pl.BlockSpec((tm, D), lambda i: (i, 0))] + [vmem] * k,
        out_specs=[pl.BlockSpec((N_DEV, n_blk, D), lambda i: (0, 0, 0))] + [vmem] * k,
        out_shape=[pltpu.HBM((N_DEV, n_blk, D), BF16)] + [jax.ShapeDtypeStruct(o.shape[1:], F32) for o in others],
        scratch_shapes=[pltpu.VMEM((D, N), F32)] + _reduce_scatter_scratch(others),
        compiler_params=_params(56, ("arbitrary",)),
    )(*_hbm(*pieces, h_bf), *others)


def _row_step(m):
    return max(t for t in range(16, 257, 16) if m % t == 0)


def _place():
    x, y, c = lax.axis_index("x"), lax.axis_index("y"), lax.axis_index("c")
    chips = [(1 - x, y), (x, 1 - y), (1 - x, 1 - y)]
    return x, y, c, chips


class _AllGather:
    def __init__(self, srcs, outs, send_sems, recv_sems, local_sems, first_sem=0):
        self.srcs, self.outs, self.n, self.first_sem = srcs, outs, len(srcs), first_sem
        self.send_sems, self.recv_sems, self.local_sems = send_sems, recv_sems, local_sems

    def _rows(self, a, px, py, pc):
        m = self.srcs[a].shape[0]
        return self.outs[a].at[pl.ds((4 * px + 2 * py + pc) * m, m), :]

    def _copy(self, a, k, block, to, src=None):
        row = self.first_sem + a
        return pltpu.make_async_remote_copy(
            src_ref=self._rows(a, *block) if src is None else src, dst_ref=self._rows(a, *block),
            send_sem=self.send_sems.at[row, k], recv_sem=self.recv_sems.at[row, k], device_id=to, device_id_type=MESH)

    def _mine(self):
        x, y, c, _ = _place()
        return [pltpu.make_async_copy(self.srcs[a], self._rows(a, x, y, c), self.local_sems.at[self.first_sem + a])
                for a in range(self.n)]

    def _first(self, far):
        x, y, c, chips = _place()
        out = []
        for a in range(self.n):
            if far:
                out.append(self._copy(a, 3, (x, y, c), (*chips[2], c), src=self.srcs[a]))
            else:
                out.append(self._copy(a, 0, (x, y, c), (x, y, 1 - c), src=self.srcs[a]))
                out += [self._copy(a, 1 + j, (x, y, c), (*chips[j], c), src=self.srcs[a]) for j in (1, 0)]
        return out

    def _passed(self, j):
        x, y, c, chips = _place()
        return [self._copy(a, 4 + j, (*chips[j], c), (x, y, 1 - c)) for a in range(self.n)]

    def start(self):
        for cp in self._mine() + self._first(far=False):
            cp.start()

    def start_far(self):
        for cp in self._first(far=True):
            cp.start()

    def from_chip(self, j):
        x, y, c, chips = _place()
        for a in range(self.n):
            self._copy(a, 1 + j, (*chips[j], c), (x, y, c)).wait_recv()
        for cp in self._passed(j):
            cp.start()

    def from_sibling(self, j=None):
        x, y, c, chips = _place()
        for a in range(self.n):
            block = (x, y, 1 - c) if j is None else (*chips[j], 1 - c)
            self._copy(a, 0 if j is None else 4 + j, block, (x, y, c)).wait_recv()

    def from_self(self):
        for cp in self._mine():
            cp.wait()

    def finish(self):
        for cp in (self._first(far=False) + self._first(far=True)
                   + self._passed(0) + self._passed(1) + self._passed(2)):
            cp.wait_send()

    def run(self):
        self.start()
        self.start_far()
        self.from_self()
        for j in range(3):
            self.from_chip(j)
        self.from_sibling()
        for j in range(3):
            self.from_sibling(j)
        self.finish()


def _gather_proj(x, gain, shards, xpos):
    S, D = x.shape
    n = len(shards)
    N = N_DEV * shards[0].shape[0]
    half = N // 2
    tm = 1024
    nsteps = S // tm

    def body(*refs):
        xpos_ref, x_ref, g_ref = refs[:3]
        ins = refs[3:3 + n]
        proj_ref, h_ref = refs[3 + n:5 + n]
        outs = refs[5 + n:5 + 2 * n]
        casts = refs[5 + 2 * n:5 + 3 * n]
        whole = refs[5 + 3 * n:5 + 4 * n]
        sems = refs[5 + 4 * n:8 + 4 * n]
        ag = _AllGather(casts[:1], whole[:1], *sems)
        later = _AllGather(casts[1:], whole[1:], *sems, first_sem=1)
        out_sems, h_all = refs[8 + 4 * n:]
        p, i = pl.program_id(0), pl.program_id(1)
        rows = pl.ds(pl.multiple_of(i * tm, tm), tm)

        @pl.when((p == 0) & (i == 0))
        def _():
            for a in range(n):
                tr = _row_step(ins[a].shape[0])

                def cast(r, carry, a=a, tr=tr):
                    at = pl.ds(pl.multiple_of(r * tr, tr), tr)
                    casts[a][at, :] = ins[a][at, :].astype(BF16)
                    return carry
                lax.fori_loop(0, ins[a].shape[0] // tr, cast, 0)
            ag.start()

        @pl.when(p == 0)
        def _():
            xv = x_ref[...]
            r = lax.rsqrt(jnp.mean(xv * xv, axis=-1, keepdims=True) + EPS)
            h = ((xv * r) * g_ref[...]).astype(BF16)
            h_ref[...] = h
            h_all[rows, :] = h

        @pl.when((p == 1) & (i == 0))
        def _():
            ag.from_self()
            ag.from_chip(1)
            ag.start_far()
            later.start()
            later.start_far()
            ag.from_sibling()
            ag.from_sibling(1)

        @pl.when((p == 2) & (i == 0))
        def _():
            for j in (0, 2):
                ag.from_chip(j)
            for j in (0, 2):
                ag.from_sibling(j)

        @pl.when(p > 0)
        def _():
            which = (xpos_ref[0] + p - 1) % 2
            w_half = whole[0][pl.ds(pl.multiple_of(which * half, half), half), :]
            proj_ref[...] = _nt(h_all[rows, :], w_half)

        @pl.when((p == 2) & (i == nsteps - 1))
        def _():
            ag.finish()
            later.from_self()
            for j in range(3):
                later.from_chip(j)
            later.from_sibling()
            for j in range(3):
                later.from_sibling(j)
            later.finish()
            to_results = [pltpu.make_async_copy(whole[a], outs[a], out_sems.at[a]) for a in range(n)]
            for cp in to_results:
                cp.start()
            for cp in to_results:
                cp.wait()

    vmem = pl.BlockSpec(memory_space=pltpu.VMEM)
    hbm = pl.BlockSpec(memory_space=pl.ANY)
    gathered = [(N_DEV * a.shape[0], a.shape[1]) for a in shards]
    x_tile = lambda p, i, xp: (jnp.where(p == 0, i, nsteps - 1), 0)
    proj_tile = lambda p, i, xp: (jnp.where(p == 0, 0, i), (xp[0] + jnp.maximum(p - 1, 0)) % 2)
    grid_spec = pltpu.PrefetchScalarGridSpec(
        num_scalar_prefetch=1, grid=(3, nsteps),
        in_specs=[pl.BlockSpec((tm, D), x_tile), pl.BlockSpec((1, D), lambda p, i, xp: (0, 0))] + [vmem] * n,
        out_specs=[pl.BlockSpec((tm, half), proj_tile), pl.BlockSpec((tm, D), x_tile)] + [hbm] * n,
        scratch_shapes=[pltpu.VMEM(a.shape, BF16) for a in shards] + [pltpu.VMEM(g, BF16) for g in gathered]
        + [pltpu.SemaphoreType.DMA((n, 7)), pltpu.SemaphoreType.DMA((n, 7)), pltpu.SemaphoreType.DMA((n,)),
           pltpu.SemaphoreType.DMA((n,)), pltpu.VMEM((S, D), BF16)])
    return pl.pallas_call(
        body, name="gather_proj", grid_spec=grid_spec,
        out_shape=[pltpu.HBM((S, N), F32), pltpu.HBM((S, D), BF16)] + [pltpu.HBM(g, BF16) for g in gathered],
        compiler_params=_params(56, ("arbitrary", "arbitrary")),
    )(xpos, *_hbm(x, gain), *shards)


ROW_NORM, ROW_MEM_NORM, ROW_V_GAIN, ROW_B, ROW_ATTN_GAINS, ROW_MEM_GAINS, ROW_W_S, ROW_LOSS = 0, 8, 16, 18, 22, 23, 24, 536
SMALL_ROWS = 544


def _gather_small(dgain, dmgain, dvg, db2, dqg, dkg, dmqg, dmkg, dws, sq):
    def body(dgain_ref, dmgain_ref, dvg_ref, db2_ref, dqg_ref, dkg_ref, dmqg_ref, dmkg_ref, dws_ref, sq_ref,
             out_ref, mine, send_sems, recv_sems, local_sems):
        first = lax.broadcasted_iota(jnp.int32, (1, 128), 1) < HEAD_DIM
        for i in range(8):
            cols = slice(128 * i, 128 * (i + 1))
            mine[ROW_NORM + i:ROW_NORM + i + 1, :] = dgain_ref[:, cols]
            mine[ROW_MEM_NORM + i:ROW_MEM_NORM + i + 1, :] = dmgain_ref[:, cols]
            mine[ROW_LOSS + i:ROW_LOSS + i + 1, :] = sq_ref[:, cols]
        mine[ROW_V_GAIN:ROW_V_GAIN + 1, :] = dvg_ref[:, 0:128]
        mine[ROW_V_GAIN + 1:ROW_V_GAIN + 2, :] = dvg_ref[:, 128:256]
        bt = db2_ref[...].T
        for h in range(4):
            mine[ROW_B + h:ROW_B + h + 1, :] = bt[HEAD_DIM * h:HEAD_DIM * h + 1, :]

        def fold_heads(t):
            return t + pltpu.roll(t, HEAD_DIM, axis=1)
        aq = fold_heads(dqg_ref[0] + dqg_ref[1] + dqg_ref[2] + dqg_ref[3])
        ak = fold_heads(dkg_ref[0] + dkg_ref[1] + dkg_ref[2] + dkg_ref[3])
        mine[ROW_ATTN_GAINS:ROW_ATTN_GAINS + 1, :] = jnp.where(first, aq, ak)
        mq = fold_heads(dmqg_ref[:, 0:128] + dmqg_ref[:, 128:256])
        mk = fold_heads(dmkg_ref[:, 0:128] + dmkg_ref[:, 128:256])
        mine[ROW_MEM_GAINS:ROW_MEM_GAINS + 1, :] = jnp.where(first, mq, mk)
        mine[ROW_W_S:ROW_W_S + 4 * CHUNK, :] = dws_ref[...]
        _AllGather([mine], [out_ref], send_sems, recv_sems, local_sems).run()

    return pl.pallas_call(
        body, name="gather_small_grads",
        out_shape=jax.ShapeDtypeStruct((N_DEV * SMALL_ROWS, 128), F32),
        scratch_shapes=[pltpu.VMEM((SMALL_ROWS, 128), F32), pltpu.SemaphoreType.DMA((1, 7)),
                        pltpu.SemaphoreType.DMA((1, 7)), pltpu.SemaphoreType.DMA((1,))],
        compiler_params=_params(16),
    )(dgain, dmgain, dvg, db2, dqg, dkg, dmqg, dmkg, dws, sq)


def _reduce_scatter_scratch(arrs):
    n = len(arrs)
    return ([pltpu.VMEM((4,) + a.shape[1:], BF16) for a in arrs] + [pltpu.VMEM((3,) + a.shape[1:], BF16) for a in arrs]
            + [pltpu.SemaphoreType.DMA((n, 7)), pltpu.SemaphoreType.DMA((n, 7))])


class _ReduceScatter:
    def __init__(self, ins, outs, *scratch):
        n = len(ins)
        self.n, self.ins, self.outs = n, ins, outs
        self.half, self.quarter = scratch[:n], scratch[n:2 * n]
        self.send_sems, self.recv_sems = scratch[2 * n:]

    def _to_sibling(self):
        x, y, c, _ = _place()
        return [pltpu.make_async_remote_copy(
            src_ref=self.ins[a].at[2 * q + (1 - c)], dst_ref=self.half[a].at[q], send_sem=self.send_sems.at[a, q],
            recv_sem=self.recv_sems.at[a, q], device_id=(x, y, 1 - c), device_id_type=MESH)
            for a in range(self.n) for q in range(4)]

    def _to_chips(self):
        _, _, c, chips = _place()
        return [pltpu.make_async_remote_copy(
            src_ref=self.half[a].at[2 * chip[0] + chip[1]], dst_ref=self.quarter[a].at[k],
            send_sem=self.send_sems.at[a, 4 + k], recv_sem=self.recv_sems.at[a, 4 + k], device_id=(*chip, c),
            device_id_type=MESH) for a in range(self.n) for k, chip in enumerate(chips)]

    def _rows(self, a, fn):
        m = self.ins[a].shape[1]
        tr = _row_step(m)

        def step(i, carry):
            fn(pl.ds(pl.multiple_of(i * tr, tr), tr))
            return carry
        lax.fori_loop(0, m // tr, step, 0)

    def start(self):
        for cp in self._to_sibling():
            cp.start()

    def middle(self):
        _, _, c, _ = _place()
        for cp in self._to_sibling():
            cp.wait_recv()
        for a in range(self.n):
            for q in range(4):
                def add_half(rows, a=a, q=q):
                    both = self.ins[a][2 * q + c, rows, :].astype(F32) + self.half[a][q, rows, :].astype(F32)
                    self.half[a][q, rows, :] = both.astype(BF16)
                self._rows(a, add_half)
        for cp in self._to_chips():
            cp.start()

    def finish(self):
        x, y, _, _ = _place()
        for cp in self._to_chips():
            cp.wait_recv()
        for a in range(self.n):
            def add_quarters(rows, a=a):
                f = lambda t: t.astype(F32)
                self.outs[a][rows, :] = ((f(self.half[a][2 * x + y, rows, :]) + f(self.quarter[a][0, rows, :]))
                                         + (f(self.quarter[a][1, rows, :]) + f(self.quarter[a][2, rows, :])))
            self._rows(a, add_quarters)
        for cp in self._to_sibling() + self._to_chips():
            cp.wait_send()


def _adamw_math(w, g, m, v):
    m = ADAM_B1 * m + (1.0 - ADAM_B1) * g
    v = ADAM_B2 * v + (1.0 - ADAM_B2) * (g * g)
    m_hat = m / (1.0 - ADAM_B1 ** ADAM_STEP)
    v_hat = v / (1.0 - ADAM_B2 ** ADAM_STEP)
    delta = -ADAM_LR * (m_hat / (jnp.sqrt(v_hat) + ADAM_EPS) + ADAM_WD * w)
    return delta, m, v


def _adamw(w, g, m, v, name):
    R, C = w.shape
    tr = _row_step(R)

    def body(w_ref, g_ref, m_ref, v_ref, d_ref, nm_ref, nv_ref):
        d_ref[...], nm_ref[...], nv_ref[...] = _adamw_math(w_ref[...], g_ref[...], m_ref[...], v_ref[...])

    tile = pl.BlockSpec((tr, C), lambda i: (i, 0))
    out = pltpu.HBM((R, C), F32)
    return pl.pallas_call(
        body, name=name, grid=(R // tr,), in_specs=[tile] * 4, out_specs=[tile] * 3, out_shape=[out] * 3,
        compiler_params=_params(16, ("arbitrary",)),
    )(*_hbm(w, g, m, v))


SMALL = ("norm_gain", "gmlp_v_gain", "gmlp_w_s", "gmlp_b", "attn_q_gain", "attn_k_gain", "mem_norm_gain",
         "mem_q_gain", "mem_k_gain")
WEIGHTS = ("norm_gain", "w_in", "gmlp_v_gain", "gmlp_w_s", "gmlp_b", "attn_q_gain", "attn_k_gain",
           "mem_norm_gain", "w_mem_kv", "mem_q_gain", "mem_k_gain", "w_out")


def _adamw_small(w, m, v, g_all):
    k = len(SMALL)
    half = slice(0, HEAD_DIM), slice(HEAD_DIM, 2 * HEAD_DIM)

    def body(*refs):
        w_refs, m_refs, v_refs = refs[:k], refs[k:2 * k], refs[2 * k:3 * k]
        g_ref = refs[3 * k]
        outs = refs[3 * k + 1:7 * k + 1]
        loss_ref, gsum = refs[7 * k + 1:]

        part = SMALL_ROWS // 4
        for p in range(4):
            acc = g_ref[part * p:part * (p + 1), :]
            for dev in range(1, N_DEV):
                acc = acc + g_ref[dev * SMALL_ROWS + part * p:dev * SMALL_ROWS + part * (p + 1), :]
            gsum[part * p:part * (p + 1), :] = acc

        def update(name, at, g):
            i = SMALL.index(name)
            d, nm, nv = _adamw_math(w_refs[i][at], g, m_refs[i][at], v_refs[i][at])
            outs[i][at], outs[k + i][at], outs[2 * k + i][at], outs[3 * k + i][at] = g, d, nm, nv

        for i in range(8):
            at = (slice(0, 1), slice(128 * i, 128 * (i + 1)))
            update("norm_gain", at, gsum[ROW_NORM + i:ROW_NORM + i + 1, :])
            update("mem_norm_gain", at, gsum[ROW_MEM_NORM + i:ROW_MEM_NORM + i + 1, :])
        for h in range(4):
            row = (0, slice(h, h + 1), slice(None))
            update("gmlp_v_gain", row, gsum[ROW_V_GAIN + h // 2:ROW_V_GAIN + h // 2 + 1, half[h % 2]])
            update("gmlp_b", row, gsum[ROW_B + h:ROW_B + h + 1, :])
            update("gmlp_w_s", (0, h), gsum[ROW_W_S + CHUNK * h:ROW_W_S + CHUNK * (h + 1), :])
        whole = (slice(0, 1), slice(None))
        update("attn_q_gain", whole, gsum[ROW_ATTN_GAINS:ROW_ATTN_GAINS + 1, half[0]])
        update("attn_k_gain", whole, gsum[ROW_ATTN_GAINS:ROW_ATTN_GAINS + 1, half[1]])
        update("mem_q_gain", whole, gsum[ROW_MEM_GAINS:ROW_MEM_GAINS + 1, half[0]])
        update("mem_k_gain", whole, gsum[ROW_MEM_GAINS:ROW_MEM_GAINS + 1, half[1]])
        loss_ref[...] = jnp.sum(gsum[ROW_LOSS:ROW_LOSS + 8, :], keepdims=True) * (0.5 / D_MODEL)

    shapes = [jax.ShapeDtypeStruct(w[name].shape, F32) for name in SMALL]
    res = pl.pallas_call(
        body, name="adamw_small",
        out_shape=shapes * 4 + [jax.ShapeDtypeStruct((1, 1), F32)],
        scratch_shapes=[pltpu.VMEM((SMALL_ROWS, 128), F32)],
        compiler_params=_params(16),
    )(*[w[n] for n in SMALL], *[m[n] for n in SMALL], *[v[n] for n in SMALL], g_all)
    trees = [dict(zip(SMALL, res[j * k:(j + 1) * k])) for j in range(4)]
    return (*trees, res[4 * k])


def _grads(x, mem, tgt, w, shards):
    bd128, bd256 = _head_blockdiag(128), _head_blockdiag(256)
    gain = w["norm_gain"].reshape(1, D_MODEL)
    vg = w["gmlp_v_gain"].reshape(1, GMLP_WIDTH)
    w_s = w["gmlp_w_s"].reshape(4, CHUNK, CHUNK)
    b2 = jnp.repeat(w["gmlp_b"].reshape(4, CHUNK).T, HEAD_DIM, axis=1)
    qg2 = jnp.tile(w["attn_q_gain"].reshape(1, HEAD_DIM), (1, 2))
    kg2 = jnp.tile(w["attn_k_gain"].reshape(1, HEAD_DIM), (1, 2))
    mqg4 = jnp.tile(w["mem_q_gain"].reshape(1, HEAD_DIM), (1, 4))
    mkg4 = jnp.tile(w["mem_k_gain"].reshape(1, HEAD_DIM), (1, 4))
    mgain = w["mem_norm_gain"].reshape(1, D_MODEL)

    xpos = lax.axis_index("x").astype(jnp.int32).reshape(1)
    proj, h_bf, win_t, wkv_bf, wout_bf = _gather_proj(x, gain, shards, xpos)
    yg = _gmlp_fwd(proj, vg, w_s, b2, bd256)
    ya, att, lse = _attn_fwd(proj, qg2, kg2, bd128)
    hm_bf, kraw, mk, mv = _mem_kv(mem, mgain, wkv_bf, mkg4, bd256)
    ym, om = _mem_fwd(proj, mk, mv, mqg4, bd256)
    dout, dycat, dwout, sq = _out_loss(yg, ya, ym, x, tgt, wout_bf)

    du, dgv, dgg, dws, db2, dvg = _gmlp_bwd(proj, dycat, vg, w_s, b2, bd256)
    dq, dk, dv, dag, dqg, dkg = _attn_bwd(proj, dycat, att, lse, qg2, kg2, bd128)
    dmq, dmg, dmk, dmv, dmqg = _mem_bwd(proj, dycat, om, mk, mv, mqg4, bd256)
    dwkv, dmgain, dmkg = _mem_kv_bwd(dmk, dmv, kraw, mem, mgain, mkg4, wkv_bf, hm_bf, bd256)
    pieces = [du, dgv, dgg, dq, dk, dv, dag, dmq, dmg]
    dwin, g_wkv, g_wout = _in_bwd_dw(pieces, h_bf, [dwkv, dwout])
    grad_x, dgain, g_win = _in_bwd_dx(pieces, x, dout, gain, win_t, dwin)
    return grad_x, g_win, g_wkv, g_wout, (dgain, dmgain, dvg, db2, dqg, dkg, dmqg, dmkg, dws, sq)


def kernel(x, mem, norm_gain, w_in, gmlp_v_gain, gmlp_w_s, gmlp_b, attn_q_gain, attn_k_gain, mem_norm_gain, w_mem_kv, mem_q_gain, mem_k_gain, w_out, loss_target, m_norm_gain, m_w_in, m_gmlp_v_gain, m_gmlp_w_s, m_gmlp_b, m_attn_q_gain, m_attn_k_gain, m_mem_norm_gain, m_w_mem_kv, m_mem_q_gain, m_mem_k_gain, m_w_out, v_norm_gain, v_w_in, v_gmlp_v_gain, v_gmlp_w_s, v_gmlp_b, v_attn_q_gain, v_attn_k_gain, v_mem_norm_gain, v_w_mem_kv, v_mem_q_gain, v_mem_k_gain, v_w_out):
    w = dict(norm_gain=norm_gain, w_in=w_in, gmlp_v_gain=gmlp_v_gain, gmlp_w_s=gmlp_w_s, gmlp_b=gmlp_b,
             attn_q_gain=attn_q_gain, attn_k_gain=attn_k_gain, mem_norm_gain=mem_norm_gain, w_mem_kv=w_mem_kv,
             mem_q_gain=mem_q_gain, mem_k_gain=mem_k_gain, w_out=w_out)
    m = dict(norm_gain=m_norm_gain, w_in=m_w_in, gmlp_v_gain=m_gmlp_v_gain, gmlp_w_s=m_gmlp_w_s, gmlp_b=m_gmlp_b,
             attn_q_gain=m_attn_q_gain, attn_k_gain=m_attn_k_gain, mem_norm_gain=m_mem_norm_gain,
             w_mem_kv=m_w_mem_kv, mem_q_gain=m_mem_q_gain, mem_k_gain=m_mem_k_gain, w_out=m_w_out)
    v = dict(norm_gain=v_norm_gain, w_in=v_w_in, gmlp_v_gain=v_gmlp_v_gain, gmlp_w_s=v_gmlp_w_s, gmlp_b=v_gmlp_b,
             attn_q_gain=v_attn_q_gain, attn_k_gain=v_attn_k_gain, mem_norm_gain=v_mem_norm_gain,
             w_mem_kv=v_w_mem_kv, mem_q_gain=v_mem_q_gain, mem_k_gain=v_mem_k_gain, w_out=v_w_out)
    transposed = lambda t: jnp.transpose(t[0])

    grad_x, g_win, g_wkv, g_wout, small = _grads(
        x[0], mem[0], loss_target[0], w, [transposed(w_in), w_mem_kv[0], w_out[0]])
    small_all = _gather_small(*small)

    out_g, out_d, out_m, out_v, loss = _adamw_small(w, m, v, small_all)
    d_, m_, v_ = _adamw(transposed(w_in), g_win, transposed(m_w_in), transposed(v_w_in), "adamw_w_in")
    for tree, t in ((out_g, g_win), (out_d, d_), (out_m, m_), (out_v, v_)):
        tree["w_in"] = jnp.transpose(t)[None]
    for name, g in (("w_mem_kv", g_wkv), ("w_out", g_wout)):
        d_, m_, v_ = _adamw(w[name][0], g, m[name][0], v[name][0], "adamw_" + name)
        out_g[name], out_d[name], out_m[name], out_v[name] = g[None], d_[None], m_[None], v_[None]

    return (loss.reshape(()), grad_x[None], *[out_g[k] for k in WEIGHTS], *[out_d[k] for k in WEIGHTS],
            *[out_m[k] for k in WEIGHTS], *[out_v[k] for k in WEIGHTS])
```

```python
import functools
import math

import jax
import jax.numpy as jnp
from jax import lax
from jax.experimental import pallas as pl
from jax.experimental.pallas import tpu as pltpu

F32 = jnp.float32
BF16 = jnp.bfloat16

N_DEV = 8
D_MODEL = 1024
HEAD_DIM = 64
GMLP_WIDTH = 256
ATTN_WIDTH = 512
MEM_WIDTH = 256
MEM_LEN = 256
CHUNK = 128
BLOCK = 128
DILATIONS = (1, 4, 16)
CONFIG_ORDER = tuple(reversed(DILATIONS))
EPS = 1e-6
SCALE = 1.0 / math.sqrt(HEAD_DIM)
NEG = -1e30

ADAM_LR = 0.001
ADAM_B1 = 0.9
ADAM_B2 = 0.999
ADAM_EPS = 1e-08
ADAM_WD = 0.01
ADAM_STEP = 10

MIB = 1024 * 1024
MESH = pl.DeviceIdType.MESH

COL_AQ, COL_AK, COL_AV, COL_AG = 6, 10, 14, 18


def _params(vmem_mib, semantics=None):
    kw = dict(vmem_limit_bytes=vmem_mib * MIB)
    if semantics is not None:
        kw["dimension_semantics"] = semantics
    return pltpu.CompilerParams(**kw)


def _hbm(*arrs):
    return [pltpu.with_memory_space_constraint(a, pltpu.HBM) for a in arrs]


def _split_dot(x, sel_bf):
    hi = x.astype(BF16)
    lo = (x - hi.astype(F32)).astype(BF16)
    return jnp.dot(hi, sel_bf, preferred_element_type=F32) + jnp.dot(lo, sel_bf, preferred_element_type=F32)


def _nt(a, b):
    return lax.dot_general(a, b, (((1,), (1,)), ((), ())), preferred_element_type=F32)


def _tn(a, b):
    return lax.dot_general(a, b, (((0,), (0,)), ((), ())), preferred_element_type=F32)


def _silu_parts(g):
    sg = jax.nn.sigmoid(g)
    return g * sg, sg * (1.0 + g * (1.0 - sg))


def _head_index(shape):
    return lax.shift_right_logical(lax.broadcasted_iota(jnp.int32, shape, 1), HEAD_DIM.bit_length() - 1)


def _head_blockdiag(width):
    i = jnp.arange(width) // HEAD_DIM
    return (i[:, None] == i[None, :]).astype(BF16)


def _gmlp_masked_weights(ws_ref, transpose):
    t = lax.broadcasted_iota(jnp.int32, (CHUNK, CHUNK), 0)
    s = lax.broadcasted_iota(jnp.int32, (CHUNK, CHUNK), 1)
    parts = []
    for h in range(4):
        wm = jnp.where(s <= t, ws_ref[h], 0.0)
        parts.append(wm.T if transpose else wm)
    return jnp.concatenate(parts, axis=1).astype(BF16)


def _head_stack(v, head):
    return jnp.concatenate([jnp.where(head == h, v, 0.0) for h in range(4)], axis=0).astype(BF16)


def _gmlp_fwd(proj, vg, w_s, b2, bd):
    S = proj.shape[0]
    tm = 512

    def body(u_ref, v_ref, g_ref, vg_ref, ws_ref, b2_ref, bd_ref, y_ref):
        v = v_ref[...]
        ms = _split_dot(v * v, bd_ref[...]) * (1.0 / HEAD_DIM)
        vn = (v * lax.rsqrt(ms + EPS)) * vg_ref[...]
        wcat = _gmlp_masked_weights(ws_ref, False)
        head = _head_index((CHUNK, GMLP_WIDTH))
        for c in range(tm // CHUNK):
            rows = slice(c * CHUNK, (c + 1) * CHUNK)
            sp = jnp.dot(wcat, _head_stack(vn[rows], head), preferred_element_type=F32) + b2_ref[...]
            silu, _ = _silu_parts(g_ref[rows, :])
            y_ref[rows, :] = ((u_ref[rows, :] * sp) * silu).astype(BF16)

    col = lambda j: pl.BlockSpec((tm, GMLP_WIDTH), lambda i, j=j: (i, j))
    const = lambda shape: pl.BlockSpec(shape, lambda i: (0,) * len(shape))
    return pl.pallas_call(
        body, name="gmlp_fwd", grid=(S // tm,),
        in_specs=[col(0), col(1), col(2), const((1, GMLP_WIDTH)), const((4, CHUNK, CHUNK)),
                  const((CHUNK, GMLP_WIDTH)), const((GMLP_WIDTH, GMLP_WIDTH))],
        out_specs=pl.BlockSpec((tm, GMLP_WIDTH), lambda i: (i, 0)),
        out_shape=pltpu.HBM((S, GMLP_WIDTH), BF16),
        compiler_params=_params(24, ("arbitrary",)),
    )(*_hbm(proj, proj, proj, vg, w_s, b2, bd))


def _gmlp_bwd(proj, dycat, vg, w_s, b2, bd):
    S = proj.shape[0]
    tm = 512
    nsteps = S // tm

    def body(u_ref, v_ref, g_ref, dy_ref, vg_ref, ws_ref, b2_ref, bd_ref,
             du_ref, dv_ref, dg_ref, dws_ref, db2_ref, dvg_ref):
        i = pl.program_id(0)

        @pl.when(i == 0)
        def _():
            dws_ref[...] = jnp.zeros_like(dws_ref)
            db2_ref[...] = jnp.zeros_like(db2_ref)
            dvg_ref[...] = jnp.zeros_like(dvg_ref)

        bdv = bd_ref[...]
        v = v_ref[...]
        ms = _split_dot(v * v, bdv) * (1.0 / HEAD_DIM)
        rv = lax.rsqrt(ms + EPS)
        xhat = v * rv
        vgv = vg_ref[...]
        vn = xhat * vgv
        wcat = _gmlp_masked_weights(ws_ref, False)
        wcat_t = _gmlp_masked_weights(ws_ref, True)
        head = _head_index((CHUNK, GMLP_WIDTH))
        dvg = jnp.zeros((1, GMLP_WIDTH), F32)
        for c in range(tm // CHUNK):
            rows = slice(c * CHUNK, (c + 1) * CHUNK)
            vn_c = vn[rows]
            spb = jnp.dot(wcat, _head_stack(vn_c, head), preferred_element_type=F32) + b2_ref[...]
            silu, dsilu = _silu_parts(g_ref[rows, :])
            dy = dy_ref[rows, :]
            u = u_ref[rows, :]
            du_ref[rows, :] = (dy * spb * silu).astype(BF16)
            dg_ref[rows, :] = (dy * u * spb * dsilu).astype(BF16)
            dsp = dy * u * silu
            db2_ref[...] += dsp
            dstack = _head_stack(dsp, head)
            dvn = jnp.dot(wcat_t, dstack, preferred_element_type=F32)
            dws_ref[...] += _nt(dstack, vn_c.astype(BF16))
            xh = xhat[rows]
            a = dvn * vgv
            mean_ax = _split_dot(a * xh, bdv) * (1.0 / HEAD_DIM)
            dv_ref[rows, :] = (rv[rows] * (a - xh * mean_ax)).astype(BF16)
            dvg = dvg + jnp.sum(dvn * xh, axis=0, keepdims=True)
        dvg_ref[...] += dvg

        @pl.when(i == nsteps - 1)
        def _():
            t = lax.broadcasted_iota(jnp.int32, (4 * CHUNK, CHUNK), 0) % CHUNK
            s = lax.broadcasted_iota(jnp.int32, (4 * CHUNK, CHUNK), 1)
            dws_ref[...] = jnp.where(s <= t, dws_ref[...], 0.0)
            db2_ref[...] = _split_dot(db2_ref[...], bdv)

    col = lambda j: pl.BlockSpec((tm, GMLP_WIDTH), lambda i, j=j: (i, j))
    const = lambda shape: pl.BlockSpec(shape, lambda i: (0,) * len(shape))
    tile = pl.BlockSpec((tm, GMLP_WIDTH), lambda i: (i, 0))
    piece = pltpu.HBM((S, GMLP_WIDTH), BF16)
    return pl.pallas_call(
        body, name="gmlp_bwd", grid=(nsteps,),
        in_specs=[col(0), col(1), col(2), col(0), const((1, GMLP_WIDTH)), const((4, CHUNK, CHUNK)),
                  const((CHUNK, GMLP_WIDTH)), const((GMLP_WIDTH, GMLP_WIDTH))],
        out_specs=[tile, tile, tile, const((4 * CHUNK, CHUNK)), const((CHUNK, GMLP_WIDTH)), const((1, GMLP_WIDTH))],
        out_shape=[piece, piece, piece, pltpu.HBM((4 * CHUNK, CHUNK), F32),
                   pltpu.HBM((CHUNK, GMLP_WIDTH), F32), pltpu.HBM((1, GMLP_WIDTH), F32)],
        compiler_params=_params(32, ("arbitrary",)),
    )(*_hbm(proj, proj, proj, dycat, vg, w_s, b2, bd))


def _band_mask():
    qi = lax.broadcasted_iota(jnp.int32, (2 * BLOCK, 2 * BLOCK), 0) % BLOCK
    ki = lax.broadcasted_iota(jnp.int32, (2 * BLOCK, 2 * BLOCK), 1)
    return ((ki < BLOCK) & (ki >= qi)) | ((ki >= BLOCK) & ((ki - BLOCK) <= qi))


def _first_block_bias(blk, blocks_per_class):
    kcol = lax.broadcasted_iota(jnp.int32, (1, 2 * BLOCK), 1)
    kill = jnp.where((blk & (blocks_per_class - 1)) == 0, NEG, 0.0)
    return jnp.where(kcol < BLOCK, kill, 0.0)


def _two_heads(q, lo):
    zero = jnp.zeros_like(q)
    return jnp.concatenate([jnp.where(lo, q, zero), jnp.where(lo, zero, q)], axis=0)


def _block_tokens(blk, d, S):
    if d == 1:
        return pl.ds(pl.multiple_of(blk * BLOCK, BLOCK), BLOCK)
    blocks_per_class = S // d // BLOCK
    r = lax.shift_right_logical(blk, blocks_per_class.bit_length() - 1)
    n = blk & (blocks_per_class - 1)
    return pl.ds(r + n * (BLOCK * d), BLOCK, stride=d)


def _padded_block(blk):
    return pl.ds(pl.multiple_of((blk + 1) * BLOCK, BLOCK), BLOCK)


def _for_blocks(n_blocks, unroll, fn):
    def group(g, carry):
        for u in range(unroll):
            fn(g * unroll + u)
        return carry
    lax.fori_loop(0, n_blocks // unroll, group, 0)


def _attn_fwd(proj, qg2, kg2, bd):
    S = proj.shape[0]
    npairs = ATTN_WIDTH // 128
    tn = 512

    def body(q_ref, k_ref, v_ref, g_ref, qg_ref, kg_ref, bd_ref, y_ref, att_ref, lse_ref, qn, kn, kc, vc):
        bdv = bd_ref[...]
        lo = lax.broadcasted_iota(jnp.int32, (BLOCK, 128), 1) < HEAD_DIM
        band_mask = _band_mask()
        kc[pl.ds(0, BLOCK), :] = jnp.zeros((BLOCK, 128), BF16)
        vc[pl.ds(0, BLOCK), :] = jnp.zeros((BLOCK, 128), BF16)

        def norm_step(i, carry):
            rows = pl.ds(pl.multiple_of(i * tn, tn), tn)
            qv = q_ref[rows, :]
            kv = k_ref[rows, :]
            qn[rows, :] = (qv * lax.rsqrt(_split_dot(qv * qv, bdv) * (1.0 / HEAD_DIM) + EPS)) * (qg_ref[...] * SCALE)
            kn[rows, :] = (kv * lax.rsqrt(_split_dot(kv * kv, bdv) * (1.0 / HEAD_DIM) + EPS)) * kg_ref[...]
            return carry
        lax.fori_loop(0, S // tn, norm_step, 0)

        def fill(blk, d):
            tokens = _block_tokens(blk, d, S)
            kc[_padded_block(blk), :] = kn[tokens, :].astype(BF16)
            vc[_padded_block(blk), :] = v_ref[tokens, :].astype(BF16)

        ones_bf = jnp.ones((2 * BLOCK, 128), BF16)

        def block(blk, d):
            tokens = _block_tokens(blk, d, S)
            keys = pl.ds(pl.multiple_of(blk * BLOCK, BLOCK), 2 * BLOCK)
            q2 = _two_heads(qn[tokens, :].astype(BF16), lo)
            s = jnp.where(band_mask, _nt(q2, kc[keys, :]), NEG) + _first_block_bias(blk, S // d // BLOCK)
            m = jnp.max(s, axis=-1, keepdims=True)
            e = jnp.exp((s - m).astype(BF16))
            ol = jnp.dot(e, jnp.concatenate([vc[keys, :], ones_bf], axis=1), preferred_element_type=F32)
            l = ol[:, 128:]
            o2 = ol[:, :128] * (1.0 / l)
            lse2 = m + jnp.log(l)
            o = jnp.where(lo, o2[:BLOCK], o2[BLOCK:])
            lse = jnp.where(lo, lse2[:BLOCK], lse2[BLOCK:])
            if d != CONFIG_ORDER[0]:
                la = lse_ref[tokens, :]
                mx = jnp.maximum(la, lse)
                wa, wb = jnp.exp(la - mx), jnp.exp(lse - mx)
                t = wa + wb
                o = (wa * att_ref[tokens, :] + wb * o) / t
                lse = mx + jnp.log(t)
            att_ref[tokens, :] = o
            lse_ref[tokens, :] = lse

        for d in CONFIG_ORDER:
            _for_blocks(S // BLOCK, 4, functools.partial(fill, d=d))
            _for_blocks(S // BLOCK, 16, functools.partial(block, d=d))

        def gate_step(i, carry):
            rows = pl.ds(pl.multiple_of(i * tn, tn), tn)
            silu, _ = _silu_parts(g_ref[rows, :])
            y_ref[rows, :] = (att_ref[rows, :] * silu).astype(BF16)
            return carry
        lax.fori_loop(0, S // tn, gate_step, 0)

    col = lambda j0: pl.BlockSpec((S, 128), lambda p, j0=j0: (0, j0 + p))
    const = lambda shape: pl.BlockSpec(shape, lambda p: (0,) * len(shape))
    out = pl.BlockSpec((S, 128), lambda p: (0, p))
    return pl.pallas_call(
        body, name="attn_fwd", grid=(npairs,),
        in_specs=[col(COL_AQ), col(COL_AK), col(COL_AV), col(COL_AG), const((1, 128)), const((1, 128)),
                  const((128, 128))],
        out_specs=[out, out, out],
        out_shape=[pltpu.HBM((S, ATTN_WIDTH), BF16), pltpu.HBM((S, ATTN_WIDTH), F32),
                   pltpu.HBM((S, ATTN_WIDTH), F32)],
        scratch_shapes=[pltpu.VMEM((S, 128), F32), pltpu.VMEM((S, 128), F32),
                        pltpu.VMEM((S + BLOCK, 128), BF16), pltpu.VMEM((S + BLOCK, 128), BF16)],
        compiler_params=_params(48, ("arbitrary",)),
    )(*_hbm(proj, proj, proj, proj, qg2, kg2, bd))


def _attn_bwd(proj, dycat, att, lse, qg2, kg2, bd):
    S = proj.shape[0]
    npairs = ATTN_WIDTH // 128
    tn = 512

    def body(q_ref, k_ref, v_ref, g_ref, dy_ref, att_ref, lse_ref, qg_ref, kg_ref, bd_ref,
             dq_ref, dk_ref, dv_ref, dg_ref, dqg_ref, dkg_ref,
             qn, kn, rq_s, rk_s, kc, vc, do_s, dd_s, dqa, dka, dva):
        bdv = bd_ref[...]
        lo = lax.broadcasted_iota(jnp.int32, (BLOCK, 128), 1) < HEAD_DIM
        kc[pl.ds(0, BLOCK), :] = jnp.zeros((BLOCK, 128), BF16)
        vc[pl.ds(0, BLOCK), :] = jnp.zeros((BLOCK, 128), BF16)

        def prepare(i, carry):
            rows = pl.ds(pl.multiple_of(i * tn, tn), tn)
            qv = q_ref[rows, :]
            kv = k_ref[rows, :]
            rq = lax.rsqrt(_split_dot(qv * qv, bdv) * (1.0 / HEAD_DIM) + EPS)
            rk = lax.rsqrt(_split_dot(kv * kv, bdv) * (1.0 / HEAD_DIM) + EPS)
            rq_s[rows, :] = rq
            rk_s[rows, :] = rk
            qn[rows, :] = (qv * rq) * (qg_ref[...] * SCALE)
            kn[rows, :] = (kv * rk) * kg_ref[...]
            silu, dsilu = _silu_parts(g_ref[rows, :])
            dy = dy_ref[rows, :]
            at = att_ref[rows, :]
            do = dy * silu
            do_s[rows, :] = do
            dd_s[rows, :] = _split_dot(do * at, bdv)
            dg_ref[rows, :] = (dy * at * dsilu).astype(BF16)
            dka[rows, :] = jnp.zeros((tn, 128), F32)
            dva[rows, :] = jnp.zeros((tn, 128), F32)
            return carry
        lax.fori_loop(0, S // tn, prepare, 0)

        kt = lax.broadcasted_iota(jnp.int32, (2 * BLOCK, 2 * BLOCK), 0)
        qt = lax.broadcasted_iota(jnp.int32, (2 * BLOCK, 2 * BLOCK), 1) % BLOCK
        band_mask_t = ((kt < BLOCK) & (kt >= qt)) | ((kt >= BLOCK) & ((kt - BLOCK) <= qt))

        def per_query_row(t):
            tt = t.T
            return jnp.concatenate([tt[0:1, :], tt[HEAD_DIM:HEAD_DIM + 1, :]], axis=1)

        def fill(blk, d):
            tokens = _block_tokens(blk, d, S)
            kc[_padded_block(blk), :] = kn[tokens, :].astype(BF16)
            vc[_padded_block(blk), :] = v_ref[tokens, :].astype(BF16)

        def block(blk, d):
            tokens = _block_tokens(blk, d, S)
            keys = pl.ds(pl.multiple_of(blk * BLOCK, BLOCK), 2 * BLOCK)
            first = (blk & (S // d // BLOCK - 1)) == 0
            q2 = _two_heads(qn[tokens, :].astype(BF16), lo)
            do2 = _two_heads(do_s[tokens, :].astype(BF16), lo)
            lse_row = per_query_row(lse_ref[tokens, :])
            dd_row = per_query_row(dd_s[tokens, :])
            kb = kc[keys, :]
            vb = vc[keys, :]
            st = jnp.where(band_mask_t, _nt(kb, q2), NEG)
            st = jnp.concatenate([st[:BLOCK] + jnp.where(first, NEG, 0.0), st[BLOCK:]], axis=0)
            pt = jnp.exp(st - lse_row)
            dst = pt * (_nt(vb, do2) - dd_row)
            ptb = pt.astype(BF16)
            dstb = dst.astype(BF16)
            dv_band = jnp.dot(ptb, do2, preferred_element_type=F32)
            dk_band = jnp.dot(dstb, q2, preferred_element_type=F32)
            before = _block_tokens(jnp.where(first, blk, blk - 1), d, S)
            dka[before, :] = dka[before, :] + dk_band[:BLOCK]
            dva[before, :] = dva[before, :] + dv_band[:BLOCK]
            dka[tokens, :] = dka[tokens, :] + dk_band[BLOCK:]
            dva[tokens, :] = dva[tokens, :] + dv_band[BLOCK:]
            dq2 = _tn(dstb, kb)
            dq = jnp.where(lo, dq2[:BLOCK], dq2[BLOCK:])
            dqa[tokens, :] = dq if d == CONFIG_ORDER[0] else dqa[tokens, :] + dq

        for d in CONFIG_ORDER:
            _for_blocks(S // BLOCK, 4, functools.partial(fill, d=d))
            _for_blocks(S // BLOCK, 8, functools.partial(block, d=d))

        def out_step(i, carry):
            dqg, dkg = carry
            rows = pl.ds(pl.multiple_of(i * tn, tn), tn)
            rq = rq_s[rows, :]
            rk = rk_s[rows, :]
            qh = q_ref[rows, :] * rq
            kh = k_ref[rows, :] * rk
            dqs = dqa[rows, :] * SCALE
            dkn = dka[rows, :]
            aq = dqs * qg_ref[...]
            ak = dkn * kg_ref[...]
            dq_ref[rows, :] = (rq * (aq - qh * (_split_dot(aq * qh, bdv) * (1.0 / HEAD_DIM)))).astype(BF16)
            dk_ref[rows, :] = (rk * (ak - kh * (_split_dot(ak * kh, bdv) * (1.0 / HEAD_DIM)))).astype(BF16)
            dv_ref[rows, :] = dva[rows, :].astype(BF16)
            dqg = dqg + jnp.sum(dqs * qh, axis=0, keepdims=True)
            dkg = dkg + jnp.sum(dkn * kh, axis=0, keepdims=True)
            return dqg, dkg
        zero = jnp.zeros((1, 128), F32)
        dqg, dkg = lax.fori_loop(0, S // tn, out_step, (zero, zero))
        dqg_ref[0] = dqg
        dkg_ref[0] = dkg

    col = lambda j0: pl.BlockSpec((S, 128), lambda p, j0=j0: (0, j0 + p))
    col1 = lambda j0: pl.BlockSpec((S, 128), lambda p, j0=j0: (0, j0 + p), pipeline_mode=pl.Buffered(1))
    const = lambda shape: pl.BlockSpec(shape, lambda p: (0,) * len(shape))
    out = pl.BlockSpec((S, 128), lambda p: (0, p))
    gain_out = pl.BlockSpec((1, 1, 128), lambda p: (p, 0, 0))
    piece = pltpu.HBM((S, ATTN_WIDTH), BF16)
    gains = pltpu.HBM((npairs, 1, 128), F32)
    f32buf = pltpu.VMEM((S, 128), F32)
    bf16pad = pltpu.VMEM((S + BLOCK, 128), BF16)
    return pl.pallas_call(
        body, name="attn_bwd", grid=(npairs,),
        in_specs=[col(COL_AQ), col(COL_AK), col(COL_AV), col1(COL_AG), col1(GMLP_WIDTH // 128), col1(0), col(0),
                  const((1, 128)), const((1, 128)), const((128, 128))],
        out_specs=[out, out, out, out, gain_out, gain_out],
        out_shape=[piece, piece, piece, piece, gains, gains],
        scratch_shapes=[f32buf, f32buf, f32buf, f32buf, bf16pad, bf16pad, f32buf, f32buf, f32buf, f32buf, f32buf],
        compiler_params=_params(60, ("arbitrary",)),
    )(*_hbm(proj, proj, proj, proj, dycat, att, lse, qg2, kg2, bd))


def _mem_kv(mem, gain, wkv_bf, kg4, bd):
    def body(mem_ref, g_ref, w_ref, kg_ref, bd_ref, hm_ref, kraw_ref, mk_ref, mv_ref):
        mv_ = mem_ref[...]
        r = lax.rsqrt(jnp.mean(mv_ * mv_, axis=-1, keepdims=True) + EPS)
        hm = ((mv_ * r) * g_ref[...]).astype(BF16)
        hm_ref[...] = hm
        kv = jnp.dot(hm, w_ref[...], preferred_element_type=F32)
        kraw = kv[:, :MEM_WIDTH]
        kraw_ref[...] = kraw
        ms = _split_dot(kraw * kraw, bd_ref[...]) * (1.0 / HEAD_DIM)
        mk_ref[...] = (kraw * lax.rsqrt(ms + EPS)) * kg_ref[...]
        mv_ref[...] = kv[:, MEM_WIDTH:]

    sq = jax.ShapeDtypeStruct((MEM_LEN, MEM_WIDTH), F32)
    return pl.pallas_call(
        body, name="mem_kv",
        out_shape=[jax.ShapeDtypeStruct((MEM_LEN, D_MODEL), BF16), sq, sq, sq],
        compiler_params=_params(16),
    )(mem, gain, wkv_bf, kg4, bd)


def _mem_fwd(proj, mk, mv, qg4, bd):
    S = proj.shape[0]
    tm = 2048

    def body(q_ref, g_ref, mk_ref, mv_ref, qg_ref, bd_ref, y_ref, om_ref):
        qv = q_ref[...]
        ms = _split_dot(qv * qv, bd_ref[...]) * (1.0 / HEAD_DIM)
        qs = (qv * lax.rsqrt(ms + EPS)) * (qg_ref[...] * SCALE)
        mkb = mk_ref[...].astype(BF16)
        mvb = mv_ref[...].astype(BF16)
        head = _head_index((tm, MEM_WIDTH))
        o = jnp.zeros((tm, MEM_WIDTH), F32)
        for h in range(4):
            s = _nt(jnp.where(head == h, qs, 0.0).astype(BF16), mkb)
            e = jnp.exp(s - jnp.max(s, axis=-1, keepdims=True))
            p = e * (1.0 / jnp.sum(e, axis=-1, keepdims=True))
            o = jnp.where(head == h, jnp.dot(p.astype(BF16), mvb, preferred_element_type=F32), o)
        om_ref[...] = o
        silu, _ = _silu_parts(g_ref[...])
        y_ref[...] = (o * silu).astype(BF16)

    col = lambda j: pl.BlockSpec((tm, MEM_WIDTH), lambda i, j=j: (i, j))
    const = lambda shape: pl.BlockSpec(shape, lambda i: (0,) * len(shape))
    tile = pl.BlockSpec((tm, MEM_WIDTH), lambda i: (i, 0))
    return pl.pallas_call(
        body, name="mem_fwd", grid=(S // tm,),
        in_specs=[col(11), col(12), const((MEM_LEN, MEM_WIDTH)), const((MEM_LEN, MEM_WIDTH)), const((1, MEM_WIDTH)),
                  const((MEM_WIDTH, MEM_WIDTH))],
        out_specs=[tile, tile],
        out_shape=[pltpu.HBM((S, MEM_WIDTH), BF16), pltpu.HBM((S, MEM_WIDTH), F32)],
        compiler_params=_params(48, ("arbitrary",)),
    )(*_hbm(proj, proj, mk, mv, qg4, bd))


def _mem_bwd(proj, dycat, om, mk, mv, qg4, bd):
    S = proj.shape[0]
    tm = 2048

    def body(q_ref, g_ref, dy_ref, om_ref, mk_ref, mv_ref, qg_ref, bd_ref,
             dq_ref, dg_ref, dmk_ref, dmv_ref, dqg_ref):
        i = pl.program_id(0)

        @pl.when(i == 0)
        def _():
            dmk_ref[...] = jnp.zeros_like(dmk_ref)
            dmv_ref[...] = jnp.zeros_like(dmv_ref)
            dqg_ref[...] = jnp.zeros_like(dqg_ref)

        bdv = bd_ref[...]
        qv = q_ref[...]
        rq = lax.rsqrt(_split_dot(qv * qv, bdv) * (1.0 / HEAD_DIM) + EPS)
        qh = qv * rq
        qs = qh * (qg_ref[...] * SCALE)
        silu, dsilu = _silu_parts(g_ref[...])
        dy = dy_ref[...]
        o = om_ref[...]
        do = dy * silu
        dg_ref[...] = (dy * o * dsilu).astype(BF16)
        dd = _split_dot(do * o, bdv)
        mkb = mk_ref[...].astype(BF16)
        mvb = mv_ref[...].astype(BF16)
        head = _head_index((tm, MEM_WIDTH))
        dqs = jnp.zeros((tm, MEM_WIDTH), F32)
        for h in range(4):
            qhd = jnp.where(head == h, qs, 0.0).astype(BF16)
            doh = jnp.where(head == h, do, 0.0).astype(BF16)
            s = _nt(qhd, mkb)
            e = jnp.exp(s - jnp.max(s, axis=-1, keepdims=True))
            p = e * (1.0 / jnp.sum(e, axis=-1, keepdims=True))
            ds = p * (_nt(doh, mvb) - dd[:, h * HEAD_DIM:h * HEAD_DIM + 1])
            dsb = ds.astype(BF16)
            dmv_ref[...] += _tn(p.astype(BF16), doh)
            dmk_ref[...] += _tn(dsb, qhd)
            dqs = jnp.where(head == h, jnp.dot(dsb, mkb, preferred_element_type=F32), dqs)
        dqs = dqs * SCALE
        a = dqs * qg_ref[...]
        dq_ref[...] = (rq * (a - qh * (_split_dot(a * qh, bdv) * (1.0 / HEAD_DIM)))).astype(BF16)
        dqg_ref[...] += jnp.sum(dqs * qh, axis=0, keepdims=True)

    col = lambda j: pl.BlockSpec((tm, MEM_WIDTH), lambda i, j=j: (i, j))
    const = lambda shape: pl.BlockSpec(shape, lambda i: (0,) * len(shape))
    tile = pl.BlockSpec((tm, MEM_WIDTH), lambda i: (i, 0))
    piece = pltpu.HBM((S, MEM_WIDTH), BF16)
    sq = pltpu.HBM((MEM_LEN, MEM_WIDTH), F32)
    return pl.pallas_call(
        body, name="mem_bwd", grid=(S // tm,),
        in_specs=[col(11), col(12), col(3), tile, const((MEM_LEN, MEM_WIDTH)), const((MEM_LEN, MEM_WIDTH)),
                  const((1, MEM_WIDTH)), const((MEM_WIDTH, MEM_WIDTH))],
        out_specs=[tile, tile, const((MEM_LEN, MEM_WIDTH)), const((MEM_LEN, MEM_WIDTH)), const((1, MEM_WIDTH))],
        out_shape=[piece, piece, sq, sq, pltpu.HBM((1, MEM_WIDTH), F32)],
        compiler_params=_params(56, ("arbitrary",)),
    )(*_hbm(proj, proj, dycat, om, mk, mv, qg4, bd))


def _mem_kv_bwd(dmk, dmv, kraw, mem, gain, kg4, wkv_bf, hm_bf, bd):
    def body(dmk_ref, dmv_ref, kraw_ref, mem_ref, g_ref, kg_ref, w_ref, hm_ref, bd_ref, dw_ref, dg_ref, dkg_ref):
        bdv = bd_ref[...]
        kraw = kraw_ref[...]
        rk = lax.rsqrt(_split_dot(kraw * kraw, bdv) * (1.0 / HEAD_DIM) + EPS)
        kh = kraw * rk
        dmkv = dmk_ref[...]
        a = dmkv * kg_ref[...]
        dkraw = rk * (a - kh * (_split_dot(a * kh, bdv) * (1.0 / HEAD_DIM)))
        dkg_ref[...] = jnp.sum(dmkv * kh, axis=0, keepdims=True)
        dkv = jnp.concatenate([dkraw, dmv_ref[...]], axis=1).astype(BF16)
        dw = _tn(hm_ref[...], dkv).astype(BF16)
        rows_blk = D_MODEL // N_DEV
        for j in range(N_DEV):
            dw_ref[j] = dw[rows_blk * j:rows_blk * (j + 1)]
        dhm = _nt(dkv, w_ref[...])
        mv_ = mem_ref[...]
        r = lax.rsqrt(jnp.mean(mv_ * mv_, axis=-1, keepdims=True) + EPS)
        dg_ref[...] = jnp.sum(dhm * (mv_ * r), axis=0, keepdims=True)

    return pl.pallas_call(
        body, name="mem_kv_bwd",
        out_shape=[jax.ShapeDtypeStruct((N_DEV, D_MODEL // N_DEV, 2 * MEM_WIDTH), BF16),
                   jax.ShapeDtypeStruct((1, D_MODEL), F32), jax.ShapeDtypeStruct((1, MEM_WIDTH), F32)],
        compiler_params=_params(24),
    )(dmk, dmv, kraw, mem, gain, kg4, wkv_bf, hm_bf, bd)


def _out_loss(yg, ya, ym, x, tgt, wout_bf):
    S, D = x.shape
    tm = 512
    nsteps = S // tm
    rows_blk = D // N_DEV

    def body(yg_ref, ya_ref, ym_ref, x_ref, t_ref, w_ref, dout_ref, dycat_ref, dw_ref, loss_ref, acc_ref):
        i = pl.program_id(0)

        @pl.when(i == 0)
        def _():
            acc_ref[...] = jnp.zeros_like(acc_ref)
            loss_ref[...] = jnp.zeros_like(loss_ref)

        ycat = jnp.concatenate([yg_ref[...], ya_ref[...], ym_ref[...]], axis=1)
        w = w_ref[...]
        diff = (x_ref[...] + jnp.dot(ycat, w, preferred_element_type=F32)) - t_ref[...]
        loss_ref[...] += jnp.sum(diff * diff, axis=0, keepdims=True)
        dout = diff * (1.0 / D)
        dout_ref[...] = dout
        db = dout.astype(BF16)
        dycat_ref[...] = _nt(db, w)
        acc_ref[...] += _tn(ycat, db)

        @pl.when(i == nsteps - 1)
        def _():
            for j in range(N_DEV):
                dw_ref[j] = acc_ref[rows_blk * j:rows_blk * (j + 1), :].astype(BF16)

    tile = lambda w: pl.BlockSpec((tm, w), lambda i: (i, 0))
    const = lambda shape: pl.BlockSpec(shape, lambda i: (0,) * len(shape))
    return pl.pallas_call(
        body, name="out_loss", grid=(nsteps,),
        in_specs=[tile(GMLP_WIDTH), tile(ATTN_WIDTH), tile(MEM_WIDTH), tile(D), tile(D), const((D, D))],
        out_specs=[tile(D), tile(D), const((N_DEV, rows_blk, D)), const((1, D))],
        out_shape=[pltpu.HBM((S, D), F32), pltpu.HBM((S, D), F32),
                   pltpu.HBM((N_DEV, rows_blk, D), BF16), pltpu.HBM((1, D), F32)],
        scratch_shapes=[pltpu.VMEM((D, D), F32)],
        compiler_params=_params(40, ("arbitrary",)),
    )(*_hbm(yg, ya, ym, x, tgt, wout_bf))


def _piece_specs(pieces, tm):
    return [pl.BlockSpec((tm, p.shape[1]), lambda i: (i, 0)) for p in pieces]


def _in_bwd_dx(pieces, x, dout, gain, w_t, dw_blocks):
    S, D = x.shape
    N = w_t.shape[0]
    tm = 256
    n = len(pieces)
    nsteps = S // tm
    middle_step = nsteps // 8

    def body(*refs):
        piece_refs = refs[:n]
        x_ref, dout_ref, g_ref, w_ref, dwb_ref, gx_ref, dg_ref, gw_ref = refs[n:n + 8]
        rs = _ReduceScatter([dwb_ref], [gw_ref], *refs[n + 8:])
        i = pl.program_id(0)

        @pl.when(i == 0)
        def _():
            dg_ref[...] = jnp.zeros_like(dg_ref)
            rs.start()

        @pl.when(i == middle_step)
        def _():
            rs.middle()

        dproj = jnp.concatenate([r[...] for r in piece_refs], axis=1)
        dh = jnp.dot(dproj, w_ref[...], preferred_element_type=F32)
        xv = x_ref[...]
        r = lax.rsqrt(jnp.mean(xv * xv, axis=-1, keepdims=True) + EPS)
        xh = xv * r
        a = dh * g_ref[...]
        gx_ref[...] = dout_ref[...] + r * (a - xh * jnp.mean(a * xh, axis=-1, keepdims=True))
        dg_ref[...] += jnp.sum(dh * xh, axis=0, keepdims=True)

        @pl.when(i == nsteps - 1)
        def _():
            rs.finish()

    tile = pl.BlockSpec((tm, D), lambda i: (i, 0))
    const = lambda shape: pl.BlockSpec(shape, lambda i: (0,) * len(shape))
    vmem = pl.BlockSpec(memory_space=pltpu.VMEM)
    return pl.pallas_call(
        body, name="in_bwd_dx", grid=(nsteps,),
        in_specs=_piece_specs(pieces, tm)
        + [tile, tile, const((1, D)), pl.BlockSpec((N, D), lambda i: (0, 0), pipeline_mode=pl.Buffered(1)), vmem],
        out_specs=[tile, const((1, D)), vmem],
        out_shape=[pltpu.HBM((S, D), F32), pltpu.HBM((1, D), F32), jax.ShapeDtypeStruct(dw_blocks.shape[1:], F32)],
        scratch_shapes=_reduce_scatter_scratch([dw_blocks]),
        compiler_params=_params(56, ("arbitrary",)),
    )(*_hbm(*pieces, x, dout, gain, w_t), dw_blocks)


def _in_bwd_dw(pieces, h_bf, others):
    S, D = h_bf.shape
    N = sum(p.shape[1] for p in pieces)
    n_blk = N // N_DEV
    tm = 512
    n = len(pieces)
    k = len(others)
    nsteps = S // tm

    def body(*refs):
        piece_refs = refs[:n]
        h_ref = refs[n]
        other_refs = refs[n + 1:n + 1 + k]
        dw_ref = refs[n + 1 + k]
        sum_refs = refs[n + 2 + k:n + 2 + 2 * k]
        acc_ref = refs[n + 2 + 2 * k]
        rs = _ReduceScatter(other_refs, sum_refs, *refs[n + 3 + 2 * k:])
        i = pl.program_id(0)

        @pl.when(i == 0)
        def _():
            acc_ref[...] = jnp.zeros_like(acc_ref)
            rs.start()

        @pl.when(i == 1)
        def _():
            rs.middle()

        dproj = jnp.concatenate([r[...] for r in piece_refs], axis=1)
        acc_ref[...] += _tn(h_ref[...], dproj)

        @pl.when(i == nsteps - 1)
        def _():
            for j in range(N_DEV):
                dw_ref[j] = acc_ref[:, n_blk * j:n_blk * (j + 1)].T.astype(BF16)
            rs.finish()

    vmem = pl.BlockSpec(memory_space=pltpu.VMEM)
    return pl.pallas_call(
        body, name="in_bwd_dw", grid=(nsteps,),
        in_specs=_piece_specs(pieces, tm) + [pl.BlockSpec((tm, D), lambda i: (i, 0))] + [vmem] * k,
        out_specs=[pl.BlockSpec((N_DEV, n_blk, D), lambda i: (0, 0, 0))] + [vmem] * k,
        out_shape=[pltpu.HBM((N_DEV, n_blk, D), BF16)] + [jax.ShapeDtypeStruct(o.shape[1:], F32) for o in others],
        scratch_shapes=[pltpu.VMEM((D, N), F32)] + _reduce_scatter_scratch(others),
        compiler_params=_params(56, ("arbitrary",)),
    )(*_hbm(*pieces, h_bf), *others)


def _row_step(m):
    return max(t for t in range(16, 257, 16) if m % t == 0)


def _place():
    x, y, c = lax.axis_index("x"), lax.axis_index("y"), lax.axis_index("c")
    chips = [(1 - x, y), (x, 1 - y), (1 - x, 1 - y)]
    return x, y, c, chips


class _AllGather:
    def __init__(self, srcs, outs, send_sems, recv_sems, local_sems, first_sem=0):
        self.srcs, self.outs, self.n, self.first_sem = srcs, outs, len(srcs), first_sem
        self.send_sems, self.recv_sems, self.local_sems = send_sems, recv_sems, local_sems

    def _rows(self, a, px, py, pc):
        m = self.srcs[a].shape[0]
        return self.outs[a].at[pl.ds((4 * px + 2 * py + pc) * m, m), :]

    def _copy(self, a, k, block, to, src=None):
        row = self.first_sem + a
        return pltpu.make_async_remote_copy(
            src_ref=self._rows(a, *block) if src is None else src, dst_ref=self._rows(a, *block),
            send_sem=self.send_sems.at[row, k], recv_sem=self.recv_sems.at[row, k], device_id=to, device_id_type=MESH)

    def _mine(self):
        x, y, c, _ = _place()
        return [pltpu.make_async_copy(self.srcs[a], self._rows(a, x, y, c), self.local_sems.at[self.first_sem + a])
                for a in range(self.n)]

    def _first(self, far):
        x, y, c, chips = _place()
        out = []
        for a in range(self.n):
            if far:
                out.append(self._copy(a, 3, (x, y, c), (*chips[2], c), src=self.srcs[a]))
            else:
                out.append(self._copy(a, 0, (x, y, c), (x, y, 1 - c), src=self.srcs[a]))
                out += [self._copy(a, 1 + j, (x, y, c), (*chips[j], c), src=self.srcs[a]) for j in (1, 0)]
        return out

    def _passed(self, j):
        x, y, c, chips = _place()
        return [self._copy(a, 4 + j, (*chips[j], c), (x, y, 1 - c)) for a in range(self.n)]

    def start(self):
        for cp in self._mine() + self._first(far=False):
            cp.start()

    def start_far(self):
        for cp in self._first(far=True):
            cp.start()

    def from_chip(self, j):
        x, y, c, chips = _place()
        for a in range(self.n):
            self._copy(a, 1 + j, (*chips[j], c), (x, y, c)).wait_recv()
        for cp in self._passed(j):
            cp.start()

    def from_sibling(self, j=None):
        x, y, c, chips = _place()
        for a in range(self.n):
            block = (x, y, 1 - c) if j is None else (*chips[j], 1 - c)
            self._copy(a, 0 if j is None else 4 + j, block, (x, y, c)).wait_recv()

    def from_self(self):
        for cp in self._mine():
            cp.wait()

    def finish(self):
        for cp in (self._first(far=False) + self._first(far=True)
                   + self._passed(0) + self._passed(1) + self._passed(2)):
            cp.wait_send()

    def run(self):
        self.start()
        self.start_far()
        self.from_self()
        for j in range(3):
            self.from_chip(j)
        self.from_sibling()
        for j in range(3):
            self.from_sibling(j)
        self.finish()


def _gather_proj(x, gain, shards, xpos):
    S, D = x.shape
    n = len(shards)
    N = N_DEV * shards[0].shape[0]
    half = N // 2
    tm = 1024
    nsteps = S // tm

    def body(*refs):
        xpos_ref, x_ref, g_ref = refs[:3]
        ins = refs[3:3 + n]
        proj_ref, h_ref = refs[3 + n:5 + n]
        outs = refs[5 + n:5 + 2 * n]
        casts = refs[5 + 2 * n:5 + 3 * n]
        whole = refs[5 + 3 * n:5 + 4 * n]
        sems = refs[5 + 4 * n:8 + 4 * n]
        ag = _AllGather(casts[:1], whole[:1], *sems)
        later = _AllGather(casts[1:], whole[1:], *sems, first_sem=1)
        out_sems, h_all = refs[8 + 4 * n:]
        p, i = pl.program_id(0), pl.program_id(1)
        rows = pl.ds(pl.multiple_of(i * tm, tm), tm)

        @pl.when((p == 0) & (i == 0))
        def _():
            for a in range(n):
                tr = _row_step(ins[a].shape[0])

                def cast(r, carry, a=a, tr=tr):
                    at = pl.ds(pl.multiple_of(r * tr, tr), tr)
                    casts[a][at, :] = ins[a][at, :].astype(BF16)
                    return carry
                lax.fori_loop(0, ins[a].shape[0] // tr, cast, 0)
            ag.start()

        @pl.when(p == 0)
        def _():
            xv = x_ref[...]
            r = lax.rsqrt(jnp.mean(xv * xv, axis=-1, keepdims=True) + EPS)
            h = ((xv * r) * g_ref[...]).astype(BF16)
            h_ref[...] = h
            h_all[rows, :] = h

        @pl.when((p == 1) & (i == 0))
        def _():
            ag.from_self()
            ag.from_chip(1)
            ag.start_far()
            later.start()
            later.start_far()
            ag.from_sibling()
            ag.from_sibling(1)

        @pl.when((p == 2) & (i == 0))
        def _():
            for j in (0, 2):
                ag.from_chip(j)
            for j in (0, 2):
                ag.from_sibling(j)

        @pl.when(p > 0)
        def _():
            which = (xpos_ref[0] + p - 1) % 2
            w_half = whole[0][pl.ds(pl.multiple_of(which * half, half), half), :]
            proj_ref[...] = _nt(h_all[rows, :], w_half)

        @pl.when((p == 2) & (i == nsteps - 1))
        def _():
            ag.finish()
            later.from_self()
            for j in range(3):
                later.from_chip(j)
            later.from_sibling()
            for j in range(3):
                later.from_sibling(j)
            later.finish()
            to_results = [pltpu.make_async_copy(whole[a], outs[a], out_sems.at[a]) for a in range(n)]
            for cp in to_results:
                cp.start()
            for cp in to_results:
                cp.wait()

    vmem = pl.BlockSpec(memory_space=pltpu.VMEM)
    hbm = pl.BlockSpec(memory_space=pl.ANY)
    gathered = [(N_DEV * a.shape[0], a.shape[1]) for a in shards]
    x_tile = lambda p, i, xp: (jnp.where(p == 0, i, nsteps - 1), 0)
    proj_tile = lambda p, i, xp: (jnp.where(p == 0, 0, i), (xp[0] + jnp.maximum(p - 1, 0)) % 2)
    grid_spec = pltpu.PrefetchScalarGridSpec(
        num_scalar_prefetch=1, grid=(3, nsteps),
        in_specs=[pl.BlockSpec((tm, D), x_tile), pl.BlockSpec((1, D), lambda p, i, xp: (0, 0))] + [vmem] * n,
        out_specs=[pl.BlockSpec((tm, half), proj_tile), pl.BlockSpec((tm, D), x_tile)] + [hbm] * n,
        scratch_shapes=[pltpu.VMEM(a.shape, BF16) for a in shards] + [pltpu.VMEM(g, BF16) for g in gathered]
        + [pltpu.SemaphoreType.DMA((n, 7)), pltpu.SemaphoreType.DMA((n, 7)), pltpu.SemaphoreType.DMA((n,)),
           pltpu.SemaphoreType.DMA((n,)), pltpu.VMEM((S, D), BF16)])
    return pl.pallas_call(
        body, name="gather_proj", grid_spec=grid_spec,
        out_shape=[pltpu.HBM((S, N), F32), pltpu.HBM((S, D), BF16)] + [pltpu.HBM(g, BF16) for g in gathered],
        compiler_params=_params(56, ("arbitrary", "arbitrary")),
    )(xpos, *_hbm(x, gain), *shards)


ROW_NORM, ROW_MEM_NORM, ROW_V_GAIN, ROW_B, ROW_ATTN_GAINS, ROW_MEM_GAINS, ROW_W_S, ROW_LOSS = 0, 8, 16, 18, 22, 23, 24, 536
SMALL_ROWS = 544


def _gather_small(dgain, dmgain, dvg, db2, dqg, dkg, dmqg, dmkg, dws, sq):
    def body(dgain_ref, dmgain_ref, dvg_ref, db2_ref, dqg_ref, dkg_ref, dmqg_ref, dmkg_ref, dws_ref, sq_ref,
             out_ref, mine, send_sems, recv_sems, local_sems):
        first = lax.broadcasted_iota(jnp.int32, (1, 128), 1) < HEAD_DIM
        for i in range(8):
            cols = slice(128 * i, 128 * (i + 1))
            mine[ROW_NORM + i:ROW_NORM + i + 1, :] = dgain_ref[:, cols]
            mine[ROW_MEM_NORM + i:ROW_MEM_NORM + i + 1, :] = dmgain_ref[:, cols]
            mine[ROW_LOSS + i:ROW_LOSS + i + 1, :] = sq_ref[:, cols]
        mine[ROW_V_GAIN:ROW_V_GAIN + 1, :] = dvg_ref[:, 0:128]
        mine[ROW_V_GAIN + 1:ROW_V_GAIN + 2, :] = dvg_ref[:, 128:256]
        bt = db2_ref[...].T
        for h in range(4):
            mine[ROW_B + h:ROW_B + h + 1, :] = bt[HEAD_DIM * h:HEAD_DIM * h + 1, :]

        def fold_heads(t):
            return t + pltpu.roll(t, HEAD_DIM, axis=1)
        aq = fold_heads(dqg_ref[0] + dqg_ref[1] + dqg_ref[2] + dqg_ref[3])
        ak = fold_heads(dkg_ref[0] + dkg_ref[1] + dkg_ref[2] + dkg_ref[3])
        mine[ROW_ATTN_GAINS:ROW_ATTN_GAINS + 1, :] = jnp.where(first, aq, ak)
        mq = fold_heads(dmqg_ref[:, 0:128] + dmqg_ref[:, 128:256])
        mk = fold_heads(dmkg_ref[:, 0:128] + dmkg_ref[:, 128:256])
        mine[ROW_MEM_GAINS:ROW_MEM_GAINS + 1, :] = jnp.where(first, mq, mk)
        mine[ROW_W_S:ROW_W_S + 4 * CHUNK, :] = dws_ref[...]
        _AllGather([mine], [out_ref], send_sems, recv_sems, local_sems).run()

    return pl.pallas_call(
        body, name="gather_small_grads",
        out_shape=jax.ShapeDtypeStruct((N_DEV * SMALL_ROWS, 128), F32),
        scratch_shapes=[pltpu.VMEM((SMALL_ROWS, 128), F32), pltpu.SemaphoreType.DMA((1, 7)),
                        pltpu.SemaphoreType.DMA((1, 7)), pltpu.SemaphoreType.DMA((1,))],
        compiler_params=_params(16),
    )(dgain, dmgain, dvg, db2, dqg, dkg, dmqg, dmkg, dws, sq)


def _reduce_scatter_scratch(arrs):
    n = len(arrs)
    return ([pltpu.VMEM((4,) + a.shape[1:], BF16) for a in arrs] + [pltpu.VMEM((3,) + a.shape[1:], BF16) for a in arrs]
            + [pltpu.SemaphoreType.DMA((n, 7)), pltpu.SemaphoreType.DMA((n, 7))])


class _ReduceScatter:
    def __init__(self, ins, outs, *scratch):
        n = len(ins)
        self.n, self.ins, self.outs = n, ins, outs
        self.half, self.quarter = scratch[:n], scratch[n:2 * n]
        self.send_sems, self.recv_sems = scratch[2 * n:]

    def _to_sibling(self):
        x, y, c, _ = _place()
        return [pltpu.make_async_remote_copy(
            src_ref=self.ins[a].at[2 * q + (1 - c)], dst_ref=self.half[a].at[q], send_sem=self.send_sems.at[a, q],
            recv_sem=self.recv_sems.at[a, q], device_id=(x, y, 1 - c), device_id_type=MESH)
            for a in range(self.n) for q in range(4)]

    def _to_chips(self):
        _, _, c, chips = _place()
        return [pltpu.make_async_remote_copy(
            src_ref=self.half[a].at[2 * chip[0] + chip[1]], dst_ref=self.quarter[a].at[k],
            send_sem=self.send_sems.at[a, 4 + k], recv_sem=self.recv_sems.at[a, 4 + k], device_id=(*chip, c),
            device_id_type=MESH) for a in range(self.n) for k, chip in enumerate(chips)]

    def _rows(self, a, fn):
        m = self.ins[a].shape[1]
        tr = _row_step(m)

        def step(i, carry):
            fn(pl.ds(pl.multiple_of(i * tr, tr), tr))
            return carry
        lax.fori_loop(0, m // tr, step, 0)

    def start(self):
        for cp in self._to_sibling():
            cp.start()

    def middle(self):
        _, _, c, _ = _place()
        for cp in self._to_sibling():
            cp.wait_recv()
        for a in range(self.n):
            for q in range(4):
                def add_half(rows, a=a, q=q):
                    both = self.ins[a][2 * q + c, rows, :].astype(F32) + self.half[a][q, rows, :].astype(F32)
                    self.half[a][q, rows, :] = both.astype(BF16)
                self._rows(a, add_half)
        for cp in self._to_chips():
            cp.start()

    def finish(self):
        x, y, _, _ = _place()
        for cp in self._to_chips():
            cp.wait_recv()
        for a in range(self.n):
            def add_quarters(rows, a=a):
                f = lambda t: t.astype(F32)
                self.outs[a][rows, :] = ((f(self.half[a][2 * x + y, rows, :]) + f(self.quarter[a][0, rows, :]))
                                         + (f(self.quarter[a][1, rows, :]) + f(self.quarter[a][2, rows, :])))
            self._rows(a, add_quarters)
        for cp in self._to_sibling() + self._to_chips():
            cp.wait_send()


def _adamw_math(w, g, m, v):
    m = ADAM_B1 * m + (1.0 - ADAM_B1) * g
    v = ADAM_B2 * v + (1.0 - ADAM_B2) * (g * g)
    m_hat = m / (1.0 - ADAM_B1 ** ADAM_STEP)
    v_hat = v / (1.0 - ADAM_B2 ** ADAM_STEP)
    delta = -ADAM_LR * (m_hat / (jnp.sqrt(v_hat) + ADAM_EPS) + ADAM_WD * w)
    return delta, m, v


def _adamw(w, g, m, v, name):
    R, C = w.shape
    tr = _row_step(R)

    def body(w_ref, g_ref, m_ref, v_ref, d_ref, nm_ref, nv_ref):
        d_ref[...], nm_ref[...], nv_ref[...] = _adamw_math(w_ref[...], g_ref[...], m_ref[...], v_ref[...])

    tile = pl.BlockSpec((tr, C), lambda i: (i, 0))
    out = pltpu.HBM((R, C), F32)
    return pl.pallas_call(
        body, name=name, grid=(R // tr,), in_specs=[tile] * 4, out_specs=[tile] * 3, out_shape=[out] * 3,
        compiler_params=_params(16, ("arbitrary",)),
    )(*_hbm(w, g, m, v))


SMALL = ("norm_gain", "gmlp_v_gain", "gmlp_w_s", "gmlp_b", "attn_q_gain", "attn_k_gain", "mem_norm_gain",
         "mem_q_gain", "mem_k_gain")
WEIGHTS = ("norm_gain", "w_in", "gmlp_v_gain", "gmlp_w_s", "gmlp_b", "attn_q_gain", "attn_k_gain",
           "mem_norm_gain", "w_mem_kv", "mem_q_gain", "mem_k_gain", "w_out")


def _adamw_small(w, m, v, g_all):
    k = len(SMALL)
    half = slice(0, HEAD_DIM), slice(HEAD_DIM, 2 * HEAD_DIM)

    def body(*refs):
        w_refs, m_refs, v_refs = refs[:k], refs[k:2 * k], refs[2 * k:3 * k]
        g_ref = refs[3 * k]
        outs = refs[3 * k + 1:7 * k + 1]
        loss_ref, gsum = refs[7 * k + 1:]

        part = SMALL_ROWS // 4
        for p in range(4):
            acc = g_ref[part * p:part * (p + 1), :]
            for dev in range(1, N_DEV):
                acc = acc + g_ref[dev * SMALL_ROWS + part * p:dev * SMALL_ROWS + part * (p + 1), :]
            gsum[part * p:part * (p + 1), :] = acc

        def update(name, at, g):
            i = SMALL.index(name)
            d, nm, nv = _adamw_math(w_refs[i][at], g, m_refs[i][at], v_refs[i][at])
            outs[i][at], outs[k + i][at], outs[2 * k + i][at], outs[3 * k + i][at] = g, d, nm, nv

        for i in range(8):
            at = (slice(0, 1), slice(128 * i, 128 * (i + 1)))
            update("norm_gain", at, gsum[ROW_NORM + i:ROW_NORM + i + 1, :])
            update("mem_norm_gain", at, gsum[ROW_MEM_NORM + i:ROW_MEM_NORM + i + 1, :])
        for h in range(4):
            row = (0, slice(h, h + 1), slice(None))
            update("gmlp_v_gain", row, gsum[ROW_V_GAIN + h // 2:ROW_V_GAIN + h // 2 + 1, half[h % 2]])
            update("gmlp_b", row, gsum[ROW_B + h:ROW_B + h + 1, :])
            update("gmlp_w_s", (0, h), gsum[ROW_W_S + CHUNK * h:ROW_W_S + CHUNK * (h + 1), :])
        whole = (slice(0, 1), slice(None))
        update("attn_q_gain", whole, gsum[ROW_ATTN_GAINS:ROW_ATTN_GAINS + 1, half[0]])
        update("attn_k_gain", whole, gsum[ROW_ATTN_GAINS:ROW_ATTN_GAINS + 1, half[1]])
        update("mem_q_gain", whole, gsum[ROW_MEM_GAINS:ROW_MEM_GAINS + 1, half[0]])
        update("mem_k_gain", whole, gsum[ROW_MEM_GAINS:ROW_MEM_GAINS + 1, half[1]])
        loss_ref[...] = jnp.sum(gsum[ROW_LOSS:ROW_LOSS + 8, :], keepdims=True) * (0.5 / D_MODEL)

    shapes = [jax.ShapeDtypeStruct(w[name].shape, F32) for name in SMALL]
    res = pl.pallas_call(
        body, name="adamw_small",
        out_shape=shapes * 4 + [jax.ShapeDtypeStruct((1, 1), F32)],
        scratch_shapes=[pltpu.VMEM((SMALL_ROWS, 128), F32)],
        compiler_params=_params(16),
    )(*[w[n] for n in SMALL], *[m[n] for n in SMALL], *[v[n] for n in SMALL], g_all)
    trees = [dict(zip(SMALL, res[j * k:(j + 1) * k])) for j in range(4)]
    return (*trees, res[4 * k])


def _grads(x, mem, tgt, w, shards):
    bd128, bd256 = _head_blockdiag(128), _head_blockdiag(256)
    gain = w["norm_gain"].reshape(1, D_MODEL)
    vg = w["gmlp_v_gain"].reshape(1, GMLP_WIDTH)
    w_s = w["gmlp_w_s"].reshape(4, CHUNK, CHUNK)
    b2 = jnp.repeat(w["gmlp_b"].reshape(4, CHUNK).T, HEAD_DIM, axis=1)
    qg2 = jnp.tile(w["attn_q_gain"].reshape(1, HEAD_DIM), (1, 2))
    kg2 = jnp.tile(w["attn_k_gain"].reshape(1, HEAD_DIM), (1, 2))
    mqg4 = jnp.tile(w["mem_q_gain"].reshape(1, HEAD_DIM), (1, 4))
    mkg4 = jnp.tile(w["mem_k_gain"].reshape(1, HEAD_DIM), (1, 4))
    mgain = w["mem_norm_gain"].reshape(1, D_MODEL)

    xpos = lax.axis_index("x").astype(jnp.int32).reshape(1)
    proj, h_bf, win_t, wkv_bf, wout_bf = _gather_proj(x, gain, shards, xpos)
    yg = _gmlp_fwd(proj, vg, w_s, b2, bd256)
    ya, att, lse = _attn_fwd(proj, qg2, kg2, bd128)
    hm_bf, kraw, mk, mv = _mem_kv(mem, mgain, wkv_bf, mkg4, bd256)
    ym, om = _mem_fwd(proj, mk, mv, mqg4, bd256)
    dout, dycat, dwout, sq = _out_loss(yg, ya, ym, x, tgt, wout_bf)

    du, dgv, dgg, dws, db2, dvg = _gmlp_bwd(proj, dycat, vg, w_s, b2, bd256)
    dq, dk, dv, dag, dqg, dkg = _attn_bwd(proj, dycat, att, lse, qg2, kg2, bd128)
    dmq, dmg, dmk, dmv, dmqg = _mem_bwd(proj, dycat, om, mk, mv, mqg4, bd256)
    dwkv, dmgain, dmkg = _mem_kv_bwd(dmk, dmv, kraw, mem, mgain, mkg4, wkv_bf, hm_bf, bd256)
    pieces = [du, dgv, dgg, dq, dk, dv, dag, dmq, dmg]
    dwin, g_wkv, g_wout = _in_bwd_dw(pieces, h_bf, [dwkv, dwout])
    grad_x, dgain, g_win = _in_bwd_dx(pieces, x, dout, gain, win_t, dwin)
    return grad_x, g_win, g_wkv, g_wout, (dgain, dmgain, dvg, db2, dqg, dkg, dmqg, dmkg, dws, sq)


def kernel(x, mem, norm_gain, w_in, gmlp_v_gain, gmlp_w_s, gmlp_b, attn_q_gain, attn_k_gain, mem_norm_gain, w_mem_kv, mem_q_gain, mem_k_gain, w_out, loss_target, m_norm_gain, m_w_in, m_gmlp_v_gain, m_gmlp_w_s, m_gmlp_b, m_attn_q_gain, m_attn_k_gain, m_mem_norm_gain, m_w_mem_kv, m_mem_q_gain, m_mem_k_gain, m_w_out, v_norm_gain, v_w_in, v_gmlp_v_gain, v_gmlp_w_s, v_gmlp_b, v_attn_q_gain, v_attn_k_gain, v_mem_norm_gain, v_w_mem_kv, v_mem_q_gain, v_mem_k_gain, v_w_out):
    w = dict(norm_gain=norm_gain, w_in=w_in, gmlp_v_gain=gmlp_v_gain, gmlp_w_s=gmlp_w_s, gmlp_b=gmlp_b,
             attn_q_gain=attn_q_gain, attn_k_gain=attn_k_gain, mem_norm_gain=mem_norm_gain, w_mem_kv=w_mem_kv,
             mem_q_gain=mem_q_gain, mem_k_gain=mem_k_gain, w_out=w_out)
    m = dict(norm_gain=m_norm_gain, w_in=m_w_in, gmlp_v_gain=m_gmlp_v_gain, gmlp_w_s=m_gmlp_w_s, gmlp_b=m_gmlp_b,
             attn_q_gain=m_attn_q_gain, attn_k_gain=m_attn_k_gain, mem_norm_gain=m_mem_norm_gain,
             w_mem_kv=m_w_mem_kv, mem_q_gain=m_mem_q_gain, mem_k_gain=m_mem_k_gain, w_out=m_w_out)
    v = dict(norm_gain=v_norm_gain, w_in=v_w_in, gmlp_v_gain=v_gmlp_v_gain, gmlp_w_s=v_gmlp_w_s, gmlp_b=v_gmlp_b,
             attn_q_gain=v_attn_q_gain, attn_k_gain=v_attn_k_gain, mem_norm_gain=v_mem_norm_gain,
             w_mem_kv=v_w_mem_kv, mem_q_gain=v_mem_q_gain, mem_k_gain=v_mem_k_gain, w_out=v_w_out)
    transposed = lambda t: jnp.transpose(t[0])

    grad_x, g_win, g_wkv, g_wout, small = _grads(
        x[0], mem[0], loss_target[0], w, [transposed(w_in), w_mem_kv[0], w_out[0]])
    small_all = _gather_small(*small)

    out_g, out_d, out_m, out_v, loss = _adamw_small(w, m, v, small_all)
    d_, m_, v_ = _adamw(transposed(w_in), g_win, transposed(m_w_in), transposed(v_w_in), "adamw_w_in")
    for tree, t in ((out_g, g_win), (out_d, d_), (out_m, m_), (out_v, v_)):
        tree["w_in"] = jnp.transpose(t)[None]
    for name, g in (("w_mem_kv", g_wkv), ("w_out", g_wout)):
        d_, m_, v_ = _adamw(w[name][0], g, m[name][0], v[name][0], "adamw_" + name)
        out_g[name], out_d[name], out_m[name], out_v[name] = g[None], d_[None], m_[None], v_[None]

    return (loss.reshape(()), grad_x[None], *[out_g[k] for k in WEIGHTS], *[out_d[k] for k in WEIGHTS],
            *[out_m[k] for k in WEIGHTS], *[out_v[k] for k in WEIGHTS])
```

```python
import functools
import math

import jax
import jax.numpy as jnp
from jax import lax
from jax.experimental import pallas as pl
from jax.experimental.pallas import tpu as pltpu

F32 = jnp.float32
BF16 = jnp.bfloat16

N_DEV = 8
D_MODEL = 1024
HEAD_DIM = 64
GMLP_WIDTH = 256
ATTN_WIDTH = 512
MEM_WIDTH = 256
MEM_LEN = 256
CHUNK = 128
BLOCK = 128
DILATIONS = (1, 4, 16)
CONFIG_ORDER = tuple(reversed(DILATIONS))
EPS = 1e-6
SCALE = 1.0 / math.sqrt(HEAD_DIM)
NEG = -1e30

ADAM_LR = 0.001
ADAM_B1 = 0.9
ADAM_B2 = 0.999
ADAM_EPS = 1e-08
ADAM_WD = 0.01
ADAM_STEP = 10

MIB = 1024 * 1024
MESH = pl.DeviceIdType.MESH

COL_AQ, COL_AK, COL_AV, COL_AG = 6, 10, 14, 18


def _params(vmem_mib, semantics=None):
    kw = dict(vmem_limit_bytes=vmem_mib * MIB)
    if semantics is not None:
        kw["dimension_semantics"] = semantics
    return pltpu.CompilerParams(**kw)


def _hbm(*arrs):
    return [pltpu.with_memory_space_constraint(a, pltpu.HBM) for a in arrs]


def _split_dot(x, sel_bf):
    hi = x.astype(BF16)
    lo = (x - hi.astype(F32)).astype(BF16)
    return jnp.dot(hi, sel_bf, preferred_element_type=F32) + jnp.dot(lo, sel_bf, preferred_element_type=F32)


def _nt(a, b):
    return lax.dot_general(a, b, (((1,), (1,)), ((), ())), preferred_element_type=F32)


def _tn(a, b):
    return lax.dot_general(a, b, (((0,), (0,)), ((), ())), preferred_element_type=F32)


def _silu_parts(g):
    sg = jax.nn.sigmoid(g)
    return g * sg, sg * (1.0 + g * (1.0 - sg))


def _head_index(shape):
    return lax.shift_right_logical(lax.broadcasted_iota(jnp.int32, shape, 1), HEAD_DIM.bit_length() - 1)


def _head_blockdiag(width):
    i = jnp.arange(width) // HEAD_DIM
    return (i[:, None] == i[None, :]).astype(BF16)


def _gmlp_masked_weights(ws_ref, transpose):
    t = lax.broadcasted_iota(jnp.int32, (CHUNK, CHUNK), 0)
    s = lax.broadcasted_iota(jnp.int32, (CHUNK, CHUNK), 1)
    parts = []
    for h in range(4):
        wm = jnp.where(s <= t, ws_ref[h], 0.0)
        parts.append(wm.T if transpose else wm)
    return jnp.concatenate(parts, axis=1).astype(BF16)


def _head_stack(v, head):
    return jnp.concatenate([jnp.where(head == h, v, 0.0) for h in range(4)], axis=0).astype(BF16)


def _gmlp_fwd(proj, vg, w_s, b2, bd):
    S = proj.shape[0]
    tm = 1024

    def body(u_ref, v_ref, g_ref, vg_ref, ws_ref, b2_ref, bd_ref, y_ref):
        v = v_ref[...]
        ms = _split_dot(v * v, bd_ref[...]) * (1.0 / HEAD_DIM)
        vn = (v * lax.rsqrt(ms + EPS)) * vg_ref[...]
        wcat = _gmlp_masked_weights(ws_ref, False)
        head = _head_index((CHUNK, GMLP_WIDTH))
        for c in range(tm // CHUNK):
            rows = slice(c * CHUNK, (c + 1) * CHUNK)
            sp = jnp.dot(wcat, _head_stack(vn[rows], head), preferred_element_type=F32) + b2_ref[...]
            silu, _ = _silu_parts(g_ref[rows, :])
            y_ref[rows, :] = ((u_ref[rows, :] * sp) * silu).astype(BF16)

    col = lambda j: pl.BlockSpec((tm, GMLP_WIDTH), lambda i, j=j: (i, j))
    const = lambda shape: pl.BlockSpec(shape, lambda i: (0,) * len(shape))
    return pl.pallas_call(
        body, name="gmlp_fwd", grid=(S // tm,),
        in_specs=[col(0), col(1), col(2), const((1, GMLP_WIDTH)), const((4, CHUNK, CHUNK)),
                  const((CHUNK, GMLP_WIDTH)), const((GMLP_WIDTH, GMLP_WIDTH))],
        out_specs=pl.BlockSpec((tm, GMLP_WIDTH), lambda i: (i, 0)),
        out_shape=pltpu.HBM((S, GMLP_WIDTH), BF16),
        compiler_params=_params(24, ("arbitrary",)),
    )(*_hbm(proj, proj, proj, vg, w_s, b2, bd))


def _gmlp_bwd(proj, dycat, vg, w_s, b2, bd):
    S = proj.shape[0]
    tm = 1024
    nsteps = S // tm

    def body(u_ref, v_ref, g_ref, dy_ref, vg_ref, ws_ref, b2_ref, bd_ref,
             du_ref, dv_ref, dg_ref, dws_ref, db2_ref, dvg_ref):
        i = pl.program_id(0)

        @pl.when(i == 0)
        def _():
            dws_ref[...] = jnp.zeros_like(dws_ref)
            db2_ref[...] = jnp.zeros_like(db2_ref)
            dvg_ref[...] = jnp.zeros_like(dvg_ref)

        bdv = bd_ref[...]
        v = v_ref[...]
        ms = _split_dot(v * v, bdv) * (1.0 / HEAD_DIM)
        rv = lax.rsqrt(ms + EPS)
        xhat = v * rv
        vgv = vg_ref[...]
        vn = xhat * vgv
        wcat = _gmlp_masked_weights(ws_ref, False)
        wcat_t = _gmlp_masked_weights(ws_ref, True)
        head = _head_index((CHUNK, GMLP_WIDTH))
        dvg = jnp.zeros((1, GMLP_WIDTH), F32)
        for c in range(tm // CHUNK):
            rows = slice(c * CHUNK, (c + 1) * CHUNK)
            vn_c = vn[rows]
            spb = jnp.dot(wcat, _head_stack(vn_c, head), preferred_element_type=F32) + b2_ref[...]
            silu, dsilu = _silu_parts(g_ref[rows, :])
            dy = dy_ref[rows, :]
            u = u_ref[rows, :]
            du_ref[rows, :] = (dy * spb * silu).astype(BF16)
            dg_ref[rows, :] = (dy * u * spb * dsilu).astype(BF16)
            dsp = dy * u * silu
            db2_ref[...] += dsp
            dstack = _head_stack(dsp, head)
            dvn = jnp.dot(wcat_t, dstack, preferred_element_type=F32)
            dws_ref[...] += _nt(dstack, vn_c.astype(BF16))
            xh = xhat[rows]
            a = dvn * vgv
            mean_ax = _split_dot(a * xh, bdv) * (1.0 / HEAD_DIM)
            dv_ref[rows, :] = (rv[rows] * (a - xh * mean_ax)).astype(BF16)
            dvg = dvg + jnp.sum(dvn * xh, axis=0, keepdims=True)
        dvg_ref[...] += dvg

        @pl.when(i == nsteps - 1)
        def _():
            t = lax.broadcasted_iota(jnp.int32, (4 * CHUNK, CHUNK), 0) % CHUNK
            s = lax.broadcasted_iota(jnp.int32, (4 * CHUNK, CHUNK), 1)
            dws_ref[...] = jnp.where(s <= t, dws_ref[...], 0.0)
            db2_ref[...] = _split_dot(db2_ref[...], bdv)

    col = lambda j: pl.BlockSpec((tm, GMLP_WIDTH), lambda i, j=j: (i, j))
    const = lambda shape: pl.BlockSpec(shape, lambda i: (0,) * len(shape))
    tile = pl.BlockSpec((tm, GMLP_WIDTH), lambda i: (i, 0))
    piece = pltpu.HBM((S, GMLP_WIDTH), BF16)
    return pl.pallas_call(
        body, name="gmlp_bwd", grid=(nsteps,),
        in_specs=[col(0), col(1), col(2), col(0), const((1, GMLP_WIDTH)), const((4, CHUNK, CHUNK)),
                  const((CHUNK, GMLP_WIDTH)), const((GMLP_WIDTH, GMLP_WIDTH))],
        out_specs=[tile, tile, tile, const((4 * CHUNK, CHUNK)), const((CHUNK, GMLP_WIDTH)), const((1, GMLP_WIDTH))],
        out_shape=[piece, piece, piece, pltpu.HBM((4 * CHUNK, CHUNK), F32),
                   pltpu.HBM((CHUNK, GMLP_WIDTH), F32), pltpu.HBM((1, GMLP_WIDTH), F32)],
        compiler_params=_params(32, ("arbitrary",)),
    )(*_hbm(proj, proj, proj, dycat, vg, w_s, b2, bd))


def _band_mask():
    qi = lax.broadcasted_iota(jnp.int32, (2 * BLOCK, 2 * BLOCK), 0) % BLOCK
    ki = lax.broadcasted_iota(jnp.int32, (2 * BLOCK, 2 * BLOCK), 1)
    return ((ki < BLOCK) & (ki >= qi)) | ((ki >= BLOCK) & ((ki - BLOCK) <= qi))


def _first_block_bias(blk, blocks_per_class):
    kcol = lax.broadcasted_iota(jnp.int32, (1, 2 * BLOCK), 1)
    kill = jnp.where((blk & (blocks_per_class - 1)) == 0, NEG, 0.0)
    return jnp.where(kcol < BLOCK, kill, 0.0)


def _two_heads(q, lo):
    zero = jnp.zeros_like(q)
    return jnp.concatenate([jnp.where(lo, q, zero), jnp.where(lo, zero, q)], axis=0)


def _block_tokens(blk, d, S):
    if d == 1:
        return pl.ds(pl.multiple_of(blk * BLOCK, BLOCK), BLOCK)
    blocks_per_class = S // d // BLOCK
    r = lax.shift_right_logical(blk, blocks_per_class.bit_length() - 1)
    n = blk & (blocks_per_class - 1)
    return pl.ds(r + n * (BLOCK * d), BLOCK, stride=d)


def _padded_block(blk):
    return pl.ds(pl.multiple_of((blk + 1) * BLOCK, BLOCK), BLOCK)


def _for_blocks(n_blocks, unroll, fn):
    def group(g, carry):
        for u in range(unroll):
            fn(g * unroll + u)
        return carry
    lax.fori_loop(0, n_blocks // unroll, group, 0)


def _attn_fwd(proj, qg2, kg2, bd):
    S = proj.shape[0]
    npairs = ATTN_WIDTH // 128
    tn = 512

    def body(q_ref, k_ref, v_ref, g_ref, qg_ref, kg_ref, bd_ref, y_ref, att_ref, lse_ref, qn, kn, kc, vc):
        bdv = bd_ref[...]
        lo = lax.broadcasted_iota(jnp.int32, (BLOCK, 128), 1) < HEAD_DIM
        band_mask = _band_mask()
        kc[pl.ds(0, BLOCK), :] = jnp.zeros((BLOCK, 128), BF16)
        vc[pl.ds(0, BLOCK), :] = jnp.zeros((BLOCK, 128), BF16)

        def norm_step(i, carry):
            rows = pl.ds(pl.multiple_of(i * tn, tn), tn)
            qv = q_ref[rows, :]
            kv = k_ref[rows, :]
            qn[rows, :] = (qv * lax.rsqrt(_split_dot(qv * qv, bdv) * (1.0 / HEAD_DIM) + EPS)) * (qg_ref[...] * SCALE)
            kn[rows, :] = (kv * lax.rsqrt(_split_dot(kv * kv, bdv) * (1.0 / HEAD_DIM) + EPS)) * kg_ref[...]
            return carry
        lax.fori_loop(0, S // tn, norm_step, 0)

        def fill(blk, d):
            tokens = _block_tokens(blk, d, S)
            kc[_padded_block(blk), :] = kn[tokens, :].astype(BF16)
            vc[_padded_block(blk), :] = v_ref[tokens, :].astype(BF16)

        ones_bf = jnp.ones((2 * BLOCK, 128), BF16)

        def block(blk, d):
            tokens = _block_tokens(blk, d, S)
            keys = pl.ds(pl.multiple_of(blk * BLOCK, BLOCK), 2 * BLOCK)
            q2 = _two_heads(qn[tokens, :].astype(BF16), lo)
            s = jnp.where(band_mask, _nt(q2, kc[keys, :]), NEG) + _first_block_bias(blk, S // d // BLOCK)
            m = jnp.max(s, axis=-1, keepdims=True)
            e = jnp.exp((s - m).astype(BF16))
            ol = jnp.dot(e, jnp.concatenate([vc[keys, :], ones_bf], axis=1), preferred_element_type=F32)
            l = ol[:, 128:]
            o2 = ol[:, :128] * (1.0 / l)
            lse2 = m + jnp.log(l)
            o = jnp.where(lo, o2[:BLOCK], o2[BLOCK:])
            lse = jnp.where(lo, lse2[:BLOCK], lse2[BLOCK:])
            if d != CONFIG_ORDER[0]:
                la = lse_ref[tokens, :]
                mx = jnp.maximum(la, lse)
                wa, wb = jnp.exp(la - mx), jnp.exp(lse - mx)
                t = wa + wb
                o = (wa * att_ref[tokens, :] + wb * o) / t
                lse = mx + jnp.log(t)
            att_ref[tokens, :] = o
            lse_ref[tokens, :] = lse

        for d in CONFIG_ORDER:
            _for_blocks(S // BLOCK, 4, functools.partial(fill, d=d))
            _for_blocks(S // BLOCK, 16, functools.partial(block, d=d))

        def gate_step(i, carry):
            rows = pl.ds(pl.multiple_of(i * tn, tn), tn)
            silu, _ = _silu_parts(g_ref[rows, :])
            y_ref[rows, :] = (att_ref[rows, :] * silu).astype(BF16)
            return carry
        lax.fori_loop(0, S // tn, gate_step, 0)

    col = lambda j0: pl.BlockSpec((S, 128), lambda p, j0=j0: (0, j0 + p))
    const = lambda shape: pl.BlockSpec(shape, lambda p: (0,) * len(shape))
    out = pl.BlockSpec((S, 128), lambda p: (0, p))
    return pl.pallas_call(
        body, name="attn_fwd", grid=(npairs,),
        in_specs=[col(COL_AQ), col(COL_AK), col(COL_AV), col(COL_AG), const((1, 128)), const((1, 128)),
                  const((128, 128))],
        out_specs=[out, out, out],
        out_shape=[pltpu.HBM((S, ATTN_WIDTH), BF16), pltpu.HBM((S, ATTN_WIDTH), F32),
                   pltpu.HBM((S, ATTN_WIDTH), F32)],
        scratch_shapes=[pltpu.VMEM((S, 128), F32), pltpu.VMEM((S, 128), F32),
                        pltpu.VMEM((S + BLOCK, 128), BF16), pltpu.VMEM((S + BLOCK, 128), BF16)],
        compiler_params=_params(48, ("arbitrary",)),
    )(*_hbm(proj, proj, proj, proj, qg2, kg2, bd))


def _attn_bwd(proj, dycat, att, lse, qg2, kg2, bd):
    S = proj.shape[0]
    npairs = ATTN_WIDTH // 128
    tn = 512

    def body(q_ref, k_ref, v_ref, g_ref, dy_ref, att_ref, lse_ref, qg_ref, kg_ref, bd_ref,
             dq_ref, dk_ref, dv_ref, dg_ref, dqg_ref, dkg_ref,
             qn, kn, rq_s, rk_s, kc, vc, do_s, dd_s, dqa, dka, dva):
        bdv = bd_ref[...]
        lo = lax.broadcasted_iota(jnp.int32, (BLOCK, 128), 1) < HEAD_DIM
        kc[pl.ds(0, BLOCK), :] = jnp.zeros((BLOCK, 128), BF16)
        vc[pl.ds(0, BLOCK), :] = jnp.zeros((BLOCK, 128), BF16)

        def prepare(i, carry):
            rows = pl.ds(pl.multiple_of(i * tn, tn), tn)
            qv = q_ref[rows, :]
            kv = k_ref[rows, :]
            rq = lax.rsqrt(_split_dot(qv * qv, bdv) * (1.0 / HEAD_DIM) + EPS)
            rk = lax.rsqrt(_split_dot(kv * kv, bdv) * (1.0 / HEAD_DIM) + EPS)
            rq_s[rows, :] = rq
            rk_s[rows, :] = rk
            qn[rows, :] = (qv * rq) * (qg_ref[...] * SCALE)
            kn[rows, :] = (kv * rk) * kg_ref[...]
            silu, dsilu = _silu_parts(g_ref[rows, :])
            dy = dy_ref[rows, :]
            at = att_ref[rows, :]
            do = dy * silu
            do_s[rows, :] = do
            dd_s[rows, :] = _split_dot(do * at, bdv)
            dg_ref[rows, :] = (dy * at * dsilu).astype(BF16)
            dka[rows, :] = jnp.zeros((tn, 128), F32)
            dva[rows, :] = jnp.zeros((tn, 128), F32)
            return carry
        lax.fori_loop(0, S // tn, prepare, 0)

        kt = lax.broadcasted_iota(jnp.int32, (2 * BLOCK, 2 * BLOCK), 0)
        qt = lax.broadcasted_iota(jnp.int32, (2 * BLOCK, 2 * BLOCK), 1) % BLOCK
        band_mask_t = ((kt < BLOCK) & (kt >= qt)) | ((kt >= BLOCK) & ((kt - BLOCK) <= qt))

        def per_query_row(t):
            tt = t.T
            return jnp.concatenate([tt[0:1, :], tt[HEAD_DIM:HEAD_DIM + 1, :]], axis=1)

        def fill(blk, d):
            tokens = _block_tokens(blk, d, S)
            kc[_padded_block(blk), :] = kn[tokens, :].astype(BF16)
            vc[_padded_block(blk), :] = v_ref[tokens, :].astype(BF16)

        def block(blk, d):
            tokens = _block_tokens(blk, d, S)
            keys = pl.ds(pl.multiple_of(blk * BLOCK, BLOCK), 2 * BLOCK)
            first = (blk & (S // d // BLOCK - 1)) == 0
            q2 = _two_heads(qn[tokens, :].astype(BF16), lo)
            do2 = _two_heads(do_s[tokens, :].astype(BF16), lo)
            lse_row = per_query_row(lse_ref[tokens, :])
            dd_row = per_query_row(dd_s[tokens, :])
            kb = kc[keys, :]
            vb = vc[keys, :]
            st = jnp.where(band_mask_t, _nt(kb, q2), NEG)
            st = jnp.concatenate([st[:BLOCK] + jnp.where(first, NEG, 0.0), st[BLOCK:]], axis=0)
            pt = jnp.exp(st - lse_row)
            dst = pt * (_nt(vb, do2) - dd_row)
            ptb = pt.astype(BF16)
            dstb = dst.astype(BF16)
            dv_band = jnp.dot(ptb, do2, preferred_element_type=F32)
            dk_band = jnp.dot(dstb, q2, preferred_element_type=F32)
            before = _block_tokens(jnp.where(first, blk, blk - 1), d, S)
            dka[before, :] = dka[before, :] + dk_band[:BLOCK]
            dva[before, :] = dva[before, :] + dv_band[:BLOCK]
            dka[tokens, :] = dka[tokens, :] + dk_band[BLOCK:]
            dva[tokens, :] = dva[tokens, :] + dv_band[BLOCK:]
            dq2 = _tn(dstb, kb)
            dq = jnp.where(lo, dq2[:BLOCK], dq2[BLOCK:])
            dqa[tokens, :] = dq if d == CONFIG_ORDER[0] else dqa[tokens, :] + dq

        for d in CONFIG_ORDER:
            _for_blocks(S // BLOCK, 4, functools.partial(fill, d=d))
            _for_blocks(S // BLOCK, 8, functools.partial(block, d=d))

        def out_step(i, carry):
            dqg, dkg = carry
            rows = pl.ds(pl.multiple_of(i * tn, tn), tn)
            rq = rq_s[rows, :]
            rk = rk_s[rows, :]
            qh = q_ref[rows, :] * rq
            kh = k_ref[rows, :] * rk
            dqs = dqa[rows, :] * SCALE
            dkn = dka[rows, :]
            aq = dqs * qg_ref[...]
            ak = dkn * kg_ref[...]
            dq_ref[rows, :] = (rq * (aq - qh * (_split_dot(aq * qh, bdv) * (1.0 / HEAD_DIM)))).astype(BF16)
            dk_ref[rows, :] = (rk * (ak - kh * (_split_dot(ak * kh, bdv) * (1.0 / HEAD_DIM)))).astype(BF16)
            dv_ref[rows, :] = dva[rows, :].astype(BF16)
            dqg = dqg + jnp.sum(dqs * qh, axis=0, keepdims=True)
            dkg = dkg + jnp.sum(dkn * kh, axis=0, keepdims=True)
            return dqg, dkg
        zero = jnp.zeros((1, 128), F32)
        dqg, dkg = lax.fori_loop(0, S // tn, out_step, (zero, zero))
        dqg_ref[0] = dqg
        dkg_ref[0] = dkg

    col = lambda j0: pl.BlockSpec((S, 128), lambda p, j0=j0: (0, j0 + p))
    col1 = lambda j0: pl.BlockSpec((S, 128), lambda p, j0=j0: (0, j0 + p), pipeline_mode=pl.Buffered(1))
    const = lambda shape: pl.BlockSpec(shape, lambda p: (0,) * len(shape))
    out = pl.BlockSpec((S, 128), lambda p: (0, p))
    gain_out = pl.BlockSpec((1, 1, 128), lambda p: (p, 0, 0))
    piece = pltpu.HBM((S, ATTN_WIDTH), BF16)
    gains = pltpu.HBM((npairs, 1, 128), F32)
    f32buf = pltpu.VMEM((S, 128), F32)
    bf16pad = pltpu.VMEM((S + BLOCK, 128), BF16)
    return pl.pallas_call(
        body, name="attn_bwd", grid=(npairs,),
        in_specs=[col(COL_AQ), col(COL_AK), col(COL_AV), col1(COL_AG), col1(GMLP_WIDTH // 128), col1(0), col(0),
                  const((1, 128)), const((1, 128)), const((128, 128))],
        out_specs=[out, out, out, out, gain_out, gain_out],
        out_shape=[piece, piece, piece, piece, gains, gains],
        scratch_shapes=[f32buf, f32buf, f32buf, f32buf, bf16pad, bf16pad, f32buf, f32buf, f32buf, f32buf, f32buf],
        compiler_params=_params(60, ("arbitrary",)),
    )(*_hbm(proj, proj, proj, proj, dycat, att, lse, qg2, kg2, bd))


def _mem_kv(mem, gain, wkv_bf, kg4, bd):
    def body(mem_ref, g_ref, w_ref, kg_ref, bd_ref, hm_ref, kraw_ref, mk_ref, mv_ref):
        mv_ = mem_ref[...]
        r = lax.rsqrt(jnp.mean(mv_ * mv_, axis=-1, keepdims=True) + EPS)
        hm = ((mv_ * r) * g_ref[...]).astype(BF16)
        hm_ref[...] = hm
        kv = jnp.dot(hm, w_ref[...], preferred_element_type=F32)
        kraw = kv[:, :MEM_WIDTH]
        kraw_ref[...] = kraw
        ms = _split_dot(kraw * kraw, bd_ref[...]) * (1.0 / HEAD_DIM)
        mk_ref[...] = (kraw * lax.rsqrt(ms + EPS)) * kg_ref[...]
        mv_ref[...] = kv[:, MEM_WIDTH:]

    sq = jax.ShapeDtypeStruct((MEM_LEN, MEM_WIDTH), F32)
    return pl.pallas_call(
        body, name="mem_kv",
        out_shape=[jax.ShapeDtypeStruct((MEM_LEN, D_MODEL), BF16), sq, sq, sq],
        compiler_params=_params(16),
    )(mem, gain, wkv_bf, kg4, bd)


def _mem_fwd(proj, mk, mv, qg4, bd):
    S = proj.shape[0]
    tm = 1024

    def body(q_ref, g_ref, mk_ref, mv_ref, qg_ref, bd_ref, y_ref, om_ref):
        qv = q_ref[...]
        ms = _split_dot(qv * qv, bd_ref[...]) * (1.0 / HEAD_DIM)
        qs = (qv * lax.rsqrt(ms + EPS)) * (qg_ref[...] * SCALE)
        mkb = mk_ref[...].astype(BF16)
        mvb = mv_ref[...].astype(BF16)
        head = _head_index((tm, MEM_WIDTH))
        o = jnp.zeros((tm, MEM_WIDTH), F32)
        for h in range(4):
            s = _nt(jnp.where(head == h, qs, 0.0).astype(BF16), mkb)
            e = jnp.exp(s - jnp.max(s, axis=-1, keepdims=True))
            p = e * (1.0 / jnp.sum(e, axis=-1, keepdims=True))
            o = jnp.where(head == h, jnp.dot(p.astype(BF16), mvb, preferred_element_type=F32), o)
        om_ref[...] = o
        silu, _ = _silu_parts(g_ref[...])
        y_ref[...] = (o * silu).astype(BF16)

    col = lambda j: pl.BlockSpec((tm, MEM_WIDTH), lambda i, j=j: (i, j))
    const = lambda shape: pl.BlockSpec(shape, lambda i: (0,) * len(shape))
    tile = pl.BlockSpec((tm, MEM_WIDTH), lambda i: (i, 0))
    return pl.pallas_call(
        body, name="mem_fwd", grid=(S // tm,),
        in_specs=[col(11), col(12), const((MEM_LEN, MEM_WIDTH)), const((MEM_LEN, MEM_WIDTH)), const((1, MEM_WIDTH)),
                  const((MEM_WIDTH, MEM_WIDTH))],
        out_specs=[tile, tile],
        out_shape=[pltpu.HBM((S, MEM_WIDTH), BF16), pltpu.HBM((S, MEM_WIDTH), F32)],
        compiler_params=_params(24, ("arbitrary",)),
    )(*_hbm(proj, proj, mk, mv, qg4, bd))


def _mem_bwd(proj, dycat, om, mk, mv, qg4, bd):
    S = proj.shape[0]
    tm = 1024

    def body(q_ref, g_ref, dy_ref, om_ref, mk_ref, mv_ref, qg_ref, bd_ref,
             dq_ref, dg_ref, dmk_ref, dmv_ref, dqg_ref):
        i = pl.program_id(0)

        @pl.when(i == 0)
        def _():
            dmk_ref[...] = jnp.zeros_like(dmk_ref)
            dmv_ref[...] = jnp.zeros_like(dmv_ref)
            dqg_ref[...] = jnp.zeros_like(dqg_ref)

        bdv = bd_ref[...]
        qv = q_ref[...]
        rq = lax.rsqrt(_split_dot(qv * qv, bdv) * (1.0 / HEAD_DIM) + EPS)
        qh = qv * rq
        qs = qh * (qg_ref[...] * SCALE)
        silu, dsilu = _silu_parts(g_ref[...])
        dy = dy_ref[...]
        o = om_ref[...]
        do = dy * silu
        dg_ref[...] = (dy * o * dsilu).astype(BF16)
        dd = _split_dot(do * o, bdv)
        mkb = mk_ref[...].astype(BF16)
        mvb = mv_ref[...].astype(BF16)
        head = _head_index((tm, MEM_WIDTH))
        dqs = jnp.zeros((tm, MEM_WIDTH), F32)
        for h in range(4):
            qhd = jnp.where(head == h, qs, 0.0).astype(BF16)
            doh = jnp.where(head == h, do, 0.0).astype(BF16)
            s = _nt(qhd, mkb)
            e = jnp.exp(s - jnp.max(s, axis=-1, keepdims=True))
            p = e * (1.0 / jnp.sum(e, axis=-1, keepdims=True))
            ds = p * (_nt(doh, mvb) - dd[:, h * HEAD_DIM:h * HEAD_DIM + 1])
            dsb = ds.astype(BF16)
            dmv_ref[...] += _tn(p.astype(BF16), doh)
            dmk_ref[...] += _tn(dsb, qhd)
            dqs = jnp.where(head == h, jnp.dot(dsb, mkb, preferred_element_type=F32), dqs)
        dqs = dqs * SCALE
        a = dqs * qg_ref[...]
        dq_ref[...] = (rq * (a - qh * (_split_dot(a * qh, bdv) * (1.0 / HEAD_DIM)))).astype(BF16)
        dqg_ref[...] += jnp.sum(dqs * qh, axis=0, keepdims=True)

    col = lambda j: pl.BlockSpec((tm, MEM_WIDTH), lambda i, j=j: (i, j))
    const = lambda shape: pl.BlockSpec(shape, lambda i: (0,) * len(shape))
    tile = pl.BlockSpec((tm, MEM_WIDTH), lambda i: (i, 0))
    piece = pltpu.HBM((S, MEM_WIDTH), BF16)
    sq = pltpu.HBM((MEM_LEN, MEM_WIDTH), F32)
    return pl.pallas_call(
        body, name="mem_bwd", grid=(S // tm,),
        in_specs=[col(11), col(12), col(3), tile, const((MEM_LEN, MEM_WIDTH)), const((MEM_LEN, MEM_WIDTH)),
                  const((1, MEM_WIDTH)), const((MEM_WIDTH, MEM_WIDTH))],
        out_specs=[tile, tile, const((MEM_LEN, MEM_WIDTH)), const((MEM_LEN, MEM_WIDTH)), const((1, MEM_WIDTH))],
        out_shape=[piece, piece, sq, sq, pltpu.HBM((1, MEM_WIDTH), F32)],
        compiler_params=_params(32, ("arbitrary",)),
    )(*_hbm(proj, proj, dycat, om, mk, mv, qg4, bd))


def _mem_kv_bwd(dmk, dmv, kraw, mem, gain, kg4, wkv_bf, hm_bf, bd):
    def body(dmk_ref, dmv_ref, kraw_ref, mem_ref, g_ref, kg_ref, w_ref, hm_ref, bd_ref, dw_ref, dg_ref, dkg_ref):
        bdv = bd_ref[...]
        kraw = kraw_ref[...]
        rk = lax.rsqrt(_split_dot(kraw * kraw, bdv) * (1.0 / HEAD_DIM) + EPS)
        kh = kraw * rk
        dmkv = dmk_ref[...]
        a = dmkv * kg_ref[...]
        dkraw = rk * (a - kh * (_split_dot(a * kh, bdv) * (1.0 / HEAD_DIM)))
        dkg_ref[...] = jnp.sum(dmkv * kh, axis=0, keepdims=True)
        dkv = jnp.concatenate([dkraw, dmv_ref[...]], axis=1).astype(BF16)
        dw = _tn(hm_ref[...], dkv).astype(BF16)
        rows_blk = D_MODEL // N_DEV
        for j in range(N_DEV):
            dw_ref[j] = dw[rows_blk * j:rows_blk * (j + 1)]
        dhm = _nt(dkv, w_ref[...])
        mv_ = mem_ref[...]
        r = lax.rsqrt(jnp.mean(mv_ * mv_, axis=-1, keepdims=True) + EPS)
        dg_ref[...] = jnp.sum(dhm * (mv_ * r), axis=0, keepdims=True)

    return pl.pallas_call(
        body, name="mem_kv_bwd",
        out_shape=[jax.ShapeDtypeStruct((N_DEV, D_MODEL // N_DEV, 2 * MEM_WIDTH), BF16),
                   jax.ShapeDtypeStruct((1, D_MODEL), F32), jax.ShapeDtypeStruct((1, MEM_WIDTH), F32)],
        compiler_params=_params(24),
    )(dmk, dmv, kraw, mem, gain, kg4, wkv_bf, hm_bf, bd)


def _out_loss(yg, ya, ym, x, tgt, wout_bf):
    S, D = x.shape
    tm = 512
    nsteps = S // tm
    rows_blk = D // N_DEV

    def body(yg_ref, ya_ref, ym_ref, x_ref, t_ref, w_ref, dout_ref, dycat_ref, dw_ref, loss_ref, acc_ref):
        i = pl.program_id(0)

        @pl.when(i == 0)
        def _():
            acc_ref[...] = jnp.zeros_like(acc_ref)
            loss_ref[...] = jnp.zeros_like(loss_ref)

        ycat = jnp.concatenate([yg_ref[...], ya_ref[...], ym_ref[...]], axis=1)
        w = w_ref[...]
        diff = (x_ref[...] + jnp.dot(ycat, w, preferred_element_type=F32)) - t_ref[...]
        loss_ref[...] += jnp.sum(diff * diff, axis=0, keepdims=True)
        dout = diff * (1.0 / D)
        dout_ref[...] = dout
        db = dout.astype(BF16)
        dycat_ref[...] = _nt(db, w)
        acc_ref[...] += _tn(ycat, db)

        @pl.when(i == nsteps - 1)
        def _():
            for j in range(N_DEV):
                dw_ref[j] = acc_ref[rows_blk * j:rows_blk * (j + 1), :].astype(BF16)

    tile = lambda w: pl.BlockSpec((tm, w), lambda i: (i, 0))
    const = lambda shape: pl.BlockSpec(shape, lambda i: (0,) * len(shape))
    return pl.pallas_call(
        body, name="out_loss", grid=(nsteps,),
        in_specs=[tile(GMLP_WIDTH), tile(ATTN_WIDTH), tile(MEM_WIDTH), tile(D), tile(D), const((D, D))],
        out_specs=[tile(D), tile(D), const((N_DEV, rows_blk, D)), const((1, D))],
        out_shape=[pltpu.HBM((S, D), F32), pltpu.HBM((S, D), F32),
                   pltpu.HBM((N_DEV, rows_blk, D), BF16), pltpu.HBM((1, D), F32)],
        scratch_shapes=[pltpu.VMEM((D, D), F32)],
        compiler_params=_params(40, ("arbitrary",)),
    )(*_hbm(yg, ya, ym, x, tgt, wout_bf))


def _piece_specs(pieces, tm):
    return [pl.BlockSpec((tm, p.shape[1]), lambda i: (i, 0)) for p in pieces]


def _in_bwd_dx(pieces, x, dout, gain, w_t, dw_blocks):
    S, D = x.shape
    N = w_t.shape[0]
    tm = 256
    n = len(pieces)
    nsteps = S // tm
    middle_step = nsteps // 8

    def body(*refs):
        piece_refs = refs[:n]
        x_ref, dout_ref, g_ref, w_ref, dwb_ref, gx_ref, dg_ref, gw_ref = refs[n:n + 8]
        rs = _ReduceScatter([dwb_ref], [gw_ref], *refs[n + 8:])
        i = pl.program_id(0)

        @pl.when(i == 0)
        def _():
            dg_ref[...] = jnp.zeros_like(dg_ref)
            rs.start()

        @pl.when(i == middle_step)
        def _():
            rs.middle()

        dproj = jnp.concatenate([r[...] for r in piece_refs], axis=1)
        dh = jnp.dot(dproj, w_ref[...], preferred_element_type=F32)
        xv = x_ref[...]
        r = lax.rsqrt(jnp.mean(xv * xv, axis=-1, keepdims=True) + EPS)
        xh = xv * r
        a = dh * g_ref[...]
        gx_ref[...] = dout_ref[...] + r * (a - xh * jnp.mean(a * xh, axis=-1, keepdims=True))
        dg_ref[...] += jnp.sum(dh * xh, axis=0, keepdims=True)

        @pl.when(i == nsteps - 1)
        def _():
            rs.finish()

    tile = pl.BlockSpec((tm, D), lambda i: (i, 0))
    const = lambda shape: pl.BlockSpec(shape, lambda i: (0,) * len(shape))
    vmem = pl.BlockSpec(memory_space=pltpu.VMEM)
    return pl.pallas_call(
        body, name="in_bwd_dx", grid=(nsteps,),
        in_specs=_piece_specs(pieces, tm)
        + [tile, tile, const((1, D)), pl.BlockSpec((N, D), lambda i: (0, 0), pipeline_mode=pl.Buffered(1)), vmem],
        out_specs=[tile, const((1, D)), vmem],
        out_shape=[pltpu.HBM((S, D), F32), pltpu.HBM((1, D), F32), jax.ShapeDtypeStruct(dw_blocks.shape[1:], F32)],
        scratch_shapes=_reduce_scatter_scratch([dw_blocks]),
        compiler_params=_params(56, ("arbitrary",)),
    )(*_hbm(*pieces, x, dout, gain, w_t), dw_blocks)


def _in_bwd_dw(pieces, h_bf, others):
    S, D = h_bf.shape
    N = sum(p.shape[1] for p in pieces)
    n_blk = N // N_DEV
    tm = 512
    n = len(pieces)
    k = len(others)
    nsteps = S // tm

    def body(*refs):
        piece_refs = refs[:n]
        h_ref = refs[n]
        other_refs = refs[n + 1:n + 1 + k]
        dw_ref = refs[n + 1 + k]
        sum_refs = refs[n + 2 + k:n + 2 + 2 * k]
        acc_ref = refs[n + 2 + 2 * k]
        rs = _ReduceScatter(other_refs, sum_refs, *refs[n + 3 + 2 * k:])
        i = pl.program_id(0)

        @pl.when(i == 0)
        def _():
            acc_ref[...] = jnp.zeros_like(acc_ref)
            rs.start()

        @pl.when(i == 1)
        def _():
            rs.middle()

        dproj = jnp.concatenate([r[...] for r in piece_refs], axis=1)
        acc_ref[...] += _tn(h_ref[...], dproj)

        @pl.when(i == nsteps - 1)
        def _():
            for j in range(N_DEV):
                dw_ref[j] = acc_ref[:, n_blk * j:n_blk * (j + 1)].T.astype(BF16)
            rs.finish()

    vmem = pl.BlockSpec(memory_space=pltpu.VMEM)
    return pl.pallas_call(
        body, name="in_bwd_dw", grid=(nsteps,),
        in_specs=_piece_specs(pieces, tm) + [pl.BlockSpec((tm, D), lambda i: (i, 0))] + [vmem] * k,
        out_specs=[pl.BlockSpec((N_DEV, n_blk, D), lambda i: (0, 0, 0))] + [vmem] * k,
        out_shape=[pltpu.HBM((N_DEV, n_blk, D), BF16)] + [jax.ShapeDtypeStruct(o.shape[1:], F32) for o in others],
        scratch_shapes=[pltpu.VMEM((D, N), F32)] + _reduce_scatter_scratch(others),
        compiler_params=_params(56, ("arbitrary",)),
    )(*_hbm(*pieces, h_bf), *others)


def _row_step(m):
    return max(t for t in range(16, 257, 16) if m % t == 0)


def _place():
    x, y, c = lax.axis_index("x"), lax.axis_index("y"), lax.axis_index("c")
    chips = [(1 - x, y), (x, 1 - y), (1 - x, 1 - y)]
    return x, y, c, chips


class _AllGather:
    def __init__(self, srcs, outs, send_sems, recv_sems, local_sems, first_sem=0):
        self.srcs, self.outs, self.n, self.first_sem = srcs, outs, len(srcs), first_sem
        self.send_sems, self.recv_sems, self.local_sems = send_sems, recv_sems, local_sems

    def _rows(self, a, px, py, pc):
        m = self.srcs[a].shape[0]
        return self.outs[a].at[pl.ds((4 * px + 2 * py + pc) * m, m), :]

    def _copy(self, a, k, block, to, src=None):
        row = self.first_sem + a
        return pltpu.make_async_remote_copy(
            src_ref=self._rows(a, *block) if src is None else src, dst_ref=self._rows(a, *block),
            send_sem=self.send_sems.at[row, k], recv_sem=self.recv_sems.at[row, k], device_id=to, device_id_type=MESH)

    def _mine(self):
        x, y, c, _ = _place()
        return [pltpu.make_async_copy(self.srcs[a], self._rows(a, x, y, c), self.local_sems.at[self.first_sem + a])
                for a in range(self.n)]

    def _first(self, far):
        x, y, c, chips = _place()
        out = []
        for a in range(self.n):
            if far:
                out.append(self._copy(a, 3, (x, y, c), (*chips[2], c), src=self.srcs[a]))
            else:
                out.append(self._copy(a, 0, (x, y, c), (x, y, 1 - c), src=self.srcs[a]))
                out += [self._copy(a, 1 + j, (x, y, c), (*chips[j], c), src=self.srcs[a]) for j in (1, 0)]
        return out

    def _passed(self, j):
        x, y, c, chips = _place()
        return [self._copy(a, 4 + j, (*chips[j], c), (x, y, 1 - c)) for a in range(self.n)]

    def start(self):
        for cp in self._mine() + self._first(far=False):
            cp.start()

    def start_far(self):
        for cp in self._first(far=True):
            cp.start()

    def from_chip(self, j):
        x, y, c, chips = _place()
        for a in range(self.n):
            self._copy(a, 1 + j, (*chips[j], c), (x, y, c)).wait_recv()
        for cp in self._passed(j):
            cp.start()

    def from_sibling(self, j=None):
        x, y, c, chips = _place()
        for a in range(self.n):
            block = (x, y, 1 - c) if j is None else (*chips[j], 1 - c)
            self._copy(a, 0 if j is None else 4 + j, block, (x, y, c)).wait_recv()

    def from_self(self):
        for cp in self._mine():
            cp.wait()

    def finish(self):
        for cp in (self._first(far=False) + self._first(far=True)
                   + self._passed(0) + self._passed(1) + self._passed(2)):
            cp.wait_send()

    def run(self):
        self.start()
        self.start_far()
        self.from_self()
        for j in range(3):
            self.from_chip(j)
        self.from_sibling()
        for j in range(3):
            self.from_sibling(j)
        self.finish()


def _gather_proj(x, gain, shards, xpos):
    S, D = x.shape
    n = len(shards)
    N = N_DEV * shards[0].shape[0]
    half = N // 2
    tm = 1024
    nsteps = S // tm

    def body(*refs):
        xpos_ref, x_ref, g_ref = refs[:3]
        ins = refs[3:3 + n]
        proj_ref, h_ref = refs[3 + n:5 + n]
        outs = refs[5 + n:5 + 2 * n]
        casts = refs[5 + 2 * n:5 + 3 * n]
        whole = refs[5 + 3 * n:5 + 4 * n]
        sems = refs[5 + 4 * n:8 + 4 * n]
        ag = _AllGather(casts[:1], whole[:1], *sems)
        later = _AllGather(casts[1:], whole[1:], *sems, first_sem=1)
        out_sems, h_all = refs[8 + 4 * n:]
        p, i = pl.program_id(0), pl.program_id(1)
        rows = pl.ds(pl.multiple_of(i * tm, tm), tm)

        @pl.when((p == 0) & (i == 0))
        def _():
            for a in range(n):
                tr = _row_step(ins[a].shape[0])

                def cast(r, carry, a=a, tr=tr):
                    at = pl.ds(pl.multiple_of(r * tr, tr), tr)
                    casts[a][at, :] = ins[a][at, :].astype(BF16)
                    return carry
                lax.fori_loop(0, ins[a].shape[0] // tr, cast, 0)
            ag.start()

        @pl.when(p == 0)
        def _():
            xv = x_ref[...]
            r = lax.rsqrt(jnp.mean(xv * xv, axis=-1, keepdims=True) + EPS)
            h = ((xv * r) * g_ref[...]).astype(BF16)
            h_ref[...] = h
            h_all[rows, :] = h

        @pl.when((p == 1) & (i == 0))
        def _():
            ag.from_self()
            ag.from_chip(1)
            ag.start_far()
            later.start()
            later.start_far()
            ag.from_sibling()
            ag.from_sibling(1)

        @pl.when((p == 2) & (i == 0))
        def _():
            for j in (0, 2):
                ag.from_chip(j)
            for j in (0, 2):
                ag.from_sibling(j)

        @pl.when(p > 0)
        def _():
            which = (xpos_ref[0] + p - 1) % 2
            w_half = whole[0][pl.ds(pl.multiple_of(which * half, half), half), :]
            proj_ref[...] = _nt(h_all[rows, :], w_half)

        @pl.when((p == 2) & (i == nsteps - 1))
        def _():
            ag.finish()
            later.from_self()
            for j in range(3):
                later.from_chip(j)
            later.from_sibling()
            for j in range(3):
                later.from_sibling(j)
            later.finish()
            to_results = [pltpu.make_async_copy(whole[a], outs[a], out_sems.at[a]) for a in range(n)]
            for cp in to_results:
                cp.start()
            for cp in to_results:
                cp.wait()

    vmem = pl.BlockSpec(memory_space=pltpu.VMEM)
    hbm = pl.BlockSpec(memory_space=pl.ANY)
    gathered = [(N_DEV * a.shape[0], a.shape[1]) for a in shards]
    x_tile = lambda p, i, xp: (jnp.where(p == 0, i, nsteps - 1), 0)
    proj_tile = lambda p, i, xp: (jnp.where(p == 0, 0, i), (xp[0] + jnp.maximum(p - 1, 0)) % 2)
    grid_spec = pltpu.PrefetchScalarGridSpec(
        num_scalar_prefetch=1, grid=(3, nsteps),
        in_specs=[pl.BlockSpec((tm, D), x_tile), pl.BlockSpec((1, D), lambda p, i, xp: (0, 0))] + [vmem] * n,
        out_specs=[pl.BlockSpec((tm, half), proj_tile), pl.BlockSpec((tm, D), x_tile)] + [hbm] * n,
        scratch_shapes=[pltpu.VMEM(a.shape, BF16) for a in shards] + [pltpu.VMEM(g, BF16) for g in gathered]
        + [pltpu.SemaphoreType.DMA((n, 7)), pltpu.SemaphoreType.DMA((n, 7)), pltpu.SemaphoreType.DMA((n,)),
           pltpu.SemaphoreType.DMA((n,)), pltpu.VMEM((S, D), BF16)])
    return pl.pallas_call(
        body, name="gather_proj", grid_spec=grid_spec,
        out_shape=[pltpu.HBM((S, N), F32), pltpu.HBM((S, D), BF16)] + [pltpu.HBM(g, BF16) for g in gathered],
        compiler_params=_params(56, ("arbitrary", "arbitrary")),
    )(xpos, *_hbm(x, gain), *shards)


ROW_NORM, ROW_MEM_NORM, ROW_V_GAIN, ROW_B, ROW_ATTN_GAINS, ROW_MEM_GAINS, ROW_W_S, ROW_LOSS = 0, 8, 16, 18, 22, 23, 24, 536
SMALL_ROWS = 544


def _gather_small(dgain, dmgain, dvg, db2, dqg, dkg, dmqg, dmkg, dws, sq):
    def body(dgain_ref, dmgain_ref, dvg_ref, db2_ref, dqg_ref, dkg_ref, dmqg_ref, dmkg_ref, dws_ref, sq_ref,
             out_ref, mine, send_sems, recv_sems, local_sems):
        first = lax.broadcasted_iota(jnp.int32, (1, 128), 1) < HEAD_DIM
        for i in range(8):
            cols = slice(128 * i, 128 * (i + 1))
            mine[ROW_NORM + i:ROW_NORM + i + 1, :] = dgain_ref[:, cols]
            mine[ROW_MEM_NORM + i:ROW_MEM_NORM + i + 1, :] = dmgain_ref[:, cols]
            mine[ROW_LOSS + i:ROW_LOSS + i + 1, :] = sq_ref[:, cols]
        mine[ROW_V_GAIN:ROW_V_GAIN + 1, :] = dvg_ref[:, 0:128]
        mine[ROW_V_GAIN + 1:ROW_V_GAIN + 2, :] = dvg_ref[:, 128:256]
        bt = db2_ref[...].T
        for h in range(4):
            mine[ROW_B + h:ROW_B + h + 1, :] = bt[HEAD_DIM * h:HEAD_DIM * h + 1, :]

        def fold_heads(t):
            return t + pltpu.roll(t, HEAD_DIM, axis=1)
        aq = fold_heads(dqg_ref[0] + dqg_ref[1] + dqg_ref[2] + dqg_ref[3])
        ak = fold_heads(dkg_ref[0] + dkg_ref[1] + dkg_ref[2] + dkg_ref[3])
        mine[ROW_ATTN_GAINS:ROW_ATTN_GAINS + 1, :] = jnp.where(first, aq, ak)
        mq = fold_heads(dmqg_ref[:, 0:128] + dmqg_ref[:, 128:256])
        mk = fold_heads(dmkg_ref[:, 0:128] + dmkg_ref[:, 128:256])
        mine[ROW_MEM_GAINS:ROW_MEM_GAINS + 1, :] = jnp.where(first, mq, mk)
        mine[ROW_W_S:ROW_W_S + 4 * CHUNK, :] = dws_ref[...]
        _AllGather([mine], [out_ref], send_sems, recv_sems, local_sems).run()

    return pl.pallas_call(
        body, name="gather_small_grads",
        out_shape=jax.ShapeDtypeStruct((N_DEV * SMALL_ROWS, 128), F32),
        scratch_shapes=[pltpu.VMEM((SMALL_ROWS, 128), F32), pltpu.SemaphoreType.DMA((1, 7)),
                        pltpu.SemaphoreType.DMA((1, 7)), pltpu.SemaphoreType.DMA((1,))],
        compiler_params=_params(16),
    )(dgain, dmgain, dvg, db2, dqg, dkg, dmqg, dmkg, dws, sq)


def _reduce_scatter_scratch(arrs):
    n = len(arrs)
    return ([pltpu.VMEM((4,) + a.shape[1:], BF16) for a in arrs] + [pltpu.VMEM((3,) + a.shape[1:], BF16) for a in arrs]
            + [pltpu.SemaphoreType.DMA((n, 7)), pltpu.SemaphoreType.DMA((n, 7))])


class _ReduceScatter:
    def __init__(self, ins, outs, *scratch):
        n = len(ins)
        self.n, self.ins, self.outs = n, ins, outs
        self.half, self.quarter = scratch[:n], scratch[n:2 * n]
        self.send_sems, self.recv_sems = scratch[2 * n:]

    def _to_sibling(self):
        x, y, c, _ = _place()
        return [pltpu.make_async_remote_copy(
            src_ref=self.ins[a].at[2 * q + (1 - c)], dst_ref=self.half[a].at[q], send_sem=self.send_sems.at[a, q],
            recv_sem=self.recv_sems.at[a, q], device_id=(x, y, 1 - c), device_id_type=MESH)
            for a in range(self.n) for q in range(4)]

    def _to_chips(self):
        _, _, c, chips = _place()
        return [pltpu.make_async_remote_copy(
            src_ref=self.half[a].at[2 * chip[0] + chip[1]], dst_ref=self.quarter[a].at[k],
            send_sem=self.send_sems.at[a, 4 + k], recv_sem=self.recv_sems.at[a, 4 + k], device_id=(*chip, c),
            device_id_type=MESH) for a in range(self.n) for k, chip in enumerate(chips)]

    def _rows(self, a, fn):
        m = self.ins[a].shape[1]
        tr = _row_step(m)

        def step(i, carry):
            fn(pl.ds(pl.multiple_of(i * tr, tr), tr))
            return carry
        lax.fori_loop(0, m // tr, step, 0)

    def start(self):
        for cp in self._to_sibling():
            cp.start()

    def middle(self):
        _, _, c, _ = _place()
        for cp in self._to_sibling():
            cp.wait_recv()
        for a in range(self.n):
            for q in range(4):
                def add_half(rows, a=a, q=q):
                    both = self.ins[a][2 * q + c, rows, :].astype(F32) + self.half[a][q, rows, :].astype(F32)
                    self.half[a][q, rows, :] = both.astype(BF16)
                self._rows(a, add_half)
        for cp in self._to_chips():
            cp.start()

    def finish(self):
        x, y, _, _ = _place()
        for cp in self._to_chips():
            cp.wait_recv()
        for a in range(self.n):
            def add_quarters(rows, a=a):
                f = lambda t: t.astype(F32)
                self.outs[a][rows, :] = ((f(self.half[a][2 * x + y, rows, :]) + f(self.quarter[a][0, rows, :]))
                                         + (f(self.quarter[a][1, rows, :]) + f(self.quarter[a][2, rows, :])))
            self._rows(a, add_quarters)
        for cp in self._to_sibling() + self._to_chips():
            cp.wait_send()


def _adamw_math(w, g, m, v):
    m = ADAM_B1 * m + (1.0 - ADAM_B1) * g
    v = ADAM_B2 * v + (1.0 - ADAM_B2) * (g * g)
    m_hat = m / (1.0 - ADAM_B1 ** ADAM_STEP)
    v_hat = v / (1.0 - ADAM_B2 ** ADAM_STEP)
    delta = -ADAM_LR * (m_hat / (jnp.sqrt(v_hat) + ADAM_EPS) + ADAM_WD * w)
    return delta, m, v


def _adamw(w, g, m, v, name):
    R, C = w.shape
    tr = _row_step(R)

    def body(w_ref, g_ref, m_ref, v_ref, d_ref, nm_ref, nv_ref):
        d_ref[...], nm_ref[...], nv_ref[...] = _adamw_math(w_ref[...], g_ref[...], m_ref[...], v_ref[...])

    tile = pl.BlockSpec((tr, C), lambda i: (i, 0))
    out = pltpu.HBM((R, C), F32)
    return pl.pallas_call(
        body, name=name, grid=(R // tr,), in_specs=[tile] * 4, out_specs=[tile] * 3, out_shape=[out] * 3,
        compiler_params=_params(16, ("arbitrary",)),
    )(*_hbm(w, g, m, v))


SMALL = ("norm_gain", "gmlp_v_gain", "gmlp_w_s", "gmlp_b", "attn_q_gain", "attn_k_gain", "mem_norm_gain",
         "mem_q_gain", "mem_k_gain")
WEIGHTS = ("norm_gain", "w_in", "gmlp_v_gain", "gmlp_w_s", "gmlp_b", "attn_q_gain", "attn_k_gain",
           "mem_norm_gain", "w_mem_kv", "mem_q_gain", "mem_k_gain", "w_out")


def _adamw_small(w, m, v, g_all):
    k = len(SMALL)
    half = slice(0, HEAD_DIM), slice(HEAD_DIM, 2 * HEAD_DIM)

    def body(*refs):
        w_refs, m_refs, v_refs = refs[:k], refs[k:2 * k], refs[2 * k:3 * k]
        g_ref = refs[3 * k]
        outs = refs[3 * k + 1:7 * k + 1]
        loss_ref, gsum = refs[7 * k + 1:]

        part = SMALL_ROWS // 4
        for p in range(4):
            acc = g_ref[part * p:part * (p + 1), :]
            for dev in range(1, N_DEV):
                acc = acc + g_ref[dev * SMALL_ROWS + part * p:dev * SMALL_ROWS + part * (p + 1), :]
            gsum[part * p:part * (p + 1), :] = acc

        def update(name, at, g):
            i = SMALL.index(name)
            d, nm, nv = _adamw_math(w_refs[i][at], g, m_refs[i][at], v_refs[i][at])
            outs[i][at], outs[k + i][at], outs[2 * k + i][at], outs[3 * k + i][at] = g, d, nm, nv

        for i in range(8):
            at = (slice(0, 1), slice(128 * i, 128 * (i + 1)))
            update("norm_gain", at, gsum[ROW_NORM + i:ROW_NORM + i + 1, :])
            update("mem_norm_gain", at, gsum[ROW_MEM_NORM + i:ROW_MEM_NORM + i + 1, :])
        for h in range(4):
            row = (0, slice(h, h + 1), slice(None))
            update("gmlp_v_gain", row, gsum[ROW_V_GAIN + h // 2:ROW_V_GAIN + h // 2 + 1, half[h % 2]])
            update("gmlp_b", row, gsum[ROW_B + h:ROW_B + h + 1, :])
            update("gmlp_w_s", (0, h), gsum[ROW_W_S + CHUNK * h:ROW_W_S + CHUNK * (h + 1), :])
        whole = (slice(0, 1), slice(None))
        update("attn_q_gain", whole, gsum[ROW_ATTN_GAINS:ROW_ATTN_GAINS + 1, half[0]])
        update("attn_k_gain", whole, gsum[ROW_ATTN_GAINS:ROW_ATTN_GAINS + 1, half[1]])
        update("mem_q_gain", whole, gsum[ROW_MEM_GAINS:ROW_MEM_GAINS + 1, half[0]])
        update("mem_k_gain", whole, gsum[ROW_MEM_GAINS:ROW_MEM_GAINS + 1, half[1]])
        loss_ref[...] = jnp.sum(gsum[ROW_LOSS:ROW_LOSS + 8, :], keepdims=True) * (0.5 / D_MODEL)

    shapes = [jax.ShapeDtypeStruct(w[name].shape, F32) for name in SMALL]
    res = pl.pallas_call(
        body, name="adamw_small",
        out_shape=shapes * 4 + [jax.ShapeDtypeStruct((1, 1), F32)],
        scratch_shapes=[pltpu.VMEM((SMALL_ROWS, 128), F32)],
        compiler_params=_params(16),
    )(*[w[n] for n in SMALL], *[m[n] for n in SMALL], *[v[n] for n in SMALL], g_all)
    trees = [dict(zip(SMALL, res[j * k:(j + 1) * k])) for j in range(4)]
    return (*trees, res[4 * k])


def _grads(x, mem, tgt, w, shards):
    bd128, bd256 = _head_blockdiag(128), _head_blockdiag(256)
    gain = w["norm_gain"].reshape(1, D_MODEL)
    vg = w["gmlp_v_gain"].reshape(1, GMLP_WIDTH)
    w_s = w["gmlp_w_s"].reshape(4, CHUNK, CHUNK)
    b2 = jnp.repeat(w["gmlp_b"].reshape(4, CHUNK).T, HEAD_DIM, axis=1)
    qg2 = jnp.tile(w["attn_q_gain"].reshape(1, HEAD_DIM), (1, 2))
    kg2 = jnp.tile(w["attn_k_gain"].reshape(1, HEAD_DIM), (1, 2))
    mqg4 = jnp.tile(w["mem_q_gain"].reshape(1, HEAD_DIM), (1, 4))
    mkg4 = jnp.tile(w["mem_k_gain"].reshape(1, HEAD_DIM), (1, 4))
    mgain = w["mem_norm_gain"].reshape(1, D_MODEL)

    xpos = lax.axis_index("x").astype(jnp.int32).reshape(1)
    proj, h_bf, win_t, wkv_bf, wout_bf = _gather_proj(x, gain, shards, xpos)
    yg = _gmlp_fwd(proj, vg, w_s, b2, bd256)
    ya, att, lse = _attn_fwd(proj, qg2, kg2, bd128)
    hm_bf, kraw, mk, mv = _mem_kv(mem, mgain, wkv_bf, mkg4, bd256)
    ym, om = _mem_fwd(proj, mk, mv, mqg4, bd256)
    dout, dycat, dwout, sq = _out_loss(yg, ya, ym, x, tgt, wout_bf)

    du, dgv, dgg, dws, db2, dvg = _gmlp_bwd(proj, dycat, vg, w_s, b2, bd256)
    dq, dk, dv, dag, dqg, dkg = _attn_bwd(proj, dycat, att, lse, qg2, kg2, bd128)
    dmq, dmg, dmk, dmv, dmqg = _mem_bwd(proj, dycat, om, mk, mv, mqg4, bd256)
    dwkv, dmgain, dmkg = _mem_kv_bwd(dmk, dmv, kraw, mem, mgain, mkg4, wkv_bf, hm_bf, bd256)
    pieces = [du, dgv, dgg, dq, dk, dv, dag, dmq, dmg]
    dwin, g_wkv, g_wout = _in_bwd_dw(pieces, h_bf, [dwkv, dwout])
    grad_x, dgain, g_win = _in_bwd_dx(pieces, x, dout, gain, win_t, dwin)
    return grad_x, g_win, g_wkv, g_wout, (dgain, dmgain, dvg, db2, dqg, dkg, dmqg, dmkg, dws, sq)


def kernel(x, mem, norm_gain, w_in, gmlp_v_gain, gmlp_w_s, gmlp_b, attn_q_gain, attn_k_gain, mem_norm_gain, w_mem_kv, mem_q_gain, mem_k_gain, w_out, loss_target, m_norm_gain, m_w_in, m_gmlp_v_gain, m_gmlp_w_s, m_gmlp_b, m_attn_q_gain, m_attn_k_gain, m_mem_norm_gain, m_w_mem_kv, m_mem_q_gain, m_mem_k_gain, m_w_out, v_norm_gain, v_w_in, v_gmlp_v_gain, v_gmlp_w_s, v_gmlp_b, v_attn_q_gain, v_attn_k_gain, v_mem_norm_gain, v_w_mem_kv, v_mem_q_gain, v_mem_k_gain, v_w_out):
    w = dict(norm_gain=norm_gain, w_in=w_in, gmlp_v_gain=gmlp_v_gain, gmlp_w_s=gmlp_w_s, gmlp_b=gmlp_b,
             attn_q_gain=attn_q_gain, attn_k_gain=attn_k_gain, mem_norm_gain=mem_norm_gain, w_mem_kv=w_mem_kv,
             mem_q_gain=mem_q_gain, mem_k_gain=mem_k_gain, w_out=w_out)
    m = dict(norm_gain=m_norm_gain, w_in=m_w_in, gmlp_v_gain=m_gmlp_v_gain, gmlp_w_s=m_gmlp_w_s, gmlp_b=m_gmlp_b,
             attn_q_gain=m_attn_q_gain, attn_k_gain=m_attn_k_gain, mem_norm_gain=m_mem_norm_gain,
             w_mem_kv=m_w_mem_kv, mem_q_gain=m_mem_q_gain, mem_k_gain=m_mem_k_gain, w_out=m_w_out)
    v = dict(norm_gain=v_norm_gain, w_in=v_w_in, gmlp_v_gain=v_gmlp_v_gain, gmlp_w_s=v_gmlp_w_s, gmlp_b=v_gmlp_b,
             attn_q_gain=v_attn_q_gain, attn_k_gain=v_attn_k_gain, mem_norm_gain=v_mem_norm_gain,
             w_mem_kv=v_w_mem_kv, mem_q_gain=v_mem_q_gain, mem_k_gain=v_mem_k_gain, w_out=v_w_out)
    transposed = lambda t: jnp.transpose(t[0])

    grad_x, g_win, g_wkv, g_wout, small = _grads(
        x[0], mem[0], loss_target[0], w, [transposed(w_in), w_mem_kv[0], w_out[0]])
    small_all = _gather_small(*small)

    out_g, out_d, out_m, out_v, loss = _adamw_small(w, m, v, small_all)
    d_, m_, v_ = _adamw(transposed(w_in), g_win, transposed(m_w_in), transposed(v_w_in), "adamw_w_in")
    for tree, t in ((out_g, g_win), (out_d, d_), (out_m, m_), (out_v, v_)):
        tree["w_in"] = jnp.transpose(t)[None]
    for name, g in (("w_mem_kv", g_wkv), ("w_out", g_wout)):
        d_, m_, v_ = _adamw(w[name][0], g, m[name][0], v[name][0], "adamw_" + name)
        out_g[name], out_d[name], out_m[name], out_v[name] = g[None], d_[None], m_[None], v_[None]

    return (loss.reshape(()), grad_x[None], *[out_g[k] for k in WEIGHTS], *[out_d[k] for k in WEIGHTS],
            *[out_m[k] for k in WEIGHTS], *[out_v[k] for k in WEIGHTS])
```

```python
import functools
import math

import jax
import jax.numpy as jnp
from jax import lax
from jax.experimental import pallas as pl
from jax.experimental.pallas import tpu as pltpu

F32 = jnp.float32
BF16 = jnp.bfloat16

N_DEV = 8
D_MODEL = 1024
HEAD_DIM = 64
GMLP_WIDTH = 256
ATTN_WIDTH = 512
MEM_WIDTH = 256
MEM_LEN = 256
CHUNK = 128
BLOCK = 128
DILATIONS = (1, 4, 16)
CONFIG_ORDER = tuple(reversed(DILATIONS))
EPS = 1e-6
SCALE = 1.0 / math.sqrt(HEAD_DIM)
NEG = -1e30

ADAM_LR = 0.001
ADAM_B1 = 0.9
ADAM_B2 = 0.999
ADAM_EPS = 1e-08
ADAM_WD = 0.01
ADAM_STEP = 10

MIB = 1024 * 1024
MESH = pl.DeviceIdType.MESH

COL_AQ, COL_AK, COL_AV, COL_AG = 6, 10, 14, 18


def _params(vmem_mib, semantics=None):
    kw = dict(vmem_limit_bytes=vmem_mib * MIB)
    if semantics is not None:
        kw["dimension_semantics"] = semantics
    return pltpu.CompilerParams(**kw)


def _hbm(*arrs):
    return [pltpu.with_memory_space_constraint(a, pltpu.HBM) for a in arrs]


def _split_dot(x, sel_bf):
    hi = x.astype(BF16)
    lo = (x - hi.astype(F32)).astype(BF16)
    return jnp.dot(hi, sel_bf, preferred_element_type=F32) + jnp.dot(lo, sel_bf, preferred_element_type=F32)


def _nt(a, b):
    return lax.dot_general(a, b, (((1,), (1,)), ((), ())), preferred_element_type=F32)


def _tn(a, b):
    return lax.dot_general(a, b, (((0,), (0,)), ((), ())), preferred_element_type=F32)


def _silu_parts(g):
    sg = jax.nn.sigmoid(g)
    return g * sg, sg * (1.0 + g * (1.0 - sg))


def _head_index(shape):
    return lax.shift_right_logical(lax.broadcasted_iota(jnp.int32, shape, 1), HEAD_DIM.bit_length() - 1)


def _head_blockdiag(width):
    i = jnp.arange(width) // HEAD_DIM
    return (i[:, None] == i[None, :]).astype(BF16)


def _gmlp_masked_weights(ws_ref, transpose):
    t = lax.broadcasted_iota(jnp.int32, (CHUNK, CHUNK), 0)
    s = lax.broadcasted_iota(jnp.int32, (CHUNK, CHUNK), 1)
    parts = []
    for h in range(4):
        wm = jnp.where(s <= t, ws_ref[h], 0.0)
        parts.append(wm.T if transpose else wm)
    return jnp.concatenate(parts, axis=1).astype(BF16)


def _head_stack(v, head):
    return jnp.concatenate([jnp.where(head == h, v, 0.0) for h in range(4)], axis=0).astype(BF16)


def _gmlp_fwd(proj, vg, w_s, b2, bd):
    S = proj.shape[0]
    tm = 2048

    def body(u_ref, v_ref, g_ref, vg_ref, ws_ref, b2_ref, bd_ref, y_ref):
        v = v_ref[...]
        ms = _split_dot(v * v, bd_ref[...]) * (1.0 / HEAD_DIM)
        vn = (v * lax.rsqrt(ms + EPS)) * vg_ref[...]
        wcat = _gmlp_masked_weights(ws_ref, False)
        head = _head_index((CHUNK, GMLP_WIDTH))
        for c in range(tm // CHUNK):
            rows = slice(c * CHUNK, (c + 1) * CHUNK)
            sp = jnp.dot(wcat, _head_stack(vn[rows], head), preferred_element_type=F32) + b2_ref[...]
            silu, _ = _silu_parts(g_ref[rows, :])
            y_ref[rows, :] = ((u_ref[rows, :] * sp) * silu).astype(BF16)

    col = lambda j: pl.BlockSpec((tm, GMLP_WIDTH), lambda i, j=j: (i, j))
    const = lambda shape: pl.BlockSpec(shape, lambda i: (0,) * len(shape))
    return pl.pallas_call(
        body, name="gmlp_fwd", grid=(S // tm,),
        in_specs=[col(0), col(1), col(2), const((1, GMLP_WIDTH)), const((4, CHUNK, CHUNK)),
                  const((CHUNK, GMLP_WIDTH)), const((GMLP_WIDTH, GMLP_WIDTH))],
        out_specs=pl.BlockSpec((tm, GMLP_WIDTH), lambda i: (i, 0)),
        out_shape=pltpu.HBM((S, GMLP_WIDTH), BF16),
        compiler_params=_params(24, ("arbitrary",)),
    )(*_hbm(proj, proj, proj, vg, w_s, b2, bd))


def _gmlp_bwd(proj, dycat, vg, w_s, b2, bd):
    S = proj.shape[0]
    tm = 2048
    nsteps = S // tm

    def body(u_ref, v_ref, g_ref, dy_ref, vg_ref, ws_ref, b2_ref, bd_ref,
             du_ref, dv_ref, dg_ref, dws_ref, db2_ref, dvg_ref):
        i = pl.program_id(0)

        @pl.when(i == 0)
        def _():
            dws_ref[...] = jnp.zeros_like(dws_ref)
            db2_ref[...] = jnp.zeros_like(db2_ref)
            dvg_ref[...] = jnp.zeros_like(dvg_ref)

        bdv = bd_ref[...]
        v = v_ref[...]
        ms = _split_dot(v * v, bdv) * (1.0 / HEAD_DIM)
        rv = lax.rsqrt(ms + EPS)
        xhat = v * rv
        vgv = vg_ref[...]
        vn = xhat * vgv
        wcat = _gmlp_masked_weights(ws_ref, False)
        wcat_t = _gmlp_masked_weights(ws_ref, True)
        head = _head_index((CHUNK, GMLP_WIDTH))
        dvg = jnp.zeros((1, GMLP_WIDTH), F32)
        for c in range(tm // CHUNK):
            rows = slice(c * CHUNK, (c + 1) * CHUNK)
            vn_c = vn[rows]
            spb = jnp.dot(wcat, _head_stack(vn_c, head), preferred_element_type=F32) + b2_ref[...]
            silu, dsilu = _silu_parts(g_ref[rows, :])
            dy = dy_ref[rows, :]
            u = u_ref[rows, :]
            du_ref[rows, :] = (dy * spb * silu).astype(BF16)
            dg_ref[rows, :] = (dy * u * spb * dsilu).astype(BF16)
            dsp = dy * u * silu
            db2_ref[...] += dsp
            dstack = _head_stack(dsp, head)
            dvn = jnp.dot(wcat_t, dstack, preferred_element_type=F32)
            dws_ref[...] += _nt(dstack, vn_c.astype(BF16))
            xh = xhat[rows]
            a = dvn * vgv
            mean_ax = _split_dot(a * xh, bdv) * (1.0 / HEAD_DIM)
            dv_ref[rows, :] = (rv[rows] * (a - xh * mean_ax)).astype(BF16)
            dvg = dvg + jnp.sum(dvn * xh, axis=0, keepdims=True)
        dvg_ref[...] += dvg

        @pl.when(i == nsteps - 1)
        def _():
            t = lax.broadcasted_iota(jnp.int32, (4 * CHUNK, CHUNK), 0) % CHUNK
            s = lax.broadcasted_iota(jnp.int32, (4 * CHUNK, CHUNK), 1)
            dws_ref[...] = jnp.where(s <= t, dws_ref[...], 0.0)
            db2_ref[...] = _split_dot(db2_ref[...], bdv)

    col = lambda j: pl.BlockSpec((tm, GMLP_WIDTH), lambda i, j=j: (i, j))
    const = lambda shape: pl.BlockSpec(shape, lambda i: (0,) * len(shape))
    tile = pl.BlockSpec((tm, GMLP_WIDTH), lambda i: (i, 0))
    piece = pltpu.HBM((S, GMLP_WIDTH), BF16)
    return pl.pallas_call(
        body, name="gmlp_bwd", grid=(nsteps,),
        in_specs=[col(0), col(1), col(2), col(0), const((1, GMLP_WIDTH)), const((4, CHUNK, CHUNK)),
                  const((CHUNK, GMLP_WIDTH)), const((GMLP_WIDTH, GMLP_WIDTH))],
        out_specs=[tile, tile, tile, const((4 * CHUNK, CHUNK)), const((CHUNK, GMLP_WIDTH)), const((1, GMLP_WIDTH))],
        out_shape=[piece, piece, piece, pltpu.HBM((4 * CHUNK, CHUNK), F32),
                   pltpu.HBM((CHUNK, GMLP_WIDTH), F32), pltpu.HBM((1, GMLP_WIDTH), F32)],
        compiler_params=_params(32, ("arbitrary",)),
    )(*_hbm(proj, proj, proj, dycat, vg, w_s, b2, bd))


def _band_mask():
    qi = lax.broadcasted_iota(jnp.int32, (2 * BLOCK, 2 * BLOCK), 0) % BLOCK
    ki = lax.broadcasted_iota(jnp.int32, (2 * BLOCK, 2 * BLOCK), 1)
    return ((ki < BLOCK) & (ki >= qi)) | ((ki >= BLOCK) & ((ki - BLOCK) <= qi))


def _first_block_bias(blk, blocks_per_class):
    kcol = lax.broadcasted_iota(jnp.int32, (1, 2 * BLOCK), 1)
    kill = jnp.where((blk & (blocks_per_class - 1)) == 0, NEG, 0.0)
    return jnp.where(kcol < BLOCK, kill, 0.0)


def _two_heads(q, lo):
    zero = jnp.zeros_like(q)
    return jnp.concatenate([jnp.where(lo, q, zero), jnp.where(lo, zero, q)], axis=0)


def _block_tokens(blk, d, S):
    if d == 1:
        return pl.ds(pl.multiple_of(blk * BLOCK, BLOCK), BLOCK)
    blocks_per_class = S // d // BLOCK
    r = lax.shift_right_logical(blk, blocks_per_class.bit_length() - 1)
    n = blk & (blocks_per_class - 1)
    return pl.ds(r + n * (BLOCK * d), BLOCK, stride=d)


def _padded_block(blk):
    return pl.ds(pl.multiple_of((blk + 1) * BLOCK, BLOCK), BLOCK)


def _for_blocks(n_blocks, unroll, fn):
    def group(g, carry):
        for u in range(unroll):
            fn(g * unroll + u)
        return carry
    lax.fori_loop(0, n_blocks // unroll, group, 0)


def _attn_fwd(proj, qg2, kg2, bd):
    S = proj.shape[0]
    npairs = ATTN_WIDTH // 128
    tn = 512

    def body(q_ref, k_ref, v_ref, g_ref, qg_ref, kg_ref, bd_ref, y_ref, att_ref, lse_ref, qn, kn, kc, vc):
        bdv = bd_ref[...]
        lo = lax.broadcasted_iota(jnp.int32, (BLOCK, 128), 1) < HEAD_DIM
        band_mask = _band_mask()
        kc[pl.ds(0, BLOCK), :] = jnp.zeros((BLOCK, 128), BF16)
        vc[pl.ds(0, BLOCK), :] = jnp.zeros((BLOCK, 128), BF16)

        def norm_step(i, carry):
            rows = pl.ds(pl.multiple_of(i * tn, tn), tn)
            qv = q_ref[rows, :]
            kv = k_ref[rows, :]
            qn[rows, :] = (qv * lax.rsqrt(_split_dot(qv * qv, bdv) * (1.0 / HEAD_DIM) + EPS)) * (qg_ref[...] * SCALE)
            kn[rows, :] = (kv * lax.rsqrt(_split_dot(kv * kv, bdv) * (1.0 / HEAD_DIM) + EPS)) * kg_ref[...]
            return carry
        lax.fori_loop(0, S // tn, norm_step, 0)

        def fill(blk, d):
            tokens = _block_tokens(blk, d, S)
            kc[_padded_block(blk), :] = kn[tokens, :].astype(BF16)
            vc[_padded_block(blk), :] = v_ref[tokens, :].astype(BF16)

        ones_bf = jnp.ones((2 * BLOCK, 128), BF16)

        def block(blk, d):
            tokens = _block_tokens(blk, d, S)
            keys = pl.ds(pl.multiple_of(blk * BLOCK, BLOCK), 2 * BLOCK)
            q2 = _two_heads(qn[tokens, :].astype(BF16), lo)
            s = jnp.where(band_mask, _nt(q2, kc[keys, :]), NEG) + _first_block_bias(blk, S // d // BLOCK)
            m = jnp.max(s, axis=-1, keepdims=True)
            e = jnp.exp((s - m).astype(BF16))
            ol = jnp.dot(e, jnp.concatenate([vc[keys, :], ones_bf], axis=1), preferred_element_type=F32)
            l = ol[:, 128:]
            o2 = ol[:, :128] * (1.0 / l)
            lse2 = m + jnp.log(l)
            o = jnp.where(lo, o2[:BLOCK], o2[BLOCK:])
            lse = jnp.where(lo, lse2[:BLOCK], lse2[BLOCK:])
            if d != CONFIG_ORDER[0]:
                la = lse_ref[tokens, :]
                mx = jnp.maximum(la, lse)
                wa, wb = jnp.exp(la - mx), jnp.exp(lse - mx)
                t = wa + wb
                o = (wa * att_ref[tokens, :] + wb * o) / t
                lse = mx + jnp.log(t)
            att_ref[tokens, :] = o
            lse_ref[tokens, :] = lse

        for d in CONFIG_ORDER:
            _for_blocks(S // BLOCK, 4, functools.partial(fill, d=d))
            _for_blocks(S // BLOCK, 16, functools.partial(block, d=d))

        def gate_step(i, carry):
            rows = pl.ds(pl.multiple_of(i * tn, tn), tn)
            silu, _ = _silu_parts(g_ref[rows, :])
            y_ref[rows, :] = (att_ref[rows, :] * silu).astype(BF16)
            return carry
        lax.fori_loop(0, S // tn, gate_step, 0)

    col = lambda j0: pl.BlockSpec((S, 128), lambda p, j0=j0: (0, j0 + p))
    const = lambda shape: pl.BlockSpec(shape, lambda p: (0,) * len(shape))
    out = pl.BlockSpec((S, 128), lambda p: (0, p))
    return pl.pallas_call(
        body, name="attn_fwd", grid=(npairs,),
        in_specs=[col(COL_AQ), col(COL_AK), col(COL_AV), col(COL_AG), const((1, 128)), const((1, 128)),
                  const((128, 128))],
        out_specs=[out, out, out],
        out_shape=[pltpu.HBM((S, ATTN_WIDTH), BF16), pltpu.HBM((S, ATTN_WIDTH), F32),
                   pltpu.HBM((S, ATTN_WIDTH), F32)],
        scratch_shapes=[pltpu.VMEM((S, 128), F32), pltpu.VMEM((S, 128), F32),
                        pltpu.VMEM((S + BLOCK, 128), BF16), pltpu.VMEM((S + BLOCK, 128), BF16)],
        compiler_params=_params(48, ("arbitrary",)),
    )(*_hbm(proj, proj, proj, proj, qg2, kg2, bd))


def _attn_bwd(proj, dycat, att, lse, qg2, kg2, bd):
    S = proj.shape[0]
    npairs = ATTN_WIDTH // 128
    tn = 512

    def body(q_ref, k_ref, v_ref, g_ref, dy_ref, att_ref, lse_ref, qg_ref, kg_ref, bd_ref,
             dq_ref, dk_ref, dv_ref, dg_ref, dqg_ref, dkg_ref,
             qn, kn, rq_s, rk_s, kc, vc, do_s, dd_s, dqa, dka, dva):
        bdv = bd_ref[...]
        lo = lax.broadcasted_iota(jnp.int32, (BLOCK, 128), 1) < HEAD_DIM
        kc[pl.ds(0, BLOCK), :] = jnp.zeros((BLOCK, 128), BF16)
        vc[pl.ds(0, BLOCK), :] = jnp.zeros((BLOCK, 128), BF16)

        def prepare(i, carry):
            rows = pl.ds(pl.multiple_of(i * tn, tn), tn)
            qv = q_ref[rows, :]
            kv = k_ref[rows, :]
            rq = lax.rsqrt(_split_dot(qv * qv, bdv) * (1.0 / HEAD_DIM) + EPS)
            rk = lax.rsqrt(_split_dot(kv * kv, bdv) * (1.0 / HEAD_DIM) + EPS)
            rq_s[rows, :] = rq
            rk_s[rows, :] = rk
            qn[rows, :] = (qv * rq) * (qg_ref[...] * SCALE)
            kn[rows, :] = (kv * rk) * kg_ref[...]
            silu, dsilu = _silu_parts(g_ref[rows, :])
            dy = dy_ref[rows, :]
            at = att_ref[rows, :]
            do = dy * silu
            do_s[rows, :] = do
            dd_s[rows, :] = _split_dot(do * at, bdv)
            dg_ref[rows, :] = (dy * at * dsilu).astype(BF16)
            dka[rows, :] = jnp.zeros((tn, 128), F32)
            dva[rows, :] = jnp.zeros((tn, 128), F32)
            return carry
        lax.fori_loop(0, S // tn, prepare, 0)

        kt = lax.broadcasted_iota(jnp.int32, (2 * BLOCK, 2 * BLOCK), 0)
        qt = lax.broadcasted_iota(jnp.int32, (2 * BLOCK, 2 * BLOCK), 1) % BLOCK
        band_mask_t = ((kt < BLOCK) & (kt >= qt)) | ((kt >= BLOCK) & ((kt - BLOCK) <= qt))

        def per_query_row(t):
            tt = t.T
            return jnp.concatenate([tt[0:1, :], tt[HEAD_DIM:HEAD_DIM + 1, :]], axis=1)

        def fill(blk, d):
            tokens = _block_tokens(blk, d, S)
            kc[_padded_block(blk), :] = kn[tokens, :].astype(BF16)
            vc[_padded_block(blk), :] = v_ref[tokens, :].astype(BF16)

        def block(blk, d):
            tokens = _block_tokens(blk, d, S)
            keys = pl.ds(pl.multiple_of(blk * BLOCK, BLOCK), 2 * BLOCK)
            first = (blk & (S // d // BLOCK - 1)) == 0
            q2 = _two_heads(qn[tokens, :].astype(BF16), lo)
            do2 = _two_heads(do_s[tokens, :].astype(BF16), lo)
            lse_row = per_query_row(lse_ref[tokens, :])
            dd_row = per_query_row(dd_s[tokens, :])
            kb = kc[keys, :]
            vb = vc[keys, :]
            st = jnp.where(band_mask_t, _nt(kb, q2), NEG)
            st = jnp.concatenate([st[:BLOCK] + jnp.where(first, NEG, 0.0), st[BLOCK:]], axis=0)
            pt = jnp.exp(st - lse_row)
            dst = pt * (_nt(vb, do2) - dd_row)
            ptb = pt.astype(BF16)
            dstb = dst.astype(BF16)
            dv_band = jnp.dot(ptb, do2, preferred_element_type=F32)
            dk_band = jnp.dot(dstb, q2, preferred_element_type=F32)
            before = _block_tokens(jnp.where(first, blk, blk - 1), d, S)
            dka[before, :] = dka[before, :] + dk_band[:BLOCK]
            dva[before, :] = dva[before, :] + dv_band[:BLOCK]
            dka[tokens, :] = dka[tokens, :] + dk_band[BLOCK:]
            dva[tokens, :] = dva[tokens, :] + dv_band[BLOCK:]
            dq2 = _tn(dstb, kb)
            dq = jnp.where(lo, dq2[:BLOCK], dq2[BLOCK:])
            dqa[tokens, :] = dq if d == CONFIG_ORDER[0] else dqa[tokens, :] + dq

        for d in CONFIG_ORDER:
            _for_blocks(S // BLOCK, 4, functools.partial(fill, d=d))
            _for_blocks(S // BLOCK, 8, functools.partial(block, d=d))

        def out_step(i, carry):
            dqg, dkg = carry
            rows = pl.ds(pl.multiple_of(i * tn, tn), tn)
            rq = rq_s[rows, :]
            rk = rk_s[rows, :]
            qh = q_ref[rows, :] * rq
            kh = k_ref[rows, :] * rk
            dqs = dqa[rows, :] * SCALE
            dkn = dka[rows, :]
            aq = dqs * qg_ref[...]
            ak = dkn * kg_ref[...]
            dq_ref[rows, :] = (rq * (aq - qh * (_split_dot(aq * qh, bdv) * (1.0 / HEAD_DIM)))).astype(BF16)
            dk_ref[rows, :] = (rk * (ak - kh * (_split_dot(ak * kh, bdv) * (1.0 / HEAD_DIM)))).astype(BF16)
            dv_ref[rows, :] = dva[rows, :].astype(BF16)
            dqg = dqg + jnp.sum(dqs * qh, axis=0, keepdims=True)
            dkg = dkg + jnp.sum(dkn * kh, axis=0, keepdims=True)
            return dqg, dkg
        zero = jnp.zeros((1, 128), F32)
        dqg, dkg = lax.fori_loop(0, S // tn, out_step, (zero, zero))
        dqg_ref[0] = dqg
        dkg_ref[0] = dkg

    col = lambda j0: pl.BlockSpec((S, 128), lambda p, j0=j0: (0, j0 + p))
    col1 = lambda j0: pl.BlockSpec((S, 128), lambda p, j0=j0: (0, j0 + p), pipeline_mode=pl.Buffered(1))
    const = lambda shape: pl.BlockSpec(shape, lambda p: (0,) * len(shape))
    out = pl.BlockSpec((S, 128), lambda p: (0, p))
    gain_out = pl.BlockSpec((1, 1, 128), lambda p: (p, 0, 0))
    piece = pltpu.HBM((S, ATTN_WIDTH), BF16)
    gains = pltpu.HBM((npairs, 1, 128), F32)
    f32buf = pltpu.VMEM((S, 128), F32)
    bf16pad = pltpu.VMEM((S + BLOCK, 128), BF16)
    return pl.pallas_call(
        body, name="attn_bwd", grid=(npairs,),
        in_specs=[col(COL_AQ), col(COL_AK), col(COL_AV), col1(COL_AG), col1(GMLP_WIDTH // 128), col1(0), col(0),
                  const((1, 128)), const((1, 128)), const((128, 128))],
        out_specs=[out, out, out, out, gain_out, gain_out],
        out_shape=[piece, piece, piece, piece, gains, gains],
        scratch_shapes=[f32buf, f32buf, f32buf, f32buf, bf16pad, bf16pad, f32buf, f32buf, f32buf, f32buf, f32buf],
        compiler_params=_params(60, ("arbitrary",)),
    )(*_hbm(proj, proj, proj, proj, dycat, att, lse, qg2, kg2, bd))


def _mem_kv(mem, gain, wkv_bf, kg4, bd):
    def body(mem_ref, g_ref, w_ref, kg_ref, bd_ref, hm_ref, kraw_ref, mk_ref, mv_ref):
        mv_ = mem_ref[...]
        r = lax.rsqrt(jnp.mean(mv_ * mv_, axis=-1, keepdims=True) + EPS)
        hm = ((mv_ * r) * g_ref[...]).astype(BF16)
        hm_ref[...] = hm
        kv = jnp.dot(hm, w_ref[...], preferred_element_type=F32)
        kraw = kv[:, :MEM_WIDTH]
        kraw_ref[...] = kraw
        ms = _split_dot(kraw * kraw, bd_ref[...]) * (1.0 / HEAD_DIM)
        mk_ref[...] = (kraw * lax.rsqrt(ms + EPS)) * kg_ref[...]
        mv_ref[...] = kv[:, MEM_WIDTH:]

    sq = jax.ShapeDtypeStruct((MEM_LEN, MEM_WIDTH), F32)
    return pl.pallas_call(
        body, name="mem_kv",
        out_shape=[jax.ShapeDtypeStruct((MEM_LEN, D_MODEL), BF16), sq, sq, sq],
        compiler_params=_params(16),
    )(mem, gain, wkv_bf, kg4, bd)


def _mem_fwd(proj, mk, mv, qg4, bd):
    S = proj.shape[0]
    tm = 1024

    def body(q_ref, g_ref, mk_ref, mv_ref, qg_ref, bd_ref, y_ref, om_ref):
        qv = q_ref[...]
        ms = _split_dot(qv * qv, bd_ref[...]) * (1.0 / HEAD_DIM)
        qs = (qv * lax.rsqrt(ms + EPS)) * (qg_ref[...] * SCALE)
        mkb = mk_ref[...].astype(BF16)
        mvb = mv_ref[...].astype(BF16)
        head = _head_index((tm, MEM_WIDTH))
        o = jnp.zeros((tm, MEM_WIDTH), F32)
        for h in range(4):
            s = _nt(jnp.where(head == h, qs, 0.0).astype(BF16), mkb)
            e = jnp.exp(s - jnp.max(s, axis=-1, keepdims=True))
            p = e * (1.0 / jnp.sum(e, axis=-1, keepdims=True))
            o = jnp.where(head == h, jnp.dot(p.astype(BF16), mvb, preferred_element_type=F32), o)
        om_ref[...] = o
        silu, _ = _silu_parts(g_ref[...])
        y_ref[...] = (o * silu).astype(BF16)

    col = lambda j: pl.BlockSpec((tm, MEM_WIDTH), lambda i, j=j: (i, j))
    const = lambda shape: pl.BlockSpec(shape, lambda i: (0,) * len(shape))
    tile = pl.BlockSpec((tm, MEM_WIDTH), lambda i: (i, 0))
    return pl.pallas_call(
        body, name="mem_fwd", grid=(S // tm,),
        in_specs=[col(11), col(12), const((MEM_LEN, MEM_WIDTH)), const((MEM_LEN, MEM_WIDTH)), const((1, MEM_WIDTH)),
                  const((MEM_WIDTH, MEM_WIDTH))],
        out_specs=[tile, tile],
        out_shape=[pltpu.HBM((S, MEM_WIDTH), BF16), pltpu.HBM((S, MEM_WIDTH), F32)],
        compiler_params=_params(24, ("arbitrary",)),
    )(*_hbm(proj, proj, mk, mv, qg4, bd))


def _mem_bwd(proj, dycat, om, mk, mv, qg4, bd):
    S = proj.shape[0]
    tm = 1024

    def body(q_ref, g_ref, dy_ref, om_ref, mk_ref, mv_ref, qg_ref, bd_ref,
             dq_ref, dg_ref, dmk_ref, dmv_ref, dqg_ref):
        i = pl.program_id(0)

        @pl.when(i == 0)
        def _():
            dmk_ref[...] = jnp.zeros_like(dmk_ref)
            dmv_ref[...] = jnp.zeros_like(dmv_ref)
            dqg_ref[...] = jnp.zeros_like(dqg_ref)

        bdv = bd_ref[...]
        qv = q_ref[...]
        rq = lax.rsqrt(_split_dot(qv * qv, bdv) * (1.0 / HEAD_DIM) + EPS)
        qh = qv * rq
        qs = qh * (qg_ref[...] * SCALE)
        silu, dsilu = _silu_parts(g_ref[...])
        dy = dy_ref[...]
        o = om_ref[...]
        do = dy * silu
        dg_ref[...] = (dy * o * dsilu).astype(BF16)
        dd = _split_dot(do * o, bdv)
        mkb = mk_ref[...].astype(BF16)
        mvb = mv_ref[...].astype(BF16)
        head = _head_index((tm, MEM_WIDTH))
        dqs = jnp.zeros((tm, MEM_WIDTH), F32)
        for h in range(4):
            qhd = jnp.where(head == h, qs, 0.0).astype(BF16)
            doh = jnp.where(head == h, do, 0.0).astype(BF16)
            s = _nt(qhd, mkb)
            e = jnp.exp(s - jnp.max(s, axis=-1, keepdims=True))
            p = e * (1.0 / jnp.sum(e, axis=-1, keepdims=True))
            ds = p * (_nt(doh, mvb) - dd[:, h * HEAD_DIM:h * HEAD_DIM + 1])
            dsb = ds.astype(BF16)
            dmv_ref[...] += _tn(p.astype(BF16), doh)
            dmk_ref[...] += _tn(dsb, qhd)
            dqs = jnp.where(head == h, jnp.dot(dsb, mkb, preferred_element_type=F32), dqs)
        dqs = dqs * SCALE
        a = dqs * qg_ref[...]
        dq_ref[...] = (rq * (a - qh * (_split_dot(a * qh, bdv) * (1.0 / HEAD_DIM)))).astype(BF16)
        dqg_ref[...] += jnp.sum(dqs * qh, axis=0, keepdims=True)

    col = lambda j: pl.BlockSpec((tm, MEM_WIDTH), lambda i, j=j: (i, j))
    const = lambda shape: pl.BlockSpec(shape, lambda i: (0,) * len(shape))
    tile = pl.BlockSpec((tm, MEM_WIDTH), lambda i: (i, 0))
    piece = pltpu.HBM((S, MEM_WIDTH), BF16)
    sq = pltpu.HBM((MEM_LEN, MEM_WIDTH), F32)
    return pl.pallas_call(
        body, name="mem_bwd", grid=(S // tm,),
        in_specs=[col(11), col(12), col(3), tile, const((MEM_LEN, MEM_WIDTH)), const((MEM_LEN, MEM_WIDTH)),
                  const((1, MEM_WIDTH)), const((MEM_WIDTH, MEM_WIDTH))],
        out_specs=[tile, tile, const((MEM_LEN, MEM_WIDTH)), const((MEM_LEN, MEM_WIDTH)), const((1, MEM_WIDTH))],
        out_shape=[piece, piece, sq, sq, pltpu.HBM((1, MEM_WIDTH), F32)],
        compiler_params=_params(32, ("arbitrary",)),
    )(*_hbm(proj, proj, dycat, om, mk, mv, qg4, bd))


def _mem_kv_bwd(dmk, dmv, kraw, mem, gain, kg4, wkv_bf, hm_bf, bd):
    def body(dmk_ref, dmv_ref, kraw_ref, mem_ref, g_ref, kg_ref, w_ref, hm_ref, bd_ref, dw_ref, dg_ref, dkg_ref):
        bdv = bd_ref[...]
        kraw = kraw_ref[...]
        rk = lax.rsqrt(_split_dot(kraw * kraw, bdv) * (1.0 / HEAD_DIM) + EPS)
        kh = kraw * rk
        dmkv = dmk_ref[...]
        a = dmkv * kg_ref[...]
        dkraw = rk * (a - kh * (_split_dot(a * kh, bdv) * (1.0 / HEAD_DIM)))
        dkg_ref[...] = jnp.sum(dmkv * kh, axis=0, keepdims=True)
        dkv = jnp.concatenate([dkraw, dmv_ref[...]], axis=1).astype(BF16)
        dw = _tn(hm_ref[...], dkv).astype(BF16)
        rows_blk = D_MODEL // N_DEV
        for j in range(N_DEV):
            dw_ref[j] = dw[rows_blk * j:rows_blk * (j + 1)]
        dhm = _nt(dkv, w_ref[...])
        mv_ = mem_ref[...]
        r = lax.rsqrt(jnp.mean(mv_ * mv_, axis=-1, keepdims=True) + EPS)
        dg_ref[...] = jnp.sum(dhm * (mv_ * r), axis=0, keepdims=True)

    return pl.pallas_call(
        body, name="mem_kv_bwd",
        out_shape=[jax.ShapeDtypeStruct((N_DEV, D_MODEL // N_DEV, 2 * MEM_WIDTH), BF16),
                   jax.ShapeDtypeStruct((1, D_MODEL), F32), jax.ShapeDtypeStruct((1, MEM_WIDTH), F32)],
        compiler_params=_params(24),
    )(dmk, dmv, kraw, mem, gain, kg4, wkv_bf, hm_bf, bd)


def _out_loss(yg, ya, ym, x, tgt, wout_bf):
    S, D = x.shape
    tm = 512
    nsteps = S // tm
    rows_blk = D // N_DEV

    def body(yg_ref, ya_ref, ym_ref, x_ref, t_ref, w_ref, dout_ref, dycat_ref, dw_ref, loss_ref, acc_ref):
        i = pl.program_id(0)

        @pl.when(i == 0)
        def _():
            acc_ref[...] = jnp.zeros_like(acc_ref)
            loss_ref[...] = jnp.zeros_like(loss_ref)

        ycat = jnp.concatenate([yg_ref[...], ya_ref[...], ym_ref[...]], axis=1)
        w = w_ref[...]
        diff = (x_ref[...] + jnp.dot(ycat, w, preferred_element_type=F32)) - t_ref[...]
        loss_ref[...] += jnp.sum(diff * diff, axis=0, keepdims=True)
        dout = diff * (1.0 / D)
        dout_ref[...] = dout
        db = dout.astype(BF16)
        dycat_ref[...] = _nt(db, w)
        acc_ref[...] += _tn(ycat, db)

        @pl.when(i == nsteps - 1)
        def _():
            for j in range(N_DEV):
                dw_ref[j] = acc_ref[rows_blk * j:rows_blk * (j + 1), :].astype(BF16)

    tile = lambda w: pl.BlockSpec((tm, w), lambda i: (i, 0))
    const = lambda shape: pl.BlockSpec(shape, lambda i: (0,) * len(shape))
    return pl.pallas_call(
        body, name="out_loss", grid=(nsteps,),
        in_specs=[tile(GMLP_WIDTH), tile(ATTN_WIDTH), tile(MEM_WIDTH), tile(D), tile(D), const((D, D))],
        out_specs=[tile(D), tile(D), const((N_DEV, rows_blk, D)), const((1, D))],
        out_shape=[pltpu.HBM((S, D), F32), pltpu.HBM((S, D), F32),
                   pltpu.HBM((N_DEV, rows_blk, D), BF16), pltpu.HBM((1, D), F32)],
        scratch_shapes=[pltpu.VMEM((D, D), F32)],
        compiler_params=_params(40, ("arbitrary",)),
    )(*_hbm(yg, ya, ym, x, tgt, wout_bf))


def _piece_specs(pieces, tm):
    return [pl.BlockSpec((tm, p.shape[1]), lambda i: (i, 0)) for p in pieces]


def _in_bwd_dx(pieces, x, dout, gain, w_t, dw_blocks):
    S, D = x.shape
    N = w_t.shape[0]
    tm = 256
    n = len(pieces)
    nsteps = S // tm
    middle_step = nsteps // 8

    def body(*refs):
        piece_refs = refs[:n]
        x_ref, dout_ref, g_ref, w_ref, dwb_ref, gx_ref, dg_ref, gw_ref = refs[n:n + 8]
        rs = _ReduceScatter([dwb_ref], [gw_ref], *refs[n + 8:])
        i = pl.program_id(0)

        @pl.when(i == 0)
        def _():
            dg_ref[...] = jnp.zeros_like(dg_ref)
            rs.start()

        @pl.when(i == middle_step)
        def _():
            rs.middle()

        dproj = jnp.concatenate([r[...] for r in piece_refs], axis=1)
        dh = jnp.dot(dproj, w_ref[...], preferred_element_type=F32)
        xv = x_ref[...]
        r = lax.rsqrt(jnp.mean(xv * xv, axis=-1, keepdims=True) + EPS)
        xh = xv * r
        a = dh * g_ref[...]
        gx_ref[...] = dout_ref[...] + r * (a - xh * jnp.mean(a * xh, axis=-1, keepdims=True))
        dg_ref[...] += jnp.sum(dh * xh, axis=0, keepdims=True)

        @pl.when(i == nsteps - 1)
        def _():
            rs.finish()

    tile = pl.BlockSpec((tm, D), lambda i: (i, 0))
    const = lambda shape: pl.BlockSpec(shape, lambda i: (0,) * len(shape))
    vmem = pl.BlockSpec(memory_space=pltpu.VMEM)
    return pl.pallas_call(
        body, name="in_bwd_dx", grid=(nsteps,),
        in_specs=_piece_specs(pieces, tm)
        + [tile, tile, const((1, D)), pl.BlockSpec((N, D), lambda i: (0, 0), pipeline_mode=pl.Buffered(1)), vmem],
        out_specs=[tile, const((1, D)), vmem],
        out_shape=[pltpu.HBM((S, D), F32), pltpu.HBM((1, D), F32), jax.ShapeDtypeStruct(dw_blocks.shape[1:], F32)],
        scratch_shapes=_reduce_scatter_scratch([dw_blocks]),
        compiler_params=_params(56, ("arbitrary",)),
    )(*_hbm(*pieces, x, dout, gain, w_t), dw_blocks)


def _in_bwd_dw(pieces, h_bf, others):
    S, D = h_bf.shape
    N = sum(p.shape[1] for p in pieces)
    n_blk = N // N_DEV
    tm = 512
    n = len(pieces)
    k = len(others)
    nsteps = S // tm

    def body(*refs):
        piece_refs = refs[:n]
        h_ref = refs[n]
        other_refs = refs[n + 1:n + 1 + k]
        dw_ref = refs[n + 1 + k]
        sum_refs = refs[n + 2 + k:n + 2 + 2 * k]
        acc_ref = refs[n + 2 + 2 * k]
        rs = _ReduceScatter(other_refs, sum_refs, *refs[n + 3 + 2 * k:])
        i = pl.program_id(0)

        @pl.when(i == 0)
        def _():
            acc_ref[...] = jnp.zeros_like(acc_ref)
            rs.start()

        @pl.when(i == 1)
        def _():
            rs.middle()

        dproj = jnp.concatenate([r[...] for r in piece_refs], axis=1)
        acc_ref[...] += _tn(h_ref[...], dproj)

        @pl.when(i == nsteps - 1)
        def _():
            for j in range(N_DEV):
                dw_ref[j] = acc_ref[:, n_blk * j:n_blk * (j + 1)].T.astype(BF16)
            rs.finish()

    vmem = pl.BlockSpec(memory_space=pltpu.VMEM)
    return pl.pallas_call(
        body, name="in_bwd_dw", grid=(nsteps,),
        in_specs=_piece_specs(pieces, tm) + [pl.BlockSpec((tm, D), lambda i: (i, 0))] + [vmem] * k,
        out_specs=[pl.BlockSpec((N_DEV, n_blk, D), lambda i: (0, 0, 0))] + [vmem] * k,
        out_shape=[pltpu.HBM((N_DEV, n_blk, D), BF16)] + [jax.ShapeDtypeStruct(o.shape[1:], F32) for o in others],
        scratch_shapes=[pltpu.VMEM((D, N), F32)] + _reduce_scatter_scratch(others),
        compiler_params=_params(56, ("arbitrary",)),
    )(*_hbm(*pieces, h_bf), *others)


def _row_step(m):
    return max(t for t in range(16, 257, 16) if m % t == 0)


def _place():
    x, y, c = lax.axis_index("x"), lax.axis_index("y"), lax.axis_index("c")
    chips = [(1 - x, y), (x, 1 - y), (1 - x, 1 - y)]
    return x, y, c, chips


class _AllGather:
    def __init__(self, srcs, outs, send_sems, recv_sems, local_sems, first_sem=0):
        self.srcs, self.outs, self.n, self.first_sem = srcs, outs, len(srcs), first_sem
        self.send_sems, self.recv_sems, self.local_sems = send_sems, recv_sems, local_sems

    def _rows(self, a, px, py, pc):
        m = self.srcs[a].shape[0]
        return self.outs[a].at[pl.ds((4 * px + 2 * py + pc) * m, m), :]

    def _copy(self, a, k, block, to, src=None):
        row = self.first_sem + a
        return pltpu.make_async_remote_copy(
            src_ref=self._rows(a, *block) if src is None else src, dst_ref=self._rows(a, *block),
            send_sem=self.send_sems.at[row, k], recv_sem=self.recv_sems.at[row, k], device_id=to, device_id_type=MESH)

    def _mine(self):
        x, y, c, _ = _place()
        return [pltpu.make_async_copy(self.srcs[a], self._rows(a, x, y, c), self.local_sems.at[self.first_sem + a])
                for a in range(self.n)]

    def _first(self, far):
        x, y, c, chips = _place()
        out = []
        for a in range(self.n):
            if far:
                out.append(self._copy(a, 3, (x, y, c), (*chips[2], c), src=self.srcs[a]))
            else:
                out.append(self._copy(a, 0, (x, y, c), (x, y, 1 - c), src=self.srcs[a]))
                out += [self._copy(a, 1 + j, (x, y, c), (*chips[j], c), src=self.srcs[a]) for j in (1, 0)]
        return out

    def _passed(self, j):
        x, y, c, chips = _place()
        return [self._copy(a, 4 + j, (*chips[j], c), (x, y, 1 - c)) for a in range(self.n)]

    def start(self):
        for cp in self._mine() + self._first(far=False):
            cp.start()

    def start_far(self):
        for cp in self._first(far=True):
            cp.start()

    def from_chip(self, j):
        x, y, c, chips = _place()
        for a in range(self.n):
            self._copy(a, 1 + j, (*chips[j], c), (x, y, c)).wait_recv()
        for cp in self._passed(j):
            cp.start()

    def from_sibling(self, j=None):
        x, y, c, chips = _place()
        for a in range(self.n):
            block = (x, y, 1 - c) if j is None else (*chips[j], 1 - c)
            self._copy(a, 0 if j is None else 4 + j, block, (x, y, c)).wait_recv()

    def from_self(self):
        for cp in self._mine():
            cp.wait()

    def finish(self):
        for cp in (self._first(far=False) + self._first(far=True)
                   + self._passed(0) + self._passed(1) + self._passed(2)):
            cp.wait_send()

    def run(self):
        self.start()
        self.start_far()
        self.from_self()
        for j in range(3):
            self.from_chip(j)
        self.from_sibling()
        for j in range(3):
            self.from_sibling(j)
        self.finish()


def _gather_proj(x, gain, shards, xpos):
    S, D = x.shape
    n = len(shards)
    N = N_DEV * shards[0].shape[0]
    half = N // 2
    tm = 1024
    nsteps = S // tm

    def body(*refs):
        xpos_ref, x_ref, g_ref = refs[:3]
        ins = refs[3:3 + n]
        proj_ref, h_ref = refs[3 + n:5 + n]
        outs = refs[5 + n:5 + 2 * n]
        casts = refs[5 + 2 * n:5 + 3 * n]
        whole = refs[5 + 3 * n:5 + 4 * n]
        sems = refs[5 + 4 * n:8 + 4 * n]
        ag = _AllGather(casts[:1], whole[:1], *sems)
        later = _AllGather(casts[1:], whole[1:], *sems, first_sem=1)
        out_sems, h_all = refs[8 + 4 * n:]
        p, i = pl.program_id(0), pl.program_id(1)
        rows = pl.ds(pl.multiple_of(i * tm, tm), tm)

        @pl.when((p == 0) & (i == 0))
        def _():
            for a in range(n):
                tr = _row_step(ins[a].shape[0])

                def cast(r, carry, a=a, tr=tr):
                    at = pl.ds(pl.multiple_of(r * tr, tr), tr)
                    casts[a][at, :] = ins[a][at, :].astype(BF16)
                    return carry
                lax.fori_loop(0, ins[a].shape[0] // tr, cast, 0)
            ag.start()

        @pl.when(p == 0)
        def _():
            xv = x_ref[...]
            r = lax.rsqrt(jnp.mean(xv * xv, axis=-1, keepdims=True) + EPS)
            h = ((xv * r) * g_ref[...]).astype(BF16)
            h_ref[...] = h
            h_all[rows, :] = h

        @pl.when((p == 1) & (i == 0))
        def _():
            ag.from_self()
            ag.from_chip(1)
            ag.start_far()
            later.start()
            later.start_far()
            ag.from_sibling()
            ag.from_sibling(1)

        @pl.when((p == 2) & (i == 0))
        def _():
            for j in (0, 2):
                ag.from_chip(j)
            for j in (0, 2):
                ag.from_sibling(j)

        @pl.when(p > 0)
        def _():
            which = (xpos_ref[0] + p - 1) % 2
            w_half = whole[0][pl.ds(pl.multiple_of(which * half, half), half), :]
            proj_ref[...] = _nt(h_all[rows, :], w_half)

        @pl.when((p == 2) & (i == nsteps - 1))
        def _():
            ag.finish()
            later.from_self()
            for j in range(3):
                later.from_chip(j)
            later.from_sibling()
            for j in range(3):
                later.from_sibling(j)
            later.finish()
            to_results = [pltpu.make_async_copy(whole[a], outs[a], out_sems.at[a]) for a in range(n)]
            for cp in to_results:
                cp.start()
            for cp in to_results:
                cp.wait()

    vmem = pl.BlockSpec(memory_space=pltpu.VMEM)
    hbm = pl.BlockSpec(memory_space=pl.ANY)
    gathered = [(N_DEV * a.shape[0], a.shape[1]) for a in shards]
    x_tile = lambda p, i, xp: (jnp.where(p == 0, i, nsteps - 1), 0)
    proj_tile = lambda p, i, xp: (jnp.where(p == 0, 0, i), (xp[0] + jnp.maximum(p - 1, 0)) % 2)
    grid_spec = pltpu.PrefetchScalarGridSpec(
        num_scalar_prefetch=1, grid=(3, nsteps),
        in_specs=[pl.BlockSpec((tm, D), x_tile), pl.BlockSpec((1, D), lambda p, i, xp: (0, 0))] + [vmem] * n,
        out_specs=[pl.BlockSpec((tm, half), proj_tile), pl.BlockSpec((tm, D), x_tile)] + [hbm] * n,
        scratch_shapes=[pltpu.VMEM(a.shape, BF16) for a in shards] + [pltpu.VMEM(g, BF16) for g in gathered]
        + [pltpu.SemaphoreType.DMA((n, 7)), pltpu.SemaphoreType.DMA((n, 7)), pltpu.SemaphoreType.DMA((n,)),
           pltpu.SemaphoreType.DMA((n,)), pltpu.VMEM((S, D), BF16)])
    return pl.pallas_call(
        body, name="gather_proj", grid_spec=grid_spec,
        out_shape=[pltpu.HBM((S, N), F32), pltpu.HBM((S, D), BF16)] + [pltpu.HBM(g, BF16) for g in gathered],
        compiler_params=_params(56, ("arbitrary", "arbitrary")),
    )(xpos, *_hbm(x, gain), *shards)


ROW_NORM, ROW_MEM_NORM, ROW_V_GAIN, ROW_B, ROW_ATTN_GAINS, ROW_MEM_GAINS, ROW_W_S, ROW_LOSS = 0, 8, 16, 18, 22, 23, 24, 536
SMALL_ROWS = 544


def _gather_small(dgain, dmgain, dvg, db2, dqg, dkg, dmqg, dmkg, dws, sq):
    def body(dgain_ref, dmgain_ref, dvg_ref, db2_ref, dqg_ref, dkg_ref, dmqg_ref, dmkg_ref, dws_ref, sq_ref,
             out_ref, mine, send_sems, recv_sems, local_sems):
        first = lax.broadcasted_iota(jnp.int32, (1, 128), 1) < HEAD_DIM
        for i in range(8):
            cols = slice(128 * i, 128 * (i + 1))
            mine[ROW_NORM + i:ROW_NORM + i + 1, :] = dgain_ref[:, cols]
            mine[ROW_MEM_NORM + i:ROW_MEM_NORM + i + 1, :] = dmgain_ref[:, cols]
            mine[ROW_LOSS + i:ROW_LOSS + i + 1, :] = sq_ref[:, cols]
        mine[ROW_V_GAIN:ROW_V_GAIN + 1, :] = dvg_ref[:, 0:128]
        mine[ROW_V_GAIN + 1:ROW_V_GAIN + 2, :] = dvg_ref[:, 128:256]
        bt = db2_ref[...].T
        for h in range(4):
            mine[ROW_B + h:ROW_B + h + 1, :] = bt[HEAD_DIM * h:HEAD_DIM * h + 1, :]

        def fold_heads(t):
            return t + pltpu.roll(t, HEAD_DIM, axis=1)
        aq = fold_heads(dqg_ref[0] + dqg_ref[1] + dqg_ref[2] + dqg_ref[3])
        ak = fold_heads(dkg_ref[0] + dkg_ref[1] + dkg_ref[2] + dkg_ref[3])
        mine[ROW_ATTN_GAINS:ROW_ATTN_GAINS + 1, :] = jnp.where(first, aq, ak)
        mq = fold_heads(dmqg_ref[:, 0:128] + dmqg_ref[:, 128:256])
        mk = fold_heads(dmkg_ref[:, 0:128] + dmkg_ref[:, 128:256])
        mine[ROW_MEM_GAINS:ROW_MEM_GAINS + 1, :] = jnp.where(first, mq, mk)
        mine[ROW_W_S:ROW_W_S + 4 * CHUNK, :] = dws_ref[...]
        _AllGather([mine], [out_ref], send_sems, recv_sems, local_sems).run()

    return pl.pallas_call(
        body, name="gather_small_grads",
        out_shape=jax.ShapeDtypeStruct((N_DEV * SMALL_ROWS, 128), F32),
        scratch_shapes=[pltpu.VMEM((SMALL_ROWS, 128), F32), pltpu.SemaphoreType.DMA((1, 7)),
                        pltpu.SemaphoreType.DMA((1, 7)), pltpu.SemaphoreType.DMA((1,))],
        compiler_params=_params(16),
    )(dgain, dmgain, dvg, db2, dqg, dkg, dmqg, dmkg, dws, sq)


def _reduce_scatter_scratch(arrs):
    n = len(arrs)
    return ([pltpu.VMEM((4,) + a.shape[1:], BF16) for a in arrs] + [pltpu.VMEM((3,) + a.shape[1:], BF16) for a in arrs]
            + [pltpu.SemaphoreType.DMA((n, 7)), pltpu.SemaphoreType.DMA((n, 7))])


class _ReduceScatter:
    def __init__(self, ins, outs, *scratch):
        n = len(ins)
        self.n, self.ins, self.outs = n, ins, outs
        self.half, self.quarter = scratch[:n], scratch[n:2 * n]
        self.send_sems, self.recv_sems = scratch[2 * n:]

    def _to_sibling(self):
        x, y, c, _ = _place()
        return [pltpu.make_async_remote_copy(
            src_ref=self.ins[a].at[2 * q + (1 - c)], dst_ref=self.half[a].at[q], send_sem=self.send_sems.at[a, q],
            recv_sem=self.recv_sems.at[a, q], device_id=(x, y, 1 - c), device_id_type=MESH)
            for a in range(self.n) for q in range(4)]

    def _to_chips(self):
        _, _, c, chips = _place()
        return [pltpu.make_async_remote_copy(
            src_ref=self.half[a].at[2 * chip[0] + chip[1]], dst_ref=self.quarter[a].at[k],
            send_sem=self.send_sems.at[a, 4 + k], recv_sem=self.recv_sems.at[a, 4 + k], device_id=(*chip, c),
            device_id_type=MESH) for a in range(self.n) for k, chip in enumerate(chips)]

    def _rows(self, a, fn):
        m = self.ins[a].shape[1]
        tr = _row_step(m)

        def step(i, carry):
            fn(pl.ds(pl.multiple_of(i * tr, tr), tr))
            return carry
        lax.fori_loop(0, m // tr, step, 0)

    def start(self):
        for cp in self._to_sibling():
            cp.start()

    def middle(self):
        _, _, c, _ = _place()
        for cp in self._to_sibling():
            cp.wait_recv()
        for a in range(self.n):
            for q in range(4):
                def add_half(rows, a=a, q=q):
                    both = self.ins[a][2 * q + c, rows, :].astype(F32) + self.half[a][q, rows, :].astype(F32)
                    self.half[a][q, rows, :] = both.astype(BF16)
                self._rows(a, add_half)
        for cp in self._to_chips():
            cp.start()

    def finish(self):
        x, y, _, _ = _place()
        for cp in self._to_chips():
            cp.wait_recv()
        for a in range(self.n):
            def add_quarters(rows, a=a):
                f = lambda t: t.astype(F32)
                self.outs[a][rows, :] = ((f(self.half[a][2 * x + y, rows, :]) + f(self.quarter[a][0, rows, :]))
                                         + (f(self.quarter[a][1, rows, :]) + f(self.quarter[a][2, rows, :])))
            self._rows(a, add_quarters)
        for cp in self._to_sibling() + self._to_chips():
            cp.wait_send()


def _adamw_math(w, g, m, v):
    m = ADAM_B1 * m + (1.0 - ADAM_B1) * g
    v = ADAM_B2 * v + (1.0 - ADAM_B2) * (g * g)
    m_hat = m / (1.0 - ADAM_B1 ** ADAM_STEP)
    v_hat = v / (1.0 - ADAM_B2 ** ADAM_STEP)
    delta = -ADAM_LR * (m_hat / (jnp.sqrt(v_hat) + ADAM_EPS) + ADAM_WD * w)
    return delta, m, v


def _adamw(w, g, m, v, name):
    R, C = w.shape
    tr = _row_step(R)

    def body(w_ref, g_ref, m_ref, v_ref, d_ref, nm_ref, nv_ref):
        d_ref[...], nm_ref[...], nv_ref[...] = _adamw_math(w_ref[...], g_ref[...], m_ref[...], v_ref[...])

    tile = pl.BlockSpec((tr, C), lambda i: (i, 0))
    out = pltpu.HBM((R, C), F32)
    return pl.pallas_call(
        body, name=name, grid=(R // tr,), in_specs=[tile] * 4, out_specs=[tile] * 3, out_shape=[out] * 3,
        compiler_params=_params(16, ("arbitrary",)),
    )(*_hbm(w, g, m, v))


SMALL = ("norm_gain", "gmlp_v_gain", "gmlp_w_s", "gmlp_b", "attn_q_gain", "attn_k_gain", "mem_norm_gain",
         "mem_q_gain", "mem_k_gain")
WEIGHTS = ("norm_gain", "w_in", "gmlp_v_gain", "gmlp_w_s", "gmlp_b", "attn_q_gain", "attn_k_gain",
           "mem_norm_gain", "w_mem_kv", "mem_q_gain", "mem_k_gain", "w_out")


def _adamw_small(w, m, v, g_all):
    k = len(SMALL)
    half = slice(0, HEAD_DIM), slice(HEAD_DIM, 2 * HEAD_DIM)

    def body(*refs):
        w_refs, m_refs, v_refs = refs[:k], refs[k:2 * k], refs[2 * k:3 * k]
        g_ref = refs[3 * k]
        outs = refs[3 * k + 1:7 * k + 1]
        loss_ref, gsum = refs[7 * k + 1:]

        part = SMALL_ROWS // 4
        for p in range(4):
            acc = g_ref[part * p:part * (p + 1), :]
            for dev in range(1, N_DEV):
                acc = acc + g_ref[dev * SMALL_ROWS + part * p:dev * SMALL_ROWS + part * (p + 1), :]
            gsum[part * p:part * (p + 1), :] = acc

        def update(name, at, g):
            i = SMALL.index(name)
            d, nm, nv = _adamw_math(w_refs[i][at], g, m_refs[i][at], v_refs[i][at])
            outs[i][at], outs[k + i][at], outs[2 * k + i][at], outs[3 * k + i][at] = g, d, nm, nv

        for i in range(8):
            at = (slice(0, 1), slice(128 * i, 128 * (i + 1)))
            update("norm_gain", at, gsum[ROW_NORM + i:ROW_NORM + i + 1, :])
            update("mem_norm_gain", at, gsum[ROW_MEM_NORM + i:ROW_MEM_NORM + i + 1, :])
        for h in range(4):
            row = (0, slice(h, h + 1), slice(None))
            update("gmlp_v_gain", row, gsum[ROW_V_GAIN + h // 2:ROW_V_GAIN + h // 2 + 1, half[h % 2]])
            update("gmlp_b", row, gsum[ROW_B + h:ROW_B + h + 1, :])
            update("gmlp_w_s", (0, h), gsum[ROW_W_S + CHUNK * h:ROW_W_S + CHUNK * (h + 1), :])
        whole = (slice(0, 1), slice(None))
        update("attn_q_gain", whole, gsum[ROW_ATTN_GAINS:ROW_ATTN_GAINS + 1, half[0]])
        update("attn_k_gain", whole, gsum[ROW_ATTN_GAINS:ROW_ATTN_GAINS + 1, half[1]])
        update("mem_q_gain", whole, gsum[ROW_MEM_GAINS:ROW_MEM_GAINS + 1, half[0]])
        update("mem_k_gain", whole, gsum[ROW_MEM_GAINS:ROW_MEM_GAINS + 1, half[1]])
        loss_ref[...] = jnp.sum(gsum[ROW_LOSS:ROW_LOSS + 8, :], keepdims=True) * (0.5 / D_MODEL)

    shapes = [jax.ShapeDtypeStruct(w[name].shape, F32) for name in SMALL]
    res = pl.pallas_call(
        body, name="adamw_small",
        out_shape=shapes * 4 + [jax.ShapeDtypeStruct((1, 1), F32)],
        scratch_shapes=[pltpu.VMEM((SMALL_ROWS, 128), F32)],
        compiler_params=_params(16),
    )(*[w[n] for n in SMALL], *[m[n] for n in SMALL], *[v[n] for n in SMALL], g_all)
    trees = [dict(zip(SMALL, res[j * k:(j + 1) * k])) for j in range(4)]
    return (*trees, res[4 * k])


def _grads(x, mem, tgt, w, shards):
    bd128, bd256 = _head_blockdiag(128), _head_blockdiag(256)
    gain = w["norm_gain"].reshape(1, D_MODEL)
    vg = w["gmlp_v_gain"].reshape(1, GMLP_WIDTH)
    w_s = w["gmlp_w_s"].reshape(4, CHUNK, CHUNK)
    b2 = jnp.repeat(w["gmlp_b"].reshape(4, CHUNK).T, HEAD_DIM, axis=1)
    qg2 = jnp.tile(w["attn_q_gain"].reshape(1, HEAD_DIM), (1, 2))
    kg2 = jnp.tile(w["attn_k_gain"].reshape(1, HEAD_DIM), (1, 2))
    mqg4 = jnp.tile(w["mem_q_gain"].reshape(1, HEAD_DIM), (1, 4))
    mkg4 = jnp.tile(w["mem_k_gain"].reshape(1, HEAD_DIM), (1, 4))
    mgain = w["mem_norm_gain"].reshape(1, D_MODEL)

    xpos = lax.axis_index("x").astype(jnp.int32).reshape(1)
    proj, h_bf, win_t, wkv_bf, wout_bf = _gather_proj(x, gain, shards, xpos)
    yg = _gmlp_fwd(proj, vg, w_s, b2, bd256)
    ya, att, lse = _attn_fwd(proj, qg2, kg2, bd128)
    hm_bf, kraw, mk, mv = _mem_kv(mem, mgain, wkv_bf, mkg4, bd256)
    ym, om = _mem_fwd(proj, mk, mv, mqg4, bd256)
    dout, dycat, dwout, sq = _out_loss(yg, ya, ym, x, tgt, wout_bf)

    du, dgv, dgg, dws, db2, dvg = _gmlp_bwd(proj, dycat, vg, w_s, b2, bd256)
    dq, dk, dv, dag, dqg, dkg = _attn_bwd(proj, dycat, att, lse, qg2, kg2, bd128)
    dmq, dmg, dmk, dmv, dmqg = _mem_bwd(proj, dycat, om, mk, mv, mqg4, bd256)
    dwkv, dmgain, dmkg = _mem_kv_bwd(dmk, dmv, kraw, mem, mgain, mkg4, wkv_bf, hm_bf, bd256)
    pieces = [du, dgv, dgg, dq, dk, dv, dag, dmq, dmg]
    dwin, g_wkv, g_wout = _in_bwd_dw(pieces, h_bf, [dwkv, dwout])
    grad_x, dgain, g_win = _in_bwd_dx(pieces, x, dout, gain, win_t, dwin)
    return grad_x, g_win, g_wkv, g_wout, (dgain, dmgain, dvg, db2, dqg, dkg, dmqg, dmkg, dws, sq)


def kernel(x, mem, norm_gain, w_in, gmlp_v_gain, gmlp_w_s, gmlp_b, attn_q_gain, attn_k_gain, mem_norm_gain, w_mem_kv, mem_q_gain, mem_k_gain, w_out, loss_target, m_norm_gain, m_w_in, m_gmlp_v_gain, m_gmlp_w_s, m_gmlp_b, m_attn_q_gain, m_attn_k_gain, m_mem_norm_gain, m_w_mem_kv, m_mem_q_gain, m_mem_k_gain, m_w_out, v_norm_gain, v_w_in, v_gmlp_v_gain, v_gmlp_w_s, v_gmlp_b, v_attn_q_gain, v_attn_k_gain, v_mem_norm_gain, v_w_mem_kv, v_mem_q_gain, v_mem_k_gain, v_w_out):
    w = dict(norm_gain=norm_gain, w_in=w_in, gmlp_v_gain=gmlp_v_gain, gmlp_w_s=gmlp_w_s, gmlp_b=gmlp_b,
             attn_q_gain=attn_q_gain, attn_k_gain=attn_k_gain, mem_norm_gain=mem_norm_gain, w_mem_kv=w_mem_kv,
             mem_q_gain=mem_q_gain, mem_k_gain=mem_k_gain, w_out=w_out)
    m = dict(norm_gain=m_norm_gain, w_in=m_w_in, gmlp_v_gain=m_gmlp_v_gain, gmlp_w_s=m_gmlp_w_s, gmlp_b=m_gmlp_b,
             attn_q_gain=m_attn_q_gain, attn_k_gain=m_attn_k_gain, mem_norm_gain=m_mem_norm_gain,
             w_mem_kv=m_w_mem_kv, mem_q_gain=m_mem_q_gain, mem_k_gain=m_mem_k_gain, w_out=m_w_out)
    v = dict(norm_gain=v_norm_gain, w_in=v_w_in, gmlp_v_gain=v_gmlp_v_gain, gmlp_w_s=v_gmlp_w_s, gmlp_b=v_gmlp_b,
             attn_q_gain=v_attn_q_gain, attn_k_gain=v_attn_k_gain, mem_norm_gain=v_mem_norm_gain,
             w_mem_kv=v_w_mem_kv, mem_q_gain=v_mem_q_gain, mem_k_gain=v_mem_k_gain, w_out=v_w_out)
    transposed = lambda t: jnp.transpose(t[0])

    grad_x, g_win, g_wkv, g_wout, small = _grads(
        x[0], mem[0], loss_target[0], w, [transposed(w_in), w_mem_kv[0], w_out[0]])
    small_all = _gather_small(*small)

    out_g, out_d, out_m, out_v, loss = _adamw_small(w, m, v, small_all)
    d_, m_, v_ = _adamw(transposed(w_in), g_win, transposed(m_w_in), transposed(v_w_in), "adamw_w_in")
    for tree, t in ((out_g, g_win), (out_d, d_), (out_m, m_), (out_v, v_)):
        tree["w_in"] = jnp.transpose(t)[None]
    for name, g in (("w_mem_kv", g_wkv), ("w_out", g_wout)):
        d_, m_, v_ = _adamw(w[name][0], g, m[name][0], v[name][0], "adamw_" + name)
        out_g[name], out_d[name], out_m[name], out_v[name] = g[None], d_[None], m_[None], v_[None]

    return (loss.reshape(()), grad_x[None], *[out_g[k] for k in WEIGHTS], *[out_d[k] for k in WEIGHTS],
            *[out_m[k] for k in WEIGHTS], *[out_v[k] for k in WEIGHTS])
```

```python
import functools
import math

import jax
import jax.numpy as jnp
from jax import lax
from jax.experimental import pallas as pl
from jax.experimental.pallas import tpu as pltpu

F32 = jnp.float32
BF16 = jnp.bfloat16

N_DEV = 8
D_MODEL = 1024
HEAD_DIM = 64
GMLP_WIDTH = 256
ATTN_WIDTH = 512
MEM_WIDTH = 256
MEM_LEN = 256
CHUNK = 128
BLOCK = 128
DILATIONS = (1, 4, 16)
CONFIG_ORDER = tuple(reversed(DILATIONS))
EPS = 1e-6
SCALE = 1.0 / math.sqrt(HEAD_DIM)
NEG = -1e30

ADAM_LR = 0.001
ADAM_B1 = 0.9
ADAM_B2 = 0.999
ADAM_EPS = 1e-08
ADAM_WD = 0.01
ADAM_STEP = 10

MIB = 1024 * 1024
MESH = pl.DeviceIdType.MESH

COL_AQ, COL_AK, COL_AV, COL_AG = 6, 10, 14, 18


def _params(vmem_mib, semantics=None):
    kw = dict(vmem_limit_bytes=vmem_mib * MIB)
    if semantics is not None:
        kw["dimension_semantics"] = semantics
    return pltpu.CompilerParams(**kw)


def _hbm(*arrs):
    return [pltpu.with_memory_space_constraint(a, pltpu.HBM) for a in arrs]


def _split_dot(x, sel_bf):
    hi = x.astype(BF16)
    lo = (x - hi.astype(F32)).astype(BF16)
    return jnp.dot(hi, sel_bf, preferred_element_type=F32) + jnp.dot(lo, sel_bf, preferred_element_type=F32)


def _nt(a, b):
    return lax.dot_general(a, b, (((1,), (1,)), ((), ())), preferred_element_type=F32)


def _tn(a, b):
    return lax.dot_general(a, b, (((0,), (0,)), ((), ())), preferred_element_type=F32)


def _silu_parts(g):
    sg = jax.nn.sigmoid(g)
    return g * sg, sg * (1.0 + g * (1.0 - sg))


def _head_index(shape):
    return lax.shift_right_logical(lax.broadcasted_iota(jnp.int32, shape, 1), HEAD_DIM.bit_length() - 1)


def _head_blockdiag(width):
    i = jnp.arange(width) // HEAD_DIM
    return (i[:, None] == i[None, :]).astype(BF16)


def _gmlp_masked_weights(ws_ref, transpose):
    t = lax.broadcasted_iota(jnp.int32, (CHUNK, CHUNK), 0)
    s = lax.broadcasted_iota(jnp.int32, (CHUNK, CHUNK), 1)
    parts = []
    for h in range(4):
        wm = jnp.where(s <= t, ws_ref[h], 0.0)
        parts.append(wm.T if transpose else wm)
    return jnp.concatenate(parts, axis=1).astype(BF16)


def _head_stack(v, head):
    return jnp.concatenate([jnp.where(head == h, v, 0.0) for h in range(4)], axis=0).astype(BF16)


def _gmlp_fwd(proj, vg, w_s, b2, bd):
    S = proj.shape[0]
    tm = 1024

    def body(u_ref, v_ref, g_ref, vg_ref, ws_ref, b2_ref, bd_ref, y_ref):
        v = v_ref[...]
        ms = _split_dot(v * v, bd_ref[...]) * (1.0 / HEAD_DIM)
        vn = (v * lax.rsqrt(ms + EPS)) * vg_ref[...]
        wcat = _gmlp_masked_weights(ws_ref, False)
        head = _head_index((CHUNK, GMLP_WIDTH))
        for c in range(tm // CHUNK):
            rows = slice(c * CHUNK, (c + 1) * CHUNK)
            sp = jnp.dot(wcat, _head_stack(vn[rows], head), preferred_element_type=F32) + b2_ref[...]
            silu, _ = _silu_parts(g_ref[rows, :])
            y_ref[rows, :] = ((u_ref[rows, :] * sp) * silu).astype(BF16)

    col = lambda j: pl.BlockSpec((tm, GMLP_WIDTH), lambda i, j=j: (i, j))
    const = lambda shape: pl.BlockSpec(shape, lambda i: (0,) * len(shape))
    return pl.pallas_call(
        body, name="gmlp_fwd", grid=(S // tm,),
        in_specs=[col(0), col(1), col(2), const((1, GMLP_WIDTH)), const((4, CHUNK, CHUNK)),
                  const((CHUNK, GMLP_WIDTH)), const((GMLP_WIDTH, GMLP_WIDTH))],
        out_specs=pl.BlockSpec((tm, GMLP_WIDTH), lambda i: (i, 0)),
        out_shape=pltpu.HBM((S, GMLP_WIDTH), BF16),
        compiler_params=_params(24, ("arbitrary",)),
    )(*_hbm(proj, proj, proj, vg, w_s, b2, bd))


def _gmlp_bwd(proj, dycat, vg, w_s, b2, bd):
    S = proj.shape[0]
    tm = 1024
    nsteps = S // tm

    def body(u_ref, v_ref, g_ref, dy_ref, vg_ref, ws_ref, b2_ref, bd_ref,
             du_ref, dv_ref, dg_ref, dws_ref, db2_ref, dvg_ref):
        i = pl.program_id(0)

        @pl.when(i == 0)
        def _():
            dws_ref[...] = jnp.zeros_like(dws_ref)
            db2_ref[...] = jnp.zeros_like(db2_ref)
            dvg_ref[...] = jnp.zeros_like(dvg_ref)

        bdv = bd_ref[...]
        v = v_ref[...]
        ms = _split_dot(v * v, bdv) * (1.0 / HEAD_DIM)
        rv = lax.rsqrt(ms + EPS)
        xhat = v * rv
        vgv = vg_ref[...]
        vn = xhat * vgv
        wcat = _gmlp_masked_weights(ws_ref, False)
        wcat_t = _gmlp_masked_weights(ws_ref, True)
        head = _head_index((CHUNK, GMLP_WIDTH))
        dvg = jnp.zeros((1, GMLP_WIDTH), F32)
        for c in range(tm // CHUNK):
            rows = slice(c * CHUNK, (c + 1) * CHUNK)
            vn_c = vn[rows]
            spb = jnp.dot(wcat, _head_stack(vn_c, head), preferred_element_type=F32) + b2_ref[...]
            silu, dsilu = _silu_parts(g_ref[rows, :])
            dy = dy_ref[rows, :]
            u = u_ref[rows, :]
            du_ref[rows, :] = (dy * spb * silu).astype(BF16)
            dg_ref[rows, :] = (dy * u * spb * dsilu).astype(BF16)
            dsp = dy * u * silu
            db2_ref[...] += dsp
            dstack = _head_stack(dsp, head)
            dvn = jnp.dot(wcat_t, dstack, preferred_element_type=F32)
            dws_ref[...] += _nt(dstack, vn_c.astype(BF16))
            xh = xhat[rows]
            a = dvn * vgv
            mean_ax = _split_dot(a * xh, bdv) * (1.0 / HEAD_DIM)
            dv_ref[rows, :] = (rv[rows] * (a - xh * mean_ax)).astype(BF16)
            dvg = dvg + jnp.sum(dvn * xh, axis=0, keepdims=True)
        dvg_ref[...] += dvg

        @pl.when(i == nsteps - 1)
        def _():
            t = lax.broadcasted_iota(jnp.int32, (4 * CHUNK, CHUNK), 0) % CHUNK
            s = lax.broadcasted_iota(jnp.int32, (4 * CHUNK, CHUNK), 1)
            dws_ref[...] = jnp.where(s <= t, dws_ref[...], 0.0)
            db2_ref[...] = _split_dot(db2_ref[...], bdv)

    col = lambda j: pl.BlockSpec((tm, GMLP_WIDTH), lambda i, j=j: (i, j))
    const = lambda shape: pl.BlockSpec(shape, lambda i: (0,) * len(shape))
    tile = pl.BlockSpec((tm, GMLP_WIDTH), lambda i: (i, 0))
    piece = pltpu.HBM((S, GMLP_WIDTH), BF16)
    return pl.pallas_call(
        body, name="gmlp_bwd", grid=(nsteps,),
        in_specs=[col(0), col(1), col(2), col(0), const((1, GMLP_WIDTH)), const((4, CHUNK, CHUNK)),
                  const((CHUNK, GMLP_WIDTH)), const((GMLP_WIDTH, GMLP_WIDTH))],
        out_specs=[tile, tile, tile, const((4 * CHUNK, CHUNK)), const((CHUNK, GMLP_WIDTH)), const((1, GMLP_WIDTH))],
        out_shape=[piece, piece, piece, pltpu.HBM((4 * CHUNK, CHUNK), F32),
                   pltpu.HBM((CHUNK, GMLP_WIDTH), F32), pltpu.HBM((1, GMLP_WIDTH), F32)],
        compiler_params=_params(32, ("arbitrary",)),
    )(*_hbm(proj, proj, proj, dycat, vg, w_s, b2, bd))


def _band_mask():
    qi = lax.broadcasted_iota(jnp.int32, (2 * BLOCK, 2 * BLOCK), 0) % BLOCK
    ki = lax.broadcasted_iota(jnp.int32, (2 * BLOCK, 2 * BLOCK), 1)
    return ((ki < BLOCK) & (ki >= qi)) | ((ki >= BLOCK) & ((ki - BLOCK) <= qi))


def _first_block_bias(blk, blocks_per_class):
    kcol = lax.broadcasted_iota(jnp.int32, (1, 2 * BLOCK), 1)
    kill = jnp.where((blk & (blocks_per_class - 1)) == 0, NEG, 0.0)
    return jnp.where(kcol < BLOCK, kill, 0.0)


def _two_heads(q, lo):
    zero = jnp.zeros_like(q)
    return jnp.concatenate([jnp.where(lo, q, zero), jnp.where(lo, zero, q)], axis=0)


def _block_tokens(blk, d, S):
    if d == 1:
        return pl.ds(pl.multiple_of(blk * BLOCK, BLOCK), BLOCK)
    blocks_per_class = S // d // BLOCK
    r = lax.shift_right_logical(blk, blocks_per_class.bit_length() - 1)
    n = blk & (blocks_per_class - 1)
    return pl.ds(r + n * (BLOCK * d), BLOCK, stride=d)


def _padded_block(blk):
    return pl.ds(pl.multiple_of((blk + 1) * BLOCK, BLOCK), BLOCK)


def _for_blocks(n_blocks, unroll, fn):
    def group(g, carry):
        for u in range(unroll):
            fn(g * unroll + u)
        return carry
    lax.fori_loop(0, n_blocks // unroll, group, 0)


def _attn_fwd(proj, qg2, kg2, bd):
    S = proj.shape[0]
    npairs = ATTN_WIDTH // 128
    tn = 512

    def body(q_ref, k_ref, v_ref, g_ref, qg_ref, kg_ref, bd_ref, y_ref, att_ref, lse_ref, qn, kn, kc, vc):
        bdv = bd_ref[...]
        lo = lax.broadcasted_iota(jnp.int32, (BLOCK, 128), 1) < HEAD_DIM
        band_mask = _band_mask()
        kc[pl.ds(0, BLOCK), :] = jnp.zeros((BLOCK, 128), BF16)
        vc[pl.ds(0, BLOCK), :] = jnp.zeros((BLOCK, 128), BF16)

        def norm_step(i, carry):
            rows = pl.ds(pl.multiple_of(i * tn, tn), tn)
            qv = q_ref[rows, :]
            kv = k_ref[rows, :]
            qn[rows, :] = (qv * lax.rsqrt(_split_dot(qv * qv, bdv) * (1.0 / HEAD_DIM) + EPS)) * (qg_ref[...] * SCALE)
            kn[rows, :] = (kv * lax.rsqrt(_split_dot(kv * kv, bdv) * (1.0 / HEAD_DIM) + EPS)) * kg_ref[...]
            return carry
        lax.fori_loop(0, S // tn, norm_step, 0)

        def fill(blk, d):
            tokens = _block_tokens(blk, d, S)
            kc[_padded_block(blk), :] = kn[tokens, :].astype(BF16)
            vc[_padded_block(blk), :] = v_ref[tokens, :].astype(BF16)

        ones_bf = jnp.ones((2 * BLOCK, 128), BF16)

        def block(blk, d):
            tokens = _block_tokens(blk, d, S)
            keys = pl.ds(pl.multiple_of(blk * BLOCK, BLOCK), 2 * BLOCK)
            q2 = _two_heads(qn[tokens, :].astype(BF16), lo)
            s = jnp.where(band_mask, _nt(q2, kc[keys, :]), NEG) + _first_block_bias(blk, S // d // BLOCK)
            m = jnp.max(s, axis=-1, keepdims=True)
            e = jnp.exp((s - m).astype(BF16))
            ol = jnp.dot(e, jnp.concatenate([vc[keys, :], ones_bf], axis=1), preferred_element_type=F32)
            l = ol[:, 128:]
            o2 = ol[:, :128] * (1.0 / l)
            lse2 = m + jnp.log(l)
            o = jnp.where(lo, o2[:BLOCK], o2[BLOCK:])
            lse = jnp.where(lo, lse2[:BLOCK], lse2[BLOCK:])
            if d != CONFIG_ORDER[0]:
                la = lse_ref[tokens, :]
                mx = jnp.maximum(la, lse)
                wa, wb = jnp.exp(la - mx), jnp.exp(lse - mx)
                t = wa + wb
                o = (wa * att_ref[tokens, :] + wb * o) / t
                lse = mx + jnp.log(t)
            att_ref[tokens, :] = o
            lse_ref[tokens, :] = lse

        for d in CONFIG_ORDER:
            _for_blocks(S // BLOCK, 4, functools.partial(fill, d=d))
            _for_blocks(S // BLOCK, 16, functools.partial(block, d=d))

        def gate_step(i, carry):
            rows = pl.ds(pl.multiple_of(i * tn, tn), tn)
            silu, _ = _silu_parts(g_ref[rows, :])
            y_ref[rows, :] = (att_ref[rows, :] * silu).astype(BF16)
            return carry
        lax.fori_loop(0, S // tn, gate_step, 0)

    col = lambda j0: pl.BlockSpec((S, 128), lambda p, j0=j0: (0, j0 + p))
    const = lambda shape: pl.BlockSpec(shape, lambda p: (0,) * len(shape))
    out = pl.BlockSpec((S, 128), lambda p: (0, p))
    return pl.pallas_call(
        body, name="attn_fwd", grid=(npairs,),
        in_specs=[col(COL_AQ), col(COL_AK), col(COL_AV), col(COL_AG), const((1, 128)), const((1, 128)),
                  const((128, 128))],
        out_specs=[out, out, out],
        out_shape=[pltpu.HBM((S, ATTN_WIDTH), BF16), pltpu.HBM((S, ATTN_WIDTH), F32),
                   pltpu.HBM((S, ATTN_WIDTH), F32)],
        scratch_shapes=[pltpu.VMEM((S, 128), F32), pltpu.VMEM((S, 128), F32),
                        pltpu.VMEM((S + BLOCK, 128), BF16), pltpu.VMEM((S + BLOCK, 128), BF16)],
        compiler_params=_params(48, ("arbitrary",)),
    )(*_hbm(proj, proj, proj, proj, qg2, kg2, bd))


def _attn_bwd(proj, dycat, att, lse, qg2, kg2, bd):
    S = proj.shape[0]
    npairs = ATTN_WIDTH // 128
    tn = 512

    def body(q_ref, k_ref, v_ref, g_ref, dy_ref, att_ref, lse_ref, qg_ref, kg_ref, bd_ref,
             dq_ref, dk_ref, dv_ref, dg_ref, dqg_ref, dkg_ref,
             qn, kn, rq_s, rk_s, kc, vc, do_s, dd_s, dqa, dka, dva):
        bdv = bd_ref[...]
        lo = lax.broadcasted_iota(jnp.int32, (BLOCK, 128), 1) < HEAD_DIM
        kc[pl.ds(0, BLOCK), :] = jnp.zeros((BLOCK, 128), BF16)
        vc[pl.ds(0, BLOCK), :] = jnp.zeros((BLOCK, 128), BF16)

        def prepare(i, carry):
            rows = pl.ds(pl.multiple_of(i * tn, tn), tn)
            qv = q_ref[rows, :]
            kv = k_ref[rows, :]
            rq = lax.rsqrt(_split_dot(qv * qv, bdv) * (1.0 / HEAD_DIM) + EPS)
            rk = lax.rsqrt(_split_dot(kv * kv, bdv) * (1.0 / HEAD_DIM) + EPS)
            rq_s[rows, :] = rq
            rk_s[rows, :] = rk
            qn[rows, :] = (qv * rq) * (qg_ref[...] * SCALE)
            kn[rows, :] = (kv * rk) * kg_ref[...]
            silu, dsilu = _silu_parts(g_ref[rows, :])
            dy = dy_ref[rows, :]
            at = att_ref[rows, :]
            do = dy * silu
            do_s[rows, :] = do
            dd_s[rows, :] = _split_dot(do * at, bdv)
            dg_ref[rows, :] = (dy * at * dsilu).astype(BF16)
            dka[rows, :] = jnp.zeros((tn, 128), F32)
            dva[rows, :] = jnp.zeros((tn, 128), F32)
            return carry
        lax.fori_loop(0, S // tn, prepare, 0)

        kt = lax.broadcasted_iota(jnp.int32, (2 * BLOCK, 2 * BLOCK), 0)
        qt = lax.broadcasted_iota(jnp.int32, (2 * BLOCK, 2 * BLOCK), 1) % BLOCK
        band_mask_t = ((kt < BLOCK) & (kt >= qt)) | ((kt >= BLOCK) & ((kt - BLOCK) <= qt))

        def per_query_row(t):
            tt = t.T
            return jnp.concatenate([tt[0:1, :], tt[HEAD_DIM:HEAD_DIM + 1, :]], axis=1)

        def fill(blk, d):
            tokens = _block_tokens(blk, d, S)
            kc[_padded_block(blk), :] = kn[tokens, :].astype(BF16)
            vc[_padded_block(blk), :] = v_ref[tokens, :].astype(BF16)

        def block(blk, d):
            tokens = _block_tokens(blk, d, S)
            keys = pl.ds(pl.multiple_of(blk * BLOCK, BLOCK), 2 * BLOCK)
            first = (blk & (S // d // BLOCK - 1)) == 0
            q2 = _two_heads(qn[tokens, :].astype(BF16), lo)
            do2 = _two_heads(do_s[tokens, :].astype(BF16), lo)
            lse_row = per_query_row(lse_ref[tokens, :])
            dd_row = per_query_row(dd_s[tokens, :])
            kb = kc[keys, :]
            vb = vc[keys, :]
            st = jnp.where(band_mask_t, _nt(kb, q2), NEG)
            st = jnp.concatenate([st[:BLOCK] + jnp.where(first, NEG, 0.0), st[BLOCK:]], axis=0)
            pt = jnp.exp(st - lse_row)
            dst = pt * (_nt(vb, do2) - dd_row)
            ptb = pt.astype(BF16)
            dstb = dst.astype(BF16)
            dv_band = jnp.dot(ptb, do2, preferred_element_type=F32)
            dk_band = jnp.dot(dstb, q2, preferred_element_type=F32)
            before = _block_tokens(jnp.where(first, blk, blk - 1), d, S)
            dka[before, :] = dka[before, :] + dk_band[:BLOCK]
            dva[before, :] = dva[before, :] + dv_band[:BLOCK]
            dka[tokens, :] = dka[tokens, :] + dk_band[BLOCK:]
            dva[tokens, :] = dva[tokens, :] + dv_band[BLOCK:]
            dq2 = _tn(dstb, kb)
            dq = jnp.where(lo, dq2[:BLOCK], dq2[BLOCK:])
            dqa[tokens, :] = dq if d == CONFIG_ORDER[0] else dqa[tokens, :] + dq

        for d in CONFIG_ORDER:
            _for_blocks(S // BLOCK, 4, functools.partial(fill, d=d))
            _for_blocks(S // BLOCK, 8, functools.partial(block, d=d))

        def out_step(i, carry):
            dqg, dkg = carry
            rows = pl.ds(pl.multiple_of(i * tn, tn), tn)
            rq = rq_s[rows, :]
            rk = rk_s[rows, :]
            qh = q_ref[rows, :] * rq
            kh = k_ref[rows, :] * rk
            dqs = dqa[rows, :] * SCALE
            dkn = dka[rows, :]
            aq = dqs * qg_ref[...]
            ak = dkn * kg_ref[...]
            dq_ref[rows, :] = (rq * (aq - qh * (_split_dot(aq * qh, bdv) * (1.0 / HEAD_DIM)))).astype(BF16)
            dk_ref[rows, :] = (rk * (ak - kh * (_split_dot(ak * kh, bdv) * (1.0 / HEAD_DIM)))).astype(BF16)
            dv_ref[rows, :] = dva[rows, :].astype(BF16)
            dqg = dqg + jnp.sum(dqs * qh, axis=0, keepdims=True)
            dkg = dkg + jnp.sum(dkn * kh, axis=0, keepdims=True)
            return dqg, dkg
        zero = jnp.zeros((1, 128), F32)
        dqg, dkg = lax.fori_loop(0, S // tn, out_step, (zero, zero))
        dqg_ref[0] = dqg
        dkg_ref[0] = dkg

    col = lambda j0: pl.BlockSpec((S, 128), lambda p, j0=j0: (0, j0 + p))
    col1 = lambda j0: pl.BlockSpec((S, 128), lambda p, j0=j0: (0, j0 + p), pipeline_mode=pl.Buffered(1))
    const = lambda shape: pl.BlockSpec(shape, lambda p: (0,) * len(shape))
    out = pl.BlockSpec((S, 128), lambda p: (0, p))
    gain_out = pl.BlockSpec((1, 1, 128), lambda p: (p, 0, 0))
    piece = pltpu.HBM((S, ATTN_WIDTH), BF16)
    gains = pltpu.HBM((npairs, 1, 128), F32)
    f32buf = pltpu.VMEM((S, 128), F32)
    bf16pad = pltpu.VMEM((S + BLOCK, 128), BF16)
    return pl.pallas_call(
        body, name="attn_bwd", grid=(npairs,),
        in_specs=[col(COL_AQ), col(COL_AK), col(COL_AV), col1(COL_AG), col1(GMLP_WIDTH // 128), col1(0), col(0),
                  const((1, 128)), const((1, 128)), const((128, 128))],
        out_specs=[out, out, out, out, gain_out, gain_out],
        out_shape=[piece, piece, piece, piece, gains, gains],
        scratch_shapes=[f32buf, f32buf, f32buf, f32buf, bf16pad, bf16pad, f32buf, f32buf, f32buf, f32buf, f32buf],
        compiler_params=_params(60, ("arbitrary",)),
    )(*_hbm(proj, proj, proj, proj, dycat, att, lse, qg2, kg2, bd))


def _mem_kv(mem, gain, wkv_bf, kg4, bd):
    def body(mem_ref, g_ref, w_ref, kg_ref, bd_ref, hm_ref, kraw_ref, mk_ref, mv_ref):
        mv_ = mem_ref[...]
        r = lax.rsqrt(jnp.mean(mv_ * mv_, axis=-1, keepdims=True) + EPS)
        hm = ((mv_ * r) * g_ref[...]).astype(BF16)
        hm_ref[...] = hm
        kv = jnp.dot(hm, w_ref[...], preferred_element_type=F32)
        kraw = kv[:, :MEM_WIDTH]
        kraw_ref[...] = kraw
        ms = _split_dot(kraw * kraw, bd_ref[...]) * (1.0 / HEAD_DIM)
        mk_ref[...] = (kraw * lax.rsqrt(ms + EPS)) * kg_ref[...]
        mv_ref[...] = kv[:, MEM_WIDTH:]

    sq = jax.ShapeDtypeStruct((MEM_LEN, MEM_WIDTH), F32)
    return pl.pallas_call(
        body, name="mem_kv",
        out_shape=[jax.ShapeDtypeStruct((MEM_LEN, D_MODEL), BF16), sq, sq, sq],
        compiler_params=_params(16),
    )(mem, gain, wkv_bf, kg4, bd)


def _mem_fwd(proj, mk, mv, qg4, bd):
    S = proj.shape[0]
    tm = 1024

    def body(q_ref, g_ref, mk_ref, mv_ref, qg_ref, bd_ref, y_ref, om_ref):
        qv = q_ref[...]
        ms = _split_dot(qv * qv, bd_ref[...]) * (1.0 / HEAD_DIM)
        qs = (qv * lax.rsqrt(ms + EPS)) * (qg_ref[...] * SCALE)
        mkb = mk_ref[...].astype(BF16)
        mvb = mv_ref[...].astype(BF16)
        head = _head_index((tm, MEM_WIDTH))
        o = jnp.zeros((tm, MEM_WIDTH), F32)
        for h in range(4):
            s = _nt(jnp.where(head == h, qs, 0.0).astype(BF16), mkb)
            e = jnp.exp(s - jnp.max(s, axis=-1, keepdims=True))
            p = e * (1.0 / jnp.sum(e, axis=-1, keepdims=True))
            o = jnp.where(head == h, jnp.dot(p.astype(BF16), mvb, preferred_element_type=F32), o)
        om_ref[...] = o
        silu, _ = _silu_parts(g_ref[...])
        y_ref[...] = (o * silu).astype(BF16)

    col = lambda j: pl.BlockSpec((tm, MEM_WIDTH), lambda i, j=j: (i, j))
    const = lambda shape: pl.BlockSpec(shape, lambda i: (0,) * len(shape))
    tile = pl.BlockSpec((tm, MEM_WIDTH), lambda i: (i, 0))
    return pl.pallas_call(
        body, name="mem_fwd", grid=(S // tm,),
        in_specs=[col(11), col(12), const((MEM_LEN, MEM_WIDTH)), const((MEM_LEN, MEM_WIDTH)), const((1, MEM_WIDTH)),
                  const((MEM_WIDTH, MEM_WIDTH))],
        out_specs=[tile, tile],
        out_shape=[pltpu.HBM((S, MEM_WIDTH), BF16), pltpu.HBM((S, MEM_WIDTH), F32)],
        compiler_params=_params(24, ("arbitrary",)),
    )(*_hbm(proj, proj, mk, mv, qg4, bd))


def _mem_bwd(proj, dycat, om, mk, mv, qg4, bd):
    S = proj.shape[0]
    tm = 1024

    def body(q_ref, g_ref, dy_ref, om_ref, mk_ref, mv_ref, qg_ref, bd_ref,
             dq_ref, dg_ref, dmk_ref, dmv_ref, dqg_ref):
        i = pl.program_id(0)

        @pl.when(i == 0)
        def _():
            dmk_ref[...] = jnp.zeros_like(dmk_ref)
            dmv_ref[...] = jnp.zeros_like(dmv_ref)
            dqg_ref[...] = jnp.zeros_like(dqg_ref)

        bdv = bd_ref[...]
        qv = q_ref[...]
        rq = lax.rsqrt(_split_dot(qv * qv, bdv) * (1.0 / HEAD_DIM) + EPS)
        qh = qv * rq
        qs = qh * (qg_ref[...] * SCALE)
        silu, dsilu = _silu_parts(g_ref[...])
        dy = dy_ref[...]
        o = om_ref[...]
        do = dy * silu
        dg_ref[...] = (dy * o * dsilu).astype(BF16)
        dd = _split_dot(do * o, bdv)
        mkb = mk_ref[...].astype(BF16)
        mvb = mv_ref[...].astype(BF16)
        head = _head_index((tm, MEM_WIDTH))
        dqs = jnp.zeros((tm, MEM_WIDTH), F32)
        for h in range(4):
            qhd = jnp.where(head == h, qs, 0.0).astype(BF16)
            doh = jnp.where(head == h, do, 0.0).astype(BF16)
            s = _nt(qhd, mkb)
            e = jnp.exp(s - jnp.max(s, axis=-1, keepdims=True))
            p = e * (1.0 / jnp.sum(e, axis=-1, keepdims=True))
            ds = p * (_nt(doh, mvb) - dd[:, h * HEAD_DIM:h * HEAD_DIM + 1])
            dsb = ds.astype(BF16)
            dmv_ref[...] += _tn(p.astype(BF16), doh)
            dmk_ref[...] += _tn(dsb, qhd)
            dqs = jnp.where(head == h, jnp.dot(dsb, mkb, preferred_element_type=F32), dqs)
        dqs = dqs * SCALE
        a = dqs * qg_ref[...]
        dq_ref[...] = (rq * (a - qh * (_split_dot(a * qh, bdv) * (1.0 / HEAD_DIM)))).astype(BF16)
        dqg_ref[...] += jnp.sum(dqs * qh, axis=0, keepdims=True)

    col = lambda j: pl.BlockSpec((tm, MEM_WIDTH), lambda i, j=j: (i, j))
    const = lambda shape: pl.BlockSpec(shape, lambda i: (0,) * len(shape))
    tile = pl.BlockSpec((tm, MEM_WIDTH), lambda i: (i, 0))
    piece = pltpu.HBM((S, MEM_WIDTH), BF16)
    sq = pltpu.HBM((MEM_LEN, MEM_WIDTH), F32)
    return pl.pallas_call(
        body, name="mem_bwd", grid=(S // tm,),
        in_specs=[col(11), col(12), col(3), tile, const((MEM_LEN, MEM_WIDTH)), const((MEM_LEN, MEM_WIDTH)),
                  const((1, MEM_WIDTH)), const((MEM_WIDTH, MEM_WIDTH))],
        out_specs=[tile, tile, const((MEM_LEN, MEM_WIDTH)), const((MEM_LEN, MEM_WIDTH)), const((1, MEM_WIDTH))],
        out_shape=[piece, piece, sq, sq, pltpu.HBM((1, MEM_WIDTH), F32)],
        compiler_params=_params(32, ("arbitrary",)),
    )(*_hbm(proj, proj, dycat, om, mk, mv, qg4, bd))


def _mem_kv_bwd(dmk, dmv, kraw, mem, gain, kg4, wkv_bf, hm_bf, bd):
    def body(dmk_ref, dmv_ref, kraw_ref, mem_ref, g_ref, kg_ref, w_ref, hm_ref, bd_ref, dw_ref, dg_ref, dkg_ref):
        bdv = bd_ref[...]
        kraw = kraw_ref[...]
        rk = lax.rsqrt(_split_dot(kraw * kraw, bdv) * (1.0 / HEAD_DIM) + EPS)
        kh = kraw * rk
        dmkv = dmk_ref[...]
        a = dmkv * kg_ref[...]
        dkraw = rk * (a - kh * (_split_dot(a * kh, bdv) * (1.0 / HEAD_DIM)))
        dkg_ref[...] = jnp.sum(dmkv * kh, axis=0, keepdims=True)
        dkv = jnp.concatenate([dkraw, dmv_ref[...]], axis=1).astype(BF16)
        dw = _tn(hm_ref[...], dkv).astype(BF16)
        rows_blk = D_MODEL // N_DEV
        for j in range(N_DEV):
            dw_ref[j] = dw[rows_blk * j:rows_blk * (j + 1)]
        dhm = _nt(dkv, w_ref[...])
        mv_ = mem_ref[...]
        r = lax.rsqrt(jnp.mean(mv_ * mv_, axis=-1, keepdims=True) + EPS)
        dg_ref[...] = jnp.sum(dhm * (mv_ * r), axis=0, keepdims=True)

    return pl.pallas_call(
        body, name="mem_kv_bwd",
        out_shape=[jax.ShapeDtypeStruct((N_DEV, D_MODEL // N_DEV, 2 * MEM_WIDTH), BF16),
                   jax.ShapeDtypeStruct((1, D_MODEL), F32), jax.ShapeDtypeStruct((1, MEM_WIDTH), F32)],
        compiler_params=_params(24),
    )(dmk, dmv, kraw, mem, gain, kg4, wkv_bf, hm_bf, bd)


def _out_loss(yg, ya, ym, x, tgt, wout_bf):
    S, D = x.shape
    tm = 1024
    nsteps = S // tm
    rows_blk = D // N_DEV

    def body(yg_ref, ya_ref, ym_ref, x_ref, t_ref, w_ref, dout_ref, dycat_ref, dw_ref, loss_ref, acc_ref):
        i = pl.program_id(0)

        @pl.when(i == 0)
        def _():
            acc_ref[...] = jnp.zeros_like(acc_ref)
            loss_ref[...] = jnp.zeros_like(loss_ref)

        ycat = jnp.concatenate([yg_ref[...], ya_ref[...], ym_ref[...]], axis=1)
        w = w_ref[...]
        diff = (x_ref[...] + jnp.dot(ycat, w, preferred_element_type=F32)) - t_ref[...]
        loss_ref[...] += jnp.sum(diff * diff, axis=0, keepdims=True)
        dout = diff * (1.0 / D)
        dout_ref[...] = dout
        db = dout.astype(BF16)
        dycat_ref[...] = _nt(db, w)
        acc_ref[...] += _tn(ycat, db)

        @pl.when(i == nsteps - 1)
        def _():
            for j in range(N_DEV):
                dw_ref[j] = acc_ref[rows_blk * j:rows_blk * (j + 1), :].astype(BF16)

    tile = lambda w: pl.BlockSpec((tm, w), lambda i: (i, 0))
    const = lambda shape: pl.BlockSpec(shape, lambda i: (0,) * len(shape))
    return pl.pallas_call(
        body, name="out_loss", grid=(nsteps,),
        in_specs=[tile(GMLP_WIDTH), tile(ATTN_WIDTH), tile(MEM_WIDTH), tile(D), tile(D), const((D, D))],
        out_specs=[tile(D), tile(D), const((N_DEV, rows_blk, D)), const((1, D))],
        out_shape=[pltpu.HBM((S, D), F32), pltpu.HBM((S, D), F32),
                   pltpu.HBM((N_DEV, rows_blk, D), BF16), pltpu.HBM((1, D), F32)],
        scratch_shapes=[pltpu.VMEM((D, D), F32)],
        compiler_params=_params(60, ("arbitrary",)),
    )(*_hbm(yg, ya, ym, x, tgt, wout_bf))


def _piece_specs(pieces, tm):
    return [pl.BlockSpec((tm, p.shape[1]), lambda i: (i, 0)) for p in pieces]


def _in_bwd_dx(pieces, x, dout, gain, w_t, dw_blocks):
    S, D = x.shape
    N = w_t.shape[0]
    tm = 256
    n = len(pieces)
    nsteps = S // tm
    middle_step = nsteps // 8

    def body(*refs):
        piece_refs = refs[:n]
        x_ref, dout_ref, g_ref, w_ref, dwb_ref, gx_ref, dg_ref, gw_ref = refs[n:n + 8]
        rs = _ReduceScatter([dwb_ref], [gw_ref], *refs[n + 8:])
        i = pl.program_id(0)

        @pl.when(i == 0)
        def _():
            dg_ref[...] = jnp.zeros_like(dg_ref)
            rs.start()

        @pl.when(i == middle_step)
        def _():
            rs.middle()

        dproj = jnp.concatenate([r[...] for r in piece_refs], axis=1)
        dh = jnp.dot(dproj, w_ref[...], preferred_element_type=F32)
        xv = x_ref[...]
        r = lax.rsqrt(jnp.mean(xv * xv, axis=-1, keepdims=True) + EPS)
        xh = xv * r
        a = dh * g_ref[...]
        gx_ref[...] = dout_ref[...] + r * (a - xh * jnp.mean(a * xh, axis=-1, keepdims=True))
        dg_ref[...] += jnp.sum(dh * xh, axis=0, keepdims=True)

        @pl.when(i == nsteps - 1)
        def _():
            rs.finish()

    tile = pl.BlockSpec((tm, D), lambda i: (i, 0))
    const = lambda shape: pl.BlockSpec(shape, lambda i: (0,) * len(shape))
    vmem = pl.BlockSpec(memory_space=pltpu.VMEM)
    return pl.pallas_call(
        body, name="in_bwd_dx", grid=(nsteps,),
        in_specs=_piece_specs(pieces, tm)
        + [tile, tile, const((1, D)), pl.BlockSpec((N, D), lambda i: (0, 0), pipeline_mode=pl.Buffered(1)), vmem],
        out_specs=[tile, const((1, D)), vmem],
        out_shape=[pltpu.HBM((S, D), F32), pltpu.HBM((1, D), F32), jax.ShapeDtypeStruct(dw_blocks.shape[1:], F32)],
        scratch_shapes=_reduce_scatter_scratch([dw_blocks]),
        compiler_params=_params(56, ("arbitrary",)),
    )(*_hbm(*pieces, x, dout, gain, w_t), dw_blocks)


def _in_bwd_dw(pieces, h_bf, others):
    S, D = h_bf.shape
    N = sum(p.shape[1] for p in pieces)
    n_blk = N // N_DEV
    tm = 512
    n = len(pieces)
    k = len(others)
    nsteps = S // tm

    def body(*refs):
        piece_refs = refs[:n]
        h_ref = refs[n]
        other_refs = refs[n + 1:n + 1 + k]
        dw_ref = refs[n + 1 + k]
        sum_refs = refs[n + 2 + k:n + 2 + 2 * k]
        acc_ref = refs[n + 2 + 2 * k]
        rs = _ReduceScatter(other_refs, sum_refs, *refs[n + 3 + 2 * k:])
        i = pl.program_id(0)

        @pl.when(i == 0)
        def _():
            acc_ref[...] = jnp.zeros_like(acc_ref)
            rs.start()

        @pl.when(i == 1)
        def _():
            rs.middle()

        dproj = jnp.concatenate([r[...] for r in piece_refs], axis=1)
        acc_ref[...] += _tn(h_ref[...], dproj)

        @pl.when(i == nsteps - 1)
        def _():
            for j in range(N_DEV):
                dw_ref[j] = acc_ref[:, n_blk * j:n_blk * (j + 1)].T.astype(BF16)
            rs.finish()

    vmem = pl.BlockSpec(memory_space=pltpu.VMEM)
    return pl.pallas_call(
        body, name="in_bwd_dw", grid=(nsteps,),
        in_specs=_piece_specs(pieces, tm) + [pl.BlockSpec((tm, D), lambda i: (i, 0))] + [vmem] * k,
        out_specs=[pl.BlockSpec((N_DEV, n_blk, D), lambda i: (0, 0, 0))] + [vmem] * k,
        out_shape=[pltpu.HBM((N_DEV, n_blk, D), BF16)] + [jax.ShapeDtypeStruct(o.shape[1:], F32) for o in others],
        scratch_shapes=[pltpu.VMEM((D, N), F32)] + _reduce_scatter_scratch(others),
        compiler_params=_params(56, ("arbitrary",)),
    )(*_hbm(*pieces, h_bf), *others)


def _row_step(m):
    return max(t for t in range(16, 257, 16) if m % t == 0)


def _place():
    x, y, c = lax.axis_index("x"), lax.axis_index("y"), lax.axis_index("c")
    chips = [(1 - x, y), (x, 1 - y), (1 - x, 1 - y)]
    return x, y, c, chips


class _AllGather:
    def __init__(self, srcs, outs, send_sems, recv_sems, local_sems, first_sem=0):
        self.srcs, self.outs, self.n, self.first_sem = srcs, outs, len(srcs), first_sem
        self.send_sems, self.recv_sems, self.local_sems = send_sems, recv_sems, local_sems

    def _rows(self, a, px, py, pc):
        m = self.srcs[a].shape[0]
        return self.outs[a].at[pl.ds((4 * px + 2 * py + pc) * m, m), :]

    def _copy(self, a, k, block, to, src=None):
        row = self.first_sem + a
        return pltpu.make_async_remote_copy(
            src_ref=self._rows(a, *block) if src is None else src, dst_ref=self._rows(a, *block),
            send_sem=self.send_sems.at[row, k], recv_sem=self.recv_sems.at[row, k], device_id=to, device_id_type=MESH)

    def _mine(self):
        x, y, c, _ = _place()
        return [pltpu.make_async_copy(self.srcs[a], self._rows(a, x, y, c), self.local_sems.at[self.first_sem + a])
                for a in range(self.n)]

    def _first(self, far):
        x, y, c, chips = _place()
        out = []
        for a in range(self.n):
            if far:
                out.append(self._copy(a, 3, (x, y, c), (*chips[2], c), src=self.srcs[a]))
            else:
                out.append(self._copy(a, 0, (x, y, c), (x, y, 1 - c), src=self.srcs[a]))
                out += [self._copy(a, 1 + j, (x, y, c), (*chips[j], c), src=self.srcs[a]) for j in (1, 0)]
        return out

    def _passed(self, j):
        x, y, c, chips = _place()
        return [self._copy(a, 4 + j, (*chips[j], c), (x, y, 1 - c)) for a in range(self.n)]

    def start(self):
        for cp in self._mine() + self._first(far=False):
            cp.start()

    def start_far(self):
        for cp in self._first(far=True):
            cp.start()

    def from_chip(self, j):
        x, y, c, chips = _place()
        for a in range(self.n):
            self._copy(a, 1 + j, (*chips[j], c), (x, y, c)).wait_recv()
        for cp in self._passed(j):
            cp.start()

    def from_sibling(self, j=None):
        x, y, c, chips = _place()
        for a in range(self.n):
            block = (x, y, 1 - c) if j is None else (*chips[j], 1 - c)
            self._copy(a, 0 if j is None else 4 + j, block, (x, y, c)).wait_recv()

    def from_self(self):
        for cp in self._mine():
            cp.wait()

    def finish(self):
        for cp in (self._first(far=False) + self._first(far=True)
                   + self._passed(0) + self._passed(1) + self._passed(2)):
            cp.wait_send()

    def run(self):
        self.start()
        self.start_far()
        self.from_self()
        for j in range(3):
            self.from_chip(j)
        self.from_sibling()
        for j in range(3):
            self.from_sibling(j)
        self.finish()


def _gather_proj(x, gain, shards, xpos):
    S, D = x.shape
    n = len(shards)
    N = N_DEV * shards[0].shape[0]
    half = N // 2
    tm = 1024
    nsteps = S // tm

    def body(*refs):
        xpos_ref, x_ref, g_ref = refs[:3]
        ins = refs[3:3 + n]
        proj_ref, h_ref = refs[3 + n:5 + n]
        outs = refs[5 + n:5 + 2 * n]
        casts = refs[5 + 2 * n:5 + 3 * n]
        whole = refs[5 + 3 * n:5 + 4 * n]
        sems = refs[5 + 4 * n:8 + 4 * n]
        ag = _AllGather(casts[:1], whole[:1], *sems)
        later = _AllGather(casts[1:], whole[1:], *sems, first_sem=1)
        out_sems, h_all = refs[8 + 4 * n:]
        p, i = pl.program_id(0), pl.program_id(1)
        rows = pl.ds(pl.multiple_of(i * tm, tm), tm)

        @pl.when((p == 0) & (i == 0))
        def _():
            for a in range(n):
                tr = _row_step(ins[a].shape[0])

                def cast(r, carry, a=a, tr=tr):
                    at = pl.ds(pl.multiple_of(r * tr, tr), tr)
                    casts[a][at, :] = ins[a][at, :].astype(BF16)
                    return carry
                lax.fori_loop(0, ins[a].shape[0] // tr, cast, 0)
            ag.start()

        @pl.when(p == 0)
        def _():
            xv = x_ref[...]
            r = lax.rsqrt(jnp.mean(xv * xv, axis=-1, keepdims=True) + EPS)
            h = ((xv * r) * g_ref[...]).astype(BF16)
            h_ref[...] = h
            h_all[rows, :] = h

        @pl.when((p == 1) & (i == 0))
        def _():
            ag.from_self()
            ag.from_chip(1)
            ag.start_far()
            later.start()
            later.start_far()
            ag.from_sibling()
            ag.from_sibling(1)

        @pl.when((p == 2) & (i == 0))
        def _():
            for j in (0, 2):
                ag.from_chip(j)
            for j in (0, 2):
                ag.from_sibling(j)

        @pl.when(p > 0)
        def _():
            which = (xpos_ref[0] + p - 1) % 2
            w_half = whole[0][pl.ds(pl.multiple_of(which * half, half), half), :]
            proj_ref[...] = _nt(h_all[rows, :], w_half)

        @pl.when((p == 2) & (i == nsteps - 1))
        def _():
            ag.finish()
            later.from_self()
            for j in range(3):
                later.from_chip(j)
            later.from_sibling()
            for j in range(3):
                later.from_sibling(j)
            later.finish()
            to_results = [pltpu.make_async_copy(whole[a], outs[a], out_sems.at[a]) for a in range(n)]
            for cp in to_results:
                cp.start()
            for cp in to_results:
                cp.wait()

    vmem = pl.BlockSpec(memory_space=pltpu.VMEM)
    hbm = pl.BlockSpec(memory_space=pl.ANY)
    gathered = [(N_DEV * a.shape[0], a.shape[1]) for a in shards]
    x_tile = lambda p, i, xp: (jnp.where(p == 0, i, nsteps - 1), 0)
    proj_tile = lambda p, i, xp: (jnp.where(p == 0, 0, i), (xp[0] + jnp.maximum(p - 1, 0)) % 2)
    grid_spec = pltpu.PrefetchScalarGridSpec(
        num_scalar_prefetch=1, grid=(3, nsteps),
        in_specs=[pl.BlockSpec((tm, D), x_tile), pl.BlockSpec((1, D), lambda p, i, xp: (0, 0))] + [vmem] * n,
        out_specs=[pl.BlockSpec((tm, half), proj_tile), pl.BlockSpec((tm, D), x_tile)] + [hbm] * n,
        scratch_shapes=[pltpu.VMEM(a.shape, BF16) for a in shards] + [pltpu.VMEM(g, BF16) for g in gathered]
        + [pltpu.SemaphoreType.DMA((n, 7)), pltpu.SemaphoreType.DMA((n, 7)), pltpu.SemaphoreType.DMA((n,)),
           pltpu.SemaphoreType.DMA((n,)), pltpu.VMEM((S, D), BF16)])
    return pl.pallas_call(
        body, name="gather_proj", grid_spec=grid_spec,
        out_shape=[pltpu.HBM((S, N), F32), pltpu.HBM((S, D), BF16)] + [pltpu.HBM(g, BF16) for g in gathered],
        compiler_params=_params(56, ("arbitrary", "arbitrary")),
    )(xpos, *_hbm(x, gain), *shards)


ROW_NORM, ROW_MEM_NORM, ROW_V_GAIN, ROW_B, ROW_ATTN_GAINS, ROW_MEM_GAINS, ROW_W_S, ROW_LOSS = 0, 8, 16, 18, 22, 23, 24, 536
SMALL_ROWS = 544


def _gather_small(dgain, dmgain, dvg, db2, dqg, dkg, dmqg, dmkg, dws, sq):
    def body(dgain_ref, dmgain_ref, dvg_ref, db2_ref, dqg_ref, dkg_ref, dmqg_ref, dmkg_ref, dws_ref, sq_ref,
             out_ref, mine, send_sems, recv_sems, local_sems):
        first = lax.broadcasted_iota(jnp.int32, (1, 128), 1) < HEAD_DIM
        for i in range(8):
            cols = slice(128 * i, 128 * (i + 1))
            mine[ROW_NORM + i:ROW_NORM + i + 1, :] = dgain_ref[:, cols]
            mine[ROW_MEM_NORM + i:ROW_MEM_NORM + i + 1, :] = dmgain_ref[:, cols]
            mine[ROW_LOSS + i:ROW_LOSS + i + 1, :] = sq_ref[:, cols]
        mine[ROW_V_GAIN:ROW_V_GAIN + 1, :] = dvg_ref[:, 0:128]
        mine[ROW_V_GAIN + 1:ROW_V_GAIN + 2, :] = dvg_ref[:, 128:256]
        bt = db2_ref[...].T
        for h in range(4):
            mine[ROW_B + h:ROW_B + h + 1, :] = bt[HEAD_DIM * h:HEAD_DIM * h + 1, :]

        def fold_heads(t):
            return t + pltpu.roll(t, HEAD_DIM, axis=1)
        aq = fold_heads(dqg_ref[0] + dqg_ref[1] + dqg_ref[2] + dqg_ref[3])
        ak = fold_heads(dkg_ref[0] + dkg_ref[1] + dkg_ref[2] + dkg_ref[3])
        mine[ROW_ATTN_GAINS:ROW_ATTN_GAINS + 1, :] = jnp.where(first, aq, ak)
        mq = fold_heads(dmqg_ref[:, 0:128] + dmqg_ref[:, 128:256])
        mk = fold_heads(dmkg_ref[:, 0:128] + dmkg_ref[:, 128:256])
        mine[ROW_MEM_GAINS:ROW_MEM_GAINS + 1, :] = jnp.where(first, mq, mk)
        mine[ROW_W_S:ROW_W_S + 4 * CHUNK, :] = dws_ref[...]
        _AllGather([mine], [out_ref], send_sems, recv_sems, local_sems).run()

    return pl.pallas_call(
        body, name="gather_small_grads",
        out_shape=jax.ShapeDtypeStruct((N_DEV * SMALL_ROWS, 128), F32),
        scratch_shapes=[pltpu.VMEM((SMALL_ROWS, 128), F32), pltpu.SemaphoreType.DMA((1, 7)),
                        pltpu.SemaphoreType.DMA((1, 7)), pltpu.SemaphoreType.DMA((1,))],
        compiler_params=_params(16),
    )(dgain, dmgain, dvg, db2, dqg, dkg, dmqg, dmkg, dws, sq)


def _reduce_scatter_scratch(arrs):
    n = len(arrs)
    return ([pltpu.VMEM((4,) + a.shape[1:], BF16) for a in arrs] + [pltpu.VMEM((3,) + a.shape[1:], BF16) for a in arrs]
            + [pltpu.SemaphoreType.DMA((n, 7)), pltpu.SemaphoreType.DMA((n, 7))])


class _ReduceScatter:
    def __init__(self, ins, outs, *scratch):
        n = len(ins)
        self.n, self.ins, self.outs = n, ins, outs
        self.half, self.quarter = scratch[:n], scratch[n:2 * n]
        self.send_sems, self.recv_sems = scratch[2 * n:]

    def _to_sibling(self):
        x, y, c, _ = _place()
        return [pltpu.make_async_remote_copy(
            src_ref=self.ins[a].at[2 * q + (1 - c)], dst_ref=self.half[a].at[q], send_sem=self.send_sems.at[a, q],
            recv_sem=self.recv_sems.at[a, q], device_id=(x, y, 1 - c), device_id_type=MESH)
            for a in range(self.n) for q in range(4)]

    def _to_chips(self):
        _, _, c, chips = _place()
        return [pltpu.make_async_remote_copy(
            src_ref=self.half[a].at[2 * chip[0] + chip[1]], dst_ref=self.quarter[a].at[k],
            send_sem=self.send_sems.at[a, 4 + k], recv_sem=self.recv_sems.at[a, 4 + k], device_id=(*chip, c),
            device_id_type=MESH) for a in range(self.n) for k, chip in enumerate(chips)]

    def _rows(self, a, fn):
        m = self.ins[a].shape[1]
        tr = _row_step(m)

        def step(i, carry):
            fn(pl.ds(pl.multiple_of(i * tr, tr), tr))
            return carry
        lax.fori_loop(0, m // tr, step, 0)

    def start(self):
        for cp in self._to_sibling():
            cp.start()

    def middle(self):
        _, _, c, _ = _place()
        for cp in self._to_sibling():
            cp.wait_recv()
        for a in range(self.n):
            for q in range(4):
                def add_half(rows, a=a, q=q):
                    both = self.ins[a][2 * q + c, rows, :].astype(F32) + self.half[a][q, rows, :].astype(F32)
                    self.half[a][q, rows, :] = both.astype(BF16)
                self._rows(a, add_half)
        for cp in self._to_chips():
            cp.start()

    def finish(self):
        x, y, _, _ = _place()
        for cp in self._to_chips():
            cp.wait_recv()
        for a in range(self.n):
            def add_quarters(rows, a=a):
                f = lambda t: t.astype(F32)
                self.outs[a][rows, :] = ((f(self.half[a][2 * x + y, rows, :]) + f(self.quarter[a][0, rows, :]))
                                         + (f(self.quarter[a][1, rows, :]) + f(self.quarter[a][2, rows, :])))
            self._rows(a, add_quarters)
        for cp in self._to_sibling() + self._to_chips():
            cp.wait_send()


def _adamw_math(w, g, m, v):
    m = ADAM_B1 * m + (1.0 - ADAM_B1) * g
    v = ADAM_B2 * v + (1.0 - ADAM_B2) * (g * g)
    m_hat = m / (1.0 - ADAM_B1 ** ADAM_STEP)
    v_hat = v / (1.0 - ADAM_B2 ** ADAM_STEP)
    delta = -ADAM_LR * (m_hat / (jnp.sqrt(v_hat) + ADAM_EPS) + ADAM_WD * w)
    return delta, m, v


def _adamw(w, g, m, v, name):
    R, C = w.shape
    tr = _row_step(R)

    def body(w_ref, g_ref, m_ref, v_ref, d_ref, nm_ref, nv_ref):
        d_ref[...], nm_ref[...], nv_ref[...] = _adamw_math(w_ref[...], g_ref[...], m_ref[...], v_ref[...])

    tile = pl.BlockSpec((tr, C), lambda i: (i, 0))
    out = pltpu.HBM((R, C), F32)
    return pl.pallas_call(
        body, name=name, grid=(R // tr,), in_specs=[tile] * 4, out_specs=[tile] * 3, out_shape=[out] * 3,
        compiler_params=_params(16, ("arbitrary",)),
    )(*_hbm(w, g, m, v))


SMALL = ("norm_gain", "gmlp_v_gain", "gmlp_w_s", "gmlp_b", "attn_q_gain", "attn_k_gain", "mem_norm_gain",
         "mem_q_gain", "mem_k_gain")
WEIGHTS = ("norm_gain", "w_in", "gmlp_v_gain", "gmlp_w_s", "gmlp_b", "attn_q_gain", "attn_k_gain",
           "mem_norm_gain", "w_mem_kv", "mem_q_gain", "mem_k_gain", "w_out")


def _adamw_small(w, m, v, g_all):
    k = len(SMALL)
    half = slice(0, HEAD_DIM), slice(HEAD_DIM, 2 * HEAD_DIM)

    def body(*refs):
        w_refs, m_refs, v_refs = refs[:k], refs[k:2 * k], refs[2 * k:3 * k]
        g_ref = refs[3 * k]
        outs = refs[3 * k + 1:7 * k + 1]
        loss_ref, gsum = refs[7 * k + 1:]

        part = SMALL_ROWS // 4
        for p in range(4):
            acc = g_ref[part * p:part * (p + 1), :]
            for dev in range(1, N_DEV):
                acc = acc + g_ref[dev * SMALL_ROWS + part * p:dev * SMALL_ROWS + part * (p + 1), :]
            gsum[part * p:part * (p + 1), :] = acc

        def update(name, at, g):
            i = SMALL.index(name)
            d, nm, nv = _adamw_math(w_refs[i][at], g, m_refs[i][at], v_refs[i][at])
            outs[i][at], outs[k + i][at], outs[2 * k + i][at], outs[3 * k + i][at] = g, d, nm, nv

        for i in range(8):
            at = (slice(0, 1), slice(128 * i, 128 * (i + 1)))
            update("norm_gain", at, gsum[ROW_NORM + i:ROW_NORM + i + 1, :])
            update("mem_norm_gain", at, gsum[ROW_MEM_NORM + i:ROW_MEM_NORM + i + 1, :])
        for h in range(4):
            row = (0, slice(h, h + 1), slice(None))
            update("gmlp_v_gain", row, gsum[ROW_V_GAIN + h // 2:ROW_V_GAIN + h // 2 + 1, half[h % 2]])
            update("gmlp_b", row, gsum[ROW_B + h:ROW_B + h + 1, :])
            update("gmlp_w_s", (0, h), gsum[ROW_W_S + CHUNK * h:ROW_W_S + CHUNK * (h + 1), :])
        whole = (slice(0, 1), slice(None))
        update("attn_q_gain", whole, gsum[ROW_ATTN_GAINS:ROW_ATTN_GAINS + 1, half[0]])
        update("attn_k_gain", whole, gsum[ROW_ATTN_GAINS:ROW_ATTN_GAINS + 1, half[1]])
        update("mem_q_gain", whole, gsum[ROW_MEM_GAINS:ROW_MEM_GAINS + 1, half[0]])
        update("mem_k_gain", whole, gsum[ROW_MEM_GAINS:ROW_MEM_GAINS + 1, half[1]])
        loss_ref[...] = jnp.sum(gsum[ROW_LOSS:ROW_LOSS + 8, :], keepdims=True) * (0.5 / D_MODEL)

    shapes = [jax.ShapeDtypeStruct(w[name].shape, F32) for name in SMALL]
    res = pl.pallas_call(
        body, name="adamw_small",
        out_shape=shapes * 4 + [jax.ShapeDtypeStruct((1, 1), F32)],
        scratch_shapes=[pltpu.VMEM((SMALL_ROWS, 128), F32)],
        compiler_params=_params(16),
    )(*[w[n] for n in SMALL], *[m[n] for n in SMALL], *[v[n] for n in SMALL], g_all)
    trees = [dict(zip(SMALL, res[j * k:(j + 1) * k])) for j in range(4)]
    return (*trees, res[4 * k])


def _grads(x, mem, tgt, w, shards):
    bd128, bd256 = _head_blockdiag(128), _head_blockdiag(256)
    gain = w["norm_gain"].reshape(1, D_MODEL)
    vg = w["gmlp_v_gain"].reshape(1, GMLP_WIDTH)
    w_s = w["gmlp_w_s"].reshape(4, CHUNK, CHUNK)
    b2 = jnp.repeat(w["gmlp_b"].reshape(4, CHUNK).T, HEAD_DIM, axis=1)
    qg2 = jnp.tile(w["attn_q_gain"].reshape(1, HEAD_DIM), (1, 2))
    kg2 = jnp.tile(w["attn_k_gain"].reshape(1, HEAD_DIM), (1, 2))
    mqg4 = jnp.tile(w["mem_q_gain"].reshape(1, HEAD_DIM), (1, 4))
    mkg4 = jnp.tile(w["mem_k_gain"].reshape(1, HEAD_DIM), (1, 4))
    mgain = w["mem_norm_gain"].reshape(1, D_MODEL)

    xpos = lax.axis_index("x").astype(jnp.int32).reshape(1)
    proj, h_bf, win_t, wkv_bf, wout_bf = _gather_proj(x, gain, shards, xpos)
    yg = _gmlp_fwd(proj, vg, w_s, b2, bd256)
    ya, att, lse = _attn_fwd(proj, qg2, kg2, bd128)
    hm_bf, kraw, mk, mv = _mem_kv(mem, mgain, wkv_bf, mkg4, bd256)
    ym, om = _mem_fwd(proj, mk, mv, mqg4, bd256)
    dout, dycat, dwout, sq = _out_loss(yg, ya, ym, x, tgt, wout_bf)

    du, dgv, dgg, dws, db2, dvg = _gmlp_bwd(proj, dycat, vg, w_s, b2, bd256)
    dq, dk, dv, dag, dqg, dkg = _attn_bwd(proj, dycat, att, lse, qg2, kg2, bd128)
    dmq, dmg, dmk, dmv, dmqg = _mem_bwd(proj, dycat, om, mk, mv, mqg4, bd256)
    dwkv, dmgain, dmkg = _mem_kv_bwd(dmk, dmv, kraw, mem, mgain, mkg4, wkv_bf, hm_bf, bd256)
    pieces = [du, dgv, dgg, dq, dk, dv, dag, dmq, dmg]
    dwin, g_wkv, g_wout = _in_bwd_dw(pieces, h_bf, [dwkv, dwout])
    grad_x, dgain, g_win = _in_bwd_dx(pieces, x, dout, gain, win_t, dwin)
    return grad_x, g_win, g_wkv, g_wout, (dgain, dmgain, dvg, db2, dqg, dkg, dmqg, dmkg, dws, sq)


def kernel(x, mem, norm_gain, w_in, gmlp_v_gain, gmlp_w_s, gmlp_b, attn_q_gain, attn_k_gain, mem_norm_gain, w_mem_kv, mem_q_gain, mem_k_gain, w_out, loss_target, m_norm_gain, m_w_in, m_gmlp_v_gain, m_gmlp_w_s, m_gmlp_b, m_attn_q_gain, m_attn_k_gain, m_mem_norm_gain, m_w_mem_kv, m_mem_q_gain, m_mem_k_gain, m_w_out, v_norm_gain, v_w_in, v_gmlp_v_gain, v_gmlp_w_s, v_gmlp_b, v_attn_q_gain, v_attn_k_gain, v_mem_norm_gain, v_w_mem_kv, v_mem_q_gain, v_mem_k_gain, v_w_out):
    w = dict(norm_gain=norm_gain, w_in=w_in, gmlp_v_gain=gmlp_v_gain, gmlp_w_s=gmlp_w_s, gmlp_b=gmlp_b,
             attn_q_gain=attn_q_gain, attn_k_gain=attn_k_gain, mem_norm_gain=mem_norm_gain, w_mem_kv=w_mem_kv,
             mem_q_gain=mem_q_gain, mem_k_gain=mem_k_gain, w_out=w_out)
    m = dict(norm_gain=m_norm_gain, w_in=m_w_in, gmlp_v_gain=m_gmlp_v_gain, gmlp_w_s=m_gmlp_w_s, gmlp_b=m_gmlp_b,
             attn_q_gain=m_attn_q_gain, attn_k_gain=m_attn_k_gain, mem_norm_gain=m_mem_norm_gain,
             w_mem_kv=m_w_mem_kv, mem_q_gain=m_mem_q_gain, mem_k_gain=m_mem_k_gain, w_out=m_w_out)
    v = dict(norm_gain=v_norm_gain, w_in=v_w_in, gmlp_v_gain=v_gmlp_v_gain, gmlp_w_s=v_gmlp_w_s, gmlp_b=v_gmlp_b,
             attn_q_gain=v_attn_q_gain, attn_k_gain=v_attn_k_gain, mem_norm_gain=v_mem_norm_gain,
             w_mem_kv=v_w_mem_kv, mem_q_gain=v_mem_q_gain, mem_k_gain=v_mem_k_gain, w_out=v_w_out)
    transposed = lambda t: jnp.transpose(t[0])

    grad_x, g_win, g_wkv, g_wout, small = _grads(
        x[0], mem[0], loss_target[0], w, [transposed(w_in), w_mem_kv[0], w_out[0]])
    small_all = _gather_small(*small)

    out_g, out_d, out_m, out_v, loss = _adamw_small(w, m, v, small_all)
    d_, m_, v_ = _adamw(transposed(w_in), g_win, transposed(m_w_in), transposed(v_w_in), "adamw_w_in")
    for tree, t in ((out_g, g_win), (out_d, d_), (out_m, m_), (out_v, v_)):
        tree["w_in"] = jnp.transpose(t)[None]
    for name, g in (("w_mem_kv", g_wkv), ("w_out", g_wout)):
        d_, m_, v_ = _adamw(w[name][0], g, m[name][0], v[name][0], "adamw_" + name)
        out_g[name], out_d[name], out_m[name], out_v[name] = g[None], d_[None], m_[None], v_[None]

    return (loss.reshape(()), grad_x[None], *[out_g[k] for k in WEIGHTS], *[out_d[k] for k in WEIGHTS],
            *[out_m[k] for k in WEIGHTS], *[out_v[k] for k in WEIGHTS])
```
